```python
import jax, jax.numpy as jnp
from jax import lax
import numpy as np

D_MODEL = 1024
BATCH = 8
SEQ = 2048
DEPTH = 1

CHUNK = 64
MIX_WIDTH = D_MODEL
RET_WIDTH = MIX_WIDTH // 2
RET_HEADS = 4
RET_DV = RET_WIDTH // RET_HEADS
RET_DK = RET_DV // 2
RET_QK = RET_HEADS * RET_DK
RWKV_WIDTH = MIX_WIDTH - RET_WIDTH
RWKV_HEAD = 64
RWKV_HEADS = RWKV_WIDTH // RWKV_HEAD
LORA_W = 64
LORA_A = 64
RET_COLS = 2 * RET_QK + 2 * RET_WIDTH
RWKV_COLS = 4 * RWKV_WIDTH + LORA_W + LORA_A
IN_COLS = RET_COLS + RWKV_COLS
ROPE_BASE = 10000.0
RMS_EPS = 1e-6
RET_GN_EPS = 1e-5
RWKV_GN_EPS = 64e-5

kernel_name = "retention_rwkv7_parallel_hybrid"


def rmsnorm(x, g):
    xf = x.astype(jnp.float32)
    y = xf * lax.rsqrt(jnp.mean(xf * xf, axis=-1, keepdims=True) + RMS_EPS)
    return (y * g.astype(jnp.float32)).astype(x.dtype)


def head_norm(x, eps):
    mu = jnp.mean(x, axis=-1, keepdims=True)
    var = jnp.mean(jnp.square(x - mu), axis=-1, keepdims=True)
    return (x - mu) * lax.rsqrt(var + eps)


def rope(x, pos):
    half = x.shape[-1] // 2
    expo = -jnp.arange(half, dtype=jnp.float32) / jnp.float32(half)
    freqs = jnp.exp(expo * jnp.float32(np.log(ROPE_BASE)))
    ang = pos.astype(jnp.float32)[:, None] * freqs[None, :]
    cos = jnp.cos(ang)[None, :, None, :]
    sin = jnp.sin(ang)[None, :, None, :]
    x1 = x[..., :half]
    x2 = x[..., half:]
    return jnp.concatenate([x1 * cos - x2 * sin, x1 * sin + x2 * cos], axis=-1)


def chunk_retention(q, k, v):
    q = q.astype(jnp.float32)
    k = k.astype(jnp.float32)
    v = v.astype(jnp.float32)
    B, T, H, dk = q.shape
    dv = v.shape[-1]
    nc = T // CHUNK
    hidx = jnp.arange(H, dtype=jnp.float32)
    lg = jnp.log(1.0 - jnp.exp2(-5.0 - hidx))
    idx = jnp.arange(CHUNK, dtype=jnp.float32)
    intra_decay = jnp.exp(lg[:, None, None] * jnp.abs(idx[:, None] - idx[None, :]))
    q_dec = jnp.transpose(jnp.exp(lg[:, None] * (idx[None, :] + 1.0)))
    k_dec = jnp.transpose(jnp.exp(lg[:, None] * (CHUNK - 1.0 - idx[None, :])))
    chunk_dec = jnp.exp(lg * CHUNK).reshape(1, H, 1, 1)

    qc = q.reshape(B, nc, CHUNK, H, dk)
    kc = k.reshape(B, nc, CHUNK, H, dk)
    vc = v.reshape(B, nc, CHUNK, H, dv)

    scores = jnp.einsum('bnqhd,bnkhd->bnhqk', qc, kc) * intra_decay[None, None]
    intra = jnp.einsum('bnhqk,bnkhe->bnqhe', scores, vc)

    kv = jnp.einsum('bnkhd,bnkhe->nbhde', kc * k_dec[None, None, :, :, None], vc)

    def step(S, kv_c):
        S_new = (S * chunk_dec + kv_c).astype(S.dtype)
        return S_new, S

    S0 = jnp.zeros(kv.shape[1:], dtype=kv.dtype)
    _, s_in = lax.scan(step, S0, kv)
    inter = jnp.einsum('bnqhd,nbhde->bnqhe', qc * q_dec[None, None, :, :, None], s_in)
    return (intra + inter).reshape(B, T, H, dv)


def wkv7_scan(r, w, k, v, kk, a):
    B, T, H, N = r.shape
    xs = tuple(jnp.moveaxis(t.astype(jnp.float32), 1, 0) for t in (r, w, k, v, kk, a))

    def step(S, inp):
        r_t, w_t, k_t, v_t, kk_t, a_t = inp
        sa = jnp.einsum('bhij,bhj->bhi', S, -kk_t)
        S_new = (S * w_t[:, :, None, :]
                 + sa[..., None] * (kk_t * a_t)[:, :, None, :]
                 + v_t[..., None] * k_t[:, :, None, :]).astype(S.dtype)
        return S_new, jnp.einsum('bhij,bhj->bhi', S_new, r_t)

    S0 = jnp.zeros((B, H, N, N), dtype=jnp.float32)
    _, o = lax.scan(step, S0, xs)
    return jnp.moveaxis(o, 0, 1)


def _fwd_setup_inputs(seed: int = 0) -> dict:
    key = jax.random.key(seed)
    ks = jax.random.split(key, 20)
    f32 = jnp.float32
    L = DEPTH
    x = jax.random.normal(ks[0], (BATCH, SEQ, D_MODEL), f32)
    norm_g = 1.0 + 0.01 * jax.random.normal(ks[1], (L, D_MODEL), f32)
    w_in = jax.random.normal(ks[2], (L, D_MODEL, IN_COLS), f32) * D_MODEL ** -0.5
    ret_gn_g = 1.0 + 0.01 * jax.random.normal(ks[3], (L, RET_WIDTH), f32)
    rwkv_mu = jax.random.uniform(ks[4], (L, RWKV_COLS), f32)
    w_lora_up = jax.random.normal(ks[5], (L, LORA_W, RWKV_WIDTH), f32) * 0.1
    w0 = jax.random.uniform(ks[6], (L, RWKV_WIDTH), f32, minval=-3.0, maxval=1.0)
    a_lora_up = jax.random.normal(ks[7], (L, LORA_A, RWKV_WIDTH), f32) * 0.1
    a0 = 0.1 * jax.random.normal(ks[8], (L, RWKV_WIDTH), f32)
    k_k = 0.85 + 0.05 * jax.random.normal(ks[9], (L, RWKV_WIDTH), f32)
    k_a = 1.0 + 0.05 * jax.random.normal(ks[10], (L, RWKV_WIDTH), f32)
    r_k = 0.1 * jax.random.normal(ks[11], (L, RWKV_HEADS, RWKV_HEAD), f32)
    rwkv_gn_g = 1.0 + 0.01 * jax.random.normal(ks[12], (L, RWKV_WIDTH), f32)
    rwkv_gn_b = 0.01 * jax.random.normal(ks[13], (L, RWKV_WIDTH), f32)
    w_out = jax.random.normal(ks[14], (L, MIX_WIDTH, D_MODEL), f32) * MIX_WIDTH ** -0.5
    final_norm_g = 1.0 + 0.01 * jax.random.normal(ks[15], (D_MODEL,), f32)
    return {"x": x, "norm_g": norm_g, "w_in": w_in, "ret_gn_g": ret_gn_g,
            "rwkv_mu": rwkv_mu, "w_lora_up": w_lora_up, "w0": w0,
            "a_lora_up": a_lora_up, "a0": a0, "k_k": k_k, "k_a": k_a, "r_k": r_k,
            "rwkv_gn_g": rwkv_gn_g, "rwkv_gn_b": rwkv_gn_b, "w_out": w_out,
            "final_norm_g": final_norm_g}


def _fwd_reference(x, norm_g, w_in, ret_gn_g, rwkv_mu, w_lora_up, w0, a_lora_up, a0,
              k_k, k_a, r_k, rwkv_gn_g, rwkv_gn_b, w_out, final_norm_g):
    B, T, _ = x.shape
    f32 = jnp.float32
    pos = jnp.arange(T, dtype=jnp.int32)
    W = RWKV_WIDTH
    h = x
    for l in range(DEPTH):
        u = rmsnorm(h, norm_g[l])
        p = jnp.einsum('btd,dc->btc', u, w_in[l]).astype(f32)
        p_ret = p[..., :RET_COLS]
        p_rwkv = p[..., RET_COLS:]

        q = p_ret[..., :RET_QK]
        k = p_ret[..., RET_QK:2 * RET_QK]
        v = p_ret[..., 2 * RET_QK:2 * RET_QK + RET_WIDTH]
        g_ret = p_ret[..., 2 * RET_QK + RET_WIDTH:]
        q = rope(q.reshape(B, T, RET_HEADS, RET_DK), pos)
        k = rope(k.reshape(B, T, RET_HEADS, RET_DK), pos) * (RET_DK ** -0.5)
        v = v.reshape(B, T, RET_HEADS, RET_DV)
        ret = head_norm(chunk_retention(q, k, v), RET_GN_EPS).reshape(B, T, RET_WIDTH)
        y_ret = jax.nn.silu(g_ret) * (ret * ret_gn_g[l].astype(f32))

        prev = jnp.pad(p_rwkv, ((0, 0), (1, 0), (0, 0)))[:, :-1]
        ps = p_rwkv + rwkv_mu[l].astype(f32) * (prev - p_rwkv)
        r = ps[..., :W]
        kr = ps[..., W:2 * W]
        vr = ps[..., 2 * W:3 * W]
        g_rw = ps[..., 3 * W:4 * W]
        xw = ps[..., 4 * W:4 * W + LORA_W]
        xa = ps[..., 4 * W + LORA_W:]
        w_log = -jax.nn.softplus(-(w0[l].astype(f32) + jnp.tanh(xw) @ w_lora_up[l].astype(f32))) - 0.5
        decay = jnp.exp(-jnp.exp(w_log))
        a = jax.nn.sigmoid(a0[l].astype(f32) + xa @ a_lora_up[l].astype(f32))
        kk = (kr * k_k[l].astype(f32)).reshape(B, T, RWKV_HEADS, RWKV_HEAD)
        kk = kk / jnp.maximum(jnp.sqrt(jnp.sum(kk * kk, axis=-1, keepdims=True)), 1e-12)
        kr = kr * (1.0 + (a - 1.0) * k_a[l].astype(f32))
        r4 = r.reshape(B, T, RWKV_HEADS, RWKV_HEAD)
        k4 = kr.reshape(B, T, RWKV_HEADS, RWKV_HEAD)
        v4 = vr.reshape(B, T, RWKV_HEADS, RWKV_HEAD)
        w4 = decay.reshape(B, T, RWKV_HEADS, RWKV_HEAD)
        a4 = a.reshape(B, T, RWKV_HEADS, RWKV_HEAD)
        o = wkv7_scan(r4, w4, k4, v4, kk, a4)
        gn_g = rwkv_gn_g[l].astype(f32).reshape(RWKV_HEADS, RWKV_HEAD)
        gn_b = rwkv_gn_b[l].astype(f32).reshape(RWKV_HEADS, RWKV_HEAD)
        o = head_norm(o, RWKV_GN_EPS) * gn_g + gn_b
        bonus = jnp.sum(r4 * k4 * r_k[l].astype(f32), axis=-1, keepdims=True) * v4
        y_rwkv = jax.nn.silu(g_rw) * (o + bonus).reshape(B, T, RWKV_WIDTH)

        y = jnp.concatenate([y_ret, y_rwkv], axis=-1).astype(x.dtype)
        h = h + jnp.einsum('btc,cd->btd', y, w_out[l]).astype(h.dtype)
    return rmsnorm(h, final_norm_g)


import jax as _jax
import jax.numpy as _jnp

TWIN_FORMAT = 'train_step'
FWD_PARAMS = ['x', 'norm_g', 'w_in', 'ret_gn_g', 'rwkv_mu', 'w_lora_up', 'w0', 'a_lora_up', 'a0', 'k_k', 'k_a', 'r_k', 'rwkv_gn_g', 'rwkv_gn_b', 'w_out', 'final_norm_g']
TWIN_WEIGHTS = ['norm_g', 'w_in', 'ret_gn_g', 'rwkv_mu', 'w_lora_up', 'w0', 'a_lora_up', 'a0', 'k_k', 'k_a', 'r_k', 'rwkv_gn_g', 'rwkv_gn_b', 'w_out', 'final_norm_g']
TWIN_DIFF_INPUT = 'x'
TWIN_INPUTS = ['x', 'norm_g', 'w_in', 'ret_gn_g', 'rwkv_mu', 'w_lora_up', 'w0', 'a_lora_up', 'a0', 'k_k', 'k_a', 'r_k', 'rwkv_gn_g', 'rwkv_gn_b', 'w_out', 'final_norm_g', 'loss_target', 'm_norm_g', 'm_w_in', 'm_ret_gn_g', 'm_rwkv_mu', 'm_w_lora_up', 'm_w0', 'm_a_lora_up', 'm_a0', 'm_k_k', 'm_k_a', 'm_r_k', 'm_rwkv_gn_g', 'm_rwkv_gn_b', 'm_w_out', 'm_final_norm_g', 'v_norm_g', 'v_w_in', 'v_ret_gn_g', 'v_rwkv_mu', 'v_w_lora_up', 'v_w0', 'v_a_lora_up', 'v_a0', 'v_k_k', 'v_k_a', 'v_r_k', 'v_rwkv_gn_g', 'v_rwkv_gn_b', 'v_w_out', 'v_final_norm_g']
TWIN_OUTPUTS = ['loss', 'grad_x', 'grad_norm_g', 'grad_w_in', 'grad_ret_gn_g', 'grad_rwkv_mu', 'grad_w_lora_up', 'grad_w0', 'grad_a_lora_up', 'grad_a0', 'grad_k_k', 'grad_k_a', 'grad_r_k', 'grad_rwkv_gn_g', 'grad_rwkv_gn_b', 'grad_w_out', 'grad_final_norm_g', 'delta_norm_g', 'delta_w_in', 'delta_ret_gn_g', 'delta_rwkv_mu', 'delta_w_lora_up', 'delta_w0', 'delta_a_lora_up', 'delta_a0', 'delta_k_k', 'delta_k_a', 'delta_r_k', 'delta_rwkv_gn_g', 'delta_rwkv_gn_b', 'delta_w_out', 'delta_final_norm_g', 'new_m_norm_g', 'new_m_w_in', 'new_m_ret_gn_g', 'new_m_rwkv_mu', 'new_m_w_lora_up', 'new_m_w0', 'new_m_a_lora_up', 'new_m_a0', 'new_m_k_k', 'new_m_k_a', 'new_m_r_k', 'new_m_rwkv_gn_g', 'new_m_rwkv_gn_b', 'new_m_w_out', 'new_m_final_norm_g', 'new_v_norm_g', 'new_v_w_in', 'new_v_ret_gn_g', 'new_v_rwkv_mu', 'new_v_w_lora_up', 'new_v_w0', 'new_v_a_lora_up', 'new_v_a0', 'new_v_k_k', 'new_v_k_a', 'new_v_r_k', 'new_v_rwkv_gn_g', 'new_v_rwkv_gn_b', 'new_v_w_out', 'new_v_final_norm_g']
TWIN_LEAF_KINDS = {'loss': 'loss', 'grad_x': 'grad_x', 'grad_norm_g': 'grad_w', 'grad_w_in': 'grad_w', 'grad_ret_gn_g': 'grad_w', 'grad_rwkv_mu': 'grad_w', 'grad_w_lora_up': 'grad_w', 'grad_w0': 'grad_w', 'grad_a_lora_up': 'grad_w', 'grad_a0': 'grad_w', 'grad_k_k': 'grad_w', 'grad_k_a': 'grad_w', 'grad_r_k': 'grad_w', 'grad_rwkv_gn_g': 'grad_w', 'grad_rwkv_gn_b': 'grad_w', 'grad_w_out': 'grad_w', 'grad_final_norm_g': 'grad_w', 'delta_norm_g': 'delta_w', 'delta_w_in': 'delta_w', 'delta_ret_gn_g': 'delta_w', 'delta_rwkv_mu': 'delta_w', 'delta_w_lora_up': 'delta_w', 'delta_w0': 'delta_w', 'delta_a_lora_up': 'delta_w', 'delta_a0': 'delta_w', 'delta_k_k': 'delta_w', 'delta_k_a': 'delta_w', 'delta_r_k': 'delta_w', 'delta_rwkv_gn_g': 'delta_w', 'delta_rwkv_gn_b': 'delta_w', 'delta_w_out': 'delta_w', 'delta_final_norm_g': 'delta_w', 'new_m_norm_g': 'new_m', 'new_m_w_in': 'new_m', 'new_m_ret_gn_g': 'new_m', 'new_m_rwkv_mu': 'new_m', 'new_m_w_lora_up': 'new_m', 'new_m_w0': 'new_m', 'new_m_a_lora_up': 'new_m', 'new_m_a0': 'new_m', 'new_m_k_k': 'new_m', 'new_m_k_a': 'new_m', 'new_m_r_k': 'new_m', 'new_m_rwkv_gn_g': 'new_m', 'new_m_rwkv_gn_b': 'new_m', 'new_m_w_out': 'new_m', 'new_m_final_norm_g': 'new_m', 'new_v_norm_g': 'new_v', 'new_v_w_in': 'new_v', 'new_v_ret_gn_g': 'new_v', 'new_v_rwkv_mu': 'new_v', 'new_v_w_lora_up': 'new_v', 'new_v_w0': 'new_v', 'new_v_a_lora_up': 'new_v', 'new_v_a0': 'new_v', 'new_v_k_k': 'new_v', 'new_v_k_a': 'new_v', 'new_v_r_k': 'new_v', 'new_v_rwkv_gn_g': 'new_v', 'new_v_rwkv_gn_b': 'new_v', 'new_v_w_out': 'new_v', 'new_v_final_norm_g': 'new_v'}


def _forward(args):
    return _fwd_reference(*[args[k] for k in FWD_PARAMS])


def _output_shape():
    out = _jax.eval_shape(lambda: _forward(_fwd_setup_inputs(0)))
    return out.shape, out.dtype

N_MICROBATCH = 1
ADAM_LR = 0.001
ADAM_B1 = 0.9
ADAM_B2 = 0.999
ADAM_EPS = 1e-08
ADAM_WD = 0.01
ADAM_STEP = 10
PER_EXAMPLE_BATCH_AXIS = {'x': 0, 'loss_target': 0}
SHARED_INPUTS = []
_WEIGHT_DTYPES = {'norm_g': _jnp.float32, 'w_in': _jnp.float32, 'ret_gn_g': _jnp.float32, 'rwkv_mu': _jnp.float32, 'w_lora_up': _jnp.float32, 'w0': _jnp.float32, 'a_lora_up': _jnp.float32, 'a0': _jnp.float32, 'k_k': _jnp.float32, 'k_a': _jnp.float32, 'r_k': _jnp.float32, 'rwkv_gn_g': _jnp.float32, 'rwkv_gn_b': _jnp.float32, 'w_out': _jnp.float32, 'final_norm_g': _jnp.float32}
MOMENT_SCALE = {'norm_g': 1.274277e-01, 'w_in': 6.505102e-02, 'ret_gn_g': 6.445974e-02, 'rwkv_mu': 9.402951e-02, 'w_lora_up': 5.417197e-03, 'w0': 2.777159e-02, 'a_lora_up': 1.928463e-02, 'a0': 2.470824e-02, 'k_k': 1.718688e-02, 'k_a': 6.329464e-02, 'r_k': 1.167866e-01, 'rwkv_gn_g': 5.382600e-02, 'rwkv_gn_b': 5.085582e-02, 'w_out': 5.978422e-02, 'final_norm_g': 1.598776e+01}


def _to_microbatches(a, axis):
    t = _jnp.moveaxis(a, axis, 0)
    t = t.reshape((N_MICROBATCH, t.shape[0] // N_MICROBATCH) + t.shape[1:])
    return _jnp.moveaxis(t, 1, axis + 1)


def setup_inputs(seed: int = 0) -> dict:
    inp = _fwd_setup_inputs(seed)
    key = _jax.random.fold_in(_jax.random.key(seed), 7919)
    shape, _ = _output_shape()
    out = dict(inp)
    out["loss_target"] = _jax.random.normal(_jax.random.fold_in(key, 0), shape, _jnp.float32)
    for i, name in enumerate(TWIN_WEIGHTS):
        w = inp[name].astype(_jnp.float32)
        if MOMENT_SCALE is None:
            s = _jnp.sqrt(_jnp.mean(_jnp.square(w)) + 1e-30)
        else:
            s = MOMENT_SCALE[name]
        km, kv = _jax.random.split(_jax.random.fold_in(key, i + 1))
        out[name] = w
        out["m_" + name] = s * _jax.random.normal(km, w.shape, _jnp.float32)
        out["v_" + name] = (s * s) * _jax.random.uniform(kv, w.shape, _jnp.float32, 0.5, 1.5)
    if N_MICROBATCH > 1:
        for name, axis in PER_EXAMPLE_BATCH_AXIS.items():
            out[name] = _to_microbatches(out[name], axis)
    return {'x': out['x'], 'norm_g': out['norm_g'], 'w_in': out['w_in'], 'ret_gn_g': out['ret_gn_g'], 'rwkv_mu': out['rwkv_mu'], 'w_lora_up': out['w_lora_up'], 'w0': out['w0'], 'a_lora_up': out['a_lora_up'], 'a0': out['a0'], 'k_k': out['k_k'], 'k_a': out['k_a'], 'r_k': out['r_k'], 'rwkv_gn_g': out['rwkv_gn_g'], 'rwkv_gn_b': out['rwkv_gn_b'], 'w_out': out['w_out'], 'final_norm_g': out['final_norm_g'], 'loss_target': out['loss_target'], 'm_norm_g': out['m_norm_g'], 'm_w_in': out['m_w_in'], 'm_ret_gn_g': out['m_ret_gn_g'], 'm_rwkv_mu': out['m_rwkv_mu'], 'm_w_lora_up': out['m_w_lora_up'], 'm_w0': out['m_w0'], 'm_a_lora_up': out['m_a_lora_up'], 'm_a0': out['m_a0'], 'm_k_k': out['m_k_k'], 'm_k_a': out['m_k_a'], 'm_r_k': out['m_r_k'], 'm_rwkv_gn_g': out['m_rwkv_gn_g'], 'm_rwkv_gn_b': out['m_rwkv_gn_b'], 'm_w_out': out['m_w_out'], 'm_final_norm_g': out['m_final_norm_g'], 'v_norm_g': out['v_norm_g'], 'v_w_in': out['v_w_in'], 'v_ret_gn_g': out['v_ret_gn_g'], 'v_rwkv_mu': out['v_rwkv_mu'], 'v_w_lora_up': out['v_w_lora_up'], 'v_w0': out['v_w0'], 'v_a_lora_up': out['v_a_lora_up'], 'v_a0': out['v_a0'], 'v_k_k': out['v_k_k'], 'v_k_a': out['v_k_a'], 'v_r_k': out['v_r_k'], 'v_rwkv_gn_g': out['v_rwkv_gn_g'], 'v_rwkv_gn_b': out['v_rwkv_gn_b'], 'v_w_out': out['v_w_out'], 'v_final_norm_g': out['v_final_norm_g']}


def _loss(weights, diff, rest, loss_target):
    with _jax.named_scope("forward"):
        args = {**rest, TWIN_DIFF_INPUT: diff, **{k: w.astype(_WEIGHT_DTYPES[k]) for k, w in weights.items()}}
        y = _forward(args)
    with _jax.named_scope("loss_head"):
        err = _jnp.square(y.astype(_jnp.float32) - loss_target)
        return 0.5 * _jnp.sum(_jnp.mean(err, axis=-1)) if err.ndim else 0.5 * err


def _adamw(w, g, m, v):
    m = ADAM_B1 * m + (1.0 - ADAM_B1) * g
    v = ADAM_B2 * v + (1.0 - ADAM_B2) * _jnp.square(g)
    m_hat = m / (1.0 - ADAM_B1 ** ADAM_STEP)
    v_hat = v / (1.0 - ADAM_B2 ** ADAM_STEP)
    delta = -ADAM_LR * (m_hat / (_jnp.sqrt(v_hat) + ADAM_EPS) + ADAM_WD * w)
    return delta, m, v


def reference(x, norm_g, w_in, ret_gn_g, rwkv_mu, w_lora_up, w0, a_lora_up, a0, k_k, k_a, r_k, rwkv_gn_g, rwkv_gn_b, w_out, final_norm_g, loss_target, m_norm_g, m_w_in, m_ret_gn_g, m_rwkv_mu, m_w_lora_up, m_w0, m_a_lora_up, m_a0, m_k_k, m_k_a, m_r_k, m_rwkv_gn_g, m_rwkv_gn_b, m_w_out, m_final_norm_g, v_norm_g, v_w_in, v_ret_gn_g, v_rwkv_mu, v_w_lora_up, v_w0, v_a_lora_up, v_a0, v_k_k, v_k_a, v_r_k, v_rwkv_gn_g, v_rwkv_gn_b, v_w_out, v_final_norm_g):
    given = dict(x=x, norm_g=norm_g, w_in=w_in, ret_gn_g=ret_gn_g, rwkv_mu=rwkv_mu, w_lora_up=w_lora_up, w0=w0, a_lora_up=a_lora_up, a0=a0, k_k=k_k, k_a=k_a, r_k=r_k, rwkv_gn_g=rwkv_gn_g, rwkv_gn_b=rwkv_gn_b, w_out=w_out, final_norm_g=final_norm_g, loss_target=loss_target, m_norm_g=m_norm_g, m_w_in=m_w_in, m_ret_gn_g=m_ret_gn_g, m_rwkv_mu=m_rwkv_mu, m_w_lora_up=m_w_lora_up, m_w0=m_w0, m_a_lora_up=m_a_lora_up, m_a0=m_a0, m_k_k=m_k_k, m_k_a=m_k_a, m_r_k=m_r_k, m_rwkv_gn_g=m_rwkv_gn_g, m_rwkv_gn_b=m_rwkv_gn_b, m_w_out=m_w_out, m_final_norm_g=m_final_norm_g, v_norm_g=v_norm_g, v_w_in=v_w_in, v_ret_gn_g=v_ret_gn_g, v_rwkv_mu=v_rwkv_mu, v_w_lora_up=v_w_lora_up, v_w0=v_w0, v_a_lora_up=v_a_lora_up, v_a0=v_a0, v_k_k=v_k_k, v_k_a=v_k_a, v_r_k=v_r_k, v_rwkv_gn_g=v_rwkv_gn_g, v_rwkv_gn_b=v_rwkv_gn_b, v_w_out=v_w_out, v_final_norm_g=v_final_norm_g)
    weights = {n: given[n] for n in TWIN_WEIGHTS}
    shared = {n: given[n] for n in SHARED_INPUTS}
    per_example = {n: given[n] for n in ['x']}
    grad_fn = _jax.value_and_grad(_loss, argnums=(0, 1))

    def one_microbatch(ex, loss_target):
        ex = dict(ex)
        diff = ex.pop(TWIN_DIFF_INPUT)
        return grad_fn(weights, diff, {**shared, **ex}, loss_target)

    if N_MICROBATCH == 1:
        loss, (grad_w, grad_x) = one_microbatch(per_example, given["loss_target"])
    else:
        def body(carry, xs):
            loss_sum, grad_sum = carry
            l_k, (gw_k, gx_k) = one_microbatch(xs[0], xs[1])
            with _jax.named_scope("update"):
                return (loss_sum + l_k, _jax.tree.map(_jnp.add, grad_sum, gw_k)), gx_k

        init = (_jnp.zeros((), _jnp.float32), _jax.tree.map(_jnp.zeros_like, weights))
        (loss, grad_w), grad_x = _jax.lax.scan(body, init, (per_example, given["loss_target"]))
    with _jax.named_scope("update"):
        delta_w, new_m, new_v = {}, {}, {}
        for n in TWIN_WEIGHTS:
            delta_w[n], new_m[n], new_v[n] = _adamw(weights[n], grad_w[n], given["m_" + n], given["v_" + n])
    return (loss, grad_x, *[grad_w[n] for n in TWIN_WEIGHTS], *[delta_w[n] for n in TWIN_WEIGHTS],
            *[new_m[n] for n in TWIN_WEIGHTS], *[new_v[n] for n in TWIN_WEIGHTS])
```

```python
import functools

import numpy as np
import jax
import jax.numpy as jnp
from jax import lax
from jax.experimental import pallas as pl
from jax.experimental.pallas import tpu as pltpu

F32 = jnp.float32
BF16 = jnp.bfloat16
HI = lax.Precision.HIGHEST

D_MODEL = 1024
CHUNK = 64
RET_HEADS = 4
RET_DV = 128
RET_DK = 64
RET_QK = 256
RET_WIDTH = 512
RWKV_WIDTH = 512
RWKV_HEAD = 64
RWKV_HEADS = 8
LORA = 64
RET_COLS = 2 * RET_QK + 2 * RET_WIDTH
RWKV_COLS = 4 * RWKV_WIDTH + 2 * LORA
IN_COLS = RET_COLS + RWKV_COLS
ROPE_BASE = 10000.0
RMS_EPS = 1e-6
RET_GN_EPS = 1e-5
RWKV_GN_EPS = 64e-5
ADAM_LR = 0.001
ADAM_B1 = 0.9
ADAM_B2 = 0.999
ADAM_EPS = 1e-08
ADAM_WD = 0.01
ADAM_STEP = 10
N_DEV = 8
SHARD_IN = IN_COLS // N_DEV
SHARD_OUT = D_MODEL // N_DEV
SHARD_LORA = RWKV_WIDTH // N_DEV
VMEM_LIMIT = 56 * 1024 * 1024
TOK_TILE = 256
WKV_CHUNK = 32

MESH = pl.DeviceIdType.MESH


def _dot_hi(a, b):
    return jnp.dot(a, b, precision=HI, preferred_element_type=F32)


def _dot_nt_hi(a, b):
    return lax.dot_general(a, b, (((1,), (1,)), ((), ())), precision=HI, preferred_element_type=F32)


def _dot_tn_hi(a, b):
    return lax.dot_general(a, b, (((0,), (0,)), ((), ())), precision=HI, preferred_element_type=F32)


def _dot_bf(a, b):
    return jnp.dot(a.astype(BF16), b.astype(BF16), preferred_element_type=F32)


def _dot_nt_bf(a, b):
    return lax.dot_general(a.astype(BF16), b.astype(BF16), (((1,), (1,)), ((), ())), preferred_element_type=F32)


def _dot_tn_bf(a, b):
    return lax.dot_general(a.astype(BF16), b.astype(BF16), (((0,), (0,)), ((), ())), preferred_element_type=F32)


def _params(**kw):
    return pltpu.CompilerParams(vmem_limit_bytes=VMEM_LIMIT, **kw)


def _full(shape):
    nd = len(shape)
    return pl.BlockSpec(shape, lambda i, _nd=nd: (0,) * _nd)


def _rows(tile, width):
    return pl.BlockSpec((tile, width), lambda i: (i, 0))


def _block_ones(n, blk, scale=1.0):
    idx = np.arange(n) // blk
    return jnp.asarray((idx[:, None] == idx[None, :]).astype(np.float32) * scale)


def _rope_tables(T):
    half = RET_DK // 2
    expo = -np.arange(half, dtype=np.float32) / np.float32(half)
    freqs = np.exp(expo * np.float32(np.log(ROPE_BASE))).astype(np.float32)
    ang = np.arange(T, dtype=np.float32)[:, None] * freqs[None, :]
    cos, sin = np.cos(ang).astype(np.float32), np.sin(ang).astype(np.float32)
    cos_h = np.concatenate([cos, cos], axis=1)
    sin_h = np.concatenate([-sin, sin], axis=1)
    cos_t = np.tile(cos_h, (1, RET_HEADS))
    sin_t = np.tile(sin_h, (1, RET_HEADS))
    d = np.arange(RET_QK)
    src = (d // RET_DK) * RET_DK + ((d % RET_DK) + half) % RET_DK
    prot = np.zeros((RET_QK, RET_QK), np.float32)
    prot[src, d] = 1.0
    return jnp.asarray(cos_t), jnp.asarray(sin_t), jnp.asarray(prot)


def _ret_tables():
    h = np.arange(RET_HEADS, dtype=np.float32)
    lg = np.log(1.0 - np.exp2(-5.0 - h)).astype(np.float32)
    idx = np.arange(CHUNK, dtype=np.float32)
    intra = np.exp(lg[:, None, None] * np.abs(idx[:, None] - idx[None, :])).astype(np.float32)
    q_dec = np.exp(lg[:, None] * (idx[None, :] + 1.0)).astype(np.float32)
    k_dec = np.exp(lg[:, None] * (CHUNK - 1.0 - idx[None, :])).astype(np.float32)
    chunk_dec = np.exp(lg * CHUNK).astype(np.float32)
    lane_head = np.arange(RET_QK) // RET_DK
    mask = (lane_head[None, :] == np.arange(RET_HEADS)[:, None]).astype(np.float32)
    m = np.broadcast_to(mask[:, None, :], (RET_HEADS, CHUNK, RET_QK)).copy()
    qd = m * q_dec[:, :, None]
    kd = m * k_dec[:, :, None]
    return jnp.asarray(intra), jnp.asarray(m), jnp.asarray(qd), jnp.asarray(kd), [float(c) for c in chunk_dec]


def _rmsnorm(x, g):
    return x * lax.rsqrt(jnp.mean(x * x, axis=-1, keepdims=True) + RMS_EPS) * g


def _inproj(x, norm_g, w_ret, w_rwkv):
    T = x.shape[0]

    def body(x_ref, g_ref, wr_ref, ww_ref, pr_ref, pw_ref, u_ref):
        ub = _rmsnorm(x_ref[...], g_ref[...]).astype(BF16)
        u_ref[...] = ub
        pr_ref[...] = jnp.dot(ub, wr_ref[...], preferred_element_type=F32)
        pw_ref[...] = jnp.dot(ub, ww_ref[...], preferred_element_type=F32)

    return pl.pallas_call(
        body, name="inproj", grid=(T // TOK_TILE,),
        in_specs=[_rows(TOK_TILE, D_MODEL), _full((1, D_MODEL)), _full((D_MODEL, RET_COLS)), _full((D_MODEL, RWKV_COLS))],
        out_specs=[_rows(TOK_TILE, RET_COLS), _rows(TOK_TILE, RWKV_COLS), _rows(TOK_TILE, D_MODEL)],
        out_shape=[jax.ShapeDtypeStruct((T, RET_COLS), F32), jax.ShapeDtypeStruct((T, RWKV_COLS), F32),
                   jax.ShapeDtypeStruct((T, D_MODEL), BF16)],
        compiler_params=_params(dimension_semantics=("arbitrary",)),
    )(x, norm_g, w_ret, w_rwkv)


def _ret_chunk(pq, pk, v_heads, s_heads, cos_t, sin_t, prot, dec, hm, qd, kd, chunk_dec):
    q = pq * cos_t + _dot_hi(pq, prot) * sin_t
    k = (pk * cos_t + _dot_hi(pk, prot) * sin_t) * (RET_DK ** -0.5)
    outs, s_out = [], []
    for h in range(RET_HEADS):
        sc = _dot_nt_hi(q * hm[h], k * hm[h]) * dec[h]
        intra = _dot_hi(sc, v_heads[h])
        kv = _dot_tn_hi(k * kd[h], v_heads[h])
        inter = _dot_hi(q * qd[h], s_heads[h])
        outs.append(intra + inter)
        s_out.append(s_heads[h] * chunk_dec[h] + kv)
    return tuple(outs), tuple(s_out)


def _ret_specs():
    const = [_full((RET_QK, RET_QK)), _full((RET_HEADS, CHUNK, CHUNK)), _full((RET_HEADS, CHUNK, RET_QK)),
             _full((RET_HEADS, CHUNK, RET_QK)), _full((RET_HEADS, CHUNK, RET_QK))]
    return const


def _ret_fwd(p_ret, tabs):
    T = p_ret.shape[0]
    nc = T // CHUNK
    cos_t, sin_t, prot, dec, hm, qd, kd, chunk_dec = tabs

    def body(p_ref, cos_ref, sin_ref, prot_ref, dec_ref, hm_ref, qd_ref, kd_ref, out_ref, sin_save_ref, s_scr):
        @pl.when(pl.program_id(0) == 0)
        def _():
            s_scr[...] = jnp.zeros_like(s_scr)

        s_heads = tuple(s_scr[h] for h in range(RET_HEADS))
        for h in range(RET_HEADS):
            sin_save_ref[0, h] = s_heads[h]
        pq = p_ref[:, 0:RET_QK]
        pk = p_ref[:, RET_QK:2 * RET_QK]
        v_heads = tuple(p_ref[:, 2 * RET_QK + RET_DV * h:2 * RET_QK + RET_DV * (h + 1)] for h in range(RET_HEADS))
        outs, s_out = _ret_chunk(pq, pk, v_heads, s_heads, cos_ref[...], sin_ref[...], prot_ref[...], dec_ref[...],
                                 hm_ref[...], qd_ref[...], kd_ref[...], chunk_dec)
        for h in range(RET_HEADS):
            out_ref[:, RET_DV * h:RET_DV * (h + 1)] = outs[h]
            s_scr[h] = s_out[h]

    return pl.pallas_call(
        body, name="ret_fwd", grid=(nc,),
        in_specs=[pl.BlockSpec((CHUNK, RET_COLS), lambda i: (i, 0)), _rows(CHUNK, RET_QK), _rows(CHUNK, RET_QK)] + _ret_specs(),
        out_specs=[_rows(CHUNK, RET_WIDTH), pl.BlockSpec((1, RET_HEADS, RET_QK, RET_DV), lambda i: (i, 0, 0, 0))],
        out_shape=[jax.ShapeDtypeStruct((T, RET_WIDTH), F32), jax.ShapeDtypeStruct((nc, RET_HEADS, RET_QK, RET_DV), F32)],
        scratch_shapes=[pltpu.VMEM((RET_HEADS, RET_QK, RET_DV), F32)],
        compiler_params=_params(dimension_semantics=("arbitrary",)),
    )(p_ret, cos_t, sin_t, prot, dec, hm, qd, kd)


def _ret_bwd(p_ret, s_saved, d_ret, tabs):
    T = p_ret.shape[0]
    nc = T // CHUNK
    cos_t, sin_t, prot, dec, hm, qd, kd, chunk_dec = tabs

    def body(p_ref, s_ref, dret_ref, cos_ref, sin_ref, prot_ref, dec_ref, hm_ref, qd_ref, kd_ref, dp_ref, ds_scr):
        @pl.when(pl.program_id(0) == 0)
        def _():
            ds_scr[...] = jnp.zeros_like(ds_scr)

        pq = p_ref[:, 0:RET_QK]
        pk = p_ref[:, RET_QK:2 * RET_QK]
        v_heads = tuple(p_ref[:, 2 * RET_QK + RET_DV * h:2 * RET_QK + RET_DV * (h + 1)] for h in range(RET_HEADS))
        s_heads = tuple(s_ref[0, h] for h in range(RET_HEADS))
        consts = (cos_ref[...], sin_ref[...], prot_ref[...], dec_ref[...], hm_ref[...], qd_ref[...], kd_ref[...])
        _, vjp = jax.vjp(lambda a, b, c, d: _ret_chunk(a, b, c, d, *consts, chunk_dec), pq, pk, v_heads, s_heads)
        d_out = tuple(dret_ref[:, RET_DV * h:RET_DV * (h + 1)] for h in range(RET_HEADS))
        d_s = tuple(ds_scr[h] for h in range(RET_HEADS))
        dq, dk, dv, ds_in = vjp((d_out, d_s))
        dp_ref[:, 0:RET_QK] = dq
        dp_ref[:, RET_QK:2 * RET_QK] = dk
        for h in range(RET_HEADS):
            dp_ref[:, 2 * RET_QK + RET_DV * h:2 * RET_QK + RET_DV * (h + 1)] = dv[h]
            ds_scr[h] = ds_in[h]

    rev = lambda i: (nc - 1 - i, 0)
    return pl.pallas_call(
        body, name="ret_bwd", grid=(nc,),
        in_specs=[pl.BlockSpec((CHUNK, RET_COLS), rev),
                  pl.BlockSpec((1, RET_HEADS, RET_QK, RET_DV), lambda i: (nc - 1 - i, 0, 0, 0)),
                  pl.BlockSpec((CHUNK, RET_WIDTH), rev), pl.BlockSpec((CHUNK, RET_QK), rev), pl.BlockSpec((CHUNK, RET_QK), rev)]
        + _ret_specs(),
        out_specs=pl.BlockSpec((CHUNK, 2 * RET_QK + RET_WIDTH), rev),
        out_shape=jax.ShapeDtypeStruct((T, 2 * RET_QK + RET_WIDTH), F32),
        scratch_shapes=[pltpu.VMEM((RET_HEADS, RET_QK, RET_DV), F32)],
        compiler_params=_params(dimension_semantics=("arbitrary",)),
    )(p_ret, s_saved, d_ret, cos_t, sin_t, prot, dec, hm, qd, kd)


def _wkv_consts():
    lane = lax.broadcasted_iota(jnp.int32, (RWKV_HEAD, RWKV_WIDTH), 1)
    sub = lax.broadcasted_iota(jnp.int32, (RWKV_HEAD, RWKV_WIDTH), 0)
    diag = ((lane & (RWKV_HEAD - 1)) == sub).astype(F32)
    r = lax.broadcasted_iota(jnp.int32, (RWKV_WIDTH, RWKV_WIDTH), 0) >> 6
    c = lax.broadcasted_iota(jnp.int32, (RWKV_WIDTH, RWKV_WIDTH), 1) >> 6
    ones = (r == c).astype(F32)
    return diag, ones


def _expand(row, diag, ones):
    return _dot_hi(row * diag, ones)


def _colsum(x):
    return jnp.sum(x, axis=0, keepdims=True)


def _wkv_step(S, kap, w, b, k, vfull, ones):
    sa = _dot_hi(S * (-kap), ones)
    return S * w + sa * b + vfull * k, sa


def _wkv_fwd(r, w, k, v, kap, a):
    T = r.shape[0]
    C = WKV_CHUNK
    nc = T // C

    def body(r_ref, w_ref, k_ref, v_ref, kap_ref, a_ref, o_ref, s0_ref, s_scr):
        @pl.when(pl.program_id(0) == 0)
        def _():
            s_scr[...] = jnp.zeros_like(s_scr)

        diag, ones = _wkv_consts()
        s0_ref[0] = s_scr[...]

        def step8(i, S):
            base = pl.multiple_of(i * 8, 8)
            rb, wb, kb, vb = (ref[pl.ds(base, 8), :] for ref in (r_ref, w_ref, k_ref, v_ref))
            kapb, ab = kap_ref[pl.ds(base, 8), :], a_ref[pl.ds(base, 8), :]
            rows = []
            for u in range(8):
                sl = slice(u, u + 1)
                S, _ = _wkv_step(S, kapb[sl], wb[sl], kapb[sl] * ab[sl], kb[sl], _expand(vb[sl], diag, ones), ones)
                rows.append(_colsum(_dot_hi(S * rb[sl], ones) * diag))
            o_ref[pl.ds(base, 8), :] = jnp.concatenate(rows, axis=0)
            return S

        s_scr[...] = lax.fori_loop(0, C // 8, step8, s_scr[...])

    spec = _rows(C, RWKV_WIDTH)
    return pl.pallas_call(
        body, name="wkv_fwd", grid=(nc,),
        in_specs=[spec] * 6,
        out_specs=[spec, pl.BlockSpec((1, RWKV_HEAD, RWKV_WIDTH), lambda i: (i, 0, 0))],
        out_shape=[jax.ShapeDtypeStruct((T, RWKV_WIDTH), F32), jax.ShapeDtypeStruct((nc, RWKV_HEAD, RWKV_WIDTH), F32)],
        scratch_shapes=[pltpu.VMEM((RWKV_HEAD, RWKV_WIDTH), F32)],
        compiler_params=_params(dimension_semantics=("arbitrary",)),
    )(r, w, k, v, kap, a)


def _wkv_bwd(r, w, k, v, kap, a, s0, d_o):
    T = r.shape[0]
    C = WKV_CHUNK
    nc = T // C

    def body(r_ref, w_ref, k_ref, v_ref, kap_ref, a_ref, s0_ref, do_ref,
             dr_ref, dw_ref, dk_ref, dv_ref, dkap_ref, da_ref, ds_scr, s_all, sa_all, v_all):
        @pl.when(pl.program_id(0) == 0)
        def _():
            ds_scr[...] = jnp.zeros_like(ds_scr)

        diag, ones = _wkv_consts()

        def fwd8(i, S):
            base = pl.multiple_of(i * 8, 8)
            wb, kb, vb = (ref[pl.ds(base, 8), :] for ref in (w_ref, k_ref, v_ref))
            kapb, ab = kap_ref[pl.ds(base, 8), :], a_ref[pl.ds(base, 8), :]
            for u in range(8):
                sl = slice(u, u + 1)
                vfull = _expand(vb[sl], diag, ones)
                s_all[base + u] = S
                v_all[base + u] = vfull
                S, sa = _wkv_step(S, kapb[sl], wb[sl], kapb[sl] * ab[sl], kb[sl], vfull, ones)
                sa_all[base + u] = sa
            return S

        s_all[C] = lax.fori_loop(0, C // 8, fwd8, s0_ref[0])

        def bwd8(i, dS):
            base = pl.multiple_of((C // 8 - 1 - i) * 8, 8)
            rb, wb, kb = (ref[pl.ds(base, 8), :] for ref in (r_ref, w_ref, k_ref))
            kapb, ab, dob = kap_ref[pl.ds(base, 8), :], a_ref[pl.ds(base, 8), :], do_ref[pl.ds(base, 8), :]
            rows = [None] * 8
            for u in reversed(range(8)):
                sl = slice(u, u + 1)
                t = base + u
                s_prev, s_new, sa, vfull = s_all[t], s_all[t + 1], sa_all[t], v_all[t]
                b = kapb[sl] * ab[sl]
                dofull = _expand(dob[sl], diag, ones)
                dSn = dS + dofull * rb[sl]
                dr = _colsum(s_new * dofull)
                dw = _colsum(dSn * s_prev)
                db = _colsum(dSn * sa)
                dk = _colsum(dSn * vfull)
                dsa = _dot_hi(dSn * b, ones)
                dv = _colsum(_dot_hi(dSn * kb[sl], ones) * diag)
                dkap = db * ab[sl] - _colsum(dsa * s_prev)
                da = db * kapb[sl]
                dS = dSn * wb[sl] - dsa * kapb[sl]
                rows[u] = (dr, dw, dk, dv, dkap, da)
            for j, ref in enumerate((dr_ref, dw_ref, dk_ref, dv_ref, dkap_ref, da_ref)):
                ref[pl.ds(base, 8), :] = jnp.concatenate([rows[u][j] for u in range(8)], axis=0)
            return dS

        ds_scr[...] = lax.fori_loop(0, C // 8, bwd8, ds_scr[...])

    spec = pl.BlockSpec((C, RWKV_WIDTH), lambda i: (nc - 1 - i, 0))
    big = pltpu.VMEM((C + 1, RWKV_HEAD, RWKV_WIDTH), F32)
    return pl.pallas_call(
        body, name="wkv_bwd", grid=(nc,),
        in_specs=[spec] * 6 + [pl.BlockSpec((1, RWKV_HEAD, RWKV_WIDTH), lambda i: (nc - 1 - i, 0, 0)), spec],
        out_specs=[spec] * 6,
        out_shape=[jax.ShapeDtypeStruct((T, RWKV_WIDTH), F32)] * 6,
        scratch_shapes=[pltpu.VMEM((RWKV_HEAD, RWKV_WIDTH), F32), big, big, big],
        compiler_params=_params(dimension_semantics=("arbitrary",)),
    )(r, w, k, v, kap, a, s0, d_o)


W = RWKV_WIDTH


def _softplus(y):
    return jnp.maximum(y, 0.0) + jnp.log(1.0 + jnp.exp(-jnp.abs(y)))


def _prep_fn(kr, xwa, w0, a0, k_k, k_a, wup_pad, aup_pad, ones64):
    w_log = -_softplus(-(w0 + _dot_hi(jnp.tanh(xwa), wup_pad))) - 0.5
    decay = jnp.exp(-jnp.exp(w_log))
    a = jax.nn.sigmoid(a0 + _dot_hi(xwa, aup_pad))
    kk = kr * k_k
    kap = kk / jnp.maximum(jnp.sqrt(_dot_hi(kk * kk, ones64)), 1e-12)
    k = kr * (1.0 + (a - 1.0) * k_a)
    return decay, k, kap, a


def _shift_down(p, first_row):
    rows = lax.broadcasted_iota(jnp.int32, p.shape, 0)
    return jnp.where(rows == 0, first_row, pltpu.roll(p, 1, axis=0))


def _shift_up(z, last_row):
    n = z.shape[0]
    rows = lax.broadcasted_iota(jnp.int32, z.shape, 0)
    return jnp.where(rows == n - 1, last_row, pltpu.roll(z, n - 1, axis=0))


def _prev_block_spec():
    return pl.BlockSpec((8, RWKV_COLS), lambda i: (jnp.maximum(i * (TOK_TILE // 8) - 1, 0), 0))


def _mixed(p_ref, prev8_ref, mu_ref, first_tile):
    p = p_ref[...]
    first_row = jnp.where(first_tile, 0.0, prev8_ref[7:8, :])
    prev = _shift_down(p, first_row)
    return p, prev, p + mu_ref[...] * (prev - p)


def _prep_fwd(p_rwkv, mu, w0, a0, k_k, k_a, wup_pad, aup_pad, ones64):
    T = p_rwkv.shape[0]

    def body(p_ref, prev8_ref, mu_ref, w0_ref, a0_ref, kk_ref, ka_ref, wup_ref, aup_ref, ones_ref,
             r_ref, w_ref, k_ref, v_ref, kap_ref, a_ref, g_ref):
        _, _, ps = _mixed(p_ref, prev8_ref, mu_ref, pl.program_id(0) == 0)
        decay, k, kap, a = _prep_fn(ps[:, W:2 * W], ps[:, 4 * W:], w0_ref[...], a0_ref[...], kk_ref[...], ka_ref[...],
                                    wup_ref[...], aup_ref[...], ones_ref[...])
        r_ref[...] = ps[:, 0:W]
        w_ref[...] = decay
        k_ref[...] = k
        v_ref[...] = ps[:, 2 * W:3 * W]
        kap_ref[...] = kap
        a_ref[...] = a
        g_ref[...] = ps[:, 3 * W:4 * W]

    vec = _full((1, W))
    return pl.pallas_call(
        body, name="prep_fwd", grid=(T // TOK_TILE,),
        in_specs=[_rows(TOK_TILE, RWKV_COLS), _prev_block_spec(), _full((1, RWKV_COLS)), vec, vec, vec, vec,
                  _full((2 * LORA, W)), _full((2 * LORA, W)), _full((W, W))],
        out_specs=[_rows(TOK_TILE, W)] * 7,
        out_shape=[jax.ShapeDtypeStruct((T, W), F32)] * 7,
        compiler_params=_params(dimension_semantics=("arbitrary",)),
    )(p_rwkv, p_rwkv, mu, w0, a0, k_k, k_a, wup_pad, aup_pad, ones64)


def _prep_bwd(p_rwkv, mu, w0, a0, k_k, k_a, wup_pad, aup_pad, ones64, dr, dw, dk, dv, dkap, da, dg, dr2, dk2, dv2):
    T = p_rwkv.shape[0]
    nt = T // TOK_TILE

    def body(p_ref, prev8_ref, mu_ref, w0_ref, a0_ref, kk_ref, ka_ref, wup_ref, aup_ref, ones_ref,
             dr_ref, dw_ref, dk_ref, dv_ref, dkap_ref, da_ref, dg_ref, dr2_ref, dk2_ref, dv2_ref,
             dp_ref, dmu_ref, dw0_ref, da0_ref, dkk_ref, dka_ref, dwup_ref, daup_ref, zrow_scr):
        i = pl.program_id(0)
        accs = (dmu_ref, dw0_ref, da0_ref, dkk_ref, dka_ref, dwup_ref, daup_ref)

        @pl.when(i == 0)
        def _():
            zrow_scr[...] = jnp.zeros_like(zrow_scr)
            for ref in accs:
                ref[...] = jnp.zeros_like(ref)

        p, prev, ps = _mixed(p_ref, prev8_ref, mu_ref, i == nt - 1)
        ones = ones_ref[...]
        _, vjp = jax.vjp(lambda *args: _prep_fn(*args, ones), ps[:, W:2 * W], ps[:, 4 * W:], w0_ref[...], a0_ref[...],
                         kk_ref[...], ka_ref[...], wup_ref[...], aup_ref[...])
        dkr, dxwa, dw0, da0, dkk, dka, dwup, daup = vjp(
            (dw_ref[...], dk_ref[...] + dk2_ref[...], dkap_ref[...], da_ref[...]))
        dps = jnp.concatenate([dr_ref[...] + dr2_ref[...], dkr, dv_ref[...] + dv2_ref[...], dg_ref[...], dxwa], axis=1)
        z = dps * mu_ref[...]
        dp_ref[...] = dps - z + _shift_up(z, zrow_scr[0:1, :])
        zrow_scr[0:1, :] = z[0:1, :]
        for ref, val in zip(accs, (_colsum(dps * (prev - p)), dw0, da0, dkk, dka, dwup, daup)):
            ref[...] += val

    rev = lambda i: (nt - 1 - i, 0)
    vec = _full((1, W))
    lora = _full((2 * LORA, W))
    tile = pl.BlockSpec((TOK_TILE, W), rev)
    prev8 = pl.BlockSpec((8, RWKV_COLS), lambda i: (jnp.maximum((nt - 1 - i) * (TOK_TILE // 8) - 1, 0), 0))
    return pl.pallas_call(
        body, name="prep_bwd", grid=(nt,),
        in_specs=[pl.BlockSpec((TOK_TILE, RWKV_COLS), rev), prev8, _full((1, RWKV_COLS)), vec, vec, vec, vec, lora, lora,
                  _full((W, W))] + [tile] * 10,
        out_specs=[pl.BlockSpec((TOK_TILE, RWKV_COLS), rev), _full((1, RWKV_COLS)), vec, vec, vec, vec, lora, lora],
        out_shape=[jax.ShapeDtypeStruct((T, RWKV_COLS), F32), jax.ShapeDtypeStruct((1, RWKV_COLS), F32)]
        + [jax.ShapeDtypeStruct((1, W), F32)] * 4 + [jax.ShapeDtypeStruct((2 * LORA, W), F32)] * 2,
        scratch_shapes=[pltpu.VMEM((8, RWKV_COLS), F32)],
        compiler_params=_params(dimension_semantics=("arbitrary",)),
    )(p_rwkv, p_rwkv, mu, w0, a0, k_k, k_a, wup_pad, aup_pad, ones64, dr, dw, dk, dv, dkap, da, dg, dr2, dk2, dv2)


def _silu(x):
    return x * jax.nn.sigmoid(x)


def _post_y(o, r, k, v, g_rw, ret_raw, g_ret, ret_gn_g, gn_g, gn_b, r_k, avg128, avg64, ones64):
    xc = ret_raw - _dot_hi(ret_raw, avg128)
    ret = xc * lax.rsqrt(_dot_hi(xc * xc, avg128) + RET_GN_EPS)
    y_ret = _silu(g_ret) * (ret * ret_gn_g)
    oc = o - _dot_hi(o, avg64)
    on = oc * lax.rsqrt(_dot_hi(oc * oc, avg64) + RWKV_GN_EPS) * gn_g + gn_b
    bonus = _dot_hi(r * k * r_k, ones64) * v
    y_rwkv = _silu(g_rw) * (on + bonus)
    return y_ret, y_rwkv


def _post_loss(h, final_g, target):
    err = _rmsnorm(h, final_g) - target
    return 0.5 * jnp.sum(jnp.mean(err * err, axis=-1))


def _post(o, r, k, v, g_rw, ret_raw, p_ret, x, target, ret_gn_g, gn_g, gn_b, r_k, final_g, w_out, avg128, avg64, ones64):
    T = x.shape[0]
    n_tok_out = 8

    def body(o_ref, r_ref, k_ref, v_ref, grw_ref, ret_ref, gret_ref, x_ref, tgt_ref, rg_ref, gg_ref, gb_ref, rk_ref, fg_ref,
             wo_ref, a128_ref, a64_ref, ones_ref, *outs):
        tok_outs, (dwo_ref, drg_ref, dgg_ref, dgb_ref, drk_ref, dfg_ref, loss_ref) = outs[:n_tok_out], outs[n_tok_out:]
        accs = (dwo_ref, drg_ref, dgg_ref, dgb_ref, drk_ref, dfg_ref, loss_ref)

        @pl.when(pl.program_id(0) == 0)
        def _():
            for ref in accs:
                ref[...] = jnp.zeros_like(ref)

        consts = (a128_ref[...], a64_ref[...], ones_ref[...])
        (y_ret, y_rwkv), vjp = jax.vjp(
            lambda *args: _post_y(*args, *consts), o_ref[...], r_ref[...], k_ref[...], v_ref[...], grw_ref[...], ret_ref[...],
            gret_ref[...], rg_ref[...], gg_ref[...], gb_ref[...], rk_ref[...])
        h = x_ref[...] + _dot_bf(y_ret, wo_ref[0:RET_WIDTH, :]) + _dot_bf(y_rwkv, wo_ref[RET_WIDTH:, :])
        loss, (dh, dfg) = jax.value_and_grad(_post_loss, argnums=(0, 1))(h, fg_ref[...], tgt_ref[...])
        dy_ret = _dot_nt_bf(dh, wo_ref[0:RET_WIDTH, :])
        dy_rwkv = _dot_nt_bf(dh, wo_ref[RET_WIDTH:, :])
        do, dr, dk, dv, dgrw, dret, dgret, drg, dgg, dgb, drk = vjp((dy_ret, dy_rwkv))
        for ref, val in zip(tok_outs, (dh, do, dr, dk, dv, dgrw, dret, dgret)):
            ref[...] = val
        dwo_ref[0:RET_WIDTH, :] += _dot_tn_bf(y_ret, dh)
        dwo_ref[RET_WIDTH:, :] += _dot_tn_bf(y_rwkv, dh)
        for ref, val in zip(accs[1:], (drg, dgg, dgb, drk, dfg, jnp.full((1, 128), loss, F32))):
            ref[...] += val

    tile = _rows(TOK_TILE, W)
    wide = _rows(TOK_TILE, D_MODEL)
    vec = _full((1, W))
    sq = _full((W, W))
    return pl.pallas_call(
        body, name="post", grid=(T // TOK_TILE,),
        in_specs=[tile] * 6 + [pl.BlockSpec((TOK_TILE, W), lambda i: (i, 2)), wide, wide, vec, vec, vec, vec,
                               _full((1, D_MODEL)), _full((D_MODEL, D_MODEL)), sq, sq, sq],
        out_specs=[wide] + [tile] * 7 + [_full((D_MODEL, D_MODEL)), vec, vec, vec, vec, _full((1, D_MODEL)), _full((1, 128))],
        out_shape=[jax.ShapeDtypeStruct((T, D_MODEL), F32)] + [jax.ShapeDtypeStruct((T, W), F32)] * 7
        + [jax.ShapeDtypeStruct((D_MODEL, D_MODEL), F32)] + [jax.ShapeDtypeStruct((1, W), F32)] * 4
        + [jax.ShapeDtypeStruct((1, D_MODEL), F32), jax.ShapeDtypeStruct((1, 128), F32)],
        compiler_params=_params(dimension_semantics=("arbitrary",)),
    )(o, r, k, v, g_rw, ret_raw, p_ret, x, target, ret_gn_g, gn_g, gn_b, r_k, final_g, w_out, avg128, avg64, ones64)


def _inproj_bwd_x(x, norm_g, dp_qkv, dg_ret, dp_rwkv, dh, w_ret, w_rwkv):
    T = x.shape[0]
    n_qkv = 2 * RET_QK + RET_WIDTH

    def body(x_ref, g_ref, dqkv_ref, dgret_ref, drwkv_ref, dh_ref, wr_ref, ww_ref, dx_ref, dg_ref):
        @pl.when(pl.program_id(0) == 0)
        def _():
            dg_ref[...] = jnp.zeros_like(dg_ref)

        _, vjp = jax.vjp(_rmsnorm, x_ref[...], g_ref[...])
        du = (_dot_nt_bf(dqkv_ref[...], wr_ref[:, 0:n_qkv]) + _dot_nt_bf(dgret_ref[...], wr_ref[:, n_qkv:])
              + _dot_nt_bf(drwkv_ref[...], ww_ref[...]))
        dx, dg = vjp(du)
        dx_ref[...] = dx + dh_ref[...]
        dg_ref[...] += dg

    return pl.pallas_call(
        body, name="inproj_bwd_x", grid=(T // TOK_TILE,),
        in_specs=[_rows(TOK_TILE, D_MODEL), _full((1, D_MODEL)), _rows(TOK_TILE, n_qkv), _rows(TOK_TILE, RET_WIDTH),
                  _rows(TOK_TILE, RWKV_COLS), _rows(TOK_TILE, D_MODEL), _full((D_MODEL, RET_COLS)), _full((D_MODEL, RWKV_COLS))],
        out_specs=[_rows(TOK_TILE, D_MODEL), _full((1, D_MODEL))],
        out_shape=[jax.ShapeDtypeStruct((T, D_MODEL), F32), jax.ShapeDtypeStruct((1, D_MODEL), F32)],
        compiler_params=_params(dimension_semantics=("arbitrary",)),
    )(x, norm_g, dp_qkv, dg_ret, dp_rwkv, dh, w_ret, w_rwkv)


def _grad_w(name, u, dp):
    T, n = dp.shape
    tile = 2 * TOK_TILE

    def body(u_ref, dp_ref, out_ref):
        @pl.when(pl.program_id(0) == 0)
        def _():
            out_ref[...] = jnp.zeros_like(out_ref)

        out_ref[...] += _dot_tn_bf(u_ref[...], dp_ref[...])

    return pl.pallas_call(
        body, name=name, grid=(T // tile,),
        in_specs=[_rows(tile, D_MODEL), _rows(tile, n)],
        out_specs=_full((D_MODEL, n)),
        out_shape=jax.ShapeDtypeStruct((D_MODEL, n), F32),
        compiler_params=_params(dimension_semantics=("arbitrary",)),
    )(u, dp)


def _pad_lora(w_up, top):
    z = jnp.zeros_like(w_up)
    return jnp.concatenate([w_up, z] if top else [z, w_up], axis=0)


def _local_grads(x, target, norm_g, w_in_bf, ret_gn_g, mu, w_lora_up, w0, a_lora_up, a0, k_k, k_a, r_k, gn_g, gn_b,
                 w_out_bf, final_g):
    T = x.shape[0]
    tabs = _rope_tables(T) + _ret_tables()
    ones64 = _block_ones(W, RWKV_HEAD)
    avg64 = _block_ones(W, RWKV_HEAD, 1.0 / RWKV_HEAD)
    avg128 = _block_ones(RET_WIDTH, RET_DV, 1.0 / RET_DV)
    w_ret, w_rwkv = w_in_bf[:, :RET_COLS], w_in_bf[:, RET_COLS:]
    wup_pad, aup_pad = _pad_lora(w_lora_up, True), _pad_lora(a_lora_up, False)

    p_ret, p_rwkv, u = _inproj(x, norm_g, w_ret, w_rwkv)
    ret_raw, s_saved = _ret_fwd(p_ret, tabs)
    r, w, k, v, kap, a, g_rw = _prep_fwd(p_rwkv, mu, w0, a0, k_k, k_a, wup_pad, aup_pad, ones64)
    o, s0 = _wkv_fwd(r, w, k, v, kap, a)
    (dh, do, dr2, dk2, dv2, dgrw, dret, dgret, d_w_out, d_ret_gn_g, d_gn_g, d_gn_b, d_r_k, d_final_g, loss) = _post(
        o, r, k, v, g_rw, ret_raw, p_ret, x, target, ret_gn_g, gn_g, gn_b, r_k, final_g, w_out_bf, avg128, avg64, ones64)
    dr, dw, dk, dv, dkap, da = _wkv_bwd(r, w, k, v, kap, a, s0, do)
    dp_rwkv, d_mu, d_w0, d_a0, d_k_k, d_k_a, d_wup, d_aup = _prep_bwd(
        p_rwkv, mu, w0, a0, k_k, k_a, wup_pad, aup_pad, ones64, dr, dw, dk, dv, dkap, da, dgrw, dr2, dk2, dv2)
    dp_qkv = _ret_bwd(p_ret, s_saved, dret, tabs)
    dx, d_norm_g = _inproj_bwd_x(x, norm_g, dp_qkv, dgret, dp_rwkv, dh, w_ret, w_rwkv)
    d_w_in = jnp.concatenate([_grad_w("grad_w_qkv", u, dp_qkv), _grad_w("grad_w_gret", u, dgret),
                              _grad_w("grad_w_rwkv", u, dp_rwkv)], axis=1)
    grads = dict(norm_g=d_norm_g, w_in=d_w_in, ret_gn_g=d_ret_gn_g, rwkv_mu=d_mu, w_lora_up=d_wup[:LORA], w0=d_w0,
                 a_lora_up=d_aup[LORA:], a0=d_a0, k_k=d_k_k, k_a=d_k_a, r_k=d_r_k, rwkv_gn_g=d_gn_g, rwkv_gn_b=d_gn_b,
                 w_out=d_w_out, final_norm_g=d_final_g)
    return loss, dx, grads


def _mesh_pos():
    return lax.axis_index("x"), lax.axis_index("y"), lax.axis_index("c")


def _all_gather(shards):
    n = len(shards)

    def body(*refs):
        x_refs, out_refs = refs[:n], refs[n:2 * n]
        send_sems, recv_sems, local_sems = refs[2 * n:]
        x, y, c = _mesh_pos()
        me, sibling = (x, y, c), (x, y, 1 - c)
        chips = [(1 - x, y), (x, 1 - y), (1 - x, 1 - y)]

        def rows(a, pos):
            m = x_refs[a].shape[0]
            return out_refs[a].at[pl.ds((4 * pos[0] + 2 * pos[1] + pos[2]) * m, m), :]

        def copy(a, k, block, to, src=None):
            return pltpu.make_async_remote_copy(
                src_ref=rows(a, block) if src is None else src, dst_ref=rows(a, block),
                send_sem=send_sems.at[a, k], recv_sem=recv_sems.at[a, k], device_id=to, device_id_type=MESH)

        mine = [pltpu.make_async_copy(x_refs[a], rows(a, me), local_sems.at[a]) for a in range(n)]
        for cp in mine:
            cp.start()
        first = []
        for a in range(n):
            first.append(copy(a, 0, me, sibling, src=x_refs[a]))
            first += [copy(a, 1 + j, me, (*chip, c), src=x_refs[a]) for j, chip in enumerate(chips)]
        for cp in first:
            cp.start()
        passed = []
        for j, chip in enumerate(chips):
            for a in range(n):
                copy(a, 1 + j, (*chip, c), me).wait_recv()
                passed.append(copy(a, 4 + j, (*chip, c), sibling))
                passed[-1].start()
        for a in range(n):
            copy(a, 0, sibling, me).wait_recv()
            for j, chip in enumerate(chips):
                copy(a, 4 + j, (*chip, 1 - c), me).wait_recv()
        for cp in first + passed:
            cp.wait_send()
        for cp in mine:
            cp.wait()

    vmem = pl.BlockSpec(memory_space=pltpu.VMEM)
    return pl.pallas_call(
        body, name="gather_weights",
        out_shape=[jax.ShapeDtypeStruct((N_DEV * s.shape[0], s.shape[1]), s.dtype) for s in shards],
        in_specs=[vmem] * n, out_specs=[vmem] * n,
        scratch_shapes=[pltpu.SemaphoreType.DMA((n, 7)), pltpu.SemaphoreType.DMA((n, 7)), pltpu.SemaphoreType.DMA((n,))],
        compiler_params=_params(),
    )(*shards)


def _exchange(parts):
    n = len(parts)

    def body(*refs):
        in_refs, out_refs = refs[:n], refs[n:2 * n]
        send_sems, recv_sems, local_sems = refs[2 * n:]
        x, y, c = _mesh_pos()
        me = 4 * x + 2 * y + c
        own = [pltpu.make_async_copy(in_refs[a].at[me], out_refs[a].at[me], local_sems.at[a]) for a in range(n)]
        for cp in own:
            cp.start()
        copies = []
        for k in range(1, N_DEV):
            peer = (x ^ (k >> 2), y ^ ((k >> 1) & 1), c ^ (k & 1))
            peer_idx = 4 * peer[0] + 2 * peer[1] + peer[2]
            for a in range(n):
                copies.append(pltpu.make_async_remote_copy(
                    src_ref=in_refs[a].at[peer_idx], dst_ref=out_refs[a].at[me],
                    send_sem=send_sems.at[a, k - 1], recv_sem=recv_sems.at[a, k - 1], device_id=peer, device_id_type=MESH))
        for cp in copies:
            cp.start()
        for cp in copies:
            cp.wait()
        for cp in own:
            cp.wait()

    hbm = pl.BlockSpec(memory_space=pl.ANY)
    return pl.pallas_call(
        body, name="exchange_grads",
        out_shape=[jax.ShapeDtypeStruct(p.shape, p.dtype) for p in parts],
        in_specs=[hbm] * n, out_specs=[hbm] * n,
        scratch_shapes=[pltpu.SemaphoreType.DMA((n, 7)), pltpu.SemaphoreType.DMA((n, 7)), pltpu.SemaphoreType.DMA((n,))],
        compiler_params=_params(),
    )(*parts)


def _adamw(w, g, m, v):
    m = ADAM_B1 * m + (1.0 - ADAM_B1) * g
    v = ADAM_B2 * v + (1.0 - ADAM_B2) * (g * g)
    m_hat = m / (1.0 - ADAM_B1 ** ADAM_STEP)
    v_hat = v / (1.0 - ADAM_B2 ** ADAM_STEP)
    return -ADAM_LR * (m_hat / (jnp.sqrt(v_hat) + ADAM_EPS) + ADAM_WD * w), m, v


def _reduce_adamw(name, parts, w, m, v, row_tile):
    _, rows, cols = parts.shape

    def body(p_ref, w_ref, m_ref, v_ref, g_ref, d_ref, nm_ref, nv_ref):
        g = p_ref[0]
        for s in range(1, N_DEV):
            g = g + p_ref[s]
        g_ref[...] = g
        d_ref[...], nm_ref[...], nv_ref[...] = _adamw(w_ref[...], g, m_ref[...], v_ref[...])

    tile = pl.BlockSpec((row_tile, cols), lambda i: (i, 0))
    return pl.pallas_call(
        body, name=name, grid=(rows // row_tile,),
        in_specs=[pl.BlockSpec((N_DEV, row_tile, cols), lambda i: (0, i, 0)), tile, tile, tile],
        out_specs=[tile] * 4,
        out_shape=[jax.ShapeDtypeStruct((rows, cols), F32)] * 4,
        compiler_params=_params(dimension_semantics=("arbitrary",)),
    )(parts, w, m, v)


_SMALL = (("norm_g", 1024), ("ret_gn_g", 512), ("rwkv_mu", 2176), ("w0", 512), ("a0", 512), ("k_k", 512), ("k_a", 512),
          ("r_k", 512), ("rwkv_gn_g", 512), ("rwkv_gn_b", 512), ("final_norm_g", 1024))
_SMALL_ROWS = 72
_WEIGHTS = ("norm_g", "w_in", "ret_gn_g", "rwkv_mu", "w_lora_up", "w0", "a_lora_up", "a0", "k_k", "k_a", "r_k", "rwkv_gn_g",
            "rwkv_gn_b", "w_out", "final_norm_g")


def _pack_small(d, extra_row=None):
    rows = [d[n].reshape(-1, 128) for n, _ in _SMALL]
    used = sum(r.shape[0] for r in rows)
    rows.append(jnp.zeros((1, 128), F32) if extra_row is None else extra_row)
    rows.append(jnp.zeros((_SMALL_ROWS - used - 1, 128), F32))
    return jnp.concatenate(rows, axis=0)


def _unpack_small(packed, shapes):
    out, row = {}, 0
    for n, size in _SMALL:
        out[n] = packed[row:row + size // 128].reshape(shapes[n])
        row += size // 128
    return out, packed[row, 0]


def kernel(x, norm_g, w_in, ret_gn_g, rwkv_mu, w_lora_up, w0, a_lora_up, a0, k_k, k_a, r_k, rwkv_gn_g, rwkv_gn_b, w_out, final_norm_g, loss_target, m_norm_g, m_w_in, m_ret_gn_g, m_rwkv_mu, m_w_lora_up, m_w0, m_a_lora_up, m_a0, m_k_k, m_k_a, m_r_k, m_rwkv_gn_g, m_rwkv_gn_b, m_w_out, m_final_norm_g, v_norm_g, v_w_in, v_ret_gn_g, v_rwkv_mu, v_w_lora_up, v_w0, v_a_lora_up, v_a0, v_k_k, v_k_a, v_r_k, v_rwkv_gn_g, v_rwkv_gn_b, v_w_out, v_final_norm_g):
    wts = dict(norm_g=norm_g, w_in=w_in, ret_gn_g=ret_gn_g, rwkv_mu=rwkv_mu, w_lora_up=w_lora_up, w0=w0, a_lora_up=a_lora_up,
               a0=a0, k_k=k_k, k_a=k_a, r_k=r_k, rwkv_gn_g=rwkv_gn_g, rwkv_gn_b=rwkv_gn_b, w_out=w_out,
               final_norm_g=final_norm_g)
    mom = dict(norm_g=m_norm_g, w_in=m_w_in, ret_gn_g=m_ret_gn_g, rwkv_mu=m_rwkv_mu, w_lora_up=m_w_lora_up, w0=m_w0,
               a_lora_up=m_a_lora_up, a0=m_a0, k_k=m_k_k, k_a=m_k_a, r_k=m_r_k, rwkv_gn_g=m_rwkv_gn_g,
               rwkv_gn_b=m_rwkv_gn_b, w_out=m_w_out, final_norm_g=m_final_norm_g)
    var = dict(norm_g=v_norm_g, w_in=v_w_in, ret_gn_g=v_ret_gn_g, rwkv_mu=v_rwkv_mu, w_lora_up=v_w_lora_up, w0=v_w0,
               a_lora_up=v_a_lora_up, a0=v_a0, k_k=v_k_k, k_a=v_k_a, r_k=v_r_k, rwkv_gn_g=v_rwkv_gn_g,
               rwkv_gn_b=v_rwkv_gn_b, w_out=v_w_out, final_norm_g=v_final_norm_g)
    shapes = {n: wts[n].shape for n in _WEIGHTS}

    g_in, g_out, g_wup, g_aup = _all_gather(
        [w_in[0].astype(BF16), w_out[0].astype(BF16), w_lora_up[0], a_lora_up[0]])
    w_in_bf = g_in.reshape(N_DEV, D_MODEL, SHARD_IN).transpose(1, 0, 2).reshape(D_MODEL, IN_COLS)
    wup_full = g_wup.reshape(N_DEV, LORA, SHARD_LORA).transpose(1, 0, 2).reshape(LORA, W)
    aup_full = g_aup.reshape(N_DEV, LORA, SHARD_LORA).transpose(1, 0, 2).reshape(LORA, W)

    loss, dx, g = _local_grads(
        x[0], loss_target[0], norm_g, w_in_bf, ret_gn_g, rwkv_mu, wup_full, w0, aup_full, a0, k_k, k_a,
        r_k.reshape(1, W), rwkv_gn_g, rwkv_gn_b, g_out, final_norm_g.reshape(1, D_MODEL))

    small = _pack_small(g, loss[:, :128])
    parts = _exchange([
        g["w_in"].reshape(D_MODEL, N_DEV, SHARD_IN).transpose(1, 0, 2),
        g["w_out"].reshape(N_DEV, SHARD_OUT, D_MODEL),
        g["w_lora_up"].reshape(LORA, N_DEV, SHARD_LORA).transpose(1, 0, 2),
        g["a_lora_up"].reshape(LORA, N_DEV, SHARD_LORA).transpose(1, 0, 2),
        jnp.broadcast_to(small[None], (N_DEV, _SMALL_ROWS, 128))])
    res = {}
    res["w_in"] = _reduce_adamw("adamw_w_in", parts[0], w_in[0], m_w_in[0], v_w_in[0], 256)
    res["w_out"] = _reduce_adamw("adamw_w_out", parts[1], w_out[0], m_w_out[0], v_w_out[0], SHARD_OUT)
    res["w_lora_up"] = _reduce_adamw("adamw_w_lora_up", parts[2], w_lora_up[0], m_w_lora_up[0], v_w_lora_up[0], LORA)
    res["a_lora_up"] = _reduce_adamw("adamw_a_lora_up", parts[3], a_lora_up[0], m_a_lora_up[0], v_a_lora_up[0], LORA)
    sm = _reduce_adamw("adamw_vectors", parts[4], _pack_small(wts), _pack_small(mom), _pack_small(var), _SMALL_ROWS)
    unpacked = [_unpack_small(t, shapes) for t in sm]
    total_loss = unpacked[0][1]
    for n, _ in _SMALL:
        res[n] = [u[0][n] for u in unpacked]
    for n in ("w_in", "w_out", "w_lora_up", "a_lora_up"):
        res[n] = [t.reshape(shapes[n]) for t in res[n]]
    return (total_loss, dx[None], *[res[n][0] for n in _WEIGHTS], *[res[n][1] for n in _WEIGHTS],
            *[res[n][2] for n in _WEIGHTS], *[res[n][3] for n in _WEIGHTS])
```

```python
import functools

import numpy as np
import jax
import jax.numpy as jnp
from jax import lax
from jax.experimental import pallas as pl
from jax.experimental.pallas import tpu as pltpu

F32 = jnp.float32
BF16 = jnp.bfloat16
HI = lax.Precision.HIGHEST

D_MODEL = 1024
CHUNK = 64
RET_HEADS = 4
RET_DV = 128
RET_DK = 64
RET_QK = 256
RET_WIDTH = 512
RWKV_WIDTH = 512
RWKV_HEAD = 64
RWKV_HEADS = 8
LORA = 64
RET_COLS = 2 * RET_QK + 2 * RET_WIDTH
RWKV_COLS = 4 * RWKV_WIDTH + 2 * LORA
IN_COLS = RET_COLS + RWKV_COLS
ROPE_BASE = 10000.0
RMS_EPS = 1e-6
RET_GN_EPS = 1e-5
RWKV_GN_EPS = 64e-5
ADAM_LR = 0.001
ADAM_B1 = 0.9
ADAM_B2 = 0.999
ADAM_EPS = 1e-08
ADAM_WD = 0.01
ADAM_STEP = 10
N_DEV = 8
SHARD_IN = IN_COLS // N_DEV
SHARD_OUT = D_MODEL // N_DEV
SHARD_LORA = RWKV_WIDTH // N_DEV
VMEM_LIMIT = 56 * 1024 * 1024
TOK_TILE = 256
WKV_CHUNK = 32

MESH = pl.DeviceIdType.MESH


def _dot_hi(a, b):
    return jnp.dot(a, b, precision=HI, preferred_element_type=F32)


def _dot_nt_hi(a, b):
    return lax.dot_general(a, b, (((1,), (1,)), ((), ())), precision=HI, preferred_element_type=F32)


def _dot_tn_hi(a, b):
    return lax.dot_general(a, b, (((0,), (0,)), ((), ())), precision=HI, preferred_element_type=F32)


def _dot_bf(a, b):
    return jnp.dot(a.astype(BF16), b.astype(BF16), preferred_element_type=F32)


def _dot_nt_bf(a, b):
    return lax.dot_general(a.astype(BF16), b.astype(BF16), (((1,), (1,)), ((), ())), preferred_element_type=F32)


def _dot_tn_bf(a, b):
    return lax.dot_general(a.astype(BF16), b.astype(BF16), (((0,), (0,)), ((), ())), preferred_element_type=F32)


def _params(**kw):
    return pltpu.CompilerParams(vmem_limit_bytes=VMEM_LIMIT, **kw)


def _full(shape):
    nd = len(shape)
    return pl.BlockSpec(shape, lambda i, _nd=nd: (0,) * _nd)


def _rows(tile, width):
    return pl.BlockSpec((tile, width), lambda i: (i, 0))


def _block_ones(n, blk, scale=1.0):
    idx = np.arange(n) // blk
    return jnp.asarray((idx[:, None] == idx[None, :]).astype(np.float32) * scale)


def _rope_tables(T):
    half = RET_DK // 2
    expo = -np.arange(half, dtype=np.float32) / np.float32(half)
    freqs = np.exp(expo * np.float32(np.log(ROPE_BASE))).astype(np.float32)
    ang = np.arange(T, dtype=np.float32)[:, None] * freqs[None, :]
    cos, sin = np.cos(ang).astype(np.float32), np.sin(ang).astype(np.float32)
    cos_h = np.concatenate([cos, cos], axis=1)
    sin_h = np.concatenate([-sin, sin], axis=1)
    cos_t = np.tile(cos_h, (1, RET_HEADS))
    sin_t = np.tile(sin_h, (1, RET_HEADS))
    d = np.arange(RET_QK)
    src = (d // RET_DK) * RET_DK + ((d % RET_DK) + half) % RET_DK
    prot = np.zeros((RET_QK, RET_QK), np.float32)
    prot[src, d] = 1.0
    return jnp.asarray(cos_t), jnp.asarray(sin_t), jnp.asarray(prot)


def _ret_tables():
    h = np.arange(RET_HEADS, dtype=np.float32)
    lg = np.log(1.0 - np.exp2(-5.0 - h)).astype(np.float32)
    idx = np.arange(CHUNK, dtype=np.float32)
    intra = np.exp(lg[:, None, None] * np.abs(idx[:, None] - idx[None, :])).astype(np.float32)
    q_dec = np.exp(lg[:, None] * (idx[None, :] + 1.0)).astype(np.float32)
    k_dec = np.exp(lg[:, None] * (CHUNK - 1.0 - idx[None, :])).astype(np.float32)
    chunk_dec = np.exp(lg * CHUNK).astype(np.float32)
    lane_head = np.arange(RET_QK) // RET_DK
    mask = (lane_head[None, :] == np.arange(RET_HEADS)[:, None]).astype(np.float32)
    m = np.broadcast_to(mask[:, None, :], (RET_HEADS, CHUNK, RET_QK)).copy()
    qd = m * q_dec[:, :, None]
    kd = m * k_dec[:, :, None]
    return jnp.asarray(intra), jnp.asarray(m), jnp.asarray(qd), jnp.asarray(kd), [float(c) for c in chunk_dec]


def _rmsnorm(x, g):
    return x * lax.rsqrt(jnp.mean(x * x, axis=-1, keepdims=True) + RMS_EPS) * g


def _inproj(x, norm_g, w_ret, w_rwkv):
    T = x.shape[0]

    def body(x_ref, g_ref, wr_ref, ww_ref, pr_ref, pw_ref, u_ref):
        ub = _rmsnorm(x_ref[...], g_ref[...]).astype(BF16)
        u_ref[...] = ub
        pr_ref[...] = jnp.dot(ub, wr_ref[...], preferred_element_type=F32)
        pw_ref[...] = jnp.dot(ub, ww_ref[...], preferred_element_type=F32)

    return pl.pallas_call(
        body, name="inproj", grid=(T // TOK_TILE,),
        in_specs=[_rows(TOK_TILE, D_MODEL), _full((1, D_MODEL)), _full((D_MODEL, RET_COLS)), _full((D_MODEL, RWKV_COLS))],
        out_specs=[_rows(TOK_TILE, RET_COLS), _rows(TOK_TILE, RWKV_COLS), _rows(TOK_TILE, D_MODEL)],
        out_shape=[jax.ShapeDtypeStruct((T, RET_COLS), F32), jax.ShapeDtypeStruct((T, RWKV_COLS), F32),
                   jax.ShapeDtypeStruct((T, D_MODEL), BF16)],
        compiler_params=_params(dimension_semantics=("arbitrary",)),
    )(x, norm_g, w_ret, w_rwkv)


def _ret_chunk(pq, pk, v_heads, s_heads, cos_t, sin_t, prot, dec, hm, qd, kd, chunk_dec):
    q = pq * cos_t + _dot_hi(pq, prot) * sin_t
    k = (pk * cos_t + _dot_hi(pk, prot) * sin_t) * (RET_DK ** -0.5)
    outs, s_out = [], []
    for h in range(RET_HEADS):
        sc = _dot_nt_hi(q * hm[h], k * hm[h]) * dec[h]
        intra = _dot_hi(sc, v_heads[h])
        kv = _dot_tn_hi(k * kd[h], v_heads[h])
        inter = _dot_hi(q * qd[h], s_heads[h])
        outs.append(intra + inter)
        s_out.append(s_heads[h] * chunk_dec[h] + kv)
    return tuple(outs), tuple(s_out)


def _ret_specs():
    const = [_full((RET_QK, RET_QK)), _full((RET_HEADS, CHUNK, CHUNK)), _full((RET_HEADS, CHUNK, RET_QK)),
             _full((RET_HEADS, CHUNK, RET_QK)), _full((RET_HEADS, CHUNK, RET_QK))]
    return const


def _ret_fwd(p_ret, tabs):
    T = p_ret.shape[0]
    nc = T // CHUNK
    cos_t, sin_t, prot, dec, hm, qd, kd, chunk_dec = tabs

    def body(p_ref, cos_ref, sin_ref, prot_ref, dec_ref, hm_ref, qd_ref, kd_ref, out_ref, sin_save_ref, s_scr):
        @pl.when(pl.program_id(0) == 0)
        def _():
            s_scr[...] = jnp.zeros_like(s_scr)

        s_heads = tuple(s_scr[h] for h in range(RET_HEADS))
        for h in range(RET_HEADS):
            sin_save_ref[0, h] = s_heads[h]
        pq = p_ref[:, 0:RET_QK]
        pk = p_ref[:, RET_QK:2 * RET_QK]
        v_heads = tuple(p_ref[:, 2 * RET_QK + RET_DV * h:2 * RET_QK + RET_DV * (h + 1)] for h in range(RET_HEADS))
        outs, s_out = _ret_chunk(pq, pk, v_heads, s_heads, cos_ref[...], sin_ref[...], prot_ref[...], dec_ref[...],
                                 hm_ref[...], qd_ref[...], kd_ref[...], chunk_dec)
        for h in range(RET_HEADS):
            out_ref[:, RET_DV * h:RET_DV * (h + 1)] = outs[h]
            s_scr[h] = s_out[h]

    return pl.pallas_call(
        body, name="ret_fwd", grid=(nc,),
        in_specs=[pl.BlockSpec((CHUNK, RET_COLS), lambda i: (i, 0)), _rows(CHUNK, RET_QK), _rows(CHUNK, RET_QK)] + _ret_specs(),
        out_specs=[_rows(CHUNK, RET_WIDTH), pl.BlockSpec((1, RET_HEADS, RET_QK, RET_DV), lambda i: (i, 0, 0, 0))],
        out_shape=[jax.ShapeDtypeStruct((T, RET_WIDTH), F32), jax.ShapeDtypeStruct((nc, RET_HEADS, RET_QK, RET_DV), F32)],
        scratch_shapes=[pltpu.VMEM((RET_HEADS, RET_QK, RET_DV), F32)],
        compiler_params=_params(dimension_semantics=("arbitrary",)),
    )(p_ret, cos_t, sin_t, prot, dec, hm, qd, kd)


def _ret_bwd(p_ret, s_saved, d_ret, tabs):
    T = p_ret.shape[0]
    nc = T // CHUNK
    cos_t, sin_t, prot, dec, hm, qd, kd, chunk_dec = tabs

    def body(p_ref, s_ref, dret_ref, cos_ref, sin_ref, prot_ref, dec_ref, hm_ref, qd_ref, kd_ref, dp_ref, ds_scr):
        @pl.when(pl.program_id(0) == 0)
        def _():
            ds_scr[...] = jnp.zeros_like(ds_scr)

        pq = p_ref[:, 0:RET_QK]
        pk = p_ref[:, RET_QK:2 * RET_QK]
        v_heads = tuple(p_ref[:, 2 * RET_QK + RET_DV * h:2 * RET_QK + RET_DV * (h + 1)] for h in range(RET_HEADS))
        s_heads = tuple(s_ref[0, h] for h in range(RET_HEADS))
        consts = (cos_ref[...], sin_ref[...], prot_ref[...], dec_ref[...], hm_ref[...], qd_ref[...], kd_ref[...])
        _, vjp = jax.vjp(lambda a, b, c, d: _ret_chunk(a, b, c, d, *consts, chunk_dec), pq, pk, v_heads, s_heads)
        d_out = tuple(dret_ref[:, RET_DV * h:RET_DV * (h + 1)] for h in range(RET_HEADS))
        d_s = tuple(ds_scr[h] for h in range(RET_HEADS))
        dq, dk, dv, ds_in = vjp((d_out, d_s))
        dp_ref[:, 0:RET_QK] = dq
        dp_ref[:, RET_QK:2 * RET_QK] = dk
        for h in range(RET_HEADS):
            dp_ref[:, 2 * RET_QK + RET_DV * h:2 * RET_QK + RET_DV * (h + 1)] = dv[h]
            ds_scr[h] = ds_in[h]

    rev = lambda i: (nc - 1 - i, 0)
    return pl.pallas_call(
        body, name="ret_bwd", grid=(nc,),
        in_specs=[pl.BlockSpec((CHUNK, RET_COLS), rev),
                  pl.BlockSpec((1, RET_HEADS, RET_QK, RET_DV), lambda i: (nc - 1 - i, 0, 0, 0)),
                  pl.BlockSpec((CHUNK, RET_WIDTH), rev), pl.BlockSpec((CHUNK, RET_QK), rev), pl.BlockSpec((CHUNK, RET_QK), rev)]
        + _ret_specs(),
        out_specs=pl.BlockSpec((CHUNK, 2 * RET_QK + RET_WIDTH), rev),
        out_shape=jax.ShapeDtypeStruct((T, 2 * RET_QK + RET_WIDTH), F32),
        scratch_shapes=[pltpu.VMEM((RET_HEADS, RET_QK, RET_DV), F32)],
        compiler_params=_params(dimension_semantics=("arbitrary",)),
    )(p_ret, s_saved, d_ret, cos_t, sin_t, prot, dec, hm, qd, kd)


def _wkv_consts():
    lane = lax.broadcasted_iota(jnp.int32, (RWKV_HEAD, RWKV_WIDTH), 1)
    sub = lax.broadcasted_iota(jnp.int32, (RWKV_HEAD, RWKV_WIDTH), 0)
    diag = ((lane & (RWKV_HEAD - 1)) == sub).astype(F32)
    r = lax.broadcasted_iota(jnp.int32, (3 * 128, 128), 0)
    c = lax.broadcasted_iota(jnp.int32, (3 * 128, 128), 1)
    ones = (((r & 127) >> 6) == (c >> 6)).astype(BF16)
    return diag, ones


def _stack(x):
    return jnp.concatenate([x[:, 128 * p:128 * (p + 1)] for p in range(4)], axis=0)


def _unstack(y):
    return jnp.concatenate([y[RWKV_HEAD * p:RWKV_HEAD * (p + 1)] for p in range(4)], axis=1)


def _split(x, n):
    pieces = []
    for _ in range(n):
        p = x.astype(BF16)
        pieces.append(p)
        x = x - p.astype(F32)
    return pieces


def _lane_sum(x, ones, n):
    lhs = jnp.concatenate(_split(_stack(x), n), axis=1)
    return _unstack(jnp.dot(lhs, ones[:128 * n], preferred_element_type=F32))


def _expand(row, diag, ones):
    lhs = jnp.concatenate([_stack((p.astype(F32) * diag).astype(BF16)) for p in _split(row, 3)], axis=1)
    return _unstack(jnp.dot(lhs, ones, preferred_element_type=F32))


def _colsum(x):
    return jnp.sum(x, axis=0, keepdims=True)


def _wkv_step(S, kap, w, b, k, vfull, ones):
    sa = _lane_sum(S * (-kap), ones, 3)
    return S * w + sa * b + vfull * k, sa


def _wkv_fwd(r, w, k, v, kap, a):
    T = r.shape[0]
    C = WKV_CHUNK
    nc = T // C

    def body(r_ref, w_ref, k_ref, v_ref, kap_ref, a_ref, o_ref, s0_ref, s_scr):
        @pl.when(pl.program_id(0) == 0)
        def _():
            s_scr[...] = jnp.zeros_like(s_scr)

        diag, ones = _wkv_consts()
        s0_ref[0] = s_scr[...]

        def step8(i, S):
            base = pl.multiple_of(i * 8, 8)
            rb, wb, kb, vb = (ref[pl.ds(base, 8), :] for ref in (r_ref, w_ref, k_ref, v_ref))
            kapb, ab = kap_ref[pl.ds(base, 8), :], a_ref[pl.ds(base, 8), :]
            rows = []
            for u in range(8):
                sl = slice(u, u + 1)
                S, _ = _wkv_step(S, kapb[sl], wb[sl], kapb[sl] * ab[sl], kb[sl], _expand(vb[sl], diag, ones), ones)
                rows.append(_colsum(_lane_sum(S * rb[sl], ones, 2) * diag))
            o_ref[pl.ds(base, 8), :] = jnp.concatenate(rows, axis=0)
            return S

        s_scr[...] = lax.fori_loop(0, C // 8, step8, s_scr[...])

    spec = _rows(C, RWKV_WIDTH)
    return pl.pallas_call(
        body, name="wkv_fwd", grid=(nc,),
        in_specs=[spec] * 6,
        out_specs=[spec, pl.BlockSpec((1, RWKV_HEAD, RWKV_WIDTH), lambda i: (i, 0, 0))],
        out_shape=[jax.ShapeDtypeStruct((T, RWKV_WIDTH), F32), jax.ShapeDtypeStruct((nc, RWKV_HEAD, RWKV_WIDTH), F32)],
        scratch_shapes=[pltpu.VMEM((RWKV_HEAD, RWKV_WIDTH), F32)],
        compiler_params=_params(dimension_semantics=("arbitrary",)),
    )(r, w, k, v, kap, a)


def _wkv_bwd(r, w, k, v, kap, a, s0, d_o):
    T = r.shape[0]
    C = WKV_CHUNK
    nc = T // C

    def body(r_ref, w_ref, k_ref, v_ref, kap_ref, a_ref, s0_ref, do_ref,
             dr_ref, dw_ref, dk_ref, dv_ref, dkap_ref, da_ref, ds_scr, s_all, sa_all, v_all):
        @pl.when(pl.program_id(0) == 0)
        def _():
            ds_scr[...] = jnp.zeros_like(ds_scr)

        diag, ones = _wkv_consts()

        def fwd8(i, S):
            base = pl.multiple_of(i * 8, 8)
            wb, kb, vb = (ref[pl.ds(base, 8), :] for ref in (w_ref, k_ref, v_ref))
            kapb, ab = kap_ref[pl.ds(base, 8), :], a_ref[pl.ds(base, 8), :]
            for u in range(8):
                sl = slice(u, u + 1)
                vfull = _expand(vb[sl], diag, ones)
                s_all[base + u] = S
                v_all[base + u] = vfull
                S, sa = _wkv_step(S, kapb[sl], wb[sl], kapb[sl] * ab[sl], kb[sl], vfull, ones)
                sa_all[base + u] = sa
            return S

        s_all[C] = lax.fori_loop(0, C // 8, fwd8, s0_ref[0])

        def bwd8(i, dS):
            base = pl.multiple_of((C // 8 - 1 - i) * 8, 8)
            rb, wb, kb = (ref[pl.ds(base, 8), :] for ref in (r_ref, w_ref, k_ref))
            kapb, ab, dob = kap_ref[pl.ds(base, 8), :], a_ref[pl.ds(base, 8), :], do_ref[pl.ds(base, 8), :]
            rows = [None] * 8
            for u in reversed(range(8)):
                sl = slice(u, u + 1)
                t = base + u
                s_prev, s_new, sa, vfull = s_all[t], s_all[t + 1], sa_all[t], v_all[t]
                b = kapb[sl] * ab[sl]
                dofull = _expand(dob[sl], diag, ones)
                dSn = dS + dofull * rb[sl]
                dr = _colsum(s_new * dofull)
                dw = _colsum(dSn * s_prev)
                db = _colsum(dSn * sa)
                dk = _colsum(dSn * vfull)
                dsa = _lane_sum(dSn * b, ones, 3)
                dv = _colsum(_lane_sum(dSn * kb[sl], ones, 2) * diag)
                dkap = db * ab[sl] - _colsum(dsa * s_prev)
                da = db * kapb[sl]
                dS = dSn * wb[sl] - dsa * kapb[sl]
                rows[u] = (dr, dw, dk, dv, dkap, da)
            for j, ref in enumerate((dr_ref, dw_ref, dk_ref, dv_ref, dkap_ref, da_ref)):
                ref[pl.ds(base, 8), :] = jnp.concatenate([rows[u][j] for u in range(8)], axis=0)
            return dS

        ds_scr[...] = lax.fori_loop(0, C // 8, bwd8, ds_scr[...])

    spec = pl.BlockSpec((C, RWKV_WIDTH), lambda i: (nc - 1 - i, 0))
    big = pltpu.VMEM((C + 1, RWKV_HEAD, RWKV_WIDTH), F32)
    return pl.pallas_call(
        body, name="wkv_bwd", grid=(nc,),
        in_specs=[spec] * 6 + [pl.BlockSpec((1, RWKV_HEAD, RWKV_WIDTH), lambda i: (nc - 1 - i, 0, 0)), spec],
        out_specs=[spec] * 6,
        out_shape=[jax.ShapeDtypeStruct((T, RWKV_WIDTH), F32)] * 6,
        scratch_shapes=[pltpu.VMEM((RWKV_HEAD, RWKV_WIDTH), F32), big, big, big],
        compiler_params=_params(dimension_semantics=("arbitrary",)),
    )(r, w, k, v, kap, a, s0, d_o)


W = RWKV_WIDTH


def _softplus(y):
    return jnp.maximum(y, 0.0) + jnp.log(1.0 + jnp.exp(-jnp.abs(y)))


def _prep_fn(kr, xwa, w0, a0, k_k, k_a, wup_pad, aup_pad, ones64):
    w_log = -_softplus(-(w0 + _dot_hi(jnp.tanh(xwa), wup_pad))) - 0.5
    decay = jnp.exp(-jnp.exp(w_log))
    a = jax.nn.sigmoid(a0 + _dot_hi(xwa, aup_pad))
    kk = kr * k_k
    kap = kk / jnp.maximum(jnp.sqrt(_dot_hi(kk * kk, ones64)), 1e-12)
    k = kr * (1.0 + (a - 1.0) * k_a)
    return decay, k, kap, a


def _shift_down(p, first_row):
    rows = lax.broadcasted_iota(jnp.int32, p.shape, 0)
    return jnp.where(rows == 0, first_row, pltpu.roll(p, 1, axis=0))


def _shift_up(z, last_row):
    n = z.shape[0]
    rows = lax.broadcasted_iota(jnp.int32, z.shape, 0)
    return jnp.where(rows == n - 1, last_row, pltpu.roll(z, n - 1, axis=0))


def _prev_block_spec():
    return pl.BlockSpec((8, RWKV_COLS), lambda i: (jnp.maximum(i * (TOK_TILE // 8) - 1, 0), 0))


def _mixed(p_ref, prev8_ref, mu_ref, first_tile):
    p = p_ref[...]
    first_row = jnp.where(first_tile, 0.0, prev8_ref[7:8, :])
    prev = _shift_down(p, first_row)
    return p, prev, p + mu_ref[...] * (prev - p)


def _prep_fwd(p_rwkv, mu, w0, a0, k_k, k_a, wup_pad, aup_pad, ones64):
    T = p_rwkv.shape[0]

    def body(p_ref, prev8_ref, mu_ref, w0_ref, a0_ref, kk_ref, ka_ref, wup_ref, aup_ref, ones_ref,
             r_ref, w_ref, k_ref, v_ref, kap_ref, a_ref, g_ref):
        _, _, ps = _mixed(p_ref, prev8_ref, mu_ref, pl.program_id(0) == 0)
        decay, k, kap, a = _prep_fn(ps[:, W:2 * W], ps[:, 4 * W:], w0_ref[...], a0_ref[...], kk_ref[...], ka_ref[...],
                                    wup_ref[...], aup_ref[...], ones_ref[...])
        r_ref[...] = ps[:, 0:W]
        w_ref[...] = decay
        k_ref[...] = k
        v_ref[...] = ps[:, 2 * W:3 * W]
        kap_ref[...] = kap
        a_ref[...] = a
        g_ref[...] = ps[:, 3 * W:4 * W]

    vec = _full((1, W))
    return pl.pallas_call(
        body, name="prep_fwd", grid=(T // TOK_TILE,),
        in_specs=[_rows(TOK_TILE, RWKV_COLS), _prev_block_spec(), _full((1, RWKV_COLS)), vec, vec, vec, vec,
                  _full((2 * LORA, W)), _full((2 * LORA, W)), _full((W, W))],
        out_specs=[_rows(TOK_TILE, W)] * 7,
        out_shape=[jax.ShapeDtypeStruct((T, W), F32)] * 7,
        compiler_params=_params(dimension_semantics=("arbitrary",)),
    )(p_rwkv, p_rwkv, mu, w0, a0, k_k, k_a, wup_pad, aup_pad, ones64)


def _prep_bwd(p_rwkv, mu, w0, a0, k_k, k_a, wup_pad, aup_pad, ones64, dr, dw, dk, dv, dkap, da, dg, dr2, dk2, dv2):
    T = p_rwkv.shape[0]
    nt = T // TOK_TILE

    def body(p_ref, prev8_ref, mu_ref, w0_ref, a0_ref, kk_ref, ka_ref, wup_ref, aup_ref, ones_ref,
             dr_ref, dw_ref, dk_ref, dv_ref, dkap_ref, da_ref, dg_ref, dr2_ref, dk2_ref, dv2_ref,
             dp_ref, dmu_ref, dw0_ref, da0_ref, dkk_ref, dka_ref, dwup_ref, daup_ref, zrow_scr):
        i = pl.program_id(0)
        accs = (dmu_ref, dw0_ref, da0_ref, dkk_ref, dka_ref, dwup_ref, daup_ref)

        @pl.when(i == 0)
        def _():
            zrow_scr[...] = jnp.zeros_like(zrow_scr)
            for ref in accs:
                ref[...] = jnp.zeros_like(ref)

        p, prev, ps = _mixed(p_ref, prev8_ref, mu_ref, i == nt - 1)
        ones = ones_ref[...]
        _, vjp = jax.vjp(lambda *args: _prep_fn(*args, ones), ps[:, W:2 * W], ps[:, 4 * W:], w0_ref[...], a0_ref[...],
                         kk_ref[...], ka_ref[...], wup_ref[...], aup_ref[...])
        dkr, dxwa, dw0, da0, dkk, dka, dwup, daup = vjp(
            (dw_ref[...], dk_ref[...] + dk2_ref[...], dkap_ref[...], da_ref[...]))
        dps = jnp.concatenate([dr_ref[...] + dr2_ref[...], dkr, dv_ref[...] + dv2_ref[...], dg_ref[...], dxwa], axis=1)
        z = dps * mu_ref[...]
        dp_ref[...] = dps - z + _shift_up(z, zrow_scr[0:1, :])
        zrow_scr[0:1, :] = z[0:1, :]
        for ref, val in zip(accs, (_colsum(dps * (prev - p)), dw0, da0, dkk, dka, dwup, daup)):
            ref[...] += val

    rev = lambda i: (nt - 1 - i, 0)
    vec = _full((1, W))
    lora = _full((2 * LORA, W))
    tile = pl.BlockSpec((TOK_TILE, W), rev)
    prev8 = pl.BlockSpec((8, RWKV_COLS), lambda i: (jnp.maximum((nt - 1 - i) * (TOK_TILE // 8) - 1, 0), 0))
    return pl.pallas_call(
        body, name="prep_bwd", grid=(nt,),
        in_specs=[pl.BlockSpec((TOK_TILE, RWKV_COLS), rev), prev8, _full((1, RWKV_COLS)), vec, vec, vec, vec, lora, lora,
                  _full((W, W))] + [tile] * 10,
        out_specs=[pl.BlockSpec((TOK_TILE, RWKV_COLS), rev), _full((1, RWKV_COLS)), vec, vec, vec, vec, lora, lora],
        out_shape=[jax.ShapeDtypeStruct((T, RWKV_COLS), F32), jax.ShapeDtypeStruct((1, RWKV_COLS), F32)]
        + [jax.ShapeDtypeStruct((1, W), F32)] * 4 + [jax.ShapeDtypeStruct((2 * LORA, W), F32)] * 2,
        scratch_shapes=[pltpu.VMEM((8, RWKV_COLS), F32)],
        compiler_params=_params(dimension_semantics=("arbitrary",)),
    )(p_rwkv, p_rwkv, mu, w0, a0, k_k, k_a, wup_pad, aup_pad, ones64, dr, dw, dk, dv, dkap, da, dg, dr2, dk2, dv2)


def _silu(x):
    return x * jax.nn.sigmoid(x)


def _post_y(o, r, k, v, g_rw, ret_raw, g_ret, ret_gn_g, gn_g, gn_b, r_k, avg128, avg64, ones64):
    xc = ret_raw - _dot_hi(ret_raw, avg128)
    ret = xc * lax.rsqrt(_dot_hi(xc * xc, avg128) + RET_GN_EPS)
    y_ret = _silu(g_ret) * (ret * ret_gn_g)
    oc = o - _dot_hi(o, avg64)
    on = oc * lax.rsqrt(_dot_hi(oc * oc, avg64) + RWKV_GN_EPS) * gn_g + gn_b
    bonus = _dot_hi(r * k * r_k, ones64) * v
    y_rwkv = _silu(g_rw) * (on + bonus)
    return y_ret, y_rwkv


def _post_loss(h, final_g, target):
    err = _rmsnorm(h, final_g) - target
    return 0.5 * jnp.sum(jnp.mean(err * err, axis=-1))


def _post(o, r, k, v, g_rw, ret_raw, p_ret, x, target, ret_gn_g, gn_g, gn_b, r_k, final_g, w_out, avg128, avg64, ones64):
    T = x.shape[0]
    n_tok_out = 8

    def body(o_ref, r_ref, k_ref, v_ref, grw_ref, ret_ref, gret_ref, x_ref, tgt_ref, rg_ref, gg_ref, gb_ref, rk_ref, fg_ref,
             wo_ref, a128_ref, a64_ref, ones_ref, *outs):
        tok_outs, (dwo_ref, drg_ref, dgg_ref, dgb_ref, drk_ref, dfg_ref, loss_ref) = outs[:n_tok_out], outs[n_tok_out:]
        accs = (dwo_ref, drg_ref, dgg_ref, dgb_ref, drk_ref, dfg_ref, loss_ref)

        @pl.when(pl.program_id(0) == 0)
        def _():
            for ref in accs:
                ref[...] = jnp.zeros_like(ref)

        consts = (a128_ref[...], a64_ref[...], ones_ref[...])
        (y_ret, y_rwkv), vjp = jax.vjp(
            lambda *args: _post_y(*args, *consts), o_ref[...], r_ref[...], k_ref[...], v_ref[...], grw_ref[...], ret_ref[...],
            gret_ref[...], rg_ref[...], gg_ref[...], gb_ref[...], rk_ref[...])
        h = x_ref[...] + _dot_bf(y_ret, wo_ref[0:RET_WIDTH, :]) + _dot_bf(y_rwkv, wo_ref[RET_WIDTH:, :])
        loss, (dh, dfg) = jax.value_and_grad(_post_loss, argnums=(0, 1))(h, fg_ref[...], tgt_ref[...])
        dy_ret = _dot_nt_bf(dh, wo_ref[0:RET_WIDTH, :])
        dy_rwkv = _dot_nt_bf(dh, wo_ref[RET_WIDTH:, :])
        do, dr, dk, dv, dgrw, dret, dgret, drg, dgg, dgb, drk = vjp((dy_ret, dy_rwkv))
        for ref, val in zip(tok_outs, (dh, do, dr, dk, dv, dgrw, dret, dgret)):
            ref[...] = val
        dwo_ref[0:RET_WIDTH, :] += _dot_tn_bf(y_ret, dh)
        dwo_ref[RET_WIDTH:, :] += _dot_tn_bf(y_rwkv, dh)
        for ref, val in zip(accs[1:], (drg, dgg, dgb, drk, dfg, jnp.full((1, 128), loss, F32))):
            ref[...] += val

    tile = _rows(TOK_TILE, W)
    wide = _rows(TOK_TILE, D_MODEL)
    vec = _full((1, W))
    sq = _full((W, W))
    return pl.pallas_call(
        body, name="post", grid=(T // TOK_TILE,),
        in_specs=[tile] * 6 + [pl.BlockSpec((TOK_TILE, W), lambda i: (i, 2)), wide, wide, vec, vec, vec, vec,
                               _full((1, D_MODEL)), _full((D_MODEL, D_MODEL)), sq, sq, sq],
        out_specs=[wide] + [tile] * 7 + [_full((D_MODEL, D_MODEL)), vec, vec, vec, vec, _full((1, D_MODEL)), _full((1, 128))],
        out_shape=[jax.ShapeDtypeStruct((T, D_MODEL), F32)] + [jax.ShapeDtypeStruct((T, W), F32)] * 7
        + [jax.ShapeDtypeStruct((D_MODEL, D_MODEL), F32)] + [jax.ShapeDtypeStruct((1, W), F32)] * 4
        + [jax.ShapeDtypeStruct((1, D_MODEL), F32), jax.ShapeDtypeStruct((1, 128), F32)],
        compiler_params=_params(dimension_semantics=("arbitrary",)),
    )(o, r, k, v, g_rw, ret_raw, p_ret, x, target, ret_gn_g, gn_g, gn_b, r_k, final_g, w_out, avg128, avg64, ones64)


def _inproj_bwd_x(x, norm_g, dp_qkv, dg_ret, dp_rwkv, dh, w_ret, w_rwkv):
    T = x.shape[0]
    n_qkv = 2 * RET_QK + RET_WIDTH

    def body(x_ref, g_ref, dqkv_ref, dgret_ref, drwkv_ref, dh_ref, wr_ref, ww_ref, dx_ref, dg_ref):
        @pl.when(pl.program_id(0) == 0)
        def _():
            dg_ref[...] = jnp.zeros_like(dg_ref)

        _, vjp = jax.vjp(_rmsnorm, x_ref[...], g_ref[...])
        du = (_dot_nt_bf(dqkv_ref[...], wr_ref[:, 0:n_qkv]) + _dot_nt_bf(dgret_ref[...], wr_ref[:, n_qkv:])
              + _dot_nt_bf(drwkv_ref[...], ww_ref[...]))
        dx, dg = vjp(du)
        dx_ref[...] = dx + dh_ref[...]
        dg_ref[...] += dg

    return pl.pallas_call(
        body, name="inproj_bwd_x", grid=(T // TOK_TILE,),
        in_specs=[_rows(TOK_TILE, D_MODEL), _full((1, D_MODEL)), _rows(TOK_TILE, n_qkv), _rows(TOK_TILE, RET_WIDTH),
                  _rows(TOK_TILE, RWKV_COLS), _rows(TOK_TILE, D_MODEL), _full((D_MODEL, RET_COLS)), _full((D_MODEL, RWKV_COLS))],
        out_specs=[_rows(TOK_TILE, D_MODEL), _full((1, D_MODEL))],
        out_shape=[jax.ShapeDtypeStruct((T, D_MODEL), F32), jax.ShapeDtypeStruct((1, D_MODEL), F32)],
        compiler_params=_params(dimension_semantics=("arbitrary",)),
    )(x, norm_g, dp_qkv, dg_ret, dp_rwkv, dh, w_ret, w_rwkv)


def _grad_w(name, u, dp):
    T, n = dp.shape
    tile = 2 * TOK_TILE

    def body(u_ref, dp_ref, out_ref):
        @pl.when(pl.program_id(0) == 0)
        def _():
            out_ref[...] = jnp.zeros_like(out_ref)

        out_ref[...] += _dot_tn_bf(u_ref[...], dp_ref[...])

    return pl.pallas_call(
        body, name=name, grid=(T // tile,),
        in_specs=[_rows(tile, D_MODEL), _rows(tile, n)],
        out_specs=_full((D_MODEL, n)),
        out_shape=jax.ShapeDtypeStruct((D_MODEL, n), F32),
        compiler_params=_params(dimension_semantics=("arbitrary",)),
    )(u, dp)


def _pad_lora(w_up, top):
    z = jnp.zeros_like(w_up)
    return jnp.concatenate([w_up, z] if top else [z, w_up], axis=0)


def _local_grads(x, target, norm_g, w_in_bf, ret_gn_g, mu, w_lora_up, w0, a_lora_up, a0, k_k, k_a, r_k, gn_g, gn_b,
                 w_out_bf, final_g):
    T = x.shape[0]
    tabs = _rope_tables(T) + _ret_tables()
    ones64 = _block_ones(W, RWKV_HEAD)
    avg64 = _block_ones(W, RWKV_HEAD, 1.0 / RWKV_HEAD)
    avg128 = _block_ones(RET_WIDTH, RET_DV, 1.0 / RET_DV)
    w_ret, w_rwkv = w_in_bf[:, :RET_COLS], w_in_bf[:, RET_COLS:]
    wup_pad, aup_pad = _pad_lora(w_lora_up, True), _pad_lora(a_lora_up, False)

    p_ret, p_rwkv, u = _inproj(x, norm_g, w_ret, w_rwkv)
    ret_raw, s_saved = _ret_fwd(p_ret, tabs)
    r, w, k, v, kap, a, g_rw = _prep_fwd(p_rwkv, mu, w0, a0, k_k, k_a, wup_pad, aup_pad, ones64)
    o, s0 = _wkv_fwd(r, w, k, v, kap, a)
    (dh, do, dr2, dk2, dv2, dgrw, dret, dgret, d_w_out, d_ret_gn_g, d_gn_g, d_gn_b, d_r_k, d_final_g, loss) = _post(
        o, r, k, v, g_rw, ret_raw, p_ret, x, target, ret_gn_g, gn_g, gn_b, r_k, final_g, w_out_bf, avg128, avg64, ones64)
    dr, dw, dk, dv, dkap, da = _wkv_bwd(r, w, k, v, kap, a, s0, do)
    dp_rwkv, d_mu, d_w0, d_a0, d_k_k, d_k_a, d_wup, d_aup = _prep_bwd(
        p_rwkv, mu, w0, a0, k_k, k_a, wup_pad, aup_pad, ones64, dr, dw, dk, dv, dkap, da, dgrw, dr2, dk2, dv2)
    dp_qkv = _ret_bwd(p_ret, s_saved, dret, tabs)
    dx, d_norm_g = _inproj_bwd_x(x, norm_g, dp_qkv, dgret, dp_rwkv, dh, w_ret, w_rwkv)
    d_w_in = jnp.concatenate([_grad_w("grad_w_qkv", u, dp_qkv), _grad_w("grad_w_gret", u, dgret),
                              _grad_w("grad_w_rwkv", u, dp_rwkv)], axis=1)
    grads = dict(norm_g=d_norm_g, w_in=d_w_in, ret_gn_g=d_ret_gn_g, rwkv_mu=d_mu, w_lora_up=d_wup[:LORA], w0=d_w0,
                 a_lora_up=d_aup[LORA:], a0=d_a0, k_k=d_k_k, k_a=d_k_a, r_k=d_r_k, rwkv_gn_g=d_gn_g, rwkv_gn_b=d_gn_b,
                 w_out=d_w_out, final_norm_g=d_final_g)
    return loss, dx, grads


def _mesh_pos():
    return lax.axis_index("x"), lax.axis_index("y"), lax.axis_index("c")


def _all_gather(shards):
    n = len(shards)

    def body(*refs):
        x_refs, out_refs = refs[:n], refs[n:2 * n]
        send_sems, recv_sems, local_sems = refs[2 * n:]
        x, y, c = _mesh_pos()
        me, sibling = (x, y, c), (x, y, 1 - c)
        chips = [(1 - x, y), (x, 1 - y), (1 - x, 1 - y)]

        def rows(a, pos):
            m = x_refs[a].shape[0]
            return out_refs[a].at[pl.ds((4 * pos[0] + 2 * pos[1] + pos[2]) * m, m), :]

        def copy(a, k, block, to, src=None):
            return pltpu.make_async_remote_copy(
                src_ref=rows(a, block) if src is None else src, dst_ref=rows(a, block),
                send_sem=send_sems.at[a, k], recv_sem=recv_sems.at[a, k], device_id=to, device_id_type=MESH)

        mine = [pltpu.make_async_copy(x_refs[a], rows(a, me), local_sems.at[a]) for a in range(n)]
        for cp in mine:
            cp.start()
        first = []
        for a in range(n):
            first.append(copy(a, 0, me, sibling, src=x_refs[a]))
            first += [copy(a, 1 + j, me, (*chip, c), src=x_refs[a]) for j, chip in enumerate(chips)]
        for cp in first:
            cp.start()
        passed = []
        for j, chip in enumerate(chips):
            for a in range(n):
                copy(a, 1 + j, (*chip, c), me).wait_recv()
                passed.append(copy(a, 4 + j, (*chip, c), sibling))
                passed[-1].start()
        for a in range(n):
            copy(a, 0, sibling, me).wait_recv()
            for j, chip in enumerate(chips):
                copy(a, 4 + j, (*chip, 1 - c), me).wait_recv()
        for cp in first + passed:
            cp.wait_send()
        for cp in mine:
            cp.wait()

    vmem = pl.BlockSpec(memory_space=pltpu.VMEM)
    return pl.pallas_call(
        body, name="gather_weights",
        out_shape=[jax.ShapeDtypeStruct((N_DEV * s.shape[0], s.shape[1]), s.dtype) for s in shards],
        in_specs=[vmem] * n, out_specs=[vmem] * n,
        scratch_shapes=[pltpu.SemaphoreType.DMA((n, 7)), pltpu.SemaphoreType.DMA((n, 7)), pltpu.SemaphoreType.DMA((n,))],
        compiler_params=_params(),
    )(*shards)


def _exchange(parts):
    n = len(parts)

    def body(*refs):
        in_refs, out_refs = refs[:n], refs[n:2 * n]
        send_sems, recv_sems, local_sems = refs[2 * n:]
        x, y, c = _mesh_pos()
        me = 4 * x + 2 * y + c
        own = [pltpu.make_async_copy(in_refs[a].at[me], out_refs[a].at[me], local_sems.at[a]) for a in range(n)]
        for cp in own:
            cp.start()
        copies = []
        for k in range(1, N_DEV):
            peer = (x ^ (k >> 2), y ^ ((k >> 1) & 1), c ^ (k & 1))
            peer_idx = 4 * peer[0] + 2 * peer[1] + peer[2]
            for a in range(n):
                copies.append(pltpu.make_async_remote_copy(
                    src_ref=in_refs[a].at[peer_idx], dst_ref=out_refs[a].at[me],
                    send_sem=send_sems.at[a, k - 1], recv_sem=recv_sems.at[a, k - 1], device_id=peer, device_id_type=MESH))
        for cp in copies:
            cp.start()
        for cp in copies:
            cp.wait()
        for cp in own:
            cp.wait()

    hbm = pl.BlockSpec(memory_space=pl.ANY)
    return pl.pallas_call(
        body, name="exchange_grads",
        out_shape=[jax.ShapeDtypeStruct(p.shape, p.dtype) for p in parts],
        in_specs=[hbm] * n, out_specs=[hbm] * n,
        scratch_shapes=[pltpu.SemaphoreType.DMA((n, 7)), pltpu.SemaphoreType.DMA((n, 7)), pltpu.SemaphoreType.DMA((n,))],
        compiler_params=_params(),
    )(*parts)


def _adamw(w, g, m, v):
    m = ADAM_B1 * m + (1.0 - ADAM_B1) * g
    v = ADAM_B2 * v + (1.0 - ADAM_B2) * (g * g)
    m_hat = m / (1.0 - ADAM_B1 ** ADAM_STEP)
    v_hat = v / (1.0 - ADAM_B2 ** ADAM_STEP)
    return -ADAM_LR * (m_hat / (jnp.sqrt(v_hat) + ADAM_EPS) + ADAM_WD * w), m, v


def _reduce_adamw(name, parts, w, m, v, row_tile):
    _, rows, cols = parts.shape

    def body(p_ref, w_ref, m_ref, v_ref, g_ref, d_ref, nm_ref, nv_ref):
        g = p_ref[0]
        for s in range(1, N_DEV):
            g = g + p_ref[s]
        g_ref[...] = g
        d_ref[...], nm_ref[...], nv_ref[...] = _adamw(w_ref[...], g, m_ref[...], v_ref[...])

    tile = pl.BlockSpec((row_tile, cols), lambda i: (i, 0))
    return pl.pallas_call(
        body, name=name, grid=(rows // row_tile,),
        in_specs=[pl.BlockSpec((N_DEV, row_tile, cols), lambda i: (0, i, 0)), tile, tile, tile],
        out_specs=[tile] * 4,
        out_shape=[jax.ShapeDtypeStruct((rows, cols), F32)] * 4,
        compiler_params=_params(dimension_semantics=("arbitrary",)),
    )(parts, w, m, v)


_SMALL = (("norm_g", 1024), ("ret_gn_g", 512), ("rwkv_mu", 2176), ("w0", 512), ("a0", 512), ("k_k", 512), ("k_a", 512),
          ("r_k", 512), ("rwkv_gn_g", 512), ("rwkv_gn_b", 512), ("final_norm_g", 1024))
_SMALL_ROWS = 72
_WEIGHTS = ("norm_g", "w_in", "ret_gn_g", "rwkv_mu", "w_lora_up", "w0", "a_lora_up", "a0", "k_k", "k_a", "r_k", "rwkv_gn_g",
            "rwkv_gn_b", "w_out", "final_norm_g")


def _pack_small(d, extra_row=None):
    rows = [d[n].reshape(-1, 128) for n, _ in _SMALL]
    used = sum(r.shape[0] for r in rows)
    rows.append(jnp.zeros((1, 128), F32) if extra_row is None else extra_row)
    rows.append(jnp.zeros((_SMALL_ROWS - used - 1, 128), F32))
    return jnp.concatenate(rows, axis=0)


def _unpack_small(packed, shapes):
    out, row = {}, 0
    for n, size in _SMALL:
        out[n] = packed[row:row + size // 128].reshape(shapes[n])
        row += size // 128
    return out, packed[row, 0]


def kernel(x, norm_g, w_in, ret_gn_g, rwkv_mu, w_lora_up, w0, a_lora_up, a0, k_k, k_a, r_k, rwkv_gn_g, rwkv_gn_b, w_out, final_norm_g, loss_target, m_norm_g, m_w_in, m_ret_gn_g, m_rwkv_mu, m_w_lora_up, m_w0, m_a_lora_up, m_a0, m_k_k, m_k_a, m_r_k, m_rwkv_gn_g, m_rwkv_gn_b, m_w_out, m_final_norm_g, v_norm_g, v_w_in, v_ret_gn_g, v_rwkv_mu, v_w_lora_up, v_w0, v_a_lora_up, v_a0, v_k_k, v_k_a, v_r_k, v_rwkv_gn_g, v_rwkv_gn_b, v_w_out, v_final_norm_g):
    wts = dict(norm_g=norm_g, w_in=w_in, ret_gn_g=ret_gn_g, rwkv_mu=rwkv_mu, w_lora_up=w_lora_up, w0=w0, a_lora_up=a_lora_up,
               a0=a0, k_k=k_k, k_a=k_a, r_k=r_k, rwkv_gn_g=rwkv_gn_g, rwkv_gn_b=rwkv_gn_b, w_out=w_out,
               final_norm_g=final_norm_g)
    mom = dict(norm_g=m_norm_g, w_in=m_w_in, ret_gn_g=m_ret_gn_g, rwkv_mu=m_rwkv_mu, w_lora_up=m_w_lora_up, w0=m_w0,
               a_lora_up=m_a_lora_up, a0=m_a0, k_k=m_k_k, k_a=m_k_a, r_k=m_r_k, rwkv_gn_g=m_rwkv_gn_g,
               rwkv_gn_b=m_rwkv_gn_b, w_out=m_w_out, final_norm_g=m_final_norm_g)
    var = dict(norm_g=v_norm_g, w_in=v_w_in, ret_gn_g=v_ret_gn_g, rwkv_mu=v_rwkv_mu, w_lora_up=v_w_lora_up, w0=v_w0,
               a_lora_up=v_a_lora_up, a0=v_a0, k_k=v_k_k, k_a=v_k_a, r_k=v_r_k, rwkv_gn_g=v_rwkv_gn_g,
               rwkv_gn_b=v_rwkv_gn_b, w_out=v_w_out, final_norm_g=v_final_norm_g)
    shapes = {n: wts[n].shape for n in _WEIGHTS}

    g_in, g_out, g_wup, g_aup = _all_gather(
        [w_in[0].astype(BF16), w_out[0].astype(BF16), w_lora_up[0], a_lora_up[0]])
    w_in_bf = g_in.reshape(N_DEV, D_MODEL, SHARD_IN).transpose(1, 0, 2).reshape(D_MODEL, IN_COLS)
    wup_full = g_wup.reshape(N_DEV, LORA, SHARD_LORA).transpose(1, 0, 2).reshape(LORA, W)
    aup_full = g_aup.reshape(N_DEV, LORA, SHARD_LORA).transpose(1, 0, 2).reshape(LORA, W)

    loss, dx, g = _local_grads(
        x[0], loss_target[0], norm_g, w_in_bf, ret_gn_g, rwkv_mu, wup_full, w0, aup_full, a0, k_k, k_a,
        r_k.reshape(1, W), rwkv_gn_g, rwkv_gn_b, g_out, final_norm_g.reshape(1, D_MODEL))

    small = _pack_small(g, loss[:, :128])
    parts = _exchange([
        g["w_in"].reshape(D_MODEL, N_DEV, SHARD_IN).transpose(1, 0, 2),
        g["w_out"].reshape(N_DEV, SHARD_OUT, D_MODEL),
        g["w_lora_up"].reshape(LORA, N_DEV, SHARD_LORA).transpose(1, 0, 2),
        g["a_lora_up"].reshape(LORA, N_DEV, SHARD_LORA).transpose(1, 0, 2),
        jnp.broadcast_to(small[None], (N_DEV, _SMALL_ROWS, 128))])
    res = {}
    res["w_in"] = _reduce_adamw("adamw_w_in", parts[0], w_in[0], m_w_in[0], v_w_in[0], 256)
    res["w_out"] = _reduce_adamw("adamw_w_out", parts[1], w_out[0], m_w_out[0], v_w_out[0], SHARD_OUT)
    res["w_lora_up"] = _reduce_adamw("adamw_w_lora_up", parts[2], w_lora_up[0], m_w_lora_up[0], v_w_lora_up[0], LORA)
    res["a_lora_up"] = _reduce_adamw("adamw_a_lora_up", parts[3], a_lora_up[0], m_a_lora_up[0], v_a_lora_up[0], LORA)
    sm = _reduce_adamw("adamw_vectors", parts[4], _pack_small(wts), _pack_small(mom), _pack_small(var), _SMALL_ROWS)
    unpacked = [_unpack_small(t, shapes) for t in sm]
    total_loss = unpacked[0][1]
    for n, _ in _SMALL:
        res[n] = [u[0][n] for u in unpacked]
    for n in ("w_in", "w_out", "w_lora_up", "a_lora_up"):
        res[n] = [t.reshape(shapes[n]) for t in res[n]]
    return (total_loss, dx[None], *[res[n][0] for n in _WEIGHTS], *[res[n][1] for n in _WEIGHTS],
            *[res[n][2] for n in _WEIGHTS], *[res[n][3] for n in _WEIGHTS])
```

```python
import functools

import numpy as np
import jax
import jax.numpy as jnp
from jax import lax
from jax.experimental import pallas as pl
from jax.experimental.pallas import tpu as pltpu

F32 = jnp.float32
BF16 = jnp.bfloat16
HI = lax.Precision.HIGHEST

D_MODEL = 1024
CHUNK = 64
RET_HEADS = 4
RET_DV = 128
RET_DK = 64
RET_QK = 256
RET_WIDTH = 512
RWKV_WIDTH = 512
RWKV_HEAD = 64
RWKV_HEADS = 8
LORA = 64
RET_COLS = 2 * RET_QK + 2 * RET_WIDTH
RWKV_COLS = 4 * RWKV_WIDTH + 2 * LORA
IN_COLS = RET_COLS + RWKV_COLS
ROPE_BASE = 10000.0
RMS_EPS = 1e-6
RET_GN_EPS = 1e-5
RWKV_GN_EPS = 64e-5
ADAM_LR = 0.001
ADAM_B1 = 0.9
ADAM_B2 = 0.999
ADAM_EPS = 1e-08
ADAM_WD = 0.01
ADAM_STEP = 10
N_DEV = 8
SHARD_IN = IN_COLS // N_DEV
SHARD_OUT = D_MODEL // N_DEV
SHARD_LORA = RWKV_WIDTH // N_DEV
VMEM_LIMIT = 56 * 1024 * 1024
TOK_TILE = 256
WKV_CHUNK = 32

MESH = pl.DeviceIdType.MESH


def _dot_hi(a, b):
    return jnp.dot(a, b, precision=HI, preferred_element_type=F32)


def _dot_nt_hi(a, b):
    return lax.dot_general(a, b, (((1,), (1,)), ((), ())), precision=HI, preferred_element_type=F32)


def _dot_tn_hi(a, b):
    return lax.dot_general(a, b, (((0,), (0,)), ((), ())), precision=HI, preferred_element_type=F32)


def _dot_bf(a, b):
    return jnp.dot(a.astype(BF16), b.astype(BF16), preferred_element_type=F32)


def _dot_nt_bf(a, b):
    return lax.dot_general(a.astype(BF16), b.astype(BF16), (((1,), (1,)), ((), ())), preferred_element_type=F32)


def _dot_tn_bf(a, b):
    return lax.dot_general(a.astype(BF16), b.astype(BF16), (((0,), (0,)), ((), ())), preferred_element_type=F32)


def _params(**kw):
    return pltpu.CompilerParams(vmem_limit_bytes=VMEM_LIMIT, **kw)


def _full(shape):
    nd = len(shape)
    return pl.BlockSpec(shape, lambda i, _nd=nd: (0,) * _nd)


def _rows(tile, width):
    return pl.BlockSpec((tile, width), lambda i: (i, 0))


def _block_ones(n, blk, scale=1.0):
    idx = np.arange(n) // blk
    return jnp.asarray((idx[:, None] == idx[None, :]).astype(np.float32) * scale)


def _rope_tables(T):
    half = RET_DK // 2
    expo = -np.arange(half, dtype=np.float32) / np.float32(half)
    freqs = np.exp(expo * np.float32(np.log(ROPE_BASE))).astype(np.float32)
    ang = np.arange(T, dtype=np.float32)[:, None] * freqs[None, :]
    cos, sin = np.cos(ang).astype(np.float32), np.sin(ang).astype(np.float32)
    cos_h = np.concatenate([cos, cos], axis=1)
    sin_h = np.concatenate([-sin, sin], axis=1)
    cos_t = np.tile(cos_h, (1, RET_HEADS))
    sin_t = np.tile(sin_h, (1, RET_HEADS))
    d = np.arange(RET_QK)
    src = (d // RET_DK) * RET_DK + ((d % RET_DK) + half) % RET_DK
    prot = np.zeros((RET_QK, RET_QK), np.float32)
    prot[src, d] = 1.0
    return jnp.asarray(cos_t), jnp.asarray(sin_t), jnp.asarray(prot)


def _ret_tables():
    h = np.arange(RET_HEADS, dtype=np.float32)
    lg = np.log(1.0 - np.exp2(-5.0 - h)).astype(np.float32)
    idx = np.arange(CHUNK, dtype=np.float32)
    intra = np.exp(lg[:, None, None] * np.abs(idx[:, None] - idx[None, :])).astype(np.float32)
    q_dec = np.exp(lg[:, None] * (idx[None, :] + 1.0)).astype(np.float32)
    k_dec = np.exp(lg[:, None] * (CHUNK - 1.0 - idx[None, :])).astype(np.float32)
    chunk_dec = np.exp(lg * CHUNK).astype(np.float32)
    lane_head = np.arange(RET_QK) // RET_DK
    mask = (lane_head[None, :] == np.arange(RET_HEADS)[:, None]).astype(np.float32)
    m = np.broadcast_to(mask[:, None, :], (RET_HEADS, CHUNK, RET_QK)).copy()
    qd = m * q_dec[:, :, None]
    kd = m * k_dec[:, :, None]
    return jnp.asarray(intra), jnp.asarray(m), jnp.asarray(qd), jnp.asarray(kd), [float(c) for c in chunk_dec]


def _rmsnorm(x, g):
    return x * lax.rsqrt(jnp.mean(x * x, axis=-1, keepdims=True) + RMS_EPS) * g


def _inproj(x, norm_g, w_ret, w_rwkv):
    T = x.shape[0]

    def body(x_ref, g_ref, wr_ref, ww_ref, pr_ref, pw_ref, u_ref):
        ub = _rmsnorm(x_ref[...], g_ref[...]).astype(BF16)
        u_ref[...] = ub
        pr_ref[...] = jnp.dot(ub, wr_ref[...], preferred_element_type=F32)
        pw_ref[...] = jnp.dot(ub, ww_ref[...], preferred_element_type=F32)

    return pl.pallas_call(
        body, name="inproj", grid=(T // TOK_TILE,),
        in_specs=[_rows(TOK_TILE, D_MODEL), _full((1, D_MODEL)), _full((D_MODEL, RET_COLS)), _full((D_MODEL, RWKV_COLS))],
        out_specs=[_rows(TOK_TILE, RET_COLS), _rows(TOK_TILE, RWKV_COLS), _rows(TOK_TILE, D_MODEL)],
        out_shape=[jax.ShapeDtypeStruct((T, RET_COLS), F32), jax.ShapeDtypeStruct((T, RWKV_COLS), F32),
                   jax.ShapeDtypeStruct((T, D_MODEL), BF16)],
        compiler_params=_params(dimension_semantics=("arbitrary",)),
    )(x, norm_g, w_ret, w_rwkv)


def _ret_chunk(pq, pk, v_heads, s_heads, cos_t, sin_t, prot, dec, hm, qd, kd, chunk_dec):
    q = pq * cos_t + _dot_hi(pq, prot) * sin_t
    k = (pk * cos_t + _dot_hi(pk, prot) * sin_t) * (RET_DK ** -0.5)
    outs, s_out = [], []
    for h in range(RET_HEADS):
        sc = _dot_nt_hi(q * hm[h], k * hm[h]) * dec[h]
        intra = _dot_hi(sc, v_heads[h])
        kv = _dot_tn_hi(k * kd[h], v_heads[h])
        inter = _dot_hi(q * qd[h], s_heads[h])
        outs.append(intra + inter)
        s_out.append(s_heads[h] * chunk_dec[h] + kv)
    return tuple(outs), tuple(s_out)


def _ret_specs():
    const = [_full((RET_QK, RET_QK)), _full((RET_HEADS, CHUNK, CHUNK)), _full((RET_HEADS, CHUNK, RET_QK)),
             _full((RET_HEADS, CHUNK, RET_QK)), _full((RET_HEADS, CHUNK, RET_QK))]
    return const


def _ret_fwd(p_ret, tabs):
    T = p_ret.shape[0]
    nc = T // CHUNK
    cos_t, sin_t, prot, dec, hm, qd, kd, chunk_dec = tabs

    def body(p_ref, cos_ref, sin_ref, prot_ref, dec_ref, hm_ref, qd_ref, kd_ref, out_ref, sin_save_ref, s_scr):
        @pl.when(pl.program_id(0) == 0)
        def _():
            s_scr[...] = jnp.zeros_like(s_scr)

        s_heads = tuple(s_scr[h] for h in range(RET_HEADS))
        for h in range(RET_HEADS):
            sin_save_ref[0, h] = s_heads[h]
        pq = p_ref[:, 0:RET_QK]
        pk = p_ref[:, RET_QK:2 * RET_QK]
        v_heads = tuple(p_ref[:, 2 * RET_QK + RET_DV * h:2 * RET_QK + RET_DV * (h + 1)] for h in range(RET_HEADS))
        outs, s_out = _ret_chunk(pq, pk, v_heads, s_heads, cos_ref[...], sin_ref[...], prot_ref[...], dec_ref[...],
                                 hm_ref[...], qd_ref[...], kd_ref[...], chunk_dec)
        for h in range(RET_HEADS):
            out_ref[:, RET_DV * h:RET_DV * (h + 1)] = outs[h]
            s_scr[h] = s_out[h]

    return pl.pallas_call(
        body, name="ret_fwd", grid=(nc,),
        in_specs=[pl.BlockSpec((CHUNK, RET_COLS), lambda i: (i, 0)), _rows(CHUNK, RET_QK), _rows(CHUNK, RET_QK)] + _ret_specs(),
        out_specs=[_rows(CHUNK, RET_WIDTH), pl.BlockSpec((1, RET_HEADS, RET_QK, RET_DV), lambda i: (i, 0, 0, 0))],
        out_shape=[jax.ShapeDtypeStruct((T, RET_WIDTH), F32), jax.ShapeDtypeStruct((nc, RET_HEADS, RET_QK, RET_DV), F32)],
        scratch_shapes=[pltpu.VMEM((RET_HEADS, RET_QK, RET_DV), F32)],
        compiler_params=_params(dimension_semantics=("arbitrary",)),
    )(p_ret, cos_t, sin_t, prot, dec, hm, qd, kd)


def _ret_bwd(p_ret, s_saved, d_ret, tabs):
    T = p_ret.shape[0]
    nc = T // CHUNK
    cos_t, sin_t, prot, dec, hm, qd, kd, chunk_dec = tabs

    def body(p_ref, s_ref, dret_ref, cos_ref, sin_ref, prot_ref, dec_ref, hm_ref, qd_ref, kd_ref, dp_ref, ds_scr):
        @pl.when(pl.program_id(0) == 0)
        def _():
            ds_scr[...] = jnp.zeros_like(ds_scr)

        pq = p_ref[:, 0:RET_QK]
        pk = p_ref[:, RET_QK:2 * RET_QK]
        v_heads = tuple(p_ref[:, 2 * RET_QK + RET_DV * h:2 * RET_QK + RET_DV * (h + 1)] for h in range(RET_HEADS))
        s_heads = tuple(s_ref[0, h] for h in range(RET_HEADS))
        consts = (cos_ref[...], sin_ref[...], prot_ref[...], dec_ref[...], hm_ref[...], qd_ref[...], kd_ref[...])
        _, vjp = jax.vjp(lambda a, b, c, d: _ret_chunk(a, b, c, d, *consts, chunk_dec), pq, pk, v_heads, s_heads)
        d_out = tuple(dret_ref[:, RET_DV * h:RET_DV * (h + 1)] for h in range(RET_HEADS))
        d_s = tuple(ds_scr[h] for h in range(RET_HEADS))
        dq, dk, dv, ds_in = vjp((d_out, d_s))
        dp_ref[:, 0:RET_QK] = dq
        dp_ref[:, RET_QK:2 * RET_QK] = dk
        for h in range(RET_HEADS):
            dp_ref[:, 2 * RET_QK + RET_DV * h:2 * RET_QK + RET_DV * (h + 1)] = dv[h]
            ds_scr[h] = ds_in[h]

    rev = lambda i: (nc - 1 - i, 0)
    return pl.pallas_call(
        body, name="ret_bwd", grid=(nc,),
        in_specs=[pl.BlockSpec((CHUNK, RET_COLS), rev),
                  pl.BlockSpec((1, RET_HEADS, RET_QK, RET_DV), lambda i: (nc - 1 - i, 0, 0, 0)),
                  pl.BlockSpec((CHUNK, RET_WIDTH), rev), pl.BlockSpec((CHUNK, RET_QK), rev), pl.BlockSpec((CHUNK, RET_QK), rev)]
        + _ret_specs(),
        out_specs=pl.BlockSpec((CHUNK, 2 * RET_QK + RET_WIDTH), rev),
        out_shape=jax.ShapeDtypeStruct((T, 2 * RET_QK + RET_WIDTH), F32),
        scratch_shapes=[pltpu.VMEM((RET_HEADS, RET_QK, RET_DV), F32)],
        compiler_params=_params(dimension_semantics=("arbitrary",)),
    )(p_ret, s_saved, d_ret, cos_t, sin_t, prot, dec, hm, qd, kd)


def _wkv_consts():
    lane = lax.broadcasted_iota(jnp.int32, (RWKV_HEAD, RWKV_WIDTH), 1)
    sub = lax.broadcasted_iota(jnp.int32, (RWKV_HEAD, RWKV_WIDTH), 0)
    diag = ((lane & (RWKV_HEAD - 1)) == sub).astype(F32)
    r = lax.broadcasted_iota(jnp.int32, (3 * 128, 128), 0)
    c = lax.broadcasted_iota(jnp.int32, (3 * 128, 128), 1)
    ones = (((r & 127) >> 6) == (c >> 6)).astype(BF16)
    return diag, ones


def _stack(x):
    return jnp.concatenate([x[:, 128 * p:128 * (p + 1)] for p in range(4)], axis=0)


def _unstack(y):
    return jnp.concatenate([y[RWKV_HEAD * p:RWKV_HEAD * (p + 1)] for p in range(4)], axis=1)


def _split(x, n):
    pieces = []
    for _ in range(n):
        p = x.astype(BF16)
        pieces.append(p)
        x = x - p.astype(F32)
    return pieces


def _lane_sum(x, ones, n):
    lhs = jnp.concatenate(_split(_stack(x), n), axis=1)
    return _unstack(jnp.dot(lhs, ones[:128 * n], preferred_element_type=F32))


def _expand(row, diag_bf, ones, n):
    lhs = jnp.concatenate([_stack(jnp.broadcast_to(p, diag_bf.shape) * diag_bf) for p in _split(row, n)], axis=1)
    return _unstack(jnp.dot(lhs, ones[:128 * n], preferred_element_type=F32))


def _colsum(x):
    return jnp.sum(x, axis=0, keepdims=True)


def _wkv_fwd(r, w, k, v, kap, a):
    T = r.shape[0]
    C = WKV_CHUNK
    nc = T // C

    def body(r_ref, w_ref, k_ref, v_ref, kap_ref, a_ref, o_ref, s_all_ref, s_scr):
        @pl.when(pl.program_id(0) == 0)
        def _():
            s_scr[...] = jnp.zeros_like(s_scr)

        diag, ones = _wkv_consts()
        diag_bf = diag.astype(BF16)

        def step8(i, S):
            base = pl.multiple_of(i * 8, 8)
            rb, wb, kb, vb = (ref[pl.ds(base, 8), :] for ref in (r_ref, w_ref, k_ref, v_ref))
            kapb, ab = kap_ref[pl.ds(base, 8), :], a_ref[pl.ds(base, 8), :]
            rows = []
            for u in range(8):
                sl = slice(u, u + 1)
                sa = _lane_sum(S * (-kapb[sl]), ones, 2)
                S = S * wb[sl] + sa * (kapb[sl] * ab[sl]) + _expand(vb[sl], diag_bf, ones, 3) * kb[sl]
                s_all_ref[base + u] = S
                rows.append(_colsum(_lane_sum(S * rb[sl], ones, 2) * diag))
            o_ref[pl.ds(base, 8), :] = jnp.concatenate(rows, axis=0)
            return S

        s_scr[...] = lax.fori_loop(0, C // 8, step8, s_scr[...])

    spec = _rows(C, RWKV_WIDTH)
    return pl.pallas_call(
        body, name="wkv_fwd", grid=(nc,),
        in_specs=[spec] * 6,
        out_specs=[spec, pl.BlockSpec((C, RWKV_HEAD, RWKV_WIDTH), lambda i: (i, 0, 0))],
        out_shape=[jax.ShapeDtypeStruct((T, RWKV_WIDTH), F32), jax.ShapeDtypeStruct((T, RWKV_HEAD, RWKV_WIDTH), F32)],
        scratch_shapes=[pltpu.VMEM((RWKV_HEAD, RWKV_WIDTH), F32)],
        compiler_params=_params(dimension_semantics=("arbitrary",)),
    )(r, w, k, v, kap, a)


def _wkv_bwd(r, w, k, v, kap, a, s_all, d_o):
    T = r.shape[0]
    C = WKV_CHUNK
    nc = T // C

    def body(r_ref, w_ref, k_ref, v_ref, kap_ref, a_ref, s_ref, s_before_ref, do_ref,
             dr_ref, dw_ref, dk_ref, dv_ref, dkap_ref, da_ref, ds_scr):
        first_chunk = pl.program_id(0) == nc - 1

        @pl.when(pl.program_id(0) == 0)
        def _():
            ds_scr[...] = jnp.zeros_like(ds_scr)

        diag, ones = _wkv_consts()
        diag_bf = diag.astype(BF16)
        s_before = jnp.where(first_chunk, 0.0, s_before_ref[0])

        def bwd8(i, dS):
            base = pl.multiple_of((C // 8 - 1 - i) * 8, 8)
            rb, wb, kb, vb = (ref[pl.ds(base, 8), :] for ref in (r_ref, w_ref, k_ref, v_ref))
            kapb, ab, dob = kap_ref[pl.ds(base, 8), :], a_ref[pl.ds(base, 8), :], do_ref[pl.ds(base, 8), :]
            rows = [None] * 8
            for u in reversed(range(8)):
                sl = slice(u, u + 1)
                s_new = s_ref[base + u]
                if u > 0:
                    s_prev = s_ref[base + u - 1]
                else:
                    s_prev = jnp.where(base == 0, s_before, s_ref[jnp.maximum(base - 1, 0)])
                b = kapb[sl] * ab[sl]
                sa = _lane_sum(s_prev * (-kapb[sl]), ones, 2)
                vfull = _expand(vb[sl], diag_bf, ones, 3)
                dofull = _expand(dob[sl], diag_bf, ones, 2)
                dSn = dS + dofull * rb[sl]
                dr = _colsum(s_new * dofull)
                dw = _colsum(dSn * s_prev)
                db = _colsum(dSn * sa)
                dk = _colsum(dSn * vfull)
                dsa = _lane_sum(dSn * b, ones, 2)
                dv = _colsum(_lane_sum(dSn * kb[sl], ones, 2) * diag)
                dkap = db * ab[sl] - _colsum(dsa * s_prev)
                da = db * kapb[sl]
                dS = dSn * wb[sl] - dsa * kapb[sl]
                rows[u] = (dr, dw, dk, dv, dkap, da)
            for j, ref in enumerate((dr_ref, dw_ref, dk_ref, dv_ref, dkap_ref, da_ref)):
                ref[pl.ds(base, 8), :] = jnp.concatenate([rows[u][j] for u in range(8)], axis=0)
            return dS

        ds_scr[...] = lax.fori_loop(0, C // 8, bwd8, ds_scr[...])

    spec = pl.BlockSpec((C, RWKV_WIDTH), lambda i: (nc - 1 - i, 0))
    states = pl.BlockSpec((C, RWKV_HEAD, RWKV_WIDTH), lambda i: (nc - 1 - i, 0, 0))
    before = pl.BlockSpec((1, RWKV_HEAD, RWKV_WIDTH), lambda i: (jnp.maximum((nc - 1 - i) * C - 1, 0), 0, 0))
    return pl.pallas_call(
        body, name="wkv_bwd", grid=(nc,),
        in_specs=[spec] * 6 + [states, before, spec],
        out_specs=[spec] * 6,
        out_shape=[jax.ShapeDtypeStruct((T, RWKV_WIDTH), F32)] * 6,
        scratch_shapes=[pltpu.VMEM((RWKV_HEAD, RWKV_WIDTH), F32)],
        compiler_params=_params(dimension_semantics=("arbitrary",)),
    )(r, w, k, v, kap, a, s_all, s_all, d_o)


W = RWKV_WIDTH


def _softplus(y):
    return jnp.maximum(y, 0.0) + jnp.log(1.0 + jnp.exp(-jnp.abs(y)))


def _prep_fn(kr, xwa, w0, a0, k_k, k_a, wup_pad, aup_pad, ones64):
    w_log = -_softplus(-(w0 + _dot_hi(jnp.tanh(xwa), wup_pad))) - 0.5
    decay = jnp.exp(-jnp.exp(w_log))
    a = jax.nn.sigmoid(a0 + _dot_hi(xwa, aup_pad))
    kk = kr * k_k
    kap = kk / jnp.maximum(jnp.sqrt(_dot_hi(kk * kk, ones64)), 1e-12)
    k = kr * (1.0 + (a - 1.0) * k_a)
    return decay, k, kap, a


def _shift_down(p, first_row):
    rows = lax.broadcasted_iota(jnp.int32, p.shape, 0)
    return jnp.where(rows == 0, first_row, pltpu.roll(p, 1, axis=0))


def _shift_up(z, last_row):
    n = z.shape[0]
    rows = lax.broadcasted_iota(jnp.int32, z.shape, 0)
    return jnp.where(rows == n - 1, last_row, pltpu.roll(z, n - 1, axis=0))


def _prev_block_spec():
    return pl.BlockSpec((8, RWKV_COLS), lambda i: (jnp.maximum(i * (TOK_TILE // 8) - 1, 0), 0))


def _mixed(p_ref, prev8_ref, mu_ref, first_tile):
    p = p_ref[...]
    first_row = jnp.where(first_tile, 0.0, prev8_ref[7:8, :])
    prev = _shift_down(p, first_row)
    return p, prev, p + mu_ref[...] * (prev - p)


def _prep_fwd(p_rwkv, mu, w0, a0, k_k, k_a, wup_pad, aup_pad, ones64):
    T = p_rwkv.shape[0]

    def body(p_ref, prev8_ref, mu_ref, w0_ref, a0_ref, kk_ref, ka_ref, wup_ref, aup_ref, ones_ref,
             r_ref, w_ref, k_ref, v_ref, kap_ref, a_ref, g_ref):
        _, _, ps = _mixed(p_ref, prev8_ref, mu_ref, pl.program_id(0) == 0)
        decay, k, kap, a = _prep_fn(ps[:, W:2 * W], ps[:, 4 * W:], w0_ref[...], a0_ref[...], kk_ref[...], ka_ref[...],
                                    wup_ref[...], aup_ref[...], ones_ref[...])
        r_ref[...] = ps[:, 0:W]
        w_ref[...] = decay
        k_ref[...] = k
        v_ref[...] = ps[:, 2 * W:3 * W]
        kap_ref[...] = kap
        a_ref[...] = a
        g_ref[...] = ps[:, 3 * W:4 * W]

    vec = _full((1, W))
    return pl.pallas_call(
        body, name="prep_fwd", grid=(T // TOK_TILE,),
        in_specs=[_rows(TOK_TILE, RWKV_COLS), _prev_block_spec(), _full((1, RWKV_COLS)), vec, vec, vec, vec,
                  _full((2 * LORA, W)), _full((2 * LORA, W)), _full((W, W))],
        out_specs=[_rows(TOK_TILE, W)] * 7,
        out_shape=[jax.ShapeDtypeStruct((T, W), F32)] * 7,
        compiler_params=_params(dimension_semantics=("arbitrary",)),
    )(p_rwkv, p_rwkv, mu, w0, a0, k_k, k_a, wup_pad, aup_pad, ones64)


def _prep_bwd(p_rwkv, mu, w0, a0, k_k, k_a, wup_pad, aup_pad, ones64, dr, dw, dk, dv, dkap, da, dg, dr2, dk2, dv2):
    T = p_rwkv.shape[0]
    nt = T // TOK_TILE

    def body(p_ref, prev8_ref, mu_ref, w0_ref, a0_ref, kk_ref, ka_ref, wup_ref, aup_ref, ones_ref,
             dr_ref, dw_ref, dk_ref, dv_ref, dkap_ref, da_ref, dg_ref, dr2_ref, dk2_ref, dv2_ref,
             dp_ref, dmu_ref, dw0_ref, da0_ref, dkk_ref, dka_ref, dwup_ref, daup_ref, zrow_scr):
        i = pl.program_id(0)
        accs = (dmu_ref, dw0_ref, da0_ref, dkk_ref, dka_ref, dwup_ref, daup_ref)

        @pl.when(i == 0)
        def _():
            zrow_scr[...] = jnp.zeros_like(zrow_scr)
            for ref in accs:
                ref[...] = jnp.zeros_like(ref)

        p, prev, ps = _mixed(p_ref, prev8_ref, mu_ref, i == nt - 1)
        ones = ones_ref[...]
        _, vjp = jax.vjp(lambda *args: _prep_fn(*args, ones), ps[:, W:2 * W], ps[:, 4 * W:], w0_ref[...], a0_ref[...],
                         kk_ref[...], ka_ref[...], wup_ref[...], aup_ref[...])
        dkr, dxwa, dw0, da0, dkk, dka, dwup, daup = vjp(
            (dw_ref[...], dk_ref[...] + dk2_ref[...], dkap_ref[...], da_ref[...]))
        dps = jnp.concatenate([dr_ref[...] + dr2_ref[...], dkr, dv_ref[...] + dv2_ref[...], dg_ref[...], dxwa], axis=1)
        z = dps * mu_ref[...]
        dp_ref[...] = dps - z + _shift_up(z, zrow_scr[0:1, :])
        zrow_scr[0:1, :] = z[0:1, :]
        for ref, val in zip(accs, (_colsum(dps * (prev - p)), dw0, da0, dkk, dka, dwup, daup)):
            ref[...] += val

    rev = lambda i: (nt - 1 - i, 0)
    vec = _full((1, W))
    lora = _full((2 * LORA, W))
    tile = pl.BlockSpec((TOK_TILE, W), rev)
    prev8 = pl.BlockSpec((8, RWKV_COLS), lambda i: (jnp.maximum((nt - 1 - i) * (TOK_TILE // 8) - 1, 0), 0))
    return pl.pallas_call(
        body, name="prep_bwd", grid=(nt,),
        in_specs=[pl.BlockSpec((TOK_TILE, RWKV_COLS), rev), prev8, _full((1, RWKV_COLS)), vec, vec, vec, vec, lora, lora,
                  _full((W, W))] + [tile] * 10,
        out_specs=[pl.BlockSpec((TOK_TILE, RWKV_COLS), rev), _full((1, RWKV_COLS)), vec, vec, vec, vec, lora, lora],
        out_shape=[jax.ShapeDtypeStruct((T, RWKV_COLS), F32), jax.ShapeDtypeStruct((1, RWKV_COLS), F32)]
        + [jax.ShapeDtypeStruct((1, W), F32)] * 4 + [jax.ShapeDtypeStruct((2 * LORA, W), F32)] * 2,
        scratch_shapes=[pltpu.VMEM((8, RWKV_COLS), F32)],
        compiler_params=_params(dimension_semantics=("arbitrary",)),
    )(p_rwkv, p_rwkv, mu, w0, a0, k_k, k_a, wup_pad, aup_pad, ones64, dr, dw, dk, dv, dkap, da, dg, dr2, dk2, dv2)


def _silu(x):
    return x * jax.nn.sigmoid(x)


def _post_y(o, r, k, v, g_rw, ret_raw, g_ret, ret_gn_g, gn_g, gn_b, r_k, avg128, avg64, ones64):
    xc = ret_raw - _dot_hi(ret_raw, avg128)
    ret = xc * lax.rsqrt(_dot_hi(xc * xc, avg128) + RET_GN_EPS)
    y_ret = _silu(g_ret) * (ret * ret_gn_g)
    oc = o - _dot_hi(o, avg64)
    on = oc * lax.rsqrt(_dot_hi(oc * oc, avg64) + RWKV_GN_EPS) * gn_g + gn_b
    bonus = _dot_hi(r * k * r_k, ones64) * v
    y_rwkv = _silu(g_rw) * (on + bonus)
    return y_ret, y_rwkv


def _post_loss(h, final_g, target):
    err = _rmsnorm(h, final_g) - target
    return 0.5 * jnp.sum(jnp.mean(err * err, axis=-1))


def _post(o, r, k, v, g_rw, ret_raw, p_ret, x, target, ret_gn_g, gn_g, gn_b, r_k, final_g, w_out, avg128, avg64, ones64):
    T = x.shape[0]
    n_tok_out = 8

    def body(o_ref, r_ref, k_ref, v_ref, grw_ref, ret_ref, gret_ref, x_ref, tgt_ref, rg_ref, gg_ref, gb_ref, rk_ref, fg_ref,
             wo_ref, a128_ref, a64_ref, ones_ref, *outs):
        tok_outs, (dwo_ref, drg_ref, dgg_ref, dgb_ref, drk_ref, dfg_ref, loss_ref) = outs[:n_tok_out], outs[n_tok_out:]
        accs = (dwo_ref, drg_ref, dgg_ref, dgb_ref, drk_ref, dfg_ref, loss_ref)

        @pl.when(pl.program_id(0) == 0)
        def _():
            for ref in accs:
                ref[...] = jnp.zeros_like(ref)

        consts = (a128_ref[...], a64_ref[...], ones_ref[...])
        (y_ret, y_rwkv), vjp = jax.vjp(
            lambda *args: _post_y(*args, *consts), o_ref[...], r_ref[...], k_ref[...], v_ref[...], grw_ref[...], ret_ref[...],
            gret_ref[...], rg_ref[...], gg_ref[...], gb_ref[...], rk_ref[...])
        h = x_ref[...] + _dot_bf(y_ret, wo_ref[0:RET_WIDTH, :]) + _dot_bf(y_rwkv, wo_ref[RET_WIDTH:, :])
        loss, (dh, dfg) = jax.value_and_grad(_post_loss, argnums=(0, 1))(h, fg_ref[...], tgt_ref[...])
        dy_ret = _dot_nt_bf(dh, wo_ref[0:RET_WIDTH, :])
        dy_rwkv = _dot_nt_bf(dh, wo_ref[RET_WIDTH:, :])
        do, dr, dk, dv, dgrw, dret, dgret, drg, dgg, dgb, drk = vjp((dy_ret, dy_rwkv))
        for ref, val in zip(tok_outs, (dh, do, dr, dk, dv, dgrw, dret, dgret)):
            ref[...] = val
        dwo_ref[0:RET_WIDTH, :] += _dot_tn_bf(y_ret, dh)
        dwo_ref[RET_WIDTH:, :] += _dot_tn_bf(y_rwkv, dh)
        for ref, val in zip(accs[1:], (drg, dgg, dgb, drk, dfg, jnp.full((1, 128), loss, F32))):
            ref[...] += val

    tile = _rows(TOK_TILE, W)
    wide = _rows(TOK_TILE, D_MODEL)
    vec = _full((1, W))
    sq = _full((W, W))
    return pl.pallas_call(
        body, name="post", grid=(T // TOK_TILE,),
        in_specs=[tile] * 6 + [pl.BlockSpec((TOK_TILE, W), lambda i: (i, 2)), wide, wide, vec, vec, vec, vec,
                               _full((1, D_MODEL)), _full((D_MODEL, D_MODEL)), sq, sq, sq],
        out_specs=[wide] + [tile] * 7 + [_full((D_MODEL, D_MODEL)), vec, vec, vec, vec, _full((1, D_MODEL)), _full((1, 128))],
        out_shape=[jax.ShapeDtypeStruct((T, D_MODEL), F32)] + [jax.ShapeDtypeStruct((T, W), F32)] * 7
        + [jax.ShapeDtypeStruct((D_MODEL, D_MODEL), F32)] + [jax.ShapeDtypeStruct((1, W), F32)] * 4
        + [jax.ShapeDtypeStruct((1, D_MODEL), F32), jax.ShapeDtypeStruct((1, 128), F32)],
        compiler_params=_params(dimension_semantics=("arbitrary",)),
    )(o, r, k, v, g_rw, ret_raw, p_ret, x, target, ret_gn_g, gn_g, gn_b, r_k, final_g, w_out, avg128, avg64, ones64)


def _inproj_bwd_x(x, norm_g, dp_qkv, dg_ret, dp_rwkv, dh, w_ret, w_rwkv):
    T = x.shape[0]
    n_qkv = 2 * RET_QK + RET_WIDTH

    def body(x_ref, g_ref, dqkv_ref, dgret_ref, drwkv_ref, dh_ref, wr_ref, ww_ref, dx_ref, dg_ref):
        @pl.when(pl.program_id(0) == 0)
        def _():
            dg_ref[...] = jnp.zeros_like(dg_ref)

        _, vjp = jax.vjp(_rmsnorm, x_ref[...], g_ref[...])
        du = (_dot_nt_bf(dqkv_ref[...], wr_ref[:, 0:n_qkv]) + _dot_nt_bf(dgret_ref[...], wr_ref[:, n_qkv:])
              + _dot_nt_bf(drwkv_ref[...], ww_ref[...]))
        dx, dg = vjp(du)
        dx_ref[...] = dx + dh_ref[...]
        dg_ref[...] += dg

    return pl.pallas_call(
        body, name="inproj_bwd_x", grid=(T // TOK_TILE,),
        in_specs=[_rows(TOK_TILE, D_MODEL), _full((1, D_MODEL)), _rows(TOK_TILE, n_qkv), _rows(TOK_TILE, RET_WIDTH),
                  _rows(TOK_TILE, RWKV_COLS), _rows(TOK_TILE, D_MODEL), _full((D_MODEL, RET_COLS)), _full((D_MODEL, RWKV_COLS))],
        out_specs=[_rows(TOK_TILE, D_MODEL), _full((1, D_MODEL))],
        out_shape=[jax.ShapeDtypeStruct((T, D_MODEL), F32), jax.ShapeDtypeStruct((1, D_MODEL), F32)],
        compiler_params=_params(dimension_semantics=("arbitrary",)),
    )(x, norm_g, dp_qkv, dg_ret, dp_rwkv, dh, w_ret, w_rwkv)


def _grad_w(name, u, dp):
    T, n = dp.shape
    tile = 2 * TOK_TILE

    def body(u_ref, dp_ref, out_ref):
        @pl.when(pl.program_id(0) == 0)
        def _():
            out_ref[...] = jnp.zeros_like(out_ref)

        out_ref[...] += _dot_tn_bf(u_ref[...], dp_ref[...])

    return pl.pallas_call(
        body, name=name, grid=(T // tile,),
        in_specs=[_rows(tile, D_MODEL), _rows(tile, n)],
        out_specs=_full((D_MODEL, n)),
        out_shape=jax.ShapeDtypeStruct((D_MODEL, n), F32),
        compiler_params=_params(dimension_semantics=("arbitrary",)),
    )(u, dp)


def _pad_lora(w_up, top):
    z = jnp.zeros_like(w_up)
    return jnp.concatenate([w_up, z] if top else [z, w_up], axis=0)


def _local_grads(x, target, norm_g, w_in_bf, ret_gn_g, mu, w_lora_up, w0, a_lora_up, a0, k_k, k_a, r_k, gn_g, gn_b,
                 w_out_bf, final_g):
    T = x.shape[0]
    tabs = _rope_tables(T) + _ret_tables()
    ones64 = _block_ones(W, RWKV_HEAD)
    avg64 = _block_ones(W, RWKV_HEAD, 1.0 / RWKV_HEAD)
    avg128 = _block_ones(RET_WIDTH, RET_DV, 1.0 / RET_DV)
    w_ret, w_rwkv = w_in_bf[:, :RET_COLS], w_in_bf[:, RET_COLS:]
    wup_pad, aup_pad = _pad_lora(w_lora_up, True), _pad_lora(a_lora_up, False)

    p_ret, p_rwkv, u = _inproj(x, norm_g, w_ret, w_rwkv)
    ret_raw, s_saved = _ret_fwd(p_ret, tabs)
    r, w, k, v, kap, a, g_rw = _prep_fwd(p_rwkv, mu, w0, a0, k_k, k_a, wup_pad, aup_pad, ones64)
    o, s_all = _wkv_fwd(r, w, k, v, kap, a)
    (dh, do, dr2, dk2, dv2, dgrw, dret, dgret, d_w_out, d_ret_gn_g, d_gn_g, d_gn_b, d_r_k, d_final_g, loss) = _post(
        o, r, k, v, g_rw, ret_raw, p_ret, x, target, ret_gn_g, gn_g, gn_b, r_k, final_g, w_out_bf, avg128, avg64, ones64)
    dr, dw, dk, dv, dkap, da = _wkv_bwd(r, w, k, v, kap, a, s_all, do)
    dp_rwkv, d_mu, d_w0, d_a0, d_k_k, d_k_a, d_wup, d_aup = _prep_bwd(
        p_rwkv, mu, w0, a0, k_k, k_a, wup_pad, aup_pad, ones64, dr, dw, dk, dv, dkap, da, dgrw, dr2, dk2, dv2)
    dp_qkv = _ret_bwd(p_ret, s_saved, dret, tabs)
    dx, d_norm_g = _inproj_bwd_x(x, norm_g, dp_qkv, dgret, dp_rwkv, dh, w_ret, w_rwkv)
    d_w_in = jnp.concatenate([_grad_w("grad_w_qkv", u, dp_qkv), _grad_w("grad_w_gret", u, dgret),
                              _grad_w("grad_w_rwkv", u, dp_rwkv)], axis=1)
    grads = dict(norm_g=d_norm_g, w_in=d_w_in, ret_gn_g=d_ret_gn_g, rwkv_mu=d_mu, w_lora_up=d_wup[:LORA], w0=d_w0,
                 a_lora_up=d_aup[LORA:], a0=d_a0, k_k=d_k_k, k_a=d_k_a, r_k=d_r_k, rwkv_gn_g=d_gn_g, rwkv_gn_b=d_gn_b,
                 w_out=d_w_out, final_norm_g=d_final_g)
    return loss, dx, grads


def _mesh_pos():
    return lax.axis_index("x"), lax.axis_index("y"), lax.axis_index("c")


def _all_gather(shards):
    n = len(shards)

    def body(*refs):
        x_refs, out_refs = refs[:n], refs[n:2 * n]
        send_sems, recv_sems, local_sems = refs[2 * n:]
        x, y, c = _mesh_pos()
        me, sibling = (x, y, c), (x, y, 1 - c)
        chips = [(1 - x, y), (x, 1 - y), (1 - x, 1 - y)]

        def rows(a, pos):
            m = x_refs[a].shape[0]
            return out_refs[a].at[pl.ds((4 * pos[0] + 2 * pos[1] + pos[2]) * m, m), :]

        def copy(a, k, block, to, src=None):
            return pltpu.make_async_remote_copy(
                src_ref=rows(a, block) if src is None else src, dst_ref=rows(a, block),
                send_sem=send_sems.at[a, k], recv_sem=recv_sems.at[a, k], device_id=to, device_id_type=MESH)

        mine = [pltpu.make_async_copy(x_refs[a], rows(a, me), local_sems.at[a]) for a in range(n)]
        for cp in mine:
            cp.start()
        first = []
        for a in range(n):
            first.append(copy(a, 0, me, sibling, src=x_refs[a]))
            first += [copy(a, 1 + j, me, (*chip, c), src=x_refs[a]) for j, chip in enumerate(chips)]
        for cp in first:
            cp.start()
        passed = []
        for j, chip in enumerate(chips):
            for a in range(n):
                copy(a, 1 + j, (*chip, c), me).wait_recv()
                passed.append(copy(a, 4 + j, (*chip, c), sibling))
                passed[-1].start()
        for a in range(n):
            copy(a, 0, sibling, me).wait_recv()
            for j, chip in enumerate(chips):
                copy(a, 4 + j, (*chip, 1 - c), me).wait_recv()
        for cp in first + passed:
            cp.wait_send()
        for cp in mine:
            cp.wait()

    vmem = pl.BlockSpec(memory_space=pltpu.VMEM)
    return pl.pallas_call(
        body, name="gather_weights",
        out_shape=[jax.ShapeDtypeStruct((N_DEV * s.shape[0], s.shape[1]), s.dtype) for s in shards],
        in_specs=[vmem] * n, out_specs=[vmem] * n,
        scratch_shapes=[pltpu.SemaphoreType.DMA((n, 7)), pltpu.SemaphoreType.DMA((n, 7)), pltpu.SemaphoreType.DMA((n,))],
        compiler_params=_params(),
    )(*shards)


def _exchange(parts):
    n = len(parts)

    def body(*refs):
        in_refs, out_refs = refs[:n], refs[n:2 * n]
        send_sems, recv_sems, local_sems = refs[2 * n:]
        x, y, c = _mesh_pos()
        me = 4 * x + 2 * y + c
        own = [pltpu.make_async_copy(in_refs[a].at[me], out_refs[a].at[me], local_sems.at[a]) for a in range(n)]
        for cp in own:
            cp.start()
        copies = []
        for k in range(1, N_DEV):
            peer = (x ^ (k >> 2), y ^ ((k >> 1) & 1), c ^ (k & 1))
            peer_idx = 4 * peer[0] + 2 * peer[1] + peer[2]
            for a in range(n):
                copies.append(pltpu.make_async_remote_copy(
                    src_ref=in_refs[a].at[peer_idx], dst_ref=out_refs[a].at[me],
                    send_sem=send_sems.at[a, k - 1], recv_sem=recv_sems.at[a, k - 1], device_id=peer, device_id_type=MESH))
        for cp in copies:
            cp.start()
        for cp in copies:
            cp.wait()
        for cp in own:
            cp.wait()

    hbm = pl.BlockSpec(memory_space=pl.ANY)
    return pl.pallas_call(
        body, name="exchange_grads",
        out_shape=[jax.ShapeDtypeStruct(p.shape, p.dtype) for p in parts],
        in_specs=[hbm] * n, out_specs=[hbm] * n,
        scratch_shapes=[pltpu.SemaphoreType.DMA((n, 7)), pltpu.SemaphoreType.DMA((n, 7)), pltpu.SemaphoreType.DMA((n,))],
        compiler_params=_params(),
    )(*parts)


def _adamw(w, g, m, v):
    m = ADAM_B1 * m + (1.0 - ADAM_B1) * g
    v = ADAM_B2 * v + (1.0 - ADAM_B2) * (g * g)
    m_hat = m / (1.0 - ADAM_B1 ** ADAM_STEP)
    v_hat = v / (1.0 - ADAM_B2 ** ADAM_STEP)
    return -ADAM_LR * (m_hat / (jnp.sqrt(v_hat) + ADAM_EPS) + ADAM_WD * w), m, v


def _reduce_adamw(name, parts, w, m, v, row_tile):
    _, rows, cols = parts.shape

    def body(p_ref, w_ref, m_ref, v_ref, g_ref, d_ref, nm_ref, nv_ref):
        g = p_ref[0]
        for s in range(1, N_DEV):
            g = g + p_ref[s]
        g_ref[...] = g
        d_ref[...], nm_ref[...], nv_ref[...] = _adamw(w_ref[...], g, m_ref[...], v_ref[...])

    tile = pl.BlockSpec((row_tile, cols), lambda i: (i, 0))
    return pl.pallas_call(
        body, name=name, grid=(rows // row_tile,),
        in_specs=[pl.BlockSpec((N_DEV, row_tile, cols), lambda i: (0, i, 0)), tile, tile, tile],
        out_specs=[tile] * 4,
        out_shape=[jax.ShapeDtypeStruct((rows, cols), F32)] * 4,
        compiler_params=_params(dimension_semantics=("arbitrary",)),
    )(parts, w, m, v)


_SMALL = (("norm_g", 1024), ("ret_gn_g", 512), ("rwkv_mu", 2176), ("w0", 512), ("a0", 512), ("k_k", 512), ("k_a", 512),
          ("r_k", 512), ("rwkv_gn_g", 512), ("rwkv_gn_b", 512), ("final_norm_g", 1024))
_SMALL_ROWS = 72
_WEIGHTS = ("norm_g", "w_in", "ret_gn_g", "rwkv_mu", "w_lora_up", "w0", "a_lora_up", "a0", "k_k", "k_a", "r_k", "rwkv_gn_g",
            "rwkv_gn_b", "w_out", "final_norm_g")


def _pack_small(d, extra_row=None):
    rows = [d[n].reshape(-1, 128) for n, _ in _SMALL]
    used = sum(r.shape[0] for r in rows)
    rows.append(jnp.zeros((1, 128), F32) if extra_row is None else extra_row)
    rows.append(jnp.zeros((_SMALL_ROWS - used - 1, 128), F32))
    return jnp.concatenate(rows, axis=0)


def _unpack_small(packed, shapes):
    out, row = {}, 0
    for n, size in _SMALL:
        out[n] = packed[row:row + size // 128].reshape(shapes[n])
        row += size // 128
    return out, packed[row, 0]


def kernel(x, norm_g, w_in, ret_gn_g, rwkv_mu, w_lora_up, w0, a_lora_up, a0, k_k, k_a, r_k, rwkv_gn_g, rwkv_gn_b, w_out, final_norm_g, loss_target, m_norm_g, m_w_in, m_ret_gn_g, m_rwkv_mu, m_w_lora_up, m_w0, m_a_lora_up, m_a0, m_k_k, m_k_a, m_r_k, m_rwkv_gn_g, m_rwkv_gn_b, m_w_out, m_final_norm_g, v_norm_g, v_w_in, v_ret_gn_g, v_rwkv_mu, v_w_lora_up, v_w0, v_a_lora_up, v_a0, v_k_k, v_k_a, v_r_k, v_rwkv_gn_g, v_rwkv_gn_b, v_w_out, v_final_norm_g):
    wts = dict(norm_g=norm_g, w_in=w_in, ret_gn_g=ret_gn_g, rwkv_mu=rwkv_mu, w_lora_up=w_lora_up, w0=w0, a_lora_up=a_lora_up,
               a0=a0, k_k=k_k, k_a=k_a, r_k=r_k, rwkv_gn_g=rwkv_gn_g, rwkv_gn_b=rwkv_gn_b, w_out=w_out,
               final_norm_g=final_norm_g)
    mom = dict(norm_g=m_norm_g, w_in=m_w_in, ret_gn_g=m_ret_gn_g, rwkv_mu=m_rwkv_mu, w_lora_up=m_w_lora_up, w0=m_w0,
               a_lora_up=m_a_lora_up, a0=m_a0, k_k=m_k_k, k_a=m_k_a, r_k=m_r_k, rwkv_gn_g=m_rwkv_gn_g,
               rwkv_gn_b=m_rwkv_gn_b, w_out=m_w_out, final_norm_g=m_final_norm_g)
    var = dict(norm_g=v_norm_g, w_in=v_w_in, ret_gn_g=v_ret_gn_g, rwkv_mu=v_rwkv_mu, w_lora_up=v_w_lora_up, w0=v_w0,
               a_lora_up=v_a_lora_up, a0=v_a0, k_k=v_k_k, k_a=v_k_a, r_k=v_r_k, rwkv_gn_g=v_rwkv_gn_g,
               rwkv_gn_b=v_rwkv_gn_b, w_out=v_w_out, final_norm_g=v_final_norm_g)
    shapes = {n: wts[n].shape for n in _WEIGHTS}

    g_in, g_out, g_wup, g_aup = _all_gather(
        [w_in[0].astype(BF16), w_out[0].astype(BF16), w_lora_up[0], a_lora_up[0]])
    w_in_bf = g_in.reshape(N_DEV, D_MODEL, SHARD_IN).transpose(1, 0, 2).reshape(D_MODEL, IN_COLS)
    wup_full = g_wup.reshape(N_DEV, LORA, SHARD_LORA).transpose(1, 0, 2).reshape(LORA, W)
    aup_full = g_aup.reshape(N_DEV, LORA, SHARD_LORA).transpose(1, 0, 2).reshape(LORA, W)

    loss, dx, g = _local_grads(
        x[0], loss_target[0], norm_g, w_in_bf, ret_gn_g, rwkv_mu, wup_full, w0, aup_full, a0, k_k, k_a,
        r_k.reshape(1, W), rwkv_gn_g, rwkv_gn_b, g_out, final_norm_g.reshape(1, D_MODEL))

    small = _pack_small(g, loss[:, :128])
    parts = _exchange([
        g["w_in"].reshape(D_MODEL, N_DEV, SHARD_IN).transpose(1, 0, 2),
        g["w_out"].reshape(N_DEV, SHARD_OUT, D_MODEL),
        g["w_lora_up"].reshape(LORA, N_DEV, SHARD_LORA).transpose(1, 0, 2),
        g["a_lora_up"].reshape(LORA, N_DEV, SHARD_LORA).transpose(1, 0, 2),
        jnp.broadcast_to(small[None], (N_DEV, _SMALL_ROWS, 128))])
    res = {}
    res["w_in"] = _reduce_adamw("adamw_w_in", parts[0], w_in[0], m_w_in[0], v_w_in[0], 256)
    res["w_out"] = _reduce_adamw("adamw_w_out", parts[1], w_out[0], m_w_out[0], v_w_out[0], SHARD_OUT)
    res["w_lora_up"] = _reduce_adamw("adamw_w_lora_up", parts[2], w_lora_up[0], m_w_lora_up[0], v_w_lora_up[0], LORA)
    res["a_lora_up"] = _reduce_adamw("adamw_a_lora_up", parts[3], a_lora_up[0], m_a_lora_up[0], v_a_lora_up[0], LORA)
    sm = _reduce_adamw("adamw_vectors", parts[4], _pack_small(wts), _pack_small(mom), _pack_small(var), _SMALL_ROWS)
    unpacked = [_unpack_small(t, shapes) for t in sm]
    total_loss = unpacked[0][1]
    for n, _ in _SMALL:
        res[n] = [u[0][n] for u in unpacked]
    for n in ("w_in", "w_out", "w_lora_up", "a_lora_up"):
        res[n] = [t.reshape(shapes[n]) for t in res[n]]
    return (total_loss, dx[None], *[res[n][0] for n in _WEIGHTS], *[res[n][1] for n in _WEIGHTS],
            *[res[n][2] for n in _WEIGHTS], *[res[n][3] for n in _WEIGHTS])
```

```python
import functools

import numpy as np
import jax
import jax.numpy as jnp
from jax import lax
from jax.experimental import pallas as pl
from jax.experimental.pallas import tpu as pltpu

F32 = jnp.float32
BF16 = jnp.bfloat16
HI = lax.Precision.HIGHEST

D_MODEL = 1024
CHUNK = 64
RET_HEADS = 4
RET_DV = 128
RET_DK = 64
RET_QK = 256
RET_WIDTH = 512
RWKV_WIDTH = 512
RWKV_HEAD = 64
RWKV_HEADS = 8
LORA = 64
RET_COLS = 2 * RET_QK + 2 * RET_WIDTH
RWKV_COLS = 4 * RWKV_WIDTH + 2 * LORA
IN_COLS = RET_COLS + RWKV_COLS
ROPE_BASE = 10000.0
RMS_EPS = 1e-6
RET_GN_EPS = 1e-5
RWKV_GN_EPS = 64e-5
ADAM_LR = 0.001
ADAM_B1 = 0.9
ADAM_B2 = 0.999
ADAM_EPS = 1e-08
ADAM_WD = 0.01
ADAM_STEP = 10
N_DEV = 8
SHARD_IN = IN_COLS // N_DEV
SHARD_OUT = D_MODEL // N_DEV
SHARD_LORA = RWKV_WIDTH // N_DEV
VMEM_LIMIT = 56 * 1024 * 1024
TOK_TILE = 256
WKV_CHUNK = 32

MESH = pl.DeviceIdType.MESH


def _dot_hi(a, b):
    return jnp.dot(a, b, precision=HI, preferred_element_type=F32)


def _dot_nt_hi(a, b):
    return lax.dot_general(a, b, (((1,), (1,)), ((), ())), precision=HI, preferred_element_type=F32)


def _dot_tn_hi(a, b):
    return lax.dot_general(a, b, (((0,), (0,)), ((), ())), precision=HI, preferred_element_type=F32)


def _dot_bf(a, b):
    return jnp.dot(a.astype(BF16), b.astype(BF16), preferred_element_type=F32)


def _dot_nt_bf(a, b):
    return lax.dot_general(a.astype(BF16), b.astype(BF16), (((1,), (1,)), ((), ())), preferred_element_type=F32)


def _dot_tn_bf(a, b):
    return lax.dot_general(a.astype(BF16), b.astype(BF16), (((0,), (0,)), ((), ())), preferred_element_type=F32)


def _params(**kw):
    return pltpu.CompilerParams(vmem_limit_bytes=VMEM_LIMIT, **kw)


def _full(shape):
    nd = len(shape)
    return pl.BlockSpec(shape, lambda i, _nd=nd: (0,) * _nd)


def _rows(tile, width):
    return pl.BlockSpec((tile, width), lambda i: (i, 0))


def _block_ones(n, blk, scale=1.0):
    idx = np.arange(n) // blk
    return jnp.asarray((idx[:, None] == idx[None, :]).astype(np.float32) * scale)


def _rope_tables(T):
    half = RET_DK // 2
    expo = -np.arange(half, dtype=np.float32) / np.float32(half)
    freqs = np.exp(expo * np.float32(np.log(ROPE_BASE))).astype(np.float32)
    ang = np.arange(T, dtype=np.float32)[:, None] * freqs[None, :]
    cos, sin = np.cos(ang).astype(np.float32), np.sin(ang).astype(np.float32)
    cos_h = np.concatenate([cos, cos], axis=1)
    sin_h = np.concatenate([-sin, sin], axis=1)
    cos_t = np.tile(cos_h, (1, RET_HEADS))
    sin_t = np.tile(sin_h, (1, RET_HEADS))
    d = np.arange(RET_QK)
    src = (d // RET_DK) * RET_DK + ((d % RET_DK) + half) % RET_DK
    prot = np.zeros((RET_QK, RET_QK), np.float32)
    prot[src, d] = 1.0
    return jnp.asarray(cos_t), jnp.asarray(sin_t), jnp.asarray(prot)


def _ret_tables():
    h = np.arange(RET_HEADS, dtype=np.float32)
    lg = np.log(1.0 - np.exp2(-5.0 - h)).astype(np.float32)
    idx = np.arange(CHUNK, dtype=np.float32)
    intra = np.exp(lg[:, None, None] * np.abs(idx[:, None] - idx[None, :])).astype(np.float32)
    q_dec = np.exp(lg[:, None] * (idx[None, :] + 1.0)).astype(np.float32)
    k_dec = np.exp(lg[:, None] * (CHUNK - 1.0 - idx[None, :])).astype(np.float32)
    chunk_dec = np.exp(lg * CHUNK).astype(np.float32)
    lane_head = np.arange(RET_QK) // RET_DK
    mask = (lane_head[None, :] == np.arange(RET_HEADS)[:, None]).astype(np.float32)
    m = np.broadcast_to(mask[:, None, :], (RET_HEADS, CHUNK, RET_QK)).copy()
    qd = m * q_dec[:, :, None]
    kd = m * k_dec[:, :, None]
    return jnp.asarray(intra), jnp.asarray(m), jnp.asarray(qd), jnp.asarray(kd), [float(c) for c in chunk_dec]


def _rmsnorm(x, g):
    return x * lax.rsqrt(jnp.mean(x * x, axis=-1, keepdims=True) + RMS_EPS) * g


def _inproj(x, norm_g, w_ret, w_rwkv):
    T = x.shape[0]

    def body(x_ref, g_ref, wr_ref, ww_ref, pr_ref, pw_ref, u_ref):
        ub = _rmsnorm(x_ref[...], g_ref[...]).astype(BF16)
        u_ref[...] = ub
        pr_ref[...] = jnp.dot(ub, wr_ref[...], preferred_element_type=F32)
        pw_ref[...] = jnp.dot(ub, ww_ref[...], preferred_element_type=F32)

    return pl.pallas_call(
        body, name="inproj", grid=(T // TOK_TILE,),
        in_specs=[_rows(TOK_TILE, D_MODEL), _full((1, D_MODEL)), _full((D_MODEL, RET_COLS)), _full((D_MODEL, RWKV_COLS))],
        out_specs=[_rows(TOK_TILE, RET_COLS), _rows(TOK_TILE, RWKV_COLS), _rows(TOK_TILE, D_MODEL)],
        out_shape=[jax.ShapeDtypeStruct((T, RET_COLS), F32), jax.ShapeDtypeStruct((T, RWKV_COLS), F32),
                   jax.ShapeDtypeStruct((T, D_MODEL), BF16)],
        compiler_params=_params(dimension_semantics=("arbitrary",)),
    )(x, norm_g, w_ret, w_rwkv)


def _ret_chunk(pq, pk, v_heads, s_heads, cos_t, sin_t, prot, dec, hm, qd, kd, chunk_dec):
    q = pq * cos_t + _dot_hi(pq, prot) * sin_t
    k = (pk * cos_t + _dot_hi(pk, prot) * sin_t) * (RET_DK ** -0.5)
    outs, s_out = [], []
    for h in range(RET_HEADS):
        sc = _dot_nt_hi(q * hm[h], k * hm[h]) * dec[h]
        intra = _dot_hi(sc, v_heads[h])
        kv = _dot_tn_hi(k * kd[h], v_heads[h])
        inter = _dot_hi(q * qd[h], s_heads[h])
        outs.append(intra + inter)
        s_out.append(s_heads[h] * chunk_dec[h] + kv)
    return tuple(outs), tuple(s_out)


def _ret_specs():
    const = [_full((RET_QK, RET_QK)), _full((RET_HEADS, CHUNK, CHUNK)), _full((RET_HEADS, CHUNK, RET_QK)),
             _full((RET_HEADS, CHUNK, RET_QK)), _full((RET_HEADS, CHUNK, RET_QK))]
    return const


def _ret_fwd(p_ret, tabs):
    T = p_ret.shape[0]
    nc = T // CHUNK
    cos_t, sin_t, prot, dec, hm, qd, kd, chunk_dec = tabs

    def body(p_ref, cos_ref, sin_ref, prot_ref, dec_ref, hm_ref, qd_ref, kd_ref, out_ref, sin_save_ref, s_scr):
        @pl.when(pl.program_id(0) == 0)
        def _():
            s_scr[...] = jnp.zeros_like(s_scr)

        s_heads = tuple(s_scr[h] for h in range(RET_HEADS))
        for h in range(RET_HEADS):
            sin_save_ref[0, h] = s_heads[h]
        pq = p_ref[:, 0:RET_QK]
        pk = p_ref[:, RET_QK:2 * RET_QK]
        v_heads = tuple(p_ref[:, 2 * RET_QK + RET_DV * h:2 * RET_QK + RET_DV * (h + 1)] for h in range(RET_HEADS))
        outs, s_out = _ret_chunk(pq, pk, v_heads, s_heads, cos_ref[...], sin_ref[...], prot_ref[...], dec_ref[...],
                                 hm_ref[...], qd_ref[...], kd_ref[...], chunk_dec)
        for h in range(RET_HEADS):
            out_ref[:, RET_DV * h:RET_DV * (h + 1)] = outs[h]
            s_scr[h] = s_out[h]

    return pl.pallas_call(
        body, name="ret_fwd", grid=(nc,),
        in_specs=[pl.BlockSpec((CHUNK, RET_COLS), lambda i: (i, 0)), _rows(CHUNK, RET_QK), _rows(CHUNK, RET_QK)] + _ret_specs(),
        out_specs=[_rows(CHUNK, RET_WIDTH), pl.BlockSpec((1, RET_HEADS, RET_QK, RET_DV), lambda i: (i, 0, 0, 0))],
        out_shape=[jax.ShapeDtypeStruct((T, RET_WIDTH), F32), jax.ShapeDtypeStruct((nc, RET_HEADS, RET_QK, RET_DV), F32)],
        scratch_shapes=[pltpu.VMEM((RET_HEADS, RET_QK, RET_DV), F32)],
        compiler_params=_params(dimension_semantics=("arbitrary",)),
    )(p_ret, cos_t, sin_t, prot, dec, hm, qd, kd)


def _ret_bwd(p_ret, s_saved, d_ret, tabs):
    T = p_ret.shape[0]
    nc = T // CHUNK
    cos_t, sin_t, prot, dec, hm, qd, kd, chunk_dec = tabs

    def body(p_ref, s_ref, dret_ref, cos_ref, sin_ref, prot_ref, dec_ref, hm_ref, qd_ref, kd_ref, dp_ref, ds_scr):
        @pl.when(pl.program_id(0) == 0)
        def _():
            ds_scr[...] = jnp.zeros_like(ds_scr)

        pq = p_ref[:, 0:RET_QK]
        pk = p_ref[:, RET_QK:2 * RET_QK]
        v_heads = tuple(p_ref[:, 2 * RET_QK + RET_DV * h:2 * RET_QK + RET_DV * (h + 1)] for h in range(RET_HEADS))
        s_heads = tuple(s_ref[0, h] for h in range(RET_HEADS))
        consts = (cos_ref[...], sin_ref[...], prot_ref[...], dec_ref[...], hm_ref[...], qd_ref[...], kd_ref[...])
        _, vjp = jax.vjp(lambda a, b, c, d: _ret_chunk(a, b, c, d, *consts, chunk_dec), pq, pk, v_heads, s_heads)
        d_out = tuple(dret_ref[:, RET_DV * h:RET_DV * (h + 1)] for h in range(RET_HEADS))
        d_s = tuple(ds_scr[h] for h in range(RET_HEADS))
        dq, dk, dv, ds_in = vjp((d_out, d_s))
        dp_ref[:, 0:RET_QK] = dq
        dp_ref[:, RET_QK:2 * RET_QK] = dk
        for h in range(RET_HEADS):
            dp_ref[:, 2 * RET_QK + RET_DV * h:2 * RET_QK + RET_DV * (h + 1)] = dv[h]
            ds_scr[h] = ds_in[h]

    rev = lambda i: (nc - 1 - i, 0)
    return pl.pallas_call(
        body, name="ret_bwd", grid=(nc,),
        in_specs=[pl.BlockSpec((CHUNK, RET_COLS), rev),
                  pl.BlockSpec((1, RET_HEADS, RET_QK, RET_DV), lambda i: (nc - 1 - i, 0, 0, 0)),
                  pl.BlockSpec((CHUNK, RET_WIDTH), rev), pl.BlockSpec((CHUNK, RET_QK), rev), pl.BlockSpec((CHUNK, RET_QK), rev)]
        + _ret_specs(),
        out_specs=pl.BlockSpec((CHUNK, 2 * RET_QK + RET_WIDTH), rev),
        out_shape=jax.ShapeDtypeStruct((T, 2 * RET_QK + RET_WIDTH), F32),
        scratch_shapes=[pltpu.VMEM((RET_HEADS, RET_QK, RET_DV), F32)],
        compiler_params=_params(dimension_semantics=("arbitrary",)),
    )(p_ret, s_saved, d_ret, cos_t, sin_t, prot, dec, hm, qd, kd)


def _wkv_consts():
    lane = lax.broadcasted_iota(jnp.int32, (RWKV_HEAD, RWKV_WIDTH), 1)
    sub = lax.broadcasted_iota(jnp.int32, (RWKV_HEAD, RWKV_WIDTH), 0)
    diag = ((lane & (RWKV_HEAD - 1)) == sub).astype(F32)
    r = lax.broadcasted_iota(jnp.int32, (3 * 128, 128), 0)
    c = lax.broadcasted_iota(jnp.int32, (3 * 128, 128), 1)
    ones = (((r & 127) >> 6) == (c >> 6)).astype(BF16)
    return diag, ones


def _stack(x):
    return jnp.concatenate([x[:, 128 * p:128 * (p + 1)] for p in range(4)], axis=0)


def _unstack(y):
    n = y.shape[0] // 4
    return jnp.concatenate([y[n * p:n * (p + 1)] for p in range(4)], axis=1)


def _split(x, n):
    pieces = []
    for _ in range(n):
        p = x.astype(BF16)
        pieces.append(p)
        x = x - p.astype(F32)
    return pieces


def _trunc(x):
    return lax.bitcast_convert_type(lax.bitcast_convert_type(x, jnp.uint32) & jnp.uint32(0xFFFF0000), F32)


def _lane_sum(x, ones):
    xs = _stack(x)
    hi = _trunc(xs)
    lhs = jnp.concatenate([hi, xs - hi], axis=1)
    return _unstack(jnp.dot(lhs, ones[:256].astype(F32), preferred_element_type=F32))


def _expand(row, diag_bf, ones, n):
    lhs = jnp.concatenate([_stack(jnp.broadcast_to(p, diag_bf.shape) * diag_bf) for p in _split(row, n)], axis=1)
    return _unstack(jnp.dot(lhs, ones[:128 * n], preferred_element_type=F32))


def _colsum(x):
    return jnp.sum(x, axis=0, keepdims=True)


def _head_sums(x, ones):
    return _unstack(jnp.dot(jnp.concatenate(_split(_stack(x), 3), axis=1), ones, preferred_element_type=F32))


def _wkv_fwd(r, w, k, v, kap, a):
    T = r.shape[0]
    C = WKV_CHUNK
    nc = T // C

    def body(r_ref, w_ref, k_ref, v_ref, kap_ref, a_ref, o_ref, s_all_ref, s_scr, vf_scr):
        @pl.when(pl.program_id(0) == 0)
        def _():
            s_scr[...] = jnp.zeros_like(s_scr)

        diag, ones = _wkv_consts()
        diag_bf = diag.astype(BF16)
        rr, ww, kk, vv, kap_, aa = (ref[...] for ref in (r_ref, w_ref, k_ref, v_ref, kap_ref, a_ref))
        bb = kap_ * aa
        z = pltpu.roll(ww, 1, axis=0) * kap_
        c1 = _head_sums(pltpu.roll(bb, 1, axis=0) * kap_, ones)
        c2 = _head_sums(pltpu.roll(kk, 1, axis=0) * kap_, ones)
        row = lambda x, t: x[t:t + 1]

        for t in range(C):
            vf_scr[t] = _expand(row(vv, t), diag_bf, ones, 2)

        s_prev = s_scr[...]
        sa = _lane_sum(s_prev * (-row(kap_, 0)), ones)
        ls_next = _lane_sum(s_prev * (-row(z, 1)), ones)
        ls_now, vf, rows = None, None, []
        for t in range(C):
            if t > 0:
                sa = ls_now - sa * row(c1, t) - vf * row(c2, t)
            vf = vf_scr[t]
            s_prev = s_prev * row(ww, t) + sa * row(bb, t) + vf * row(kk, t)
            ls_now = ls_next
            if t + 2 < C:
                ls_next = _lane_sum(s_prev * (-row(z, t + 2)), ones)
            s_all_ref[t] = s_prev
            rows.append(_colsum(_lane_sum(s_prev * row(rr, t), ones) * diag))
            if t % 8 == 7:
                o_ref[t - 7:t + 1, :] = jnp.concatenate(rows, axis=0)
                rows = []
        s_scr[...] = s_prev

    spec = _rows(C, RWKV_WIDTH)
    return pl.pallas_call(
        body, name="wkv_fwd", grid=(nc,),
        in_specs=[spec] * 6,
        out_specs=[spec, pl.BlockSpec((C, RWKV_HEAD, RWKV_WIDTH), lambda i: (i, 0, 0))],
        out_shape=[jax.ShapeDtypeStruct((T, RWKV_WIDTH), F32), jax.ShapeDtypeStruct((T, RWKV_HEAD, RWKV_WIDTH), F32)],
        scratch_shapes=[pltpu.VMEM((RWKV_HEAD, RWKV_WIDTH), F32), pltpu.VMEM((C, RWKV_HEAD, RWKV_WIDTH), F32)],
        compiler_params=_params(dimension_semantics=("arbitrary",)),
    )(r, w, k, v, kap, a)


def _wkv_bwd(r, w, k, v, kap, a, s_all, d_o):
    T = r.shape[0]
    C = WKV_CHUNK
    nc = T // C

    def body(r_ref, w_ref, k_ref, v_ref, kap_ref, a_ref, s_ref, s_before_ref, do_ref,
             dr_ref, dw_ref, dk_ref, dv_ref, dkap_ref, da_ref, ds_scr, vf_scr, dof_scr, sa_scr):
        first_chunk = pl.program_id(0) == nc - 1

        @pl.when(pl.program_id(0) == 0)
        def _():
            ds_scr[...] = jnp.zeros_like(ds_scr)

        diag, ones = _wkv_consts()
        diag_bf = diag.astype(BF16)
        rr, ww, kk, vv, kap_, aa, dd = (ref[...] for ref in (r_ref, w_ref, k_ref, v_ref, kap_ref, a_ref, do_ref))
        bb = kap_ * aa
        zb = pltpu.roll(ww, C - 1, axis=0) * bb
        e1 = _head_sums(pltpu.roll(kap_, C - 1, axis=0) * bb, ones)
        e2 = _head_sums(rr * bb, ones)
        row = lambda x, t: x[t:t + 1]

        def state_before(t):
            return s_ref[t - 1] if t > 0 else jnp.where(first_chunk, 0.0, s_before_ref[0])

        for t in range(C):
            vf_scr[t] = _expand(row(vv, t), diag_bf, ones, 2)
            dof_scr[t] = _expand(row(dd, t), diag_bf, ones, 2)
            sa_scr[t] = _lane_sum(state_before(t) * (-row(kap_, t)), ones)

        d_s = ds_scr[...]
        ls_now, dsa, rows = None, None, [None] * C
        for t in reversed(range(C)):
            dof = dof_scr[t]
            d_sn = d_s + dof * row(rr, t)
            if t == C - 1:
                dsa = _lane_sum(d_sn * row(bb, t), ones)
            else:
                dsa = ls_now - dsa * row(e1, t) + dof * row(e2, t)
            if t > 0:
                ls_now = _lane_sum(d_sn * row(zb, t - 1), ones)
            s_prev = state_before(t)
            dv = _colsum(_lane_sum(d_sn * row(kk, t), ones) * diag)
            db = _colsum(d_sn * sa_scr[t])
            rows[t] = (_colsum(s_ref[t] * dof), _colsum(d_sn * s_prev), _colsum(d_sn * vf_scr[t]), dv,
                       db * row(aa, t) - _colsum(dsa * s_prev), db * row(kap_, t))
            d_s = d_sn * row(ww, t) - dsa * row(kap_, t)
            if t % 8 == 0:
                for j, ref in enumerate((dr_ref, dw_ref, dk_ref, dv_ref, dkap_ref, da_ref)):
                    ref[t:t + 8, :] = jnp.concatenate([rows[u][j] for u in range(t, t + 8)], axis=0)
        ds_scr[...] = d_s

    spec = pl.BlockSpec((C, RWKV_WIDTH), lambda i: (nc - 1 - i, 0))
    states = pl.BlockSpec((C, RWKV_HEAD, RWKV_WIDTH), lambda i: (nc - 1 - i, 0, 0))
    before = pl.BlockSpec((1, RWKV_HEAD, RWKV_WIDTH), lambda i: (jnp.maximum((nc - 1 - i) * C - 1, 0), 0, 0))
    big = pltpu.VMEM((C, RWKV_HEAD, RWKV_WIDTH), F32)
    return pl.pallas_call(
        body, name="wkv_bwd", grid=(nc,),
        in_specs=[spec] * 6 + [states, before, spec],
        out_specs=[spec] * 6,
        out_shape=[jax.ShapeDtypeStruct((T, RWKV_WIDTH), F32)] * 6,
        scratch_shapes=[pltpu.VMEM((RWKV_HEAD, RWKV_WIDTH), F32), big, big, big],
        compiler_params=_params(dimension_semantics=("arbitrary",)),
    )(r, w, k, v, kap, a, s_all, s_all, d_o)


W = RWKV_WIDTH


def _softplus(y):
    return jnp.maximum(y, 0.0) + jnp.log(1.0 + jnp.exp(-jnp.abs(y)))


def _prep_fn(kr, xwa, w0, a0, k_k, k_a, wup_pad, aup_pad, ones64):
    w_log = -_softplus(-(w0 + _dot_hi(jnp.tanh(xwa), wup_pad))) - 0.5
    decay = jnp.exp(-jnp.exp(w_log))
    a = jax.nn.sigmoid(a0 + _dot_hi(xwa, aup_pad))
    kk = kr * k_k
    kap = kk / jnp.maximum(jnp.sqrt(_dot_hi(kk * kk, ones64)), 1e-12)
    k = kr * (1.0 + (a - 1.0) * k_a)
    return decay, k, kap, a


def _shift_down(p, first_row):
    rows = lax.broadcasted_iota(jnp.int32, p.shape, 0)
    return jnp.where(rows == 0, first_row, pltpu.roll(p, 1, axis=0))


def _shift_up(z, last_row):
    n = z.shape[0]
    rows = lax.broadcasted_iota(jnp.int32, z.shape, 0)
    return jnp.where(rows == n - 1, last_row, pltpu.roll(z, n - 1, axis=0))


def _prev_block_spec():
    return pl.BlockSpec((8, RWKV_COLS), lambda i: (jnp.maximum(i * (TOK_TILE // 8) - 1, 0), 0))


def _mixed(p_ref, prev8_ref, mu_ref, first_tile):
    p = p_ref[...]
    first_row = jnp.where(first_tile, 0.0, prev8_ref[7:8, :])
    prev = _shift_down(p, first_row)
    return p, prev, p + mu_ref[...] * (prev - p)


def _prep_fwd(p_rwkv, mu, w0, a0, k_k, k_a, wup_pad, aup_pad, ones64):
    T = p_rwkv.shape[0]

    def body(p_ref, prev8_ref, mu_ref, w0_ref, a0_ref, kk_ref, ka_ref, wup_ref, aup_ref, ones_ref,
             r_ref, w_ref, k_ref, v_ref, kap_ref, a_ref, g_ref):
        _, _, ps = _mixed(p_ref, prev8_ref, mu_ref, pl.program_id(0) == 0)
        decay, k, kap, a = _prep_fn(ps[:, W:2 * W], ps[:, 4 * W:], w0_ref[...], a0_ref[...], kk_ref[...], ka_ref[...],
                                    wup_ref[...], aup_ref[...], ones_ref[...])
        r_ref[...] = ps[:, 0:W]
        w_ref[...] = decay
        k_ref[...] = k
        v_ref[...] = ps[:, 2 * W:3 * W]
        kap_ref[...] = kap
        a_ref[...] = a
        g_ref[...] = ps[:, 3 * W:4 * W]

    vec = _full((1, W))
    return pl.pallas_call(
        body, name="prep_fwd", grid=(T // TOK_TILE,),
        in_specs=[_rows(TOK_TILE, RWKV_COLS), _prev_block_spec(), _full((1, RWKV_COLS)), vec, vec, vec, vec,
                  _full((2 * LORA, W)), _full((2 * LORA, W)), _full((W, W))],
        out_specs=[_rows(TOK_TILE, W)] * 7,
        out_shape=[jax.ShapeDtypeStruct((T, W), F32)] * 7,
        compiler_params=_params(dimension_semantics=("arbitrary",)),
    )(p_rwkv, p_rwkv, mu, w0, a0, k_k, k_a, wup_pad, aup_pad, ones64)


def _prep_bwd(p_rwkv, mu, w0, a0, k_k, k_a, wup_pad, aup_pad, ones64, dr, dw, dk, dv, dkap, da, dg, dr2, dk2, dv2):
    T = p_rwkv.shape[0]
    nt = T // TOK_TILE

    def body(p_ref, prev8_ref, mu_ref, w0_ref, a0_ref, kk_ref, ka_ref, wup_ref, aup_ref, ones_ref,
             dr_ref, dw_ref, dk_ref, dv_ref, dkap_ref, da_ref, dg_ref, dr2_ref, dk2_ref, dv2_ref,
             dp_ref, dmu_ref, dw0_ref, da0_ref, dkk_ref, dka_ref, dwup_ref, daup_ref, zrow_scr):
        i = pl.program_id(0)
        accs = (dmu_ref, dw0_ref, da0_ref, dkk_ref, dka_ref, dwup_ref, daup_ref)

        @pl.when(i == 0)
        def _():
            zrow_scr[...] = jnp.zeros_like(zrow_scr)
            for ref in accs:
                ref[...] = jnp.zeros_like(ref)

        p, prev, ps = _mixed(p_ref, prev8_ref, mu_ref, i == nt - 1)
        ones = ones_ref[...]
        _, vjp = jax.vjp(lambda *args: _prep_fn(*args, ones), ps[:, W:2 * W], ps[:, 4 * W:], w0_ref[...], a0_ref[...],
                         kk_ref[...], ka_ref[...], wup_ref[...], aup_ref[...])
        dkr, dxwa, dw0, da0, dkk, dka, dwup, daup = vjp(
            (dw_ref[...], dk_ref[...] + dk2_ref[...], dkap_ref[...], da_ref[...]))
        dps = jnp.concatenate([dr_ref[...] + dr2_ref[...], dkr, dv_ref[...] + dv2_ref[...], dg_ref[...], dxwa], axis=1)
        z = dps * mu_ref[...]
        dp_ref[...] = dps - z + _shift_up(z, zrow_scr[0:1, :])
        zrow_scr[0:1, :] = z[0:1, :]
        for ref, val in zip(accs, (_colsum(dps * (prev - p)), dw0, da0, dkk, dka, dwup, daup)):
            ref[...] += val

    rev = lambda i: (nt - 1 - i, 0)
    vec = _full((1, W))
    lora = _full((2 * LORA, W))
    tile = pl.BlockSpec((TOK_TILE, W), rev)
    prev8 = pl.BlockSpec((8, RWKV_COLS), lambda i: (jnp.maximum((nt - 1 - i) * (TOK_TILE // 8) - 1, 0), 0))
    return pl.pallas_call(
        body, name="prep_bwd", grid=(nt,),
        in_specs=[pl.BlockSpec((TOK_TILE, RWKV_COLS), rev), prev8, _full((1, RWKV_COLS)), vec, vec, vec, vec, lora, lora,
                  _full((W, W))] + [tile] * 10,
        out_specs=[pl.BlockSpec((TOK_TILE, RWKV_COLS), rev), _full((1, RWKV_COLS)), vec, vec, vec, vec, lora, lora],
        out_shape=[jax.ShapeDtypeStruct((T, RWKV_COLS), F32), jax.ShapeDtypeStruct((1, RWKV_COLS), F32)]
        + [jax.ShapeDtypeStruct((1, W), F32)] * 4 + [jax.ShapeDtypeStruct((2 * LORA, W), F32)] * 2,
        scratch_shapes=[pltpu.VMEM((8, RWKV_COLS), F32)],
        compiler_params=_params(dimension_semantics=("arbitrary",)),
    )(p_rwkv, p_rwkv, mu, w0, a0, k_k, k_a, wup_pad, aup_pad, ones64, dr, dw, dk, dv, dkap, da, dg, dr2, dk2, dv2)


def _silu(x):
    return x * jax.nn.sigmoid(x)


def _post_y(o, r, k, v, g_rw, ret_raw, g_ret, ret_gn_g, gn_g, gn_b, r_k, avg128, avg64, ones64):
    xc = ret_raw - _dot_hi(ret_raw, avg128)
    ret = xc * lax.rsqrt(_dot_hi(xc * xc, avg128) + RET_GN_EPS)
    y_ret = _silu(g_ret) * (ret * ret_gn_g)
    oc = o - _dot_hi(o, avg64)
    on = oc * lax.rsqrt(_dot_hi(oc * oc, avg64) + RWKV_GN_EPS) * gn_g + gn_b
    bonus = _dot_hi(r * k * r_k, ones64) * v
    y_rwkv = _silu(g_rw) * (on + bonus)
    return y_ret, y_rwkv


def _post_loss(h, final_g, target):
    err = _rmsnorm(h, final_g) - target
    return 0.5 * jnp.sum(jnp.mean(err * err, axis=-1))


def _post(o, r, k, v, g_rw, ret_raw, p_ret, x, target, ret_gn_g, gn_g, gn_b, r_k, final_g, w_out, avg128, avg64, ones64):
    T = x.shape[0]
    n_tok_out = 8

    def body(o_ref, r_ref, k_ref, v_ref, grw_ref, ret_ref, gret_ref, x_ref, tgt_ref, rg_ref, gg_ref, gb_ref, rk_ref, fg_ref,
             wo_ref, a128_ref, a64_ref, ones_ref, *outs):
        tok_outs, (dwo_ref, drg_ref, dgg_ref, dgb_ref, drk_ref, dfg_ref, loss_ref) = outs[:n_tok_out], outs[n_tok_out:]
        accs = (dwo_ref, drg_ref, dgg_ref, dgb_ref, drk_ref, dfg_ref, loss_ref)

        @pl.when(pl.program_id(0) == 0)
        def _():
            for ref in accs:
                ref[...] = jnp.zeros_like(ref)

        consts = (a128_ref[...], a64_ref[...], ones_ref[...])
        (y_ret, y_rwkv), vjp = jax.vjp(
            lambda *args: _post_y(*args, *consts), o_ref[...], r_ref[...], k_ref[...], v_ref[...], grw_ref[...], ret_ref[...],
            gret_ref[...], rg_ref[...], gg_ref[...], gb_ref[...], rk_ref[...])
        h = x_ref[...] + _dot_bf(y_ret, wo_ref[0:RET_WIDTH, :]) + _dot_bf(y_rwkv, wo_ref[RET_WIDTH:, :])
        loss, (dh, dfg) = jax.value_and_grad(_post_loss, argnums=(0, 1))(h, fg_ref[...], tgt_ref[...])
        dy_ret = _dot_nt_bf(dh, wo_ref[0:RET_WIDTH, :])
        dy_rwkv = _dot_nt_bf(dh, wo_ref[RET_WIDTH:, :])
        do, dr, dk, dv, dgrw, dret, dgret, drg, dgg, dgb, drk = vjp((dy_ret, dy_rwkv))
        for ref, val in zip(tok_outs, (dh, do, dr, dk, dv, dgrw, dret, dgret)):
            ref[...] = val
        dwo_ref[0:RET_WIDTH, :] += _dot_tn_bf(y_ret, dh)
        dwo_ref[RET_WIDTH:, :] += _dot_tn_bf(y_rwkv, dh)
        for ref, val in zip(accs[1:], (drg, dgg, dgb, drk, dfg, jnp.full((1, 128), loss, F32))):
            ref[...] += val

    tile = _rows(TOK_TILE, W)
    wide = _rows(TOK_TILE, D_MODEL)
    vec = _full((1, W))
    sq = _full((W, W))
    return pl.pallas_call(
        body, name="post", grid=(T // TOK_TILE,),
        in_specs=[tile] * 6 + [pl.BlockSpec((TOK_TILE, W), lambda i: (i, 2)), wide, wide, vec, vec, vec, vec,
                               _full((1, D_MODEL)), _full((D_MODEL, D_MODEL)), sq, sq, sq],
        out_specs=[wide] + [tile] * 7 + [_full((D_MODEL, D_MODEL)), vec, vec, vec, vec, _full((1, D_MODEL)), _full((1, 128))],
        out_shape=[jax.ShapeDtypeStruct((T, D_MODEL), F32)] + [jax.ShapeDtypeStruct((T, W), F32)] * 7
        + [jax.ShapeDtypeStruct((D_MODEL, D_MODEL), F32)] + [jax.ShapeDtypeStruct((1, W), F32)] * 4
        + [jax.ShapeDtypeStruct((1, D_MODEL), F32), jax.ShapeDtypeStruct((1, 128), F32)],
        compiler_params=_params(dimension_semantics=("arbitrary",)),
    )(o, r, k, v, g_rw, ret_raw, p_ret, x, target, ret_gn_g, gn_g, gn_b, r_k, final_g, w_out, avg128, avg64, ones64)


def _inproj_bwd_x(x, norm_g, dp_qkv, dg_ret, dp_rwkv, dh, w_ret, w_rwkv):
    T = x.shape[0]
    n_qkv = 2 * RET_QK + RET_WIDTH

    def body(x_ref, g_ref, dqkv_ref, dgret_ref, drwkv_ref, dh_ref, wr_ref, ww_ref, dx_ref, dg_ref):
        @pl.when(pl.program_id(0) == 0)
        def _():
            dg_ref[...] = jnp.zeros_like(dg_ref)

        _, vjp = jax.vjp(_rmsnorm, x_ref[...], g_ref[...])
        du = (_dot_nt_bf(dqkv_ref[...], wr_ref[:, 0:n_qkv]) + _dot_nt_bf(dgret_ref[...], wr_ref[:, n_qkv:])
              + _dot_nt_bf(drwkv_ref[...], ww_ref[...]))
        dx, dg = vjp(du)
        dx_ref[...] = dx + dh_ref[...]
        dg_ref[...] += dg

    return pl.pallas_call(
        body, name="inproj_bwd_x", grid=(T // TOK_TILE,),
        in_specs=[_rows(TOK_TILE, D_MODEL), _full((1, D_MODEL)), _rows(TOK_TILE, n_qkv), _rows(TOK_TILE, RET_WIDTH),
                  _rows(TOK_TILE, RWKV_COLS), _rows(TOK_TILE, D_MODEL), _full((D_MODEL, RET_COLS)), _full((D_MODEL, RWKV_COLS))],
        out_specs=[_rows(TOK_TILE, D_MODEL), _full((1, D_MODEL))],
        out_shape=[jax.ShapeDtypeStruct((T, D_MODEL), F32), jax.ShapeDtypeStruct((1, D_MODEL), F32)],
        compiler_params=_params(dimension_semantics=("arbitrary",)),
    )(x, norm_g, dp_qkv, dg_ret, dp_rwkv, dh, w_ret, w_rwkv)


def _grad_w(name, u, dp):
    T, n = dp.shape
    tile = 2 * TOK_TILE

    def body(u_ref, dp_ref, out_ref):
        @pl.when(pl.program_id(0) == 0)
        def _():
            out_ref[...] = jnp.zeros_like(out_ref)

        out_ref[...] += _dot_tn_bf(u_ref[...], dp_ref[...])

    return pl.pallas_call(
        body, name=name, grid=(T // tile,),
        in_specs=[_rows(tile, D_MODEL), _rows(tile, n)],
        out_specs=_full((D_MODEL, n)),
        out_shape=jax.ShapeDtypeStruct((D_MODEL, n), F32),
        compiler_params=_params(dimension_semantics=("arbitrary",)),
    )(u, dp)


def _pad_lora(w_up, top):
    z = jnp.zeros_like(w_up)
    return jnp.concatenate([w_up, z] if top else [z, w_up], axis=0)


def _local_grads(x, target, norm_g, w_in_bf, ret_gn_g, mu, w_lora_up, w0, a_lora_up, a0, k_k, k_a, r_k, gn_g, gn_b,
                 w_out_bf, final_g):
    T = x.shape[0]
    tabs = _rope_tables(T) + _ret_tables()
    ones64 = _block_ones(W, RWKV_HEAD)
    avg64 = _block_ones(W, RWKV_HEAD, 1.0 / RWKV_HEAD)
    avg128 = _block_ones(RET_WIDTH, RET_DV, 1.0 / RET_DV)
    w_ret, w_rwkv = w_in_bf[:, :RET_COLS], w_in_bf[:, RET_COLS:]
    wup_pad, aup_pad = _pad_lora(w_lora_up, True), _pad_lora(a_lora_up, False)

    p_ret, p_rwkv, u = _inproj(x, norm_g, w_ret, w_rwkv)
    ret_raw, s_saved = _ret_fwd(p_ret, tabs)
    r, w, k, v, kap, a, g_rw = _prep_fwd(p_rwkv, mu, w0, a0, k_k, k_a, wup_pad, aup_pad, ones64)
    o, s_all = _wkv_fwd(r, w, k, v, kap, a)
    (dh, do, dr2, dk2, dv2, dgrw, dret, dgret, d_w_out, d_ret_gn_g, d_gn_g, d_gn_b, d_r_k, d_final_g, loss) = _post(
        o, r, k, v, g_rw, ret_raw, p_ret, x, target, ret_gn_g, gn_g, gn_b, r_k, final_g, w_out_bf, avg128, avg64, ones64)
    dr, dw, dk, dv, dkap, da = _wkv_bwd(r, w, k, v, kap, a, s_all, do)
    dp_rwkv, d_mu, d_w0, d_a0, d_k_k, d_k_a, d_wup, d_aup = _prep_bwd(
        p_rwkv, mu, w0, a0, k_k, k_a, wup_pad, aup_pad, ones64, dr, dw, dk, dv, dkap, da, dgrw, dr2, dk2, dv2)
    dp_qkv = _ret_bwd(p_ret, s_saved, dret, tabs)
    dx, d_norm_g = _inproj_bwd_x(x, norm_g, dp_qkv, dgret, dp_rwkv, dh, w_ret, w_rwkv)
    d_w_in = jnp.concatenate([_grad_w("grad_w_qkv", u, dp_qkv), _grad_w("grad_w_gret", u, dgret),
                              _grad_w("grad_w_rwkv", u, dp_rwkv)], axis=1)
    grads = dict(norm_g=d_norm_g, w_in=d_w_in, ret_gn_g=d_ret_gn_g, rwkv_mu=d_mu, w_lora_up=d_wup[:LORA], w0=d_w0,
                 a_lora_up=d_aup[LORA:], a0=d_a0, k_k=d_k_k, k_a=d_k_a, r_k=d_r_k, rwkv_gn_g=d_gn_g, rwkv_gn_b=d_gn_b,
                 w_out=d_w_out, final_norm_g=d_final_g)
    return loss, dx, grads


def _mesh_pos():
    return lax.axis_index("x"), lax.axis_index("y"), lax.axis_index("c")


def _all_gather(shards):
    n = len(shards)

    def body(*refs):
        x_refs, out_refs = refs[:n], refs[n:2 * n]
        send_sems, recv_sems, local_sems = refs[2 * n:]
        x, y, c = _mesh_pos()
        me, sibling = (x, y, c), (x, y, 1 - c)
        chips = [(1 - x, y), (x, 1 - y), (1 - x, 1 - y)]

        def rows(a, pos):
            m = x_refs[a].shape[0]
            return out_refs[a].at[pl.ds((4 * pos[0] + 2 * pos[1] + pos[2]) * m, m), :]

        def copy(a, k, block, to, src=None):
            return pltpu.make_async_remote_copy(
                src_ref=rows(a, block) if src is None else src, dst_ref=rows(a, block),
                send_sem=send_sems.at[a, k], recv_sem=recv_sems.at[a, k], device_id=to, device_id_type=MESH)

        mine = [pltpu.make_async_copy(x_refs[a], rows(a, me), local_sems.at[a]) for a in range(n)]
        for cp in mine:
            cp.start()
        first = []
        for a in range(n):
            first.append(copy(a, 0, me, sibling, src=x_refs[a]))
            first += [copy(a, 1 + j, me, (*chip, c), src=x_refs[a]) for j, chip in enumerate(chips)]
        for cp in first:
            cp.start()
        passed = []
        for j, chip in enumerate(chips):
            for a in range(n):
                copy(a, 1 + j, (*chip, c), me).wait_recv()
                passed.append(copy(a, 4 + j, (*chip, c), sibling))
                passed[-1].start()
        for a in range(n):
            copy(a, 0, sibling, me).wait_recv()
            for j, chip in enumerate(chips):
                copy(a, 4 + j, (*chip, 1 - c), me).wait_recv()
        for cp in first + passed:
            cp.wait_send()
        for cp in mine:
            cp.wait()

    vmem = pl.BlockSpec(memory_space=pltpu.VMEM)
    return pl.pallas_call(
        body, name="gather_weights",
        out_shape=[jax.ShapeDtypeStruct((N_DEV * s.shape[0], s.shape[1]), s.dtype) for s in shards],
        in_specs=[vmem] * n, out_specs=[vmem] * n,
        scratch_shapes=[pltpu.SemaphoreType.DMA((n, 7)), pltpu.SemaphoreType.DMA((n, 7)), pltpu.SemaphoreType.DMA((n,))],
        compiler_params=_params(),
    )(*shards)


def _exchange(parts):
    n = len(parts)

    def body(*refs):
        in_refs, out_refs = refs[:n], refs[n:2 * n]
        send_sems, recv_sems, local_sems = refs[2 * n:]
        x, y, c = _mesh_pos()
        me = 4 * x + 2 * y + c
        own = [pltpu.make_async_copy(in_refs[a].at[me], out_refs[a].at[me], local_sems.at[a]) for a in range(n)]
        for cp in own:
            cp.start()
        copies = []
        for k in range(1, N_DEV):
            peer = (x ^ (k >> 2), y ^ ((k >> 1) & 1), c ^ (k & 1))
            peer_idx = 4 * peer[0] + 2 * peer[1] + peer[2]
            for a in range(n):
                copies.append(pltpu.make_async_remote_copy(
                    src_ref=in_refs[a].at[peer_idx], dst_ref=out_refs[a].at[me],
                    send_sem=send_sems.at[a, k - 1], recv_sem=recv_sems.at[a, k - 1], device_id=peer, device_id_type=MESH))
        for cp in copies:
            cp.start()
        for cp in copies:
            cp.wait()
        for cp in own:
            cp.wait()

    hbm = pl.BlockSpec(memory_space=pl.ANY)
    return pl.pallas_call(
        body, name="exchange_grads",
        out_shape=[jax.ShapeDtypeStruct(p.shape, p.dtype) for p in parts],
        in_specs=[hbm] * n, out_specs=[hbm] * n,
        scratch_shapes=[pltpu.SemaphoreType.DMA((n, 7)), pltpu.SemaphoreType.DMA((n, 7)), pltpu.SemaphoreType.DMA((n,))],
        compiler_params=_params(),
    )(*parts)


def _adamw(w, g, m, v):
    m = ADAM_B1 * m + (1.0 - ADAM_B1) * g
    v = ADAM_B2 * v + (1.0 - ADAM_B2) * (g * g)
    m_hat = m / (1.0 - ADAM_B1 ** ADAM_STEP)
    v_hat = v / (1.0 - ADAM_B2 ** ADAM_STEP)
    return -ADAM_LR * (m_hat / (jnp.sqrt(v_hat) + ADAM_EPS) + ADAM_WD * w), m, v


def _reduce_adamw(name, parts, w, m, v, row_tile):
    _, rows, cols = parts.shape

    def body(p_ref, w_ref, m_ref, v_ref, g_ref, d_ref, nm_ref, nv_ref):
        g = p_ref[0]
        for s in range(1, N_DEV):
            g = g + p_ref[s]
        g_ref[...] = g
        d_ref[...], nm_ref[...], nv_ref[...] = _adamw(w_ref[...], g, m_ref[...], v_ref[...])

    tile = pl.BlockSpec((row_tile, cols), lambda i: (i, 0))
    return pl.pallas_call(
        body, name=name, grid=(rows // row_tile,),
        in_specs=[pl.BlockSpec((N_DEV, row_tile, cols), lambda i: (0, i, 0)), tile, tile, tile],
        out_specs=[tile] * 4,
        out_shape=[jax.ShapeDtypeStruct((rows, cols), F32)] * 4,
        compiler_params=_params(dimension_semantics=("arbitrary",)),
    )(parts, w, m, v)


_SMALL = (("norm_g", 1024), ("ret_gn_g", 512), ("rwkv_mu", 2176), ("w0", 512), ("a0", 512), ("k_k", 512), ("k_a", 512),
          ("r_k", 512), ("rwkv_gn_g", 512), ("rwkv_gn_b", 512), ("final_norm_g", 1024))
_SMALL_ROWS = 72
_WEIGHTS = ("norm_g", "w_in", "ret_gn_g", "rwkv_mu", "w_lora_up", "w0", "a_lora_up", "a0", "k_k", "k_a", "r_k", "rwkv_gn_g",
            "rwkv_gn_b", "w_out", "final_norm_g")


def _pack_small(d, extra_row=None):
    rows = [d[n].reshape(-1, 128) for n, _ in _SMALL]
    used = sum(r.shape[0] for r in rows)
    rows.append(jnp.zeros((1, 128), F32) if extra_row is None else extra_row)
    rows.append(jnp.zeros((_SMALL_ROWS - used - 1, 128), F32))
    return jnp.concatenate(rows, axis=0)


def _unpack_small(packed, shapes):
    out, row = {}, 0
    for n, size in _SMALL:
        out[n] = packed[row:row + size // 128].reshape(shapes[n])
        row += size // 128
    return out, packed[row, 0]


def kernel(x, norm_g, w_in, ret_gn_g, rwkv_mu, w_lora_up, w0, a_lora_up, a0, k_k, k_a, r_k, rwkv_gn_g, rwkv_gn_b, w_out, final_norm_g, loss_target, m_norm_g, m_w_in, m_ret_gn_g, m_rwkv_mu, m_w_lora_up, m_w0, m_a_lora_up, m_a0, m_k_k, m_k_a, m_r_k, m_rwkv_gn_g, m_rwkv_gn_b, m_w_out, m_final_norm_g, v_norm_g, v_w_in, v_ret_gn_g, v_rwkv_mu, v_w_lora_up, v_w0, v_a_lora_up, v_a0, v_k_k, v_k_a, v_r_k, v_rwkv_gn_g, v_rwkv_gn_b, v_w_out, v_final_norm_g):
    wts = dict(norm_g=norm_g, w_in=w_in, ret_gn_g=ret_gn_g, rwkv_mu=rwkv_mu, w_lora_up=w_lora_up, w0=w0, a_lora_up=a_lora_up,
               a0=a0, k_k=k_k, k_a=k_a, r_k=r_k, rwkv_gn_g=rwkv_gn_g, rwkv_gn_b=rwkv_gn_b, w_out=w_out,
               final_norm_g=final_norm_g)
    mom = dict(norm_g=m_norm_g, w_in=m_w_in, ret_gn_g=m_ret_gn_g, rwkv_mu=m_rwkv_mu, w_lora_up=m_w_lora_up, w0=m_w0,
               a_lora_up=m_a_lora_up, a0=m_a0, k_k=m_k_k, k_a=m_k_a, r_k=m_r_k, rwkv_gn_g=m_rwkv_gn_g,
               rwkv_gn_b=m_rwkv_gn_b, w_out=m_w_out, final_norm_g=m_final_norm_g)
    var = dict(norm_g=v_norm_g, w_in=v_w_in, ret_gn_g=v_ret_gn_g, rwkv_mu=v_rwkv_mu, w_lora_up=v_w_lora_up, w0=v_w0,
               a_lora_up=v_a_lora_up, a0=v_a0, k_k=v_k_k, k_a=v_k_a, r_k=v_r_k, rwkv_gn_g=v_rwkv_gn_g,
               rwkv_gn_b=v_rwkv_gn_b, w_out=v_w_out, final_norm_g=v_final_norm_g)
    shapes = {n: wts[n].shape for n in _WEIGHTS}

    g_in, g_out, g_wup, g_aup = _all_gather(
        [w_in[0].astype(BF16), w_out[0].astype(BF16), w_lora_up[0], a_lora_up[0]])
    w_in_bf = g_in.reshape(N_DEV, D_MODEL, SHARD_IN).transpose(1, 0, 2).reshape(D_MODEL, IN_COLS)
    wup_full = g_wup.reshape(N_DEV, LORA, SHARD_LORA).transpose(1, 0, 2).reshape(LORA, W)
    aup_full = g_aup.reshape(N_DEV, LORA, SHARD_LORA).transpose(1, 0, 2).reshape(LORA, W)

    loss, dx, g = _local_grads(
        x[0], loss_target[0], norm_g, w_in_bf, ret_gn_g, rwkv_mu, wup_full, w0, aup_full, a0, k_k, k_a,
        r_k.reshape(1, W), rwkv_gn_g, rwkv_gn_b, g_out, final_norm_g.reshape(1, D_MODEL))

    small = _pack_small(g, loss[:, :128])
    parts = _exchange([
        g["w_in"].reshape(D_MODEL, N_DEV, SHARD_IN).transpose(1, 0, 2),
        g["w_out"].reshape(N_DEV, SHARD_OUT, D_MODEL),
        g["w_lora_up"].reshape(LORA, N_DEV, SHARD_LORA).transpose(1, 0, 2),
        g["a_lora_up"].reshape(LORA, N_DEV, SHARD_LORA).transpose(1, 0, 2),
        jnp.broadcast_to(small[None], (N_DEV, _SMALL_ROWS, 128))])
    res = {}
    res["w_in"] = _reduce_adamw("adamw_w_in", parts[0], w_in[0], m_w_in[0], v_w_in[0], 256)
    res["w_out"] = _reduce_adamw("adamw_w_out", parts[1], w_out[0], m_w_out[0], v_w_out[0], SHARD_OUT)
    res["w_lora_up"] = _reduce_adamw("adamw_w_lora_up", parts[2], w_lora_up[0], m_w_lora_up[0], v_w_lora_up[0], LORA)
    res["a_lora_up"] = _reduce_adamw("adamw_a_lora_up", parts[3], a_lora_up[0], m_a_lora_up[0], v_a_lora_up[0], LORA)
    sm = _reduce_adamw("adamw_vectors", parts[4], _pack_small(wts), _pack_small(mom), _pack_small(var), _SMALL_ROWS)
    unpacked = [_unpack_small(t, shapes) for t in sm]
    total_loss = unpacked[0][1]
    for n, _ in _SMALL:
        res[n] = [u[0][n] for u in unpacked]
    for n in ("w_in", "w_out", "w_lora_up", "a_lora_up"):
        res[n] = [t.reshape(shapes[n]) for t in res[n]]
    return (total_loss, dx[None], *[res[n][0] for n in _WEIGHTS], *[res[n][1] for n in _WEIGHTS],
            *[res[n][2] for n in _WEIGHTS], *[res[n][3] for n in _WEIGHTS])
```

```python
import functools

import numpy as np
import jax
import jax.numpy as jnp
from jax import lax
from jax.experimental import pallas as pl
from jax.experimental.pallas import tpu as pltpu

F32 = jnp.float32
BF16 = jnp.bfloat16

D_MODEL = 1024
CHUNK = 64
RET_HEADS = 4
RET_DV = 128
RET_DK = 64
RET_QK = 256
RET_WIDTH = 512
RWKV_WIDTH = 512
RWKV_HEAD = 64
RWKV_HEADS = 8
LORA = 64
RET_COLS = 2 * RET_QK + 2 * RET_WIDTH
RWKV_COLS = 4 * RWKV_WIDTH + 2 * LORA
IN_COLS = RET_COLS + RWKV_COLS
ROPE_BASE = 10000.0
RMS_EPS = 1e-6
RET_GN_EPS = 1e-5
RWKV_GN_EPS = 64e-5
ADAM_LR = 0.001
ADAM_B1 = 0.9
ADAM_B2 = 0.999
ADAM_EPS = 1e-08
ADAM_WD = 0.01
ADAM_STEP = 10
N_DEV = 8
SHARD_IN = IN_COLS // N_DEV
SHARD_OUT = D_MODEL // N_DEV
SHARD_LORA = RWKV_WIDTH // N_DEV
VMEM_LIMIT = 56 * 1024 * 1024
TOK_TILE = 256
WKV_CHUNK = 32

MESH = pl.DeviceIdType.MESH


def _dot_bf(a, b):
    return jnp.dot(a.astype(BF16), b.astype(BF16), preferred_element_type=F32)


def _dot_nt_bf(a, b):
    return lax.dot_general(a.astype(BF16), b.astype(BF16), (((1,), (1,)), ((), ())), preferred_element_type=F32)


def _dot_tn_bf(a, b):
    return lax.dot_general(a.astype(BF16), b.astype(BF16), (((0,), (0,)), ((), ())), preferred_element_type=F32)


@jax.custom_vjp
def _mm(a, b):
    return _dot_bf(a, b)


@jax.custom_vjp
def _mm_nt(a, b):
    return _dot_nt_bf(a, b)


@jax.custom_vjp
def _mm_tn(a, b):
    return _dot_tn_bf(a, b)


_mm.defvjp(lambda a, b: (_dot_bf(a, b), (a, b)), lambda res, g: (_dot_nt_bf(g, res[1]), _dot_tn_bf(res[0], g)))
_mm_nt.defvjp(lambda a, b: (_dot_nt_bf(a, b), (a, b)), lambda res, g: (_dot_bf(g, res[1]), _dot_tn_bf(g, res[0])))
_mm_tn.defvjp(lambda a, b: (_dot_tn_bf(a, b), (a, b)), lambda res, g: (_dot_nt_bf(res[1], g), _dot_bf(res[0], g)))


def _trunc(x):
    return lax.bitcast_convert_type(lax.bitcast_convert_type(x, jnp.uint32) & jnp.uint32(0xFFFF0000), F32)


def _two_piece(x):
    hi = _trunc(x)
    return jnp.concatenate([hi, x - hi], axis=1)


def _mix_raw(x, mat2):
    return jnp.dot(_two_piece(x), mat2, preferred_element_type=F32)


@jax.custom_vjp
def _head_mix(x, mat2):
    return _mix_raw(x, mat2)


_head_mix.defvjp(lambda x, mat2: (_mix_raw(x, mat2), mat2), lambda mat2, g: (_mix_raw(g, mat2), jnp.zeros_like(mat2)))


def _swap_halves(x):
    lane = lax.broadcasted_iota(jnp.int32, x.shape, 1)
    return jnp.where((lane & (RET_DK - 1)) < RET_DK // 2, pltpu.roll(x, RET_QK - RET_DK // 2, axis=1),
                     pltpu.roll(x, RET_DK // 2, axis=1))


@jax.custom_vjp
def _rot(x):
    return _swap_halves(x)


_rot.defvjp(lambda x: (_swap_halves(x), None), lambda _, g: (_swap_halves(g),))


def _params(**kw):
    return pltpu.CompilerParams(vmem_limit_bytes=VMEM_LIMIT, **kw)


def _full(shape):
    nd = len(shape)
    return pl.BlockSpec(shape, lambda i, _nd=nd: (0,) * _nd)


def _rows(tile, width):
    return pl.BlockSpec((tile, width), lambda i: (i, 0))


def _block_mix(n, blk, scale=1.0):
    idx = np.arange(n) // blk
    m = (idx[:, None] == idx[None, :]).astype(np.float32) * scale
    return jnp.asarray(np.concatenate([m, m], axis=0))


def _rope_tables(T):
    half = RET_DK // 2
    expo = -np.arange(half, dtype=np.float32) / np.float32(half)
    freqs = np.exp(expo * np.float32(np.log(ROPE_BASE))).astype(np.float32)
    ang = np.arange(T, dtype=np.float32)[:, None] * freqs[None, :]
    cos, sin = np.cos(ang).astype(np.float32), np.sin(ang).astype(np.float32)
    cos_h = np.concatenate([cos, cos], axis=1)
    sin_h = np.concatenate([-sin, sin], axis=1)
    cos_t = np.tile(cos_h, (1, RET_HEADS))
    sin_t = np.tile(sin_h, (1, RET_HEADS))
    return jnp.asarray(cos_t), jnp.asarray(sin_t)


def _ret_tables():
    h = np.arange(RET_HEADS, dtype=np.float32)
    lg = np.log(1.0 - np.exp2(-5.0 - h)).astype(np.float32)
    idx = np.arange(CHUNK, dtype=np.float32)
    intra = np.exp(lg[:, None, None] * np.abs(idx[:, None] - idx[None, :])).astype(np.float32)
    q_dec = np.exp(lg[:, None] * (idx[None, :] + 1.0)).astype(np.float32)
    k_dec = np.exp(lg[:, None] * (CHUNK - 1.0 - idx[None, :])).astype(np.float32)
    chunk_dec = np.exp(lg * CHUNK).astype(np.float32)
    lane_head = np.arange(RET_QK) // RET_DK
    mask = (lane_head[None, :] == np.arange(RET_HEADS)[:, None]).astype(np.float32)
    m = np.broadcast_to(mask[:, None, :], (RET_HEADS, CHUNK, RET_QK)).copy()
    qd = m * q_dec[:, :, None]
    kd = m * k_dec[:, :, None]
    return jnp.asarray(intra), jnp.asarray(m), jnp.asarray(qd), jnp.asarray(kd), [float(c) for c in chunk_dec]


def _rmsnorm(x, g):
    return x * lax.rsqrt(jnp.mean(x * x, axis=-1, keepdims=True) + RMS_EPS) * g


def _inproj(x, norm_g, w_ret, w_rwkv):
    T = x.shape[0]

    def body(x_ref, g_ref, wr_ref, ww_ref, pr_ref, pw_ref, u_ref):
        ub = _rmsnorm(x_ref[...], g_ref[...]).astype(BF16)
        u_ref[...] = ub
        pr_ref[...] = jnp.dot(ub, wr_ref[...], preferred_element_type=F32)
        pw_ref[...] = jnp.dot(ub, ww_ref[...], preferred_element_type=F32)

    return pl.pallas_call(
        body, name="inproj", grid=(T // TOK_TILE,),
        in_specs=[_rows(TOK_TILE, D_MODEL), _full((1, D_MODEL)), _full((D_MODEL, RET_COLS)), _full((D_MODEL, RWKV_COLS))],
        out_specs=[_rows(TOK_TILE, RET_COLS), _rows(TOK_TILE, RWKV_COLS), _rows(TOK_TILE, D_MODEL)],
        out_shape=[jax.ShapeDtypeStruct((T, RET_COLS), F32), jax.ShapeDtypeStruct((T, RWKV_COLS), F32),
                   jax.ShapeDtypeStruct((T, D_MODEL), BF16)],
        compiler_params=_params(dimension_semantics=("arbitrary",)),
    )(x, norm_g, w_ret, w_rwkv)


def _ret_chunk(pq, pk, v_heads, s_heads, cos_t, sin_t, dec, hm, qd, kd, chunk_dec):
    q = pq * cos_t + _rot(pq) * sin_t
    k = (pk * cos_t + _rot(pk) * sin_t) * (RET_DK ** -0.5)
    outs, s_out = [], []
    for h in range(RET_HEADS):
        sc = _mm_nt(q * hm[h], k * hm[h]) * dec[h]
        intra = _mm(sc, v_heads[h])
        kv = _mm_tn(k * kd[h], v_heads[h])
        inter = _mm(q * qd[h], s_heads[h])
        outs.append(intra + inter)
        s_out.append(s_heads[h] * chunk_dec[h] + kv)
    return tuple(outs), tuple(s_out)


def _ret_specs():
    const = [_full((RET_HEADS, CHUNK, CHUNK)), _full((RET_HEADS, CHUNK, RET_QK)),
             _full((RET_HEADS, CHUNK, RET_QK)), _full((RET_HEADS, CHUNK, RET_QK))]
    return const


def _ret_fwd(p_ret, tabs):
    T = p_ret.shape[0]
    nc = T // CHUNK
    cos_t, sin_t, dec, hm, qd, kd, chunk_dec = tabs

    def body(p_ref, cos_ref, sin_ref, dec_ref, hm_ref, qd_ref, kd_ref, out_ref, sin_save_ref, s_scr):
        @pl.when(pl.program_id(0) == 0)
        def _():
            s_scr[...] = jnp.zeros_like(s_scr)

        s_heads = tuple(s_scr[h] for h in range(RET_HEADS))
        for h in range(RET_HEADS):
            sin_save_ref[0, h] = s_heads[h]
        pq = p_ref[:, 0:RET_QK]
        pk = p_ref[:, RET_QK:2 * RET_QK]
        v_heads = tuple(p_ref[:, 2 * RET_QK + RET_DV * h:2 * RET_QK + RET_DV * (h + 1)] for h in range(RET_HEADS))
        outs, s_out = _ret_chunk(pq, pk, v_heads, s_heads, cos_ref[...], sin_ref[...], dec_ref[...],
                                 hm_ref[...], qd_ref[...], kd_ref[...], chunk_dec)
        for h in range(RET_HEADS):
            out_ref[:, RET_DV * h:RET_DV * (h + 1)] = outs[h]
            s_scr[h] = s_out[h]

    return pl.pallas_call(
        body, name="ret_fwd", grid=(nc,),
        in_specs=[pl.BlockSpec((CHUNK, RET_COLS), lambda i: (i, 0)), _rows(CHUNK, RET_QK), _rows(CHUNK, RET_QK)] + _ret_specs(),
        out_specs=[_rows(CHUNK, RET_WIDTH), pl.BlockSpec((1, RET_HEADS, RET_QK, RET_DV), lambda i: (i, 0, 0, 0))],
        out_shape=[jax.ShapeDtypeStruct((T, RET_WIDTH), F32), jax.ShapeDtypeStruct((nc, RET_HEADS, RET_QK, RET_DV), F32)],
        scratch_shapes=[pltpu.VMEM((RET_HEADS, RET_QK, RET_DV), F32)],
        compiler_params=_params(dimension_semantics=("arbitrary",)),
    )(p_ret, cos_t, sin_t, dec, hm, qd, kd)


def _ret_bwd(p_ret, s_saved, d_ret, tabs):
    T = p_ret.shape[0]
    nc = T // CHUNK
    cos_t, sin_t, dec, hm, qd, kd, chunk_dec = tabs

    def body(p_ref, s_ref, dret_ref, cos_ref, sin_ref, dec_ref, hm_ref, qd_ref, kd_ref, dp_ref, ds_scr):
        @pl.when(pl.program_id(0) == 0)
        def _():
            ds_scr[...] = jnp.zeros_like(ds_scr)

        pq = p_ref[:, 0:RET_QK]
        pk = p_ref[:, RET_QK:2 * RET_QK]
        v_heads = tuple(p_ref[:, 2 * RET_QK + RET_DV * h:2 * RET_QK + RET_DV * (h + 1)] for h in range(RET_HEADS))
        s_heads = tuple(s_ref[0, h] for h in range(RET_HEADS))
        consts = (cos_ref[...], sin_ref[...], dec_ref[...], hm_ref[...], qd_ref[...], kd_ref[...])
        _, vjp = jax.vjp(lambda a, b, c, d: _ret_chunk(a, b, c, d, *consts, chunk_dec), pq, pk, v_heads, s_heads)
        d_out = tuple(dret_ref[:, RET_DV * h:RET_DV * (h + 1)] for h in range(RET_HEADS))
        d_s = tuple(ds_scr[h] for h in range(RET_HEADS))
        dq, dk, dv, ds_in = vjp((d_out, d_s))
        dp_ref[:, 0:RET_QK] = dq
        dp_ref[:, RET_QK:2 * RET_QK] = dk
        for h in range(RET_HEADS):
            dp_ref[:, 2 * RET_QK + RET_DV * h:2 * RET_QK + RET_DV * (h + 1)] = dv[h]
            ds_scr[h] = ds_in[h]

    rev = lambda i: (nc - 1 - i, 0)
    return pl.pallas_call(
        body, name="ret_bwd", grid=(nc,),
        in_specs=[pl.BlockSpec((CHUNK, RET_COLS), rev),
                  pl.BlockSpec((1, RET_HEADS, RET_QK, RET_DV), lambda i: (nc - 1 - i, 0, 0, 0)),
                  pl.BlockSpec((CHUNK, RET_WIDTH), rev), pl.BlockSpec((CHUNK, RET_QK), rev), pl.BlockSpec((CHUNK, RET_QK), rev)]
        + _ret_specs(),
        out_specs=pl.BlockSpec((CHUNK, 2 * RET_QK + RET_WIDTH), rev),
        out_shape=jax.ShapeDtypeStruct((T, 2 * RET_QK + RET_WIDTH), F32),
        scratch_shapes=[pltpu.VMEM((RET_HEADS, RET_QK, RET_DV), F32)],
        compiler_params=_params(dimension_semantics=("arbitrary",)),
    )(p_ret, s_saved, d_ret, cos_t, sin_t, dec, hm, qd, kd)


def _wkv_consts():
    lane = lax.broadcasted_iota(jnp.int32, (RWKV_HEAD, RWKV_WIDTH), 1)
    sub = lax.broadcasted_iota(jnp.int32, (RWKV_HEAD, RWKV_WIDTH), 0)
    diag = ((lane & (RWKV_HEAD - 1)) == sub).astype(F32)
    r = lax.broadcasted_iota(jnp.int32, (3 * 128, 128), 0)
    c = lax.broadcasted_iota(jnp.int32, (3 * 128, 128), 1)
    ones = (((r & 127) >> 6) == (c >> 6)).astype(BF16)
    return diag, ones


def _stack(x):
    return jnp.concatenate([x[:, 128 * p:128 * (p + 1)] for p in range(4)], axis=0)


def _unstack(y):
    n = y.shape[0] // 4
    return jnp.concatenate([y[n * p:n * (p + 1)] for p in range(4)], axis=1)


def _split(x, n):
    pieces = []
    for _ in range(n):
        p = x.astype(BF16)
        pieces.append(p)
        x = x - p.astype(F32)
    return pieces


def _lane_sum(x, ones):
    return _unstack(jnp.dot(_two_piece(_stack(x)), ones[:256].astype(F32), preferred_element_type=F32))


def _expand(row, diag_bf, ones, n):
    lhs = jnp.concatenate([_stack(jnp.broadcast_to(p, diag_bf.shape) * diag_bf) for p in _split(row, n)], axis=1)
    return _unstack(jnp.dot(lhs, ones[:128 * n], preferred_element_type=F32))


def _colsum(x):
    return jnp.sum(x, axis=0, keepdims=True)


def _head_sums(x, ones):
    return _unstack(jnp.dot(jnp.concatenate(_split(_stack(x), 3), axis=1), ones, preferred_element_type=F32))


def _wkv_fwd(r, w, k, v, kap, a):
    T = r.shape[0]
    C = WKV_CHUNK
    nc = T // C

    def body(r_ref, w_ref, k_ref, v_ref, kap_ref, a_ref, o_ref, s_all_ref, s_scr, vf_scr):
        @pl.when(pl.program_id(0) == 0)
        def _():
            s_scr[...] = jnp.zeros_like(s_scr)

        diag, ones = _wkv_consts()
        diag_bf = diag.astype(BF16)
        rr, ww, kk, vv, kap_, aa = (ref[...] for ref in (r_ref, w_ref, k_ref, v_ref, kap_ref, a_ref))
        bb = kap_ * aa
        z = pltpu.roll(ww, 1, axis=0) * kap_
        c1 = _head_sums(pltpu.roll(bb, 1, axis=0) * kap_, ones)
        c2 = _head_sums(pltpu.roll(kk, 1, axis=0) * kap_, ones)
        row = lambda x, t: x[t:t + 1]

        for t in range(C):
            vf_scr[t] = _expand(row(vv, t), diag_bf, ones, 2)

        s_prev = s_scr[...]
        sa = _lane_sum(s_prev * (-row(kap_, 0)), ones)
        ls_next = _lane_sum(s_prev * (-row(z, 1)), ones)
        ls_now, vf, rows = None, None, []
        for t in range(C):
            if t > 0:
                sa = ls_now - sa * row(c1, t) - vf * row(c2, t)
            vf = vf_scr[t]
            s_prev = s_prev * row(ww, t) + sa * row(bb, t) + vf * row(kk, t)
            ls_now = ls_next
            if t + 2 < C:
                ls_next = _lane_sum(s_prev * (-row(z, t + 2)), ones)
            s_all_ref[t] = s_prev
            rows.append(_colsum(_lane_sum(s_prev * row(rr, t), ones) * diag))
            if t % 8 == 7:
                o_ref[t - 7:t + 1, :] = jnp.concatenate(rows, axis=0)
                rows = []
        s_scr[...] = s_prev

    spec = _rows(C, RWKV_WIDTH)
    return pl.pallas_call(
        body, name="wkv_fwd", grid=(nc,),
        in_specs=[spec] * 6,
        out_specs=[spec, pl.BlockSpec((C, RWKV_HEAD, RWKV_WIDTH), lambda i: (i, 0, 0))],
        out_shape=[jax.ShapeDtypeStruct((T, RWKV_WIDTH), F32), jax.ShapeDtypeStruct((T, RWKV_HEAD, RWKV_WIDTH), F32)],
        scratch_shapes=[pltpu.VMEM((RWKV_HEAD, RWKV_WIDTH), F32), pltpu.VMEM((C, RWKV_HEAD, RWKV_WIDTH), F32)],
        compiler_params=_params(dimension_semantics=("arbitrary",)),
    )(r, w, k, v, kap, a)


def _wkv_bwd(r, w, k, v, kap, a, s_all, d_o):
    T = r.shape[0]
    C = WKV_CHUNK
    nc = T // C

    def body(r_ref, w_ref, k_ref, v_ref, kap_ref, a_ref, s_ref, s_before_ref, do_ref,
             dr_ref, dw_ref, dk_ref, dv_ref, dkap_ref, da_ref, ds_scr, vf_scr, dof_scr, sa_scr):
        first_chunk = pl.program_id(0) == nc - 1

        @pl.when(pl.program_id(0) == 0)
        def _():
            ds_scr[...] = jnp.zeros_like(ds_scr)

        diag, ones = _wkv_consts()
        diag_bf = diag.astype(BF16)
        rr, ww, kk, vv, kap_, aa, dd = (ref[...] for ref in (r_ref, w_ref, k_ref, v_ref, kap_ref, a_ref, do_ref))
        bb = kap_ * aa
        zb = pltpu.roll(ww, C - 1, axis=0) * bb
        e1 = _head_sums(pltpu.roll(kap_, C - 1, axis=0) * bb, ones)
        e2 = _head_sums(rr * bb, ones)
        row = lambda x, t: x[t:t + 1]

        def state_before(t):
            return s_ref[t - 1] if t > 0 else jnp.where(first_chunk, 0.0, s_before_ref[0])

        for t in range(C):
            vf_scr[t] = _expand(row(vv, t), diag_bf, ones, 2)
            dof_scr[t] = _expand(row(dd, t), diag_bf, ones, 2)
            sa_scr[t] = _lane_sum(state_before(t) * (-row(kap_, t)), ones)

        d_s = ds_scr[...]
        ls_now, dsa, rows = None, None, [None] * C
        for t in reversed(range(C)):
            dof = dof_scr[t]
            d_sn = d_s + dof * row(rr, t)
            if t == C - 1:
                dsa = _lane_sum(d_sn * row(bb, t), ones)
            else:
                dsa = ls_now - dsa * row(e1, t) + dof * row(e2, t)
            if t > 0:
                ls_now = _lane_sum(d_sn * row(zb, t - 1), ones)
            s_prev = state_before(t)
            dv = _colsum(_lane_sum(d_sn * row(kk, t), ones) * diag)
            db = _colsum(d_sn * sa_scr[t])
            rows[t] = (_colsum(s_ref[t] * dof), _colsum(d_sn * s_prev), _colsum(d_sn * vf_scr[t]), dv,
                       db * row(aa, t) - _colsum(dsa * s_prev), db * row(kap_, t))
            d_s = d_sn * row(ww, t) - dsa * row(kap_, t)
            if t % 8 == 0:
                for j, ref in enumerate((dr_ref, dw_ref, dk_ref, dv_ref, dkap_ref, da_ref)):
                    ref[t:t + 8, :] = jnp.concatenate([rows[u][j] for u in range(t, t + 8)], axis=0)
        ds_scr[...] = d_s

    spec = pl.BlockSpec((C, RWKV_WIDTH), lambda i: (nc - 1 - i, 0))
    states = pl.BlockSpec((C, RWKV_HEAD, RWKV_WIDTH), lambda i: (nc - 1 - i, 0, 0))
    before = pl.BlockSpec((1, RWKV_HEAD, RWKV_WIDTH), lambda i: (jnp.maximum((nc - 1 - i) * C - 1, 0), 0, 0))
    big = pltpu.VMEM((C, RWKV_HEAD, RWKV_WIDTH), F32)
    return pl.pallas_call(
        body, name="wkv_bwd", grid=(nc,),
        in_specs=[spec] * 6 + [states, before, spec],
        out_specs=[spec] * 6,
        out_shape=[jax.ShapeDtypeStruct((T, RWKV_WIDTH), F32)] * 6,
        scratch_shapes=[pltpu.VMEM((RWKV_HEAD, RWKV_WIDTH), F32), big, big, big],
        compiler_params=_params(dimension_semantics=("arbitrary",)),
    )(r, w, k, v, kap, a, s_all, s_all, d_o)


W = RWKV_WIDTH


def _softplus(y):
    return jnp.maximum(y, 0.0) + jnp.log(1.0 + jnp.exp(-jnp.abs(y)))


def _prep_fn(kr, xwa, w0, a0, k_k, k_a, wup_pad, aup_pad, ones64):
    w_log = -_softplus(-(w0 + _mm(jnp.tanh(xwa), wup_pad))) - 0.5
    decay = jnp.exp(-jnp.exp(w_log))
    a = jax.nn.sigmoid(a0 + _mm(xwa, aup_pad))
    kk = kr * k_k
    kap = kk / jnp.maximum(jnp.sqrt(_head_mix(kk * kk, ones64)), 1e-12)
    k = kr * (1.0 + (a - 1.0) * k_a)
    return decay, k, kap, a


def _shift_down(p, first_row):
    rows = lax.broadcasted_iota(jnp.int32, p.shape, 0)
    return jnp.where(rows == 0, first_row, pltpu.roll(p, 1, axis=0))


def _shift_up(z, last_row):
    n = z.shape[0]
    rows = lax.broadcasted_iota(jnp.int32, z.shape, 0)
    return jnp.where(rows == n - 1, last_row, pltpu.roll(z, n - 1, axis=0))


def _prev_block_spec():
    return pl.BlockSpec((8, RWKV_COLS), lambda i: (jnp.maximum(i * (TOK_TILE // 8) - 1, 0), 0))


def _mixed(p_ref, prev8_ref, mu_ref, first_tile):
    p = p_ref[...]
    first_row = jnp.where(first_tile, 0.0, prev8_ref[7:8, :])
    prev = _shift_down(p, first_row)
    return p, prev, p + mu_ref[...] * (prev - p)


def _prep_fwd(p_rwkv, mu, w0, a0, k_k, k_a, wup_pad, aup_pad, ones64):
    T = p_rwkv.shape[0]

    def body(p_ref, prev8_ref, mu_ref, w0_ref, a0_ref, kk_ref, ka_ref, wup_ref, aup_ref, ones_ref,
             r_ref, w_ref, k_ref, v_ref, kap_ref, a_ref, g_ref):
        _, _, ps = _mixed(p_ref, prev8_ref, mu_ref, pl.program_id(0) == 0)
        decay, k, kap, a = _prep_fn(ps[:, W:2 * W], ps[:, 4 * W:], w0_ref[...], a0_ref[...], kk_ref[...], ka_ref[...],
                                    wup_ref[...], aup_ref[...], ones_ref[...])
        r_ref[...] = ps[:, 0:W]
        w_ref[...] = decay
        k_ref[...] = k
        v_ref[...] = ps[:, 2 * W:3 * W]
        kap_ref[...] = kap
        a_ref[...] = a
        g_ref[...] = ps[:, 3 * W:4 * W]

    vec = _full((1, W))
    return pl.pallas_call(
        body, name="prep_fwd", grid=(T // TOK_TILE,),
        in_specs=[_rows(TOK_TILE, RWKV_COLS), _prev_block_spec(), _full((1, RWKV_COLS)), vec, vec, vec, vec,
                  _full((2 * LORA, W)), _full((2 * LORA, W)), _full((2 * W, W))],
        out_specs=[_rows(TOK_TILE, W)] * 7,
        out_shape=[jax.ShapeDtypeStruct((T, W), F32)] * 7,
        compiler_params=_params(dimension_semantics=("arbitrary",)),
    )(p_rwkv, p_rwkv, mu, w0, a0, k_k, k_a, wup_pad, aup_pad, ones64)


def _prep_bwd(p_rwkv, mu, w0, a0, k_k, k_a, wup_pad, aup_pad, ones64, dr, dw, dk, dv, dkap, da, dg, dr2, dk2, dv2):
    T = p_rwkv.shape[0]
    nt = T // TOK_TILE

    def body(p_ref, prev8_ref, mu_ref, w0_ref, a0_ref, kk_ref, ka_ref, wup_ref, aup_ref, ones_ref,
             dr_ref, dw_ref, dk_ref, dv_ref, dkap_ref, da_ref, dg_ref, dr2_ref, dk2_ref, dv2_ref,
             dp_ref, dmu_ref, dw0_ref, da0_ref, dkk_ref, dka_ref, dwup_ref, daup_ref, zrow_scr):
        i = pl.program_id(0)
        accs = (dmu_ref, dw0_ref, da0_ref, dkk_ref, dka_ref, dwup_ref, daup_ref)

        @pl.when(i == 0)
        def _():
            zrow_scr[...] = jnp.zeros_like(zrow_scr)
            for ref in accs:
                ref[...] = jnp.zeros_like(ref)

        p, prev, ps = _mixed(p_ref, prev8_ref, mu_ref, i == nt - 1)
        ones = ones_ref[...]
        _, vjp = jax.vjp(lambda *args: _prep_fn(*args, ones), ps[:, W:2 * W], ps[:, 4 * W:], w0_ref[...], a0_ref[...],
                         kk_ref[...], ka_ref[...], wup_ref[...], aup_ref[...])
        dkr, dxwa, dw0, da0, dkk, dka, dwup, daup = vjp(
            (dw_ref[...], dk_ref[...] + dk2_ref[...], dkap_ref[...], da_ref[...]))
        dps = jnp.concatenate([dr_ref[...] + dr2_ref[...], dkr, dv_ref[...] + dv2_ref[...], dg_ref[...], dxwa], axis=1)
        z = dps * mu_ref[...]
        dp_ref[...] = dps - z + _shift_up(z, zrow_scr[0:1, :])
        zrow_scr[0:1, :] = z[0:1, :]
        for ref, val in zip(accs, (_colsum(dps * (prev - p)), dw0, da0, dkk, dka, dwup, daup)):
            ref[...] += val

    rev = lambda i: (nt - 1 - i, 0)
    vec = _full((1, W))
    lora = _full((2 * LORA, W))
    tile = pl.BlockSpec((TOK_TILE, W), rev)
    prev8 = pl.BlockSpec((8, RWKV_COLS), lambda i: (jnp.maximum((nt - 1 - i) * (TOK_TILE // 8) - 1, 0), 0))
    return pl.pallas_call(
        body, name="prep_bwd", grid=(nt,),
        in_specs=[pl.BlockSpec((TOK_TILE, RWKV_COLS), rev), prev8, _full((1, RWKV_COLS)), vec, vec, vec, vec, lora, lora,
                  _full((2 * W, W))] + [tile] * 10,
        out_specs=[pl.BlockSpec((TOK_TILE, RWKV_COLS), rev), _full((1, RWKV_COLS)), vec, vec, vec, vec, lora, lora],
        out_shape=[jax.ShapeDtypeStruct((T, RWKV_COLS), F32), jax.ShapeDtypeStruct((1, RWKV_COLS), F32)]
        + [jax.ShapeDtypeStruct((1, W), F32)] * 4 + [jax.ShapeDtypeStruct((2 * LORA, W), F32)] * 2,
        scratch_shapes=[pltpu.VMEM((8, RWKV_COLS), F32)],
        compiler_params=_params(dimension_semantics=("arbitrary",)),
    )(p_rwkv, p_rwkv, mu, w0, a0, k_k, k_a, wup_pad, aup_pad, ones64, dr, dw, dk, dv, dkap, da, dg, dr2, dk2, dv2)


def _silu(x):
    return x * jax.nn.sigmoid(x)


def _post_y(o, r, k, v, g_rw, ret_raw, g_ret, ret_gn_g, gn_g, gn_b, r_k, avg128, avg64, ones64):
    xc = ret_raw - _head_mix(ret_raw, avg128)
    ret = xc * lax.rsqrt(_head_mix(xc * xc, avg128) + RET_GN_EPS)
    y_ret = _silu(g_ret) * (ret * ret_gn_g)
    oc = o - _head_mix(o, avg64)
    on = oc * lax.rsqrt(_head_mix(oc * oc, avg64) + RWKV_GN_EPS) * gn_g + gn_b
    bonus = _head_mix(r * k * r_k, ones64) * v
    y_rwkv = _silu(g_rw) * (on + bonus)
    return y_ret, y_rwkv


def _post_loss(h, final_g, target):
    err = _rmsnorm(h, final_g) - target
    return 0.5 * jnp.sum(jnp.mean(err * err, axis=-1))


def _post(o, r, k, v, g_rw, ret_raw, p_ret, x, target, ret_gn_g, gn_g, gn_b, r_k, final_g, w_out, avg128, avg64, ones64):
    T = x.shape[0]
    n_tok_out = 8

    def body(o_ref, r_ref, k_ref, v_ref, grw_ref, ret_ref, gret_ref, x_ref, tgt_ref, rg_ref, gg_ref, gb_ref, rk_ref, fg_ref,
             wo_ref, a128_ref, a64_ref, ones_ref, *outs):
        tok_outs, (dwo_ref, drg_ref, dgg_ref, dgb_ref, drk_ref, dfg_ref, loss_ref) = outs[:n_tok_out], outs[n_tok_out:]
        accs = (dwo_ref, drg_ref, dgg_ref, dgb_ref, drk_ref, dfg_ref, loss_ref)

        @pl.when(pl.program_id(0) == 0)
        def _():
            for ref in accs:
                ref[...] = jnp.zeros_like(ref)

        consts = (a128_ref[...], a64_ref[...], ones_ref[...])
        (y_ret, y_rwkv), vjp = jax.vjp(
            lambda *args: _post_y(*args, *consts), o_ref[...], r_ref[...], k_ref[...], v_ref[...], grw_ref[...], ret_ref[...],
            gret_ref[...], rg_ref[...], gg_ref[...], gb_ref[...], rk_ref[...])
        h = x_ref[...] + _dot_bf(y_ret, wo_ref[0:RET_WIDTH, :]) + _dot_bf(y_rwkv, wo_ref[RET_WIDTH:, :])
        loss, (dh, dfg) = jax.value_and_grad(_post_loss, argnums=(0, 1))(h, fg_ref[...], tgt_ref[...])
        dy_ret = _dot_nt_bf(dh, wo_ref[0:RET_WIDTH, :])
        dy_rwkv = _dot_nt_bf(dh, wo_ref[RET_WIDTH:, :])
        do, dr, dk, dv, dgrw, dret, dgret, drg, dgg, dgb, drk = vjp((dy_ret, dy_rwkv))
        for ref, val in zip(tok_outs, (dh, do, dr, dk, dv, dgrw, dret, dgret)):
            ref[...] = val
        dwo_ref[0:RET_WIDTH, :] += _dot_tn_bf(y_ret, dh)
        dwo_ref[RET_WIDTH:, :] += _dot_tn_bf(y_rwkv, dh)
        for ref, val in zip(accs[1:], (drg, dgg, dgb, drk, dfg, jnp.full((1, 128), loss, F32))):
            ref[...] += val

    tile = _rows(TOK_TILE, W)
    wide = _rows(TOK_TILE, D_MODEL)
    vec = _full((1, W))
    sq = _full((2 * W, W))
    return pl.pallas_call(
        body, name="post", grid=(T // TOK_TILE,),
        in_specs=[tile] * 6 + [pl.BlockSpec((TOK_TILE, W), lambda i: (i, 2)), wide, wide, vec, vec, vec, vec,
                               _full((1, D_MODEL)), _full((D_MODEL, D_MODEL)), sq, sq, sq],
        out_specs=[wide] + [tile] * 7 + [_full((D_MODEL, D_MODEL)), vec, vec, vec, vec, _full((1, D_MODEL)), _full((1, 128))],
        out_shape=[jax.ShapeDtypeStruct((T, D_MODEL), F32)] + [jax.ShapeDtypeStruct((T, W), F32)] * 7
        + [jax.ShapeDtypeStruct((D_MODEL, D_MODEL), F32)] + [jax.ShapeDtypeStruct((1, W), F32)] * 4
        + [jax.ShapeDtypeStruct((1, D_MODEL), F32), jax.ShapeDtypeStruct((1, 128), F32)],
        compiler_params=_params(dimension_semantics=("arbitrary",)),
    )(o, r, k, v, g_rw, ret_raw, p_ret, x, target, ret_gn_g, gn_g, gn_b, r_k, final_g, w_out, avg128, avg64, ones64)


def _inproj_bwd_x(x, norm_g, dp_qkv, dg_ret, dp_rwkv, dh, w_ret, w_rwkv):
    T = x.shape[0]
    n_qkv = 2 * RET_QK + RET_WIDTH

    def body(x_ref, g_ref, dqkv_ref, dgret_ref, drwkv_ref, dh_ref, wr_ref, ww_ref, dx_ref, dg_ref):
        @pl.when(pl.program_id(0) == 0)
        def _():
            dg_ref[...] = jnp.zeros_like(dg_ref)

        _, vjp = jax.vjp(_rmsnorm, x_ref[...], g_ref[...])
        du = (_dot_nt_bf(dqkv_ref[...], wr_ref[:, 0:n_qkv]) + _dot_nt_bf(dgret_ref[...], wr_ref[:, n_qkv:])
              + _dot_nt_bf(drwkv_ref[...], ww_ref[...]))
        dx, dg = vjp(du)
        dx_ref[...] = dx + dh_ref[...]
        dg_ref[...] += dg

    return pl.pallas_call(
        body, name="inproj_bwd_x", grid=(T // TOK_TILE,),
        in_specs=[_rows(TOK_TILE, D_MODEL), _full((1, D_MODEL)), _rows(TOK_TILE, n_qkv), _rows(TOK_TILE, RET_WIDTH),
                  _rows(TOK_TILE, RWKV_COLS), _rows(TOK_TILE, D_MODEL), _full((D_MODEL, RET_COLS)), _full((D_MODEL, RWKV_COLS))],
        out_specs=[_rows(TOK_TILE, D_MODEL), _full((1, D_MODEL))],
        out_shape=[jax.ShapeDtypeStruct((T, D_MODEL), F32), jax.ShapeDtypeStruct((1, D_MODEL), F32)],
        compiler_params=_params(dimension_semantics=("arbitrary",)),
    )(x, norm_g, dp_qkv, dg_ret, dp_rwkv, dh, w_ret, w_rwkv)


def _grad_w(name, u, dp):
    T, n = dp.shape
    tile = 2 * TOK_TILE

    def body(u_ref, dp_ref, out_ref):
        @pl.when(pl.program_id(0) == 0)
        def _():
            out_ref[...] = jnp.zeros_like(out_ref)

        out_ref[...] += _dot_tn_bf(u_ref[...], dp_ref[...])

    return pl.pallas_call(
        body, name=name, grid=(T // tile,),
        in_specs=[_rows(tile, D_MODEL), _rows(tile, n)],
        out_specs=_full((D_MODEL, n)),
        out_shape=jax.ShapeDtypeStruct((D_MODEL, n), F32),
        compiler_params=_params(dimension_semantics=("arbitrary",)),
    )(u, dp)


def _pad_lora(w_up, top):
    z = jnp.zeros_like(w_up)
    return jnp.concatenate([w_up, z] if top else [z, w_up], axis=0)


def _local_grads(x, target, norm_g, w_in_bf, ret_gn_g, mu, w_lora_up, w0, a_lora_up, a0, k_k, k_a, r_k, gn_g, gn_b,
                 w_out_bf, final_g):
    T = x.shape[0]
    tabs = _rope_tables(T) + _ret_tables()
    ones64 = _block_mix(W, RWKV_HEAD)
    avg64 = _block_mix(W, RWKV_HEAD, 1.0 / RWKV_HEAD)
    avg128 = _block_mix(RET_WIDTH, RET_DV, 1.0 / RET_DV)
    w_ret, w_rwkv = w_in_bf[:, :RET_COLS], w_in_bf[:, RET_COLS:]
    wup_pad, aup_pad = _pad_lora(w_lora_up, True), _pad_lora(a_lora_up, False)

    p_ret, p_rwkv, u = _inproj(x, norm_g, w_ret, w_rwkv)
    ret_raw, s_saved = _ret_fwd(p_ret, tabs)
    r, w, k, v, kap, a, g_rw = _prep_fwd(p_rwkv, mu, w0, a0, k_k, k_a, wup_pad, aup_pad, ones64)
    o, s_all = _wkv_fwd(r, w, k, v, kap, a)
    (dh, do, dr2, dk2, dv2, dgrw, dret, dgret, d_w_out, d_ret_gn_g, d_gn_g, d_gn_b, d_r_k, d_final_g, loss) = _post(
        o, r, k, v, g_rw, ret_raw, p_ret, x, target, ret_gn_g, gn_g, gn_b, r_k, final_g, w_out_bf, avg128, avg64, ones64)
    dr, dw, dk, dv, dkap, da = _wkv_bwd(r, w, k, v, kap, a, s_all, do)
    dp_rwkv, d_mu, d_w0, d_a0, d_k_k, d_k_a, d_wup, d_aup = _prep_bwd(
        p_rwkv, mu, w0, a0, k_k, k_a, wup_pad, aup_pad, ones64, dr, dw, dk, dv, dkap, da, dgrw, dr2, dk2, dv2)
    dp_qkv = _ret_bwd(p_ret, s_saved, dret, tabs)
    dx, d_norm_g = _inproj_bwd_x(x, norm_g, dp_qkv, dgret, dp_rwkv, dh, w_ret, w_rwkv)
    d_w_in = jnp.concatenate([_grad_w("grad_w_qkv", u, dp_qkv), _grad_w("grad_w_gret", u, dgret),
                              _grad_w("grad_w_rwkv", u, dp_rwkv)], axis=1)
    grads = dict(norm_g=d_norm_g, w_in=d_w_in, ret_gn_g=d_ret_gn_g, rwkv_mu=d_mu, w_lora_up=d_wup[:LORA], w0=d_w0,
                 a_lora_up=d_aup[LORA:], a0=d_a0, k_k=d_k_k, k_a=d_k_a, r_k=d_r_k, rwkv_gn_g=d_gn_g, rwkv_gn_b=d_gn_b,
                 w_out=d_w_out, final_norm_g=d_final_g)
    return loss, dx, grads


def _mesh_pos():
    return lax.axis_index("x"), lax.axis_index("y"), lax.axis_index("c")


def _all_gather(shards):
    n = len(shards)

    def body(*refs):
        x_refs, out_refs = refs[:n], refs[n:2 * n]
        send_sems, recv_sems, local_sems = refs[2 * n:]
        x, y, c = _mesh_pos()
        me, sibling = (x, y, c), (x, y, 1 - c)
        chips = [(1 - x, y), (x, 1 - y), (1 - x, 1 - y)]

        def rows(a, pos):
            m = x_refs[a].shape[0]
            return out_refs[a].at[pl.ds((4 * pos[0] + 2 * pos[1] + pos[2]) * m, m), :]

        def copy(a, k, block, to, src=None):
            return pltpu.make_async_remote_copy(
                src_ref=rows(a, block) if src is None else src, dst_ref=rows(a, block),
                send_sem=send_sems.at[a, k], recv_sem=recv_sems.at[a, k], device_id=to, device_id_type=MESH)

        mine = [pltpu.make_async_copy(x_refs[a], rows(a, me), local_sems.at[a]) for a in range(n)]
        for cp in mine:
            cp.start()
        first = []
        for a in range(n):
            first.append(copy(a, 0, me, sibling, src=x_refs[a]))
            first += [copy(a, 1 + j, me, (*chip, c), src=x_refs[a]) for j, chip in enumerate(chips)]
        for cp in first:
            cp.start()
        passed = []
        for j, chip in enumerate(chips):
            for a in range(n):
                copy(a, 1 + j, (*chip, c), me).wait_recv()
                passed.append(copy(a, 4 + j, (*chip, c), sibling))
                passed[-1].start()
        for a in range(n):
            copy(a, 0, sibling, me).wait_recv()
            for j, chip in enumerate(chips):
                copy(a, 4 + j, (*chip, 1 - c), me).wait_recv()
        for cp in first + passed:
            cp.wait_send()
        for cp in mine:
            cp.wait()

    vmem = pl.BlockSpec(memory_space=pltpu.VMEM)
    return pl.pallas_call(
        body, name="gather_weights",
        out_shape=[jax.ShapeDtypeStruct((N_DEV * s.shape[0], s.shape[1]), s.dtype) for s in shards],
        in_specs=[vmem] * n, out_specs=[vmem] * n,
        scratch_shapes=[pltpu.SemaphoreType.DMA((n, 7)), pltpu.SemaphoreType.DMA((n, 7)), pltpu.SemaphoreType.DMA((n,))],
        compiler_params=_params(),
    )(*shards)


def _exchange(parts):
    n = len(parts)

    def body(*refs):
        in_refs, out_refs = refs[:n], refs[n:2 * n]
        send_sems, recv_sems, local_sems = refs[2 * n:]
        x, y, c = _mesh_pos()
        me = 4 * x + 2 * y + c
        own = [pltpu.make_async_copy(in_refs[a].at[me], out_refs[a].at[me], local_sems.at[a]) for a in range(n)]
        for cp in own:
            cp.start()
        copies = []
        for k in range(1, N_DEV):
            peer = (x ^ (k >> 2), y ^ ((k >> 1) & 1), c ^ (k & 1))
            peer_idx = 4 * peer[0] + 2 * peer[1] + peer[2]
            for a in range(n):
                copies.append(pltpu.make_async_remote_copy(
                    src_ref=in_refs[a].at[peer_idx], dst_ref=out_refs[a].at[me],
                    send_sem=send_sems.at[a, k - 1], recv_sem=recv_sems.at[a, k - 1], device_id=peer, device_id_type=MESH))
        for cp in copies:
            cp.start()
        for cp in copies:
            cp.wait()
        for cp in own:
            cp.wait()

    hbm = pl.BlockSpec(memory_space=pl.ANY)
    return pl.pallas_call(
        body, name="exchange_grads",
        out_shape=[jax.ShapeDtypeStruct(p.shape, p.dtype) for p in parts],
        in_specs=[hbm] * n, out_specs=[hbm] * n,
        scratch_shapes=[pltpu.SemaphoreType.DMA((n, 7)), pltpu.SemaphoreType.DMA((n, 7)), pltpu.SemaphoreType.DMA((n,))],
        compiler_params=_params(),
    )(*parts)


def _adamw(w, g, m, v):
    m = ADAM_B1 * m + (1.0 - ADAM_B1) * g
    v = ADAM_B2 * v + (1.0 - ADAM_B2) * (g * g)
    m_hat = m / (1.0 - ADAM_B1 ** ADAM_STEP)
    v_hat = v / (1.0 - ADAM_B2 ** ADAM_STEP)
    return -ADAM_LR * (m_hat / (jnp.sqrt(v_hat) + ADAM_EPS) + ADAM_WD * w), m, v


def _reduce_adamw(name, parts, w, m, v, row_tile):
    _, rows, cols = parts.shape

    def body(p_ref, w_ref, m_ref, v_ref, g_ref, d_ref, nm_ref, nv_ref):
        g = p_ref[0].astype(F32)
        for s in range(1, N_DEV):
            g = g + p_ref[s].astype(F32)
        g_ref[...] = g
        d_ref[...], nm_ref[...], nv_ref[...] = _adamw(w_ref[...], g, m_ref[...], v_ref[...])

    tile = pl.BlockSpec((row_tile, cols), lambda i: (i, 0))
    return pl.pallas_call(
        body, name=name, grid=(rows // row_tile,),
        in_specs=[pl.BlockSpec((N_DEV, row_tile, cols), lambda i: (0, i, 0)), tile, tile, tile],
        out_specs=[tile] * 4,
        out_shape=[jax.ShapeDtypeStruct((rows, cols), F32)] * 4,
        compiler_params=_params(dimension_semantics=("arbitrary",)),
    )(parts, w, m, v)


_SMALL = (("norm_g", 1024), ("ret_gn_g", 512), ("rwkv_mu", 2176), ("w0", 512), ("a0", 512), ("k_k", 512), ("k_a", 512),
          ("r_k", 512), ("rwkv_gn_g", 512), ("rwkv_gn_b", 512), ("final_norm_g", 1024))
_SMALL_LANES = sum(n for _, n in _SMALL) + 128
_WEIGHTS = ("norm_g", "w_in", "ret_gn_g", "rwkv_mu", "w_lora_up", "w0", "a_lora_up", "a0", "k_k", "k_a", "r_k", "rwkv_gn_g",
            "rwkv_gn_b", "w_out", "final_norm_g")


def _adamw_vectors(parts, wts, mom, var):
    k = len(_SMALL)

    def body(p_ref, *refs):
        w_refs, m_refs, v_refs, outs = refs[:k], refs[k:2 * k], refs[2 * k:3 * k], refs[3 * k:]
        g_all = p_ref[0]
        for s in range(1, N_DEV):
            g_all = g_all + p_ref[s]
        off = 0
        for i, (_, n) in enumerate(_SMALL):
            g = g_all[:, off:off + n]
            off += n
            outs[4 * i][...] = g
            outs[4 * i + 1][...], outs[4 * i + 2][...], outs[4 * i + 3][...] = _adamw(
                w_refs[i][...], g, m_refs[i][...], v_refs[i][...])
        outs[4 * k][...] = g_all[:, off:off + 128]

    vmem = pl.BlockSpec(memory_space=pltpu.VMEM)
    shapes = [jax.ShapeDtypeStruct((1, n), F32) for _, n in _SMALL for _ in range(4)] + [jax.ShapeDtypeStruct((1, 128), F32)]
    res = pl.pallas_call(
        body, name="adamw_vectors", out_shape=shapes,
        in_specs=[vmem] * (1 + 3 * k), out_specs=[vmem] * len(shapes), compiler_params=_params(),
    )(parts, *[wts[n] for n, _ in _SMALL], *[mom[n] for n, _ in _SMALL], *[var[n] for n, _ in _SMALL])
    return {n: res[4 * i:4 * i + 4] for i, (n, _) in enumerate(_SMALL)}, res[4 * k]


def kernel(x, norm_g, w_in, ret_gn_g, rwkv_mu, w_lora_up, w0, a_lora_up, a0, k_k, k_a, r_k, rwkv_gn_g, rwkv_gn_b, w_out, final_norm_g, loss_target, m_norm_g, m_w_in, m_ret_gn_g, m_rwkv_mu, m_w_lora_up, m_w0, m_a_lora_up, m_a0, m_k_k, m_k_a, m_r_k, m_rwkv_gn_g, m_rwkv_gn_b, m_w_out, m_final_norm_g, v_norm_g, v_w_in, v_ret_gn_g, v_rwkv_mu, v_w_lora_up, v_w0, v_a_lora_up, v_a0, v_k_k, v_k_a, v_r_k, v_rwkv_gn_g, v_rwkv_gn_b, v_w_out, v_final_norm_g):
    wts = dict(norm_g=norm_g, w_in=w_in, ret_gn_g=ret_gn_g, rwkv_mu=rwkv_mu, w_lora_up=w_lora_up, w0=w0, a_lora_up=a_lora_up,
               a0=a0, k_k=k_k, k_a=k_a, r_k=r_k, rwkv_gn_g=rwkv_gn_g, rwkv_gn_b=rwkv_gn_b, w_out=w_out,
               final_norm_g=final_norm_g)
    mom = dict(norm_g=m_norm_g, w_in=m_w_in, ret_gn_g=m_ret_gn_g, rwkv_mu=m_rwkv_mu, w_lora_up=m_w_lora_up, w0=m_w0,
               a_lora_up=m_a_lora_up, a0=m_a0, k_k=m_k_k, k_a=m_k_a, r_k=m_r_k, rwkv_gn_g=m_rwkv_gn_g,
               rwkv_gn_b=m_rwkv_gn_b, w_out=m_w_out, final_norm_g=m_final_norm_g)
    var = dict(norm_g=v_norm_g, w_in=v_w_in, ret_gn_g=v_ret_gn_g, rwkv_mu=v_rwkv_mu, w_lora_up=v_w_lora_up, w0=v_w0,
               a_lora_up=v_a_lora_up, a0=v_a0, k_k=v_k_k, k_a=v_k_a, r_k=v_r_k, rwkv_gn_g=v_rwkv_gn_g,
               rwkv_gn_b=v_rwkv_gn_b, w_out=v_w_out, final_norm_g=v_final_norm_g)
    shapes = {n: wts[n].shape for n in _WEIGHTS}

    g_in, g_out, g_wup, g_aup = _all_gather(
        [w_in[0].astype(BF16), w_out[0].astype(BF16), w_lora_up[0], a_lora_up[0]])
    w_in_bf = g_in.reshape(N_DEV, D_MODEL, SHARD_IN).transpose(1, 0, 2).reshape(D_MODEL, IN_COLS)
    wup_full = g_wup.reshape(N_DEV, LORA, SHARD_LORA).transpose(1, 0, 2).reshape(LORA, W)
    aup_full = g_aup.reshape(N_DEV, LORA, SHARD_LORA).transpose(1, 0, 2).reshape(LORA, W)

    loss, dx, g = _local_grads(
        x[0], loss_target[0], norm_g, w_in_bf, ret_gn_g, rwkv_mu, wup_full, w0, aup_full, a0, k_k, k_a,
        r_k.reshape(1, W), rwkv_gn_g, rwkv_gn_b, g_out, final_norm_g.reshape(1, D_MODEL))

    small = jnp.concatenate([g[n] for n, _ in _SMALL] + [loss], axis=1)
    parts = _exchange([
        g["w_in"].reshape(D_MODEL, N_DEV, SHARD_IN).transpose(1, 0, 2).astype(BF16),
        g["w_out"].reshape(N_DEV, SHARD_OUT, D_MODEL).astype(BF16),
        g["w_lora_up"].reshape(LORA, N_DEV, SHARD_LORA).transpose(1, 0, 2),
        g["a_lora_up"].reshape(LORA, N_DEV, SHARD_LORA).transpose(1, 0, 2),
        jnp.broadcast_to(small[None], (N_DEV, 1, _SMALL_LANES))])
    res = {}
    res["w_in"] = _reduce_adamw("adamw_w_in", parts[0], w_in[0], m_w_in[0], v_w_in[0], 256)
    res["w_out"] = _reduce_adamw("adamw_w_out", parts[1], w_out[0], m_w_out[0], v_w_out[0], SHARD_OUT)
    res["w_lora_up"] = _reduce_adamw("adamw_w_lora_up", parts[2], w_lora_up[0], m_w_lora_up[0], v_w_lora_up[0], LORA)
    res["a_lora_up"] = _reduce_adamw("adamw_a_lora_up", parts[3], a_lora_up[0], m_a_lora_up[0], v_a_lora_up[0], LORA)
    as_row = lambda d: {n: d[n].reshape(1, size) for n, size in _SMALL}
    vec, loss_row = _adamw_vectors(parts[4], as_row(wts), as_row(mom), as_row(var))
    res.update(vec)
    res = {n: [t.reshape(shapes[n]) for t in res[n]] for n in _WEIGHTS}
    return (loss_row[0, 0], dx[None], *[res[n][0] for n in _WEIGHTS], *[res[n][1] for n in _WEIGHTS],
            *[res[n][2] for n in _WEIGHTS], *[res[n][3] for n in _WEIGHTS])
```

```python
import functools

import numpy as np
import jax
import jax.numpy as jnp
from jax import lax
from jax.experimental import pallas as pl
from jax.experimental.pallas import tpu as pltpu

F32 = jnp.float32
BF16 = jnp.bfloat16

D_MODEL = 1024
CHUNK = 64
RET_HEADS = 4
RET_DV = 128
RET_DK = 64
RET_QK = 256
RET_WIDTH = 512
RWKV_WIDTH = 512
RWKV_HEAD = 64
RWKV_HEADS = 8
LORA = 64
RET_COLS = 2 * RET_QK + 2 * RET_WIDTH
RWKV_COLS = 4 * RWKV_WIDTH + 2 * LORA
IN_COLS = RET_COLS + RWKV_COLS
ROPE_BASE = 10000.0
RMS_EPS = 1e-6
RET_GN_EPS = 1e-5
RWKV_GN_EPS = 64e-5
ADAM_LR = 0.001
ADAM_B1 = 0.9
ADAM_B2 = 0.999
ADAM_EPS = 1e-08
ADAM_WD = 0.01
ADAM_STEP = 10
N_DEV = 8
SHARD_IN = IN_COLS // N_DEV
SHARD_OUT = D_MODEL // N_DEV
SHARD_LORA = RWKV_WIDTH // N_DEV
VMEM_LIMIT = 56 * 1024 * 1024
TOK_TILE = 256
WKV_CHUNK = 32

MESH = pl.DeviceIdType.MESH


def _dot_bf(a, b):
    return jnp.dot(a.astype(BF16), b.astype(BF16), preferred_element_type=F32)


def _dot_nt_bf(a, b):
    return lax.dot_general(a.astype(BF16), b.astype(BF16), (((1,), (1,)), ((), ())), preferred_element_type=F32)


def _dot_tn_bf(a, b):
    return lax.dot_general(a.astype(BF16), b.astype(BF16), (((0,), (0,)), ((), ())), preferred_element_type=F32)


@jax.custom_vjp
def _mm(a, b):
    return _dot_bf(a, b)


@jax.custom_vjp
def _mm_nt(a, b):
    return _dot_nt_bf(a, b)


@jax.custom_vjp
def _mm_tn(a, b):
    return _dot_tn_bf(a, b)


_mm.defvjp(lambda a, b: (_dot_bf(a, b), (a, b)), lambda res, g: (_dot_nt_bf(g, res[1]), _dot_tn_bf(res[0], g)))
_mm_nt.defvjp(lambda a, b: (_dot_nt_bf(a, b), (a, b)), lambda res, g: (_dot_bf(g, res[1]), _dot_tn_bf(g, res[0])))
_mm_tn.defvjp(lambda a, b: (_dot_tn_bf(a, b), (a, b)), lambda res, g: (_dot_nt_bf(res[1], g), _dot_bf(res[0], g)))


def _trunc(x):
    return lax.bitcast_convert_type(lax.bitcast_convert_type(x, jnp.uint32) & jnp.uint32(0xFFFF0000), F32)


def _two_piece(x):
    hi = _trunc(x)
    return jnp.concatenate([hi, x - hi], axis=1)


def _mix_raw(x, mat2):
    return jnp.dot(_two_piece(x), mat2, preferred_element_type=F32)


@jax.custom_vjp
def _head_mix(x, mat2):
    return _mix_raw(x, mat2)


_head_mix.defvjp(lambda x, mat2: (_mix_raw(x, mat2), mat2), lambda mat2, g: (_mix_raw(g, mat2), jnp.zeros_like(mat2)))


def _swap_halves(x):
    lane = lax.broadcasted_iota(jnp.int32, x.shape, 1)
    return jnp.where((lane & (RET_DK - 1)) < RET_DK // 2, pltpu.roll(x, RET_QK - RET_DK // 2, axis=1),
                     pltpu.roll(x, RET_DK // 2, axis=1))


@jax.custom_vjp
def _rot(x):
    return _swap_halves(x)


_rot.defvjp(lambda x: (_swap_halves(x), None), lambda _, g: (_swap_halves(g),))


def _params(**kw):
    return pltpu.CompilerParams(vmem_limit_bytes=VMEM_LIMIT, **kw)


def _full(shape):
    nd = len(shape)
    return pl.BlockSpec(shape, lambda i, _nd=nd: (0,) * _nd)


def _rows(tile, width):
    return pl.BlockSpec((tile, width), lambda i: (i, 0))


def _block_mix(n, blk, scale=1.0):
    idx = np.arange(n) // blk
    m = (idx[:, None] == idx[None, :]).astype(np.float32) * scale
    return jnp.asarray(np.concatenate([m, m], axis=0))


def _rope_tables(T):
    half = RET_DK // 2
    expo = -np.arange(half, dtype=np.float32) / np.float32(half)
    freqs = np.exp(expo * np.float32(np.log(ROPE_BASE))).astype(np.float32)
    ang = np.arange(T, dtype=np.float32)[:, None] * freqs[None, :]
    cos, sin = np.cos(ang).astype(np.float32), np.sin(ang).astype(np.float32)
    cos_h = np.concatenate([cos, cos], axis=1)
    sin_h = np.concatenate([-sin, sin], axis=1)
    cos_t = np.tile(cos_h, (1, RET_HEADS))
    sin_t = np.tile(sin_h, (1, RET_HEADS))
    return jnp.asarray(cos_t), jnp.asarray(sin_t)


def _ret_tables():
    h = np.arange(RET_HEADS, dtype=np.float32)
    lg = np.log(1.0 - np.exp2(-5.0 - h)).astype(np.float32)
    idx = np.arange(CHUNK, dtype=np.float32)
    intra = np.exp(lg[:, None, None] * np.abs(idx[:, None] - idx[None, :])).astype(np.float32)
    q_dec = np.exp(lg[:, None] * (idx[None, :] + 1.0)).astype(np.float32)
    k_dec = np.exp(lg[:, None] * (CHUNK - 1.0 - idx[None, :])).astype(np.float32)
    chunk_dec = np.exp(lg * CHUNK).astype(np.float32)
    lane_head = np.arange(RET_QK) // RET_DK
    mask = (lane_head[None, :] == np.arange(RET_HEADS)[:, None]).astype(np.float32)
    m = np.broadcast_to(mask[:, None, :], (RET_HEADS, CHUNK, RET_QK)).copy()
    qd = m * q_dec[:, :, None]
    kd = m * k_dec[:, :, None]
    return jnp.asarray(intra), jnp.asarray(m), jnp.asarray(qd), jnp.asarray(kd), [float(c) for c in chunk_dec]


def _rmsnorm(x, g):
    return x * lax.rsqrt(jnp.mean(x * x, axis=-1, keepdims=True) + RMS_EPS) * g


def _inproj(x, norm_g, w_ret, w_rwkv):
    T = x.shape[0]

    def body(x_ref, g_ref, wr_ref, ww_ref, pr_ref, pw_ref, u_ref):
        ub = _rmsnorm(x_ref[...], g_ref[...]).astype(BF16)
        u_ref[...] = ub
        pr_ref[...] = jnp.dot(ub, wr_ref[...], preferred_element_type=F32)
        pw_ref[...] = jnp.dot(ub, ww_ref[...], preferred_element_type=F32)

    return pl.pallas_call(
        body, name="inproj", grid=(T // TOK_TILE,),
        in_specs=[_rows(TOK_TILE, D_MODEL), _full((1, D_MODEL)), _full((D_MODEL, RET_COLS)), _full((D_MODEL, RWKV_COLS))],
        out_specs=[_rows(TOK_TILE, RET_COLS), _rows(TOK_TILE, RWKV_COLS), _rows(TOK_TILE, D_MODEL)],
        out_shape=[jax.ShapeDtypeStruct((T, RET_COLS), F32), jax.ShapeDtypeStruct((T, RWKV_COLS), F32),
                   jax.ShapeDtypeStruct((T, D_MODEL), BF16)],
        compiler_params=_params(dimension_semantics=("arbitrary",)),
    )(x, norm_g, w_ret, w_rwkv)


def _ret_chunk(pq, pk, v_heads, s_heads, cos_t, sin_t, dec, hm, qd, kd, chunk_dec):
    q = pq * cos_t + _rot(pq) * sin_t
    k = (pk * cos_t + _rot(pk) * sin_t) * (RET_DK ** -0.5)
    outs, s_out = [], []
    for h in range(RET_HEADS):
        sc = _mm_nt(q * hm[h], k * hm[h]) * dec[h]
        intra = _mm(sc, v_heads[h])
        kv = _mm_tn(k * kd[h], v_heads[h])
        inter = _mm(q * qd[h], s_heads[h])
        outs.append(intra + inter)
        s_out.append(s_heads[h] * chunk_dec[h] + kv)
    return tuple(outs), tuple(s_out)


def _ret_specs():
    const = [_full((RET_HEADS, CHUNK, CHUNK)), _full((RET_HEADS, CHUNK, RET_QK)),
             _full((RET_HEADS, CHUNK, RET_QK)), _full((RET_HEADS, CHUNK, RET_QK))]
    return const


def _ret_fwd(p_ret, tabs):
    T = p_ret.shape[0]
    nc = T // CHUNK
    cos_t, sin_t, dec, hm, qd, kd, chunk_dec = tabs

    def body(p_ref, cos_ref, sin_ref, dec_ref, hm_ref, qd_ref, kd_ref, out_ref, sin_save_ref, s_scr):
        @pl.when(pl.program_id(0) == 0)
        def _():
            s_scr[...] = jnp.zeros_like(s_scr)

        s_heads = tuple(s_scr[h] for h in range(RET_HEADS))
        for h in range(RET_HEADS):
            sin_save_ref[0, h] = s_heads[h]
        pq = p_ref[:, 0:RET_QK]
        pk = p_ref[:, RET_QK:2 * RET_QK]
        v_heads = tuple(p_ref[:, 2 * RET_QK + RET_DV * h:2 * RET_QK + RET_DV * (h + 1)] for h in range(RET_HEADS))
        outs, s_out = _ret_chunk(pq, pk, v_heads, s_heads, cos_ref[...], sin_ref[...], dec_ref[...],
                                 hm_ref[...], qd_ref[...], kd_ref[...], chunk_dec)
        for h in range(RET_HEADS):
            out_ref[:, RET_DV * h:RET_DV * (h + 1)] = outs[h]
            s_scr[h] = s_out[h]

    return pl.pallas_call(
        body, name="ret_fwd", grid=(nc,),
        in_specs=[pl.BlockSpec((CHUNK, RET_COLS), lambda i: (i, 0)), _rows(CHUNK, RET_QK), _rows(CHUNK, RET_QK)] + _ret_specs(),
        out_specs=[_rows(CHUNK, RET_WIDTH), pl.BlockSpec((1, RET_HEADS, RET_QK, RET_DV), lambda i: (i, 0, 0, 0))],
        out_shape=[jax.ShapeDtypeStruct((T, RET_WIDTH), F32), jax.ShapeDtypeStruct((nc, RET_HEADS, RET_QK, RET_DV), F32)],
        scratch_shapes=[pltpu.VMEM((RET_HEADS, RET_QK, RET_DV), F32)],
        compiler_params=_params(dimension_semantics=("arbitrary",)),
    )(p_ret, cos_t, sin_t, dec, hm, qd, kd)


def _ret_bwd(p_ret, s_saved, d_ret, tabs):
    T = p_ret.shape[0]
    nc = T // CHUNK
    cos_t, sin_t, dec, hm, qd, kd, chunk_dec = tabs

    def body(p_ref, s_ref, dret_ref, cos_ref, sin_ref, dec_ref, hm_ref, qd_ref, kd_ref, dp_ref, ds_scr):
        @pl.when(pl.program_id(0) == 0)
        def _():
            ds_scr[...] = jnp.zeros_like(ds_scr)

        pq = p_ref[:, 0:RET_QK]
        pk = p_ref[:, RET_QK:2 * RET_QK]
        v_heads = tuple(p_ref[:, 2 * RET_QK + RET_DV * h:2 * RET_QK + RET_DV * (h + 1)] for h in range(RET_HEADS))
        s_heads = tuple(s_ref[0, h] for h in range(RET_HEADS))
        consts = (cos_ref[...], sin_ref[...], dec_ref[...], hm_ref[...], qd_ref[...], kd_ref[...])
        _, vjp = jax.vjp(lambda a, b, c, d: _ret_chunk(a, b, c, d, *consts, chunk_dec), pq, pk, v_heads, s_heads)
        d_out = tuple(dret_ref[:, RET_DV * h:RET_DV * (h + 1)] for h in range(RET_HEADS))
        d_s = tuple(ds_scr[h] for h in range(RET_HEADS))
        dq, dk, dv, ds_in = vjp((d_out, d_s))
        dp_ref[:, 0:RET_QK] = dq
        dp_ref[:, RET_QK:2 * RET_QK] = dk
        for h in range(RET_HEADS):
            dp_ref[:, 2 * RET_QK + RET_DV * h:2 * RET_QK + RET_DV * (h + 1)] = dv[h]
            ds_scr[h] = ds_in[h]

    rev = lambda i: (nc - 1 - i, 0)
    return pl.pallas_call(
        body, name="ret_bwd", grid=(nc,),
        in_specs=[pl.BlockSpec((CHUNK, RET_COLS), rev),
                  pl.BlockSpec((1, RET_HEADS, RET_QK, RET_DV), lambda i: (nc - 1 - i, 0, 0, 0)),
                  pl.BlockSpec((CHUNK, RET_WIDTH), rev), pl.BlockSpec((CHUNK, RET_QK), rev), pl.BlockSpec((CHUNK, RET_QK), rev)]
        + _ret_specs(),
        out_specs=pl.BlockSpec((CHUNK, 2 * RET_QK + RET_WIDTH), rev),
        out_shape=jax.ShapeDtypeStruct((T, 2 * RET_QK + RET_WIDTH), F32),
        scratch_shapes=[pltpu.VMEM((RET_HEADS, RET_QK, RET_DV), F32)],
        compiler_params=_params(dimension_semantics=("arbitrary",)),
    )(p_ret, s_saved, d_ret, cos_t, sin_t, dec, hm, qd, kd)


def _wkv_consts():
    lane = lax.broadcasted_iota(jnp.int32, (RWKV_HEAD, RWKV_WIDTH), 1)
    sub = lax.broadcasted_iota(jnp.int32, (RWKV_HEAD, RWKV_WIDTH), 0)
    diag = ((lane & (RWKV_HEAD - 1)) == sub).astype(F32)
    r = lax.broadcasted_iota(jnp.int32, (3 * 128, 128), 0)
    c = lax.broadcasted_iota(jnp.int32, (3 * 128, 128), 1)
    ones = (((r & 127) >> 6) == (c >> 6)).astype(BF16)
    return diag, ones


def _stack(x):
    return jnp.concatenate([x[:, 128 * p:128 * (p + 1)] for p in range(4)], axis=0)


def _unstack(y):
    n = y.shape[0] // 4
    return jnp.concatenate([y[n * p:n * (p + 1)] for p in range(4)], axis=1)


def _split(x, n):
    pieces = []
    for _ in range(n):
        p = x.astype(BF16)
        pieces.append(p)
        x = x - p.astype(F32)
    return pieces


def _lane_sum(x, ones):
    return _unstack(jnp.dot(_two_piece(_stack(x)), ones[:256].astype(F32), preferred_element_type=F32))


def _lane_sum_bf(x, ones):
    return _unstack(jnp.dot(_stack(x).astype(BF16), ones[:128], preferred_element_type=F32))


def _expand(row, diag_bf, ones, n):
    lhs = jnp.concatenate([_stack(jnp.broadcast_to(p, diag_bf.shape) * diag_bf) for p in _split(row, n)], axis=1)
    return _unstack(jnp.dot(lhs, ones[:128 * n], preferred_element_type=F32))


def _colsum(x):
    return jnp.sum(x, axis=0, keepdims=True)


def _head_sums(x, ones):
    return _unstack(jnp.dot(jnp.concatenate(_split(_stack(x), 3), axis=1), ones, preferred_element_type=F32))


def _wkv_fwd(r, w, k, v, kap, a):
    T = r.shape[0]
    C = WKV_CHUNK
    nc = T // C

    def body(r_ref, w_ref, k_ref, v_ref, kap_ref, a_ref, o_ref, s_all_ref, s_scr, vf_scr):
        @pl.when(pl.program_id(0) == 0)
        def _():
            s_scr[...] = jnp.zeros_like(s_scr)

        diag, ones = _wkv_consts()
        diag_bf = diag.astype(BF16)
        rr, ww, kk, vv, kap_, aa = (ref[...] for ref in (r_ref, w_ref, k_ref, v_ref, kap_ref, a_ref))
        bb = kap_ * aa
        c1 = _head_sums(pltpu.roll(bb, 1, axis=0) * kap_, ones)
        row = lambda x, t: x[t:t + 1]

        for t in range(C):
            vf_scr[t] = _expand(row(vv, t), diag_bf, ones, 2)

        s_prev = s_scr[...]
        sa = _lane_sum(s_prev * (-row(kap_, 0)), ones)
        ls, rows = None, []

        def emit_o(t, s_t):
            rows.append(_colsum(_lane_sum_bf(s_t * row(rr, t), ones) * diag))
            if t % 8 == 7:
                o_ref[t - 7:t + 1, :] = jnp.concatenate(rows, axis=0)
                rows.clear()

        for t in range(C):
            u = s_prev * row(ww, t) + vf_scr[t] * row(kk, t)
            if t > 0:
                sa = ls - sa * row(c1, t)
            if t + 1 < C:
                ls = _lane_sum(u * (-row(kap_, t + 1)), ones)
            if t > 0:
                emit_o(t - 1, s_prev)
            s_prev = u + sa * row(bb, t)
            s_all_ref[t] = s_prev
        emit_o(C - 1, s_prev)
        s_scr[...] = s_prev

    spec = _rows(C, RWKV_WIDTH)
    return pl.pallas_call(
        body, name="wkv_fwd", grid=(nc,),
        in_specs=[spec] * 6,
        out_specs=[spec, pl.BlockSpec((C, RWKV_HEAD, RWKV_WIDTH), lambda i: (i, 0, 0))],
        out_shape=[jax.ShapeDtypeStruct((T, RWKV_WIDTH), F32), jax.ShapeDtypeStruct((T, RWKV_HEAD, RWKV_WIDTH), F32)],
        scratch_shapes=[pltpu.VMEM((RWKV_HEAD, RWKV_WIDTH), F32), pltpu.VMEM((C, RWKV_HEAD, RWKV_WIDTH), F32)],
        compiler_params=_params(dimension_semantics=("arbitrary",)),
    )(r, w, k, v, kap, a)


def _wkv_bwd(r, w, k, v, kap, a, s_all, d_o):
    T = r.shape[0]
    C = WKV_CHUNK
    nc = T // C

    def body(r_ref, w_ref, k_ref, v_ref, kap_ref, a_ref, s_ref, s_before_ref, do_ref,
             dr_ref, dw_ref, dk_ref, dv_ref, dkap_ref, da_ref, ds_scr, vf_scr, dof_scr, sa_scr):
        first_chunk = pl.program_id(0) == nc - 1

        @pl.when(pl.program_id(0) == 0)
        def _():
            ds_scr[...] = jnp.zeros_like(ds_scr)

        diag, ones = _wkv_consts()
        diag_bf = diag.astype(BF16)
        rr, ww, kk, vv, kap_, aa, dd = (ref[...] for ref in (r_ref, w_ref, k_ref, v_ref, kap_ref, a_ref, do_ref))
        bb = kap_ * aa
        e1 = _head_sums(pltpu.roll(kap_, C - 1, axis=0) * bb, ones)
        row = lambda x, t: x[t:t + 1]

        def state_before(t):
            return s_ref[t - 1] if t > 0 else jnp.where(first_chunk, 0.0, s_before_ref[0])

        for t in range(C):
            vf_scr[t] = _expand(row(vv, t), diag_bf, ones, 2)
            dof_scr[t] = _expand(row(dd, t), diag_bf, ones, 2)
            sa_scr[t] = _lane_sum(state_before(t) * (-row(kap_, t)), ones)

        d_sn, dsa, rows = None, None, [None] * C

        def emit_rows(t, d_sn_t, dsa_t):
            s_prev, dof = state_before(t), dof_scr[t]
            dv = _colsum(_lane_sum_bf(d_sn_t * row(kk, t), ones) * diag)
            db = _colsum(d_sn_t * sa_scr[t])
            rows[t] = (_colsum(s_ref[t] * dof), _colsum(d_sn_t * s_prev), _colsum(d_sn_t * vf_scr[t]), dv,
                       db * row(aa, t) - _colsum(dsa_t * s_prev), db * row(kap_, t))
            if t % 8 == 0:
                for j, ref in enumerate((dr_ref, dw_ref, dk_ref, dv_ref, dkap_ref, da_ref)):
                    ref[t:t + 8, :] = jnp.concatenate([rows[u][j] for u in range(t, t + 8)], axis=0)

        for t in reversed(range(C)):
            dof = dof_scr[t]
            if t == C - 1:
                d_sn = ds_scr[...] + dof * row(rr, t)
                dsa = _lane_sum(d_sn * row(bb, t), ones)
            else:
                v_t = d_sn * row(ww, t + 1) + dof * row(rr, t)
                ls = _lane_sum(v_t * row(bb, t), ones)
                emit_rows(t + 1, d_sn, dsa)
                d_sn = v_t - dsa * row(kap_, t + 1)
                dsa = ls - dsa * row(e1, t)
        emit_rows(0, d_sn, dsa)
        d_s = d_sn * row(ww, 0) - dsa * row(kap_, 0)
        ds_scr[...] = d_s

    spec = pl.BlockSpec((C, RWKV_WIDTH), lambda i: (nc - 1 - i, 0))
    states = pl.BlockSpec((C, RWKV_HEAD, RWKV_WIDTH), lambda i: (nc - 1 - i, 0, 0))
    before = pl.BlockSpec((1, RWKV_HEAD, RWKV_WIDTH), lambda i: (jnp.maximum((nc - 1 - i) * C - 1, 0), 0, 0))
    big = pltpu.VMEM((C, RWKV_HEAD, RWKV_WIDTH), F32)
    return pl.pallas_call(
        body, name="wkv_bwd", grid=(nc,),
        in_specs=[spec] * 6 + [states, before, spec],
        out_specs=[spec] * 6,
        out_shape=[jax.ShapeDtypeStruct((T, RWKV_WIDTH), F32)] * 6,
        scratch_shapes=[pltpu.VMEM((RWKV_HEAD, RWKV_WIDTH), F32), big, big, big],
        compiler_params=_params(dimension_semantics=("arbitrary",)),
    )(r, w, k, v, kap, a, s_all, s_all, d_o)


W = RWKV_WIDTH


def _softplus(y):
    return jnp.maximum(y, 0.0) + jnp.log(1.0 + jnp.exp(-jnp.abs(y)))


def _prep_fn(kr, xwa, w0, a0, k_k, k_a, wup_pad, aup_pad, ones64):
    w_log = -_softplus(-(w0 + _mm(jnp.tanh(xwa), wup_pad))) - 0.5
    decay = jnp.exp(-jnp.exp(w_log))
    a = jax.nn.sigmoid(a0 + _mm(xwa, aup_pad))
    kk = kr * k_k
    kap = kk / jnp.maximum(jnp.sqrt(_head_mix(kk * kk, ones64)), 1e-12)
    k = kr * (1.0 + (a - 1.0) * k_a)
    return decay, k, kap, a


def _shift_down(p, first_row):
    rows = lax.broadcasted_iota(jnp.int32, p.shape, 0)
    return jnp.where(rows == 0, first_row, pltpu.roll(p, 1, axis=0))


def _shift_up(z, last_row):
    n = z.shape[0]
    rows = lax.broadcasted_iota(jnp.int32, z.shape, 0)
    return jnp.where(rows == n - 1, last_row, pltpu.roll(z, n - 1, axis=0))


def _prev_block_spec():
    return pl.BlockSpec((8, RWKV_COLS), lambda i: (jnp.maximum(i * (TOK_TILE // 8) - 1, 0), 0))


def _mixed(p_ref, prev8_ref, mu_ref, first_tile):
    p = p_ref[...]
    first_row = jnp.where(first_tile, 0.0, prev8_ref[7:8, :])
    prev = _shift_down(p, first_row)
    return p, prev, p + mu_ref[...] * (prev - p)


def _prep_fwd(p_rwkv, mu, w0, a0, k_k, k_a, wup_pad, aup_pad, ones64):
    T = p_rwkv.shape[0]

    def body(p_ref, prev8_ref, mu_ref, w0_ref, a0_ref, kk_ref, ka_ref, wup_ref, aup_ref, ones_ref,
             r_ref, w_ref, k_ref, v_ref, kap_ref, a_ref, g_ref):
        _, _, ps = _mixed(p_ref, prev8_ref, mu_ref, pl.program_id(0) == 0)
        decay, k, kap, a = _prep_fn(ps[:, W:2 * W], ps[:, 4 * W:], w0_ref[...], a0_ref[...], kk_ref[...], ka_ref[...],
                                    wup_ref[...], aup_ref[...], ones_ref[...])
        r_ref[...] = ps[:, 0:W]
        w_ref[...] = decay
        k_ref[...] = k
        v_ref[...] = ps[:, 2 * W:3 * W]
        kap_ref[...] = kap
        a_ref[...] = a
        g_ref[...] = ps[:, 3 * W:4 * W]

    vec = _full((1, W))
    return pl.pallas_call(
        body, name="prep_fwd", grid=(T // TOK_TILE,),
        in_specs=[_rows(TOK_TILE, RWKV_COLS), _prev_block_spec(), _full((1, RWKV_COLS)), vec, vec, vec, vec,
                  _full((2 * LORA, W)), _full((2 * LORA, W)), _full((2 * W, W))],
        out_specs=[_rows(TOK_TILE, W)] * 7,
        out_shape=[jax.ShapeDtypeStruct((T, W), F32)] * 7,
        compiler_params=_params(dimension_semantics=("arbitrary",)),
    )(p_rwkv, p_rwkv, mu, w0, a0, k_k, k_a, wup_pad, aup_pad, ones64)


def _prep_bwd(p_rwkv, mu, w0, a0, k_k, k_a, wup_pad, aup_pad, ones64, dr, dw, dk, dv, dkap, da, dg, dr2, dk2, dv2):
    T = p_rwkv.shape[0]
    nt = T // TOK_TILE

    def body(p_ref, prev8_ref, mu_ref, w0_ref, a0_ref, kk_ref, ka_ref, wup_ref, aup_ref, ones_ref,
             dr_ref, dw_ref, dk_ref, dv_ref, dkap_ref, da_ref, dg_ref, dr2_ref, dk2_ref, dv2_ref,
             dp_ref, dmu_ref, dw0_ref, da0_ref, dkk_ref, dka_ref, dwup_ref, daup_ref, zrow_scr):
        i = pl.program_id(0)
        accs = (dmu_ref, dw0_ref, da0_ref, dkk_ref, dka_ref, dwup_ref, daup_ref)

        @pl.when(i == 0)
        def _():
            zrow_scr[...] = jnp.zeros_like(zrow_scr)
            for ref in accs:
                ref[...] = jnp.zeros_like(ref)

        p, prev, ps = _mixed(p_ref, prev8_ref, mu_ref, i == nt - 1)
        ones = ones_ref[...]
        _, vjp = jax.vjp(lambda *args: _prep_fn(*args, ones), ps[:, W:2 * W], ps[:, 4 * W:], w0_ref[...], a0_ref[...],
                         kk_ref[...], ka_ref[...], wup_ref[...], aup_ref[...])
        dkr, dxwa, dw0, da0, dkk, dka, dwup, daup = vjp(
            (dw_ref[...], dk_ref[...] + dk2_ref[...], dkap_ref[...], da_ref[...]))
        dps = jnp.concatenate([dr_ref[...] + dr2_ref[...], dkr, dv_ref[...] + dv2_ref[...], dg_ref[...], dxwa], axis=1)
        z = dps * mu_ref[...]
        dp_ref[...] = dps - z + _shift_up(z, zrow_scr[0:1, :])
        zrow_scr[0:1, :] = z[0:1, :]
        for ref, val in zip(accs, (_colsum(dps * (prev - p)), dw0, da0, dkk, dka, dwup, daup)):
            ref[...] += val

    rev = lambda i: (nt - 1 - i, 0)
    vec = _full((1, W))
    lora = _full((2 * LORA, W))
    tile = pl.BlockSpec((TOK_TILE, W), rev)
    prev8 = pl.BlockSpec((8, RWKV_COLS), lambda i: (jnp.maximum((nt - 1 - i) * (TOK_TILE // 8) - 1, 0), 0))
    return pl.pallas_call(
        body, name="prep_bwd", grid=(nt,),
        in_specs=[pl.BlockSpec((TOK_TILE, RWKV_COLS), rev), prev8, _full((1, RWKV_COLS)), vec, vec, vec, vec, lora, lora,
                  _full((2 * W, W))] + [tile] * 10,
        out_specs=[pl.BlockSpec((TOK_TILE, RWKV_COLS), rev), _full((1, RWKV_COLS)), vec, vec, vec, vec, lora, lora],
        out_shape=[jax.ShapeDtypeStruct((T, RWKV_COLS), F32), jax.ShapeDtypeStruct((1, RWKV_COLS), F32)]
        + [jax.ShapeDtypeStruct((1, W), F32)] * 4 + [jax.ShapeDtypeStruct((2 * LORA, W), F32)] * 2,
        scratch_shapes=[pltpu.VMEM((8, RWKV_COLS), F32)],
        compiler_params=_params(dimension_semantics=("arbitrary",)),
    )(p_rwkv, p_rwkv, mu, w0, a0, k_k, k_a, wup_pad, aup_pad, ones64, dr, dw, dk, dv, dkap, da, dg, dr2, dk2, dv2)


def _silu(x):
    return x * jax.nn.sigmoid(x)


def _post_y(o, r, k, v, g_rw, ret_raw, g_ret, ret_gn_g, gn_g, gn_b, r_k, avg128, avg64, ones64):
    xc = ret_raw - _head_mix(ret_raw, avg128)
    ret = xc * lax.rsqrt(_head_mix(xc * xc, avg128) + RET_GN_EPS)
    y_ret = _silu(g_ret) * (ret * ret_gn_g)
    oc = o - _head_mix(o, avg64)
    on = oc * lax.rsqrt(_head_mix(oc * oc, avg64) + RWKV_GN_EPS) * gn_g + gn_b
    bonus = _head_mix(r * k * r_k, ones64) * v
    y_rwkv = _silu(g_rw) * (on + bonus)
    return y_ret, y_rwkv


def _post_loss(h, final_g, target):
    err = _rmsnorm(h, final_g) - target
    return 0.5 * jnp.sum(jnp.mean(err * err, axis=-1))


def _post(o, r, k, v, g_rw, ret_raw, p_ret, x, target, ret_gn_g, gn_g, gn_b, r_k, final_g, w_out, avg128, avg64, ones64):
    T = x.shape[0]
    n_tok_out = 8

    def body(o_ref, r_ref, k_ref, v_ref, grw_ref, ret_ref, gret_ref, x_ref, tgt_ref, rg_ref, gg_ref, gb_ref, rk_ref, fg_ref,
             wo_ref, a128_ref, a64_ref, ones_ref, *outs):
        tok_outs, (dwo_ref, drg_ref, dgg_ref, dgb_ref, drk_ref, dfg_ref, loss_ref) = outs[:n_tok_out], outs[n_tok_out:]
        accs = (dwo_ref, drg_ref, dgg_ref, dgb_ref, drk_ref, dfg_ref, loss_ref)

        @pl.when(pl.program_id(0) == 0)
        def _():
            for ref in accs:
                ref[...] = jnp.zeros_like(ref)

        consts = (a128_ref[...], a64_ref[...], ones_ref[...])
        (y_ret, y_rwkv), vjp = jax.vjp(
            lambda *args: _post_y(*args, *consts), o_ref[...], r_ref[...], k_ref[...], v_ref[...], grw_ref[...], ret_ref[...],
            gret_ref[...], rg_ref[...], gg_ref[...], gb_ref[...], rk_ref[...])
        h = x_ref[...] + _dot_bf(y_ret, wo_ref[0:RET_WIDTH, :]) + _dot_bf(y_rwkv, wo_ref[RET_WIDTH:, :])
        loss, (dh, dfg) = jax.value_and_grad(_post_loss, argnums=(0, 1))(h, fg_ref[...], tgt_ref[...])
        dy_ret = _dot_nt_bf(dh, wo_ref[0:RET_WIDTH, :])
        dy_rwkv = _dot_nt_bf(dh, wo_ref[RET_WIDTH:, :])
        do, dr, dk, dv, dgrw, dret, dgret, drg, dgg, dgb, drk = vjp((dy_ret, dy_rwkv))
        for ref, val in zip(tok_outs, (dh, do, dr, dk, dv, dgrw, dret, dgret)):
            ref[...] = val
        dwo_ref[0:RET_WIDTH, :] += _dot_tn_bf(y_ret, dh)
        dwo_ref[RET_WIDTH:, :] += _dot_tn_bf(y_rwkv, dh)
        for ref, val in zip(accs[1:], (drg, dgg, dgb, drk, dfg, jnp.full((1, 128), loss, F32))):
            ref[...] += val

    tile = _rows(TOK_TILE, W)
    wide = _rows(TOK_TILE, D_MODEL)
    vec = _full((1, W))
    sq = _full((2 * W, W))
    return pl.pallas_call(
        body, name="post", grid=(T // TOK_TILE,),
        in_specs=[tile] * 6 + [pl.BlockSpec((TOK_TILE, W), lambda i: (i, 2)), wide, wide, vec, vec, vec, vec,
                               _full((1, D_MODEL)), _full((D_MODEL, D_MODEL)), sq, sq, sq],
        out_specs=[wide] + [tile] * 7 + [_full((D_MODEL, D_MODEL)), vec, vec, vec, vec, _full((1, D_MODEL)), _full((1, 128))],
        out_shape=[jax.ShapeDtypeStruct((T, D_MODEL), F32)] + [jax.ShapeDtypeStruct((T, W), F32)] * 7
        + [jax.ShapeDtypeStruct((D_MODEL, D_MODEL), F32)] + [jax.ShapeDtypeStruct((1, W), F32)] * 4
        + [jax.ShapeDtypeStruct((1, D_MODEL), F32), jax.ShapeDtypeStruct((1, 128), F32)],
        compiler_params=_params(dimension_semantics=("arbitrary",)),
    )(o, r, k, v, g_rw, ret_raw, p_ret, x, target, ret_gn_g, gn_g, gn_b, r_k, final_g, w_out, avg128, avg64, ones64)


def _inproj_bwd_x(x, norm_g, dp_qkv, dg_ret, dp_rwkv, dh, w_ret, w_rwkv):
    T = x.shape[0]
    n_qkv = 2 * RET_QK + RET_WIDTH

    def body(x_ref, g_ref, dqkv_ref, dgret_ref, drwkv_ref, dh_ref, wr_ref, ww_ref, dx_ref, dg_ref):
        @pl.when(pl.program_id(0) == 0)
        def _():
            dg_ref[...] = jnp.zeros_like(dg_ref)

        _, vjp = jax.vjp(_rmsnorm, x_ref[...], g_ref[...])
        du = (_dot_nt_bf(dqkv_ref[...], wr_ref[:, 0:n_qkv]) + _dot_nt_bf(dgret_ref[...], wr_ref[:, n_qkv:])
              + _dot_nt_bf(drwkv_ref[...], ww_ref[...]))
        dx, dg = vjp(du)
        dx_ref[...] = dx + dh_ref[...]
        dg_ref[...] += dg

    return pl.pallas_call(
        body, name="inproj_bwd_x", grid=(T // TOK_TILE,),
        in_specs=[_rows(TOK_TILE, D_MODEL), _full((1, D_MODEL)), _rows(TOK_TILE, n_qkv), _rows(TOK_TILE, RET_WIDTH),
                  _rows(TOK_TILE, RWKV_COLS), _rows(TOK_TILE, D_MODEL), _full((D_MODEL, RET_COLS)), _full((D_MODEL, RWKV_COLS))],
        out_specs=[_rows(TOK_TILE, D_MODEL), _full((1, D_MODEL))],
        out_shape=[jax.ShapeDtypeStruct((T, D_MODEL), F32), jax.ShapeDtypeStruct((1, D_MODEL), F32)],
        compiler_params=_params(dimension_semantics=("arbitrary",)),
    )(x, norm_g, dp_qkv, dg_ret, dp_rwkv, dh, w_ret, w_rwkv)


def _grad_w(name, u, dp):
    T, n = dp.shape
    tile = 2 * TOK_TILE

    def body(u_ref, dp_ref, out_ref):
        @pl.when(pl.program_id(0) == 0)
        def _():
            out_ref[...] = jnp.zeros_like(out_ref)

        out_ref[...] += _dot_tn_bf(u_ref[...], dp_ref[...])

    return pl.pallas_call(
        body, name=name, grid=(T // tile,),
        in_specs=[_rows(tile, D_MODEL), _rows(tile, n)],
        out_specs=_full((D_MODEL, n)),
        out_shape=jax.ShapeDtypeStruct((D_MODEL, n), F32),
        compiler_params=_params(dimension_semantics=("arbitrary",)),
    )(u, dp)


def _pad_lora(w_up, top):
    z = jnp.zeros_like(w_up)
    return jnp.concatenate([w_up, z] if top else [z, w_up], axis=0)


def _local_grads(x, target, norm_g, w_in_bf, ret_gn_g, mu, w_lora_up, w0, a_lora_up, a0, k_k, k_a, r_k, gn_g, gn_b,
                 w_out_bf, final_g):
    T = x.shape[0]
    tabs = _rope_tables(T) + _ret_tables()
    ones64 = _block_mix(W, RWKV_HEAD)
    avg64 = _block_mix(W, RWKV_HEAD, 1.0 / RWKV_HEAD)
    avg128 = _block_mix(RET_WIDTH, RET_DV, 1.0 / RET_DV)
    w_ret, w_rwkv = w_in_bf[:, :RET_COLS], w_in_bf[:, RET_COLS:]
    wup_pad, aup_pad = _pad_lora(w_lora_up, True), _pad_lora(a_lora_up, False)

    p_ret, p_rwkv, u = _inproj(x, norm_g, w_ret, w_rwkv)
    ret_raw, s_saved = _ret_fwd(p_ret, tabs)
    r, w, k, v, kap, a, g_rw = _prep_fwd(p_rwkv, mu, w0, a0, k_k, k_a, wup_pad, aup_pad, ones64)
    o, s_all = _wkv_fwd(r, w, k, v, kap, a)
    (dh, do, dr2, dk2, dv2, dgrw, dret, dgret, d_w_out, d_ret_gn_g, d_gn_g, d_gn_b, d_r_k, d_final_g, loss) = _post(
        o, r, k, v, g_rw, ret_raw, p_ret, x, target, ret_gn_g, gn_g, gn_b, r_k, final_g, w_out_bf, avg128, avg64, ones64)
    dr, dw, dk, dv, dkap, da = _wkv_bwd(r, w, k, v, kap, a, s_all, do)
    dp_rwkv, d_mu, d_w0, d_a0, d_k_k, d_k_a, d_wup, d_aup = _prep_bwd(
        p_rwkv, mu, w0, a0, k_k, k_a, wup_pad, aup_pad, ones64, dr, dw, dk, dv, dkap, da, dgrw, dr2, dk2, dv2)
    dp_qkv = _ret_bwd(p_ret, s_saved, dret, tabs)
    dx, d_norm_g = _inproj_bwd_x(x, norm_g, dp_qkv, dgret, dp_rwkv, dh, w_ret, w_rwkv)
    d_w_in = jnp.concatenate([_grad_w("grad_w_qkv", u, dp_qkv), _grad_w("grad_w_gret", u, dgret),
                              _grad_w("grad_w_rwkv", u, dp_rwkv)], axis=1)
    grads = dict(norm_g=d_norm_g, w_in=d_w_in, ret_gn_g=d_ret_gn_g, rwkv_mu=d_mu, w_lora_up=d_wup[:LORA], w0=d_w0,
                 a_lora_up=d_aup[LORA:], a0=d_a0, k_k=d_k_k, k_a=d_k_a, r_k=d_r_k, rwkv_gn_g=d_gn_g, rwkv_gn_b=d_gn_b,
                 w_out=d_w_out, final_norm_g=d_final_g)
    return loss, dx, grads


def _mesh_pos():
    return lax.axis_index("x"), lax.axis_index("y"), lax.axis_index("c")


def _all_gather(shards):
    n = len(shards)

    def body(*refs):
        x_refs, out_refs = refs[:n], refs[n:2 * n]
        send_sems, recv_sems, local_sems = refs[2 * n:]
        x, y, c = _mesh_pos()
        me, sibling = (x, y, c), (x, y, 1 - c)
        chips = [(1 - x, y), (x, 1 - y), (1 - x, 1 - y)]

        def rows(a, pos):
            m = x_refs[a].shape[0]
            return out_refs[a].at[pl.ds((4 * pos[0] + 2 * pos[1] + pos[2]) * m, m), :]

        def copy(a, k, block, to, src=None):
            return pltpu.make_async_remote_copy(
                src_ref=rows(a, block) if src is None else src, dst_ref=rows(a, block),
                send_sem=send_sems.at[a, k], recv_sem=recv_sems.at[a, k], device_id=to, device_id_type=MESH)

        mine = [pltpu.make_async_copy(x_refs[a], rows(a, me), local_sems.at[a]) for a in range(n)]
        for cp in mine:
            cp.start()
        first = []
        for a in range(n):
            first.append(copy(a, 0, me, sibling, src=x_refs[a]))
            first += [copy(a, 1 + j, me, (*chip, c), src=x_refs[a]) for j, chip in enumerate(chips)]
        for cp in first:
            cp.start()
        passed = []
        for j, chip in enumerate(chips):
            for a in range(n):
                copy(a, 1 + j, (*chip, c), me).wait_recv()
                passed.append(copy(a, 4 + j, (*chip, c), sibling))
                passed[-1].start()
        for a in range(n):
            copy(a, 0, sibling, me).wait_recv()
            for j, chip in enumerate(chips):
                copy(a, 4 + j, (*chip, 1 - c), me).wait_recv()
        for cp in first + passed:
            cp.wait_send()
        for cp in mine:
            cp.wait()

    vmem = pl.BlockSpec(memory_space=pltpu.VMEM)
    return pl.pallas_call(
        body, name="gather_weights",
        out_shape=[jax.ShapeDtypeStruct((N_DEV * s.shape[0], s.shape[1]), s.dtype) for s in shards],
        in_specs=[vmem] * n, out_specs=[vmem] * n,
        scratch_shapes=[pltpu.SemaphoreType.DMA((n, 7)), pltpu.SemaphoreType.DMA((n, 7)), pltpu.SemaphoreType.DMA((n,))],
        compiler_params=_params(),
    )(*shards)


def _exchange(parts):
    n = len(parts)

    def body(*refs):
        in_refs, out_refs = refs[:n], refs[n:2 * n]
        send_sems, recv_sems, local_sems = refs[2 * n:]
        x, y, c = _mesh_pos()
        me = 4 * x + 2 * y + c
        own = [pltpu.make_async_copy(in_refs[a].at[me], out_refs[a].at[me], local_sems.at[a]) for a in range(n)]
        for cp in own:
            cp.start()
        copies = []
        for k in range(1, N_DEV):
            peer = (x ^ (k >> 2), y ^ ((k >> 1) & 1), c ^ (k & 1))
            peer_idx = 4 * peer[0] + 2 * peer[1] + peer[2]
            for a in range(n):
                copies.append(pltpu.make_async_remote_copy(
                    src_ref=in_refs[a].at[peer_idx], dst_ref=out_refs[a].at[me],
                    send_sem=send_sems.at[a, k - 1], recv_sem=recv_sems.at[a, k - 1], device_id=peer, device_id_type=MESH))
        for cp in copies:
            cp.start()
        for cp in copies:
            cp.wait()
        for cp in own:
            cp.wait()

    hbm = pl.BlockSpec(memory_space=pl.ANY)
    return pl.pallas_call(
        body, name="exchange_grads",
        out_shape=[jax.ShapeDtypeStruct(p.shape, p.dtype) for p in parts],
        in_specs=[hbm] * n, out_specs=[hbm] * n,
        scratch_shapes=[pltpu.SemaphoreType.DMA((n, 7)), pltpu.SemaphoreType.DMA((n, 7)), pltpu.SemaphoreType.DMA((n,))],
        compiler_params=_params(),
    )(*parts)


def _adamw(w, g, m, v):
    m = ADAM_B1 * m + (1.0 - ADAM_B1) * g
    v = ADAM_B2 * v + (1.0 - ADAM_B2) * (g * g)
    m_hat = m / (1.0 - ADAM_B1 ** ADAM_STEP)
    v_hat = v / (1.0 - ADAM_B2 ** ADAM_STEP)
    return -ADAM_LR * (m_hat / (jnp.sqrt(v_hat) + ADAM_EPS) + ADAM_WD * w), m, v


def _reduce_adamw(name, parts, w, m, v, row_tile):
    _, rows, cols = parts.shape

    def body(p_ref, w_ref, m_ref, v_ref, g_ref, d_ref, nm_ref, nv_ref):
        g = p_ref[0].astype(F32)
        for s in range(1, N_DEV):
            g = g + p_ref[s].astype(F32)
        g_ref[...] = g
        d_ref[...], nm_ref[...], nv_ref[...] = _adamw(w_ref[...], g, m_ref[...], v_ref[...])

    tile = pl.BlockSpec((row_tile, cols), lambda i: (i, 0))
    return pl.pallas_call(
        body, name=name, grid=(rows // row_tile,),
        in_specs=[pl.BlockSpec((N_DEV, row_tile, cols), lambda i: (0, i, 0)), tile, tile, tile],
        out_specs=[tile] * 4,
        out_shape=[jax.ShapeDtypeStruct((rows, cols), F32)] * 4,
        compiler_params=_params(dimension_semantics=("arbitrary",)),
    )(parts, w, m, v)


_SMALL = (("norm_g", 1024), ("ret_gn_g", 512), ("rwkv_mu", 2176), ("w0", 512), ("a0", 512), ("k_k", 512), ("k_a", 512),
          ("r_k", 512), ("rwkv_gn_g", 512), ("rwkv_gn_b", 512), ("final_norm_g", 1024))
_SMALL_LANES = sum(n for _, n in _SMALL) + 128
_WEIGHTS = ("norm_g", "w_in", "ret_gn_g", "rwkv_mu", "w_lora_up", "w0", "a_lora_up", "a0", "k_k", "k_a", "r_k", "rwkv_gn_g",
            "rwkv_gn_b", "w_out", "final_norm_g")


def _adamw_vectors(parts, wts, mom, var):
    k = len(_SMALL)

    def body(p_ref, *refs):
        w_refs, m_refs, v_refs, outs = refs[:k], refs[k:2 * k], refs[2 * k:3 * k], refs[3 * k:]
        g_all = p_ref[0]
        for s in range(1, N_DEV):
            g_all = g_all + p_ref[s]
        off = 0
        for i, (_, n) in enumerate(_SMALL):
            g = g_all[:, off:off + n]
            off += n
            outs[4 * i][...] = g
            outs[4 * i + 1][...], outs[4 * i + 2][...], outs[4 * i + 3][...] = _adamw(
                w_refs[i][...], g, m_refs[i][...], v_refs[i][...])
        outs[4 * k][...] = g_all[:, off:off + 128]

    vmem = pl.BlockSpec(memory_space=pltpu.VMEM)
    shapes = [jax.ShapeDtypeStruct((1, n), F32) for _, n in _SMALL for _ in range(4)] + [jax.ShapeDtypeStruct((1, 128), F32)]
    res = pl.pallas_call(
        body, name="adamw_vectors", out_shape=shapes,
        in_specs=[vmem] * (1 + 3 * k), out_specs=[vmem] * len(shapes), compiler_params=_params(),
    )(parts, *[wts[n] for n, _ in _SMALL], *[mom[n] for n, _ in _SMALL], *[var[n] for n, _ in _SMALL])
    return {n: res[4 * i:4 * i + 4] for i, (n, _) in enumerate(_SMALL)}, res[4 * k]


def kernel(x, norm_g, w_in, ret_gn_g, rwkv_mu, w_lora_up, w0, a_lora_up, a0, k_k, k_a, r_k, rwkv_gn_g, rwkv_gn_b, w_out, final_norm_g, loss_target, m_norm_g, m_w_in, m_ret_gn_g, m_rwkv_mu, m_w_lora_up, m_w0, m_a_lora_up, m_a0, m_k_k, m_k_a, m_r_k, m_rwkv_gn_g, m_rwkv_gn_b, m_w_out, m_final_norm_g, v_norm_g, v_w_in, v_ret_gn_g, v_rwkv_mu, v_w_lora_up, v_w0, v_a_lora_up, v_a0, v_k_k, v_k_a, v_r_k, v_rwkv_gn_g, v_rwkv_gn_b, v_w_out, v_final_norm_g):
    wts = dict(norm_g=norm_g, w_in=w_in, ret_gn_g=ret_gn_g, rwkv_mu=rwkv_mu, w_lora_up=w_lora_up, w0=w0, a_lora_up=a_lora_up,
               a0=a0, k_k=k_k, k_a=k_a, r_k=r_k, rwkv_gn_g=rwkv_gn_g, rwkv_gn_b=rwkv_gn_b, w_out=w_out,
               final_norm_g=final_norm_g)
    mom = dict(norm_g=m_norm_g, w_in=m_w_in, ret_gn_g=m_ret_gn_g, rwkv_mu=m_rwkv_mu, w_lora_up=m_w_lora_up, w0=m_w0,
               a_lora_up=m_a_lora_up, a0=m_a0, k_k=m_k_k, k_a=m_k_a, r_k=m_r_k, rwkv_gn_g=m_rwkv_gn_g,
               rwkv_gn_b=m_rwkv_gn_b, w_out=m_w_out, final_norm_g=m_final_norm_g)
    var = dict(norm_g=v_norm_g, w_in=v_w_in, ret_gn_g=v_ret_gn_g, rwkv_mu=v_rwkv_mu, w_lora_up=v_w_lora_up, w0=v_w0,
               a_lora_up=v_a_lora_up, a0=v_a0, k_k=v_k_k, k_a=v_k_a, r_k=v_r_k, rwkv_gn_g=v_rwkv_gn_g,
               rwkv_gn_b=v_rwkv_gn_b, w_out=v_w_out, final_norm_g=v_final_norm_g)
    shapes = {n: wts[n].shape for n in _WEIGHTS}

    g_in, g_out, g_wup, g_aup = _all_gather(
        [w_in[0].astype(BF16), w_out[0].astype(BF16), w_lora_up[0], a_lora_up[0]])
    w_in_bf = g_in.reshape(N_DEV, D_MODEL, SHARD_IN).transpose(1, 0, 2).reshape(D_MODEL, IN_COLS)
    wup_full = g_wup.reshape(N_DEV, LORA, SHARD_LORA).transpose(1, 0, 2).reshape(LORA, W)
    aup_full = g_aup.reshape(N_DEV, LORA, SHARD_LORA).transpose(1, 0, 2).reshape(LORA, W)

    loss, dx, g = _local_grads(
        x[0], loss_target[0], norm_g, w_in_bf, ret_gn_g, rwkv_mu, wup_full, w0, aup_full, a0, k_k, k_a,
        r_k.reshape(1, W), rwkv_gn_g, rwkv_gn_b, g_out, final_norm_g.reshape(1, D_MODEL))

    small = jnp.concatenate([g[n] for n, _ in _SMALL] + [loss], axis=1)
    parts = _exchange([
        g["w_in"].reshape(D_MODEL, N_DEV, SHARD_IN).transpose(1, 0, 2).astype(BF16),
        g["w_out"].reshape(N_DEV, SHARD_OUT, D_MODEL).astype(BF16),
        g["w_lora_up"].reshape(LORA, N_DEV, SHARD_LORA).transpose(1, 0, 2),
        g["a_lora_up"].reshape(LORA, N_DEV, SHARD_LORA).transpose(1, 0, 2),
        jnp.broadcast_to(small[None], (N_DEV, 1, _SMALL_LANES))])
    res = {}
    res["w_in"] = _reduce_adamw("adamw_w_in", parts[0], w_in[0], m_w_in[0], v_w_in[0], 256)
    res["w_out"] = _reduce_adamw("adamw_w_out", parts[1], w_out[0], m_w_out[0], v_w_out[0], SHARD_OUT)
    res["w_lora_up"] = _reduce_adamw("adamw_w_lora_up", parts[2], w_lora_up[0], m_w_lora_up[0], v_w_lora_up[0], LORA)
    res["a_lora_up"] = _reduce_adamw("adamw_a_lora_up", parts[3], a_lora_up[0], m_a_lora_up[0], v_a_lora_up[0], LORA)
    as_row = lambda d: {n: d[n].reshape(1, size) for n, size in _SMALL}
    vec, loss_row = _adamw_vectors(parts[4], as_row(wts), as_row(mom), as_row(var))
    res.update(vec)
    res = {n: [t.reshape(shapes[n]) for t in res[n]] for n in _WEIGHTS}
    return (loss_row[0, 0], dx[None], *[res[n][0] for n in _WEIGHTS], *[res[n][1] for n in _WEIGHTS],
            *[res[n][2] for n in _WEIGHTS], *[res[n][3] for n in _WEIGHTS])
```

```python
import functools

import numpy as np
import jax
import jax.numpy as jnp
from jax import lax
from jax.experimental import pallas as pl
from jax.experimental.pallas import tpu as pltpu

F32 = jnp.float32
BF16 = jnp.bfloat16

D_MODEL = 1024
CHUNK = 64
RET_HEADS = 4
RET_DV = 128
RET_DK = 64
RET_QK = 256
RET_WIDTH = 512
RWKV_WIDTH = 512
RWKV_HEAD = 64
RWKV_HEADS = 8
LORA = 64
RET_COLS = 2 * RET_QK + 2 * RET_WIDTH
RWKV_COLS = 4 * RWKV_WIDTH + 2 * LORA
IN_COLS = RET_COLS + RWKV_COLS
ROPE_BASE = 10000.0
RMS_EPS = 1e-6
RET_GN_EPS = 1e-5
RWKV_GN_EPS = 64e-5
ADAM_LR = 0.001
ADAM_B1 = 0.9
ADAM_B2 = 0.999
ADAM_EPS = 1e-08
ADAM_WD = 0.01
ADAM_STEP = 10
N_DEV = 8
SHARD_IN = IN_COLS // N_DEV
SHARD_OUT = D_MODEL // N_DEV
SHARD_LORA = RWKV_WIDTH // N_DEV
VMEM_LIMIT = 56 * 1024 * 1024
TOK_TILE = 256
WKV_CHUNK = 32

MESH = pl.DeviceIdType.MESH


def _dot_bf(a, b):
    return jnp.dot(a.astype(BF16), b.astype(BF16), preferred_element_type=F32)


def _dot_nt_bf(a, b):
    return lax.dot_general(a.astype(BF16), b.astype(BF16), (((1,), (1,)), ((), ())), preferred_element_type=F32)


def _dot_tn_bf(a, b):
    return lax.dot_general(a.astype(BF16), b.astype(BF16), (((0,), (0,)), ((), ())), preferred_element_type=F32)


@jax.custom_vjp
def _mm(a, b):
    return _dot_bf(a, b)


@jax.custom_vjp
def _mm_nt(a, b):
    return _dot_nt_bf(a, b)


@jax.custom_vjp
def _mm_tn(a, b):
    return _dot_tn_bf(a, b)


_mm.defvjp(lambda a, b: (_dot_bf(a, b), (a, b)), lambda res, g: (_dot_nt_bf(g, res[1]), _dot_tn_bf(res[0], g)))
_mm_nt.defvjp(lambda a, b: (_dot_nt_bf(a, b), (a, b)), lambda res, g: (_dot_bf(g, res[1]), _dot_tn_bf(g, res[0])))
_mm_tn.defvjp(lambda a, b: (_dot_tn_bf(a, b), (a, b)), lambda res, g: (_dot_nt_bf(res[1], g), _dot_bf(res[0], g)))


def _trunc(x):
    return lax.bitcast_convert_type(lax.bitcast_convert_type(x, jnp.uint32) & jnp.uint32(0xFFFF0000), F32)


def _two_piece(x):
    hi = _trunc(x)
    return jnp.concatenate([hi, x - hi], axis=1)


def _mix_raw(x, mat2):
    return _unstack(jnp.dot(_two_piece(_stack(x)), mat2, preferred_element_type=F32))


@jax.custom_vjp
def _head_mix(x, mat2):
    return _mix_raw(x, mat2)


_head_mix.defvjp(lambda x, mat2: (_mix_raw(x, mat2), mat2), lambda mat2, g: (_mix_raw(g, mat2), jnp.zeros_like(mat2)))


def _swap_halves(x):
    lane = lax.broadcasted_iota(jnp.int32, x.shape, 1)
    return jnp.where((lane & (RET_DK - 1)) < RET_DK // 2, pltpu.roll(x, RET_QK - RET_DK // 2, axis=1),
                     pltpu.roll(x, RET_DK // 2, axis=1))


@jax.custom_vjp
def _rot(x):
    return _swap_halves(x)


_rot.defvjp(lambda x: (_swap_halves(x), None), lambda _, g: (_swap_halves(g),))


def _params(**kw):
    return pltpu.CompilerParams(vmem_limit_bytes=VMEM_LIMIT, **kw)


def _full(shape):
    nd = len(shape)
    return pl.BlockSpec(shape, lambda i, _nd=nd: (0,) * _nd)


def _rows(tile, width):
    return pl.BlockSpec((tile, width), lambda i: (i, 0))


def _block_mix(n, blk, scale=1.0):
    idx = np.arange(n) // blk
    m = (idx[:, None] == idx[None, :]).astype(np.float32) * scale
    return jnp.asarray(np.concatenate([m, m], axis=0))


def _rope_tables(T):
    half = RET_DK // 2
    expo = -np.arange(half, dtype=np.float32) / np.float32(half)
    freqs = np.exp(expo * np.float32(np.log(ROPE_BASE))).astype(np.float32)
    ang = np.arange(T, dtype=np.float32)[:, None] * freqs[None, :]
    cos, sin = np.cos(ang).astype(np.float32), np.sin(ang).astype(np.float32)
    cos_h = np.concatenate([cos, cos], axis=1)
    sin_h = np.concatenate([-sin, sin], axis=1)
    cos_t = np.tile(cos_h, (1, RET_HEADS))
    sin_t = np.tile(sin_h, (1, RET_HEADS))
    return jnp.asarray(cos_t), jnp.asarray(sin_t)


def _ret_tables():
    h = np.arange(RET_HEADS, dtype=np.float32)
    lg = np.log(1.0 - np.exp2(-5.0 - h)).astype(np.float32)
    idx = np.arange(CHUNK, dtype=np.float32)
    intra = np.exp(lg[:, None, None] * np.abs(idx[:, None] - idx[None, :])).astype(np.float32)
    q_dec = np.exp(lg[:, None] * (idx[None, :] + 1.0)).astype(np.float32)
    k_dec = np.exp(lg[:, None] * (CHUNK - 1.0 - idx[None, :])).astype(np.float32)
    chunk_dec = np.exp(lg * CHUNK).astype(np.float32)
    lane_head = np.arange(RET_QK) // RET_DK
    mask = (lane_head[None, :] == np.arange(RET_HEADS)[:, None]).astype(np.float32)
    m = np.broadcast_to(mask[:, None, :], (RET_HEADS, CHUNK, RET_QK)).copy()
    qd = m * q_dec[:, :, None]
    kd = m * k_dec[:, :, None]
    return jnp.asarray(intra), jnp.asarray(m), jnp.asarray(qd), jnp.asarray(kd), [float(c) for c in chunk_dec]


def _rmsnorm(x, g):
    return x * lax.rsqrt(jnp.mean(x * x, axis=-1, keepdims=True) + RMS_EPS) * g


def _inproj(x, norm_g, w_in):
    T = x.shape[0]

    def body(x_ref, g_ref, w_ref, pr_ref, pw_ref, u_ref):
        ub = _rmsnorm(x_ref[...], g_ref[...]).astype(BF16)
        u_ref[...] = ub
        pr_ref[...] = jnp.dot(ub, w_ref[:, :RET_COLS], preferred_element_type=F32)
        pw_ref[...] = jnp.dot(ub, w_ref[:, RET_COLS:], preferred_element_type=F32)

    return pl.pallas_call(
        body, name="inproj", grid=(T // TOK_TILE,),
        in_specs=[_rows(TOK_TILE, D_MODEL), _full((1, D_MODEL)), _full((D_MODEL, IN_COLS))],
        out_specs=[_rows(TOK_TILE, RET_COLS), _rows(TOK_TILE, RWKV_COLS), _rows(TOK_TILE, D_MODEL)],
        out_shape=[jax.ShapeDtypeStruct((T, RET_COLS), F32), jax.ShapeDtypeStruct((T, RWKV_COLS), F32),
                   jax.ShapeDtypeStruct((T, D_MODEL), BF16)],
        compiler_params=_params(dimension_semantics=("arbitrary",)),
    )(x, norm_g, w_in)


def _ret_chunk(pq, pk, v_heads, s_heads, cos_t, sin_t, dec, hm, qd, kd, chunk_dec):
    q = pq * cos_t + _rot(pq) * sin_t
    k = (pk * cos_t + _rot(pk) * sin_t) * (RET_DK ** -0.5)
    outs, s_out = [], []
    for h in range(RET_HEADS):
        sc = _mm_nt(q * hm[h], k * hm[h]) * dec[h]
        intra = _mm(sc, v_heads[h])
        kv = _mm_tn(k * kd[h], v_heads[h])
        inter = _mm(q * qd[h], s_heads[h])
        outs.append(intra + inter)
        s_out.append(s_heads[h] * chunk_dec[h] + kv)
    return tuple(outs), tuple(s_out)


def _ret_specs():
    const = [_full((RET_HEADS, CHUNK, CHUNK)), _full((RET_HEADS, CHUNK, RET_QK)),
             _full((RET_HEADS, CHUNK, RET_QK)), _full((RET_HEADS, CHUNK, RET_QK))]
    return const


def _ret_fwd(p_ret, tabs):
    T = p_ret.shape[0]
    nc = T // CHUNK
    cos_t, sin_t, dec, hm, qd, kd, chunk_dec = tabs

    def body(p_ref, cos_ref, sin_ref, dec_ref, hm_ref, qd_ref, kd_ref, out_ref, sin_save_ref, s_scr):
        @pl.when(pl.program_id(0) == 0)
        def _():
            s_scr[...] = jnp.zeros_like(s_scr)

        s_heads = tuple(s_scr[h] for h in range(RET_HEADS))
        for h in range(RET_HEADS):
            sin_save_ref[0, h] = s_heads[h]
        pq = p_ref[:, 0:RET_QK]
        pk = p_ref[:, RET_QK:2 * RET_QK]
        v_heads = tuple(p_ref[:, 2 * RET_QK + RET_DV * h:2 * RET_QK + RET_DV * (h + 1)] for h in range(RET_HEADS))
        outs, s_out = _ret_chunk(pq, pk, v_heads, s_heads, cos_ref[...], sin_ref[...], dec_ref[...],
                                 hm_ref[...], qd_ref[...], kd_ref[...], chunk_dec)
        for h in range(RET_HEADS):
            out_ref[:, RET_DV * h:RET_DV * (h + 1)] = outs[h]
            s_scr[h] = s_out[h]

    return pl.pallas_call(
        body, name="ret_fwd", grid=(nc,),
        in_specs=[pl.BlockSpec((CHUNK, RET_COLS), lambda i: (i, 0)), _rows(CHUNK, RET_QK), _rows(CHUNK, RET_QK)] + _ret_specs(),
        out_specs=[_rows(CHUNK, RET_WIDTH), pl.BlockSpec((1, RET_HEADS, RET_QK, RET_DV), lambda i: (i, 0, 0, 0))],
        out_shape=[jax.ShapeDtypeStruct((T, RET_WIDTH), F32), jax.ShapeDtypeStruct((nc, RET_HEADS, RET_QK, RET_DV), F32)],
        scratch_shapes=[pltpu.VMEM((RET_HEADS, RET_QK, RET_DV), F32)],
        compiler_params=_params(dimension_semantics=("arbitrary",)),
    )(p_ret, cos_t, sin_t, dec, hm, qd, kd)


def _ret_bwd(p_ret, s_saved, d_ret, tabs):
    T = p_ret.shape[0]
    nc = T // CHUNK
    cos_t, sin_t, dec, hm, qd, kd, chunk_dec = tabs

    def body(p_ref, s_ref, dret_ref, cos_ref, sin_ref, dec_ref, hm_ref, qd_ref, kd_ref, dp_ref, ds_scr):
        @pl.when(pl.program_id(0) == 0)
        def _():
            ds_scr[...] = jnp.zeros_like(ds_scr)

        pq = p_ref[:, 0:RET_QK]
        pk = p_ref[:, RET_QK:2 * RET_QK]
        v_heads = tuple(p_ref[:, 2 * RET_QK + RET_DV * h:2 * RET_QK + RET_DV * (h + 1)] for h in range(RET_HEADS))
        s_heads = tuple(s_ref[0, h] for h in range(RET_HEADS))
        consts = (cos_ref[...], sin_ref[...], dec_ref[...], hm_ref[...], qd_ref[...], kd_ref[...])
        _, vjp = jax.vjp(lambda a, b, c, d: _ret_chunk(a, b, c, d, *consts, chunk_dec), pq, pk, v_heads, s_heads)
        d_out = tuple(dret_ref[:, RET_DV * h:RET_DV * (h + 1)] for h in range(RET_HEADS))
        d_s = tuple(ds_scr[h] for h in range(RET_HEADS))
        dq, dk, dv, ds_in = vjp((d_out, d_s))
        dp_ref[:, 0:RET_QK] = dq
        dp_ref[:, RET_QK:2 * RET_QK] = dk
        for h in range(RET_HEADS):
            dp_ref[:, 2 * RET_QK + RET_DV * h:2 * RET_QK + RET_DV * (h + 1)] = dv[h]
            ds_scr[h] = ds_in[h]

    rev = lambda i: (nc - 1 - i, 0)
    return pl.pallas_call(
        body, name="ret_bwd", grid=(nc,),
        in_specs=[pl.BlockSpec((CHUNK, RET_COLS), rev),
                  pl.BlockSpec((1, RET_HEADS, RET_QK, RET_DV), lambda i: (nc - 1 - i, 0, 0, 0)),
                  pl.BlockSpec((CHUNK, RET_WIDTH), rev), pl.BlockSpec((CHUNK, RET_QK), rev), pl.BlockSpec((CHUNK, RET_QK), rev)]
        + _ret_specs(),
        out_specs=pl.BlockSpec((CHUNK, 2 * RET_QK + RET_WIDTH), rev),
        out_shape=jax.ShapeDtypeStruct((T, 2 * RET_QK + RET_WIDTH), F32),
        scratch_shapes=[pltpu.VMEM((RET_HEADS, RET_QK, RET_DV), F32)],
        compiler_params=_params(dimension_semantics=("arbitrary",)),
    )(p_ret, s_saved, d_ret, cos_t, sin_t, dec, hm, qd, kd)


def _wkv_consts():
    lane = lax.broadcasted_iota(jnp.int32, (RWKV_HEAD, RWKV_WIDTH), 1)
    sub = lax.broadcasted_iota(jnp.int32, (RWKV_HEAD, RWKV_WIDTH), 0)
    diag = ((lane & (RWKV_HEAD - 1)) == sub).astype(F32)
    r = lax.broadcasted_iota(jnp.int32, (3 * 128, 128), 0)
    c = lax.broadcasted_iota(jnp.int32, (3 * 128, 128), 1)
    ones = (((r & 127) >> 6) == (c >> 6)).astype(BF16)
    return diag, ones


def _stack(x):
    return jnp.concatenate([x[:, 128 * p:128 * (p + 1)] for p in range(4)], axis=0)


def _unstack(y):
    n = y.shape[0] // 4
    return jnp.concatenate([y[n * p:n * (p + 1)] for p in range(4)], axis=1)


def _split(x, n):
    pieces = []
    for _ in range(n):
        p = x.astype(BF16)
        pieces.append(p)
        x = x - p.astype(F32)
    return pieces


def _lane_sum(x, ones):
    return _unstack(jnp.dot(_two_piece(_stack(x)), ones[:256].astype(F32), preferred_element_type=F32))


def _lane_sum_bf(x, ones):
    return _unstack(jnp.dot(_stack(x).astype(BF16), ones[:128], preferred_element_type=F32))


def _expand(row, diag_bf, ones, n):
    lhs = jnp.concatenate([_stack(jnp.broadcast_to(p, diag_bf.shape) * diag_bf) for p in _split(row, n)], axis=1)
    return _unstack(jnp.dot(lhs, ones[:128 * n], preferred_element_type=F32))


def _colsum(x):
    return jnp.sum(x, axis=0, keepdims=True)


def _head_sums(x, ones):
    return _unstack(jnp.dot(jnp.concatenate(_split(_stack(x), 3), axis=1), ones, preferred_element_type=F32))


def _wkv_fwd(r, w, k, v, kap, a):
    T = r.shape[0]
    C = WKV_CHUNK
    nc = T // C

    def body(r_ref, w_ref, k_ref, v_ref, kap_ref, a_ref, o_ref, s_all_ref, sa_all_ref, s_scr, vf_scr):
        @pl.when(pl.program_id(0) == 0)
        def _():
            s_scr[...] = jnp.zeros_like(s_scr)

        diag, ones = _wkv_consts()
        diag_bf = diag.astype(BF16)
        rr, ww, kk, vv, kap_, aa = (ref[...] for ref in (r_ref, w_ref, k_ref, v_ref, kap_ref, a_ref))
        bb = kap_ * aa
        c1 = _head_sums(pltpu.roll(bb, 1, axis=0) * kap_, ones)
        row = lambda x, t: x[t:t + 1]

        for t in range(C):
            vf_scr[t] = _expand(row(vv, t), diag_bf, ones, 2)

        s_prev = s_scr[...]
        sa = _lane_sum(s_prev * (-row(kap_, 0)), ones)
        ls, rows = None, []

        def emit_o(t, s_t):
            rows.append(_colsum(_lane_sum_bf(s_t * row(rr, t), ones) * diag))
            if t % 8 == 7:
                o_ref[t - 7:t + 1, :] = jnp.concatenate(rows, axis=0)
                rows.clear()

        for t in range(C):
            u = s_prev * row(ww, t) + vf_scr[t] * row(kk, t)
            if t > 0:
                sa = ls - sa * row(c1, t)
            if t + 1 < C:
                ls = _lane_sum(u * (-row(kap_, t + 1)), ones)
            if t > 0:
                emit_o(t - 1, s_prev)
            s_prev = u + sa * row(bb, t)
            s_all_ref[t] = s_prev
            sa_all_ref[t] = sa.astype(BF16)
        emit_o(C - 1, s_prev)
        s_scr[...] = s_prev

    spec = _rows(C, RWKV_WIDTH)
    return pl.pallas_call(
        body, name="wkv_fwd", grid=(nc,),
        in_specs=[spec] * 6,
        out_specs=[spec, pl.BlockSpec((C, RWKV_HEAD, RWKV_WIDTH), lambda i: (i, 0, 0)),
                   pl.BlockSpec((C, RWKV_HEAD, RWKV_WIDTH), lambda i: (i, 0, 0))],
        out_shape=[jax.ShapeDtypeStruct((T, RWKV_WIDTH), F32), jax.ShapeDtypeStruct((T, RWKV_HEAD, RWKV_WIDTH), F32),
                   jax.ShapeDtypeStruct((T, RWKV_HEAD, RWKV_WIDTH), BF16)],
        scratch_shapes=[pltpu.VMEM((RWKV_HEAD, RWKV_WIDTH), F32), pltpu.VMEM((C, RWKV_HEAD, RWKV_WIDTH), F32)],
        compiler_params=_params(dimension_semantics=("arbitrary",)),
    )(r, w, k, v, kap, a)


def _wkv_bwd(r, w, k, v, kap, a, s_all, sa_all, d_o):
    T = r.shape[0]
    C = WKV_CHUNK
    nc = T // C

    def body(r_ref, w_ref, k_ref, v_ref, kap_ref, a_ref, s_ref, s_before_ref, sa_ref, do_ref,
             dr_ref, dw_ref, dk_ref, dv_ref, dkap_ref, da_ref, ds_scr, vf_scr, dof_scr):
        first_chunk = pl.program_id(0) == nc - 1

        @pl.when(pl.program_id(0) == 0)
        def _():
            ds_scr[...] = jnp.zeros_like(ds_scr)

        diag, ones = _wkv_consts()
        diag_bf = diag.astype(BF16)
        rr, ww, kk, vv, kap_, aa, dd = (ref[...] for ref in (r_ref, w_ref, k_ref, v_ref, kap_ref, a_ref, do_ref))
        bb = kap_ * aa
        e1 = _head_sums(pltpu.roll(kap_, C - 1, axis=0) * bb, ones)
        row = lambda x, t: x[t:t + 1]

        def state_before(t):
            return s_ref[t - 1] if t > 0 else jnp.where(first_chunk, 0.0, s_before_ref[0])

        for t in range(C):
            vf_scr[t] = _expand(row(vv, t), diag_bf, ones, 1)
            dof_scr[t] = _expand(row(dd, t), diag_bf, ones, 2)

        d_sn, dsa, rows = None, None, [None] * C

        def emit_rows(t, d_sn_t, dsa_t):
            s_prev, dof = state_before(t), dof_scr[t]
            dv = _colsum(_lane_sum_bf(d_sn_t * row(kk, t), ones) * diag)
            db = _colsum(d_sn_t * sa_ref[t].astype(F32))
            rows[t] = (_colsum(s_ref[t] * dof), _colsum(d_sn_t * s_prev), _colsum(d_sn_t * vf_scr[t]), dv,
                       db * row(aa, t) - _colsum(dsa_t * s_prev), db * row(kap_, t))
            if t % 8 == 0:
                for j, ref in enumerate((dr_ref, dw_ref, dk_ref, dv_ref, dkap_ref, da_ref)):
                    ref[t:t + 8, :] = jnp.concatenate([rows[u][j] for u in range(t, t + 8)], axis=0)

        for t in reversed(range(C)):
            dof = dof_scr[t]
            if t == C - 1:
                d_sn = ds_scr[...] + dof * row(rr, t)
                dsa = _lane_sum(d_sn * row(bb, t), ones)
            else:
                v_t = d_sn * row(ww, t + 1) + dof * row(rr, t)
                ls = _lane_sum(v_t * row(bb, t), ones)
                emit_rows(t + 1, d_sn, dsa)
                d_sn = v_t - dsa * row(kap_, t + 1)
                dsa = ls - dsa * row(e1, t)
        emit_rows(0, d_sn, dsa)
        d_s = d_sn * row(ww, 0) - dsa * row(kap_, 0)
        ds_scr[...] = d_s

    spec = pl.BlockSpec((C, RWKV_WIDTH), lambda i: (nc - 1 - i, 0))
    states = pl.BlockSpec((C, RWKV_HEAD, RWKV_WIDTH), lambda i: (nc - 1 - i, 0, 0))
    before = pl.BlockSpec((1, RWKV_HEAD, RWKV_WIDTH), lambda i: (jnp.maximum((nc - 1 - i) * C - 1, 0), 0, 0))
    big = pltpu.VMEM((C, RWKV_HEAD, RWKV_WIDTH), F32)
    return pl.pallas_call(
        body, name="wkv_bwd", grid=(nc,),
        in_specs=[spec] * 6 + [states, before, states, spec],
        out_specs=[spec] * 6,
        out_shape=[jax.ShapeDtypeStruct((T, RWKV_WIDTH), F32)] * 6,
        scratch_shapes=[pltpu.VMEM((RWKV_HEAD, RWKV_WIDTH), F32), big, big],
        compiler_params=_params(dimension_semantics=("arbitrary",)),
    )(r, w, k, v, kap, a, s_all, s_all, sa_all, d_o)


W = RWKV_WIDTH


def _softplus(y):
    return jnp.maximum(y, 0.0) + jnp.log(1.0 + jnp.exp(-jnp.abs(y)))


def _prep_fn(kr, xwa, w0, a0, k_k, k_a, wup_pad, aup_pad, ones64):
    w_log = -_softplus(-(w0 + _mm(jnp.tanh(xwa), wup_pad))) - 0.5
    decay = jnp.exp(-jnp.exp(w_log))
    a = jax.nn.sigmoid(a0 + _mm(xwa, aup_pad))
    kk = kr * k_k
    kap = kk / jnp.maximum(jnp.sqrt(_head_mix(kk * kk, ones64)), 1e-12)
    k = kr * (1.0 + (a - 1.0) * k_a)
    return decay, k, kap, a


def _shift_down(p, first_row):
    rows = lax.broadcasted_iota(jnp.int32, p.shape, 0)
    return jnp.where(rows == 0, first_row, pltpu.roll(p, 1, axis=0))


def _shift_up(z, last_row):
    n = z.shape[0]
    rows = lax.broadcasted_iota(jnp.int32, z.shape, 0)
    return jnp.where(rows == n - 1, last_row, pltpu.roll(z, n - 1, axis=0))


def _prev_block_spec():
    return pl.BlockSpec((8, RWKV_COLS), lambda i: (jnp.maximum(i * (TOK_TILE // 8) - 1, 0), 0))


def _mixed(p_ref, prev8_ref, mu_ref, first_tile):
    p = p_ref[...]
    first_row = jnp.where(first_tile, 0.0, prev8_ref[7:8, :])
    prev = _shift_down(p, first_row)
    return p, prev, p + mu_ref[...] * (prev - p)


def _prep_fwd(p_rwkv, mu, w0, a0, k_k, k_a, wup_pad, aup_pad, ones64):
    T = p_rwkv.shape[0]

    def body(p_ref, prev8_ref, mu_ref, w0_ref, a0_ref, kk_ref, ka_ref, wup_ref, aup_ref, ones_ref,
             r_ref, w_ref, k_ref, v_ref, kap_ref, a_ref, g_ref):
        _, _, ps = _mixed(p_ref, prev8_ref, mu_ref, pl.program_id(0) == 0)
        decay, k, kap, a = _prep_fn(ps[:, W:2 * W], ps[:, 4 * W:], w0_ref[...], a0_ref[...], kk_ref[...], ka_ref[...],
                                    wup_ref[...], aup_ref[...], ones_ref[...])
        r_ref[...] = ps[:, 0:W]
        w_ref[...] = decay
        k_ref[...] = k
        v_ref[...] = ps[:, 2 * W:3 * W]
        kap_ref[...] = kap
        a_ref[...] = a
        g_ref[...] = ps[:, 3 * W:4 * W]

    vec = _full((1, W))
    return pl.pallas_call(
        body, name="prep_fwd", grid=(T // TOK_TILE,),
        in_specs=[_rows(TOK_TILE, RWKV_COLS), _prev_block_spec(), _full((1, RWKV_COLS)), vec, vec, vec, vec,
                  _full((2 * LORA, W)), _full((2 * LORA, W)), _full((256, 128))],
        out_specs=[_rows(TOK_TILE, W)] * 7,
        out_shape=[jax.ShapeDtypeStruct((T, W), F32)] * 7,
        compiler_params=_params(dimension_semantics=("arbitrary",)),
    )(p_rwkv, p_rwkv, mu, w0, a0, k_k, k_a, wup_pad, aup_pad, ones64)


def _prep_bwd(p_rwkv, mu, w0, a0, k_k, k_a, wup_pad, aup_pad, ones64, dr, dw, dk, dv, dkap, da, dg, dr2, dk2, dv2):
    T = p_rwkv.shape[0]
    nt = T // TOK_TILE

    def body(p_ref, prev8_ref, mu_ref, w0_ref, a0_ref, kk_ref, ka_ref, wup_ref, aup_ref, ones_ref,
             dr_ref, dw_ref, dk_ref, dv_ref, dkap_ref, da_ref, dg_ref, dr2_ref, dk2_ref, dv2_ref,
             dp_ref, dmu_ref, dw0_ref, da0_ref, dkk_ref, dka_ref, dwup_ref, daup_ref, zrow_scr):
        i = pl.program_id(0)
        accs = (dmu_ref, dw0_ref, da0_ref, dkk_ref, dka_ref, dwup_ref, daup_ref)

        @pl.when(i == 0)
        def _():
            zrow_scr[...] = jnp.zeros_like(zrow_scr)
            for ref in accs:
                ref[...] = jnp.zeros_like(ref)

        p, prev, ps = _mixed(p_ref, prev8_ref, mu_ref, i == nt - 1)
        ones = ones_ref[...]
        _, vjp = jax.vjp(lambda *args: _prep_fn(*args, ones), ps[:, W:2 * W], ps[:, 4 * W:], w0_ref[...], a0_ref[...],
                         kk_ref[...], ka_ref[...], wup_ref[...], aup_ref[...])
        dkr, dxwa, dw0, da0, dkk, dka, dwup, daup = vjp(
            (dw_ref[...], dk_ref[...] + dk2_ref[...], dkap_ref[...], da_ref[...]))
        dps = jnp.concatenate([dr_ref[...] + dr2_ref[...], dkr, dv_ref[...] + dv2_ref[...], dg_ref[...], dxwa], axis=1)
        z = dps * mu_ref[...]
        dp_ref[...] = dps - z + _shift_up(z, zrow_scr[0:1, :])
        zrow_scr[0:1, :] = z[0:1, :]
        for ref, val in zip(accs, (_colsum(dps * (prev - p)), dw0, da0, dkk, dka, dwup, daup)):
            ref[...] += val

    rev = lambda i: (nt - 1 - i, 0)
    vec = _full((1, W))
    lora = _full((2 * LORA, W))
    tile = pl.BlockSpec((TOK_TILE, W), rev)
    prev8 = pl.BlockSpec((8, RWKV_COLS), lambda i: (jnp.maximum((nt - 1 - i) * (TOK_TILE // 8) - 1, 0), 0))
    return pl.pallas_call(
        body, name="prep_bwd", grid=(nt,),
        in_specs=[pl.BlockSpec((TOK_TILE, RWKV_COLS), rev), prev8, _full((1, RWKV_COLS)), vec, vec, vec, vec, lora, lora,
                  _full((256, 128))] + [tile] * 10,
        out_specs=[pl.BlockSpec((TOK_TILE, RWKV_COLS), rev), _full((1, RWKV_COLS)), vec, vec, vec, vec, lora, lora],
        out_shape=[jax.ShapeDtypeStruct((T, RWKV_COLS), F32), jax.ShapeDtypeStruct((1, RWKV_COLS), F32)]
        + [jax.ShapeDtypeStruct((1, W), F32)] * 4 + [jax.ShapeDtypeStruct((2 * LORA, W), F32)] * 2,
        scratch_shapes=[pltpu.VMEM((8, RWKV_COLS), F32)],
        compiler_params=_params(dimension_semantics=("arbitrary",)),
    )(p_rwkv, p_rwkv, mu, w0, a0, k_k, k_a, wup_pad, aup_pad, ones64, dr, dw, dk, dv, dkap, da, dg, dr2, dk2, dv2)


def _silu(x):
    return x * jax.nn.sigmoid(x)


def _post_y(o, r, k, v, g_rw, ret_raw, g_ret, ret_gn_g, gn_g, gn_b, r_k, avg128, avg64, ones64):
    xc = ret_raw - _head_mix(ret_raw, avg128)
    ret = xc * lax.rsqrt(_head_mix(xc * xc, avg128) + RET_GN_EPS)
    y_ret = _silu(g_ret) * (ret * ret_gn_g)
    oc = o - _head_mix(o, avg64)
    on = oc * lax.rsqrt(_head_mix(oc * oc, avg64) + RWKV_GN_EPS) * gn_g + gn_b
    bonus = _head_mix(r * k * r_k, ones64) * v
    y_rwkv = _silu(g_rw) * (on + bonus)
    return y_ret, y_rwkv


def _post_loss(h, final_g, target):
    err = _rmsnorm(h, final_g) - target
    return 0.5 * jnp.sum(jnp.mean(err * err, axis=-1))


def _post(o, r, k, v, g_rw, ret_raw, p_ret, x, target, ret_gn_g, gn_g, gn_b, r_k, final_g, w_out, avg128, avg64, ones64):
    T = x.shape[0]
    n_tok_out = 8

    def body(o_ref, r_ref, k_ref, v_ref, grw_ref, ret_ref, gret_ref, x_ref, tgt_ref, rg_ref, gg_ref, gb_ref, rk_ref, fg_ref,
             wo_ref, a128_ref, a64_ref, ones_ref, *outs):
        tok_outs, (dwo_ref, drg_ref, dgg_ref, dgb_ref, drk_ref, dfg_ref, loss_ref) = outs[:n_tok_out], outs[n_tok_out:]
        accs = (dwo_ref, drg_ref, dgg_ref, dgb_ref, drk_ref, dfg_ref, loss_ref)

        @pl.when(pl.program_id(0) == 0)
        def _():
            for ref in accs:
                ref[...] = jnp.zeros_like(ref)

        consts = (a128_ref[...], a64_ref[...], ones_ref[...])
        (y_ret, y_rwkv), vjp = jax.vjp(
            lambda *args: _post_y(*args, *consts), o_ref[...], r_ref[...], k_ref[...], v_ref[...], grw_ref[...], ret_ref[...],
            gret_ref[...], rg_ref[...], gg_ref[...], gb_ref[...], rk_ref[...])
        h = x_ref[...] + _dot_bf(y_ret, wo_ref[0:RET_WIDTH, :]) + _dot_bf(y_rwkv, wo_ref[RET_WIDTH:, :])
        loss, (dh, dfg) = jax.value_and_grad(_post_loss, argnums=(0, 1))(h, fg_ref[...], tgt_ref[...])
        dy_ret = _dot_nt_bf(dh, wo_ref[0:RET_WIDTH, :])
        dy_rwkv = _dot_nt_bf(dh, wo_ref[RET_WIDTH:, :])
        do, dr, dk, dv, dgrw, dret, dgret, drg, dgg, dgb, drk = vjp((dy_ret, dy_rwkv))
        for ref, val in zip(tok_outs, (dh, do, dr, dk, dv, dgrw, dret, dgret)):
            ref[...] = val
        dwo_ref[0:RET_WIDTH, :] += _dot_tn_bf(y_ret, dh)
        dwo_ref[RET_WIDTH:, :] += _dot_tn_bf(y_rwkv, dh)
        for ref, val in zip(accs[1:], (drg, dgg, dgb, drk, dfg, jnp.full((1, 128), loss, F32))):
            ref[...] += val

    tile = _rows(TOK_TILE, W)
    wide = _rows(TOK_TILE, D_MODEL)
    vec = _full((1, W))
    sq = _full((256, 128))
    return pl.pallas_call(
        body, name="post", grid=(T // TOK_TILE,),
        in_specs=[tile] * 6 + [pl.BlockSpec((TOK_TILE, W), lambda i: (i, 2)), wide, wide, vec, vec, vec, vec,
                               _full((1, D_MODEL)), _full((D_MODEL, D_MODEL)), sq, sq, sq],
        out_specs=[wide] + [tile] * 7 + [_full((D_MODEL, D_MODEL)), vec, vec, vec, vec, _full((1, D_MODEL)), _full((1, 128))],
        out_shape=[jax.ShapeDtypeStruct((T, D_MODEL), F32)] + [jax.ShapeDtypeStruct((T, W), F32)] * 7
        + [jax.ShapeDtypeStruct((D_MODEL, D_MODEL), F32)] + [jax.ShapeDtypeStruct((1, W), F32)] * 4
        + [jax.ShapeDtypeStruct((1, D_MODEL), F32), jax.ShapeDtypeStruct((1, 128), F32)],
        compiler_params=_params(dimension_semantics=("arbitrary",)),
    )(o, r, k, v, g_rw, ret_raw, p_ret, x, target, ret_gn_g, gn_g, gn_b, r_k, final_g, w_out, avg128, avg64, ones64)


def _inproj_bwd_x(x, norm_g, dp_qkv, dg_ret, dp_rwkv, dh, w_in):
    T = x.shape[0]
    n_qkv = 2 * RET_QK + RET_WIDTH

    def body(x_ref, g_ref, dqkv_ref, dgret_ref, drwkv_ref, dh_ref, w_ref, dx_ref, dg_ref):
        @pl.when(pl.program_id(0) == 0)
        def _():
            dg_ref[...] = jnp.zeros_like(dg_ref)

        _, vjp = jax.vjp(_rmsnorm, x_ref[...], g_ref[...])
        du = (_dot_nt_bf(dqkv_ref[...], w_ref[:, 0:n_qkv]) + _dot_nt_bf(dgret_ref[...], w_ref[:, n_qkv:RET_COLS])
              + _dot_nt_bf(drwkv_ref[...], w_ref[:, RET_COLS:]))
        dx, dg = vjp(du)
        dx_ref[...] = dx + dh_ref[...]
        dg_ref[...] += dg

    return pl.pallas_call(
        body, name="inproj_bwd_x", grid=(T // TOK_TILE,),
        in_specs=[_rows(TOK_TILE, D_MODEL), _full((1, D_MODEL)), _rows(TOK_TILE, n_qkv), _rows(TOK_TILE, RET_WIDTH),
                  _rows(TOK_TILE, RWKV_COLS), _rows(TOK_TILE, D_MODEL), _full((D_MODEL, IN_COLS))],
        out_specs=[_rows(TOK_TILE, D_MODEL), _full((1, D_MODEL))],
        out_shape=[jax.ShapeDtypeStruct((T, D_MODEL), F32), jax.ShapeDtypeStruct((1, D_MODEL), F32)],
        compiler_params=_params(dimension_semantics=("arbitrary",)),
    )(x, norm_g, dp_qkv, dg_ret, dp_rwkv, dh, w_in)


def _grad_w(name, u, dp):
    T, n = dp.shape
    tile = 2 * TOK_TILE

    def body(u_ref, dp_ref, out_ref):
        @pl.when(pl.program_id(0) == 0)
        def _():
            out_ref[...] = jnp.zeros_like(out_ref)

        out_ref[...] += _dot_tn_bf(u_ref[...], dp_ref[...])

    return pl.pallas_call(
        body, name=name, grid=(T // tile,),
        in_specs=[_rows(tile, D_MODEL), _rows(tile, n)],
        out_specs=_full((D_MODEL, n)),
        out_shape=jax.ShapeDtypeStruct((D_MODEL, n), F32),
        compiler_params=_params(dimension_semantics=("arbitrary",)),
    )(u, dp)


def _pad_lora(w_up, top):
    z = jnp.zeros_like(w_up)
    return jnp.concatenate([w_up, z] if top else [z, w_up], axis=0)


def _local_grads(x, target, norm_g, w_in_bf, ret_gn_g, mu, w_lora_up, w0, a_lora_up, a0, k_k, k_a, r_k, gn_g, gn_b,
                 w_out_bf, final_g):
    T = x.shape[0]
    tabs = _rope_tables(T) + _ret_tables()
    ones64 = _block_mix(128, RWKV_HEAD)
    avg64 = _block_mix(128, RWKV_HEAD, 1.0 / RWKV_HEAD)
    avg128 = _block_mix(128, RET_DV, 1.0 / RET_DV)
    wup_pad, aup_pad = _pad_lora(w_lora_up, True), _pad_lora(a_lora_up, False)

    p_ret, p_rwkv, u = _inproj(x, norm_g, w_in_bf)
    ret_raw, s_saved = _ret_fwd(p_ret, tabs)
    r, w, k, v, kap, a, g_rw = _prep_fwd(p_rwkv, mu, w0, a0, k_k, k_a, wup_pad, aup_pad, ones64)
    o, s_all, sa_all = _wkv_fwd(r, w, k, v, kap, a)
    (dh, do, dr2, dk2, dv2, dgrw, dret, dgret, d_w_out, d_ret_gn_g, d_gn_g, d_gn_b, d_r_k, d_final_g, loss) = _post(
        o, r, k, v, g_rw, ret_raw, p_ret, x, target, ret_gn_g, gn_g, gn_b, r_k, final_g, w_out_bf, avg128, avg64, ones64)
    dr, dw, dk, dv, dkap, da = _wkv_bwd(r, w, k, v, kap, a, s_all, sa_all, do)
    dp_rwkv, d_mu, d_w0, d_a0, d_k_k, d_k_a, d_wup, d_aup = _prep_bwd(
        p_rwkv, mu, w0, a0, k_k, k_a, wup_pad, aup_pad, ones64, dr, dw, dk, dv, dkap, da, dgrw, dr2, dk2, dv2)
    dp_qkv = _ret_bwd(p_ret, s_saved, dret, tabs)
    dx, d_norm_g = _inproj_bwd_x(x, norm_g, dp_qkv, dgret, dp_rwkv, dh, w_in_bf)
    d_w_in = jnp.concatenate([_grad_w("grad_w_qkv", u, dp_qkv), _grad_w("grad_w_gret", u, dgret),
                              _grad_w("grad_w_rwkv", u, dp_rwkv)], axis=1)
    grads = dict(norm_g=d_norm_g, w_in=d_w_in, ret_gn_g=d_ret_gn_g, rwkv_mu=d_mu, w_lora_up=d_wup[:LORA], w0=d_w0,
                 a_lora_up=d_aup[LORA:], a0=d_a0, k_k=d_k_k, k_a=d_k_a, r_k=d_r_k, rwkv_gn_g=d_gn_g, rwkv_gn_b=d_gn_b,
                 w_out=d_w_out, final_norm_g=d_final_g)
    return loss, dx, grads


def _mesh_pos():
    return lax.axis_index("x"), lax.axis_index("y"), lax.axis_index("c")


def _all_gather(shards):
    n = len(shards)

    def body(*refs):
        x_refs, out_refs = refs[:n], refs[n:2 * n]
        send_sems, recv_sems, local_sems = refs[2 * n:]
        x, y, c = _mesh_pos()
        me, sibling = (x, y, c), (x, y, 1 - c)
        chips = [(1 - x, y), (x, 1 - y), (1 - x, 1 - y)]

        def rows(a, pos):
            m = x_refs[a].shape[0]
            return out_refs[a].at[pl.ds((4 * pos[0] + 2 * pos[1] + pos[2]) * m, m), :]

        def copy(a, k, block, to, src=None):
            return pltpu.make_async_remote_copy(
                src_ref=rows(a, block) if src is None else src, dst_ref=rows(a, block),
                send_sem=send_sems.at[a, k], recv_sem=recv_sems.at[a, k], device_id=to, device_id_type=MESH)

        mine = [pltpu.make_async_copy(x_refs[a], rows(a, me), local_sems.at[a]) for a in range(n)]
        for cp in mine:
            cp.start()
        first = []
        for a in range(n):
            first.append(copy(a, 0, me, sibling, src=x_refs[a]))
            first += [copy(a, 1 + j, me, (*chip, c), src=x_refs[a]) for j, chip in enumerate(chips)]
        for cp in first:
            cp.start()
        passed = []
        for j, chip in enumerate(chips):
            for a in range(n):
                copy(a, 1 + j, (*chip, c), me).wait_recv()
                passed.append(copy(a, 4 + j, (*chip, c), sibling))
                passed[-1].start()
        for a in range(n):
            copy(a, 0, sibling, me).wait_recv()
            for j, chip in enumerate(chips):
                copy(a, 4 + j, (*chip, 1 - c), me).wait_recv()
        for cp in first + passed:
            cp.wait_send()
        for cp in mine:
            cp.wait()

    vmem = pl.BlockSpec(memory_space=pltpu.VMEM)
    return pl.pallas_call(
        body, name="gather_weights",
        out_shape=[jax.ShapeDtypeStruct((N_DEV * s.shape[0], s.shape[1]), s.dtype) for s in shards],
        in_specs=[vmem] * n, out_specs=[vmem] * n,
        scratch_shapes=[pltpu.SemaphoreType.DMA((n, 7)), pltpu.SemaphoreType.DMA((n, 7)), pltpu.SemaphoreType.DMA((n,))],
        compiler_params=_params(),
    )(*shards)


def _exchange(parts):
    n = len(parts)

    def body(*refs):
        in_refs, out_refs = refs[:n], refs[n:2 * n]
        send_sems, recv_sems, local_sems = refs[2 * n:]
        x, y, c = _mesh_pos()
        me = 4 * x + 2 * y + c
        own = [pltpu.make_async_copy(in_refs[a].at[me], out_refs[a].at[me], local_sems.at[a]) for a in range(n)]
        for cp in own:
            cp.start()
        copies = []
        for k in range(1, N_DEV):
            peer = (x ^ (k >> 2), y ^ ((k >> 1) & 1), c ^ (k & 1))
            peer_idx = 4 * peer[0] + 2 * peer[1] + peer[2]
            for a in range(n):
                copies.append(pltpu.make_async_remote_copy(
                    src_ref=in_refs[a].at[peer_idx], dst_ref=out_refs[a].at[me],
                    send_sem=send_sems.at[a, k - 1], recv_sem=recv_sems.at[a, k - 1], device_id=peer, device_id_type=MESH))
        for cp in copies:
            cp.start()
        for cp in copies:
            cp.wait()
        for cp in own:
            cp.wait()

    hbm = pl.BlockSpec(memory_space=pl.ANY)
    return pl.pallas_call(
        body, name="exchange_grads",
        out_shape=[jax.ShapeDtypeStruct(p.shape, p.dtype) for p in parts],
        in_specs=[hbm] * n, out_specs=[hbm] * n,
        scratch_shapes=[pltpu.SemaphoreType.DMA((n, 7)), pltpu.SemaphoreType.DMA((n, 7)), pltpu.SemaphoreType.DMA((n,))],
        compiler_params=_params(),
    )(*parts)


def _adamw(w, g, m, v):
    m = ADAM_B1 * m + (1.0 - ADAM_B1) * g
    v = ADAM_B2 * v + (1.0 - ADAM_B2) * (g * g)
    m_hat = m / (1.0 - ADAM_B1 ** ADAM_STEP)
    v_hat = v / (1.0 - ADAM_B2 ** ADAM_STEP)
    return -ADAM_LR * (m_hat / (jnp.sqrt(v_hat) + ADAM_EPS) + ADAM_WD * w), m, v


def _reduce_adamw(name, parts, w, m, v, row_tile):
    _, rows, cols = parts.shape

    def body(p_ref, w_ref, m_ref, v_ref, g_ref, d_ref, nm_ref, nv_ref):
        g = p_ref[0].astype(F32)
        for s in range(1, N_DEV):
            g = g + p_ref[s].astype(F32)
        g_ref[...] = g
        d_ref[...], nm_ref[...], nv_ref[...] = _adamw(w_ref[...], g, m_ref[...], v_ref[...])

    tile = pl.BlockSpec((row_tile, cols), lambda i: (i, 0))
    return pl.pallas_call(
        body, name=name, grid=(rows // row_tile,),
        in_specs=[pl.BlockSpec((N_DEV, row_tile, cols), lambda i: (0, i, 0)), tile, tile, tile],
        out_specs=[tile] * 4,
        out_shape=[jax.ShapeDtypeStruct((rows, cols), F32)] * 4,
        compiler_params=_params(dimension_semantics=("arbitrary",)),
    )(parts, w, m, v)


_SMALL = (("norm_g", 1024), ("ret_gn_g", 512), ("rwkv_mu", 2176), ("w0", 512), ("a0", 512), ("k_k", 512), ("k_a", 512),
          ("r_k", 512), ("rwkv_gn_g", 512), ("rwkv_gn_b", 512), ("final_norm_g", 1024))
_SMALL_LANES = sum(n for _, n in _SMALL) + 128
_WEIGHTS = ("norm_g", "w_in", "ret_gn_g", "rwkv_mu", "w_lora_up", "w0", "a_lora_up", "a0", "k_k", "k_a", "r_k", "rwkv_gn_g",
            "rwkv_gn_b", "w_out", "final_norm_g")


def _adamw_vectors(parts, wts, mom, var):
    k = len(_SMALL)

    def body(p_ref, *refs):
        w_refs, m_refs, v_refs, outs = refs[:k], refs[k:2 * k], refs[2 * k:3 * k], refs[3 * k:]
        g_all = p_ref[0]
        for s in range(1, N_DEV):
            g_all = g_all + p_ref[s]
        off = 0
        for i, (_, n) in enumerate(_SMALL):
            g = g_all[:, off:off + n]
            off += n
            outs[4 * i][...] = g
            outs[4 * i + 1][...], outs[4 * i + 2][...], outs[4 * i + 3][...] = _adamw(
                w_refs[i][...], g, m_refs[i][...], v_refs[i][...])
        outs[4 * k][...] = g_all[:, off:off + 128]

    vmem = pl.BlockSpec(memory_space=pltpu.VMEM)
    shapes = [jax.ShapeDtypeStruct((1, n), F32) for _, n in _SMALL for _ in range(4)] + [jax.ShapeDtypeStruct((1, 128), F32)]
    res = pl.pallas_call(
        body, name="adamw_vectors", out_shape=shapes,
        in_specs=[vmem] * (1 + 3 * k), out_specs=[vmem] * len(shapes), compiler_params=_params(),
    )(parts, *[wts[n] for n, _ in _SMALL], *[mom[n] for n, _ in _SMALL], *[var[n] for n, _ in _SMALL])
    return {n: res[4 * i:4 * i + 4] for i, (n, _) in enumerate(_SMALL)}, res[4 * k]


def kernel(x, norm_g, w_in, ret_gn_g, rwkv_mu, w_lora_up, w0, a_lora_up, a0, k_k, k_a, r_k, rwkv_gn_g, rwkv_gn_b, w_out, final_norm_g, loss_target, m_norm_g, m_w_in, m_ret_gn_g, m_rwkv_mu, m_w_lora_up, m_w0, m_a_lora_up, m_a0, m_k_k, m_k_a, m_r_k, m_rwkv_gn_g, m_rwkv_gn_b, m_w_out, m_final_norm_g, v_norm_g, v_w_in, v_ret_gn_g, v_rwkv_mu, v_w_lora_up, v_w0, v_a_lora_up, v_a0, v_k_k, v_k_a, v_r_k, v_rwkv_gn_g, v_rwkv_gn_b, v_w_out, v_final_norm_g):
    wts = dict(norm_g=norm_g, w_in=w_in, ret_gn_g=ret_gn_g, rwkv_mu=rwkv_mu, w_lora_up=w_lora_up, w0=w0, a_lora_up=a_lora_up,
               a0=a0, k_k=k_k, k_a=k_a, r_k=r_k, rwkv_gn_g=rwkv_gn_g, rwkv_gn_b=rwkv_gn_b, w_out=w_out,
               final_norm_g=final_norm_g)
    mom = dict(norm_g=m_norm_g, w_in=m_w_in, ret_gn_g=m_ret_gn_g, rwkv_mu=m_rwkv_mu, w_lora_up=m_w_lora_up, w0=m_w0,
               a_lora_up=m_a_lora_up, a0=m_a0, k_k=m_k_k, k_a=m_k_a, r_k=m_r_k, rwkv_gn_g=m_rwkv_gn_g,
               rwkv_gn_b=m_rwkv_gn_b, w_out=m_w_out, final_norm_g=m_final_norm_g)
    var = dict(norm_g=v_norm_g, w_in=v_w_in, ret_gn_g=v_ret_gn_g, rwkv_mu=v_rwkv_mu, w_lora_up=v_w_lora_up, w0=v_w0,
               a_lora_up=v_a_lora_up, a0=v_a0, k_k=v_k_k, k_a=v_k_a, r_k=v_r_k, rwkv_gn_g=v_rwkv_gn_g,
               rwkv_gn_b=v_rwkv_gn_b, w_out=v_w_out, final_norm_g=v_final_norm_g)
    shapes = {n: wts[n].shape for n in _WEIGHTS}

    g_in, g_out, g_wup, g_aup = _all_gather(
        [w_in[0].astype(BF16), w_out[0].astype(BF16), w_lora_up[0], a_lora_up[0]])
    w_in_bf = g_in.reshape(N_DEV, D_MODEL, SHARD_IN).transpose(1, 0, 2).reshape(D_MODEL, IN_COLS)
    wup_full = g_wup.reshape(N_DEV, LORA, SHARD_LORA).transpose(1, 0, 2).reshape(LORA, W)
    aup_full = g_aup.reshape(N_DEV, LORA, SHARD_LORA).transpose(1, 0, 2).reshape(LORA, W)

    loss, dx, g = _local_grads(
        x[0], loss_target[0], norm_g, w_in_bf, ret_gn_g, rwkv_mu, wup_full, w0, aup_full, a0, k_k, k_a,
        r_k.reshape(1, W), rwkv_gn_g, rwkv_gn_b, g_out, final_norm_g.reshape(1, D_MODEL))

    small = jnp.concatenate([g[n] for n, _ in _SMALL] + [loss], axis=1)
    parts = _exchange([
        g["w_in"].reshape(D_MODEL, N_DEV, SHARD_IN).transpose(1, 0, 2).astype(BF16),
        g["w_out"].reshape(N_DEV, SHARD_OUT, D_MODEL).astype(BF16),
        g["w_lora_up"].reshape(LORA, N_DEV, SHARD_LORA).transpose(1, 0, 2),
        g["a_lora_up"].reshape(LORA, N_DEV, SHARD_LORA).transpose(1, 0, 2),
        jnp.broadcast_to(small[None], (N_DEV, 1, _SMALL_LANES))])
    res = {}
    res["w_in"] = _reduce_adamw("adamw_w_in", parts[0], w_in[0], m_w_in[0], v_w_in[0], 256)
    res["w_out"] = _reduce_adamw("adamw_w_out", parts[1], w_out[0], m_w_out[0], v_w_out[0], SHARD_OUT)
    res["w_lora_up"] = _reduce_adamw("adamw_w_lora_up", parts[2], w_lora_up[0], m_w_lora_up[0], v_w_lora_up[0], LORA)
    res["a_lora_up"] = _reduce_adamw("adamw_a_lora_up", parts[3], a_lora_up[0], m_a_lora_up[0], v_a_lora_up[0], LORA)
    as_row = lambda d: {n: d[n].reshape(1, size) for n, size in _SMALL}
    vec, loss_row = _adamw_vectors(parts[4], as_row(wts), as_row(mom), as_row(var))
    res.update(vec)
    res = {n: [t.reshape(shapes[n]) for t in res[n]] for n in _WEIGHTS}
    return (loss_row[0, 0], dx[None], *[res[n][0] for n in _WEIGHTS], *[res[n][1] for n in _WEIGHTS],
            *[res[n][2] for n in _WEIGHTS], *[res[n][3] for n in _WEIGHTS])
```

```python
import functools

import numpy as np
import jax
import jax.numpy as jnp
from jax import lax
from jax.experimental import pallas as pl
from jax.experimental.pallas import tpu as pltpu

F32 = jnp.float32
BF16 = jnp.bfloat16

D_MODEL = 1024
CHUNK = 64
RET_HEADS = 4
RET_DV = 128
RET_DK = 64
RET_QK = 256
RET_WIDTH = 512
RWKV_WIDTH = 512
RWKV_HEAD = 64
RWKV_HEADS = 8
LORA = 64
RET_COLS = 2 * RET_QK + 2 * RET_WIDTH
RWKV_COLS = 4 * RWKV_WIDTH + 2 * LORA
IN_COLS = RET_COLS + RWKV_COLS
ROPE_BASE = 10000.0
RMS_EPS = 1e-6
RET_GN_EPS = 1e-5
RWKV_GN_EPS = 64e-5
ADAM_LR = 0.001
ADAM_B1 = 0.9
ADAM_B2 = 0.999
ADAM_EPS = 1e-08
ADAM_WD = 0.01
ADAM_STEP = 10
N_DEV = 8
SHARD_IN = IN_COLS // N_DEV
SHARD_OUT = D_MODEL // N_DEV
SHARD_LORA = RWKV_WIDTH // N_DEV
VMEM_LIMIT = 56 * 1024 * 1024
TOK_TILE = 256
WKV_CHUNK = 32

MESH = pl.DeviceIdType.MESH


def _dot_bf(a, b):
    return jnp.dot(a.astype(BF16), b.astype(BF16), preferred_element_type=F32)


def _dot_nt_bf(a, b):
    return lax.dot_general(a.astype(BF16), b.astype(BF16), (((1,), (1,)), ((), ())), preferred_element_type=F32)


def _dot_tn_bf(a, b):
    return lax.dot_general(a.astype(BF16), b.astype(BF16), (((0,), (0,)), ((), ())), preferred_element_type=F32)


@jax.custom_vjp
def _mm(a, b):
    return _dot_bf(a, b)


@jax.custom_vjp
def _mm_nt(a, b):
    return _dot_nt_bf(a, b)


@jax.custom_vjp
def _mm_tn(a, b):
    return _dot_tn_bf(a, b)


_mm.defvjp(lambda a, b: (_dot_bf(a, b), (a, b)), lambda res, g: (_dot_nt_bf(g, res[1]), _dot_tn_bf(res[0], g)))
_mm_nt.defvjp(lambda a, b: (_dot_nt_bf(a, b), (a, b)), lambda res, g: (_dot_bf(g, res[1]), _dot_tn_bf(g, res[0])))
_mm_tn.defvjp(lambda a, b: (_dot_tn_bf(a, b), (a, b)), lambda res, g: (_dot_nt_bf(res[1], g), _dot_bf(res[0], g)))


def _trunc(x):
    return lax.bitcast_convert_type(lax.bitcast_convert_type(x, jnp.uint32) & jnp.uint32(0xFFFF0000), F32)


def _two_piece(x):
    hi = _trunc(x)
    return jnp.concatenate([hi, x - hi], axis=1)


def _mix_raw(x, mat2):
    return _unstack(jnp.dot(_two_piece(_stack(x)), mat2, preferred_element_type=F32))


@jax.custom_vjp
def _head_mix(x, mat2):
    return _mix_raw(x, mat2)


_head_mix.defvjp(lambda x, mat2: (_mix_raw(x, mat2), mat2), lambda mat2, g: (_mix_raw(g, mat2), jnp.zeros_like(mat2)))


def _swap_halves(x):
    lane = lax.broadcasted_iota(jnp.int32, x.shape, 1)
    return jnp.where((lane & (RET_DK - 1)) < RET_DK // 2, pltpu.roll(x, RET_QK - RET_DK // 2, axis=1),
                     pltpu.roll(x, RET_DK // 2, axis=1))


@jax.custom_vjp
def _rot(x):
    return _swap_halves(x)


_rot.defvjp(lambda x: (_swap_halves(x), None), lambda _, g: (_swap_halves(g),))


def _params(**kw):
    return pltpu.CompilerParams(vmem_limit_bytes=VMEM_LIMIT, **kw)


def _full(shape):
    nd = len(shape)
    return pl.BlockSpec(shape, lambda i, _nd=nd: (0,) * _nd)


def _rows(tile, width):
    return pl.BlockSpec((tile, width), lambda i: (i, 0))


def _block_mix(n, blk, scale=1.0):
    idx = np.arange(n) // blk
    m = (idx[:, None] == idx[None, :]).astype(np.float32) * scale
    return jnp.asarray(np.concatenate([m, m], axis=0))


def _rope_tables(T):
    half = RET_DK // 2
    expo = -np.arange(half, dtype=np.float32) / np.float32(half)
    freqs = np.exp(expo * np.float32(np.log(ROPE_BASE))).astype(np.float32)
    ang = np.arange(T, dtype=np.float32)[:, None] * freqs[None, :]
    cos, sin = np.cos(ang).astype(np.float32), np.sin(ang).astype(np.float32)
    cos_h = np.concatenate([cos, cos], axis=1)
    sin_h = np.concatenate([-sin, sin], axis=1)
    cos_t = np.tile(cos_h, (1, RET_HEADS))
    sin_t = np.tile(sin_h, (1, RET_HEADS))
    return jnp.asarray(cos_t), jnp.asarray(sin_t)


def _ret_tables():
    h = np.arange(RET_HEADS, dtype=np.float32)
    lg = np.log(1.0 - np.exp2(-5.0 - h)).astype(np.float32)
    idx = np.arange(CHUNK, dtype=np.float32)
    intra = np.exp(lg[:, None, None] * np.abs(idx[:, None] - idx[None, :])).astype(np.float32)
    q_dec = np.exp(lg[:, None] * (idx[None, :] + 1.0)).astype(np.float32)
    k_dec = np.exp(lg[:, None] * (CHUNK - 1.0 - idx[None, :])).astype(np.float32)
    chunk_dec = np.exp(lg * CHUNK).astype(np.float32)
    lane_head = np.arange(RET_QK) // RET_DK
    mask = (lane_head[None, :] == np.arange(RET_HEADS)[:, None]).astype(np.float32)
    m = np.broadcast_to(mask[:, None, :], (RET_HEADS, CHUNK, RET_QK)).copy()
    qd = m * q_dec[:, :, None]
    kd = m * k_dec[:, :, None]
    return jnp.asarray(intra), jnp.asarray(m), jnp.asarray(qd), jnp.asarray(kd), [float(c) for c in chunk_dec]


def _rmsnorm(x, g):
    return x * lax.rsqrt(jnp.mean(x * x, axis=-1, keepdims=True) + RMS_EPS) * g


def _inproj(x, norm_g, w_in):
    T = x.shape[0]

    def body(x_ref, g_ref, w_ref, pr_ref, pw_ref, u_ref):
        ub = _rmsnorm(x_ref[...], g_ref[...]).astype(BF16)
        u_ref[...] = ub
        pr_ref[...] = jnp.dot(ub, w_ref[:, :RET_COLS], preferred_element_type=F32)
        pw_ref[...] = jnp.dot(ub, w_ref[:, RET_COLS:], preferred_element_type=F32)

    return pl.pallas_call(
        body, name="inproj", grid=(T // TOK_TILE,),
        in_specs=[_rows(TOK_TILE, D_MODEL), _full((1, D_MODEL)), _full((D_MODEL, IN_COLS))],
        out_specs=[_rows(TOK_TILE, RET_COLS), _rows(TOK_TILE, RWKV_COLS), _rows(TOK_TILE, D_MODEL)],
        out_shape=[jax.ShapeDtypeStruct((T, RET_COLS), F32), jax.ShapeDtypeStruct((T, RWKV_COLS), F32),
                   jax.ShapeDtypeStruct((T, D_MODEL), BF16)],
        compiler_params=_params(dimension_semantics=("arbitrary",)),
    )(x, norm_g, w_in)


def _ret_chunk(pq, pk, v_heads, s_heads, cos_t, sin_t, dec, hm, qd, kd, chunk_dec):
    q = pq * cos_t + _rot(pq) * sin_t
    k = (pk * cos_t + _rot(pk) * sin_t) * (RET_DK ** -0.5)
    outs, s_out = [], []
    for h in range(RET_HEADS):
        sc = _mm_nt(q * hm[h], k * hm[h]) * dec[h]
        intra = _mm(sc, v_heads[h])
        kv = _mm_tn(k * kd[h], v_heads[h])
        inter = _mm(q * qd[h], s_heads[h])
        outs.append(intra + inter)
        s_out.append(s_heads[h] * chunk_dec[h] + kv)
    return tuple(outs), tuple(s_out)


def _ret_specs():
    const = [_full((RET_HEADS, CHUNK, CHUNK)), _full((RET_HEADS, CHUNK, RET_QK)),
             _full((RET_HEADS, CHUNK, RET_QK)), _full((RET_HEADS, CHUNK, RET_QK))]
    return const


def _ret_fwd(p_ret, tabs):
    T = p_ret.shape[0]
    nc = T // CHUNK
    cos_t, sin_t, dec, hm, qd, kd, chunk_dec = tabs

    def body(p_ref, cos_ref, sin_ref, dec_ref, hm_ref, qd_ref, kd_ref, out_ref, sin_save_ref, s_scr):
        @pl.when(pl.program_id(0) == 0)
        def _():
            s_scr[...] = jnp.zeros_like(s_scr)

        s_heads = tuple(s_scr[h] for h in range(RET_HEADS))
        for h in range(RET_HEADS):
            sin_save_ref[0, h] = s_heads[h]
        pq = p_ref[:, 0:RET_QK]
        pk = p_ref[:, RET_QK:2 * RET_QK]
        v_heads = tuple(p_ref[:, 2 * RET_QK + RET_DV * h:2 * RET_QK + RET_DV * (h + 1)] for h in range(RET_HEADS))
        outs, s_out = _ret_chunk(pq, pk, v_heads, s_heads, cos_ref[...], sin_ref[...], dec_ref[...],
                                 hm_ref[...], qd_ref[...], kd_ref[...], chunk_dec)
        for h in range(RET_HEADS):
            out_ref[:, RET_DV * h:RET_DV * (h + 1)] = outs[h]
            s_scr[h] = s_out[h]

    return pl.pallas_call(
        body, name="ret_fwd", grid=(nc,),
        in_specs=[pl.BlockSpec((CHUNK, RET_COLS), lambda i: (i, 0)), _rows(CHUNK, RET_QK), _rows(CHUNK, RET_QK)] + _ret_specs(),
        out_specs=[_rows(CHUNK, RET_WIDTH), pl.BlockSpec((1, RET_HEADS, RET_QK, RET_DV), lambda i: (i, 0, 0, 0))],
        out_shape=[jax.ShapeDtypeStruct((T, RET_WIDTH), F32), jax.ShapeDtypeStruct((nc, RET_HEADS, RET_QK, RET_DV), F32)],
        scratch_shapes=[pltpu.VMEM((RET_HEADS, RET_QK, RET_DV), F32)],
        compiler_params=_params(dimension_semantics=("arbitrary",)),
    )(p_ret, cos_t, sin_t, dec, hm, qd, kd)


def _ret_bwd(p_ret, s_saved, d_ret, tabs):
    T = p_ret.shape[0]
    nc = T // CHUNK
    cos_t, sin_t, dec, hm, qd, kd, chunk_dec = tabs

    def body(p_ref, s_ref, dret_ref, cos_ref, sin_ref, dec_ref, hm_ref, qd_ref, kd_ref, dp_ref, ds_scr):
        @pl.when(pl.program_id(0) == 0)
        def _():
            ds_scr[...] = jnp.zeros_like(ds_scr)

        pq = p_ref[:, 0:RET_QK]
        pk = p_ref[:, RET_QK:2 * RET_QK]
        v_heads = tuple(p_ref[:, 2 * RET_QK + RET_DV * h:2 * RET_QK + RET_DV * (h + 1)] for h in range(RET_HEADS))
        s_heads = tuple(s_ref[0, h] for h in range(RET_HEADS))
        consts = (cos_ref[...], sin_ref[...], dec_ref[...], hm_ref[...], qd_ref[...], kd_ref[...])
        _, vjp = jax.vjp(lambda a, b, c, d: _ret_chunk(a, b, c, d, *consts, chunk_dec), pq, pk, v_heads, s_heads)
        d_out = tuple(dret_ref[:, RET_DV * h:RET_DV * (h + 1)] for h in range(RET_HEADS))
        d_s = tuple(ds_scr[h] for h in range(RET_HEADS))
        dq, dk, dv, ds_in = vjp((d_out, d_s))
        dp_ref[:, 0:RET_QK] = dq
        dp_ref[:, RET_QK:2 * RET_QK] = dk
        for h in range(RET_HEADS):
            dp_ref[:, 2 * RET_QK + RET_DV * h:2 * RET_QK + RET_DV * (h + 1)] = dv[h]
            ds_scr[h] = ds_in[h]

    rev = lambda i: (nc - 1 - i, 0)
    return pl.pallas_call(
        body, name="ret_bwd", grid=(nc,),
        in_specs=[pl.BlockSpec((CHUNK, RET_COLS), rev),
                  pl.BlockSpec((1, RET_HEADS, RET_QK, RET_DV), lambda i: (nc - 1 - i, 0, 0, 0)),
                  pl.BlockSpec((CHUNK, RET_WIDTH), rev), pl.BlockSpec((CHUNK, RET_QK), rev), pl.BlockSpec((CHUNK, RET_QK), rev)]
        + _ret_specs(),
        out_specs=pl.BlockSpec((CHUNK, 2 * RET_QK + RET_WIDTH), rev),
        out_shape=jax.ShapeDtypeStruct((T, 2 * RET_QK + RET_WIDTH), F32),
        scratch_shapes=[pltpu.VMEM((RET_HEADS, RET_QK, RET_DV), F32)],
        compiler_params=_params(dimension_semantics=("arbitrary",)),
    )(p_ret, s_saved, d_ret, cos_t, sin_t, dec, hm, qd, kd)


def _wkv_consts():
    lane = lax.broadcasted_iota(jnp.int32, (RWKV_HEAD, RWKV_WIDTH), 1)
    sub = lax.broadcasted_iota(jnp.int32, (RWKV_HEAD, RWKV_WIDTH), 0)
    diag = ((lane & (RWKV_HEAD - 1)) == sub).astype(F32)
    r = lax.broadcasted_iota(jnp.int32, (3 * 128, 128), 0)
    c = lax.broadcasted_iota(jnp.int32, (3 * 128, 128), 1)
    ones = (((r & 127) >> 6) == (c >> 6)).astype(BF16)
    return diag, ones


def _stack(x):
    return jnp.concatenate([x[:, 128 * p:128 * (p + 1)] for p in range(4)], axis=0)


def _unstack(y):
    n = y.shape[0] // 4
    return jnp.concatenate([y[n * p:n * (p + 1)] for p in range(4)], axis=1)


def _split(x, n):
    pieces = []
    for _ in range(n):
        p = x.astype(BF16)
        pieces.append(p)
        x = x - p.astype(F32)
    return pieces


def _lane_sum(x, ones):
    return _unstack(jnp.dot(_two_piece(_stack(x)), ones[:256].astype(F32), preferred_element_type=F32))


def _lane_sum_bf(x, ones):
    return _unstack(jnp.dot(_stack(x).astype(BF16), ones[:128], preferred_element_type=F32))


def _expand(row, diag_bf, ones, n):
    lhs = jnp.concatenate([_stack(jnp.broadcast_to(p, diag_bf.shape) * diag_bf) for p in _split(row, n)], axis=1)
    return _unstack(jnp.dot(lhs, ones[:128 * n], preferred_element_type=F32))


def _colsum(x):
    return jnp.sum(x, axis=0, keepdims=True)


def _head_sums(x, ones):
    return _unstack(jnp.dot(jnp.concatenate(_split(_stack(x), 3), axis=1), ones, preferred_element_type=F32))


def _wkv_fwd(r, w, k, v, kap, a):
    T = r.shape[0]
    C = WKV_CHUNK
    nc = T // C

    def body(r_ref, w_ref, k_ref, v_ref, kap_ref, a_ref, o_ref, s_all_ref, sa_all_ref, s_scr, vf_scr):
        @pl.when(pl.program_id(0) == 0)
        def _():
            s_scr[...] = jnp.zeros_like(s_scr)

        diag, ones = _wkv_consts()
        diag_bf = diag.astype(BF16)
        rr, ww, kk, vv, kap_, aa = (ref[...] for ref in (r_ref, w_ref, k_ref, v_ref, kap_ref, a_ref))
        bb = kap_ * aa
        c1 = _head_sums(pltpu.roll(bb, 1, axis=0) * kap_, ones)
        row = lambda x, t: x[t:t + 1]

        for t in range(C):
            vf_scr[t] = _expand(row(vv, t), diag_bf, ones, 2)

        s_prev = s_scr[...]
        sa = _lane_sum(s_prev * (-row(kap_, 0)), ones)
        ls, rows = None, []

        def emit_o(t, s_t):
            rows.append(_colsum(_lane_sum_bf(s_t * row(rr, t), ones) * diag))
            if t % 8 == 7:
                o_ref[t - 7:t + 1, :] = jnp.concatenate(rows, axis=0)
                rows.clear()

        for t in range(C):
            u = s_prev * row(ww, t) + vf_scr[t] * row(kk, t)
            if t > 0:
                sa = ls - sa * row(c1, t)
            if t + 1 < C:
                ls = _lane_sum(u * (-row(kap_, t + 1)), ones)
            if t > 0:
                emit_o(t - 1, s_prev)
            s_prev = u + sa * row(bb, t)
            s_all_ref[t] = s_prev
            sa_all_ref[t] = sa.astype(BF16)
        emit_o(C - 1, s_prev)
        s_scr[...] = s_prev

    spec = _rows(C, RWKV_WIDTH)
    return pl.pallas_call(
        body, name="wkv_fwd", grid=(nc,),
        in_specs=[spec] * 6,
        out_specs=[spec, pl.BlockSpec((C, RWKV_HEAD, RWKV_WIDTH), lambda i: (i, 0, 0)),
                   pl.BlockSpec((C, RWKV_HEAD, RWKV_WIDTH), lambda i: (i, 0, 0))],
        out_shape=[jax.ShapeDtypeStruct((T, RWKV_WIDTH), F32), jax.ShapeDtypeStruct((T, RWKV_HEAD, RWKV_WIDTH), F32),
                   jax.ShapeDtypeStruct((T, RWKV_HEAD, RWKV_WIDTH), BF16)],
        scratch_shapes=[pltpu.VMEM((RWKV_HEAD, RWKV_WIDTH), F32), pltpu.VMEM((C, RWKV_HEAD, RWKV_WIDTH), F32)],
        compiler_params=_params(dimension_semantics=("arbitrary",)),
    )(r, w, k, v, kap, a)


def _wkv_bwd(r, w, k, v, kap, a, s_all, sa_all, d_o):
    T = r.shape[0]
    C = WKV_CHUNK
    nc = T // C

    def body(r_ref, w_ref, k_ref, v_ref, kap_ref, a_ref, s_ref, s_before_ref, sa_ref, do_ref,
             dr_ref, dw_ref, dk_ref, dv_ref, dkap_ref, da_ref, ds_scr, vf_scr, dof_scr):
        first_chunk = pl.program_id(0) == nc - 1

        @pl.when(pl.program_id(0) == 0)
        def _():
            ds_scr[...] = jnp.zeros_like(ds_scr)

        diag, ones = _wkv_consts()
        diag_bf = diag.astype(BF16)
        rr, ww, kk, vv, kap_, aa, dd = (ref[...] for ref in (r_ref, w_ref, k_ref, v_ref, kap_ref, a_ref, do_ref))
        bb = kap_ * aa
        e1 = _head_sums(pltpu.roll(kap_, C - 1, axis=0) * bb, ones)
        row = lambda x, t: x[t:t + 1]

        def state_before(t):
            return s_ref[t - 1] if t > 0 else jnp.where(first_chunk, 0.0, s_before_ref[0])

        for t in range(C):
            vf_scr[t] = _expand(row(vv, t), diag_bf, ones, 1)
            dof_scr[t] = _expand(row(dd, t), diag_bf, ones, 2)

        d_sn, dsa, rows = None, None, [None] * C

        def emit_rows(t, d_sn_t, dsa_t):
            s_prev, dof = state_before(t), dof_scr[t]
            dv = _colsum(_lane_sum_bf(d_sn_t * row(kk, t), ones) * diag)
            db = _colsum(d_sn_t * sa_ref[t].astype(F32))
            rows[t] = (_colsum(s_ref[t] * dof), _colsum(d_sn_t * s_prev), _colsum(d_sn_t * vf_scr[t]), dv,
                       db * row(aa, t) - _colsum(dsa_t * s_prev), db * row(kap_, t))
            if t % 8 == 0:
                for j, ref in enumerate((dr_ref, dw_ref, dk_ref, dv_ref, dkap_ref, da_ref)):
                    ref[t:t + 8, :] = jnp.concatenate([rows[u][j] for u in range(t, t + 8)], axis=0)

        for t in reversed(range(C)):
            dof = dof_scr[t]
            if t == C - 1:
                d_sn = ds_scr[...] + dof * row(rr, t)
                dsa = _lane_sum(d_sn * row(bb, t), ones)
            else:
                v_t = d_sn * row(ww, t + 1) + dof * row(rr, t)
                ls = _lane_sum(v_t * row(bb, t), ones)
                emit_rows(t + 1, d_sn, dsa)
                d_sn = v_t - dsa * row(kap_, t + 1)
                dsa = ls - dsa * row(e1, t)
        emit_rows(0, d_sn, dsa)
        d_s = d_sn * row(ww, 0) - dsa * row(kap_, 0)
        ds_scr[...] = d_s

    spec = pl.BlockSpec((C, RWKV_WIDTH), lambda i: (nc - 1 - i, 0))
    states = pl.BlockSpec((C, RWKV_HEAD, RWKV_WIDTH), lambda i: (nc - 1 - i, 0, 0))
    before = pl.BlockSpec((1, RWKV_HEAD, RWKV_WIDTH), lambda i: (jnp.maximum((nc - 1 - i) * C - 1, 0), 0, 0))
    big = pltpu.VMEM((C, RWKV_HEAD, RWKV_WIDTH), F32)
    return pl.pallas_call(
        body, name="wkv_bwd", grid=(nc,),
        in_specs=[spec] * 6 + [states, before, states, spec],
        out_specs=[spec] * 6,
        out_shape=[jax.ShapeDtypeStruct((T, RWKV_WIDTH), F32)] * 6,
        scratch_shapes=[pltpu.VMEM((RWKV_HEAD, RWKV_WIDTH), F32), big, big],
        compiler_params=_params(dimension_semantics=("arbitrary",)),
    )(r, w, k, v, kap, a, s_all, s_all, sa_all, d_o)


W = RWKV_WIDTH


def _softplus(y):
    return jnp.maximum(y, 0.0) + jnp.log(1.0 + jnp.exp(-jnp.abs(y)))


def _prep_fn(kr, xwa, w0, a0, k_k, k_a, wup_pad, aup_pad, ones64):
    w_log = -_softplus(-(w0 + _mm(jnp.tanh(xwa), wup_pad))) - 0.5
    decay = jnp.exp(-jnp.exp(w_log))
    a = jax.nn.sigmoid(a0 + _mm(xwa, aup_pad))
    kk = kr * k_k
    kap = kk / jnp.maximum(jnp.sqrt(_head_mix(kk * kk, ones64)), 1e-12)
    k = kr * (1.0 + (a - 1.0) * k_a)
    return decay, k, kap, a


def _shift_down(p, first_row):
    rows = lax.broadcasted_iota(jnp.int32, p.shape, 0)
    return jnp.where(rows == 0, first_row, pltpu.roll(p, 1, axis=0))


def _shift_up(z, last_row):
    n = z.shape[0]
    rows = lax.broadcasted_iota(jnp.int32, z.shape, 0)
    return jnp.where(rows == n - 1, last_row, pltpu.roll(z, n - 1, axis=0))


def _prev_block_spec():
    return pl.BlockSpec((8, RWKV_COLS), lambda i: (jnp.maximum(i * (TOK_TILE // 8) - 1, 0), 0))


def _mixed(p_ref, prev8_ref, mu_ref, first_tile):
    p = p_ref[...]
    first_row = jnp.where(first_tile, 0.0, prev8_ref[7:8, :])
    prev = _shift_down(p, first_row)
    return p, prev, p + mu_ref[...] * (prev - p)


def _prep_fwd(p_rwkv, mu, w0, a0, k_k, k_a, wup_pad, aup_pad, ones64):
    T = p_rwkv.shape[0]

    def body(p_ref, prev8_ref, mu_ref, w0_ref, a0_ref, kk_ref, ka_ref, wup_ref, aup_ref, ones_ref,
             r_ref, w_ref, k_ref, v_ref, kap_ref, a_ref, g_ref):
        _, _, ps = _mixed(p_ref, prev8_ref, mu_ref, pl.program_id(0) == 0)
        decay, k, kap, a = _prep_fn(ps[:, W:2 * W], ps[:, 4 * W:], w0_ref[...], a0_ref[...], kk_ref[...], ka_ref[...],
                                    wup_ref[...], aup_ref[...], ones_ref[...])
        r_ref[...] = ps[:, 0:W]
        w_ref[...] = decay
        k_ref[...] = k
        v_ref[...] = ps[:, 2 * W:3 * W]
        kap_ref[...] = kap
        a_ref[...] = a
        g_ref[...] = ps[:, 3 * W:4 * W]

    vec = _full((1, W))
    return pl.pallas_call(
        body, name="prep_fwd", grid=(T // TOK_TILE,),
        in_specs=[_rows(TOK_TILE, RWKV_COLS), _prev_block_spec(), _full((1, RWKV_COLS)), vec, vec, vec, vec,
                  _full((2 * LORA, W)), _full((2 * LORA, W)), _full((256, 128))],
        out_specs=[_rows(TOK_TILE, W)] * 7,
        out_shape=[jax.ShapeDtypeStruct((T, W), F32)] * 7,
        compiler_params=_params(dimension_semantics=("arbitrary",)),
    )(p_rwkv, p_rwkv, mu, w0, a0, k_k, k_a, wup_pad, aup_pad, ones64)


def _prep_bwd(p_rwkv, mu, w0, a0, k_k, k_a, wup_pad, aup_pad, ones64, dr, dw, dk, dv, dkap, da, dg, dr2, dk2, dv2):
    T = p_rwkv.shape[0]
    nt = T // TOK_TILE

    def body(p_ref, prev8_ref, mu_ref, w0_ref, a0_ref, kk_ref, ka_ref, wup_ref, aup_ref, ones_ref,
             dr_ref, dw_ref, dk_ref, dv_ref, dkap_ref, da_ref, dg_ref, dr2_ref, dk2_ref, dv2_ref,
             dp_ref, dmu_ref, dw0_ref, da0_ref, dkk_ref, dka_ref, dwup_ref, daup_ref, zrow_scr):
        i = pl.program_id(0)
        accs = (dmu_ref, dw0_ref, da0_ref, dkk_ref, dka_ref, dwup_ref, daup_ref)

        @pl.when(i == 0)
        def _():
            zrow_scr[...] = jnp.zeros_like(zrow_scr)
            for ref in accs:
                ref[...] = jnp.zeros_like(ref)

        p, prev, ps = _mixed(p_ref, prev8_ref, mu_ref, i == nt - 1)
        ones = ones_ref[...]
        _, vjp = jax.vjp(lambda *args: _prep_fn(*args, ones), ps[:, W:2 * W], ps[:, 4 * W:], w0_ref[...], a0_ref[...],
                         kk_ref[...], ka_ref[...], wup_ref[...], aup_ref[...])
        dkr, dxwa, dw0, da0, dkk, dka, dwup, daup = vjp(
            (dw_ref[...], dk_ref[...] + dk2_ref[...], dkap_ref[...], da_ref[...]))
        dps = jnp.concatenate([dr_ref[...] + dr2_ref[...], dkr, dv_ref[...] + dv2_ref[...], dg_ref[...], dxwa], axis=1)
        z = dps * mu_ref[...]
        dp_ref[...] = dps - z + _shift_up(z, zrow_scr[0:1, :])
        zrow_scr[0:1, :] = z[0:1, :]
        for ref, val in zip(accs, (_colsum(dps * (prev - p)), dw0, da0, dkk, dka, dwup, daup)):
            ref[...] += val

    rev = lambda i: (nt - 1 - i, 0)
    vec = _full((1, W))
    lora = _full((2 * LORA, W))
    tile = pl.BlockSpec((TOK_TILE, W), rev)
    prev8 = pl.BlockSpec((8, RWKV_COLS), lambda i: (jnp.maximum((nt - 1 - i) * (TOK_TILE // 8) - 1, 0), 0))
    return pl.pallas_call(
        body, name="prep_bwd", grid=(nt,),
        in_specs=[pl.BlockSpec((TOK_TILE, RWKV_COLS), rev), prev8, _full((1, RWKV_COLS)), vec, vec, vec, vec, lora, lora,
                  _full((256, 128))] + [tile] * 10,
        out_specs=[pl.BlockSpec((TOK_TILE, RWKV_COLS), rev), _full((1, RWKV_COLS)), vec, vec, vec, vec, lora, lora],
        out_shape=[jax.ShapeDtypeStruct((T, RWKV_COLS), F32), jax.ShapeDtypeStruct((1, RWKV_COLS), F32)]
        + [jax.ShapeDtypeStruct((1, W), F32)] * 4 + [jax.ShapeDtypeStruct((2 * LORA, W), F32)] * 2,
        scratch_shapes=[pltpu.VMEM((8, RWKV_COLS), F32)],
        compiler_params=_params(dimension_semantics=("arbitrary",)),
    )(p_rwkv, p_rwkv, mu, w0, a0, k_k, k_a, wup_pad, aup_pad, ones64, dr, dw, dk, dv, dkap, da, dg, dr2, dk2, dv2)


def _silu(x):
    return x * jax.nn.sigmoid(x)


def _post_y(o, r, k, v, g_rw, ret_raw, g_ret, ret_gn_g, gn_g, gn_b, r_k, avg128, avg64, ones64):
    xc = ret_raw - _head_mix(ret_raw, avg128)
    ret = xc * lax.rsqrt(_head_mix(xc * xc, avg128) + RET_GN_EPS)
    y_ret = _silu(g_ret) * (ret * ret_gn_g)
    oc = o - _head_mix(o, avg64)
    on = oc * lax.rsqrt(_head_mix(oc * oc, avg64) + RWKV_GN_EPS) * gn_g + gn_b
    bonus = _head_mix(r * k * r_k, ones64) * v
    y_rwkv = _silu(g_rw) * (on + bonus)
    return y_ret, y_rwkv


def _post_loss(h, final_g, target):
    err = _rmsnorm(h, final_g) - target
    return 0.5 * jnp.sum(jnp.mean(err * err, axis=-1))


def _post(o, r, k, v, g_rw, ret_raw, p_ret, x, target, ret_gn_g, gn_g, gn_b, r_k, final_g, w_out, avg128, avg64, ones64):
    T = x.shape[0]
    n_tok_out = 8

    def body(o_ref, r_ref, k_ref, v_ref, grw_ref, ret_ref, gret_ref, x_ref, tgt_ref, rg_ref, gg_ref, gb_ref, rk_ref, fg_ref,
             wo_ref, a128_ref, a64_ref, ones_ref, *outs):
        tok_outs, (dwo_ref, drg_ref, dgg_ref, dgb_ref, drk_ref, dfg_ref, loss_ref) = outs[:n_tok_out], outs[n_tok_out:]
        accs = (dwo_ref, drg_ref, dgg_ref, dgb_ref, drk_ref, dfg_ref, loss_ref)

        @pl.when(pl.program_id(0) == 0)
        def _():
            for ref in accs:
                ref[...] = jnp.zeros_like(ref)

        consts = (a128_ref[...], a64_ref[...], ones_ref[...])
        (y_ret, y_rwkv), vjp = jax.vjp(
            lambda *args: _post_y(*args, *consts), o_ref[...], r_ref[...], k_ref[...], v_ref[...], grw_ref[...], ret_ref[...],
            gret_ref[...], rg_ref[...], gg_ref[...], gb_ref[...], rk_ref[...])
        h = x_ref[...] + _dot_bf(y_ret, wo_ref[0:RET_WIDTH, :]) + _dot_bf(y_rwkv, wo_ref[RET_WIDTH:, :])
        loss, (dh, dfg) = jax.value_and_grad(_post_loss, argnums=(0, 1))(h, fg_ref[...], tgt_ref[...])
        dy_ret = _dot_nt_bf(dh, wo_ref[0:RET_WIDTH, :])
        dy_rwkv = _dot_nt_bf(dh, wo_ref[RET_WIDTH:, :])
        do, dr, dk, dv, dgrw, dret, dgret, drg, dgg, dgb, drk = vjp((dy_ret, dy_rwkv))
        for ref, val in zip(tok_outs, (dh, do, dr, dk, dv, dgrw, dret, dgret)):
            ref[...] = val
        dwo_ref[0:RET_WIDTH, :] += _dot_tn_bf(y_ret, dh)
        dwo_ref[RET_WIDTH:, :] += _dot_tn_bf(y_rwkv, dh)
        for ref, val in zip(accs[1:], (drg, dgg, dgb, drk, dfg, jnp.full((1, 128), loss, F32))):
            ref[...] += val

    tile = _rows(TOK_TILE, W)
    wide = _rows(TOK_TILE, D_MODEL)
    vec = _full((1, W))
    sq = _full((256, 128))
    return pl.pallas_call(
        body, name="post", grid=(T // TOK_TILE,),
        in_specs=[tile] * 6 + [pl.BlockSpec((TOK_TILE, W), lambda i: (i, 2)), wide, wide, vec, vec, vec, vec,
                               _full((1, D_MODEL)), _full((D_MODEL, D_MODEL)), sq, sq, sq],
        out_specs=[wide] + [tile] * 7 + [_full((D_MODEL, D_MODEL)), vec, vec, vec, vec, _full((1, D_MODEL)), _full((1, 128))],
        out_shape=[jax.ShapeDtypeStruct((T, D_MODEL), F32)] + [jax.ShapeDtypeStruct((T, W), F32)] * 7
        + [jax.ShapeDtypeStruct((D_MODEL, D_MODEL), F32)] + [jax.ShapeDtypeStruct((1, W), F32)] * 4
        + [jax.ShapeDtypeStruct((1, D_MODEL), F32), jax.ShapeDtypeStruct((1, 128), F32)],
        compiler_params=_params(dimension_semantics=("arbitrary",)),
    )(o, r, k, v, g_rw, ret_raw, p_ret, x, target, ret_gn_g, gn_g, gn_b, r_k, final_g, w_out, avg128, avg64, ones64)


def _inproj_bwd_x(x, norm_g, dp_qkv, dg_ret, dp_rwkv, dh, w_in):
    T = x.shape[0]
    n_qkv = 2 * RET_QK + RET_WIDTH

    def body(x_ref, g_ref, dqkv_ref, dgret_ref, drwkv_ref, dh_ref, w_ref, dx_ref, dg_ref):
        @pl.when(pl.program_id(0) == 0)
        def _():
            dg_ref[...] = jnp.zeros_like(dg_ref)

        _, vjp = jax.vjp(_rmsnorm, x_ref[...], g_ref[...])
        du = (_dot_nt_bf(dqkv_ref[...], w_ref[:, 0:n_qkv]) + _dot_nt_bf(dgret_ref[...], w_ref[:, n_qkv:RET_COLS])
              + _dot_nt_bf(drwkv_ref[...], w_ref[:, RET_COLS:]))
        dx, dg = vjp(du)
        dx_ref[...] = dx + dh_ref[...]
        dg_ref[...] += dg

    return pl.pallas_call(
        body, name="inproj_bwd_x", grid=(T // TOK_TILE,),
        in_specs=[_rows(TOK_TILE, D_MODEL), _full((1, D_MODEL)), _rows(TOK_TILE, n_qkv), _rows(TOK_TILE, RET_WIDTH),
                  _rows(TOK_TILE, RWKV_COLS), _rows(TOK_TILE, D_MODEL), _full((D_MODEL, IN_COLS))],
        out_specs=[_rows(TOK_TILE, D_MODEL), _full((1, D_MODEL))],
        out_shape=[jax.ShapeDtypeStruct((T, D_MODEL), F32), jax.ShapeDtypeStruct((1, D_MODEL), F32)],
        compiler_params=_params(dimension_semantics=("arbitrary",)),
    )(x, norm_g, dp_qkv, dg_ret, dp_rwkv, dh, w_in)


def _grad_w(name, u, dp):
    T, n = dp.shape
    tile = 2 * TOK_TILE

    def body(u_ref, dp_ref, out_ref):
        @pl.when(pl.program_id(0) == 0)
        def _():
            out_ref[...] = jnp.zeros_like(out_ref)

        out_ref[...] += _dot_tn_bf(u_ref[...], dp_ref[...])

    return pl.pallas_call(
        body, name=name, grid=(T // tile,),
        in_specs=[_rows(tile, D_MODEL), _rows(tile, n)],
        out_specs=_full((D_MODEL, n)),
        out_shape=jax.ShapeDtypeStruct((D_MODEL, n), F32),
        compiler_params=_params(dimension_semantics=("arbitrary",)),
    )(u, dp)


def _pad_lora(w_up, top):
    z = jnp.zeros_like(w_up)
    return jnp.concatenate([w_up, z] if top else [z, w_up], axis=0)


def _local_grads(x, target, norm_g, w_in_bf, ret_gn_g, mu, w_lora_up, w0, a_lora_up, a0, k_k, k_a, r_k, gn_g, gn_b,
                 w_out_bf, final_g):
    T = x.shape[0]
    tabs = _rope_tables(T) + _ret_tables()
    ones64 = _block_mix(128, RWKV_HEAD)
    avg64 = _block_mix(128, RWKV_HEAD, 1.0 / RWKV_HEAD)
    avg128 = _block_mix(128, RET_DV, 1.0 / RET_DV)
    wup_pad, aup_pad = _pad_lora(w_lora_up, True), _pad_lora(a_lora_up, False)

    p_ret, p_rwkv, u = _inproj(x, norm_g, w_in_bf)
    ret_raw, s_saved = _ret_fwd(p_ret, tabs)
    r, w, k, v, kap, a, g_rw = _prep_fwd(p_rwkv, mu, w0, a0, k_k, k_a, wup_pad, aup_pad, ones64)
    o, s_all, sa_all = _wkv_fwd(r, w, k, v, kap, a)
    (dh, do, dr2, dk2, dv2, dgrw, dret, dgret, d_w_out, d_ret_gn_g, d_gn_g, d_gn_b, d_r_k, d_final_g, loss) = _post(
        o, r, k, v, g_rw, ret_raw, p_ret, x, target, ret_gn_g, gn_g, gn_b, r_k, final_g, w_out_bf, avg128, avg64, ones64)
    dr, dw, dk, dv, dkap, da = _wkv_bwd(r, w, k, v, kap, a, s_all, sa_all, do)
    dp_rwkv, d_mu, d_w0, d_a0, d_k_k, d_k_a, d_wup, d_aup = _prep_bwd(
        p_rwkv, mu, w0, a0, k_k, k_a, wup_pad, aup_pad, ones64, dr, dw, dk, dv, dkap, da, dgrw, dr2, dk2, dv2)
    dp_qkv = _ret_bwd(p_ret, s_saved, dret, tabs)
    dx, d_norm_g = _inproj_bwd_x(x, norm_g, dp_qkv, dgret, dp_rwkv, dh, w_in_bf)
    d_w_in = jnp.concatenate([_grad_w("grad_w_qkv", u, dp_qkv), _grad_w("grad_w_gret", u, dgret),
                              _grad_w("grad_w_rwkv", u, dp_rwkv)], axis=1)
    grads = dict(norm_g=d_norm_g, w_in=d_w_in, ret_gn_g=d_ret_gn_g, rwkv_mu=d_mu, w_lora_up=d_wup[:LORA], w0=d_w0,
                 a_lora_up=d_aup[LORA:], a0=d_a0, k_k=d_k_k, k_a=d_k_a, r_k=d_r_k, rwkv_gn_g=d_gn_g, rwkv_gn_b=d_gn_b,
                 w_out=d_w_out, final_norm_g=d_final_g)
    return loss, dx, grads


def _mesh_pos():
    return lax.axis_index("x"), lax.axis_index("y"), lax.axis_index("c")


def _all_gather(shards):
    n = len(shards)

    def body(*refs):
        x_refs, out_refs = refs[:n], refs[n:2 * n]
        send_sems, recv_sems, local_sems = refs[2 * n:]
        x, y, c = _mesh_pos()
        me, sibling = (x, y, c), (x, y, 1 - c)
        chips = [(1 - x, y), (x, 1 - y), (1 - x, 1 - y)]

        def rows(a, pos):
            m = x_refs[a].shape[0]
            return out_refs[a].at[pl.ds((4 * pos[0] + 2 * pos[1] + pos[2]) * m, m), :]

        def copy(a, k, block, to, src=None):
            return pltpu.make_async_remote_copy(
                src_ref=rows(a, block) if src is None else src, dst_ref=rows(a, block),
                send_sem=send_sems.at[a, k], recv_sem=recv_sems.at[a, k], device_id=to, device_id_type=MESH)

        mine = [pltpu.make_async_copy(x_refs[a], rows(a, me), local_sems.at[a]) for a in range(n)]
        for cp in mine:
            cp.start()
        first = []
        for a in range(n):
            first.append(copy(a, 0, me, sibling, src=x_refs[a]))
            first += [copy(a, 1 + j, me, (*chip, c), src=x_refs[a]) for j, chip in enumerate(chips)]
        for cp in first:
            cp.start()
        passed = []
        for j, chip in enumerate(chips):
            for a in range(n):
                copy(a, 1 + j, (*chip, c), me).wait_recv()
                passed.append(copy(a, 4 + j, (*chip, c), sibling))
                passed[-1].start()
        for a in range(n):
            copy(a, 0, sibling, me).wait_recv()
            for j, chip in enumerate(chips):
                copy(a, 4 + j, (*chip, 1 - c), me).wait_recv()
        for cp in first + passed:
            cp.wait_send()
        for cp in mine:
            cp.wait()

    vmem = pl.BlockSpec(memory_space=pltpu.VMEM)
    return pl.pallas_call(
        body, name="gather_weights",
        out_shape=[jax.ShapeDtypeStruct((N_DEV * s.shape[0], s.shape[1]), s.dtype) for s in shards],
        in_specs=[vmem] * n, out_specs=[vmem] * n,
        scratch_shapes=[pltpu.SemaphoreType.DMA((n, 7)), pltpu.SemaphoreType.DMA((n, 7)), pltpu.SemaphoreType.DMA((n,))],
        compiler_params=_params(),
    )(*shards)


N_CHIP = 4


def _exchange_pairs(big, small):
    nb, ns = len(big), len(small)

    def body(*refs):
        big_in, small_in = refs[:nb], refs[nb:nb + ns]
        outs = refs[nb + ns:3 * nb + 2 * ns]
        mine, theirs, small_out = outs[0:2 * nb:2], outs[1:2 * nb:2], outs[2 * nb:]
        pair_send, pair_recv, pair_local, send_sems, recv_sems, local_sems = refs[3 * nb + 2 * ns:]
        x, y, c = _mesh_pos()
        me = 4 * x + 2 * y + c
        sibling = (x, y, 1 - c)
        local = [pltpu.make_async_copy(small_in[a].at[me], small_out[a].at[me], local_sems.at[a]) for a in range(ns)]
        local += [pltpu.make_async_copy(big_in[a].at[2 * q + c], mine[a].at[q], pair_local.at[a, q])
                  for a in range(nb) for q in range(N_CHIP)]
        for cp in local:
            cp.start()
        copies = [pltpu.make_async_remote_copy(
            src_ref=big_in[a].at[2 * q + 1 - c], dst_ref=theirs[a].at[q], send_sem=pair_send.at[a, q],
            recv_sem=pair_recv.at[a, q], device_id=sibling, device_id_type=MESH) for a in range(nb) for q in range(N_CHIP)]
        for k in range(1, N_DEV):
            peer = (x ^ (k >> 2), y ^ ((k >> 1) & 1), c ^ (k & 1))
            peer_idx = 4 * peer[0] + 2 * peer[1] + peer[2]
            copies += [pltpu.make_async_remote_copy(
                src_ref=small_in[a].at[peer_idx], dst_ref=small_out[a].at[me], send_sem=send_sems.at[a, k - 1],
                recv_sem=recv_sems.at[a, k - 1], device_id=peer, device_id_type=MESH) for a in range(ns)]
        for cp in copies:
            cp.start()
        for cp in copies:
            cp.wait()
        for cp in local:
            cp.wait()

    hbm = pl.BlockSpec(memory_space=pl.ANY)
    half = lambda p: jax.ShapeDtypeStruct((N_CHIP,) + p.shape[1:], p.dtype)
    out_shape = [half(p) for p in big for _ in range(2)] + [jax.ShapeDtypeStruct(p.shape, p.dtype) for p in small]
    dma = pltpu.SemaphoreType.DMA
    res = pl.pallas_call(
        body, name="exchange_pairs", out_shape=out_shape,
        in_specs=[hbm] * (nb + ns), out_specs=[hbm] * len(out_shape),
        scratch_shapes=[dma((nb, N_CHIP)), dma((nb, N_CHIP)), dma((nb, N_CHIP)), dma((ns, 7)), dma((ns, 7)), dma((ns,))],
        compiler_params=_params(),
    )(*big, *small)
    return [(res[2 * a], res[2 * a + 1]) for a in range(nb)], res[2 * nb:]


def _pair_sum(name, mine, theirs, row_tile):
    _, rows, cols = mine.shape

    def body(a_ref, b_ref, o_ref):
        o_ref[...] = (a_ref[...].astype(F32) + b_ref[...].astype(F32)).astype(o_ref.dtype)

    spec = pl.BlockSpec((N_CHIP, row_tile, cols), lambda i: (0, i, 0))
    return pl.pallas_call(
        body, name=name, grid=(rows // row_tile,), in_specs=[spec, spec], out_specs=spec,
        out_shape=jax.ShapeDtypeStruct(mine.shape, mine.dtype),
        compiler_params=_params(dimension_semantics=("arbitrary",)),
    )(mine, theirs)


def _exchange_chips(parts):
    n = len(parts)

    def body(*refs):
        in_refs, out_refs = refs[:n], refs[n:2 * n]
        send_sems, recv_sems, local_sems = refs[2 * n:]
        x, y, c = _mesh_pos()
        my_chip = 2 * x + y
        own = [pltpu.make_async_copy(in_refs[a].at[my_chip], out_refs[a].at[my_chip], local_sems.at[a]) for a in range(n)]
        for cp in own:
            cp.start()
        copies = []
        for k in range(1, N_CHIP):
            px, py = x ^ (k >> 1), y ^ (k & 1)
            copies += [pltpu.make_async_remote_copy(
                src_ref=in_refs[a].at[2 * px + py], dst_ref=out_refs[a].at[my_chip], send_sem=send_sems.at[a, k - 1],
                recv_sem=recv_sems.at[a, k - 1], device_id=(px, py, c), device_id_type=MESH) for a in range(n)]
        for cp in copies:
            cp.start()
        for cp in copies:
            cp.wait()
        for cp in own:
            cp.wait()

    hbm = pl.BlockSpec(memory_space=pl.ANY)
    dma = pltpu.SemaphoreType.DMA
    return pl.pallas_call(
        body, name="exchange_chips",
        out_shape=[jax.ShapeDtypeStruct(p.shape, p.dtype) for p in parts],
        in_specs=[hbm] * n, out_specs=[hbm] * n,
        scratch_shapes=[dma((n, N_CHIP - 1)), dma((n, N_CHIP - 1)), dma((n,))],
        compiler_params=_params(),
    )(*parts)


def _adamw(w, g, m, v):
    m = ADAM_B1 * m + (1.0 - ADAM_B1) * g
    v = ADAM_B2 * v + (1.0 - ADAM_B2) * (g * g)
    m_hat = m / (1.0 - ADAM_B1 ** ADAM_STEP)
    v_hat = v / (1.0 - ADAM_B2 ** ADAM_STEP)
    return -ADAM_LR * (m_hat / (jnp.sqrt(v_hat) + ADAM_EPS) + ADAM_WD * w), m, v


def _reduce_adamw(name, parts, w, m, v, row_tile):
    n_parts, rows, cols = parts.shape

    def body(p_ref, w_ref, m_ref, v_ref, g_ref, d_ref, nm_ref, nv_ref):
        g = p_ref[0].astype(F32)
        for s in range(1, n_parts):
            g = g + p_ref[s].astype(F32)
        g_ref[...] = g
        d_ref[...], nm_ref[...], nv_ref[...] = _adamw(w_ref[...], g, m_ref[...], v_ref[...])

    tile = pl.BlockSpec((row_tile, cols), lambda i: (i, 0))
    return pl.pallas_call(
        body, name=name, grid=(rows // row_tile,),
        in_specs=[pl.BlockSpec((n_parts, row_tile, cols), lambda i: (0, i, 0)), tile, tile, tile],
        out_specs=[tile] * 4,
        out_shape=[jax.ShapeDtypeStruct((rows, cols), F32)] * 4,
        compiler_params=_params(dimension_semantics=("arbitrary",)),
    )(parts, w, m, v)


_SMALL = (("norm_g", 1024), ("ret_gn_g", 512), ("rwkv_mu", 2176), ("w0", 512), ("a0", 512), ("k_k", 512), ("k_a", 512),
          ("r_k", 512), ("rwkv_gn_g", 512), ("rwkv_gn_b", 512), ("final_norm_g", 1024))
_SMALL_LANES = sum(n for _, n in _SMALL) + 128
_WEIGHTS = ("norm_g", "w_in", "ret_gn_g", "rwkv_mu", "w_lora_up", "w0", "a_lora_up", "a0", "k_k", "k_a", "r_k", "rwkv_gn_g",
            "rwkv_gn_b", "w_out", "final_norm_g")


def _adamw_vectors(parts, wts, mom, var):
    k = len(_SMALL)

    def body(p_ref, *refs):
        w_refs, m_refs, v_refs, outs = refs[:k], refs[k:2 * k], refs[2 * k:3 * k], refs[3 * k:]
        g_all = p_ref[0]
        for s in range(1, N_DEV):
            g_all = g_all + p_ref[s]
        off = 0
        for i, (_, n) in enumerate(_SMALL):
            g = g_all[:, off:off + n]
            off += n
            outs[4 * i][...] = g
            outs[4 * i + 1][...], outs[4 * i + 2][...], outs[4 * i + 3][...] = _adamw(
                w_refs[i][...], g, m_refs[i][...], v_refs[i][...])
        outs[4 * k][...] = g_all[:, off:off + 128]

    vmem = pl.BlockSpec(memory_space=pltpu.VMEM)
    shapes = [jax.ShapeDtypeStruct((1, n), F32) for _, n in _SMALL for _ in range(4)] + [jax.ShapeDtypeStruct((1, 128), F32)]
    res = pl.pallas_call(
        body, name="adamw_vectors", out_shape=shapes,
        in_specs=[vmem] * (1 + 3 * k), out_specs=[vmem] * len(shapes), compiler_params=_params(),
    )(parts, *[wts[n] for n, _ in _SMALL], *[mom[n] for n, _ in _SMALL], *[var[n] for n, _ in _SMALL])
    return {n: res[4 * i:4 * i + 4] for i, (n, _) in enumerate(_SMALL)}, res[4 * k]


def kernel(x, norm_g, w_in, ret_gn_g, rwkv_mu, w_lora_up, w0, a_lora_up, a0, k_k, k_a, r_k, rwkv_gn_g, rwkv_gn_b, w_out, final_norm_g, loss_target, m_norm_g, m_w_in, m_ret_gn_g, m_rwkv_mu, m_w_lora_up, m_w0, m_a_lora_up, m_a0, m_k_k, m_k_a, m_r_k, m_rwkv_gn_g, m_rwkv_gn_b, m_w_out, m_final_norm_g, v_norm_g, v_w_in, v_ret_gn_g, v_rwkv_mu, v_w_lora_up, v_w0, v_a_lora_up, v_a0, v_k_k, v_k_a, v_r_k, v_rwkv_gn_g, v_rwkv_gn_b, v_w_out, v_final_norm_g):
    wts = dict(norm_g=norm_g, w_in=w_in, ret_gn_g=ret_gn_g, rwkv_mu=rwkv_mu, w_lora_up=w_lora_up, w0=w0, a_lora_up=a_lora_up,
               a0=a0, k_k=k_k, k_a=k_a, r_k=r_k, rwkv_gn_g=rwkv_gn_g, rwkv_gn_b=rwkv_gn_b, w_out=w_out,
               final_norm_g=final_norm_g)
    mom = dict(norm_g=m_norm_g, w_in=m_w_in, ret_gn_g=m_ret_gn_g, rwkv_mu=m_rwkv_mu, w_lora_up=m_w_lora_up, w0=m_w0,
               a_lora_up=m_a_lora_up, a0=m_a0, k_k=m_k_k, k_a=m_k_a, r_k=m_r_k, rwkv_gn_g=m_rwkv_gn_g,
               rwkv_gn_b=m_rwkv_gn_b, w_out=m_w_out, final_norm_g=m_final_norm_g)
    var = dict(norm_g=v_norm_g, w_in=v_w_in, ret_gn_g=v_ret_gn_g, rwkv_mu=v_rwkv_mu, w_lora_up=v_w_lora_up, w0=v_w0,
               a_lora_up=v_a_lora_up, a0=v_a0, k_k=v_k_k, k_a=v_k_a, r_k=v_r_k, rwkv_gn_g=v_rwkv_gn_g,
               rwkv_gn_b=v_rwkv_gn_b, w_out=v_w_out, final_norm_g=v_final_norm_g)
    shapes = {n: wts[n].shape for n in _WEIGHTS}

    g_in, g_out, g_wup, g_aup = _all_gather(
        [w_in[0].astype(BF16), w_out[0].astype(BF16), w_lora_up[0], a_lora_up[0]])
    w_in_bf = g_in.reshape(N_DEV, D_MODEL, SHARD_IN).transpose(1, 0, 2).reshape(D_MODEL, IN_COLS)
    wup_full = g_wup.reshape(N_DEV, LORA, SHARD_LORA).transpose(1, 0, 2).reshape(LORA, W)
    aup_full = g_aup.reshape(N_DEV, LORA, SHARD_LORA).transpose(1, 0, 2).reshape(LORA, W)

    loss, dx, g = _local_grads(
        x[0], loss_target[0], norm_g, w_in_bf, ret_gn_g, rwkv_mu, wup_full, w0, aup_full, a0, k_k, k_a,
        r_k.reshape(1, W), rwkv_gn_g, rwkv_gn_b, g_out, final_norm_g.reshape(1, D_MODEL))

    small = jnp.concatenate([g[n] for n, _ in _SMALL] + [loss], axis=1)
    pairs, parts = _exchange_pairs(
        [g["w_in"].reshape(D_MODEL, N_DEV, SHARD_IN).transpose(1, 0, 2).astype(BF16),
         g["w_out"].reshape(N_DEV, SHARD_OUT, D_MODEL).astype(BF16)],
        [g["w_lora_up"].reshape(LORA, N_DEV, SHARD_LORA).transpose(1, 0, 2),
         g["a_lora_up"].reshape(LORA, N_DEV, SHARD_LORA).transpose(1, 0, 2),
         jnp.broadcast_to(small[None], (N_DEV, 1, _SMALL_LANES))])
    by_chip = _exchange_chips([_pair_sum("pair_sum_w_in", *pairs[0], 256), _pair_sum("pair_sum_w_out", *pairs[1], SHARD_OUT)])
    res = {}
    res["w_in"] = _reduce_adamw("adamw_w_in", by_chip[0], w_in[0], m_w_in[0], v_w_in[0], 256)
    res["w_out"] = _reduce_adamw("adamw_w_out", by_chip[1], w_out[0], m_w_out[0], v_w_out[0], SHARD_OUT)
    res["w_lora_up"] = _reduce_adamw("adamw_w_lora_up", parts[0], w_lora_up[0], m_w_lora_up[0], v_w_lora_up[0], LORA)
    res["a_lora_up"] = _reduce_adamw("adamw_a_lora_up", parts[1], a_lora_up[0], m_a_lora_up[0], v_a_lora_up[0], LORA)
    as_row = lambda d: {n: d[n].reshape(1, size) for n, size in _SMALL}
    vec, loss_row = _adamw_vectors(parts[2], as_row(wts), as_row(mom), as_row(var))
    res.update(vec)
    res = {n: [t.reshape(shapes[n]) for t in res[n]] for n in _WEIGHTS}
    return (loss_row[0, 0], dx[None], *[res[n][0] for n in _WEIGHTS], *[res[n][1] for n in _WEIGHTS],
            *[res[n][2] for n in _WEIGHTS], *[res[n][3] for n in _WEIGHTS])
```

```python
import functools

import numpy as np
import jax
import jax.numpy as jnp
from jax import lax
from jax.experimental import pallas as pl
from jax.experimental.pallas import tpu as pltpu

F32 = jnp.float32
BF16 = jnp.bfloat16

D_MODEL = 1024
CHUNK = 64
RET_HEADS = 4
RET_DV = 128
RET_DK = 64
RET_QK = 256
RET_WIDTH = 512
RWKV_WIDTH = 512
RWKV_HEAD = 64
RWKV_HEADS = 8
LORA = 64
RET_COLS = 2 * RET_QK + 2 * RET_WIDTH
RWKV_COLS = 4 * RWKV_WIDTH + 2 * LORA
IN_COLS = RET_COLS + RWKV_COLS
ROPE_BASE = 10000.0
RMS_EPS = 1e-6
RET_GN_EPS = 1e-5
RWKV_GN_EPS = 64e-5
ADAM_LR = 0.001
ADAM_B1 = 0.9
ADAM_B2 = 0.999
ADAM_EPS = 1e-08
ADAM_WD = 0.01
ADAM_STEP = 10
N_DEV = 8
SHARD_IN = IN_COLS // N_DEV
SHARD_OUT = D_MODEL // N_DEV
SHARD_LORA = RWKV_WIDTH // N_DEV
VMEM_LIMIT = 56 * 1024 * 1024
TOK_TILE = 256
WKV_CHUNK = 32

MESH = pl.DeviceIdType.MESH


def _dot_bf(a, b):
    return jnp.dot(a.astype(BF16), b.astype(BF16), preferred_element_type=F32)


def _dot_nt_bf(a, b):
    return lax.dot_general(a.astype(BF16), b.astype(BF16), (((1,), (1,)), ((), ())), preferred_element_type=F32)


def _dot_tn_bf(a, b):
    return lax.dot_general(a.astype(BF16), b.astype(BF16), (((0,), (0,)), ((), ())), preferred_element_type=F32)


@jax.custom_vjp
def _mm(a, b):
    return _dot_bf(a, b)


@jax.custom_vjp
def _mm_nt(a, b):
    return _dot_nt_bf(a, b)


@jax.custom_vjp
def _mm_tn(a, b):
    return _dot_tn_bf(a, b)


_mm.defvjp(lambda a, b: (_dot_bf(a, b), (a, b)), lambda res, g: (_dot_nt_bf(g, res[1]), _dot_tn_bf(res[0], g)))
_mm_nt.defvjp(lambda a, b: (_dot_nt_bf(a, b), (a, b)), lambda res, g: (_dot_bf(g, res[1]), _dot_tn_bf(g, res[0])))
_mm_tn.defvjp(lambda a, b: (_dot_tn_bf(a, b), (a, b)), lambda res, g: (_dot_nt_bf(res[1], g), _dot_bf(res[0], g)))


def _trunc(x):
    return lax.bitcast_convert_type(lax.bitcast_convert_type(x, jnp.uint32) & jnp.uint32(0xFFFF0000), F32)


def _two_piece(x):
    hi = _trunc(x)
    return jnp.concatenate([hi, x - hi], axis=1)


def _mix_raw(x, mat2):
    return _unstack(jnp.dot(_two_piece(_stack(x)), mat2, preferred_element_type=F32))


@jax.custom_vjp
def _head_mix(x, mat2):
    return _mix_raw(x, mat2)


_head_mix.defvjp(lambda x, mat2: (_mix_raw(x, mat2), mat2), lambda mat2, g: (_mix_raw(g, mat2), jnp.zeros_like(mat2)))


def _swap_halves(x):
    lane = lax.broadcasted_iota(jnp.int32, x.shape, 1)
    return jnp.where((lane & (RET_DK - 1)) < RET_DK // 2, pltpu.roll(x, RET_QK - RET_DK // 2, axis=1),
                     pltpu.roll(x, RET_DK // 2, axis=1))


@jax.custom_vjp
def _rot(x):
    return _swap_halves(x)


_rot.defvjp(lambda x: (_swap_halves(x), None), lambda _, g: (_swap_halves(g),))


def _params(**kw):
    return pltpu.CompilerParams(vmem_limit_bytes=VMEM_LIMIT, **kw)


def _full(shape):
    nd = len(shape)
    return pl.BlockSpec(shape, lambda i, _nd=nd: (0,) * _nd)


def _rows(tile, width):
    return pl.BlockSpec((tile, width), lambda i: (i, 0))


def _block_mix(n, blk, scale=1.0):
    idx = np.arange(n) // blk
    m = (idx[:, None] == idx[None, :]).astype(np.float32) * scale
    return jnp.asarray(np.concatenate([m, m], axis=0))


def _rope_tables(T):
    half = RET_DK // 2
    expo = -np.arange(half, dtype=np.float32) / np.float32(half)
    freqs = np.exp(expo * np.float32(np.log(ROPE_BASE))).astype(np.float32)
    ang = np.arange(T, dtype=np.float32)[:, None] * freqs[None, :]
    cos, sin = np.cos(ang).astype(np.float32), np.sin(ang).astype(np.float32)
    cos_h = np.concatenate([cos, cos], axis=1)
    sin_h = np.concatenate([-sin, sin], axis=1)
    cos_t = np.tile(cos_h, (1, RET_HEADS))
    sin_t = np.tile(sin_h, (1, RET_HEADS))
    return jnp.asarray(cos_t), jnp.asarray(sin_t)


def _ret_tables():
    h = np.arange(RET_HEADS, dtype=np.float32)
    lg = np.log(1.0 - np.exp2(-5.0 - h)).astype(np.float32)
    idx = np.arange(CHUNK, dtype=np.float32)
    intra = np.exp(lg[:, None, None] * np.abs(idx[:, None] - idx[None, :])).astype(np.float32)
    q_dec = np.exp(lg[:, None] * (idx[None, :] + 1.0)).astype(np.float32)
    k_dec = np.exp(lg[:, None] * (CHUNK - 1.0 - idx[None, :])).astype(np.float32)
    chunk_dec = np.exp(lg * CHUNK).astype(np.float32)
    lane_head = np.arange(RET_QK) // RET_DK
    mask = (lane_head[None, :] == np.arange(RET_HEADS)[:, None]).astype(np.float32)
    m = np.broadcast_to(mask[:, None, :], (RET_HEADS, CHUNK, RET_QK)).copy()
    qd = m * q_dec[:, :, None]
    kd = m * k_dec[:, :, None]
    return jnp.asarray(intra), jnp.asarray(m), jnp.asarray(qd), jnp.asarray(kd), [float(c) for c in chunk_dec]


def _rmsnorm(x, g):
    return x * lax.rsqrt(jnp.mean(x * x, axis=-1, keepdims=True) + RMS_EPS) * g


def _inproj(x, norm_g, w_in):
    T = x.shape[0]

    def body(x_ref, g_ref, w_ref, pr_ref, pw_ref, u_ref):
        ub = _rmsnorm(x_ref[...], g_ref[...]).astype(BF16)
        u_ref[...] = ub
        pr_ref[...] = jnp.dot(ub, w_ref[:, :RET_COLS], preferred_element_type=F32)
        pw_ref[...] = jnp.dot(ub, w_ref[:, RET_COLS:], preferred_element_type=F32)

    return pl.pallas_call(
        body, name="inproj", grid=(T // TOK_TILE,),
        in_specs=[_rows(TOK_TILE, D_MODEL), _full((1, D_MODEL)), _full((D_MODEL, IN_COLS))],
        out_specs=[_rows(TOK_TILE, RET_COLS), _rows(TOK_TILE, RWKV_COLS), _rows(TOK_TILE, D_MODEL)],
        out_shape=[jax.ShapeDtypeStruct((T, RET_COLS), F32), jax.ShapeDtypeStruct((T, RWKV_COLS), F32),
                   jax.ShapeDtypeStruct((T, D_MODEL), BF16)],
        compiler_params=_params(dimension_semantics=("arbitrary",)),
    )(x, norm_g, w_in)


def _ret_chunk(pq, pk, v_heads, s_heads, cos_t, sin_t, dec, hm, qd, kd, chunk_dec):
    q = pq * cos_t + _rot(pq) * sin_t
    k = (pk * cos_t + _rot(pk) * sin_t) * (RET_DK ** -0.5)
    outs, s_out = [], []
    for h in range(RET_HEADS):
        sc = _mm_nt(q * hm[h], k * hm[h]) * dec[h]
        intra = _mm(sc, v_heads[h])
        kv = _mm_tn(k * kd[h], v_heads[h])
        inter = _mm(q * qd[h], s_heads[h])
        outs.append(intra + inter)
        s_out.append(s_heads[h] * chunk_dec[h] + kv)
    return tuple(outs), tuple(s_out)


def _ret_specs():
    const = [_full((RET_HEADS, CHUNK, CHUNK)), _full((RET_HEADS, CHUNK, RET_QK)),
             _full((RET_HEADS, CHUNK, RET_QK)), _full((RET_HEADS, CHUNK, RET_QK))]
    return const


def _ret_fwd(p_ret, tabs):
    T = p_ret.shape[0]
    nc = T // CHUNK
    cos_t, sin_t, dec, hm, qd, kd, chunk_dec = tabs

    def body(p_ref, cos_ref, sin_ref, dec_ref, hm_ref, qd_ref, kd_ref, out_ref, sin_save_ref, s_scr):
        @pl.when(pl.program_id(0) == 0)
        def _():
            s_scr[...] = jnp.zeros_like(s_scr)

        s_heads = tuple(s_scr[h] for h in range(RET_HEADS))
        for h in range(RET_HEADS):
            sin_save_ref[0, h] = s_heads[h]
        pq = p_ref[:, 0:RET_QK]
        pk = p_ref[:, RET_QK:2 * RET_QK]
        v_heads = tuple(p_ref[:, 2 * RET_QK + RET_DV * h:2 * RET_QK + RET_DV * (h + 1)] for h in range(RET_HEADS))
        outs, s_out = _ret_chunk(pq, pk, v_heads, s_heads, cos_ref[...], sin_ref[...], dec_ref[...],
                                 hm_ref[...], qd_ref[...], kd_ref[...], chunk_dec)
        for h in range(RET_HEADS):
            out_ref[:, RET_DV * h:RET_DV * (h + 1)] = outs[h]
            s_scr[h] = s_out[h]

    return pl.pallas_call(
        body, name="ret_fwd", grid=(nc,),
        in_specs=[pl.BlockSpec((CHUNK, RET_COLS), lambda i: (i, 0)), _rows(CHUNK, RET_QK), _rows(CHUNK, RET_QK)] + _ret_specs(),
        out_specs=[_rows(CHUNK, RET_WIDTH), pl.BlockSpec((1, RET_HEADS, RET_QK, RET_DV), lambda i: (i, 0, 0, 0))],
        out_shape=[jax.ShapeDtypeStruct((T, RET_WIDTH), F32), jax.ShapeDtypeStruct((nc, RET_HEADS, RET_QK, RET_DV), F32)],
        scratch_shapes=[pltpu.VMEM((RET_HEADS, RET_QK, RET_DV), F32)],
        compiler_params=_params(dimension_semantics=("arbitrary",)),
    )(p_ret, cos_t, sin_t, dec, hm, qd, kd)


def _ret_bwd(p_ret, s_saved, d_ret, tabs):
    T = p_ret.shape[0]
    nc = T // CHUNK
    cos_t, sin_t, dec, hm, qd, kd, chunk_dec = tabs

    def body(p_ref, s_ref, dret_ref, cos_ref, sin_ref, dec_ref, hm_ref, qd_ref, kd_ref, dp_ref, ds_scr):
        @pl.when(pl.program_id(0) == 0)
        def _():
            ds_scr[...] = jnp.zeros_like(ds_scr)

        pq = p_ref[:, 0:RET_QK]
        pk = p_ref[:, RET_QK:2 * RET_QK]
        v_heads = tuple(p_ref[:, 2 * RET_QK + RET_DV * h:2 * RET_QK + RET_DV * (h + 1)] for h in range(RET_HEADS))
        s_heads = tuple(s_ref[0, h] for h in range(RET_HEADS))
        consts = (cos_ref[...], sin_ref[...], dec_ref[...], hm_ref[...], qd_ref[...], kd_ref[...])
        _, vjp = jax.vjp(lambda a, b, c, d: _ret_chunk(a, b, c, d, *consts, chunk_dec), pq, pk, v_heads, s_heads)
        d_out = tuple(dret_ref[:, RET_DV * h:RET_DV * (h + 1)] for h in range(RET_HEADS))
        d_s = tuple(ds_scr[h] for h in range(RET_HEADS))
        dq, dk, dv, ds_in = vjp((d_out, d_s))
        dp_ref[:, 0:RET_QK] = dq
        dp_ref[:, RET_QK:2 * RET_QK] = dk
        for h in range(RET_HEADS):
            dp_ref[:, 2 * RET_QK + RET_DV * h:2 * RET_QK + RET_DV * (h + 1)] = dv[h]
            ds_scr[h] = ds_in[h]

    rev = lambda i: (nc - 1 - i, 0)
    return pl.pallas_call(
        body, name="ret_bwd", grid=(nc,),
        in_specs=[pl.BlockSpec((CHUNK, RET_COLS), rev),
                  pl.BlockSpec((1, RET_HEADS, RET_QK, RET_DV), lambda i: (nc - 1 - i, 0, 0, 0)),
                  pl.BlockSpec((CHUNK, RET_WIDTH), rev), pl.BlockSpec((CHUNK, RET_QK), rev), pl.BlockSpec((CHUNK, RET_QK), rev)]
        + _ret_specs(),
        out_specs=pl.BlockSpec((CHUNK, 2 * RET_QK + RET_WIDTH), rev),
        out_shape=jax.ShapeDtypeStruct((T, 2 * RET_QK + RET_WIDTH), F32),
        scratch_shapes=[pltpu.VMEM((RET_HEADS, RET_QK, RET_DV), F32)],
        compiler_params=_params(dimension_semantics=("arbitrary",)),
    )(p_ret, s_saved, d_ret, cos_t, sin_t, dec, hm, qd, kd)


def _wkv_consts():
    lane = lax.broadcasted_iota(jnp.int32, (RWKV_HEAD, RWKV_WIDTH), 1)
    sub = lax.broadcasted_iota(jnp.int32, (RWKV_HEAD, RWKV_WIDTH), 0)
    diag = ((lane & (RWKV_HEAD - 1)) == sub).astype(F32)
    r = lax.broadcasted_iota(jnp.int32, (3 * 128, 128), 0)
    c = lax.broadcasted_iota(jnp.int32, (3 * 128, 128), 1)
    ones = (((r & 127) >> 6) == (c >> 6)).astype(BF16)
    return diag, ones


def _stack(x):
    return jnp.concatenate([x[:, 128 * p:128 * (p + 1)] for p in range(4)], axis=0)


def _unstack(y):
    n = y.shape[0] // 4
    return jnp.concatenate([y[n * p:n * (p + 1)] for p in range(4)], axis=1)


def _split(x, n):
    pieces = []
    for _ in range(n):
        p = x.astype(BF16)
        pieces.append(p)
        x = x - p.astype(F32)
    return pieces


def _lane_sum(x, ones):
    return _unstack(jnp.dot(_two_piece(_stack(x)), ones[:256].astype(F32), preferred_element_type=F32))


def _lane_sum_bf(x, ones):
    return _unstack(jnp.dot(_stack(x).astype(BF16), ones[:128], preferred_element_type=F32))


def _expand(row, diag_bf, ones, n):
    lhs = jnp.concatenate([_stack(jnp.broadcast_to(p, diag_bf.shape) * diag_bf) for p in _split(row, n)], axis=1)
    return _unstack(jnp.dot(lhs, ones[:128 * n], preferred_element_type=F32))


def _colsum(x):
    return jnp.sum(x, axis=0, keepdims=True)


def _head_sums(x, ones):
    return _unstack(jnp.dot(jnp.concatenate(_split(_stack(x), 3), axis=1), ones, preferred_element_type=F32))


def _wkv_fwd(r, w, k, v, kap, a):
    T = r.shape[0]
    C = WKV_CHUNK
    nc = T // C

    def body(r_ref, w_ref, k_ref, v_ref, kap_ref, a_ref, o_ref, s_all_ref, sa_all_ref, s_scr, vf_scr):
        @pl.when(pl.program_id(0) == 0)
        def _():
            s_scr[...] = jnp.zeros_like(s_scr)

        diag, ones = _wkv_consts()
        diag_bf = diag.astype(BF16)
        rr, ww, kk, vv, kap_, aa = (ref[...] for ref in (r_ref, w_ref, k_ref, v_ref, kap_ref, a_ref))
        bb = kap_ * aa
        c1 = _head_sums(pltpu.roll(bb, 1, axis=0) * kap_, ones)
        row = lambda x, t: x[t:t + 1]

        for t in range(C):
            vf_scr[t] = _expand(row(vv, t), diag_bf, ones, 2)

        s_prev = s_scr[...]
        sa = _lane_sum(s_prev * (-row(kap_, 0)), ones)
        ls, rows = None, []

        def emit_o(t, s_t):
            rows.append(_colsum(_lane_sum_bf(s_t * row(rr, t), ones) * diag))
            if t % 8 == 7:
                o_ref[t - 7:t + 1, :] = jnp.concatenate(rows, axis=0)
                rows.clear()

        for t in range(C):
            u = s_prev * row(ww, t) + vf_scr[t] * row(kk, t)
            if t > 0:
                sa = ls - sa * row(c1, t)
            if t + 1 < C:
                ls = _lane_sum(u * (-row(kap_, t + 1)), ones)
            if t > 0:
                emit_o(t - 1, s_prev)
            s_prev = u + sa * row(bb, t)
            s_all_ref[t] = s_prev
            sa_all_ref[t] = sa.astype(BF16)
        emit_o(C - 1, s_prev)
        s_scr[...] = s_prev

    spec = _rows(C, RWKV_WIDTH)
    return pl.pallas_call(
        body, name="wkv_fwd", grid=(nc,),
        in_specs=[spec] * 6,
        out_specs=[spec, pl.BlockSpec((C, RWKV_HEAD, RWKV_WIDTH), lambda i: (i, 0, 0)),
                   pl.BlockSpec((C, RWKV_HEAD, RWKV_WIDTH), lambda i: (i, 0, 0))],
        out_shape=[jax.ShapeDtypeStruct((T, RWKV_WIDTH), F32), jax.ShapeDtypeStruct((T, RWKV_HEAD, RWKV_WIDTH), F32),
                   jax.ShapeDtypeStruct((T, RWKV_HEAD, RWKV_WIDTH), BF16)],
        scratch_shapes=[pltpu.VMEM((RWKV_HEAD, RWKV_WIDTH), F32), pltpu.VMEM((C, RWKV_HEAD, RWKV_WIDTH), F32)],
        compiler_params=_params(dimension_semantics=("arbitrary",)),
    )(r, w, k, v, kap, a)


def _wkv_bwd(r, w, k, v, kap, a, s_all, sa_all, d_o):
    T = r.shape[0]
    C = WKV_CHUNK
    nc = T // C

    def body(r_ref, w_ref, k_ref, v_ref, kap_ref, a_ref, s_ref, s_before_ref, sa_ref, do_ref,
             dr_ref, dw_ref, dk_ref, dv_ref, dkap_ref, da_ref, ds_scr, vf_scr, dof_scr):
        first_chunk = pl.program_id(0) == nc - 1

        @pl.when(pl.program_id(0) == 0)
        def _():
            ds_scr[...] = jnp.zeros_like(ds_scr)

        diag, ones = _wkv_consts()
        diag_bf = diag.astype(BF16)
        rr, ww, kk, vv, kap_, aa, dd = (ref[...] for ref in (r_ref, w_ref, k_ref, v_ref, kap_ref, a_ref, do_ref))
        bb = kap_ * aa
        e1 = _head_sums(pltpu.roll(kap_, C - 1, axis=0) * bb, ones)
        row = lambda x, t: x[t:t + 1]

        def state_before(t):
            return s_ref[t - 1] if t > 0 else jnp.where(first_chunk, 0.0, s_before_ref[0])

        for t in range(C):
            vf_scr[t] = _expand(row(vv, t), diag_bf, ones, 1)
            dof_scr[t] = _expand(row(dd, t), diag_bf, ones, 2)

        d_sn, dsa, rows = None, None, [None] * C

        def emit_rows(t, d_sn_t, dsa_t):
            s_prev, dof = state_before(t), dof_scr[t]
            dv = _colsum(_lane_sum_bf(d_sn_t * row(kk, t), ones) * diag)
            db = _colsum(d_sn_t * sa_ref[t].astype(F32))
            rows[t] = (_colsum(s_ref[t] * dof), _colsum(d_sn_t * s_prev), _colsum(d_sn_t * vf_scr[t]), dv,
                       db * row(aa, t) - _colsum(dsa_t * s_prev), db * row(kap_, t))
            if t % 8 == 0:
                for j, ref in enumerate((dr_ref, dw_ref, dk_ref, dv_ref, dkap_ref, da_ref)):
                    ref[t:t + 8, :] = jnp.concatenate([rows[u][j] for u in range(t, t + 8)], axis=0)

        for t in reversed(range(C)):
            dof = dof_scr[t]
            if t == C - 1:
                d_sn = ds_scr[...] + dof * row(rr, t)
                dsa = _lane_sum(d_sn * row(bb, t), ones)
            else:
                v_t = d_sn * row(ww, t + 1) + dof * row(rr, t)
                ls = _lane_sum(v_t * row(bb, t), ones)
                emit_rows(t + 1, d_sn, dsa)
                d_sn = v_t - dsa * row(kap_, t + 1)
                dsa = ls - dsa * row(e1, t)
        emit_rows(0, d_sn, dsa)
        d_s = d_sn * row(ww, 0) - dsa * row(kap_, 0)
        ds_scr[...] = d_s

    spec = pl.BlockSpec((C, RWKV_WIDTH), lambda i: (nc - 1 - i, 0))
    states = pl.BlockSpec((C, RWKV_HEAD, RWKV_WIDTH), lambda i: (nc - 1 - i, 0, 0))
    before = pl.BlockSpec((1, RWKV_HEAD, RWKV_WIDTH), lambda i: (jnp.maximum((nc - 1 - i) * C - 1, 0), 0, 0))
    big = pltpu.VMEM((C, RWKV_HEAD, RWKV_WIDTH), F32)
    return pl.pallas_call(
        body, name="wkv_bwd", grid=(nc,),
        in_specs=[spec] * 6 + [states, before, states, spec],
        out_specs=[spec] * 6,
        out_shape=[jax.ShapeDtypeStruct((T, RWKV_WIDTH), F32)] * 6,
        scratch_shapes=[pltpu.VMEM((RWKV_HEAD, RWKV_WIDTH), F32), big, big],
        compiler_params=_params(dimension_semantics=("arbitrary",)),
    )(r, w, k, v, kap, a, s_all, s_all, sa_all, d_o)


W = RWKV_WIDTH


def _softplus(y):
    return jnp.maximum(y, 0.0) + jnp.log(1.0 + jnp.exp(-jnp.abs(y)))


def _prep_fn(kr, xwa, w0, a0, k_k, k_a, wup_pad, aup_pad, ones64):
    w_log = -_softplus(-(w0 + _mm(jnp.tanh(xwa), wup_pad))) - 0.5
    decay = jnp.exp(-jnp.exp(w_log))
    a = jax.nn.sigmoid(a0 + _mm(xwa, aup_pad))
    kk = kr * k_k
    kap = kk / jnp.maximum(jnp.sqrt(_head_mix(kk * kk, ones64)), 1e-12)
    k = kr * (1.0 + (a - 1.0) * k_a)
    return decay, k, kap, a


def _shift_down(p, first_row):
    rows = lax.broadcasted_iota(jnp.int32, p.shape, 0)
    return jnp.where(rows == 0, first_row, pltpu.roll(p, 1, axis=0))


def _shift_up(z, last_row):
    n = z.shape[0]
    rows = lax.broadcasted_iota(jnp.int32, z.shape, 0)
    return jnp.where(rows == n - 1, last_row, pltpu.roll(z, n - 1, axis=0))


def _prev_block_spec():
    return pl.BlockSpec((8, RWKV_COLS), lambda i: (jnp.maximum(i * (TOK_TILE // 8) - 1, 0), 0))


def _mixed(p_ref, prev8_ref, mu_ref, first_tile):
    p = p_ref[...]
    first_row = jnp.where(first_tile, 0.0, prev8_ref[7:8, :])
    prev = _shift_down(p, first_row)
    return p, prev, p + mu_ref[...] * (prev - p)


def _prep_fwd(p_rwkv, mu, w0, a0, k_k, k_a, wup_pad, aup_pad, ones64):
    T = p_rwkv.shape[0]

    def body(p_ref, prev8_ref, mu_ref, w0_ref, a0_ref, kk_ref, ka_ref, wup_ref, aup_ref, ones_ref,
             r_ref, w_ref, k_ref, v_ref, kap_ref, a_ref, g_ref):
        _, _, ps = _mixed(p_ref, prev8_ref, mu_ref, pl.program_id(0) == 0)
        decay, k, kap, a = _prep_fn(ps[:, W:2 * W], ps[:, 4 * W:], w0_ref[...], a0_ref[...], kk_ref[...], ka_ref[...],
                                    wup_ref[...], aup_ref[...], ones_ref[...])
        r_ref[...] = ps[:, 0:W]
        w_ref[...] = decay
        k_ref[...] = k
        v_ref[...] = ps[:, 2 * W:3 * W]
        kap_ref[...] = kap
        a_ref[...] = a
        g_ref[...] = ps[:, 3 * W:4 * W]

    vec = _full((1, W))
    return pl.pallas_call(
        body, name="prep_fwd", grid=(T // TOK_TILE,),
        in_specs=[_rows(TOK_TILE, RWKV_COLS), _prev_block_spec(), _full((1, RWKV_COLS)), vec, vec, vec, vec,
                  _full((2 * LORA, W)), _full((2 * LORA, W)), _full((256, 128))],
        out_specs=[_rows(TOK_TILE, W)] * 7,
        out_shape=[jax.ShapeDtypeStruct((T, W), F32)] * 7,
        compiler_params=_params(dimension_semantics=("arbitrary",)),
    )(p_rwkv, p_rwkv, mu, w0, a0, k_k, k_a, wup_pad, aup_pad, ones64)


def _prep_bwd(p_rwkv, mu, w0, a0, k_k, k_a, wup_pad, aup_pad, ones64, dr, dw, dk, dv, dkap, da, dg, dr2, dk2, dv2):
    T = p_rwkv.shape[0]
    nt = T // TOK_TILE

    def body(p_ref, prev8_ref, mu_ref, w0_ref, a0_ref, kk_ref, ka_ref, wup_ref, aup_ref, ones_ref,
             dr_ref, dw_ref, dk_ref, dv_ref, dkap_ref, da_ref, dg_ref, dr2_ref, dk2_ref, dv2_ref,
             dp_ref, dmu_ref, dw0_ref, da0_ref, dkk_ref, dka_ref, dwup_ref, daup_ref, zrow_scr):
        i = pl.program_id(0)
        accs = (dmu_ref, dw0_ref, da0_ref, dkk_ref, dka_ref, dwup_ref, daup_ref)

        @pl.when(i == 0)
        def _():
            zrow_scr[...] = jnp.zeros_like(zrow_scr)
            for ref in accs:
                ref[...] = jnp.zeros_like(ref)

        p, prev, ps = _mixed(p_ref, prev8_ref, mu_ref, i == nt - 1)
        ones = ones_ref[...]
        _, vjp = jax.vjp(lambda *args: _prep_fn(*args, ones), ps[:, W:2 * W], ps[:, 4 * W:], w0_ref[...], a0_ref[...],
                         kk_ref[...], ka_ref[...], wup_ref[...], aup_ref[...])
        dkr, dxwa, dw0, da0, dkk, dka, dwup, daup = vjp(
            (dw_ref[...], dk_ref[...] + dk2_ref[...], dkap_ref[...], da_ref[...]))
        dps = jnp.concatenate([dr_ref[...] + dr2_ref[...], dkr, dv_ref[...] + dv2_ref[...], dg_ref[...], dxwa], axis=1)
        z = dps * mu_ref[...]
        dp_ref[...] = dps - z + _shift_up(z, zrow_scr[0:1, :])
        zrow_scr[0:1, :] = z[0:1, :]
        for ref, val in zip(accs, (_colsum(dps * (prev - p)), dw0, da0, dkk, dka, dwup, daup)):
            ref[...] += val

    rev = lambda i: (nt - 1 - i, 0)
    vec = _full((1, W))
    lora = _full((2 * LORA, W))
    tile = pl.BlockSpec((TOK_TILE, W), rev)
    prev8 = pl.BlockSpec((8, RWKV_COLS), lambda i: (jnp.maximum((nt - 1 - i) * (TOK_TILE // 8) - 1, 0), 0))
    return pl.pallas_call(
        body, name="prep_bwd", grid=(nt,),
        in_specs=[pl.BlockSpec((TOK_TILE, RWKV_COLS), rev), prev8, _full((1, RWKV_COLS)), vec, vec, vec, vec, lora, lora,
                  _full((256, 128))] + [tile] * 10,
        out_specs=[pl.BlockSpec((TOK_TILE, RWKV_COLS), rev), _full((1, RWKV_COLS)), vec, vec, vec, vec, lora, lora],
        out_shape=[jax.ShapeDtypeStruct((T, RWKV_COLS), F32), jax.ShapeDtypeStruct((1, RWKV_COLS), F32)]
        + [jax.ShapeDtypeStruct((1, W), F32)] * 4 + [jax.ShapeDtypeStruct((2 * LORA, W), F32)] * 2,
        scratch_shapes=[pltpu.VMEM((8, RWKV_COLS), F32)],
        compiler_params=_params(dimension_semantics=("arbitrary",)),
    )(p_rwkv, p_rwkv, mu, w0, a0, k_k, k_a, wup_pad, aup_pad, ones64, dr, dw, dk, dv, dkap, da, dg, dr2, dk2, dv2)


def _silu(x):
    return x * jax.nn.sigmoid(x)


def _post_y(o, r, k, v, g_rw, ret_raw, g_ret, ret_gn_g, gn_g, gn_b, r_k, avg128, avg64, ones64):
    xc = ret_raw - _head_mix(ret_raw, avg128)
    ret = xc * lax.rsqrt(_head_mix(xc * xc, avg128) + RET_GN_EPS)
    y_ret = _silu(g_ret) * (ret * ret_gn_g)
    oc = o - _head_mix(o, avg64)
    on = oc * lax.rsqrt(_head_mix(oc * oc, avg64) + RWKV_GN_EPS) * gn_g + gn_b
    bonus = _head_mix(r * k * r_k, ones64) * v
    y_rwkv = _silu(g_rw) * (on + bonus)
    return y_ret, y_rwkv


def _post_loss(h, final_g, target):
    err = _rmsnorm(h, final_g) - target
    return 0.5 * jnp.sum(jnp.mean(err * err, axis=-1))


def _post(o, r, k, v, g_rw, ret_raw, p_ret, x, target, ret_gn_g, gn_g, gn_b, r_k, final_g, w_out, avg128, avg64, ones64):
    T = x.shape[0]
    n_tok_out = 8

    def body(o_ref, r_ref, k_ref, v_ref, grw_ref, ret_ref, gret_ref, x_ref, tgt_ref, rg_ref, gg_ref, gb_ref, rk_ref, fg_ref,
             wo_ref, a128_ref, a64_ref, ones_ref, *outs):
        tok_outs, (dwo_ref, drg_ref, dgg_ref, dgb_ref, drk_ref, dfg_ref, loss_ref) = outs[:n_tok_out], outs[n_tok_out:]
        accs = (dwo_ref, drg_ref, dgg_ref, dgb_ref, drk_ref, dfg_ref, loss_ref)

        @pl.when(pl.program_id(0) == 0)
        def _():
            for ref in accs:
                ref[...] = jnp.zeros_like(ref)

        consts = (a128_ref[...], a64_ref[...], ones_ref[...])
        (y_ret, y_rwkv), vjp = jax.vjp(
            lambda *args: _post_y(*args, *consts), o_ref[...], r_ref[...], k_ref[...], v_ref[...], grw_ref[...], ret_ref[...],
            gret_ref[...], rg_ref[...], gg_ref[...], gb_ref[...], rk_ref[...])
        h = x_ref[...] + _dot_bf(y_ret, wo_ref[0:RET_WIDTH, :]) + _dot_bf(y_rwkv, wo_ref[RET_WIDTH:, :])
        loss, (dh, dfg) = jax.value_and_grad(_post_loss, argnums=(0, 1))(h, fg_ref[...], tgt_ref[...])
        dy_ret = _dot_nt_bf(dh, wo_ref[0:RET_WIDTH, :])
        dy_rwkv = _dot_nt_bf(dh, wo_ref[RET_WIDTH:, :])
        do, dr, dk, dv, dgrw, dret, dgret, drg, dgg, dgb, drk = vjp((dy_ret, dy_rwkv))
        for ref, val in zip(tok_outs, (dh, do, dr, dk, dv, dgrw, dret, dgret)):
            ref[...] = val
        dwo_ref[0:RET_WIDTH, :] += _dot_tn_bf(y_ret, dh)
        dwo_ref[RET_WIDTH:, :] += _dot_tn_bf(y_rwkv, dh)
        for ref, val in zip(accs[1:], (drg, dgg, dgb, drk, dfg, jnp.full((1, 128), loss, F32))):
            ref[...] += val

    tile = _rows(TOK_TILE, W)
    wide = _rows(TOK_TILE, D_MODEL)
    vec = _full((1, W))
    sq = _full((256, 128))
    return pl.pallas_call(
        body, name="post", grid=(T // TOK_TILE,),
        in_specs=[tile] * 6 + [pl.BlockSpec((TOK_TILE, W), lambda i: (i, 2)), wide, wide, vec, vec, vec, vec,
                               _full((1, D_MODEL)), _full((D_MODEL, D_MODEL)), sq, sq, sq],
        out_specs=[wide] + [tile] * 7 + [_full((D_MODEL, D_MODEL)), vec, vec, vec, vec, _full((1, D_MODEL)), _full((1, 128))],
        out_shape=[jax.ShapeDtypeStruct((T, D_MODEL), F32)] + [jax.ShapeDtypeStruct((T, W), F32)] * 7
        + [jax.ShapeDtypeStruct((D_MODEL, D_MODEL), F32)] + [jax.ShapeDtypeStruct((1, W), F32)] * 4
        + [jax.ShapeDtypeStruct((1, D_MODEL), F32), jax.ShapeDtypeStruct((1, 128), F32)],
        compiler_params=_params(dimension_semantics=("arbitrary",)),
    )(o, r, k, v, g_rw, ret_raw, p_ret, x, target, ret_gn_g, gn_g, gn_b, r_k, final_g, w_out, avg128, avg64, ones64)


def _inproj_bwd_x(x, norm_g, dp_qkv, dg_ret, dp_rwkv, dh, w_in):
    T = x.shape[0]
    n_qkv = 2 * RET_QK + RET_WIDTH

    def body(x_ref, g_ref, dqkv_ref, dgret_ref, drwkv_ref, dh_ref, w_ref, dx_ref, dg_ref):
        @pl.when(pl.program_id(0) == 0)
        def _():
            dg_ref[...] = jnp.zeros_like(dg_ref)

        _, vjp = jax.vjp(_rmsnorm, x_ref[...], g_ref[...])
        du = (_dot_nt_bf(dqkv_ref[...], w_ref[:, 0:n_qkv]) + _dot_nt_bf(dgret_ref[...], w_ref[:, n_qkv:RET_COLS])
              + _dot_nt_bf(drwkv_ref[...], w_ref[:, RET_COLS:]))
        dx, dg = vjp(du)
        dx_ref[...] = dx + dh_ref[...]
        dg_ref[...] += dg

    return pl.pallas_call(
        body, name="inproj_bwd_x", grid=(T // TOK_TILE,),
        in_specs=[_rows(TOK_TILE, D_MODEL), _full((1, D_MODEL)), _rows(TOK_TILE, n_qkv), _rows(TOK_TILE, RET_WIDTH),
                  _rows(TOK_TILE, RWKV_COLS), _rows(TOK_TILE, D_MODEL), _full((D_MODEL, IN_COLS))],
        out_specs=[_rows(TOK_TILE, D_MODEL), _full((1, D_MODEL))],
        out_shape=[jax.ShapeDtypeStruct((T, D_MODEL), F32), jax.ShapeDtypeStruct((1, D_MODEL), F32)],
        compiler_params=_params(dimension_semantics=("arbitrary",)),
    )(x, norm_g, dp_qkv, dg_ret, dp_rwkv, dh, w_in)


def _grad_w(name, u, dp):
    T, n = dp.shape
    tile = 2 * TOK_TILE

    def body(u_ref, dp_ref, out_ref):
        @pl.when(pl.program_id(0) == 0)
        def _():
            out_ref[...] = jnp.zeros_like(out_ref)

        out_ref[...] += _dot_tn_bf(u_ref[...], dp_ref[...])

    return pl.pallas_call(
        body, name=name, grid=(T // tile,),
        in_specs=[_rows(tile, D_MODEL), _rows(tile, n)],
        out_specs=_full((D_MODEL, n)),
        out_shape=jax.ShapeDtypeStruct((D_MODEL, n), F32),
        compiler_params=_params(dimension_semantics=("arbitrary",)),
    )(u, dp)


def _pad_lora(w_up, top):
    z = jnp.zeros_like(w_up)
    return jnp.concatenate([w_up, z] if top else [z, w_up], axis=0)


def _local_grads(x, target, norm_g, w_in_bf, ret_gn_g, mu, w_lora_up, w0, a_lora_up, a0, k_k, k_a, r_k, gn_g, gn_b,
                 w_out_bf, final_g):
    T = x.shape[0]
    tabs = _rope_tables(T) + _ret_tables()
    ones64 = _block_mix(128, RWKV_HEAD)
    avg64 = _block_mix(128, RWKV_HEAD, 1.0 / RWKV_HEAD)
    avg128 = _block_mix(128, RET_DV, 1.0 / RET_DV)
    wup_pad, aup_pad = _pad_lora(w_lora_up, True), _pad_lora(a_lora_up, False)

    p_ret, p_rwkv, u = _inproj(x, norm_g, w_in_bf)
    ret_raw, s_saved = _ret_fwd(p_ret, tabs)
    r, w, k, v, kap, a, g_rw = _prep_fwd(p_rwkv, mu, w0, a0, k_k, k_a, wup_pad, aup_pad, ones64)
    o, s_all, sa_all = _wkv_fwd(r, w, k, v, kap, a)
    (dh, do, dr2, dk2, dv2, dgrw, dret, dgret, d_w_out, d_ret_gn_g, d_gn_g, d_gn_b, d_r_k, d_final_g, loss) = _post(
        o, r, k, v, g_rw, ret_raw, p_ret, x, target, ret_gn_g, gn_g, gn_b, r_k, final_g, w_out_bf, avg128, avg64, ones64)
    dr, dw, dk, dv, dkap, da = _wkv_bwd(r, w, k, v, kap, a, s_all, sa_all, do)
    dp_rwkv, d_mu, d_w0, d_a0, d_k_k, d_k_a, d_wup, d_aup = _prep_bwd(
        p_rwkv, mu, w0, a0, k_k, k_a, wup_pad, aup_pad, ones64, dr, dw, dk, dv, dkap, da, dgrw, dr2, dk2, dv2)
    dp_qkv = _ret_bwd(p_ret, s_saved, dret, tabs)
    dx, d_norm_g = _inproj_bwd_x(x, norm_g, dp_qkv, dgret, dp_rwkv, dh, w_in_bf)
    d_w_in = jnp.concatenate([_grad_w("grad_w_qkv", u, dp_qkv), _grad_w("grad_w_gret", u, dgret),
                              _grad_w("grad_w_rwkv", u, dp_rwkv)], axis=1)
    grads = dict(norm_g=d_norm_g, w_in=d_w_in, ret_gn_g=d_ret_gn_g, rwkv_mu=d_mu, w_lora_up=d_wup[:LORA], w0=d_w0,
                 a_lora_up=d_aup[LORA:], a0=d_a0, k_k=d_k_k, k_a=d_k_a, r_k=d_r_k, rwkv_gn_g=d_gn_g, rwkv_gn_b=d_gn_b,
                 w_out=d_w_out, final_norm_g=d_final_g)
    return loss, dx, grads


def _mesh_pos():
    return lax.axis_index("x"), lax.axis_index("y"), lax.axis_index("c")


def _all_gather(shards):
    n = len(shards)

    def body(*refs):
        x_refs, out_refs = refs[:n], refs[n:2 * n]
        send_sems, recv_sems, local_sems = refs[2 * n:]
        x, y, c = _mesh_pos()
        me, sibling = (x, y, c), (x, y, 1 - c)
        chips = [(1 - x, y), (x, 1 - y), (1 - x, 1 - y)]

        def rows(a, pos):
            m = x_refs[a].shape[0]
            return out_refs[a].at[pl.ds((4 * pos[0] + 2 * pos[1] + pos[2]) * m, m), :]

        def copy(a, k, block, to, src=None):
            return pltpu.make_async_remote_copy(
                src_ref=rows(a, block) if src is None else src, dst_ref=rows(a, block),
                send_sem=send_sems.at[a, k], recv_sem=recv_sems.at[a, k], device_id=to, device_id_type=MESH)

        mine = [pltpu.make_async_copy(x_refs[a], rows(a, me), local_sems.at[a]) for a in range(n)]
        for cp in mine:
            cp.start()
        first = []
        for a in range(n):
            first.append(copy(a, 0, me, sibling, src=x_refs[a]))
            first += [copy(a, 1 + j, me, (*chip, c), src=x_refs[a]) for j, chip in enumerate(chips)]
        for cp in first:
            cp.start()
        passed = []
        for j, chip in enumerate(chips):
            for a in range(n):
                copy(a, 1 + j, (*chip, c), me).wait_recv()
                passed.append(copy(a, 4 + j, (*chip, c), sibling))
                passed[-1].start()
        for a in range(n):
            copy(a, 0, sibling, me).wait_recv()
            for j, chip in enumerate(chips):
                copy(a, 4 + j, (*chip, 1 - c), me).wait_recv()
        for cp in first + passed:
            cp.wait_send()
        for cp in mine:
            cp.wait()

    vmem = pl.BlockSpec(memory_space=pltpu.VMEM)
    return pl.pallas_call(
        body, name="gather_weights",
        out_shape=[jax.ShapeDtypeStruct((N_DEV * s.shape[0], s.shape[1]), s.dtype) for s in shards],
        in_specs=[vmem] * n, out_specs=[vmem] * n,
        scratch_shapes=[pltpu.SemaphoreType.DMA((n, 7)), pltpu.SemaphoreType.DMA((n, 7)), pltpu.SemaphoreType.DMA((n,))],
        compiler_params=_params(),
    )(*shards)


N_CHIP = 4


def _exchange_pairs(big, small):
    nb, ns = len(big), len(small)

    def body(*refs):
        big_in, small_in = refs[:nb], refs[nb:nb + ns]
        theirs, small_out = refs[nb + ns:2 * nb + ns], refs[2 * nb + ns:2 * nb + 2 * ns]
        pair_send, pair_recv, send_sems, recv_sems, local_sems = refs[2 * nb + 2 * ns:]
        x, y, c = _mesh_pos()
        me = 4 * x + 2 * y + c
        local = [pltpu.make_async_copy(small_in[a].at[me], small_out[a].at[me], local_sems.at[a]) for a in range(ns)]
        for cp in local:
            cp.start()
        copies = [pltpu.make_async_remote_copy(
            src_ref=big_in[a], dst_ref=theirs[a], send_sem=pair_send.at[a], recv_sem=pair_recv.at[a],
            device_id=(x, y, 1 - c), device_id_type=MESH) for a in range(nb)]
        for k in range(1, N_DEV):
            peer = (x ^ (k >> 2), y ^ ((k >> 1) & 1), c ^ (k & 1))
            peer_idx = 4 * peer[0] + 2 * peer[1] + peer[2]
            copies += [pltpu.make_async_remote_copy(
                src_ref=small_in[a].at[peer_idx], dst_ref=small_out[a].at[me], send_sem=send_sems.at[a, k - 1],
                recv_sem=recv_sems.at[a, k - 1], device_id=peer, device_id_type=MESH) for a in range(ns)]
        for cp in copies:
            cp.start()
        for cp in copies:
            cp.wait()
        for cp in local:
            cp.wait()

    hbm = pl.BlockSpec(memory_space=pl.ANY)
    out_shape = [jax.ShapeDtypeStruct(p.shape, p.dtype) for p in big + small]
    dma = pltpu.SemaphoreType.DMA
    res = pl.pallas_call(
        body, name="exchange_pairs", out_shape=out_shape,
        in_specs=[hbm] * (nb + ns), out_specs=[hbm] * len(out_shape),
        scratch_shapes=[dma((nb,)), dma((nb,)), dma((ns, 7)), dma((ns, 7)), dma((ns,))],
        compiler_params=_params(),
    )(*big, *small)
    return res[:nb], res[nb:]


def _pair_sum(name, mine, theirs, row_tile):
    _, rows, cols = mine.shape

    def body(a_ref, b_ref, o_ref):
        o_ref[...] = (a_ref[...].astype(F32) + b_ref[...].astype(F32)).astype(o_ref.dtype)

    spec = pl.BlockSpec((N_CHIP, row_tile, cols), lambda i: (0, i, 0))
    return pl.pallas_call(
        body, name=name, grid=(rows // row_tile,), in_specs=[spec, spec], out_specs=spec,
        out_shape=jax.ShapeDtypeStruct(mine.shape, mine.dtype),
        compiler_params=_params(dimension_semantics=("arbitrary",)),
    )(mine, theirs)


def _exchange_chips(parts):
    n = len(parts)

    def body(*refs):
        in_refs, out_refs = refs[:n], refs[n:2 * n]
        send_sems, recv_sems, local_sems = refs[2 * n:]
        x, y, c = _mesh_pos()
        my_chip = 2 * x + y
        own = [pltpu.make_async_copy(in_refs[a].at[my_chip], out_refs[a].at[my_chip], local_sems.at[a]) for a in range(n)]
        for cp in own:
            cp.start()
        copies = []
        for k in range(1, N_CHIP):
            px, py = x ^ (k >> 1), y ^ (k & 1)
            copies += [pltpu.make_async_remote_copy(
                src_ref=in_refs[a].at[2 * px + py], dst_ref=out_refs[a].at[my_chip], send_sem=send_sems.at[a, k - 1],
                recv_sem=recv_sems.at[a, k - 1], device_id=(px, py, c), device_id_type=MESH) for a in range(n)]
        for cp in copies:
            cp.start()
        for cp in copies:
            cp.wait()
        for cp in own:
            cp.wait()

    hbm = pl.BlockSpec(memory_space=pl.ANY)
    dma = pltpu.SemaphoreType.DMA
    return pl.pallas_call(
        body, name="exchange_chips",
        out_shape=[jax.ShapeDtypeStruct(p.shape, p.dtype) for p in parts],
        in_specs=[hbm] * n, out_specs=[hbm] * n,
        scratch_shapes=[dma((n, N_CHIP - 1)), dma((n, N_CHIP - 1)), dma((n,))],
        compiler_params=_params(),
    )(*parts)


def _adamw(w, g, m, v):
    m = ADAM_B1 * m + (1.0 - ADAM_B1) * g
    v = ADAM_B2 * v + (1.0 - ADAM_B2) * (g * g)
    m_hat = m / (1.0 - ADAM_B1 ** ADAM_STEP)
    v_hat = v / (1.0 - ADAM_B2 ** ADAM_STEP)
    return -ADAM_LR * (m_hat / (jnp.sqrt(v_hat) + ADAM_EPS) + ADAM_WD * w), m, v


def _reduce_adamw(name, parts, w, m, v, row_tile):
    n_parts, rows, cols = parts.shape

    def body(p_ref, w_ref, m_ref, v_ref, g_ref, d_ref, nm_ref, nv_ref):
        g = p_ref[0].astype(F32)
        for s in range(1, n_parts):
            g = g + p_ref[s].astype(F32)
        g_ref[...] = g
        d_ref[...], nm_ref[...], nv_ref[...] = _adamw(w_ref[...], g, m_ref[...], v_ref[...])

    tile = pl.BlockSpec((row_tile, cols), lambda i: (i, 0))
    return pl.pallas_call(
        body, name=name, grid=(rows // row_tile,),
        in_specs=[pl.BlockSpec((n_parts, row_tile, cols), lambda i: (0, i, 0)), tile, tile, tile],
        out_specs=[tile] * 4,
        out_shape=[jax.ShapeDtypeStruct((rows, cols), F32)] * 4,
        compiler_params=_params(dimension_semantics=("arbitrary",)),
    )(parts, w, m, v)


_SMALL = (("norm_g", 1024), ("ret_gn_g", 512), ("rwkv_mu", 2176), ("w0", 512), ("a0", 512), ("k_k", 512), ("k_a", 512),
          ("r_k", 512), ("rwkv_gn_g", 512), ("rwkv_gn_b", 512), ("final_norm_g", 1024))
_SMALL_LANES = sum(n for _, n in _SMALL) + 128
_WEIGHTS = ("norm_g", "w_in", "ret_gn_g", "rwkv_mu", "w_lora_up", "w0", "a_lora_up", "a0", "k_k", "k_a", "r_k", "rwkv_gn_g",
            "rwkv_gn_b", "w_out", "final_norm_g")


def _adamw_vectors(parts, wts, mom, var):
    k = len(_SMALL)

    def body(p_ref, *refs):
        w_refs, m_refs, v_refs, outs = refs[:k], refs[k:2 * k], refs[2 * k:3 * k], refs[3 * k:]
        g_all = p_ref[0]
        for s in range(1, N_DEV):
            g_all = g_all + p_ref[s]
        off = 0
        for i, (_, n) in enumerate(_SMALL):
            g = g_all[:, off:off + n]
            off += n
            outs[4 * i][...] = g
            outs[4 * i + 1][...], outs[4 * i + 2][...], outs[4 * i + 3][...] = _adamw(
                w_refs[i][...], g, m_refs[i][...], v_refs[i][...])
        outs[4 * k][...] = g_all[:, off:off + 128]

    vmem = pl.BlockSpec(memory_space=pltpu.VMEM)
    shapes = [jax.ShapeDtypeStruct((1, n), F32) for _, n in _SMALL for _ in range(4)] + [jax.ShapeDtypeStruct((1, 128), F32)]
    res = pl.pallas_call(
        body, name="adamw_vectors", out_shape=shapes,
        in_specs=[vmem] * (1 + 3 * k), out_specs=[vmem] * len(shapes), compiler_params=_params(),
    )(parts, *[wts[n] for n, _ in _SMALL], *[mom[n] for n, _ in _SMALL], *[var[n] for n, _ in _SMALL])
    return {n: res[4 * i:4 * i + 4] for i, (n, _) in enumerate(_SMALL)}, res[4 * k]


def kernel(x, norm_g, w_in, ret_gn_g, rwkv_mu, w_lora_up, w0, a_lora_up, a0, k_k, k_a, r_k, rwkv_gn_g, rwkv_gn_b, w_out, final_norm_g, loss_target, m_norm_g, m_w_in, m_ret_gn_g, m_rwkv_mu, m_w_lora_up, m_w0, m_a_lora_up, m_a0, m_k_k, m_k_a, m_r_k, m_rwkv_gn_g, m_rwkv_gn_b, m_w_out, m_final_norm_g, v_norm_g, v_w_in, v_ret_gn_g, v_rwkv_mu, v_w_lora_up, v_w0, v_a_lora_up, v_a0, v_k_k, v_k_a, v_r_k, v_rwkv_gn_g, v_rwkv_gn_b, v_w_out, v_final_norm_g):
    wts = dict(norm_g=norm_g, w_in=w_in, ret_gn_g=ret_gn_g, rwkv_mu=rwkv_mu, w_lora_up=w_lora_up, w0=w0, a_lora_up=a_lora_up,
               a0=a0, k_k=k_k, k_a=k_a, r_k=r_k, rwkv_gn_g=rwkv_gn_g, rwkv_gn_b=rwkv_gn_b, w_out=w_out,
               final_norm_g=final_norm_g)
    mom = dict(norm_g=m_norm_g, w_in=m_w_in, ret_gn_g=m_ret_gn_g, rwkv_mu=m_rwkv_mu, w_lora_up=m_w_lora_up, w0=m_w0,
               a_lora_up=m_a_lora_up, a0=m_a0, k_k=m_k_k, k_a=m_k_a, r_k=m_r_k, rwkv_gn_g=m_rwkv_gn_g,
               rwkv_gn_b=m_rwkv_gn_b, w_out=m_w_out, final_norm_g=m_final_norm_g)
    var = dict(norm_g=v_norm_g, w_in=v_w_in, ret_gn_g=v_ret_gn_g, rwkv_mu=v_rwkv_mu, w_lora_up=v_w_lora_up, w0=v_w0,
               a_lora_up=v_a_lora_up, a0=v_a0, k_k=v_k_k, k_a=v_k_a, r_k=v_r_k, rwkv_gn_g=v_rwkv_gn_g,
               rwkv_gn_b=v_rwkv_gn_b, w_out=v_w_out, final_norm_g=v_final_norm_g)
    shapes = {n: wts[n].shape for n in _WEIGHTS}

    g_in, g_out, g_wup, g_aup = _all_gather(
        [w_in[0].astype(BF16), w_out[0].astype(BF16), w_lora_up[0], a_lora_up[0]])
    w_in_bf = g_in.reshape(N_DEV, D_MODEL, SHARD_IN).transpose(1, 0, 2).reshape(D_MODEL, IN_COLS)
    wup_full = g_wup.reshape(N_DEV, LORA, SHARD_LORA).transpose(1, 0, 2).reshape(LORA, W)
    aup_full = g_aup.reshape(N_DEV, LORA, SHARD_LORA).transpose(1, 0, 2).reshape(LORA, W)

    loss, dx, g = _local_grads(
        x[0], loss_target[0], norm_g, w_in_bf, ret_gn_g, rwkv_mu, wup_full, w0, aup_full, a0, k_k, k_a,
        r_k.reshape(1, W), rwkv_gn_g, rwkv_gn_b, g_out, final_norm_g.reshape(1, D_MODEL))

    small = jnp.concatenate([g[n] for n, _ in _SMALL] + [loss], axis=1)
    core = lax.axis_index("c")
    by_core = lambda t: [lax.dynamic_index_in_dim(t, i, axis=1, keepdims=False) for i in (core, 1 - core)]
    in_mine, in_sib = by_core(g["w_in"].reshape(D_MODEL, N_CHIP, 2, SHARD_IN).transpose(1, 2, 0, 3).astype(BF16))
    out_mine, out_sib = by_core(g["w_out"].reshape(N_CHIP, 2, SHARD_OUT, D_MODEL).astype(BF16))
    (in_theirs, out_theirs), parts = _exchange_pairs(
        [in_sib, out_sib],
        [g["w_lora_up"].reshape(LORA, N_DEV, SHARD_LORA).transpose(1, 0, 2),
         g["a_lora_up"].reshape(LORA, N_DEV, SHARD_LORA).transpose(1, 0, 2),
         jnp.broadcast_to(small[None], (N_DEV, 1, _SMALL_LANES))])
    by_chip = _exchange_chips([_pair_sum("pair_sum_w_in", in_mine, in_theirs, 256),
                               _pair_sum("pair_sum_w_out", out_mine, out_theirs, SHARD_OUT)])
    res = {}
    res["w_in"] = _reduce_adamw("adamw_w_in", by_chip[0], w_in[0], m_w_in[0], v_w_in[0], 256)
    res["w_out"] = _reduce_adamw("adamw_w_out", by_chip[1], w_out[0], m_w_out[0], v_w_out[0], SHARD_OUT)
    res["w_lora_up"] = _reduce_adamw("adamw_w_lora_up", parts[0], w_lora_up[0], m_w_lora_up[0], v_w_lora_up[0], LORA)
    res["a_lora_up"] = _reduce_adamw("adamw_a_lora_up", parts[1], a_lora_up[0], m_a_lora_up[0], v_a_lora_up[0], LORA)
    as_row = lambda d: {n: d[n].reshape(1, size) for n, size in _SMALL}
    vec, loss_row = _adamw_vectors(parts[2], as_row(wts), as_row(mom), as_row(var))
    res.update(vec)
    res = {n: [t.reshape(shapes[n]) for t in res[n]] for n in _WEIGHTS}
    return (loss_row[0, 0], dx[None], *[res[n][0] for n in _WEIGHTS], *[res[n][1] for n in _WEIGHTS],
            *[res[n][2] for n in _WEIGHTS], *[res[n][3] for n in _WEIGHTS])
```

```python
import functools

import numpy as np
import jax
import jax.numpy as jnp
from jax import lax
from jax.experimental import pallas as pl
from jax.experimental.pallas import tpu as pltpu

F32 = jnp.float32
BF16 = jnp.bfloat16

D_MODEL = 1024
CHUNK = 64
RET_HEADS = 4
RET_DV = 128
RET_DK = 64
RET_QK = 256
RET_WIDTH = 512
RWKV_WIDTH = 512
RWKV_HEAD = 64
RWKV_HEADS = 8
LORA = 64
RET_COLS = 2 * RET_QK + 2 * RET_WIDTH
RWKV_COLS = 4 * RWKV_WIDTH + 2 * LORA
IN_COLS = RET_COLS + RWKV_COLS
ROPE_BASE = 10000.0
RMS_EPS = 1e-6
RET_GN_EPS = 1e-5
RWKV_GN_EPS = 64e-5
ADAM_LR = 0.001
ADAM_B1 = 0.9
ADAM_B2 = 0.999
ADAM_EPS = 1e-08
ADAM_WD = 0.01
ADAM_STEP = 10
N_DEV = 8
SHARD_IN = IN_COLS // N_DEV
SHARD_OUT = D_MODEL // N_DEV
SHARD_LORA = RWKV_WIDTH // N_DEV
VMEM_LIMIT = 56 * 1024 * 1024
TOK_TILE = 256
WKV_CHUNK = 32

MESH = pl.DeviceIdType.MESH


def _dot_bf(a, b):
    return jnp.dot(a.astype(BF16), b.astype(BF16), preferred_element_type=F32)


def _dot_nt_bf(a, b):
    return lax.dot_general(a.astype(BF16), b.astype(BF16), (((1,), (1,)), ((), ())), preferred_element_type=F32)


def _dot_tn_bf(a, b):
    return lax.dot_general(a.astype(BF16), b.astype(BF16), (((0,), (0,)), ((), ())), preferred_element_type=F32)


@jax.custom_vjp
def _mm(a, b):
    return _dot_bf(a, b)


@jax.custom_vjp
def _mm_nt(a, b):
    return _dot_nt_bf(a, b)


@jax.custom_vjp
def _mm_tn(a, b):
    return _dot_tn_bf(a, b)


_mm.defvjp(lambda a, b: (_dot_bf(a, b), (a, b)), lambda res, g: (_dot_nt_bf(g, res[1]), _dot_tn_bf(res[0], g)))
_mm_nt.defvjp(lambda a, b: (_dot_nt_bf(a, b), (a, b)), lambda res, g: (_dot_bf(g, res[1]), _dot_tn_bf(g, res[0])))
_mm_tn.defvjp(lambda a, b: (_dot_tn_bf(a, b), (a, b)), lambda res, g: (_dot_nt_bf(res[1], g), _dot_bf(res[0], g)))


def _trunc(x):
    return lax.bitcast_convert_type(lax.bitcast_convert_type(x, jnp.uint32) & jnp.uint32(0xFFFF0000), F32)


def _two_piece(x):
    hi = _trunc(x)
    return jnp.concatenate([hi, x - hi], axis=1)


def _mix_raw(x, mat2):
    return _unstack(jnp.dot(_two_piece(_stack(x)), mat2, preferred_element_type=F32))


@jax.custom_vjp
def _head_mix(x, mat2):
    return _mix_raw(x, mat2)


_head_mix.defvjp(lambda x, mat2: (_mix_raw(x, mat2), mat2), lambda mat2, g: (_mix_raw(g, mat2), jnp.zeros_like(mat2)))


def _swap_halves(x):
    lane = lax.broadcasted_iota(jnp.int32, x.shape, 1)
    return jnp.where((lane & (RET_DK - 1)) < RET_DK // 2, pltpu.roll(x, RET_QK - RET_DK // 2, axis=1),
                     pltpu.roll(x, RET_DK // 2, axis=1))


@jax.custom_vjp
def _rot(x):
    return _swap_halves(x)


_rot.defvjp(lambda x: (_swap_halves(x), None), lambda _, g: (_swap_halves(g),))


def _params(**kw):
    return pltpu.CompilerParams(vmem_limit_bytes=VMEM_LIMIT, **kw)


def _full(shape):
    nd = len(shape)
    return pl.BlockSpec(shape, lambda i, _nd=nd: (0,) * _nd)


def _rows(tile, width):
    return pl.BlockSpec((tile, width), lambda i: (i, 0))


def _block_mix(n, blk, scale=1.0):
    idx = np.arange(n) // blk
    m = (idx[:, None] == idx[None, :]).astype(np.float32) * scale
    return jnp.asarray(np.concatenate([m, m], axis=0))


def _rope_tables(T):
    half = RET_DK // 2
    expo = -np.arange(half, dtype=np.float32) / np.float32(half)
    freqs = np.exp(expo * np.float32(np.log(ROPE_BASE))).astype(np.float32)
    ang = np.arange(T, dtype=np.float32)[:, None] * freqs[None, :]
    cos, sin = np.cos(ang).astype(np.float32), np.sin(ang).astype(np.float32)
    cos_h = np.concatenate([cos, cos], axis=1)
    sin_h = np.concatenate([-sin, sin], axis=1)
    cos_t = np.tile(cos_h, (1, RET_HEADS))
    sin_t = np.tile(sin_h, (1, RET_HEADS))
    return jnp.asarray(cos_t), jnp.asarray(sin_t)


def _ret_tables():
    h = np.arange(RET_HEADS, dtype=np.float32)
    lg = np.log(1.0 - np.exp2(-5.0 - h)).astype(np.float32)
    idx = np.arange(CHUNK, dtype=np.float32)
    intra = np.exp(lg[:, None, None] * np.abs(idx[:, None] - idx[None, :])).astype(np.float32)
    q_dec = np.exp(lg[:, None] * (idx[None, :] + 1.0)).astype(np.float32)
    k_dec = np.exp(lg[:, None] * (CHUNK - 1.0 - idx[None, :])).astype(np.float32)
    chunk_dec = np.exp(lg * CHUNK).astype(np.float32)
    lane_head = np.arange(RET_QK) // RET_DK
    mask = (lane_head[None, :] == np.arange(RET_HEADS)[:, None]).astype(np.float32)
    m = np.broadcast_to(mask[:, None, :], (RET_HEADS, CHUNK, RET_QK)).copy()
    qd = m * q_dec[:, :, None]
    kd = m * k_dec[:, :, None]
    return jnp.asarray(intra), jnp.asarray(m), jnp.asarray(qd), jnp.asarray(kd), [float(c) for c in chunk_dec]


def _rmsnorm(x, g):
    return x * lax.rsqrt(jnp.mean(x * x, axis=-1, keepdims=True) + RMS_EPS) * g


def _inproj(x, norm_g, w_in):
    T = x.shape[0]

    def body(x_ref, g_ref, w_ref, pr_ref, pw_ref, u_ref):
        ub = _rmsnorm(x_ref[...], g_ref[...]).astype(BF16)
        u_ref[...] = ub
        pr_ref[...] = jnp.dot(ub, w_ref[:, :RET_COLS], preferred_element_type=F32)
        pw_ref[...] = jnp.dot(ub, w_ref[:, RET_COLS:], preferred_element_type=F32)

    return pl.pallas_call(
        body, name="inproj", grid=(T // TOK_TILE,),
        in_specs=[_rows(TOK_TILE, D_MODEL), _full((1, D_MODEL)), _full((D_MODEL, IN_COLS))],
        out_specs=[_rows(TOK_TILE, RET_COLS), _rows(TOK_TILE, RWKV_COLS), _rows(TOK_TILE, D_MODEL)],
        out_shape=[jax.ShapeDtypeStruct((T, RET_COLS), F32), jax.ShapeDtypeStruct((T, RWKV_COLS), F32),
                   jax.ShapeDtypeStruct((T, D_MODEL), BF16)],
        compiler_params=_params(dimension_semantics=("arbitrary",)),
    )(x, norm_g, w_in)


def _ret_chunk(pq, pk, v_heads, s_heads, cos_t, sin_t, dec, hm, qd, kd, chunk_dec):
    q = pq * cos_t + _rot(pq) * sin_t
    k = (pk * cos_t + _rot(pk) * sin_t) * (RET_DK ** -0.5)
    outs, s_out = [], []
    for h in range(RET_HEADS):
        sc = _mm_nt(q * hm[h], k * hm[h]) * dec[h]
        intra = _mm(sc, v_heads[h])
        kv = _mm_tn(k * kd[h], v_heads[h])
        inter = _mm(q * qd[h], s_heads[h])
        outs.append(intra + inter)
        s_out.append(s_heads[h] * chunk_dec[h] + kv)
    return tuple(outs), tuple(s_out)


def _ret_specs():
    const = [_full((RET_HEADS, CHUNK, CHUNK)), _full((RET_HEADS, CHUNK, RET_QK)),
             _full((RET_HEADS, CHUNK, RET_QK)), _full((RET_HEADS, CHUNK, RET_QK))]
    return const


def _ret_fwd(p_ret, tabs):
    T = p_ret.shape[0]
    nc = T // CHUNK
    cos_t, sin_t, dec, hm, qd, kd, chunk_dec = tabs

    def body(p_ref, cos_ref, sin_ref, dec_ref, hm_ref, qd_ref, kd_ref, out_ref, sin_save_ref, s_scr):
        @pl.when(pl.program_id(0) == 0)
        def _():
            s_scr[...] = jnp.zeros_like(s_scr)

        s_heads = tuple(s_scr[h] for h in range(RET_HEADS))
        for h in range(RET_HEADS):
            sin_save_ref[0, h] = s_heads[h]
        pq = p_ref[:, 0:RET_QK]
        pk = p_ref[:, RET_QK:2 * RET_QK]
        v_heads = tuple(p_ref[:, 2 * RET_QK + RET_DV * h:2 * RET_QK + RET_DV * (h + 1)] for h in range(RET_HEADS))
        outs, s_out = _ret_chunk(pq, pk, v_heads, s_heads, cos_ref[...], sin_ref[...], dec_ref[...],
                                 hm_ref[...], qd_ref[...], kd_ref[...], chunk_dec)
        for h in range(RET_HEADS):
            out_ref[:, RET_DV * h:RET_DV * (h + 1)] = outs[h]
            s_scr[h] = s_out[h]

    return pl.pallas_call(
        body, name="ret_fwd", grid=(nc,),
        in_specs=[pl.BlockSpec((CHUNK, RET_COLS), lambda i: (i, 0)), _rows(CHUNK, RET_QK), _rows(CHUNK, RET_QK)] + _ret_specs(),
        out_specs=[_rows(CHUNK, RET_WIDTH), pl.BlockSpec((1, RET_HEADS, RET_QK, RET_DV), lambda i: (i, 0, 0, 0))],
        out_shape=[jax.ShapeDtypeStruct((T, RET_WIDTH), F32), jax.ShapeDtypeStruct((nc, RET_HEADS, RET_QK, RET_DV), F32)],
        scratch_shapes=[pltpu.VMEM((RET_HEADS, RET_QK, RET_DV), F32)],
        compiler_params=_params(dimension_semantics=("arbitrary",)),
    )(p_ret, cos_t, sin_t, dec, hm, qd, kd)


def _ret_bwd(p_ret, s_saved, d_ret, tabs):
    T = p_ret.shape[0]
    nc = T // CHUNK
    cos_t, sin_t, dec, hm, qd, kd, chunk_dec = tabs

    def body(p_ref, s_ref, dret_ref, cos_ref, sin_ref, dec_ref, hm_ref, qd_ref, kd_ref, dp_ref, ds_scr):
        @pl.when(pl.program_id(0) == 0)
        def _():
            ds_scr[...] = jnp.zeros_like(ds_scr)

        pq = p_ref[:, 0:RET_QK]
        pk = p_ref[:, RET_QK:2 * RET_QK]
        v_heads = tuple(p_ref[:, 2 * RET_QK + RET_DV * h:2 * RET_QK + RET_DV * (h + 1)] for h in range(RET_HEADS))
        s_heads = tuple(s_ref[0, h] for h in range(RET_HEADS))
        consts = (cos_ref[...], sin_ref[...], dec_ref[...], hm_ref[...], qd_ref[...], kd_ref[...])
        _, vjp = jax.vjp(lambda a, b, c, d: _ret_chunk(a, b, c, d, *consts, chunk_dec), pq, pk, v_heads, s_heads)
        d_out = tuple(dret_ref[:, RET_DV * h:RET_DV * (h + 1)] for h in range(RET_HEADS))
        d_s = tuple(ds_scr[h] for h in range(RET_HEADS))
        dq, dk, dv, ds_in = vjp((d_out, d_s))
        dp_ref[:, 0:RET_QK] = dq
        dp_ref[:, RET_QK:2 * RET_QK] = dk
        for h in range(RET_HEADS):
            dp_ref[:, 2 * RET_QK + RET_DV * h:2 * RET_QK + RET_DV * (h + 1)] = dv[h]
            ds_scr[h] = ds_in[h]

    rev = lambda i: (nc - 1 - i, 0)
    return pl.pallas_call(
        body, name="ret_bwd", grid=(nc,),
        in_specs=[pl.BlockSpec((CHUNK, RET_COLS), rev),
                  pl.BlockSpec((1, RET_HEADS, RET_QK, RET_DV), lambda i: (nc - 1 - i, 0, 0, 0)),
                  pl.BlockSpec((CHUNK, RET_WIDTH), rev), pl.BlockSpec((CHUNK, RET_QK), rev), pl.BlockSpec((CHUNK, RET_QK), rev)]
        + _ret_specs(),
        out_specs=pl.BlockSpec((CHUNK, 2 * RET_QK + RET_WIDTH), rev),
        out_shape=jax.ShapeDtypeStruct((T, 2 * RET_QK + RET_WIDTH), F32),
        scratch_shapes=[pltpu.VMEM((RET_HEADS, RET_QK, RET_DV), F32)],
        compiler_params=_params(dimension_semantics=("arbitrary",)),
    )(p_ret, s_saved, d_ret, cos_t, sin_t, dec, hm, qd, kd)


def _wkv_consts():
    lane = lax.broadcasted_iota(jnp.int32, (RWKV_HEAD, RWKV_WIDTH), 1)
    sub = lax.broadcasted_iota(jnp.int32, (RWKV_HEAD, RWKV_WIDTH), 0)
    diag = ((lane & (RWKV_HEAD - 1)) == sub).astype(F32)
    r = lax.broadcasted_iota(jnp.int32, (3 * 128, 128), 0)
    c = lax.broadcasted_iota(jnp.int32, (3 * 128, 128), 1)
    ones = (((r & 127) >> 6) == (c >> 6)).astype(BF16)
    return diag, ones


def _stack(x):
    return jnp.concatenate([x[:, 128 * p:128 * (p + 1)] for p in range(4)], axis=0)


def _unstack(y):
    n = y.shape[0] // 4
    return jnp.concatenate([y[n * p:n * (p + 1)] for p in range(4)], axis=1)


def _split(x, n):
    pieces = []
    for _ in range(n):
        p = x.astype(BF16)
        pieces.append(p)
        x = x - p.astype(F32)
    return pieces


def _lane_sum(x, ones):
    return _unstack(jnp.dot(_two_piece(_stack(x)), ones[:256].astype(F32), preferred_element_type=F32))


def _lane_sum_bf(x, ones):
    return _unstack(jnp.dot(_stack(x).astype(BF16), ones[:128], preferred_element_type=F32))


def _colsum(x):
    return jnp.sum(x, axis=0, keepdims=True)


def _expand_cols(xt, t):
    lane = lax.broadcasted_iota(jnp.int32, (RWKV_HEAD, 128), 1)
    tiles = []
    for p in range(4):
        lo = jnp.broadcast_to(xt[128 * p:128 * p + RWKV_HEAD, t:t + 1], (RWKV_HEAD, 128))
        hi = jnp.broadcast_to(xt[128 * p + RWKV_HEAD:128 * (p + 1), t:t + 1], (RWKV_HEAD, 128))
        tiles.append(jnp.where(lane < RWKV_HEAD, lo, hi))
    return jnp.concatenate(tiles, axis=1)


def _head_sums(x, ones):
    return _unstack(jnp.dot(jnp.concatenate(_split(_stack(x), 3), axis=1), ones, preferred_element_type=F32))


def _wkv_fwd(r, w, k, v, kap, a):
    T = r.shape[0]
    C = WKV_CHUNK
    nc = T // C

    def body(r_ref, w_ref, k_ref, v_ref, kap_ref, a_ref, o_ref, s_all_ref, sa_all_ref, s_scr):
        @pl.when(pl.program_id(0) == 0)
        def _():
            s_scr[...] = jnp.zeros_like(s_scr)

        diag, ones = _wkv_consts()
        rr, ww, kk, vv, kap_, aa = (ref[...] for ref in (r_ref, w_ref, k_ref, v_ref, kap_ref, a_ref))
        bb = kap_ * aa
        c1 = _head_sums(pltpu.roll(bb, 1, axis=0) * kap_, ones)
        row = lambda x, t: x[t:t + 1]

        v_cols = vv.T

        s_prev = s_scr[...]
        sa = _lane_sum(s_prev * (-row(kap_, 0)), ones)
        ls, rows = None, []

        def emit_o(t, s_t):
            rows.append(_colsum(_lane_sum_bf(s_t * row(rr, t), ones) * diag))
            if t % 8 == 7:
                o_ref[t - 7:t + 1, :] = jnp.concatenate(rows, axis=0)
                rows.clear()

        for t in range(C):
            u = s_prev * row(ww, t) + _expand_cols(v_cols, t) * row(kk, t)
            if t > 0:
                sa = ls - sa * row(c1, t)
            if t + 1 < C:
                ls = _lane_sum(u * (-row(kap_, t + 1)), ones)
            if t > 0:
                emit_o(t - 1, s_prev)
            s_prev = u + sa * row(bb, t)
            s_all_ref[t] = s_prev
            sa_all_ref[t] = sa.astype(BF16)
        emit_o(C - 1, s_prev)
        s_scr[...] = s_prev

    spec = _rows(C, RWKV_WIDTH)
    return pl.pallas_call(
        body, name="wkv_fwd", grid=(nc,),
        in_specs=[spec] * 6,
        out_specs=[spec, pl.BlockSpec((C, RWKV_HEAD, RWKV_WIDTH), lambda i: (i, 0, 0)),
                   pl.BlockSpec((C, RWKV_HEAD, RWKV_WIDTH), lambda i: (i, 0, 0))],
        out_shape=[jax.ShapeDtypeStruct((T, RWKV_WIDTH), F32), jax.ShapeDtypeStruct((T, RWKV_HEAD, RWKV_WIDTH), F32),
                   jax.ShapeDtypeStruct((T, RWKV_HEAD, RWKV_WIDTH), BF16)],
        scratch_shapes=[pltpu.VMEM((RWKV_HEAD, RWKV_WIDTH), F32)],
        compiler_params=_params(dimension_semantics=("arbitrary",)),
    )(r, w, k, v, kap, a)


def _wkv_bwd(r, w, k, v, kap, a, s_all, sa_all, d_o):
    T = r.shape[0]
    C = WKV_CHUNK
    nc = T // C

    def body(r_ref, w_ref, k_ref, v_ref, kap_ref, a_ref, s_ref, s_before_ref, sa_ref, do_ref,
             dr_ref, dw_ref, dk_ref, dv_ref, dkap_ref, da_ref, ds_scr):
        first_chunk = pl.program_id(0) == nc - 1

        @pl.when(pl.program_id(0) == 0)
        def _():
            ds_scr[...] = jnp.zeros_like(ds_scr)

        diag, ones = _wkv_consts()
        rr, ww, kk, vv, kap_, aa, dd = (ref[...] for ref in (r_ref, w_ref, k_ref, v_ref, kap_ref, a_ref, do_ref))
        bb = kap_ * aa
        e1 = _head_sums(pltpu.roll(kap_, C - 1, axis=0) * bb, ones)
        row = lambda x, t: x[t:t + 1]

        def state_before(t):
            return s_ref[t - 1] if t > 0 else jnp.where(first_chunk, 0.0, s_before_ref[0])

        v_cols, do_cols = vv.T, dd.T

        d_sn, dsa, rows = None, None, [None] * C

        def emit_rows(t, d_sn_t, dsa_t):
            s_prev, dof = state_before(t), _expand_cols(do_cols, t)
            dv = _colsum(_lane_sum_bf(d_sn_t * row(kk, t), ones) * diag)
            db = _colsum(d_sn_t * sa_ref[t].astype(F32))
            rows[t] = (_colsum(s_ref[t] * dof), _colsum(d_sn_t * s_prev), _colsum(d_sn_t * _expand_cols(v_cols, t)), dv,
                       db * row(aa, t) - _colsum(dsa_t * s_prev), db * row(kap_, t))
            if t % 8 == 0:
                for j, ref in enumerate((dr_ref, dw_ref, dk_ref, dv_ref, dkap_ref, da_ref)):
                    ref[t:t + 8, :] = jnp.concatenate([rows[u][j] for u in range(t, t + 8)], axis=0)

        for t in reversed(range(C)):
            dof = _expand_cols(do_cols, t)
            if t == C - 1:
                d_sn = ds_scr[...] + dof * row(rr, t)
                dsa = _lane_sum(d_sn * row(bb, t), ones)
            else:
                v_t = d_sn * row(ww, t + 1) + dof * row(rr, t)
                ls = _lane_sum(v_t * row(bb, t), ones)
                emit_rows(t + 1, d_sn, dsa)
                d_sn = v_t - dsa * row(kap_, t + 1)
                dsa = ls - dsa * row(e1, t)
        emit_rows(0, d_sn, dsa)
        d_s = d_sn * row(ww, 0) - dsa * row(kap_, 0)
        ds_scr[...] = d_s

    spec = pl.BlockSpec((C, RWKV_WIDTH), lambda i: (nc - 1 - i, 0))
    states = pl.BlockSpec((C, RWKV_HEAD, RWKV_WIDTH), lambda i: (nc - 1 - i, 0, 0))
    before = pl.BlockSpec((1, RWKV_HEAD, RWKV_WIDTH), lambda i: (jnp.maximum((nc - 1 - i) * C - 1, 0), 0, 0))
    return pl.pallas_call(
        body, name="wkv_bwd", grid=(nc,),
        in_specs=[spec] * 6 + [states, before, states, spec],
        out_specs=[spec] * 6,
        out_shape=[jax.ShapeDtypeStruct((T, RWKV_WIDTH), F32)] * 6,
        scratch_shapes=[pltpu.VMEM((RWKV_HEAD, RWKV_WIDTH), F32)],
        compiler_params=_params(dimension_semantics=("arbitrary",)),
    )(r, w, k, v, kap, a, s_all, s_all, sa_all, d_o)


W = RWKV_WIDTH


def _softplus(y):
    return jnp.maximum(y, 0.0) + jnp.log(1.0 + jnp.exp(-jnp.abs(y)))


def _prep_fn(kr, xwa, w0, a0, k_k, k_a, wup_pad, aup_pad, ones64):
    w_log = -_softplus(-(w0 + _mm(jnp.tanh(xwa), wup_pad))) - 0.5
    decay = jnp.exp(-jnp.exp(w_log))
    a = jax.nn.sigmoid(a0 + _mm(xwa, aup_pad))
    kk = kr * k_k
    kap = kk / jnp.maximum(jnp.sqrt(_head_mix(kk * kk, ones64)), 1e-12)
    k = kr * (1.0 + (a - 1.0) * k_a)
    return decay, k, kap, a


def _shift_down(p, first_row):
    rows = lax.broadcasted_iota(jnp.int32, p.shape, 0)
    return jnp.where(rows == 0, first_row, pltpu.roll(p, 1, axis=0))


def _shift_up(z, last_row):
    n = z.shape[0]
    rows = lax.broadcasted_iota(jnp.int32, z.shape, 0)
    return jnp.where(rows == n - 1, last_row, pltpu.roll(z, n - 1, axis=0))


def _prev_block_spec():
    return pl.BlockSpec((8, RWKV_COLS), lambda i: (jnp.maximum(i * (TOK_TILE // 8) - 1, 0), 0))


def _mixed(p_ref, prev8_ref, mu_ref, first_tile):
    p = p_ref[...]
    first_row = jnp.where(first_tile, 0.0, prev8_ref[7:8, :])
    prev = _shift_down(p, first_row)
    return p, prev, p + mu_ref[...] * (prev - p)


def _prep_fwd(p_rwkv, mu, w0, a0, k_k, k_a, wup_pad, aup_pad, ones64):
    T = p_rwkv.shape[0]

    def body(p_ref, prev8_ref, mu_ref, w0_ref, a0_ref, kk_ref, ka_ref, wup_ref, aup_ref, ones_ref,
             r_ref, w_ref, k_ref, v_ref, kap_ref, a_ref, g_ref):
        _, _, ps = _mixed(p_ref, prev8_ref, mu_ref, pl.program_id(0) == 0)
        decay, k, kap, a = _prep_fn(ps[:, W:2 * W], ps[:, 4 * W:], w0_ref[...], a0_ref[...], kk_ref[...], ka_ref[...],
                                    wup_ref[...], aup_ref[...], ones_ref[...])
        r_ref[...] = ps[:, 0:W]
        w_ref[...] = decay
        k_ref[...] = k
        v_ref[...] = ps[:, 2 * W:3 * W]
        kap_ref[...] = kap
        a_ref[...] = a
        g_ref[...] = ps[:, 3 * W:4 * W]

    vec = _full((1, W))
    return pl.pallas_call(
        body, name="prep_fwd", grid=(T // TOK_TILE,),
        in_specs=[_rows(TOK_TILE, RWKV_COLS), _prev_block_spec(), _full((1, RWKV_COLS)), vec, vec, vec, vec,
                  _full((2 * LORA, W)), _full((2 * LORA, W)), _full((256, 128))],
        out_specs=[_rows(TOK_TILE, W)] * 7,
        out_shape=[jax.ShapeDtypeStruct((T, W), F32)] * 7,
        compiler_params=_params(dimension_semantics=("arbitrary",)),
    )(p_rwkv, p_rwkv, mu, w0, a0, k_k, k_a, wup_pad, aup_pad, ones64)


def _prep_bwd(p_rwkv, mu, w0, a0, k_k, k_a, wup_pad, aup_pad, ones64, dr, dw, dk, dv, dkap, da, dg, dr2, dk2, dv2):
    T = p_rwkv.shape[0]
    nt = T // TOK_TILE

    def body(p_ref, prev8_ref, mu_ref, w0_ref, a0_ref, kk_ref, ka_ref, wup_ref, aup_ref, ones_ref,
             dr_ref, dw_ref, dk_ref, dv_ref, dkap_ref, da_ref, dg_ref, dr2_ref, dk2_ref, dv2_ref,
             dp_ref, dmu_ref, dw0_ref, da0_ref, dkk_ref, dka_ref, dwup_ref, daup_ref, zrow_scr):
        i = pl.program_id(0)
        accs = (dmu_ref, dw0_ref, da0_ref, dkk_ref, dka_ref, dwup_ref, daup_ref)

        @pl.when(i == 0)
        def _():
            zrow_scr[...] = jnp.zeros_like(zrow_scr)
            for ref in accs:
                ref[...] = jnp.zeros_like(ref)

        p, prev, ps = _mixed(p_ref, prev8_ref, mu_ref, i == nt - 1)
        ones = ones_ref[...]
        _, vjp = jax.vjp(lambda *args: _prep_fn(*args, ones), ps[:, W:2 * W], ps[:, 4 * W:], w0_ref[...], a0_ref[...],
                         kk_ref[...], ka_ref[...], wup_ref[...], aup_ref[...])
        dkr, dxwa, dw0, da0, dkk, dka, dwup, daup = vjp(
            (dw_ref[...], dk_ref[...] + dk2_ref[...], dkap_ref[...], da_ref[...]))
        dps = jnp.concatenate([dr_ref[...] + dr2_ref[...], dkr, dv_ref[...] + dv2_ref[...], dg_ref[...], dxwa], axis=1)
        z = dps * mu_ref[...]
        dp_ref[...] = dps - z + _shift_up(z, zrow_scr[0:1, :])
        zrow_scr[0:1, :] = z[0:1, :]
        for ref, val in zip(accs, (_colsum(dps * (prev - p)), dw0, da0, dkk, dka, dwup, daup)):
            ref[...] += val

    rev = lambda i: (nt - 1 - i, 0)
    vec = _full((1, W))
    lora = _full((2 * LORA, W))
    tile = pl.BlockSpec((TOK_TILE, W), rev)
    prev8 = pl.BlockSpec((8, RWKV_COLS), lambda i: (jnp.maximum((nt - 1 - i) * (TOK_TILE // 8) - 1, 0), 0))
    return pl.pallas_call(
        body, name="prep_bwd", grid=(nt,),
        in_specs=[pl.BlockSpec((TOK_TILE, RWKV_COLS), rev), prev8, _full((1, RWKV_COLS)), vec, vec, vec, vec, lora, lora,
                  _full((256, 128))] + [tile] * 10,
        out_specs=[pl.BlockSpec((TOK_TILE, RWKV_COLS), rev), _full((1, RWKV_COLS)), vec, vec, vec, vec, lora, lora],
        out_shape=[jax.ShapeDtypeStruct((T, RWKV_COLS), F32), jax.ShapeDtypeStruct((1, RWKV_COLS), F32)]
        + [jax.ShapeDtypeStruct((1, W), F32)] * 4 + [jax.ShapeDtypeStruct((2 * LORA, W), F32)] * 2,
        scratch_shapes=[pltpu.VMEM((8, RWKV_COLS), F32)],
        compiler_params=_params(dimension_semantics=("arbitrary",)),
    )(p_rwkv, p_rwkv, mu, w0, a0, k_k, k_a, wup_pad, aup_pad, ones64, dr, dw, dk, dv, dkap, da, dg, dr2, dk2, dv2)


def _silu(x):
    return x * jax.nn.sigmoid(x)


def _post_y(o, r, k, v, g_rw, ret_raw, g_ret, ret_gn_g, gn_g, gn_b, r_k, avg128, avg64, ones64):
    xc = ret_raw - _head_mix(ret_raw, avg128)
    ret = xc * lax.rsqrt(_head_mix(xc * xc, avg128) + RET_GN_EPS)
    y_ret = _silu(g_ret) * (ret * ret_gn_g)
    oc = o - _head_mix(o, avg64)
    on = oc * lax.rsqrt(_head_mix(oc * oc, avg64) + RWKV_GN_EPS) * gn_g + gn_b
    bonus = _head_mix(r * k * r_k, ones64) * v
    y_rwkv = _silu(g_rw) * (on + bonus)
    return y_ret, y_rwkv


def _post_loss(h, final_g, target):
    err = _rmsnorm(h, final_g) - target
    return 0.5 * jnp.sum(jnp.mean(err * err, axis=-1))


def _post(o, r, k, v, g_rw, ret_raw, p_ret, x, target, ret_gn_g, gn_g, gn_b, r_k, final_g, w_out, avg128, avg64, ones64):
    T = x.shape[0]
    n_tok_out = 8

    def body(o_ref, r_ref, k_ref, v_ref, grw_ref, ret_ref, gret_ref, x_ref, tgt_ref, rg_ref, gg_ref, gb_ref, rk_ref, fg_ref,
             wo_ref, a128_ref, a64_ref, ones_ref, *outs):
        tok_outs, (dwo_ref, drg_ref, dgg_ref, dgb_ref, drk_ref, dfg_ref, loss_ref) = outs[:n_tok_out], outs[n_tok_out:]
        accs = (dwo_ref, drg_ref, dgg_ref, dgb_ref, drk_ref, dfg_ref, loss_ref)

        @pl.when(pl.program_id(0) == 0)
        def _():
            for ref in accs:
                ref[...] = jnp.zeros_like(ref)

        consts = (a128_ref[...], a64_ref[...], ones_ref[...])
        (y_ret, y_rwkv), vjp = jax.vjp(
            lambda *args: _post_y(*args, *consts), o_ref[...], r_ref[...], k_ref[...], v_ref[...], grw_ref[...], ret_ref[...],
            gret_ref[...], rg_ref[...], gg_ref[...], gb_ref[...], rk_ref[...])
        h = x_ref[...] + _dot_bf(y_ret, wo_ref[0:RET_WIDTH, :]) + _dot_bf(y_rwkv, wo_ref[RET_WIDTH:, :])
        loss, (dh, dfg) = jax.value_and_grad(_post_loss, argnums=(0, 1))(h, fg_ref[...], tgt_ref[...])
        dy_ret = _dot_nt_bf(dh, wo_ref[0:RET_WIDTH, :])
        dy_rwkv = _dot_nt_bf(dh, wo_ref[RET_WIDTH:, :])
        do, dr, dk, dv, dgrw, dret, dgret, drg, dgg, dgb, drk = vjp((dy_ret, dy_rwkv))
        for ref, val in zip(tok_outs, (dh, do, dr, dk, dv, dgrw, dret, dgret)):
            ref[...] = val
        dwo_ref[0:RET_WIDTH, :] += _dot_tn_bf(y_ret, dh)
        dwo_ref[RET_WIDTH:, :] += _dot_tn_bf(y_rwkv, dh)
        for ref, val in zip(accs[1:], (drg, dgg, dgb, drk, dfg, jnp.full((1, 128), loss, F32))):
            ref[...] += val

    tile = _rows(TOK_TILE, W)
    wide = _rows(TOK_TILE, D_MODEL)
    vec = _full((1, W))
    sq = _full((256, 128))
    return pl.pallas_call(
        body, name="post", grid=(T // TOK_TILE,),
        in_specs=[tile] * 6 + [pl.BlockSpec((TOK_TILE, W), lambda i: (i, 2)), wide, wide, vec, vec, vec, vec,
                               _full((1, D_MODEL)), _full((D_MODEL, D_MODEL)), sq, sq, sq],
        out_specs=[wide] + [tile] * 7 + [_full((D_MODEL, D_MODEL)), vec, vec, vec, vec, _full((1, D_MODEL)), _full((1, 128))],
        out_shape=[jax.ShapeDtypeStruct((T, D_MODEL), F32)] + [jax.ShapeDtypeStruct((T, W), F32)] * 7
        + [jax.ShapeDtypeStruct((D_MODEL, D_MODEL), F32)] + [jax.ShapeDtypeStruct((1, W), F32)] * 4
        + [jax.ShapeDtypeStruct((1, D_MODEL), F32), jax.ShapeDtypeStruct((1, 128), F32)],
        compiler_params=_params(dimension_semantics=("arbitrary",)),
    )(o, r, k, v, g_rw, ret_raw, p_ret, x, target, ret_gn_g, gn_g, gn_b, r_k, final_g, w_out, avg128, avg64, ones64)


def _inproj_bwd_x(x, norm_g, dp_qkv, dg_ret, dp_rwkv, dh, w_in):
    T = x.shape[0]
    n_qkv = 2 * RET_QK + RET_WIDTH

    def body(x_ref, g_ref, dqkv_ref, dgret_ref, drwkv_ref, dh_ref, w_ref, dx_ref, dg_ref):
        @pl.when(pl.program_id(0) == 0)
        def _():
            dg_ref[...] = jnp.zeros_like(dg_ref)

        _, vjp = jax.vjp(_rmsnorm, x_ref[...], g_ref[...])
        du = (_dot_nt_bf(dqkv_ref[...], w_ref[:, 0:n_qkv]) + _dot_nt_bf(dgret_ref[...], w_ref[:, n_qkv:RET_COLS])
              + _dot_nt_bf(drwkv_ref[...], w_ref[:, RET_COLS:]))
        dx, dg = vjp(du)
        dx_ref[...] = dx + dh_ref[...]
        dg_ref[...] += dg

    return pl.pallas_call(
        body, name="inproj_bwd_x", grid=(T // TOK_TILE,),
        in_specs=[_rows(TOK_TILE, D_MODEL), _full((1, D_MODEL)), _rows(TOK_TILE, n_qkv), _rows(TOK_TILE, RET_WIDTH),
                  _rows(TOK_TILE, RWKV_COLS), _rows(TOK_TILE, D_MODEL), _full((D_MODEL, IN_COLS))],
        out_specs=[_rows(TOK_TILE, D_MODEL), _full((1, D_MODEL))],
        out_shape=[jax.ShapeDtypeStruct((T, D_MODEL), F32), jax.ShapeDtypeStruct((1, D_MODEL), F32)],
        compiler_params=_params(dimension_semantics=("arbitrary",)),
    )(x, norm_g, dp_qkv, dg_ret, dp_rwkv, dh, w_in)


def _grad_w(name, u, dp):
    T, n = dp.shape
    tile = 2 * TOK_TILE

    def body(u_ref, dp_ref, out_ref):
        @pl.when(pl.program_id(0) == 0)
        def _():
            out_ref[...] = jnp.zeros_like(out_ref)

        out_ref[...] += _dot_tn_bf(u_ref[...], dp_ref[...])

    return pl.pallas_call(
        body, name=name, grid=(T // tile,),
        in_specs=[_rows(tile, D_MODEL), _rows(tile, n)],
        out_specs=_full((D_MODEL, n)),
        out_shape=jax.ShapeDtypeStruct((D_MODEL, n), F32),
        compiler_params=_params(dimension_semantics=("arbitrary",)),
    )(u, dp)


def _pad_lora(w_up, top):
    z = jnp.zeros_like(w_up)
    return jnp.concatenate([w_up, z] if top else [z, w_up], axis=0)


def _local_grads(x, target, norm_g, w_in_bf, ret_gn_g, mu, w_lora_up, w0, a_lora_up, a0, k_k, k_a, r_k, gn_g, gn_b,
                 w_out_bf, final_g):
    T = x.shape[0]
    tabs = _rope_tables(T) + _ret_tables()
    ones64 = _block_mix(128, RWKV_HEAD)
    avg64 = _block_mix(128, RWKV_HEAD, 1.0 / RWKV_HEAD)
    avg128 = _block_mix(128, RET_DV, 1.0 / RET_DV)
    wup_pad, aup_pad = _pad_lora(w_lora_up, True), _pad_lora(a_lora_up, False)

    p_ret, p_rwkv, u = _inproj(x, norm_g, w_in_bf)
    ret_raw, s_saved = _ret_fwd(p_ret, tabs)
    r, w, k, v, kap, a, g_rw = _prep_fwd(p_rwkv, mu, w0, a0, k_k, k_a, wup_pad, aup_pad, ones64)
    o, s_all, sa_all = _wkv_fwd(r, w, k, v, kap, a)
    (dh, do, dr2, dk2, dv2, dgrw, dret, dgret, d_w_out, d_ret_gn_g, d_gn_g, d_gn_b, d_r_k, d_final_g, loss) = _post(
        o, r, k, v, g_rw, ret_raw, p_ret, x, target, ret_gn_g, gn_g, gn_b, r_k, final_g, w_out_bf, avg128, avg64, ones64)
    dr, dw, dk, dv, dkap, da = _wkv_bwd(r, w, k, v, kap, a, s_all, sa_all, do)
    dp_rwkv, d_mu, d_w0, d_a0, d_k_k, d_k_a, d_wup, d_aup = _prep_bwd(
        p_rwkv, mu, w0, a0, k_k, k_a, wup_pad, aup_pad, ones64, dr, dw, dk, dv, dkap, da, dgrw, dr2, dk2, dv2)
    dp_qkv = _ret_bwd(p_ret, s_saved, dret, tabs)
    dx, d_norm_g = _inproj_bwd_x(x, norm_g, dp_qkv, dgret, dp_rwkv, dh, w_in_bf)
    d_w_in = jnp.concatenate([_grad_w("grad_w_qkv", u, dp_qkv), _grad_w("grad_w_gret", u, dgret),
                              _grad_w("grad_w_rwkv", u, dp_rwkv)], axis=1)
    grads = dict(norm_g=d_norm_g, w_in=d_w_in, ret_gn_g=d_ret_gn_g, rwkv_mu=d_mu, w_lora_up=d_wup[:LORA], w0=d_w0,
                 a_lora_up=d_aup[LORA:], a0=d_a0, k_k=d_k_k, k_a=d_k_a, r_k=d_r_k, rwkv_gn_g=d_gn_g, rwkv_gn_b=d_gn_b,
                 w_out=d_w_out, final_norm_g=d_final_g)
    return loss, dx, grads


def _mesh_pos():
    return lax.axis_index("x"), lax.axis_index("y"), lax.axis_index("c")


def _all_gather(shards):
    n = len(shards)

    def body(*refs):
        x_refs, out_refs = refs[:n], refs[n:2 * n]
        send_sems, recv_sems, local_sems = refs[2 * n:]
        x, y, c = _mesh_pos()
        me, sibling = (x, y, c), (x, y, 1 - c)
        chips = [(1 - x, y), (x, 1 - y), (1 - x, 1 - y)]

        def rows(a, pos):
            m = x_refs[a].shape[0]
            return out_refs[a].at[pl.ds((4 * pos[0] + 2 * pos[1] + pos[2]) * m, m), :]

        def copy(a, k, block, to, src=None):
            return pltpu.make_async_remote_copy(
                src_ref=rows(a, block) if src is None else src, dst_ref=rows(a, block),
                send_sem=send_sems.at[a, k], recv_sem=recv_sems.at[a, k], device_id=to, device_id_type=MESH)

        mine = [pltpu.make_async_copy(x_refs[a], rows(a, me), local_sems.at[a]) for a in range(n)]
        for cp in mine:
            cp.start()
        first = []
        for a in range(n):
            first.append(copy(a, 0, me, sibling, src=x_refs[a]))
            first += [copy(a, 1 + j, me, (*chip, c), src=x_refs[a]) for j, chip in enumerate(chips)]
        for cp in first:
            cp.start()
        passed = []
        for j, chip in enumerate(chips):
            for a in range(n):
                copy(a, 1 + j, (*chip, c), me).wait_recv()
                passed.append(copy(a, 4 + j, (*chip, c), sibling))
                passed[-1].start()
        for a in range(n):
            copy(a, 0, sibling, me).wait_recv()
            for j, chip in enumerate(chips):
                copy(a, 4 + j, (*chip, 1 - c), me).wait_recv()
        for cp in first + passed:
            cp.wait_send()
        for cp in mine:
            cp.wait()

    vmem = pl.BlockSpec(memory_space=pltpu.VMEM)
    return pl.pallas_call(
        body, name="gather_weights",
        out_shape=[jax.ShapeDtypeStruct((N_DEV * s.shape[0], s.shape[1]), s.dtype) for s in shards],
        in_specs=[vmem] * n, out_specs=[vmem] * n,
        scratch_shapes=[pltpu.SemaphoreType.DMA((n, 7)), pltpu.SemaphoreType.DMA((n, 7)), pltpu.SemaphoreType.DMA((n,))],
        compiler_params=_params(),
    )(*shards)


N_CHIP = 4


def _exchange_pairs(big, small):
    nb, ns = len(big), len(small)

    def body(*refs):
        big_in, small_in = refs[:nb], refs[nb:nb + ns]
        theirs, small_out = refs[nb + ns:2 * nb + ns], refs[2 * nb + ns:2 * nb + 2 * ns]
        pair_send, pair_recv, send_sems, recv_sems, local_sems = refs[2 * nb + 2 * ns:]
        x, y, c = _mesh_pos()
        me = 4 * x + 2 * y + c
        local = [pltpu.make_async_copy(small_in[a].at[me], small_out[a].at[me], local_sems.at[a]) for a in range(ns)]
        for cp in local:
            cp.start()
        copies = [pltpu.make_async_remote_copy(
            src_ref=big_in[a], dst_ref=theirs[a], send_sem=pair_send.at[a], recv_sem=pair_recv.at[a],
            device_id=(x, y, 1 - c), device_id_type=MESH) for a in range(nb)]
        for k in range(1, N_DEV):
            peer = (x ^ (k >> 2), y ^ ((k >> 1) & 1), c ^ (k & 1))
            peer_idx = 4 * peer[0] + 2 * peer[1] + peer[2]
            copies += [pltpu.make_async_remote_copy(
                src_ref=small_in[a].at[peer_idx], dst_ref=small_out[a].at[me], send_sem=send_sems.at[a, k - 1],
                recv_sem=recv_sems.at[a, k - 1], device_id=peer, device_id_type=MESH) for a in range(ns)]
        for cp in copies:
            cp.start()
        for cp in copies:
            cp.wait()
        for cp in local:
            cp.wait()

    hbm = pl.BlockSpec(memory_space=pl.ANY)
    out_shape = [jax.ShapeDtypeStruct(p.shape, p.dtype) for p in big + small]
    dma = pltpu.SemaphoreType.DMA
    res = pl.pallas_call(
        body, name="exchange_pairs", out_shape=out_shape,
        in_specs=[hbm] * (nb + ns), out_specs=[hbm] * len(out_shape),
        scratch_shapes=[dma((nb,)), dma((nb,)), dma((ns, 7)), dma((ns, 7)), dma((ns,))],
        compiler_params=_params(),
    )(*big, *small)
    return res[:nb], res[nb:]


def _pair_sum(name, mine, theirs, row_tile):
    _, rows, cols = mine.shape

    def body(a_ref, b_ref, o_ref):
        o_ref[...] = (a_ref[...].astype(F32) + b_ref[...].astype(F32)).astype(o_ref.dtype)

    spec = pl.BlockSpec((N_CHIP, row_tile, cols), lambda i: (0, i, 0))
    return pl.pallas_call(
        body, name=name, grid=(rows // row_tile,), in_specs=[spec, spec], out_specs=spec,
        out_shape=jax.ShapeDtypeStruct(mine.shape, mine.dtype),
        compiler_params=_params(dimension_semantics=("arbitrary",)),
    )(mine, theirs)


def _exchange_chips(parts):
    n = len(parts)

    def body(*refs):
        in_refs, out_refs = refs[:n], refs[n:2 * n]
        send_sems, recv_sems, local_sems = refs[2 * n:]
        x, y, c = _mesh_pos()
        my_chip = 2 * x + y
        own = [pltpu.make_async_copy(in_refs[a].at[my_chip], out_refs[a].at[my_chip], local_sems.at[a]) for a in range(n)]
        for cp in own:
            cp.start()
        copies = []
        for k in range(1, N_CHIP):
            px, py = x ^ (k >> 1), y ^ (k & 1)
            copies += [pltpu.make_async_remote_copy(
                src_ref=in_refs[a].at[2 * px + py], dst_ref=out_refs[a].at[my_chip], send_sem=send_sems.at[a, k - 1],
                recv_sem=recv_sems.at[a, k - 1], device_id=(px, py, c), device_id_type=MESH) for a in range(n)]
        for cp in copies:
            cp.start()
        for cp in copies:
            cp.wait()
        for cp in own:
            cp.wait()

    hbm = pl.BlockSpec(memory_space=pl.ANY)
    dma = pltpu.SemaphoreType.DMA
    return pl.pallas_call(
        body, name="exchange_chips",
        out_shape=[jax.ShapeDtypeStruct(p.shape, p.dtype) for p in parts],
        in_specs=[hbm] * n, out_specs=[hbm] * n,
        scratch_shapes=[dma((n, N_CHIP - 1)), dma((n, N_CHIP - 1)), dma((n,))],
        compiler_params=_params(),
    )(*parts)


def _adamw(w, g, m, v):
    m = ADAM_B1 * m + (1.0 - ADAM_B1) * g
    v = ADAM_B2 * v + (1.0 - ADAM_B2) * (g * g)
    m_hat = m / (1.0 - ADAM_B1 ** ADAM_STEP)
    v_hat = v / (1.0 - ADAM_B2 ** ADAM_STEP)
    return -ADAM_LR * (m_hat / (jnp.sqrt(v_hat) + ADAM_EPS) + ADAM_WD * w), m, v


def _reduce_adamw(name, parts, w, m, v, row_tile):
    n_parts, rows, cols = parts.shape

    def body(p_ref, w_ref, m_ref, v_ref, g_ref, d_ref, nm_ref, nv_ref):
        g = p_ref[0].astype(F32)
        for s in range(1, n_parts):
            g = g + p_ref[s].astype(F32)
        g_ref[...] = g
        d_ref[...], nm_ref[...], nv_ref[...] = _adamw(w_ref[...], g, m_ref[...], v_ref[...])

    tile = pl.BlockSpec((row_tile, cols), lambda i: (i, 0))
    return pl.pallas_call(
        body, name=name, grid=(rows // row_tile,),
        in_specs=[pl.BlockSpec((n_parts, row_tile, cols), lambda i: (0, i, 0)), tile, tile, tile],
        out_specs=[tile] * 4,
        out_shape=[jax.ShapeDtypeStruct((rows, cols), F32)] * 4,
        compiler_params=_params(dimension_semantics=("arbitrary",)),
    )(parts, w, m, v)


_SMALL = (("norm_g", 1024), ("ret_gn_g", 512), ("rwkv_mu", 2176), ("w0", 512), ("a0", 512), ("k_k", 512), ("k_a", 512),
          ("r_k", 512), ("rwkv_gn_g", 512), ("rwkv_gn_b", 512), ("final_norm_g", 1024))
_SMALL_LANES = sum(n for _, n in _SMALL) + 128
_WEIGHTS = ("norm_g", "w_in", "ret_gn_g", "rwkv_mu", "w_lora_up", "w0", "a_lora_up", "a0", "k_k", "k_a", "r_k", "rwkv_gn_g",
            "rwkv_gn_b", "w_out", "final_norm_g")


def _adamw_vectors(parts, wts, mom, var):
    k = len(_SMALL)

    def body(p_ref, *refs):
        w_refs, m_refs, v_refs, outs = refs[:k], refs[k:2 * k], refs[2 * k:3 * k], refs[3 * k:]
        g_all = p_ref[0]
        for s in range(1, N_DEV):
            g_all = g_all + p_ref[s]
        off = 0
        for i, (_, n) in enumerate(_SMALL):
            g = g_all[:, off:off + n]
            off += n
            outs[4 * i][...] = g
            outs[4 * i + 1][...], outs[4 * i + 2][...], outs[4 * i + 3][...] = _adamw(
                w_refs[i][...], g, m_refs[i][...], v_refs[i][...])
        outs[4 * k][...] = g_all[:, off:off + 128]

    vmem = pl.BlockSpec(memory_space=pltpu.VMEM)
    shapes = [jax.ShapeDtypeStruct((1, n), F32) for _, n in _SMALL for _ in range(4)] + [jax.ShapeDtypeStruct((1, 128), F32)]
    res = pl.pallas_call(
        body, name="adamw_vectors", out_shape=shapes,
        in_specs=[vmem] * (1 + 3 * k), out_specs=[vmem] * len(shapes), compiler_params=_params(),
    )(parts, *[wts[n] for n, _ in _SMALL], *[mom[n] for n, _ in _SMALL], *[var[n] for n, _ in _SMALL])
    return {n: res[4 * i:4 * i + 4] for i, (n, _) in enumerate(_SMALL)}, res[4 * k]


def kernel(x, norm_g, w_in, ret_gn_g, rwkv_mu, w_lora_up, w0, a_lora_up, a0, k_k, k_a, r_k, rwkv_gn_g, rwkv_gn_b, w_out, final_norm_g, loss_target, m_norm_g, m_w_in, m_ret_gn_g, m_rwkv_mu, m_w_lora_up, m_w0, m_a_lora_up, m_a0, m_k_k, m_k_a, m_r_k, m_rwkv_gn_g, m_rwkv_gn_b, m_w_out, m_final_norm_g, v_norm_g, v_w_in, v_ret_gn_g, v_rwkv_mu, v_w_lora_up, v_w0, v_a_lora_up, v_a0, v_k_k, v_k_a, v_r_k, v_rwkv_gn_g, v_rwkv_gn_b, v_w_out, v_final_norm_g):
    wts = dict(norm_g=norm_g, w_in=w_in, ret_gn_g=ret_gn_g, rwkv_mu=rwkv_mu, w_lora_up=w_lora_up, w0=w0, a_lora_up=a_lora_up,
               a0=a0, k_k=k_k, k_a=k_a, r_k=r_k, rwkv_gn_g=rwkv_gn_g, rwkv_gn_b=rwkv_gn_b, w_out=w_out,
               final_norm_g=final_norm_g)
    mom = dict(norm_g=m_norm_g, w_in=m_w_in, ret_gn_g=m_ret_gn_g, rwkv_mu=m_rwkv_mu, w_lora_up=m_w_lora_up, w0=m_w0,
               a_lora_up=m_a_lora_up, a0=m_a0, k_k=m_k_k, k_a=m_k_a, r_k=m_r_k, rwkv_gn_g=m_rwkv_gn_g,
               rwkv_gn_b=m_rwkv_gn_b, w_out=m_w_out, final_norm_g=m_final_norm_g)
    var = dict(norm_g=v_norm_g, w_in=v_w_in, ret_gn_g=v_ret_gn_g, rwkv_mu=v_rwkv_mu, w_lora_up=v_w_lora_up, w0=v_w0,
               a_lora_up=v_a_lora_up, a0=v_a0, k_k=v_k_k, k_a=v_k_a, r_k=v_r_k, rwkv_gn_g=v_rwkv_gn_g,
               rwkv_gn_b=v_rwkv_gn_b, w_out=v_w_out, final_norm_g=v_final_norm_g)
    shapes = {n: wts[n].shape for n in _WEIGHTS}

    g_in, g_out, g_wup, g_aup = _all_gather(
        [w_in[0].astype(BF16), w_out[0].astype(BF16), w_lora_up[0], a_lora_up[0]])
    w_in_bf = g_in.reshape(N_DEV, D_MODEL, SHARD_IN).transpose(1, 0, 2).reshape(D_MODEL, IN_COLS)
    wup_full = g_wup.reshape(N_DEV, LORA, SHARD_LORA).transpose(1, 0, 2).reshape(LORA, W)
    aup_full = g_aup.reshape(N_DEV, LORA, SHARD_LORA).transpose(1, 0, 2).reshape(LORA, W)

    loss, dx, g = _local_grads(
        x[0], loss_target[0], norm_g, w_in_bf, ret_gn_g, rwkv_mu, wup_full, w0, aup_full, a0, k_k, k_a,
        r_k.reshape(1, W), rwkv_gn_g, rwkv_gn_b, g_out, final_norm_g.reshape(1, D_MODEL))

    small = jnp.concatenate([g[n] for n, _ in _SMALL] + [loss], axis=1)
    core = lax.axis_index("c")
    by_core = lambda t: [lax.dynamic_index_in_dim(t, i, axis=1, keepdims=False) for i in (core, 1 - core)]
    in_mine, in_sib = by_core(g["w_in"].reshape(D_MODEL, N_CHIP, 2, SHARD_IN).transpose(1, 2, 0, 3).astype(BF16))
    out_mine, out_sib = by_core(g["w_out"].reshape(N_CHIP, 2, SHARD_OUT, D_MODEL).astype(BF16))
    (in_theirs, out_theirs), parts = _exchange_pairs(
        [in_sib, out_sib],
        [g["w_lora_up"].reshape(LORA, N_DEV, SHARD_LORA).transpose(1, 0, 2),
         g["a_lora_up"].reshape(LORA, N_DEV, SHARD_LORA).transpose(1, 0, 2),
         jnp.broadcast_to(small[None], (N_DEV, 1, _SMALL_LANES))])
    by_chip = _exchange_chips([_pair_sum("pair_sum_w_in", in_mine, in_theirs, 256),
                               _pair_sum("pair_sum_w_out", out_mine, out_theirs, SHARD_OUT)])
    res = {}
    res["w_in"] = _reduce_adamw("adamw_w_in", by_chip[0], w_in[0], m_w_in[0], v_w_in[0], 256)
    res["w_out"] = _reduce_adamw("adamw_w_out", by_chip[1], w_out[0], m_w_out[0], v_w_out[0], SHARD_OUT)
    res["w_lora_up"] = _reduce_adamw("adamw_w_lora_up", parts[0], w_lora_up[0], m_w_lora_up[0], v_w_lora_up[0], LORA)
    res["a_lora_up"] = _reduce_adamw("adamw_a_lora_up", parts[1], a_lora_up[0], m_a_lora_up[0], v_a_lora_up[0], LORA)
    as_row = lambda d: {n: d[n].reshape(1, size) for n, size in _SMALL}
    vec, loss_row = _adamw_vectors(parts[2], as_row(wts), as_row(mom), as_row(var))
    res.update(vec)
    res = {n: [t.reshape(shapes[n]) for t in res[n]] for n in _WEIGHTS}
    return (loss_row[0, 0], dx[None], *[res[n][0] for n in _WEIGHTS], *[res[n][1] for n in _WEIGHTS],
            *[res[n][2] for n in _WEIGHTS], *[res[n][3] for n in _WEIGHTS])
```

```python
import functools

import numpy as np
import jax
import jax.numpy as jnp
from jax import lax
from jax.experimental import pallas as pl
from jax.experimental.pallas import tpu as pltpu

F32 = jnp.float32
BF16 = jnp.bfloat16

D_MODEL = 1024
CHUNK = 64
RET_HEADS = 4
RET_DV = 128
RET_DK = 64
RET_QK = 256
RET_WIDTH = 512
RWKV_WIDTH = 512
RWKV_HEAD = 64
RWKV_HEADS = 8
LORA = 64
RET_COLS = 2 * RET_QK + 2 * RET_WIDTH
RWKV_COLS = 4 * RWKV_WIDTH + 2 * LORA
IN_COLS = RET_COLS + RWKV_COLS
ROPE_BASE = 10000.0
RMS_EPS = 1e-6
RET_GN_EPS = 1e-5
RWKV_GN_EPS = 64e-5
ADAM_LR = 0.001
ADAM_B1 = 0.9
ADAM_B2 = 0.999
ADAM_EPS = 1e-08
ADAM_WD = 0.01
ADAM_STEP = 10
N_DEV = 8
SHARD_IN = IN_COLS // N_DEV
SHARD_OUT = D_MODEL // N_DEV
SHARD_LORA = RWKV_WIDTH // N_DEV
VMEM_LIMIT = 56 * 1024 * 1024
TOK_TILE = 256
WKV_CHUNK = 64

MESH = pl.DeviceIdType.MESH


def _dot_bf(a, b):
    return jnp.dot(a.astype(BF16), b.astype(BF16), preferred_element_type=F32)


def _dot_nt_bf(a, b):
    return lax.dot_general(a.astype(BF16), b.astype(BF16), (((1,), (1,)), ((), ())), preferred_element_type=F32)


def _dot_tn_bf(a, b):
    return lax.dot_general(a.astype(BF16), b.astype(BF16), (((0,), (0,)), ((), ())), preferred_element_type=F32)


@jax.custom_vjp
def _mm(a, b):
    return _dot_bf(a, b)


@jax.custom_vjp
def _mm_nt(a, b):
    return _dot_nt_bf(a, b)


@jax.custom_vjp
def _mm_tn(a, b):
    return _dot_tn_bf(a, b)


_mm.defvjp(lambda a, b: (_dot_bf(a, b), (a, b)), lambda res, g: (_dot_nt_bf(g, res[1]), _dot_tn_bf(res[0], g)))
_mm_nt.defvjp(lambda a, b: (_dot_nt_bf(a, b), (a, b)), lambda res, g: (_dot_bf(g, res[1]), _dot_tn_bf(g, res[0])))
_mm_tn.defvjp(lambda a, b: (_dot_tn_bf(a, b), (a, b)), lambda res, g: (_dot_nt_bf(res[1], g), _dot_bf(res[0], g)))


def _trunc(x):
    return lax.bitcast_convert_type(lax.bitcast_convert_type(x, jnp.uint32) & jnp.uint32(0xFFFF0000), F32)


def _two_piece(x):
    hi = _trunc(x)
    return jnp.concatenate([hi, x - hi], axis=1)


def _mix_raw(x, mat2):
    return _unstack(jnp.dot(_two_piece(_stack(x)), mat2, preferred_element_type=F32))


@jax.custom_vjp
def _head_mix(x, mat2):
    return _mix_raw(x, mat2)


_head_mix.defvjp(lambda x, mat2: (_mix_raw(x, mat2), mat2), lambda mat2, g: (_mix_raw(g, mat2), jnp.zeros_like(mat2)))


def _swap_halves(x):
    lane = lax.broadcasted_iota(jnp.int32, x.shape, 1)
    return jnp.where((lane & (RET_DK - 1)) < RET_DK // 2, pltpu.roll(x, RET_QK - RET_DK // 2, axis=1),
                     pltpu.roll(x, RET_DK // 2, axis=1))


@jax.custom_vjp
def _rot(x):
    return _swap_halves(x)


_rot.defvjp(lambda x: (_swap_halves(x), None), lambda _, g: (_swap_halves(g),))


def _params(**kw):
    return pltpu.CompilerParams(vmem_limit_bytes=VMEM_LIMIT, **kw)


def _full(shape):
    nd = len(shape)
    return pl.BlockSpec(shape, lambda i, _nd=nd: (0,) * _nd)


def _rows(tile, width):
    return pl.BlockSpec((tile, width), lambda i: (i, 0))


def _block_mix(n, blk, scale=1.0):
    idx = np.arange(n) // blk
    m = (idx[:, None] == idx[None, :]).astype(np.float32) * scale
    return jnp.asarray(np.concatenate([m, m], axis=0))


def _rope_tables(T):
    half = RET_DK // 2
    expo = -np.arange(half, dtype=np.float32) / np.float32(half)
    freqs = np.exp(expo * np.float32(np.log(ROPE_BASE))).astype(np.float32)
    ang = np.arange(T, dtype=np.float32)[:, None] * freqs[None, :]
    cos, sin = np.cos(ang).astype(np.float32), np.sin(ang).astype(np.float32)
    cos_h = np.concatenate([cos, cos], axis=1)
    sin_h = np.concatenate([-sin, sin], axis=1)
    cos_t = np.tile(cos_h, (1, RET_HEADS))
    sin_t = np.tile(sin_h, (1, RET_HEADS))
    return jnp.asarray(cos_t), jnp.asarray(sin_t)


def _ret_tables():
    h = np.arange(RET_HEADS, dtype=np.float32)
    lg = np.log(1.0 - np.exp2(-5.0 - h)).astype(np.float32)
    idx = np.arange(CHUNK, dtype=np.float32)
    intra = np.exp(lg[:, None, None] * np.abs(idx[:, None] - idx[None, :])).astype(np.float32)
    q_dec = np.exp(lg[:, None] * (idx[None, :] + 1.0)).astype(np.float32)
    k_dec = np.exp(lg[:, None] * (CHUNK - 1.0 - idx[None, :])).astype(np.float32)
    chunk_dec = np.exp(lg * CHUNK).astype(np.float32)
    lane_head = np.arange(RET_QK) // RET_DK
    mask = (lane_head[None, :] == np.arange(RET_HEADS)[:, None]).astype(np.float32)
    m = np.broadcast_to(mask[:, None, :], (RET_HEADS, CHUNK, RET_QK)).copy()
    qd = m * q_dec[:, :, None]
    kd = m * k_dec[:, :, None]
    return jnp.asarray(intra), jnp.asarray(m), jnp.asarray(qd), jnp.asarray(kd), [float(c) for c in chunk_dec]


def _rmsnorm(x, g):
    return x * lax.rsqrt(jnp.mean(x * x, axis=-1, keepdims=True) + RMS_EPS) * g


def _inproj(x, norm_g, w_in):
    T = x.shape[0]

    def body(x_ref, g_ref, w_ref, pr_ref, pw_ref, u_ref):
        ub = _rmsnorm(x_ref[...], g_ref[...]).astype(BF16)
        u_ref[...] = ub
        pr_ref[...] = jnp.dot(ub, w_ref[:, :RET_COLS], preferred_element_type=F32)
        pw_ref[...] = jnp.dot(ub, w_ref[:, RET_COLS:], preferred_element_type=F32)

    return pl.pallas_call(
        body, name="inproj", grid=(T // TOK_TILE,),
        in_specs=[_rows(TOK_TILE, D_MODEL), _full((1, D_MODEL)), _full((D_MODEL, IN_COLS))],
        out_specs=[_rows(TOK_TILE, RET_COLS), _rows(TOK_TILE, RWKV_COLS), _rows(TOK_TILE, D_MODEL)],
        out_shape=[jax.ShapeDtypeStruct((T, RET_COLS), F32), jax.ShapeDtypeStruct((T, RWKV_COLS), F32),
                   jax.ShapeDtypeStruct((T, D_MODEL), BF16)],
        compiler_params=_params(dimension_semantics=("arbitrary",)),
    )(x, norm_g, w_in)


def _ret_chunk(pq, pk, v_heads, s_heads, cos_t, sin_t, dec, hm, qd, kd, chunk_dec):
    q = pq * cos_t + _rot(pq) * sin_t
    k = (pk * cos_t + _rot(pk) * sin_t) * (RET_DK ** -0.5)
    outs, s_out = [], []
    for h in range(RET_HEADS):
        sc = _mm_nt(q * hm[h], k * hm[h]) * dec[h]
        intra = _mm(sc, v_heads[h])
        kv = _mm_tn(k * kd[h], v_heads[h])
        inter = _mm(q * qd[h], s_heads[h])
        outs.append(intra + inter)
        s_out.append(s_heads[h] * chunk_dec[h] + kv)
    return tuple(outs), tuple(s_out)


def _ret_specs():
    const = [_full((RET_HEADS, CHUNK, CHUNK)), _full((RET_HEADS, CHUNK, RET_QK)),
             _full((RET_HEADS, CHUNK, RET_QK)), _full((RET_HEADS, CHUNK, RET_QK))]
    return const


def _ret_fwd(p_ret, tabs):
    T = p_ret.shape[0]
    nc = T // CHUNK
    cos_t, sin_t, dec, hm, qd, kd, chunk_dec = tabs

    def body(p_ref, cos_ref, sin_ref, dec_ref, hm_ref, qd_ref, kd_ref, out_ref, sin_save_ref, s_scr):
        @pl.when(pl.program_id(0) == 0)
        def _():
            s_scr[...] = jnp.zeros_like(s_scr)

        s_heads = tuple(s_scr[h] for h in range(RET_HEADS))
        for h in range(RET_HEADS):
            sin_save_ref[0, h] = s_heads[h]
        pq = p_ref[:, 0:RET_QK]
        pk = p_ref[:, RET_QK:2 * RET_QK]
        v_heads = tuple(p_ref[:, 2 * RET_QK + RET_DV * h:2 * RET_QK + RET_DV * (h + 1)] for h in range(RET_HEADS))
        outs, s_out = _ret_chunk(pq, pk, v_heads, s_heads, cos_ref[...], sin_ref[...], dec_ref[...],
                                 hm_ref[...], qd_ref[...], kd_ref[...], chunk_dec)
        for h in range(RET_HEADS):
            out_ref[:, RET_DV * h:RET_DV * (h + 1)] = outs[h]
            s_scr[h] = s_out[h]

    return pl.pallas_call(
        body, name="ret_fwd", grid=(nc,),
        in_specs=[pl.BlockSpec((CHUNK, RET_COLS), lambda i: (i, 0)), _rows(CHUNK, RET_QK), _rows(CHUNK, RET_QK)] + _ret_specs(),
        out_specs=[_rows(CHUNK, RET_WIDTH), pl.BlockSpec((1, RET_HEADS, RET_QK, RET_DV), lambda i: (i, 0, 0, 0))],
        out_shape=[jax.ShapeDtypeStruct((T, RET_WIDTH), F32), jax.ShapeDtypeStruct((nc, RET_HEADS, RET_QK, RET_DV), F32)],
        scratch_shapes=[pltpu.VMEM((RET_HEADS, RET_QK, RET_DV), F32)],
        compiler_params=_params(dimension_semantics=("arbitrary",)),
    )(p_ret, cos_t, sin_t, dec, hm, qd, kd)


def _ret_bwd(p_ret, s_saved, d_ret, tabs):
    T = p_ret.shape[0]
    nc = T // CHUNK
    cos_t, sin_t, dec, hm, qd, kd, chunk_dec = tabs

    def body(p_ref, s_ref, dret_ref, cos_ref, sin_ref, dec_ref, hm_ref, qd_ref, kd_ref, dp_ref, ds_scr):
        @pl.when(pl.program_id(0) == 0)
        def _():
            ds_scr[...] = jnp.zeros_like(ds_scr)

        pq = p_ref[:, 0:RET_QK]
        pk = p_ref[:, RET_QK:2 * RET_QK]
        v_heads = tuple(p_ref[:, 2 * RET_QK + RET_DV * h:2 * RET_QK + RET_DV * (h + 1)] for h in range(RET_HEADS))
        s_heads = tuple(s_ref[0, h] for h in range(RET_HEADS))
        consts = (cos_ref[...], sin_ref[...], dec_ref[...], hm_ref[...], qd_ref[...], kd_ref[...])
        _, vjp = jax.vjp(lambda a, b, c, d: _ret_chunk(a, b, c, d, *consts, chunk_dec), pq, pk, v_heads, s_heads)
        d_out = tuple(dret_ref[:, RET_DV * h:RET_DV * (h + 1)] for h in range(RET_HEADS))
        d_s = tuple(ds_scr[h] for h in range(RET_HEADS))
        dq, dk, dv, ds_in = vjp((d_out, d_s))
        dp_ref[:, 0:RET_QK] = dq
        dp_ref[:, RET_QK:2 * RET_QK] = dk
        for h in range(RET_HEADS):
            dp_ref[:, 2 * RET_QK + RET_DV * h:2 * RET_QK + RET_DV * (h + 1)] = dv[h]
            ds_scr[h] = ds_in[h]

    rev = lambda i: (nc - 1 - i, 0)
    return pl.pallas_call(
        body, name="ret_bwd", grid=(nc,),
        in_specs=[pl.BlockSpec((CHUNK, RET_COLS), rev),
                  pl.BlockSpec((1, RET_HEADS, RET_QK, RET_DV), lambda i: (nc - 1 - i, 0, 0, 0)),
                  pl.BlockSpec((CHUNK, RET_WIDTH), rev), pl.BlockSpec((CHUNK, RET_QK), rev), pl.BlockSpec((CHUNK, RET_QK), rev)]
        + _ret_specs(),
        out_specs=pl.BlockSpec((CHUNK, 2 * RET_QK + RET_WIDTH), rev),
        out_shape=jax.ShapeDtypeStruct((T, 2 * RET_QK + RET_WIDTH), F32),
        scratch_shapes=[pltpu.VMEM((RET_HEADS, RET_QK, RET_DV), F32)],
        compiler_params=_params(dimension_semantics=("arbitrary",)),
    )(p_ret, s_saved, d_ret, cos_t, sin_t, dec, hm, qd, kd)


def _wkv_consts():
    lane = lax.broadcasted_iota(jnp.int32, (RWKV_HEAD, RWKV_WIDTH), 1)
    sub = lax.broadcasted_iota(jnp.int32, (RWKV_HEAD, RWKV_WIDTH), 0)
    diag = ((lane & (RWKV_HEAD - 1)) == sub).astype(F32)
    r = lax.broadcasted_iota(jnp.int32, (3 * 128, 128), 0)
    c = lax.broadcasted_iota(jnp.int32, (3 * 128, 128), 1)
    ones = (((r & 127) >> 6) == (c >> 6)).astype(BF16)
    return diag, ones


def _stack(x):
    return jnp.concatenate([x[:, 128 * p:128 * (p + 1)] for p in range(4)], axis=0)


def _unstack(y):
    n = y.shape[0] // 4
    return jnp.concatenate([y[n * p:n * (p + 1)] for p in range(4)], axis=1)


def _split(x, n):
    pieces = []
    for _ in range(n):
        p = x.astype(BF16)
        pieces.append(p)
        x = x - p.astype(F32)
    return pieces


def _lane_sum(x, ones):
    return _unstack(jnp.dot(_two_piece(_stack(x)), ones[:256].astype(F32), preferred_element_type=F32))


def _lane_sum_bf(x, ones):
    return _unstack(jnp.dot(_stack(x).astype(BF16), ones[:128], preferred_element_type=F32))


def _colsum(x):
    return jnp.sum(x, axis=0, keepdims=True)


def _expand_cols(xt, t):
    lane = lax.broadcasted_iota(jnp.int32, (RWKV_HEAD, 128), 1)
    tiles = []
    for p in range(4):
        lo = jnp.broadcast_to(xt[128 * p:128 * p + RWKV_HEAD, t:t + 1], (RWKV_HEAD, 128))
        hi = jnp.broadcast_to(xt[128 * p + RWKV_HEAD:128 * (p + 1), t:t + 1], (RWKV_HEAD, 128))
        tiles.append(jnp.where(lane < RWKV_HEAD, lo, hi))
    return jnp.concatenate(tiles, axis=1)


def _head_sums(x, ones):
    return _unstack(jnp.dot(jnp.concatenate(_split(_stack(x), 3), axis=1), ones, preferred_element_type=F32))


def _wkv_fwd(r, w, k, v, kap, a):
    T = r.shape[0]
    C = WKV_CHUNK
    nc = T // C

    def body(r_ref, w_ref, k_ref, v_ref, kap_ref, a_ref, o_ref, s_all_ref, sa_all_ref, s_scr):
        @pl.when(pl.program_id(0) == 0)
        def _():
            s_scr[...] = jnp.zeros_like(s_scr)

        diag, ones = _wkv_consts()
        rr, ww, kk, vv, kap_, aa = (ref[...] for ref in (r_ref, w_ref, k_ref, v_ref, kap_ref, a_ref))
        bb = kap_ * aa
        c1 = _head_sums(pltpu.roll(bb, 1, axis=0) * kap_, ones)
        row = lambda x, t: x[t:t + 1]

        v_cols = vv.T

        s_prev = s_scr[...]
        sa = _lane_sum(s_prev * (-row(kap_, 0)), ones)
        ls, rows = None, []

        def emit_o(t, s_t):
            rows.append(_colsum(_lane_sum_bf(s_t * row(rr, t), ones) * diag))
            if t % 8 == 7:
                o_ref[t - 7:t + 1, :] = jnp.concatenate(rows, axis=0)
                rows.clear()

        for t in range(C):
            u = s_prev * row(ww, t) + _expand_cols(v_cols, t) * row(kk, t)
            if t > 0:
                sa = ls - sa * row(c1, t)
            if t + 1 < C:
                ls = _lane_sum(u * (-row(kap_, t + 1)), ones)
            if t > 0:
                emit_o(t - 1, s_prev)
            s_prev = u + sa * row(bb, t)
            s_all_ref[t] = s_prev
            sa_all_ref[t] = sa.astype(BF16)
        emit_o(C - 1, s_prev)
        s_scr[...] = s_prev

    spec = _rows(C, RWKV_WIDTH)
    return pl.pallas_call(
        body, name="wkv_fwd", grid=(nc,),
        in_specs=[spec] * 6,
        out_specs=[spec, pl.BlockSpec((C, RWKV_HEAD, RWKV_WIDTH), lambda i: (i, 0, 0)),
                   pl.BlockSpec((C, RWKV_HEAD, RWKV_WIDTH), lambda i: (i, 0, 0))],
        out_shape=[jax.ShapeDtypeStruct((T, RWKV_WIDTH), F32), jax.ShapeDtypeStruct((T, RWKV_HEAD, RWKV_WIDTH), F32),
                   jax.ShapeDtypeStruct((T, RWKV_HEAD, RWKV_WIDTH), BF16)],
        scratch_shapes=[pltpu.VMEM((RWKV_HEAD, RWKV_WIDTH), F32)],
        compiler_params=_params(dimension_semantics=("arbitrary",)),
    )(r, w, k, v, kap, a)


def _wkv_bwd(r, w, k, v, kap, a, s_all, sa_all, d_o):
    T = r.shape[0]
    C = WKV_CHUNK
    nc = T // C

    def body(r_ref, w_ref, k_ref, v_ref, kap_ref, a_ref, s_ref, s_before_ref, sa_ref, do_ref,
             dr_ref, dw_ref, dk_ref, dv_ref, dkap_ref, da_ref, ds_scr):
        first_chunk = pl.program_id(0) == nc - 1

        @pl.when(pl.program_id(0) == 0)
        def _():
            ds_scr[...] = jnp.zeros_like(ds_scr)

        diag, ones = _wkv_consts()
        rr, ww, kk, vv, kap_, aa, dd = (ref[...] for ref in (r_ref, w_ref, k_ref, v_ref, kap_ref, a_ref, do_ref))
        bb = kap_ * aa
        e1 = _head_sums(pltpu.roll(kap_, C - 1, axis=0) * bb, ones)
        row = lambda x, t: x[t:t + 1]

        def state_before(t):
            return s_ref[t - 1] if t > 0 else jnp.where(first_chunk, 0.0, s_before_ref[0])

        v_cols, do_cols = vv.T, dd.T

        d_sn, dsa, rows = None, None, [None] * C

        def emit_rows(t, d_sn_t, dsa_t):
            s_prev, dof = state_before(t), _expand_cols(do_cols, t)
            dv = _colsum(_lane_sum_bf(d_sn_t * row(kk, t), ones) * diag)
            db = _colsum(d_sn_t * sa_ref[t].astype(F32))
            rows[t] = (_colsum(s_ref[t] * dof), _colsum(d_sn_t * s_prev), _colsum(d_sn_t * _expand_cols(v_cols, t)), dv,
                       db * row(aa, t) - _colsum(dsa_t * s_prev), db * row(kap_, t))
            if t % 8 == 0:
                for j, ref in enumerate((dr_ref, dw_ref, dk_ref, dv_ref, dkap_ref, da_ref)):
                    ref[t:t + 8, :] = jnp.concatenate([rows[u][j] for u in range(t, t + 8)], axis=0)

        for t in reversed(range(C)):
            dof = _expand_cols(do_cols, t)
            if t == C - 1:
                d_sn = ds_scr[...] + dof * row(rr, t)
                dsa = _lane_sum(d_sn * row(bb, t), ones)
            else:
                v_t = d_sn * row(ww, t + 1) + dof * row(rr, t)
                ls = _lane_sum(v_t * row(bb, t), ones)
                emit_rows(t + 1, d_sn, dsa)
                d_sn = v_t - dsa * row(kap_, t + 1)
                dsa = ls - dsa * row(e1, t)
        emit_rows(0, d_sn, dsa)
        d_s = d_sn * row(ww, 0) - dsa * row(kap_, 0)
        ds_scr[...] = d_s

    spec = pl.BlockSpec((C, RWKV_WIDTH), lambda i: (nc - 1 - i, 0))
    states = pl.BlockSpec((C, RWKV_HEAD, RWKV_WIDTH), lambda i: (nc - 1 - i, 0, 0))
    before = pl.BlockSpec((1, RWKV_HEAD, RWKV_WIDTH), lambda i: (jnp.maximum((nc - 1 - i) * C - 1, 0), 0, 0))
    return pl.pallas_call(
        body, name="wkv_bwd", grid=(nc,),
        in_specs=[spec] * 6 + [states, before, states, spec],
        out_specs=[spec] * 6,
        out_shape=[jax.ShapeDtypeStruct((T, RWKV_WIDTH), F32)] * 6,
        scratch_shapes=[pltpu.VMEM((RWKV_HEAD, RWKV_WIDTH), F32)],
        compiler_params=_params(dimension_semantics=("arbitrary",)),
    )(r, w, k, v, kap, a, s_all, s_all, sa_all, d_o)


W = RWKV_WIDTH


def _softplus(y):
    return jnp.maximum(y, 0.0) + jnp.log(1.0 + jnp.exp(-jnp.abs(y)))


def _prep_fn(kr, xwa, w0, a0, k_k, k_a, wup_pad, aup_pad, ones64):
    w_log = -_softplus(-(w0 + _mm(jnp.tanh(xwa), wup_pad))) - 0.5
    decay = jnp.exp(-jnp.exp(w_log))
    a = jax.nn.sigmoid(a0 + _mm(xwa, aup_pad))
    kk = kr * k_k
    kap = kk / jnp.maximum(jnp.sqrt(_head_mix(kk * kk, ones64)), 1e-12)
    k = kr * (1.0 + (a - 1.0) * k_a)
    return decay, k, kap, a


def _shift_down(p, first_row):
    rows = lax.broadcasted_iota(jnp.int32, p.shape, 0)
    return jnp.where(rows == 0, first_row, pltpu.roll(p, 1, axis=0))


def _shift_up(z, last_row):
    n = z.shape[0]
    rows = lax.broadcasted_iota(jnp.int32, z.shape, 0)
    return jnp.where(rows == n - 1, last_row, pltpu.roll(z, n - 1, axis=0))


def _prev_block_spec():
    return pl.BlockSpec((8, RWKV_COLS), lambda i: (jnp.maximum(i * (TOK_TILE // 8) - 1, 0), 0))


def _mixed(p_ref, prev8_ref, mu_ref, first_tile):
    p = p_ref[...]
    first_row = jnp.where(first_tile, 0.0, prev8_ref[7:8, :])
    prev = _shift_down(p, first_row)
    return p, prev, p + mu_ref[...] * (prev - p)


def _prep_fwd(p_rwkv, mu, w0, a0, k_k, k_a, wup_pad, aup_pad, ones64):
    T = p_rwkv.shape[0]

    def body(p_ref, prev8_ref, mu_ref, w0_ref, a0_ref, kk_ref, ka_ref, wup_ref, aup_ref, ones_ref,
             r_ref, w_ref, k_ref, v_ref, kap_ref, a_ref, g_ref):
        _, _, ps = _mixed(p_ref, prev8_ref, mu_ref, pl.program_id(0) == 0)
        decay, k, kap, a = _prep_fn(ps[:, W:2 * W], ps[:, 4 * W:], w0_ref[...], a0_ref[...], kk_ref[...], ka_ref[...],
                                    wup_ref[...], aup_ref[...], ones_ref[...])
        r_ref[...] = ps[:, 0:W]
        w_ref[...] = decay
        k_ref[...] = k
        v_ref[...] = ps[:, 2 * W:3 * W]
        kap_ref[...] = kap
        a_ref[...] = a
        g_ref[...] = ps[:, 3 * W:4 * W]

    vec = _full((1, W))
    return pl.pallas_call(
        body, name="prep_fwd", grid=(T // TOK_TILE,),
        in_specs=[_rows(TOK_TILE, RWKV_COLS), _prev_block_spec(), _full((1, RWKV_COLS)), vec, vec, vec, vec,
                  _full((2 * LORA, W)), _full((2 * LORA, W)), _full((256, 128))],
        out_specs=[_rows(TOK_TILE, W)] * 7,
        out_shape=[jax.ShapeDtypeStruct((T, W), F32)] * 7,
        compiler_params=_params(dimension_semantics=("arbitrary",)),
    )(p_rwkv, p_rwkv, mu, w0, a0, k_k, k_a, wup_pad, aup_pad, ones64)


def _prep_bwd(p_rwkv, mu, w0, a0, k_k, k_a, wup_pad, aup_pad, ones64, dr, dw, dk, dv, dkap, da, dg, dr2, dk2, dv2):
    T = p_rwkv.shape[0]
    nt = T // TOK_TILE

    def body(p_ref, prev8_ref, mu_ref, w0_ref, a0_ref, kk_ref, ka_ref, wup_ref, aup_ref, ones_ref,
             dr_ref, dw_ref, dk_ref, dv_ref, dkap_ref, da_ref, dg_ref, dr2_ref, dk2_ref, dv2_ref,
             dp_ref, dmu_ref, dw0_ref, da0_ref, dkk_ref, dka_ref, dwup_ref, daup_ref, zrow_scr):
        i = pl.program_id(0)
        accs = (dmu_ref, dw0_ref, da0_ref, dkk_ref, dka_ref, dwup_ref, daup_ref)

        @pl.when(i == 0)
        def _():
            zrow_scr[...] = jnp.zeros_like(zrow_scr)
            for ref in accs:
                ref[...] = jnp.zeros_like(ref)

        p, prev, ps = _mixed(p_ref, prev8_ref, mu_ref, i == nt - 1)
        ones = ones_ref[...]
        _, vjp = jax.vjp(lambda *args: _prep_fn(*args, ones), ps[:, W:2 * W], ps[:, 4 * W:], w0_ref[...], a0_ref[...],
                         kk_ref[...], ka_ref[...], wup_ref[...], aup_ref[...])
        dkr, dxwa, dw0, da0, dkk, dka, dwup, daup = vjp(
            (dw_ref[...], dk_ref[...] + dk2_ref[...], dkap_ref[...], da_ref[...]))
        dps = jnp.concatenate([dr_ref[...] + dr2_ref[...], dkr, dv_ref[...] + dv2_ref[...], dg_ref[...], dxwa], axis=1)
        z = dps * mu_ref[...]
        dp_ref[...] = dps - z + _shift_up(z, zrow_scr[0:1, :])
        zrow_scr[0:1, :] = z[0:1, :]
        for ref, val in zip(accs, (_colsum(dps * (prev - p)), dw0, da0, dkk, dka, dwup, daup)):
            ref[...] += val

    rev = lambda i: (nt - 1 - i, 0)
    vec = _full((1, W))
    lora = _full((2 * LORA, W))
    tile = pl.BlockSpec((TOK_TILE, W), rev)
    prev8 = pl.BlockSpec((8, RWKV_COLS), lambda i: (jnp.maximum((nt - 1 - i) * (TOK_TILE // 8) - 1, 0), 0))
    return pl.pallas_call(
        body, name="prep_bwd", grid=(nt,),
        in_specs=[pl.BlockSpec((TOK_TILE, RWKV_COLS), rev), prev8, _full((1, RWKV_COLS)), vec, vec, vec, vec, lora, lora,
                  _full((256, 128))] + [tile] * 10,
        out_specs=[pl.BlockSpec((TOK_TILE, RWKV_COLS), rev), _full((1, RWKV_COLS)), vec, vec, vec, vec, lora, lora],
        out_shape=[jax.ShapeDtypeStruct((T, RWKV_COLS), F32), jax.ShapeDtypeStruct((1, RWKV_COLS), F32)]
        + [jax.ShapeDtypeStruct((1, W), F32)] * 4 + [jax.ShapeDtypeStruct((2 * LORA, W), F32)] * 2,
        scratch_shapes=[pltpu.VMEM((8, RWKV_COLS), F32)],
        compiler_params=_params(dimension_semantics=("arbitrary",)),
    )(p_rwkv, p_rwkv, mu, w0, a0, k_k, k_a, wup_pad, aup_pad, ones64, dr, dw, dk, dv, dkap, da, dg, dr2, dk2, dv2)


def _silu(x):
    return x * jax.nn.sigmoid(x)


def _post_y(o, r, k, v, g_rw, ret_raw, g_ret, ret_gn_g, gn_g, gn_b, r_k, avg128, avg64, ones64):
    xc = ret_raw - _head_mix(ret_raw, avg128)
    ret = xc * lax.rsqrt(_head_mix(xc * xc, avg128) + RET_GN_EPS)
    y_ret = _silu(g_ret) * (ret * ret_gn_g)
    oc = o - _head_mix(o, avg64)
    on = oc * lax.rsqrt(_head_mix(oc * oc, avg64) + RWKV_GN_EPS) * gn_g + gn_b
    bonus = _head_mix(r * k * r_k, ones64) * v
    y_rwkv = _silu(g_rw) * (on + bonus)
    return y_ret, y_rwkv


def _post_loss(h, final_g, target):
    err = _rmsnorm(h, final_g) - target
    return 0.5 * jnp.sum(jnp.mean(err * err, axis=-1))


def _post(o, r, k, v, g_rw, ret_raw, p_ret, x, target, ret_gn_g, gn_g, gn_b, r_k, final_g, w_out, avg128, avg64, ones64):
    T = x.shape[0]
    n_tok_out = 8

    def body(o_ref, r_ref, k_ref, v_ref, grw_ref, ret_ref, gret_ref, x_ref, tgt_ref, rg_ref, gg_ref, gb_ref, rk_ref, fg_ref,
             wo_ref, a128_ref, a64_ref, ones_ref, *outs):
        tok_outs, (dwo_ref, drg_ref, dgg_ref, dgb_ref, drk_ref, dfg_ref, loss_ref) = outs[:n_tok_out], outs[n_tok_out:]
        accs = (dwo_ref, drg_ref, dgg_ref, dgb_ref, drk_ref, dfg_ref, loss_ref)

        @pl.when(pl.program_id(0) == 0)
        def _():
            for ref in accs:
                ref[...] = jnp.zeros_like(ref)

        consts = (a128_ref[...], a64_ref[...], ones_ref[...])
        (y_ret, y_rwkv), vjp = jax.vjp(
            lambda *args: _post_y(*args, *consts), o_ref[...], r_ref[...], k_ref[...], v_ref[...], grw_ref[...], ret_ref[...],
            gret_ref[...], rg_ref[...], gg_ref[...], gb_ref[...], rk_ref[...])
        h = x_ref[...] + _dot_bf(y_ret, wo_ref[0:RET_WIDTH, :]) + _dot_bf(y_rwkv, wo_ref[RET_WIDTH:, :])
        loss, (dh, dfg) = jax.value_and_grad(_post_loss, argnums=(0, 1))(h, fg_ref[...], tgt_ref[...])
        dy_ret = _dot_nt_bf(dh, wo_ref[0:RET_WIDTH, :])
        dy_rwkv = _dot_nt_bf(dh, wo_ref[RET_WIDTH:, :])
        do, dr, dk, dv, dgrw, dret, dgret, drg, dgg, dgb, drk = vjp((dy_ret, dy_rwkv))
        for ref, val in zip(tok_outs, (dh, do, dr, dk, dv, dgrw, dret, dgret)):
            ref[...] = val
        dwo_ref[0:RET_WIDTH, :] += _dot_tn_bf(y_ret, dh)
        dwo_ref[RET_WIDTH:, :] += _dot_tn_bf(y_rwkv, dh)
        for ref, val in zip(accs[1:], (drg, dgg, dgb, drk, dfg, jnp.full((1, 128), loss, F32))):
            ref[...] += val

    tile = _rows(TOK_TILE, W)
    wide = _rows(TOK_TILE, D_MODEL)
    vec = _full((1, W))
    sq = _full((256, 128))
    return pl.pallas_call(
        body, name="post", grid=(T // TOK_TILE,),
        in_specs=[tile] * 6 + [pl.BlockSpec((TOK_TILE, W), lambda i: (i, 2)), wide, wide, vec, vec, vec, vec,
                               _full((1, D_MODEL)), _full((D_MODEL, D_MODEL)), sq, sq, sq],
        out_specs=[wide] + [tile] * 7 + [_full((D_MODEL, D_MODEL)), vec, vec, vec, vec, _full((1, D_MODEL)), _full((1, 128))],
        out_shape=[jax.ShapeDtypeStruct((T, D_MODEL), F32)] + [jax.ShapeDtypeStruct((T, W), F32)] * 7
        + [jax.ShapeDtypeStruct((D_MODEL, D_MODEL), F32)] + [jax.ShapeDtypeStruct((1, W), F32)] * 4
        + [jax.ShapeDtypeStruct((1, D_MODEL), F32), jax.ShapeDtypeStruct((1, 128), F32)],
        compiler_params=_params(dimension_semantics=("arbitrary",)),
    )(o, r, k, v, g_rw, ret_raw, p_ret, x, target, ret_gn_g, gn_g, gn_b, r_k, final_g, w_out, avg128, avg64, ones64)


def _inproj_bwd_x(x, norm_g, dp_qkv, dg_ret, dp_rwkv, dh, w_in):
    T = x.shape[0]
    n_qkv = 2 * RET_QK + RET_WIDTH

    def body(x_ref, g_ref, dqkv_ref, dgret_ref, drwkv_ref, dh_ref, w_ref, dx_ref, dg_ref):
        @pl.when(pl.program_id(0) == 0)
        def _():
            dg_ref[...] = jnp.zeros_like(dg_ref)

        _, vjp = jax.vjp(_rmsnorm, x_ref[...], g_ref[...])
        du = (_dot_nt_bf(dqkv_ref[...], w_ref[:, 0:n_qkv]) + _dot_nt_bf(dgret_ref[...], w_ref[:, n_qkv:RET_COLS])
              + _dot_nt_bf(drwkv_ref[...], w_ref[:, RET_COLS:]))
        dx, dg = vjp(du)
        dx_ref[...] = dx + dh_ref[...]
        dg_ref[...] += dg

    return pl.pallas_call(
        body, name="inproj_bwd_x", grid=(T // TOK_TILE,),
        in_specs=[_rows(TOK_TILE, D_MODEL), _full((1, D_MODEL)), _rows(TOK_TILE, n_qkv), _rows(TOK_TILE, RET_WIDTH),
                  _rows(TOK_TILE, RWKV_COLS), _rows(TOK_TILE, D_MODEL), _full((D_MODEL, IN_COLS))],
        out_specs=[_rows(TOK_TILE, D_MODEL), _full((1, D_MODEL))],
        out_shape=[jax.ShapeDtypeStruct((T, D_MODEL), F32), jax.ShapeDtypeStruct((1, D_MODEL), F32)],
        compiler_params=_params(dimension_semantics=("arbitrary",)),
    )(x, norm_g, dp_qkv, dg_ret, dp_rwkv, dh, w_in)


def _grad_w(name, u, dp):
    T, n = dp.shape
    tile = 2 * TOK_TILE

    def body(u_ref, dp_ref, out_ref):
        @pl.when(pl.program_id(0) == 0)
        def _():
            out_ref[...] = jnp.zeros_like(out_ref)

        out_ref[...] += _dot_tn_bf(u_ref[...], dp_ref[...])

    return pl.pallas_call(
        body, name=name, grid=(T // tile,),
        in_specs=[_rows(tile, D_MODEL), _rows(tile, n)],
        out_specs=_full((D_MODEL, n)),
        out_shape=jax.ShapeDtypeStruct((D_MODEL, n), F32),
        compiler_params=_params(dimension_semantics=("arbitrary",)),
    )(u, dp)


def _pad_lora(w_up, top):
    z = jnp.zeros_like(w_up)
    return jnp.concatenate([w_up, z] if top else [z, w_up], axis=0)


def _local_grads(x, target, norm_g, w_in_bf, ret_gn_g, mu, w_lora_up, w0, a_lora_up, a0, k_k, k_a, r_k, gn_g, gn_b,
                 w_out_bf, final_g):
    T = x.shape[0]
    tabs = _rope_tables(T) + _ret_tables()
    ones64 = _block_mix(128, RWKV_HEAD)
    avg64 = _block_mix(128, RWKV_HEAD, 1.0 / RWKV_HEAD)
    avg128 = _block_mix(128, RET_DV, 1.0 / RET_DV)
    wup_pad, aup_pad = _pad_lora(w_lora_up, True), _pad_lora(a_lora_up, False)

    p_ret, p_rwkv, u = _inproj(x, norm_g, w_in_bf)
    ret_raw, s_saved = _ret_fwd(p_ret, tabs)
    r, w, k, v, kap, a, g_rw = _prep_fwd(p_rwkv, mu, w0, a0, k_k, k_a, wup_pad, aup_pad, ones64)
    o, s_all, sa_all = _wkv_fwd(r, w, k, v, kap, a)
    (dh, do, dr2, dk2, dv2, dgrw, dret, dgret, d_w_out, d_ret_gn_g, d_gn_g, d_gn_b, d_r_k, d_final_g, loss) = _post(
        o, r, k, v, g_rw, ret_raw, p_ret, x, target, ret_gn_g, gn_g, gn_b, r_k, final_g, w_out_bf, avg128, avg64, ones64)
    dr, dw, dk, dv, dkap, da = _wkv_bwd(r, w, k, v, kap, a, s_all, sa_all, do)
    dp_rwkv, d_mu, d_w0, d_a0, d_k_k, d_k_a, d_wup, d_aup = _prep_bwd(
        p_rwkv, mu, w0, a0, k_k, k_a, wup_pad, aup_pad, ones64, dr, dw, dk, dv, dkap, da, dgrw, dr2, dk2, dv2)
    dp_qkv = _ret_bwd(p_ret, s_saved, dret, tabs)
    dx, d_norm_g = _inproj_bwd_x(x, norm_g, dp_qkv, dgret, dp_rwkv, dh, w_in_bf)
    d_w_in = jnp.concatenate([_grad_w("grad_w_qkv", u, dp_qkv), _grad_w("grad_w_gret", u, dgret),
                              _grad_w("grad_w_rwkv", u, dp_rwkv)], axis=1)
    grads = dict(norm_g=d_norm_g, w_in=d_w_in, ret_gn_g=d_ret_gn_g, rwkv_mu=d_mu, w_lora_up=d_wup[:LORA], w0=d_w0,
                 a_lora_up=d_aup[LORA:], a0=d_a0, k_k=d_k_k, k_a=d_k_a, r_k=d_r_k, rwkv_gn_g=d_gn_g, rwkv_gn_b=d_gn_b,
                 w_out=d_w_out, final_norm_g=d_final_g)
    return loss, dx, grads


def _mesh_pos():
    return lax.axis_index("x"), lax.axis_index("y"), lax.axis_index("c")


def _all_gather(shards):
    n = len(shards)

    def body(*refs):
        x_refs, out_refs = refs[:n], refs[n:2 * n]
        send_sems, recv_sems, local_sems = refs[2 * n:]
        x, y, c = _mesh_pos()
        me, sibling = (x, y, c), (x, y, 1 - c)
        chips = [(1 - x, y), (x, 1 - y), (1 - x, 1 - y)]

        def rows(a, pos):
            m = x_refs[a].shape[0]
            return out_refs[a].at[pl.ds((4 * pos[0] + 2 * pos[1] + pos[2]) * m, m), :]

        def copy(a, k, block, to, src=None):
            return pltpu.make_async_remote_copy(
                src_ref=rows(a, block) if src is None else src, dst_ref=rows(a, block),
                send_sem=send_sems.at[a, k], recv_sem=recv_sems.at[a, k], device_id=to, device_id_type=MESH)

        mine = [pltpu.make_async_copy(x_refs[a], rows(a, me), local_sems.at[a]) for a in range(n)]
        for cp in mine:
            cp.start()
        first = []
        for a in range(n):
            first.append(copy(a, 0, me, sibling, src=x_refs[a]))
            first += [copy(a, 1 + j, me, (*chip, c), src=x_refs[a]) for j, chip in enumerate(chips)]
        for cp in first:
            cp.start()
        passed = []
        for j, chip in enumerate(chips):
            for a in range(n):
                copy(a, 1 + j, (*chip, c), me).wait_recv()
                passed.append(copy(a, 4 + j, (*chip, c), sibling))
                passed[-1].start()
        for a in range(n):
            copy(a, 0, sibling, me).wait_recv()
            for j, chip in enumerate(chips):
                copy(a, 4 + j, (*chip, 1 - c), me).wait_recv()
        for cp in first + passed:
            cp.wait_send()
        for cp in mine:
            cp.wait()

    vmem = pl.BlockSpec(memory_space=pltpu.VMEM)
    return pl.pallas_call(
        body, name="gather_weights",
        out_shape=[jax.ShapeDtypeStruct((N_DEV * s.shape[0], s.shape[1]), s.dtype) for s in shards],
        in_specs=[vmem] * n, out_specs=[vmem] * n,
        scratch_shapes=[pltpu.SemaphoreType.DMA((n, 7)), pltpu.SemaphoreType.DMA((n, 7)), pltpu.SemaphoreType.DMA((n,))],
        compiler_params=_params(),
    )(*shards)


N_CHIP = 4


def _exchange_pairs(big, small):
    nb, ns = len(big), len(small)

    def body(*refs):
        big_in, small_in = refs[:nb], refs[nb:nb + ns]
        theirs, small_out = refs[nb + ns:2 * nb + ns], refs[2 * nb + ns:2 * nb + 2 * ns]
        pair_send, pair_recv, send_sems, recv_sems, local_sems = refs[2 * nb + 2 * ns:]
        x, y, c = _mesh_pos()
        me = 4 * x + 2 * y + c
        local = [pltpu.make_async_copy(small_in[a].at[me], small_out[a].at[me], local_sems.at[a]) for a in range(ns)]
        for cp in local:
            cp.start()
        copies = [pltpu.make_async_remote_copy(
            src_ref=big_in[a], dst_ref=theirs[a], send_sem=pair_send.at[a], recv_sem=pair_recv.at[a],
            device_id=(x, y, 1 - c), device_id_type=MESH) for a in range(nb)]
        for k in range(1, N_DEV):
            peer = (x ^ (k >> 2), y ^ ((k >> 1) & 1), c ^ (k & 1))
            peer_idx = 4 * peer[0] + 2 * peer[1] + peer[2]
            copies += [pltpu.make_async_remote_copy(
                src_ref=small_in[a].at[peer_idx], dst_ref=small_out[a].at[me], send_sem=send_sems.at[a, k - 1],
                recv_sem=recv_sems.at[a, k - 1], device_id=peer, device_id_type=MESH) for a in range(ns)]
        for cp in copies:
            cp.start()
        for cp in copies:
            cp.wait()
        for cp in local:
            cp.wait()

    hbm = pl.BlockSpec(memory_space=pl.ANY)
    out_shape = [jax.ShapeDtypeStruct(p.shape, p.dtype) for p in big + small]
    dma = pltpu.SemaphoreType.DMA
    res = pl.pallas_call(
        body, name="exchange_pairs", out_shape=out_shape,
        in_specs=[hbm] * (nb + ns), out_specs=[hbm] * len(out_shape),
        scratch_shapes=[dma((nb,)), dma((nb,)), dma((ns, 7)), dma((ns, 7)), dma((ns,))],
        compiler_params=_params(),
    )(*big, *small)
    return res[:nb], res[nb:]


def _pair_sum(name, mine, theirs, row_tile):
    _, rows, cols = mine.shape

    def body(a_ref, b_ref, o_ref):
        o_ref[...] = (a_ref[...].astype(F32) + b_ref[...].astype(F32)).astype(o_ref.dtype)

    spec = pl.BlockSpec((N_CHIP, row_tile, cols), lambda i: (0, i, 0))
    return pl.pallas_call(
        body, name=name, grid=(rows // row_tile,), in_specs=[spec, spec], out_specs=spec,
        out_shape=jax.ShapeDtypeStruct(mine.shape, mine.dtype),
        compiler_params=_params(dimension_semantics=("arbitrary",)),
    )(mine, theirs)


def _exchange_chips(parts):
    n = len(parts)

    def body(*refs):
        in_refs, out_refs = refs[:n], refs[n:2 * n]
        send_sems, recv_sems, local_sems = refs[2 * n:]
        x, y, c = _mesh_pos()
        my_chip = 2 * x + y
        own = [pltpu.make_async_copy(in_refs[a].at[my_chip], out_refs[a].at[my_chip], local_sems.at[a]) for a in range(n)]
        for cp in own:
            cp.start()
        copies = []
        for k in range(1, N_CHIP):
            px, py = x ^ (k >> 1), y ^ (k & 1)
            copies += [pltpu.make_async_remote_copy(
                src_ref=in_refs[a].at[2 * px + py], dst_ref=out_refs[a].at[my_chip], send_sem=send_sems.at[a, k - 1],
                recv_sem=recv_sems.at[a, k - 1], device_id=(px, py, c), device_id_type=MESH) for a in range(n)]
        for cp in copies:
            cp.start()
        for cp in copies:
            cp.wait()
        for cp in own:
            cp.wait()

    hbm = pl.BlockSpec(memory_space=pl.ANY)
    dma = pltpu.SemaphoreType.DMA
    return pl.pallas_call(
        body, name="exchange_chips",
        out_shape=[jax.ShapeDtypeStruct(p.shape, p.dtype) for p in parts],
        in_specs=[hbm] * n, out_specs=[hbm] * n,
        scratch_shapes=[dma((n, N_CHIP - 1)), dma((n, N_CHIP - 1)), dma((n,))],
        compiler_params=_params(),
    )(*parts)


def _adamw(w, g, m, v):
    m = ADAM_B1 * m + (1.0 - ADAM_B1) * g
    v = ADAM_B2 * v + (1.0 - ADAM_B2) * (g * g)
    m_hat = m / (1.0 - ADAM_B1 ** ADAM_STEP)
    v_hat = v / (1.0 - ADAM_B2 ** ADAM_STEP)
    return -ADAM_LR * (m_hat / (jnp.sqrt(v_hat) + ADAM_EPS) + ADAM_WD * w), m, v


def _reduce_adamw(name, parts, w, m, v, row_tile):
    n_parts, rows, cols = parts.shape

    def body(p_ref, w_ref, m_ref, v_ref, g_ref, d_ref, nm_ref, nv_ref):
        g = p_ref[0].astype(F32)
        for s in range(1, n_parts):
            g = g + p_ref[s].astype(F32)
        g_ref[...] = g
        d_ref[...], nm_ref[...], nv_ref[...] = _adamw(w_ref[...], g, m_ref[...], v_ref[...])

    tile = pl.BlockSpec((row_tile, cols), lambda i: (i, 0))
    return pl.pallas_call(
        body, name=name, grid=(rows // row_tile,),
        in_specs=[pl.BlockSpec((n_parts, row_tile, cols), lambda i: (0, i, 0)), tile, tile, tile],
        out_specs=[tile] * 4,
        out_shape=[jax.ShapeDtypeStruct((rows, cols), F32)] * 4,
        compiler_params=_params(dimension_semantics=("arbitrary",)),
    )(parts, w, m, v)


_SMALL = (("norm_g", 1024), ("ret_gn_g", 512), ("rwkv_mu", 2176), ("w0", 512), ("a0", 512), ("k_k", 512), ("k_a", 512),
          ("r_k", 512), ("rwkv_gn_g", 512), ("rwkv_gn_b", 512), ("final_norm_g", 1024))
_SMALL_LANES = sum(n for _, n in _SMALL) + 128
_WEIGHTS = ("norm_g", "w_in", "ret_gn_g", "rwkv_mu", "w_lora_up", "w0", "a_lora_up", "a0", "k_k", "k_a", "r_k", "rwkv_gn_g",
            "rwkv_gn_b", "w_out", "final_norm_g")


def _adamw_vectors(parts, wts, mom, var):
    k = len(_SMALL)

    def body(p_ref, *refs):
        w_refs, m_refs, v_refs, outs = refs[:k], refs[k:2 * k], refs[2 * k:3 * k], refs[3 * k:]
        g_all = p_ref[0]
        for s in range(1, N_DEV):
            g_all = g_all + p_ref[s]
        off = 0
        for i, (_, n) in enumerate(_SMALL):
            g = g_all[:, off:off + n]
            off += n
            outs[4 * i][...] = g
            outs[4 * i + 1][...], outs[4 * i + 2][...], outs[4 * i + 3][...] = _adamw(
                w_refs[i][...], g, m_refs[i][...], v_refs[i][...])
        outs[4 * k][...] = g_all[:, off:off + 128]

    vmem = pl.BlockSpec(memory_space=pltpu.VMEM)
    shapes = [jax.ShapeDtypeStruct((1, n), F32) for _, n in _SMALL for _ in range(4)] + [jax.ShapeDtypeStruct((1, 128), F32)]
    res = pl.pallas_call(
        body, name="adamw_vectors", out_shape=shapes,
        in_specs=[vmem] * (1 + 3 * k), out_specs=[vmem] * len(shapes), compiler_params=_params(),
    )(parts, *[wts[n] for n, _ in _SMALL], *[mom[n] for n, _ in _SMALL], *[var[n] for n, _ in _SMALL])
    return {n: res[4 * i:4 * i + 4] for i, (n, _) in enumerate(_SMALL)}, res[4 * k]


def kernel(x, norm_g, w_in, ret_gn_g, rwkv_mu, w_lora_up, w0, a_lora_up, a0, k_k, k_a, r_k, rwkv_gn_g, rwkv_gn_b, w_out, final_norm_g, loss_target, m_norm_g, m_w_in, m_ret_gn_g, m_rwkv_mu, m_w_lora_up, m_w0, m_a_lora_up, m_a0, m_k_k, m_k_a, m_r_k, m_rwkv_gn_g, m_rwkv_gn_b, m_w_out, m_final_norm_g, v_norm_g, v_w_in, v_ret_gn_g, v_rwkv_mu, v_w_lora_up, v_w0, v_a_lora_up, v_a0, v_k_k, v_k_a, v_r_k, v_rwkv_gn_g, v_rwkv_gn_b, v_w_out, v_final_norm_g):
    wts = dict(norm_g=norm_g, w_in=w_in, ret_gn_g=ret_gn_g, rwkv_mu=rwkv_mu, w_lora_up=w_lora_up, w0=w0, a_lora_up=a_lora_up,
               a0=a0, k_k=k_k, k_a=k_a, r_k=r_k, rwkv_gn_g=rwkv_gn_g, rwkv_gn_b=rwkv_gn_b, w_out=w_out,
               final_norm_g=final_norm_g)
    mom = dict(norm_g=m_norm_g, w_in=m_w_in, ret_gn_g=m_ret_gn_g, rwkv_mu=m_rwkv_mu, w_lora_up=m_w_lora_up, w0=m_w0,
               a_lora_up=m_a_lora_up, a0=m_a0, k_k=m_k_k, k_a=m_k_a, r_k=m_r_k, rwkv_gn_g=m_rwkv_gn_g,
               rwkv_gn_b=m_rwkv_gn_b, w_out=m_w_out, final_norm_g=m_final_norm_g)
    var = dict(norm_g=v_norm_g, w_in=v_w_in, ret_gn_g=v_ret_gn_g, rwkv_mu=v_rwkv_mu, w_lora_up=v_w_lora_up, w0=v_w0,
               a_lora_up=v_a_lora_up, a0=v_a0, k_k=v_k_k, k_a=v_k_a, r_k=v_r_k, rwkv_gn_g=v_rwkv_gn_g,
               rwkv_gn_b=v_rwkv_gn_b, w_out=v_w_out, final_norm_g=v_final_norm_g)
    shapes = {n: wts[n].shape for n in _WEIGHTS}

    g_in, g_out, g_wup, g_aup = _all_gather(
        [w_in[0].astype(BF16), w_out[0].astype(BF16), w_lora_up[0], a_lora_up[0]])
    w_in_bf = g_in.reshape(N_DEV, D_MODEL, SHARD_IN).transpose(1, 0, 2).reshape(D_MODEL, IN_COLS)
    wup_full = g_wup.reshape(N_DEV, LORA, SHARD_LORA).transpose(1, 0, 2).reshape(LORA, W)
    aup_full = g_aup.reshape(N_DEV, LORA, SHARD_LORA).transpose(1, 0, 2).reshape(LORA, W)

    loss, dx, g = _local_grads(
        x[0], loss_target[0], norm_g, w_in_bf, ret_gn_g, rwkv_mu, wup_full, w0, aup_full, a0, k_k, k_a,
        r_k.reshape(1, W), rwkv_gn_g, rwkv_gn_b, g_out, final_norm_g.reshape(1, D_MODEL))

    small = jnp.concatenate([g[n] for n, _ in _SMALL] + [loss], axis=1)
    core = lax.axis_index("c")
    by_core = lambda t: [lax.dynamic_index_in_dim(t, i, axis=1, keepdims=False) for i in (core, 1 - core)]
    in_mine, in_sib = by_core(g["w_in"].reshape(D_MODEL, N_CHIP, 2, SHARD_IN).transpose(1, 2, 0, 3).astype(BF16))
    out_mine, out_sib = by_core(g["w_out"].reshape(N_CHIP, 2, SHARD_OUT, D_MODEL).astype(BF16))
    (in_theirs, out_theirs), parts = _exchange_pairs(
        [in_sib, out_sib],
        [g["w_lora_up"].reshape(LORA, N_DEV, SHARD_LORA).transpose(1, 0, 2),
         g["a_lora_up"].reshape(LORA, N_DEV, SHARD_LORA).transpose(1, 0, 2),
         jnp.broadcast_to(small[None], (N_DEV, 1, _SMALL_LANES))])
    by_chip = _exchange_chips([_pair_sum("pair_sum_w_in", in_mine, in_theirs, 256),
                               _pair_sum("pair_sum_w_out", out_mine, out_theirs, SHARD_OUT)])
    res = {}
    res["w_in"] = _reduce_adamw("adamw_w_in", by_chip[0], w_in[0], m_w_in[0], v_w_in[0], 256)
    res["w_out"] = _reduce_adamw("adamw_w_out", by_chip[1], w_out[0], m_w_out[0], v_w_out[0], SHARD_OUT)
    res["w_lora_up"] = _reduce_adamw("adamw_w_lora_up", parts[0], w_lora_up[0], m_w_lora_up[0], v_w_lora_up[0], LORA)
    res["a_lora_up"] = _reduce_adamw("adamw_a_lora_up", parts[1], a_lora_up[0], m_a_lora_up[0], v_a_lora_up[0], LORA)
    as_row = lambda d: {n: d[n].reshape(1, size) for n, size in _SMALL}
    vec, loss_row = _adamw_vectors(parts[2], as_row(wts), as_row(mom), as_row(var))
    res.update(vec)
    res = {n: [t.reshape(shapes[n]) for t in res[n]] for n in _WEIGHTS}
    return (loss_row[0, 0], dx[None], *[res[n][0] for n in _WEIGHTS], *[res[n][1] for n in _WEIGHTS],
            *[res[n][2] for n in _WEIGHTS], *[res[n][3] for n in _WEIGHTS])
```

```python
import functools

import numpy as np
import jax
import jax.numpy as jnp
from jax import lax
from jax.experimental import pallas as pl
from jax.experimental.pallas import tpu as pltpu

F32 = jnp.float32
BF16 = jnp.bfloat16

D_MODEL = 1024
CHUNK = 64
RET_HEADS = 4
RET_DV = 128
RET_DK = 64
RET_QK = 256
RET_WIDTH = 512
RWKV_WIDTH = 512
RWKV_HEAD = 64
RWKV_HEADS = 8
LORA = 64
RET_COLS = 2 * RET_QK + 2 * RET_WIDTH
RWKV_COLS = 4 * RWKV_WIDTH + 2 * LORA
IN_COLS = RET_COLS + RWKV_COLS
ROPE_BASE = 10000.0
RMS_EPS = 1e-6
RET_GN_EPS = 1e-5
RWKV_GN_EPS = 64e-5
ADAM_LR = 0.001
ADAM_B1 = 0.9
ADAM_B2 = 0.999
ADAM_EPS = 1e-08
ADAM_WD = 0.01
ADAM_STEP = 10
N_DEV = 8
SHARD_IN = IN_COLS // N_DEV
SHARD_OUT = D_MODEL // N_DEV
SHARD_LORA = RWKV_WIDTH // N_DEV
VMEM_LIMIT = 56 * 1024 * 1024
TOK_TILE = 256
WKV_CHUNK = 64

MESH = pl.DeviceIdType.MESH


def _dot_bf(a, b):
    return jnp.dot(a.astype(BF16), b.astype(BF16), preferred_element_type=F32)


def _dot_nt_bf(a, b):
    return lax.dot_general(a.astype(BF16), b.astype(BF16), (((1,), (1,)), ((), ())), preferred_element_type=F32)


def _dot_tn_bf(a, b):
    return lax.dot_general(a.astype(BF16), b.astype(BF16), (((0,), (0,)), ((), ())), preferred_element_type=F32)


@jax.custom_vjp
def _mm(a, b):
    return _dot_bf(a, b)


@jax.custom_vjp
def _mm_nt(a, b):
    return _dot_nt_bf(a, b)


@jax.custom_vjp
def _mm_tn(a, b):
    return _dot_tn_bf(a, b)


_mm.defvjp(lambda a, b: (_dot_bf(a, b), (a, b)), lambda res, g: (_dot_nt_bf(g, res[1]), _dot_tn_bf(res[0], g)))
_mm_nt.defvjp(lambda a, b: (_dot_nt_bf(a, b), (a, b)), lambda res, g: (_dot_bf(g, res[1]), _dot_tn_bf(g, res[0])))
_mm_tn.defvjp(lambda a, b: (_dot_tn_bf(a, b), (a, b)), lambda res, g: (_dot_nt_bf(res[1], g), _dot_bf(res[0], g)))


def _trunc(x):
    return lax.bitcast_convert_type(lax.bitcast_convert_type(x, jnp.uint32) & jnp.uint32(0xFFFF0000), F32)


def _two_piece(x):
    hi = _trunc(x)
    return jnp.concatenate([hi, x - hi], axis=1)


def _mix_raw(x, mat2):
    return _unstack(jnp.dot(_two_piece(_stack(x)), mat2, preferred_element_type=F32))


@jax.custom_vjp
def _head_mix(x, mat2):
    return _mix_raw(x, mat2)


_head_mix.defvjp(lambda x, mat2: (_mix_raw(x, mat2), mat2), lambda mat2, g: (_mix_raw(g, mat2), jnp.zeros_like(mat2)))


def _swap_halves(x):
    lane = lax.broadcasted_iota(jnp.int32, x.shape, 1)
    return jnp.where((lane & (RET_DK - 1)) < RET_DK // 2, pltpu.roll(x, RET_QK - RET_DK // 2, axis=1),
                     pltpu.roll(x, RET_DK // 2, axis=1))


@jax.custom_vjp
def _rot(x):
    return _swap_halves(x)


_rot.defvjp(lambda x: (_swap_halves(x), None), lambda _, g: (_swap_halves(g),))


def _params(**kw):
    return pltpu.CompilerParams(vmem_limit_bytes=VMEM_LIMIT, **kw)


def _full(shape):
    nd = len(shape)
    return pl.BlockSpec(shape, lambda i, _nd=nd: (0,) * _nd)


def _rows(tile, width):
    return pl.BlockSpec((tile, width), lambda i: (i, 0))


def _block_mix(n, blk, scale=1.0):
    idx = np.arange(n) // blk
    m = (idx[:, None] == idx[None, :]).astype(np.float32) * scale
    return jnp.asarray(np.concatenate([m, m], axis=0))


def _rope_tables(T):
    half = RET_DK // 2
    expo = -np.arange(half, dtype=np.float32) / np.float32(half)
    freqs = np.exp(expo * np.float32(np.log(ROPE_BASE))).astype(np.float32)
    ang = np.arange(T, dtype=np.float32)[:, None] * freqs[None, :]
    cos, sin = np.cos(ang).astype(np.float32), np.sin(ang).astype(np.float32)
    cos_h = np.concatenate([cos, cos], axis=1)
    sin_h = np.concatenate([-sin, sin], axis=1)
    cos_t = np.tile(cos_h, (1, RET_HEADS))
    sin_t = np.tile(sin_h, (1, RET_HEADS))
    return jnp.asarray(cos_t), jnp.asarray(sin_t)


def _ret_tables():
    h = np.arange(RET_HEADS, dtype=np.float32)
    lg = np.log(1.0 - np.exp2(-5.0 - h)).astype(np.float32)
    idx = np.arange(CHUNK, dtype=np.float32)
    intra = np.exp(lg[:, None, None] * np.abs(idx[:, None] - idx[None, :])).astype(np.float32)
    q_dec = np.exp(lg[:, None] * (idx[None, :] + 1.0)).astype(np.float32)
    k_dec = np.exp(lg[:, None] * (CHUNK - 1.0 - idx[None, :])).astype(np.float32)
    chunk_dec = np.exp(lg * CHUNK).astype(np.float32)
    lane_head = np.arange(RET_QK) // RET_DK
    mask = (lane_head[None, :] == np.arange(RET_HEADS)[:, None]).astype(np.float32)
    m = np.broadcast_to(mask[:, None, :], (RET_HEADS, CHUNK, RET_QK)).copy()
    qd = m * q_dec[:, :, None]
    kd = m * k_dec[:, :, None]
    return jnp.asarray(intra), jnp.asarray(m), jnp.asarray(qd), jnp.asarray(kd), [float(c) for c in chunk_dec]


def _rmsnorm(x, g):
    return x * lax.rsqrt(jnp.mean(x * x, axis=-1, keepdims=True) + RMS_EPS) * g


def _inproj(x, norm_g, w_in):
    T = x.shape[0]

    def body(x_ref, g_ref, w_ref, pr_ref, pw_ref, u_ref):
        ub = _rmsnorm(x_ref[...], g_ref[...]).astype(BF16)
        u_ref[...] = ub
        pr_ref[...] = jnp.dot(ub, w_ref[:, :RET_COLS], preferred_element_type=F32)
        pw_ref[...] = jnp.dot(ub, w_ref[:, RET_COLS:], preferred_element_type=F32)

    return pl.pallas_call(
        body, name="inproj", grid=(T // TOK_TILE,),
        in_specs=[_rows(TOK_TILE, D_MODEL), _full((1, D_MODEL)), _full((D_MODEL, IN_COLS))],
        out_specs=[_rows(TOK_TILE, RET_COLS), _rows(TOK_TILE, RWKV_COLS), _rows(TOK_TILE, D_MODEL)],
        out_shape=[jax.ShapeDtypeStruct((T, RET_COLS), F32), jax.ShapeDtypeStruct((T, RWKV_COLS), F32),
                   jax.ShapeDtypeStruct((T, D_MODEL), BF16)],
        compiler_params=_params(dimension_semantics=("arbitrary",)),
    )(x, norm_g, w_in)


def _ret_chunk(pq, pk, v_heads, s_heads, cos_t, sin_t, dec, hm, qd, kd, chunk_dec):
    q = pq * cos_t + _rot(pq) * sin_t
    k = (pk * cos_t + _rot(pk) * sin_t) * (RET_DK ** -0.5)
    outs, s_out = [], []
    for h in range(RET_HEADS):
        sc = _mm_nt(q * hm[h], k * hm[h]) * dec[h]
        intra = _mm(sc, v_heads[h])
        kv = _mm_tn(k * kd[h], v_heads[h])
        inter = _mm(q * qd[h], s_heads[h])
        outs.append(intra + inter)
        s_out.append(s_heads[h] * chunk_dec[h] + kv)
    return tuple(outs), tuple(s_out)


def _ret_specs():
    const = [_full((RET_HEADS, CHUNK, CHUNK)), _full((RET_HEADS, CHUNK, RET_QK)),
             _full((RET_HEADS, CHUNK, RET_QK)), _full((RET_HEADS, CHUNK, RET_QK))]
    return const


RET_GROUP = 4


def _ret_fwd(p_ret, tabs):
    T = p_ret.shape[0]
    G = RET_GROUP
    ng = T // (CHUNK * G)
    cos_t, sin_t, dec, hm, qd, kd, chunk_dec = tabs

    def body(p_ref, cos_ref, sin_ref, dec_ref, hm_ref, qd_ref, kd_ref, out_ref, sin_save_ref, s_scr):
        @pl.when(pl.program_id(0) == 0)
        def _():
            s_scr[...] = jnp.zeros_like(s_scr)

        consts = (dec_ref[...], hm_ref[...], qd_ref[...], kd_ref[...])
        s_heads = tuple(s_scr[h] for h in range(RET_HEADS))
        for c in range(G):
            rows = pl.ds(c * CHUNK, CHUNK)
            for h in range(RET_HEADS):
                sin_save_ref[c, h] = s_heads[h]
            v_heads = tuple(p_ref[rows, 2 * RET_QK + RET_DV * h:2 * RET_QK + RET_DV * (h + 1)] for h in range(RET_HEADS))
            outs, s_heads = _ret_chunk(p_ref[rows, 0:RET_QK], p_ref[rows, RET_QK:2 * RET_QK], v_heads, s_heads,
                                       cos_ref[rows, :], sin_ref[rows, :], *consts, chunk_dec)
            for h in range(RET_HEADS):
                out_ref[rows, RET_DV * h:RET_DV * (h + 1)] = outs[h]
        for h in range(RET_HEADS):
            s_scr[h] = s_heads[h]

    tok = CHUNK * G
    return pl.pallas_call(
        body, name="ret_fwd", grid=(ng,),
        in_specs=[pl.BlockSpec((tok, RET_COLS), lambda i: (i, 0)), _rows(tok, RET_QK), _rows(tok, RET_QK)] + _ret_specs(),
        out_specs=[_rows(tok, RET_WIDTH), pl.BlockSpec((G, RET_HEADS, RET_QK, RET_DV), lambda i: (i, 0, 0, 0))],
        out_shape=[jax.ShapeDtypeStruct((T, RET_WIDTH), F32),
                   jax.ShapeDtypeStruct((T // CHUNK, RET_HEADS, RET_QK, RET_DV), F32)],
        scratch_shapes=[pltpu.VMEM((RET_HEADS, RET_QK, RET_DV), F32)],
        compiler_params=_params(dimension_semantics=("arbitrary",)),
    )(p_ret, cos_t, sin_t, dec, hm, qd, kd)


def _ret_bwd(p_ret, s_saved, d_ret, tabs):
    T = p_ret.shape[0]
    G = RET_GROUP
    ng = T // (CHUNK * G)
    cos_t, sin_t, dec, hm, qd, kd, chunk_dec = tabs

    def body(p_ref, s_ref, dret_ref, cos_ref, sin_ref, dec_ref, hm_ref, qd_ref, kd_ref, dp_ref, ds_scr):
        @pl.when(pl.program_id(0) == 0)
        def _():
            ds_scr[...] = jnp.zeros_like(ds_scr)

        consts = (dec_ref[...], hm_ref[...], qd_ref[...], kd_ref[...])
        d_s = tuple(ds_scr[h] for h in range(RET_HEADS))
        for c in reversed(range(G)):
            rows = pl.ds(c * CHUNK, CHUNK)
            v_heads = tuple(p_ref[rows, 2 * RET_QK + RET_DV * h:2 * RET_QK + RET_DV * (h + 1)] for h in range(RET_HEADS))
            s_heads = tuple(s_ref[c, h] for h in range(RET_HEADS))
            tables = (cos_ref[rows, :], sin_ref[rows, :]) + consts
            _, vjp = jax.vjp(lambda a, b, c_, d: _ret_chunk(a, b, c_, d, *tables, chunk_dec),
                             p_ref[rows, 0:RET_QK], p_ref[rows, RET_QK:2 * RET_QK], v_heads, s_heads)
            d_out = tuple(dret_ref[rows, RET_DV * h:RET_DV * (h + 1)] for h in range(RET_HEADS))
            dq, dk, dv, d_s = vjp((d_out, d_s))
            dp_ref[rows, 0:RET_QK] = dq
            dp_ref[rows, RET_QK:2 * RET_QK] = dk
            for h in range(RET_HEADS):
                dp_ref[rows, 2 * RET_QK + RET_DV * h:2 * RET_QK + RET_DV * (h + 1)] = dv[h]
        for h in range(RET_HEADS):
            ds_scr[h] = d_s[h]

    tok = CHUNK * G
    rev = lambda i: (ng - 1 - i, 0)
    return pl.pallas_call(
        body, name="ret_bwd", grid=(ng,),
        in_specs=[pl.BlockSpec((tok, RET_COLS), rev),
                  pl.BlockSpec((G, RET_HEADS, RET_QK, RET_DV), lambda i: (ng - 1 - i, 0, 0, 0)),
                  pl.BlockSpec((tok, RET_WIDTH), rev), pl.BlockSpec((tok, RET_QK), rev), pl.BlockSpec((tok, RET_QK), rev)]
        + _ret_specs(),
        out_specs=pl.BlockSpec((tok, 2 * RET_QK + RET_WIDTH), rev),
        out_shape=jax.ShapeDtypeStruct((T, 2 * RET_QK + RET_WIDTH), F32),
        scratch_shapes=[pltpu.VMEM((RET_HEADS, RET_QK, RET_DV), F32)],
        compiler_params=_params(dimension_semantics=("arbitrary",)),
    )(p_ret, s_saved, d_ret, cos_t, sin_t, dec, hm, qd, kd)


def _wkv_consts():
    lane = lax.broadcasted_iota(jnp.int32, (RWKV_HEAD, RWKV_WIDTH), 1)
    sub = lax.broadcasted_iota(jnp.int32, (RWKV_HEAD, RWKV_WIDTH), 0)
    diag = ((lane & (RWKV_HEAD - 1)) == sub).astype(F32)
    r = lax.broadcasted_iota(jnp.int32, (3 * 128, 128), 0)
    c = lax.broadcasted_iota(jnp.int32, (3 * 128, 128), 1)
    ones = (((r & 127) >> 6) == (c >> 6)).astype(BF16)
    return diag, ones


def _stack(x):
    return jnp.concatenate([x[:, 128 * p:128 * (p + 1)] for p in range(4)], axis=0)


def _unstack(y):
    n = y.shape[0] // 4
    return jnp.concatenate([y[n * p:n * (p + 1)] for p in range(4)], axis=1)


def _split(x, n):
    pieces = []
    for _ in range(n):
        p = x.astype(BF16)
        pieces.append(p)
        x = x - p.astype(F32)
    return pieces


def _lane_sum(x, ones):
    return _unstack(jnp.dot(_two_piece(_stack(x)), ones[:256].astype(F32), preferred_element_type=F32))


def _lane_sum_bf(x, ones):
    return _unstack(jnp.dot(_stack(x).astype(BF16), ones[:128], preferred_element_type=F32))


def _colsum(x):
    return jnp.sum(x, axis=0, keepdims=True)


def _expand_cols(xt, t):
    lane = lax.broadcasted_iota(jnp.int32, (RWKV_HEAD, 128), 1)
    tiles = []
    for p in range(4):
        lo = jnp.broadcast_to(xt[128 * p:128 * p + RWKV_HEAD, t:t + 1], (RWKV_HEAD, 128))
        hi = jnp.broadcast_to(xt[128 * p + RWKV_HEAD:128 * (p + 1), t:t + 1], (RWKV_HEAD, 128))
        tiles.append(jnp.where(lane < RWKV_HEAD, lo, hi))
    return jnp.concatenate(tiles, axis=1)


def _head_sums(x, ones):
    return _unstack(jnp.dot(jnp.concatenate(_split(_stack(x), 3), axis=1), ones, preferred_element_type=F32))


def _wkv_fwd(r, w, k, v, kap, a):
    T = r.shape[0]
    C = WKV_CHUNK
    nc = T // C

    def body(r_ref, w_ref, k_ref, v_ref, kap_ref, a_ref, o_ref, s_all_ref, sa_all_ref, s_scr):
        @pl.when(pl.program_id(0) == 0)
        def _():
            s_scr[...] = jnp.zeros_like(s_scr)

        diag, ones = _wkv_consts()
        rr, ww, kk, vv, kap_, aa = (ref[...] for ref in (r_ref, w_ref, k_ref, v_ref, kap_ref, a_ref))
        bb = kap_ * aa
        c1 = _head_sums(pltpu.roll(bb, 1, axis=0) * kap_, ones)
        row = lambda x, t: x[t:t + 1]

        v_cols = vv.T

        s_prev = s_scr[...]
        sa = _lane_sum(s_prev * (-row(kap_, 0)), ones)
        ls, rows = None, []

        def emit_o(t, s_t):
            rows.append(_colsum(_lane_sum_bf(s_t * row(rr, t), ones) * diag))
            if t % 8 == 7:
                o_ref[t - 7:t + 1, :] = jnp.concatenate(rows, axis=0)
                rows.clear()

        for t in range(C):
            u = s_prev * row(ww, t) + _expand_cols(v_cols, t) * row(kk, t)
            if t > 0:
                sa = ls - sa * row(c1, t)
            if t + 1 < C:
                ls = _lane_sum(u * (-row(kap_, t + 1)), ones)
            if t > 0:
                emit_o(t - 1, s_prev)
            s_prev = u + sa * row(bb, t)
            s_all_ref[t] = s_prev
            sa_all_ref[t] = sa.astype(BF16)
        emit_o(C - 1, s_prev)
        s_scr[...] = s_prev

    spec = _rows(C, RWKV_WIDTH)
    return pl.pallas_call(
        body, name="wkv_fwd", grid=(nc,),
        in_specs=[spec] * 6,
        out_specs=[spec, pl.BlockSpec((C, RWKV_HEAD, RWKV_WIDTH), lambda i: (i, 0, 0)),
                   pl.BlockSpec((C, RWKV_HEAD, RWKV_WIDTH), lambda i: (i, 0, 0))],
        out_shape=[jax.ShapeDtypeStruct((T, RWKV_WIDTH), F32), jax.ShapeDtypeStruct((T, RWKV_HEAD, RWKV_WIDTH), F32),
                   jax.ShapeDtypeStruct((T, RWKV_HEAD, RWKV_WIDTH), BF16)],
        scratch_shapes=[pltpu.VMEM((RWKV_HEAD, RWKV_WIDTH), F32)],
        compiler_params=_params(dimension_semantics=("arbitrary",)),
    )(r, w, k, v, kap, a)


def _wkv_bwd(r, w, k, v, kap, a, s_all, sa_all, d_o):
    T = r.shape[0]
    C = WKV_CHUNK
    nc = T // C

    def body(r_ref, w_ref, k_ref, v_ref, kap_ref, a_ref, s_ref, s_before_ref, sa_ref, do_ref,
             dr_ref, dw_ref, dk_ref, dv_ref, dkap_ref, da_ref, ds_scr):
        first_chunk = pl.program_id(0) == nc - 1

        @pl.when(pl.program_id(0) == 0)
        def _():
            ds_scr[...] = jnp.zeros_like(ds_scr)

        diag, ones = _wkv_consts()
        rr, ww, kk, vv, kap_, aa, dd = (ref[...] for ref in (r_ref, w_ref, k_ref, v_ref, kap_ref, a_ref, do_ref))
        bb = kap_ * aa
        e1 = _head_sums(pltpu.roll(kap_, C - 1, axis=0) * bb, ones)
        row = lambda x, t: x[t:t + 1]

        def state_before(t):
            return s_ref[t - 1] if t > 0 else jnp.where(first_chunk, 0.0, s_before_ref[0])

        v_cols, do_cols = vv.T, dd.T

        d_sn, dsa, rows = None, None, [None] * C

        def emit_rows(t, d_sn_t, dsa_t):
            s_prev, dof = state_before(t), _expand_cols(do_cols, t)
            dv = _colsum(_lane_sum_bf(d_sn_t * row(kk, t), ones) * diag)
            db = _colsum(d_sn_t * sa_ref[t].astype(F32))
            rows[t] = (_colsum(s_ref[t] * dof), _colsum(d_sn_t * s_prev), _colsum(d_sn_t * _expand_cols(v_cols, t)), dv,
                       db * row(aa, t) - _colsum(dsa_t * s_prev), db * row(kap_, t))
            if t % 8 == 0:
                for j, ref in enumerate((dr_ref, dw_ref, dk_ref, dv_ref, dkap_ref, da_ref)):
                    ref[t:t + 8, :] = jnp.concatenate([rows[u][j] for u in range(t, t + 8)], axis=0)

        for t in reversed(range(C)):
            dof = _expand_cols(do_cols, t)
            if t == C - 1:
                d_sn = ds_scr[...] + dof * row(rr, t)
                dsa = _lane_sum(d_sn * row(bb, t), ones)
            else:
                v_t = d_sn * row(ww, t + 1) + dof * row(rr, t)
                ls = _lane_sum(v_t * row(bb, t), ones)
                emit_rows(t + 1, d_sn, dsa)
                d_sn = v_t - dsa * row(kap_, t + 1)
                dsa = ls - dsa * row(e1, t)
        emit_rows(0, d_sn, dsa)
        d_s = d_sn * row(ww, 0) - dsa * row(kap_, 0)
        ds_scr[...] = d_s

    spec = pl.BlockSpec((C, RWKV_WIDTH), lambda i: (nc - 1 - i, 0))
    states = pl.BlockSpec((C, RWKV_HEAD, RWKV_WIDTH), lambda i: (nc - 1 - i, 0, 0))
    before = pl.BlockSpec((1, RWKV_HEAD, RWKV_WIDTH), lambda i: (jnp.maximum((nc - 1 - i) * C - 1, 0), 0, 0))
    return pl.pallas_call(
        body, name="wkv_bwd", grid=(nc,),
        in_specs=[spec] * 6 + [states, before, states, spec],
        out_specs=[spec] * 6,
        out_shape=[jax.ShapeDtypeStruct((T, RWKV_WIDTH), F32)] * 6,
        scratch_shapes=[pltpu.VMEM((RWKV_HEAD, RWKV_WIDTH), F32)],
        compiler_params=_params(dimension_semantics=("arbitrary",)),
    )(r, w, k, v, kap, a, s_all, s_all, sa_all, d_o)


W = RWKV_WIDTH


def _softplus(y):
    return jnp.maximum(y, 0.0) + jnp.log(1.0 + jnp.exp(-jnp.abs(y)))


def _prep_fn(kr, xwa, w0, a0, k_k, k_a, wup_pad, aup_pad, ones64):
    w_log = -_softplus(-(w0 + _mm(jnp.tanh(xwa), wup_pad))) - 0.5
    decay = jnp.exp(-jnp.exp(w_log))
    a = jax.nn.sigmoid(a0 + _mm(xwa, aup_pad))
    kk = kr * k_k
    kap = kk / jnp.maximum(jnp.sqrt(_head_mix(kk * kk, ones64)), 1e-12)
    k = kr * (1.0 + (a - 1.0) * k_a)
    return decay, k, kap, a


def _shift_down(p, first_row):
    rows = lax.broadcasted_iota(jnp.int32, p.shape, 0)
    return jnp.where(rows == 0, first_row, pltpu.roll(p, 1, axis=0))


def _shift_up(z, last_row):
    n = z.shape[0]
    rows = lax.broadcasted_iota(jnp.int32, z.shape, 0)
    return jnp.where(rows == n - 1, last_row, pltpu.roll(z, n - 1, axis=0))


def _prev_block_spec():
    return pl.BlockSpec((8, RWKV_COLS), lambda i: (jnp.maximum(i * (TOK_TILE // 8) - 1, 0), 0))


def _mixed(p_ref, prev8_ref, mu_ref, first_tile):
    p = p_ref[...]
    first_row = jnp.where(first_tile, 0.0, prev8_ref[7:8, :])
    prev = _shift_down(p, first_row)
    return p, prev, p + mu_ref[...] * (prev - p)


def _prep_fwd(p_rwkv, mu, w0, a0, k_k, k_a, wup_pad, aup_pad, ones64):
    T = p_rwkv.shape[0]

    def body(p_ref, prev8_ref, mu_ref, w0_ref, a0_ref, kk_ref, ka_ref, wup_ref, aup_ref, ones_ref,
             r_ref, w_ref, k_ref, v_ref, kap_ref, a_ref, g_ref):
        _, _, ps = _mixed(p_ref, prev8_ref, mu_ref, pl.program_id(0) == 0)
        decay, k, kap, a = _prep_fn(ps[:, W:2 * W], ps[:, 4 * W:], w0_ref[...], a0_ref[...], kk_ref[...], ka_ref[...],
                                    wup_ref[...], aup_ref[...], ones_ref[...])
        r_ref[...] = ps[:, 0:W]
        w_ref[...] = decay
        k_ref[...] = k
        v_ref[...] = ps[:, 2 * W:3 * W]
        kap_ref[...] = kap
        a_ref[...] = a
        g_ref[...] = ps[:, 3 * W:4 * W]

    vec = _full((1, W))
    return pl.pallas_call(
        body, name="prep_fwd", grid=(T // TOK_TILE,),
        in_specs=[_rows(TOK_TILE, RWKV_COLS), _prev_block_spec(), _full((1, RWKV_COLS)), vec, vec, vec, vec,
                  _full((2 * LORA, W)), _full((2 * LORA, W)), _full((256, 128))],
        out_specs=[_rows(TOK_TILE, W)] * 7,
        out_shape=[jax.ShapeDtypeStruct((T, W), F32)] * 7,
        compiler_params=_params(dimension_semantics=("arbitrary",)),
    )(p_rwkv, p_rwkv, mu, w0, a0, k_k, k_a, wup_pad, aup_pad, ones64)


def _prep_bwd(p_rwkv, mu, w0, a0, k_k, k_a, wup_pad, aup_pad, ones64, dr, dw, dk, dv, dkap, da, dg, dr2, dk2, dv2):
    T = p_rwkv.shape[0]
    nt = T // TOK_TILE

    def body(p_ref, prev8_ref, mu_ref, w0_ref, a0_ref, kk_ref, ka_ref, wup_ref, aup_ref, ones_ref,
             dr_ref, dw_ref, dk_ref, dv_ref, dkap_ref, da_ref, dg_ref, dr2_ref, dk2_ref, dv2_ref,
             dp_ref, dmu_ref, dw0_ref, da0_ref, dkk_ref, dka_ref, dwup_ref, daup_ref, zrow_scr):
        i = pl.program_id(0)
        accs = (dmu_ref, dw0_ref, da0_ref, dkk_ref, dka_ref, dwup_ref, daup_ref)

        @pl.when(i == 0)
        def _():
            zrow_scr[...] = jnp.zeros_like(zrow_scr)
            for ref in accs:
                ref[...] = jnp.zeros_like(ref)

        p, prev, ps = _mixed(p_ref, prev8_ref, mu_ref, i == nt - 1)
        ones = ones_ref[...]
        _, vjp = jax.vjp(lambda *args: _prep_fn(*args, ones), ps[:, W:2 * W], ps[:, 4 * W:], w0_ref[...], a0_ref[...],
                         kk_ref[...], ka_ref[...], wup_ref[...], aup_ref[...])
        dkr, dxwa, dw0, da0, dkk, dka, dwup, daup = vjp(
            (dw_ref[...], dk_ref[...] + dk2_ref[...], dkap_ref[...], da_ref[...]))
        dps = jnp.concatenate([dr_ref[...] + dr2_ref[...], dkr, dv_ref[...] + dv2_ref[...], dg_ref[...], dxwa], axis=1)
        z = dps * mu_ref[...]
        dp_ref[...] = dps - z + _shift_up(z, zrow_scr[0:1, :])
        zrow_scr[0:1, :] = z[0:1, :]
        for ref, val in zip(accs, (_colsum(dps * (prev - p)), dw0, da0, dkk, dka, dwup, daup)):
            ref[...] += val

    rev = lambda i: (nt - 1 - i, 0)
    vec = _full((1, W))
    lora = _full((2 * LORA, W))
    tile = pl.BlockSpec((TOK_TILE, W), rev)
    prev8 = pl.BlockSpec((8, RWKV_COLS), lambda i: (jnp.maximum((nt - 1 - i) * (TOK_TILE // 8) - 1, 0), 0))
    return pl.pallas_call(
        body, name="prep_bwd", grid=(nt,),
        in_specs=[pl.BlockSpec((TOK_TILE, RWKV_COLS), rev), prev8, _full((1, RWKV_COLS)), vec, vec, vec, vec, lora, lora,
                  _full((256, 128))] + [tile] * 10,
        out_specs=[pl.BlockSpec((TOK_TILE, RWKV_COLS), rev), _full((1, RWKV_COLS)), vec, vec, vec, vec, lora, lora],
        out_shape=[jax.ShapeDtypeStruct((T, RWKV_COLS), F32), jax.ShapeDtypeStruct((1, RWKV_COLS), F32)]
        + [jax.ShapeDtypeStruct((1, W), F32)] * 4 + [jax.ShapeDtypeStruct((2 * LORA, W), F32)] * 2,
        scratch_shapes=[pltpu.VMEM((8, RWKV_COLS), F32)],
        compiler_params=_params(dimension_semantics=("arbitrary",)),
    )(p_rwkv, p_rwkv, mu, w0, a0, k_k, k_a, wup_pad, aup_pad, ones64, dr, dw, dk, dv, dkap, da, dg, dr2, dk2, dv2)


def _silu(x):
    return x * jax.nn.sigmoid(x)


def _post_y(o, r, k, v, g_rw, ret_raw, g_ret, ret_gn_g, gn_g, gn_b, r_k, avg128, avg64, ones64):
    xc = ret_raw - _head_mix(ret_raw, avg128)
    ret = xc * lax.rsqrt(_head_mix(xc * xc, avg128) + RET_GN_EPS)
    y_ret = _silu(g_ret) * (ret * ret_gn_g)
    oc = o - _head_mix(o, avg64)
    on = oc * lax.rsqrt(_head_mix(oc * oc, avg64) + RWKV_GN_EPS) * gn_g + gn_b
    bonus = _head_mix(r * k * r_k, ones64) * v
    y_rwkv = _silu(g_rw) * (on + bonus)
    return y_ret, y_rwkv


def _post_loss(h, final_g, target):
    err = _rmsnorm(h, final_g) - target
    return 0.5 * jnp.sum(jnp.mean(err * err, axis=-1))


def _post(o, r, k, v, g_rw, ret_raw, p_ret, x, target, ret_gn_g, gn_g, gn_b, r_k, final_g, w_out, avg128, avg64, ones64):
    T = x.shape[0]
    n_tok_out = 8

    def body(o_ref, r_ref, k_ref, v_ref, grw_ref, ret_ref, gret_ref, x_ref, tgt_ref, rg_ref, gg_ref, gb_ref, rk_ref, fg_ref,
             wo_ref, a128_ref, a64_ref, ones_ref, *outs):
        tok_outs, (dwo_ref, drg_ref, dgg_ref, dgb_ref, drk_ref, dfg_ref, loss_ref) = outs[:n_tok_out], outs[n_tok_out:]
        accs = (dwo_ref, drg_ref, dgg_ref, dgb_ref, drk_ref, dfg_ref, loss_ref)

        @pl.when(pl.program_id(0) == 0)
        def _():
            for ref in accs:
                ref[...] = jnp.zeros_like(ref)

        consts = (a128_ref[...], a64_ref[...], ones_ref[...])
        (y_ret, y_rwkv), vjp = jax.vjp(
            lambda *args: _post_y(*args, *consts), o_ref[...], r_ref[...], k_ref[...], v_ref[...], grw_ref[...], ret_ref[...],
            gret_ref[...], rg_ref[...], gg_ref[...], gb_ref[...], rk_ref[...])
        h = x_ref[...] + _dot_bf(y_ret, wo_ref[0:RET_WIDTH, :]) + _dot_bf(y_rwkv, wo_ref[RET_WIDTH:, :])
        loss, (dh, dfg) = jax.value_and_grad(_post_loss, argnums=(0, 1))(h, fg_ref[...], tgt_ref[...])
        dy_ret = _dot_nt_bf(dh, wo_ref[0:RET_WIDTH, :])
        dy_rwkv = _dot_nt_bf(dh, wo_ref[RET_WIDTH:, :])
        do, dr, dk, dv, dgrw, dret, dgret, drg, dgg, dgb, drk = vjp((dy_ret, dy_rwkv))
        for ref, val in zip(tok_outs, (dh, do, dr, dk, dv, dgrw, dret, dgret)):
            ref[...] = val
        dwo_ref[0:RET_WIDTH, :] += _dot_tn_bf(y_ret, dh)
        dwo_ref[RET_WIDTH:, :] += _dot_tn_bf(y_rwkv, dh)
        for ref, val in zip(accs[1:], (drg, dgg, dgb, drk, dfg, jnp.full((1, 128), loss, F32))):
            ref[...] += val

    tile = _rows(TOK_TILE, W)
    wide = _rows(TOK_TILE, D_MODEL)
    vec = _full((1, W))
    sq = _full((256, 128))
    return pl.pallas_call(
        body, name="post", grid=(T // TOK_TILE,),
        in_specs=[tile] * 6 + [pl.BlockSpec((TOK_TILE, W), lambda i: (i, 2)), wide, wide, vec, vec, vec, vec,
                               _full((1, D_MODEL)), _full((D_MODEL, D_MODEL)), sq, sq, sq],
        out_specs=[wide] + [tile] * 7 + [_full((D_MODEL, D_MODEL)), vec, vec, vec, vec, _full((1, D_MODEL)), _full((1, 128))],
        out_shape=[jax.ShapeDtypeStruct((T, D_MODEL), F32)] + [jax.ShapeDtypeStruct((T, W), F32)] * 7
        + [jax.ShapeDtypeStruct((D_MODEL, D_MODEL), F32)] + [jax.ShapeDtypeStruct((1, W), F32)] * 4
        + [jax.ShapeDtypeStruct((1, D_MODEL), F32), jax.ShapeDtypeStruct((1, 128), F32)],
        compiler_params=_params(dimension_semantics=("arbitrary",)),
    )(o, r, k, v, g_rw, ret_raw, p_ret, x, target, ret_gn_g, gn_g, gn_b, r_k, final_g, w_out, avg128, avg64, ones64)


def _inproj_bwd_x(x, norm_g, dp_qkv, dg_ret, dp_rwkv, dh, w_in):
    T = x.shape[0]
    n_qkv = 2 * RET_QK + RET_WIDTH

    def body(x_ref, g_ref, dqkv_ref, dgret_ref, drwkv_ref, dh_ref, w_ref, dx_ref, dg_ref):
        @pl.when(pl.program_id(0) == 0)
        def _():
            dg_ref[...] = jnp.zeros_like(dg_ref)

        _, vjp = jax.vjp(_rmsnorm, x_ref[...], g_ref[...])
        du = (_dot_nt_bf(dqkv_ref[...], w_ref[:, 0:n_qkv]) + _dot_nt_bf(dgret_ref[...], w_ref[:, n_qkv:RET_COLS])
              + _dot_nt_bf(drwkv_ref[...], w_ref[:, RET_COLS:]))
        dx, dg = vjp(du)
        dx_ref[...] = dx + dh_ref[...]
        dg_ref[...] += dg

    return pl.pallas_call(
        body, name="inproj_bwd_x", grid=(T // TOK_TILE,),
        in_specs=[_rows(TOK_TILE, D_MODEL), _full((1, D_MODEL)), _rows(TOK_TILE, n_qkv), _rows(TOK_TILE, RET_WIDTH),
                  _rows(TOK_TILE, RWKV_COLS), _rows(TOK_TILE, D_MODEL), _full((D_MODEL, IN_COLS))],
        out_specs=[_rows(TOK_TILE, D_MODEL), _full((1, D_MODEL))],
        out_shape=[jax.ShapeDtypeStruct((T, D_MODEL), F32), jax.ShapeDtypeStruct((1, D_MODEL), F32)],
        compiler_params=_params(dimension_semantics=("arbitrary",)),
    )(x, norm_g, dp_qkv, dg_ret, dp_rwkv, dh, w_in)


def _grad_w(name, u, dp):
    T, n = dp.shape
    tile = 2 * TOK_TILE

    def body(u_ref, dp_ref, out_ref):
        @pl.when(pl.program_id(0) == 0)
        def _():
            out_ref[...] = jnp.zeros_like(out_ref)

        out_ref[...] += _dot_tn_bf(u_ref[...], dp_ref[...])

    return pl.pallas_call(
        body, name=name, grid=(T // tile,),
        in_specs=[_rows(tile, D_MODEL), _rows(tile, n)],
        out_specs=_full((D_MODEL, n)),
        out_shape=jax.ShapeDtypeStruct((D_MODEL, n), F32),
        compiler_params=_params(dimension_semantics=("arbitrary",)),
    )(u, dp)


def _pad_lora(w_up, top):
    z = jnp.zeros_like(w_up)
    return jnp.concatenate([w_up, z] if top else [z, w_up], axis=0)


def _local_grads(x, target, norm_g, w_in_bf, ret_gn_g, mu, w_lora_up, w0, a_lora_up, a0, k_k, k_a, r_k, gn_g, gn_b,
                 w_out_bf, final_g):
    T = x.shape[0]
    tabs = _rope_tables(T) + _ret_tables()
    ones64 = _block_mix(128, RWKV_HEAD)
    avg64 = _block_mix(128, RWKV_HEAD, 1.0 / RWKV_HEAD)
    avg128 = _block_mix(128, RET_DV, 1.0 / RET_DV)
    wup_pad, aup_pad = _pad_lora(w_lora_up, True), _pad_lora(a_lora_up, False)

    p_ret, p_rwkv, u = _inproj(x, norm_g, w_in_bf)
    ret_raw, s_saved = _ret_fwd(p_ret, tabs)
    r, w, k, v, kap, a, g_rw = _prep_fwd(p_rwkv, mu, w0, a0, k_k, k_a, wup_pad, aup_pad, ones64)
    o, s_all, sa_all = _wkv_fwd(r, w, k, v, kap, a)
    (dh, do, dr2, dk2, dv2, dgrw, dret, dgret, d_w_out, d_ret_gn_g, d_gn_g, d_gn_b, d_r_k, d_final_g, loss) = _post(
        o, r, k, v, g_rw, ret_raw, p_ret, x, target, ret_gn_g, gn_g, gn_b, r_k, final_g, w_out_bf, avg128, avg64, ones64)
    dr, dw, dk, dv, dkap, da = _wkv_bwd(r, w, k, v, kap, a, s_all, sa_all, do)
    dp_rwkv, d_mu, d_w0, d_a0, d_k_k, d_k_a, d_wup, d_aup = _prep_bwd(
        p_rwkv, mu, w0, a0, k_k, k_a, wup_pad, aup_pad, ones64, dr, dw, dk, dv, dkap, da, dgrw, dr2, dk2, dv2)
    dp_qkv = _ret_bwd(p_ret, s_saved, dret, tabs)
    dx, d_norm_g = _inproj_bwd_x(x, norm_g, dp_qkv, dgret, dp_rwkv, dh, w_in_bf)
    d_w_in = jnp.concatenate([_grad_w("grad_w_qkv", u, dp_qkv), _grad_w("grad_w_gret", u, dgret),
                              _grad_w("grad_w_rwkv", u, dp_rwkv)], axis=1)
    grads = dict(norm_g=d_norm_g, w_in=d_w_in, ret_gn_g=d_ret_gn_g, rwkv_mu=d_mu, w_lora_up=d_wup[:LORA], w0=d_w0,
                 a_lora_up=d_aup[LORA:], a0=d_a0, k_k=d_k_k, k_a=d_k_a, r_k=d_r_k, rwkv_gn_g=d_gn_g, rwkv_gn_b=d_gn_b,
                 w_out=d_w_out, final_norm_g=d_final_g)
    return loss, dx, grads


def _mesh_pos():
    return lax.axis_index("x"), lax.axis_index("y"), lax.axis_index("c")


def _all_gather(shards):
    n = len(shards)

    def body(*refs):
        x_refs, out_refs = refs[:n], refs[n:2 * n]
        send_sems, recv_sems, local_sems = refs[2 * n:]
        x, y, c = _mesh_pos()
        me, sibling = (x, y, c), (x, y, 1 - c)
        chips = [(1 - x, y), (x, 1 - y), (1 - x, 1 - y)]

        def rows(a, pos):
            m = x_refs[a].shape[0]
            return out_refs[a].at[pl.ds((4 * pos[0] + 2 * pos[1] + pos[2]) * m, m), :]

        def copy(a, k, block, to, src=None):
            return pltpu.make_async_remote_copy(
                src_ref=rows(a, block) if src is None else src, dst_ref=rows(a, block),
                send_sem=send_sems.at[a, k], recv_sem=recv_sems.at[a, k], device_id=to, device_id_type=MESH)

        mine = [pltpu.make_async_copy(x_refs[a], rows(a, me), local_sems.at[a]) for a in range(n)]
        for cp in mine:
            cp.start()
        first = []
        for a in range(n):
            first.append(copy(a, 0, me, sibling, src=x_refs[a]))
            first += [copy(a, 1 + j, me, (*chip, c), src=x_refs[a]) for j, chip in enumerate(chips)]
        for cp in first:
            cp.start()
        passed = []
        for j, chip in enumerate(chips):
            for a in range(n):
                copy(a, 1 + j, (*chip, c), me).wait_recv()
                passed.append(copy(a, 4 + j, (*chip, c), sibling))
                passed[-1].start()
        for a in range(n):
            copy(a, 0, sibling, me).wait_recv()
            for j, chip in enumerate(chips):
                copy(a, 4 + j, (*chip, 1 - c), me).wait_recv()
        for cp in first + passed:
            cp.wait_send()
        for cp in mine:
            cp.wait()

    vmem = pl.BlockSpec(memory_space=pltpu.VMEM)
    return pl.pallas_call(
        body, name="gather_weights",
        out_shape=[jax.ShapeDtypeStruct((N_DEV * s.shape[0], s.shape[1]), s.dtype) for s in shards],
        in_specs=[vmem] * n, out_specs=[vmem] * n,
        scratch_shapes=[pltpu.SemaphoreType.DMA((n, 7)), pltpu.SemaphoreType.DMA((n, 7)), pltpu.SemaphoreType.DMA((n,))],
        compiler_params=_params(),
    )(*shards)


N_CHIP = 4


def _exchange_pairs(big, small):
    nb, ns = len(big), len(small)

    def body(*refs):
        big_in, small_in = refs[:nb], refs[nb:nb + ns]
        theirs, small_out = refs[nb + ns:2 * nb + ns], refs[2 * nb + ns:2 * nb + 2 * ns]
        pair_send, pair_recv, send_sems, recv_sems, local_sems = refs[2 * nb + 2 * ns:]
        x, y, c = _mesh_pos()
        me = 4 * x + 2 * y + c
        local = [pltpu.make_async_copy(small_in[a].at[me], small_out[a].at[me], local_sems.at[a]) for a in range(ns)]
        for cp in local:
            cp.start()
        copies = [pltpu.make_async_remote_copy(
            src_ref=big_in[a], dst_ref=theirs[a], send_sem=pair_send.at[a], recv_sem=pair_recv.at[a],
            device_id=(x, y, 1 - c), device_id_type=MESH) for a in range(nb)]
        for k in range(1, N_DEV):
            peer = (x ^ (k >> 2), y ^ ((k >> 1) & 1), c ^ (k & 1))
            peer_idx = 4 * peer[0] + 2 * peer[1] + peer[2]
            copies += [pltpu.make_async_remote_copy(
                src_ref=small_in[a].at[peer_idx], dst_ref=small_out[a].at[me], send_sem=send_sems.at[a, k - 1],
                recv_sem=recv_sems.at[a, k - 1], device_id=peer, device_id_type=MESH) for a in range(ns)]
        for cp in copies:
            cp.start()
        for cp in copies:
            cp.wait()
        for cp in local:
            cp.wait()

    hbm = pl.BlockSpec(memory_space=pl.ANY)
    out_shape = [jax.ShapeDtypeStruct(p.shape, p.dtype) for p in big + small]
    dma = pltpu.SemaphoreType.DMA
    res = pl.pallas_call(
        body, name="exchange_pairs", out_shape=out_shape,
        in_specs=[hbm] * (nb + ns), out_specs=[hbm] * len(out_shape),
        scratch_shapes=[dma((nb,)), dma((nb,)), dma((ns, 7)), dma((ns, 7)), dma((ns,))],
        compiler_params=_params(),
    )(*big, *small)
    return res[:nb], res[nb:]


def _pair_sum(name, mine, theirs, row_tile):
    _, rows, cols = mine.shape

    def body(a_ref, b_ref, o_ref):
        o_ref[...] = (a_ref[...].astype(F32) + b_ref[...].astype(F32)).astype(o_ref.dtype)

    spec = pl.BlockSpec((N_CHIP, row_tile, cols), lambda i: (0, i, 0))
    return pl.pallas_call(
        body, name=name, grid=(rows // row_tile,), in_specs=[spec, spec], out_specs=spec,
        out_shape=jax.ShapeDtypeStruct(mine.shape, mine.dtype),
        compiler_params=_params(dimension_semantics=("arbitrary",)),
    )(mine, theirs)


def _exchange_chips(parts):
    n = len(parts)

    def body(*refs):
        in_refs, out_refs = refs[:n], refs[n:2 * n]
        send_sems, recv_sems, local_sems = refs[2 * n:]
        x, y, c = _mesh_pos()
        my_chip = 2 * x + y
        own = [pltpu.make_async_copy(in_refs[a].at[my_chip], out_refs[a].at[my_chip], local_sems.at[a]) for a in range(n)]
        for cp in own:
            cp.start()
        copies = []
        for k in range(1, N_CHIP):
            px, py = x ^ (k >> 1), y ^ (k & 1)
            copies += [pltpu.make_async_remote_copy(
                src_ref=in_refs[a].at[2 * px + py], dst_ref=out_refs[a].at[my_chip], send_sem=send_sems.at[a, k - 1],
                recv_sem=recv_sems.at[a, k - 1], device_id=(px, py, c), device_id_type=MESH) for a in range(n)]
        for cp in copies:
            cp.start()
        for cp in copies:
            cp.wait()
        for cp in own:
            cp.wait()

    hbm = pl.BlockSpec(memory_space=pl.ANY)
    dma = pltpu.SemaphoreType.DMA
    return pl.pallas_call(
        body, name="exchange_chips",
        out_shape=[jax.ShapeDtypeStruct(p.shape, p.dtype) for p in parts],
        in_specs=[hbm] * n, out_specs=[hbm] * n,
        scratch_shapes=[dma((n, N_CHIP - 1)), dma((n, N_CHIP - 1)), dma((n,))],
        compiler_params=_params(),
    )(*parts)


def _adamw(w, g, m, v):
    m = ADAM_B1 * m + (1.0 - ADAM_B1) * g
    v = ADAM_B2 * v + (1.0 - ADAM_B2) * (g * g)
    m_hat = m / (1.0 - ADAM_B1 ** ADAM_STEP)
    v_hat = v / (1.0 - ADAM_B2 ** ADAM_STEP)
    return -ADAM_LR * (m_hat / (jnp.sqrt(v_hat) + ADAM_EPS) + ADAM_WD * w), m, v


def _reduce_adamw(name, parts, w, m, v, row_tile):
    n_parts, rows, cols = parts.shape

    def body(p_ref, w_ref, m_ref, v_ref, g_ref, d_ref, nm_ref, nv_ref):
        g = p_ref[0].astype(F32)
        for s in range(1, n_parts):
            g = g + p_ref[s].astype(F32)
        g_ref[...] = g
        d_ref[...], nm_ref[...], nv_ref[...] = _adamw(w_ref[...], g, m_ref[...], v_ref[...])

    tile = pl.BlockSpec((row_tile, cols), lambda i: (i, 0))
    return pl.pallas_call(
        body, name=name, grid=(rows // row_tile,),
        in_specs=[pl.BlockSpec((n_parts, row_tile, cols), lambda i: (0, i, 0)), tile, tile, tile],
        out_specs=[tile] * 4,
        out_shape=[jax.ShapeDtypeStruct((rows, cols), F32)] * 4,
        compiler_params=_params(dimension_semantics=("arbitrary",)),
    )(parts, w, m, v)


_SMALL = (("norm_g", 1024), ("ret_gn_g", 512), ("rwkv_mu", 2176), ("w0", 512), ("a0", 512), ("k_k", 512), ("k_a", 512),
          ("r_k", 512), ("rwkv_gn_g", 512), ("rwkv_gn_b", 512), ("final_norm_g", 1024))
_SMALL_LANES = sum(n for _, n in _SMALL) + 128
_WEIGHTS = ("norm_g", "w_in", "ret_gn_g", "rwkv_mu", "w_lora_up", "w0", "a_lora_up", "a0", "k_k", "k_a", "r_k", "rwkv_gn_g",
            "rwkv_gn_b", "w_out", "final_norm_g")


def _adamw_vectors(parts, wts, mom, var):
    k = len(_SMALL)

    def body(p_ref, *refs):
        w_refs, m_refs, v_refs, outs = refs[:k], refs[k:2 * k], refs[2 * k:3 * k], refs[3 * k:]
        g_all = p_ref[0]
        for s in range(1, N_DEV):
            g_all = g_all + p_ref[s]
        off = 0
        for i, (_, n) in enumerate(_SMALL):
            g = g_all[:, off:off + n]
            off += n
            outs[4 * i][...] = g
            outs[4 * i + 1][...], outs[4 * i + 2][...], outs[4 * i + 3][...] = _adamw(
                w_refs[i][...], g, m_refs[i][...], v_refs[i][...])
        outs[4 * k][...] = g_all[:, off:off + 128]

    vmem = pl.BlockSpec(memory_space=pltpu.VMEM)
    shapes = [jax.ShapeDtypeStruct((1, n), F32) for _, n in _SMALL for _ in range(4)] + [jax.ShapeDtypeStruct((1, 128), F32)]
    res = pl.pallas_call(
        body, name="adamw_vectors", out_shape=shapes,
        in_specs=[vmem] * (1 + 3 * k), out_specs=[vmem] * len(shapes), compiler_params=_params(),
    )(parts, *[wts[n] for n, _ in _SMALL], *[mom[n] for n, _ in _SMALL], *[var[n] for n, _ in _SMALL])
    return {n: res[4 * i:4 * i + 4] for i, (n, _) in enumerate(_SMALL)}, res[4 * k]


def kernel(x, norm_g, w_in, ret_gn_g, rwkv_mu, w_lora_up, w0, a_lora_up, a0, k_k, k_a, r_k, rwkv_gn_g, rwkv_gn_b, w_out, final_norm_g, loss_target, m_norm_g, m_w_in, m_ret_gn_g, m_rwkv_mu, m_w_lora_up, m_w0, m_a_lora_up, m_a0, m_k_k, m_k_a, m_r_k, m_rwkv_gn_g, m_rwkv_gn_b, m_w_out, m_final_norm_g, v_norm_g, v_w_in, v_ret_gn_g, v_rwkv_mu, v_w_lora_up, v_w0, v_a_lora_up, v_a0, v_k_k, v_k_a, v_r_k, v_rwkv_gn_g, v_rwkv_gn_b, v_w_out, v_final_norm_g):
    wts = dict(norm_g=norm_g, w_in=w_in, ret_gn_g=ret_gn_g, rwkv_mu=rwkv_mu, w_lora_up=w_lora_up, w0=w0, a_lora_up=a_lora_up,
               a0=a0, k_k=k_k, k_a=k_a, r_k=r_k, rwkv_gn_g=rwkv_gn_g, rwkv_gn_b=rwkv_gn_b, w_out=w_out,
               final_norm_g=final_norm_g)
    mom = dict(norm_g=m_norm_g, w_in=m_w_in, ret_gn_g=m_ret_gn_g, rwkv_mu=m_rwkv_mu, w_lora_up=m_w_lora_up, w0=m_w0,
               a_lora_up=m_a_lora_up, a0=m_a0, k_k=m_k_k, k_a=m_k_a, r_k=m_r_k, rwkv_gn_g=m_rwkv_gn_g,
               rwkv_gn_b=m_rwkv_gn_b, w_out=m_w_out, final_norm_g=m_final_norm_g)
    var = dict(norm_g=v_norm_g, w_in=v_w_in, ret_gn_g=v_ret_gn_g, rwkv_mu=v_rwkv_mu, w_lora_up=v_w_lora_up, w0=v_w0,
               a_lora_up=v_a_lora_up, a0=v_a0, k_k=v_k_k, k_a=v_k_a, r_k=v_r_k, rwkv_gn_g=v_rwkv_gn_g,
               rwkv_gn_b=v_rwkv_gn_b, w_out=v_w_out, final_norm_g=v_final_norm_g)
    shapes = {n: wts[n].shape for n in _WEIGHTS}

    g_in, g_out, g_wup, g_aup = _all_gather(
        [w_in[0].astype(BF16), w_out[0].astype(BF16), w_lora_up[0], a_lora_up[0]])
    w_in_bf = g_in.reshape(N_DEV, D_MODEL, SHARD_IN).transpose(1, 0, 2).reshape(D_MODEL, IN_COLS)
    wup_full = g_wup.reshape(N_DEV, LORA, SHARD_LORA).transpose(1, 0, 2).reshape(LORA, W)
    aup_full = g_aup.reshape(N_DEV, LORA, SHARD_LORA).transpose(1, 0, 2).reshape(LORA, W)

    loss, dx, g = _local_grads(
        x[0], loss_target[0], norm_g, w_in_bf, ret_gn_g, rwkv_mu, wup_full, w0, aup_full, a0, k_k, k_a,
        r_k.reshape(1, W), rwkv_gn_g, rwkv_gn_b, g_out, final_norm_g.reshape(1, D_MODEL))

    small = jnp.concatenate([g[n] for n, _ in _SMALL] + [loss], axis=1)
    core = lax.axis_index("c")
    by_core = lambda t: [lax.dynamic_index_in_dim(t, i, axis=1, keepdims=False) for i in (core, 1 - core)]
    in_mine, in_sib = by_core(g["w_in"].reshape(D_MODEL, N_CHIP, 2, SHARD_IN).transpose(1, 2, 0, 3).astype(BF16))
    out_mine, out_sib = by_core(g["w_out"].reshape(N_CHIP, 2, SHARD_OUT, D_MODEL).astype(BF16))
    (in_theirs, out_theirs), parts = _exchange_pairs(
        [in_sib, out_sib],
        [g["w_lora_up"].reshape(LORA, N_DEV, SHARD_LORA).transpose(1, 0, 2),
         g["a_lora_up"].reshape(LORA, N_DEV, SHARD_LORA).transpose(1, 0, 2),
         jnp.broadcast_to(small[None], (N_DEV, 1, _SMALL_LANES))])
    by_chip = _exchange_chips([_pair_sum("pair_sum_w_in", in_mine, in_theirs, 256),
                               _pair_sum("pair_sum_w_out", out_mine, out_theirs, SHARD_OUT)])
    res = {}
    res["w_in"] = _reduce_adamw("adamw_w_in", by_chip[0], w_in[0], m_w_in[0], v_w_in[0], 256)
    res["w_out"] = _reduce_adamw("adamw_w_out", by_chip[1], w_out[0], m_w_out[0], v_w_out[0], SHARD_OUT)
    res["w_lora_up"] = _reduce_adamw("adamw_w_lora_up", parts[0], w_lora_up[0], m_w_lora_up[0], v_w_lora_up[0], LORA)
    res["a_lora_up"] = _reduce_adamw("adamw_a_lora_up", parts[1], a_lora_up[0], m_a_lora_up[0], v_a_lora_up[0], LORA)
    as_row = lambda d: {n: d[n].reshape(1, size) for n, size in _SMALL}
    vec, loss_row = _adamw_vectors(parts[2], as_row(wts), as_row(mom), as_row(var))
    res.update(vec)
    res = {n: [t.reshape(shapes[n]) for t in res[n]] for n in _WEIGHTS}
    return (loss_row[0, 0], dx[None], *[res[n][0] for n in _WEIGHTS], *[res[n][1] for n in _WEIGHTS],
            *[res[n][2] for n in _WEIGHTS], *[res[n][3] for n in _WEIGHTS])
```

```python
import functools

import numpy as np
import jax
import jax.numpy as jnp
from jax import lax
from jax.experimental import pallas as pl
from jax.experimental.pallas import tpu as pltpu

F32 = jnp.float32
BF16 = jnp.bfloat16

D_MODEL = 1024
CHUNK = 64
RET_HEADS = 4
RET_DV = 128
RET_DK = 64
RET_QK = 256
RET_WIDTH = 512
RWKV_WIDTH = 512
RWKV_HEAD = 64
RWKV_HEADS = 8
LORA = 64
RET_COLS = 2 * RET_QK + 2 * RET_WIDTH
RWKV_COLS = 4 * RWKV_WIDTH + 2 * LORA
IN_COLS = RET_COLS + RWKV_COLS
ROPE_BASE = 10000.0
RMS_EPS = 1e-6
RET_GN_EPS = 1e-5
RWKV_GN_EPS = 64e-5
ADAM_LR = 0.001
ADAM_B1 = 0.9
ADAM_B2 = 0.999
ADAM_EPS = 1e-08
ADAM_WD = 0.01
ADAM_STEP = 10
N_DEV = 8
SHARD_IN = IN_COLS // N_DEV
SHARD_OUT = D_MODEL // N_DEV
SHARD_LORA = RWKV_WIDTH // N_DEV
VMEM_LIMIT = 56 * 1024 * 1024
TOK_TILE = 256
WKV_CHUNK = 64

MESH = pl.DeviceIdType.MESH


def _dot_bf(a, b):
    return jnp.dot(a.astype(BF16), b.astype(BF16), preferred_element_type=F32)


def _dot_nt_bf(a, b):
    return lax.dot_general(a.astype(BF16), b.astype(BF16), (((1,), (1,)), ((), ())), preferred_element_type=F32)


def _dot_tn_bf(a, b):
    return lax.dot_general(a.astype(BF16), b.astype(BF16), (((0,), (0,)), ((), ())), preferred_element_type=F32)


@jax.custom_vjp
def _mm(a, b):
    return _dot_bf(a, b)


@jax.custom_vjp
def _mm_nt(a, b):
    return _dot_nt_bf(a, b)


@jax.custom_vjp
def _mm_tn(a, b):
    return _dot_tn_bf(a, b)


_mm.defvjp(lambda a, b: (_dot_bf(a, b), (a, b)), lambda res, g: (_dot_nt_bf(g, res[1]), _dot_tn_bf(res[0], g)))
_mm_nt.defvjp(lambda a, b: (_dot_nt_bf(a, b), (a, b)), lambda res, g: (_dot_bf(g, res[1]), _dot_tn_bf(g, res[0])))
_mm_tn.defvjp(lambda a, b: (_dot_tn_bf(a, b), (a, b)), lambda res, g: (_dot_nt_bf(res[1], g), _dot_bf(res[0], g)))


def _trunc(x):
    return lax.bitcast_convert_type(lax.bitcast_convert_type(x, jnp.uint32) & jnp.uint32(0xFFFF0000), F32)


def _two_piece(x):
    hi = _trunc(x)
    return jnp.concatenate([hi, x - hi], axis=1)


def _mix_raw(x, mat2):
    return _unstack(jnp.dot(_two_piece(_stack(x)), mat2, preferred_element_type=F32))


@jax.custom_vjp
def _head_mix(x, mat2):
    return _mix_raw(x, mat2)


_head_mix.defvjp(lambda x, mat2: (_mix_raw(x, mat2), mat2), lambda mat2, g: (_mix_raw(g, mat2), jnp.zeros_like(mat2)))


def _swap_halves(x):
    lane = lax.broadcasted_iota(jnp.int32, x.shape, 1)
    return jnp.where((lane & (RET_DK - 1)) < RET_DK // 2, pltpu.roll(x, RET_QK - RET_DK // 2, axis=1),
                     pltpu.roll(x, RET_DK // 2, axis=1))


@jax.custom_vjp
def _rot(x):
    return _swap_halves(x)


_rot.defvjp(lambda x: (_swap_halves(x), None), lambda _, g: (_swap_halves(g),))


def _params(**kw):
    return pltpu.CompilerParams(vmem_limit_bytes=VMEM_LIMIT, **kw)


def _full(shape):
    nd = len(shape)
    return pl.BlockSpec(shape, lambda i, _nd=nd: (0,) * _nd)


def _rows(tile, width):
    return pl.BlockSpec((tile, width), lambda i: (i, 0))


def _block_mix(n, blk, scale=1.0):
    idx = np.arange(n) // blk
    m = (idx[:, None] == idx[None, :]).astype(np.float32) * scale
    return jnp.asarray(np.concatenate([m, m], axis=0))


def _rope_tables(T):
    half = RET_DK // 2
    expo = -np.arange(half, dtype=np.float32) / np.float32(half)
    freqs = np.exp(expo * np.float32(np.log(ROPE_BASE))).astype(np.float32)
    ang = np.arange(T, dtype=np.float32)[:, None] * freqs[None, :]
    cos, sin = np.cos(ang).astype(np.float32), np.sin(ang).astype(np.float32)
    cos_h = np.concatenate([cos, cos], axis=1)
    sin_h = np.concatenate([-sin, sin], axis=1)
    cos_t = np.tile(cos_h, (1, RET_HEADS))
    sin_t = np.tile(sin_h, (1, RET_HEADS))
    return jnp.asarray(cos_t), jnp.asarray(sin_t)


def _ret_tables():
    h = np.arange(RET_HEADS, dtype=np.float32)
    lg = np.log(1.0 - np.exp2(-5.0 - h)).astype(np.float32)
    idx = np.arange(CHUNK, dtype=np.float32)
    intra = np.exp(lg[:, None, None] * np.abs(idx[:, None] - idx[None, :])).astype(np.float32)
    q_dec = np.exp(lg[:, None] * (idx[None, :] + 1.0)).astype(np.float32)
    k_dec = np.exp(lg[:, None] * (CHUNK - 1.0 - idx[None, :])).astype(np.float32)
    chunk_dec = np.exp(lg * CHUNK).astype(np.float32)
    lane_head = np.arange(RET_QK) // RET_DK
    mask = (lane_head[None, :] == np.arange(RET_HEADS)[:, None]).astype(np.float32)
    m = np.broadcast_to(mask[:, None, :], (RET_HEADS, CHUNK, RET_QK)).copy()
    qd = m * q_dec[:, :, None]
    kd = m * k_dec[:, :, None]
    return jnp.asarray(intra), jnp.asarray(m), jnp.asarray(qd), jnp.asarray(kd), [float(c) for c in chunk_dec]


def _rmsnorm(x, g):
    return x * lax.rsqrt(jnp.mean(x * x, axis=-1, keepdims=True) + RMS_EPS) * g


def _inproj(x, norm_g, w_in):
    T = x.shape[0]

    def body(x_ref, g_ref, w_ref, pr_ref, pw_ref, u_ref):
        ub = _rmsnorm(x_ref[...], g_ref[...]).astype(BF16)
        u_ref[...] = ub
        pr_ref[...] = jnp.dot(ub, w_ref[:, :RET_COLS], preferred_element_type=F32)
        pw_ref[...] = jnp.dot(ub, w_ref[:, RET_COLS:], preferred_element_type=F32)

    return pl.pallas_call(
        body, name="inproj", grid=(T // TOK_TILE,),
        in_specs=[_rows(TOK_TILE, D_MODEL), _full((1, D_MODEL)), _full((D_MODEL, IN_COLS))],
        out_specs=[_rows(TOK_TILE, RET_COLS), _rows(TOK_TILE, RWKV_COLS), _rows(TOK_TILE, D_MODEL)],
        out_shape=[jax.ShapeDtypeStruct((T, RET_COLS), F32), jax.ShapeDtypeStruct((T, RWKV_COLS), F32),
                   jax.ShapeDtypeStruct((T, D_MODEL), BF16)],
        compiler_params=_params(dimension_semantics=("arbitrary",)),
    )(x, norm_g, w_in)


def _ret_chunk(pq, pk, v_heads, s_heads, cos_t, sin_t, dec, hm, qd, kd, chunk_dec):
    q = pq * cos_t + _rot(pq) * sin_t
    k = (pk * cos_t + _rot(pk) * sin_t) * (RET_DK ** -0.5)
    outs, s_out = [], []
    for h in range(RET_HEADS):
        sc = _mm_nt(q * hm[h], k * hm[h]) * dec[h]
        intra = _mm(sc, v_heads[h])
        kv = _mm_tn(k * kd[h], v_heads[h])
        inter = _mm(q * qd[h], s_heads[h])
        outs.append(intra + inter)
        s_out.append(s_heads[h] * chunk_dec[h] + kv)
    return tuple(outs), tuple(s_out)


def _ret_specs():
    const = [_full((RET_HEADS, CHUNK, CHUNK)), _full((RET_HEADS, CHUNK, RET_QK)),
             _full((RET_HEADS, CHUNK, RET_QK)), _full((RET_HEADS, CHUNK, RET_QK))]
    return const


RET_GROUP = 4


def _ret_fwd(p_ret, tabs):
    T = p_ret.shape[0]
    G = RET_GROUP
    ng = T // (CHUNK * G)
    cos_t, sin_t, dec, hm, qd, kd, chunk_dec = tabs

    def body(p_ref, cos_ref, sin_ref, dec_ref, hm_ref, qd_ref, kd_ref, out_ref, sin_save_ref, s_scr):
        @pl.when(pl.program_id(0) == 0)
        def _():
            s_scr[...] = jnp.zeros_like(s_scr)

        consts = (dec_ref[...], hm_ref[...], qd_ref[...], kd_ref[...])
        s_heads = tuple(s_scr[h] for h in range(RET_HEADS))
        for c in range(G):
            rows = pl.ds(c * CHUNK, CHUNK)
            for h in range(RET_HEADS):
                sin_save_ref[c, h] = s_heads[h]
            v_heads = tuple(p_ref[rows, 2 * RET_QK + RET_DV * h:2 * RET_QK + RET_DV * (h + 1)] for h in range(RET_HEADS))
            outs, s_heads = _ret_chunk(p_ref[rows, 0:RET_QK], p_ref[rows, RET_QK:2 * RET_QK], v_heads, s_heads,
                                       cos_ref[rows, :], sin_ref[rows, :], *consts, chunk_dec)
            for h in range(RET_HEADS):
                out_ref[rows, RET_DV * h:RET_DV * (h + 1)] = outs[h]
        for h in range(RET_HEADS):
            s_scr[h] = s_heads[h]

    tok = CHUNK * G
    return pl.pallas_call(
        body, name="ret_fwd", grid=(ng,),
        in_specs=[pl.BlockSpec((tok, RET_COLS), lambda i: (i, 0)), _rows(tok, RET_QK), _rows(tok, RET_QK)] + _ret_specs(),
        out_specs=[_rows(tok, RET_WIDTH), pl.BlockSpec((G, RET_HEADS, RET_QK, RET_DV), lambda i: (i, 0, 0, 0))],
        out_shape=[jax.ShapeDtypeStruct((T, RET_WIDTH), F32),
                   jax.ShapeDtypeStruct((T // CHUNK, RET_HEADS, RET_QK, RET_DV), F32)],
        scratch_shapes=[pltpu.VMEM((RET_HEADS, RET_QK, RET_DV), F32)],
        compiler_params=_params(dimension_semantics=("arbitrary",)),
    )(p_ret, cos_t, sin_t, dec, hm, qd, kd)


def _ret_bwd(p_ret, s_saved, d_ret, tabs):
    T = p_ret.shape[0]
    G = RET_GROUP
    ng = T // (CHUNK * G)
    cos_t, sin_t, dec, hm, qd, kd, chunk_dec = tabs

    def body(p_ref, s_ref, dret_ref, cos_ref, sin_ref, dec_ref, hm_ref, qd_ref, kd_ref, dp_ref, ds_scr):
        @pl.when(pl.program_id(0) == 0)
        def _():
            ds_scr[...] = jnp.zeros_like(ds_scr)

        consts = (dec_ref[...], hm_ref[...], qd_ref[...], kd_ref[...])
        d_s = tuple(ds_scr[h] for h in range(RET_HEADS))
        for c in reversed(range(G)):
            rows = pl.ds(c * CHUNK, CHUNK)
            v_heads = tuple(p_ref[rows, 2 * RET_QK + RET_DV * h:2 * RET_QK + RET_DV * (h + 1)] for h in range(RET_HEADS))
            s_heads = tuple(s_ref[c, h] for h in range(RET_HEADS))
            tables = (cos_ref[rows, :], sin_ref[rows, :]) + consts
            _, vjp = jax.vjp(lambda a, b, c_, d: _ret_chunk(a, b, c_, d, *tables, chunk_dec),
                             p_ref[rows, 0:RET_QK], p_ref[rows, RET_QK:2 * RET_QK], v_heads, s_heads)
            d_out = tuple(dret_ref[rows, RET_DV * h:RET_DV * (h + 1)] for h in range(RET_HEADS))
            dq, dk, dv, d_s = vjp((d_out, d_s))
            dp_ref[rows, 0:RET_QK] = dq
            dp_ref[rows, RET_QK:2 * RET_QK] = dk
            for h in range(RET_HEADS):
                dp_ref[rows, 2 * RET_QK + RET_DV * h:2 * RET_QK + RET_DV * (h + 1)] = dv[h]
        for h in range(RET_HEADS):
            ds_scr[h] = d_s[h]

    tok = CHUNK * G
    rev = lambda i: (ng - 1 - i, 0)
    return pl.pallas_call(
        body, name="ret_bwd", grid=(ng,),
        in_specs=[pl.BlockSpec((tok, RET_COLS), rev),
                  pl.BlockSpec((G, RET_HEADS, RET_QK, RET_DV), lambda i: (ng - 1 - i, 0, 0, 0)),
                  pl.BlockSpec((tok, RET_WIDTH), rev), pl.BlockSpec((tok, RET_QK), rev), pl.BlockSpec((tok, RET_QK), rev)]
        + _ret_specs(),
        out_specs=pl.BlockSpec((tok, 2 * RET_QK + RET_WIDTH), rev),
        out_shape=jax.ShapeDtypeStruct((T, 2 * RET_QK + RET_WIDTH), F32),
        scratch_shapes=[pltpu.VMEM((RET_HEADS, RET_QK, RET_DV), F32)],
        compiler_params=_params(dimension_semantics=("arbitrary",)),
    )(p_ret, s_saved, d_ret, cos_t, sin_t, dec, hm, qd, kd)


def _wkv_consts():
    lane = lax.broadcasted_iota(jnp.int32, (RWKV_HEAD, RWKV_WIDTH), 1)
    sub = lax.broadcasted_iota(jnp.int32, (RWKV_HEAD, RWKV_WIDTH), 0)
    diag = ((lane & (RWKV_HEAD - 1)) == sub).astype(F32)
    r = lax.broadcasted_iota(jnp.int32, (3 * 128, 128), 0)
    c = lax.broadcasted_iota(jnp.int32, (3 * 128, 128), 1)
    ones = (((r & 127) >> 6) == (c >> 6)).astype(BF16)
    return diag, ones


def _stack(x):
    return jnp.concatenate([x[:, 128 * p:128 * (p + 1)] for p in range(4)], axis=0)


def _unstack(y):
    n = y.shape[0] // 4
    return jnp.concatenate([y[n * p:n * (p + 1)] for p in range(4)], axis=1)


def _split(x, n):
    pieces = []
    for _ in range(n):
        p = x.astype(BF16)
        pieces.append(p)
        x = x - p.astype(F32)
    return pieces


def _lane_sum(x, ones):
    return _unstack(jnp.dot(_two_piece(_stack(x)), ones[:256].astype(F32), preferred_element_type=F32))


def _lane_sum_bf(x, ones):
    return _unstack(jnp.dot(_stack(x).astype(BF16), ones[:128], preferred_element_type=F32))


def _colsum(x):
    return jnp.sum(x, axis=0, keepdims=True)


def _expand_cols(xt, t):
    lane = lax.broadcasted_iota(jnp.int32, (RWKV_HEAD, 128), 1)
    tiles = []
    for p in range(4):
        lo = jnp.broadcast_to(xt[128 * p:128 * p + RWKV_HEAD, t:t + 1], (RWKV_HEAD, 128))
        hi = jnp.broadcast_to(xt[128 * p + RWKV_HEAD:128 * (p + 1), t:t + 1], (RWKV_HEAD, 128))
        tiles.append(jnp.where(lane < RWKV_HEAD, lo, hi))
    return jnp.concatenate(tiles, axis=1)


def _head_sums(x, ones):
    return _unstack(jnp.dot(jnp.concatenate(_split(_stack(x), 3), axis=1), ones, preferred_element_type=F32))


def _wkv_fwd(r, w, k, v, kap, a):
    T = r.shape[0]
    C = WKV_CHUNK
    nc = T // C

    def body(r_ref, w_ref, k_ref, v_ref, kap_ref, a_ref, o_ref, s_all_ref, sa_all_ref, s_scr):
        @pl.when(pl.program_id(0) == 0)
        def _():
            s_scr[...] = jnp.zeros_like(s_scr)

        diag, ones = _wkv_consts()
        rr, ww, kk, vv, kap_, aa = (ref[...] for ref in (r_ref, w_ref, k_ref, v_ref, kap_ref, a_ref))
        bb = kap_ * aa
        c1 = _head_sums(pltpu.roll(bb, 1, axis=0) * kap_, ones)
        row = lambda x, t: x[t:t + 1]

        v_cols = vv.T

        s_prev = s_scr[...]
        sa = _lane_sum(s_prev * (-row(kap_, 0)), ones)
        ls, rows = None, []

        def emit_o(t, s_t):
            rows.append(_colsum(_lane_sum_bf(s_t * row(rr, t), ones) * diag))
            if t % 8 == 7:
                o_ref[t - 7:t + 1, :] = jnp.concatenate(rows, axis=0)
                rows.clear()

        for t in range(C):
            u = s_prev * row(ww, t) + _expand_cols(v_cols, t) * row(kk, t)
            if t > 0:
                sa = ls - sa * row(c1, t)
            if t + 1 < C:
                ls = _lane_sum(u * (-row(kap_, t + 1)), ones)
            if t > 0:
                emit_o(t - 1, s_prev)
            s_prev = u + sa * row(bb, t)
            s_all_ref[t] = s_prev
            sa_all_ref[t] = sa.astype(BF16)
        emit_o(C - 1, s_prev)
        s_scr[...] = s_prev

    spec = _rows(C, RWKV_WIDTH)
    return pl.pallas_call(
        body, name="wkv_fwd", grid=(nc,),
        in_specs=[spec] * 6,
        out_specs=[spec, pl.BlockSpec((C, RWKV_HEAD, RWKV_WIDTH), lambda i: (i, 0, 0)),
                   pl.BlockSpec((C, RWKV_HEAD, RWKV_WIDTH), lambda i: (i, 0, 0))],
        out_shape=[jax.ShapeDtypeStruct((T, RWKV_WIDTH), F32), jax.ShapeDtypeStruct((T, RWKV_HEAD, RWKV_WIDTH), F32),
                   jax.ShapeDtypeStruct((T, RWKV_HEAD, RWKV_WIDTH), BF16)],
        scratch_shapes=[pltpu.VMEM((RWKV_HEAD, RWKV_WIDTH), F32)],
        compiler_params=_params(dimension_semantics=("arbitrary",)),
    )(r, w, k, v, kap, a)


def _wkv_bwd(r, w, k, v, kap, a, s_all, sa_all, d_o):
    T = r.shape[0]
    C = WKV_CHUNK
    nc = T // C

    def body(r_ref, w_ref, k_ref, v_ref, kap_ref, a_ref, s_ref, s_before_ref, sa_ref, do_ref,
             dr_ref, dw_ref, dk_ref, dv_ref, dkap_ref, da_ref, ds_scr):
        first_chunk = pl.program_id(0) == nc - 1

        @pl.when(pl.program_id(0) == 0)
        def _():
            ds_scr[...] = jnp.zeros_like(ds_scr)

        diag, ones = _wkv_consts()
        rr, ww, kk, vv, kap_, aa, dd = (ref[...] for ref in (r_ref, w_ref, k_ref, v_ref, kap_ref, a_ref, do_ref))
        bb = kap_ * aa
        e1 = _head_sums(pltpu.roll(kap_, C - 1, axis=0) * bb, ones)
        row = lambda x, t: x[t:t + 1]

        def state_before(t):
            return s_ref[t - 1] if t > 0 else jnp.where(first_chunk, 0.0, s_before_ref[0])

        v_cols, do_cols = vv.T, dd.T

        d_sn, dsa, rows = None, None, [None] * C

        def emit_rows(t, d_sn_t, dsa_t):
            s_prev, dof = state_before(t), _expand_cols(do_cols, t)
            dv = _colsum(_lane_sum_bf(d_sn_t * row(kk, t), ones) * diag)
            db = _colsum(d_sn_t * sa_ref[t].astype(F32))
            rows[t] = (_colsum(s_ref[t] * dof), _colsum(d_sn_t * s_prev), _colsum(d_sn_t * _expand_cols(v_cols, t)), dv,
                       db * row(aa, t) - _colsum(dsa_t * s_prev), db * row(kap_, t))
            if t % 8 == 0:
                for j, ref in enumerate((dr_ref, dw_ref, dk_ref, dv_ref, dkap_ref, da_ref)):
                    ref[t:t + 8, :] = jnp.concatenate([rows[u][j] for u in range(t, t + 8)], axis=0)

        for t in reversed(range(C)):
            dof = _expand_cols(do_cols, t)
            if t == C - 1:
                d_sn = ds_scr[...] + dof * row(rr, t)
                dsa = _lane_sum(d_sn * row(bb, t), ones)
            else:
                v_t = d_sn * row(ww, t + 1) + dof * row(rr, t)
                ls = _lane_sum(v_t * row(bb, t), ones)
                emit_rows(t + 1, d_sn, dsa)
                d_sn = v_t - dsa * row(kap_, t + 1)
                dsa = ls - dsa * row(e1, t)
        emit_rows(0, d_sn, dsa)
        d_s = d_sn * row(ww, 0) - dsa * row(kap_, 0)
        ds_scr[...] = d_s

    spec = pl.BlockSpec((C, RWKV_WIDTH), lambda i: (nc - 1 - i, 0))
    states = pl.BlockSpec((C, RWKV_HEAD, RWKV_WIDTH), lambda i: (nc - 1 - i, 0, 0))
    before = pl.BlockSpec((1, RWKV_HEAD, RWKV_WIDTH), lambda i: (jnp.maximum((nc - 1 - i) * C - 1, 0), 0, 0))
    return pl.pallas_call(
        body, name="wkv_bwd", grid=(nc,),
        in_specs=[spec] * 6 + [states, before, states, spec],
        out_specs=[spec] * 6,
        out_shape=[jax.ShapeDtypeStruct((T, RWKV_WIDTH), F32)] * 6,
        scratch_shapes=[pltpu.VMEM((RWKV_HEAD, RWKV_WIDTH), F32)],
        compiler_params=_params(dimension_semantics=("arbitrary",)),
    )(r, w, k, v, kap, a, s_all, s_all, sa_all, d_o)


W = RWKV_WIDTH


def _softplus(y):
    return jnp.maximum(y, 0.0) + jnp.log(1.0 + jnp.exp(-jnp.abs(y)))


def _prep_fn(kr, xwa, w0, a0, k_k, k_a, wup_pad, aup_pad, ones64):
    w_log = -_softplus(-(w0 + _mm(jnp.tanh(xwa), wup_pad))) - 0.5
    decay = jnp.exp(-jnp.exp(w_log))
    a = jax.nn.sigmoid(a0 + _mm(xwa, aup_pad))
    kk = kr * k_k
    kap = kk / jnp.maximum(jnp.sqrt(_head_mix(kk * kk, ones64)), 1e-12)
    k = kr * (1.0 + (a - 1.0) * k_a)
    return decay, k, kap, a


def _shift_down(p, first_row):
    rows = lax.broadcasted_iota(jnp.int32, p.shape, 0)
    return jnp.where(rows == 0, first_row, pltpu.roll(p, 1, axis=0))


def _shift_up(z, last_row):
    n = z.shape[0]
    rows = lax.broadcasted_iota(jnp.int32, z.shape, 0)
    return jnp.where(rows == n - 1, last_row, pltpu.roll(z, n - 1, axis=0))


def _prev_block_spec():
    return pl.BlockSpec((8, RWKV_COLS), lambda i: (jnp.maximum(i * (TOK_TILE // 8) - 1, 0), 0))


def _mixed(p_ref, prev8_ref, mu_ref, first_tile):
    p = p_ref[...]
    first_row = jnp.where(first_tile, 0.0, prev8_ref[7:8, :])
    prev = _shift_down(p, first_row)
    return p, prev, p + mu_ref[...] * (prev - p)


def _prep_fwd(p_rwkv, mu, w0, a0, k_k, k_a, wup_pad, aup_pad, ones64):
    T = p_rwkv.shape[0]

    def body(p_ref, prev8_ref, mu_ref, w0_ref, a0_ref, kk_ref, ka_ref, wup_ref, aup_ref, ones_ref,
             r_ref, w_ref, k_ref, v_ref, kap_ref, a_ref, g_ref):
        _, _, ps = _mixed(p_ref, prev8_ref, mu_ref, pl.program_id(0) == 0)
        decay, k, kap, a = _prep_fn(ps[:, W:2 * W], ps[:, 4 * W:], w0_ref[...], a0_ref[...], kk_ref[...], ka_ref[...],
                                    wup_ref[...], aup_ref[...], ones_ref[...])
        r_ref[...] = ps[:, 0:W]
        w_ref[...] = decay
        k_ref[...] = k
        v_ref[...] = ps[:, 2 * W:3 * W]
        kap_ref[...] = kap
        a_ref[...] = a
        g_ref[...] = ps[:, 3 * W:4 * W]

    vec = _full((1, W))
    return pl.pallas_call(
        body, name="prep_fwd", grid=(T // TOK_TILE,),
        in_specs=[_rows(TOK_TILE, RWKV_COLS), _prev_block_spec(), _full((1, RWKV_COLS)), vec, vec, vec, vec,
                  _full((2 * LORA, W)), _full((2 * LORA, W)), _full((256, 128))],
        out_specs=[_rows(TOK_TILE, W)] * 7,
        out_shape=[jax.ShapeDtypeStruct((T, W), F32)] * 7,
        compiler_params=_params(dimension_semantics=("arbitrary",)),
    )(p_rwkv, p_rwkv, mu, w0, a0, k_k, k_a, wup_pad, aup_pad, ones64)


def _prep_bwd(p_rwkv, mu, w0, a0, k_k, k_a, wup_pad, aup_pad, ones64, dr, dw, dk, dv, dkap, da, dg, dr2, dk2, dv2):
    T = p_rwkv.shape[0]
    nt = T // TOK_TILE

    def body(p_ref, prev8_ref, mu_ref, w0_ref, a0_ref, kk_ref, ka_ref, wup_ref, aup_ref, ones_ref,
             dr_ref, dw_ref, dk_ref, dv_ref, dkap_ref, da_ref, dg_ref, dr2_ref, dk2_ref, dv2_ref,
             dp_ref, dmu_ref, dw0_ref, da0_ref, dkk_ref, dka_ref, dwup_ref, daup_ref, zrow_scr):
        i = pl.program_id(0)
        accs = (dmu_ref, dw0_ref, da0_ref, dkk_ref, dka_ref, dwup_ref, daup_ref)

        @pl.when(i == 0)
        def _():
            zrow_scr[...] = jnp.zeros_like(zrow_scr)
            for ref in accs:
                ref[...] = jnp.zeros_like(ref)

        p, prev, ps = _mixed(p_ref, prev8_ref, mu_ref, i == nt - 1)
        ones = ones_ref[...]
        _, vjp = jax.vjp(lambda *args: _prep_fn(*args, ones), ps[:, W:2 * W], ps[:, 4 * W:], w0_ref[...], a0_ref[...],
                         kk_ref[...], ka_ref[...], wup_ref[...], aup_ref[...])
        dkr, dxwa, dw0, da0, dkk, dka, dwup, daup = vjp(
            (dw_ref[...], dk_ref[...] + dk2_ref[...], dkap_ref[...], da_ref[...]))
        dps = jnp.concatenate([dr_ref[...] + dr2_ref[...], dkr, dv_ref[...] + dv2_ref[...], dg_ref[...], dxwa], axis=1)
        z = dps * mu_ref[...]
        dp_ref[...] = dps - z + _shift_up(z, zrow_scr[0:1, :])
        zrow_scr[0:1, :] = z[0:1, :]
        for ref, val in zip(accs, (_colsum(dps * (prev - p)), dw0, da0, dkk, dka, dwup, daup)):
            ref[...] += val

    rev = lambda i: (nt - 1 - i, 0)
    vec = _full((1, W))
    lora = _full((2 * LORA, W))
    tile = pl.BlockSpec((TOK_TILE, W), rev)
    prev8 = pl.BlockSpec((8, RWKV_COLS), lambda i: (jnp.maximum((nt - 1 - i) * (TOK_TILE // 8) - 1, 0), 0))
    return pl.pallas_call(
        body, name="prep_bwd", grid=(nt,),
        in_specs=[pl.BlockSpec((TOK_TILE, RWKV_COLS), rev), prev8, _full((1, RWKV_COLS)), vec, vec, vec, vec, lora, lora,
                  _full((256, 128))] + [tile] * 10,
        out_specs=[pl.BlockSpec((TOK_TILE, RWKV_COLS), rev), _full((1, RWKV_COLS)), vec, vec, vec, vec, lora, lora],
        out_shape=[jax.ShapeDtypeStruct((T, RWKV_COLS), F32), jax.ShapeDtypeStruct((1, RWKV_COLS), F32)]
        + [jax.ShapeDtypeStruct((1, W), F32)] * 4 + [jax.ShapeDtypeStruct((2 * LORA, W), F32)] * 2,
        scratch_shapes=[pltpu.VMEM((8, RWKV_COLS), F32)],
        compiler_params=_params(dimension_semantics=("arbitrary",)),
    )(p_rwkv, p_rwkv, mu, w0, a0, k_k, k_a, wup_pad, aup_pad, ones64, dr, dw, dk, dv, dkap, da, dg, dr2, dk2, dv2)


def _silu(x):
    return x * jax.nn.sigmoid(x)


def _post_y(o, r, k, v, g_rw, ret_raw, g_ret, ret_gn_g, gn_g, gn_b, r_k, avg128, avg64, ones64):
    xc = ret_raw - _head_mix(ret_raw, avg128)
    ret = xc * lax.rsqrt(_head_mix(xc * xc, avg128) + RET_GN_EPS)
    y_ret = _silu(g_ret) * (ret * ret_gn_g)
    oc = o - _head_mix(o, avg64)
    on = oc * lax.rsqrt(_head_mix(oc * oc, avg64) + RWKV_GN_EPS) * gn_g + gn_b
    bonus = _head_mix(r * k * r_k, ones64) * v
    y_rwkv = _silu(g_rw) * (on + bonus)
    return y_ret, y_rwkv


def _post_loss(h, final_g, target):
    err = _rmsnorm(h, final_g) - target
    return 0.5 * jnp.sum(jnp.mean(err * err, axis=-1))


def _post(o, r, k, v, g_rw, ret_raw, p_ret, x, target, ret_gn_g, gn_g, gn_b, r_k, final_g, w_out, avg128, avg64, ones64):
    T = x.shape[0]
    n_tok_out = 8

    def body(o_ref, r_ref, k_ref, v_ref, grw_ref, ret_ref, gret_ref, x_ref, tgt_ref, rg_ref, gg_ref, gb_ref, rk_ref, fg_ref,
             wo_ref, a128_ref, a64_ref, ones_ref, *outs):
        tok_outs, (dwo_ref, drg_ref, dgg_ref, dgb_ref, drk_ref, dfg_ref, loss_ref) = outs[:n_tok_out], outs[n_tok_out:]
        accs = (dwo_ref, drg_ref, dgg_ref, dgb_ref, drk_ref, dfg_ref, loss_ref)

        @pl.when(pl.program_id(0) == 0)
        def _():
            for ref in accs:
                ref[...] = jnp.zeros_like(ref)

        consts = (a128_ref[...], a64_ref[...], ones_ref[...])
        (y_ret, y_rwkv), vjp = jax.vjp(
            lambda *args: _post_y(*args, *consts), o_ref[...], r_ref[...], k_ref[...], v_ref[...], grw_ref[...], ret_ref[...],
            gret_ref[...], rg_ref[...], gg_ref[...], gb_ref[...], rk_ref[...])
        h = x_ref[...] + _dot_bf(y_ret, wo_ref[0:RET_WIDTH, :]) + _dot_bf(y_rwkv, wo_ref[RET_WIDTH:, :])
        loss, (dh, dfg) = jax.value_and_grad(_post_loss, argnums=(0, 1))(h, fg_ref[...], tgt_ref[...])
        dy_ret = _dot_nt_bf(dh, wo_ref[0:RET_WIDTH, :])
        dy_rwkv = _dot_nt_bf(dh, wo_ref[RET_WIDTH:, :])
        do, dr, dk, dv, dgrw, dret, dgret, drg, dgg, dgb, drk = vjp((dy_ret, dy_rwkv))
        for ref, val in zip(tok_outs, (dh, do, dr, dk, dv, dgrw, dret, dgret)):
            ref[...] = val
        dwo_ref[0:RET_WIDTH, :] += _dot_tn_bf(y_ret, dh)
        dwo_ref[RET_WIDTH:, :] += _dot_tn_bf(y_rwkv, dh)
        for ref, val in zip(accs[1:], (drg, dgg, dgb, drk, dfg, jnp.full((1, 128), loss, F32))):
            ref[...] += val

    tile = _rows(TOK_TILE, W)
    wide = _rows(TOK_TILE, D_MODEL)
    vec = _full((1, W))
    sq = _full((256, 128))
    return pl.pallas_call(
        body, name="post", grid=(T // TOK_TILE,),
        in_specs=[tile] * 6 + [pl.BlockSpec((TOK_TILE, W), lambda i: (i, 2)), wide, wide, vec, vec, vec, vec,
                               _full((1, D_MODEL)), _full((D_MODEL, D_MODEL)), sq, sq, sq],
        out_specs=[wide] + [tile] * 7 + [_full((D_MODEL, D_MODEL)), vec, vec, vec, vec, _full((1, D_MODEL)), _full((1, 128))],
        out_shape=[jax.ShapeDtypeStruct((T, D_MODEL), F32)] + [jax.ShapeDtypeStruct((T, W), F32)] * 7
        + [jax.ShapeDtypeStruct((D_MODEL, D_MODEL), F32)] + [jax.ShapeDtypeStruct((1, W), F32)] * 4
        + [jax.ShapeDtypeStruct((1, D_MODEL), F32), jax.ShapeDtypeStruct((1, 128), F32)],
        compiler_params=_params(dimension_semantics=("arbitrary",)),
    )(o, r, k, v, g_rw, ret_raw, p_ret, x, target, ret_gn_g, gn_g, gn_b, r_k, final_g, w_out, avg128, avg64, ones64)


def _inproj_bwd_x(x, norm_g, dp_qkv, dg_ret, dp_rwkv, dh, w_in):
    T = x.shape[0]
    n_qkv = 2 * RET_QK + RET_WIDTH

    def body(x_ref, g_ref, dqkv_ref, dgret_ref, drwkv_ref, dh_ref, w_ref, dx_ref, dg_ref):
        @pl.when(pl.program_id(0) == 0)
        def _():
            dg_ref[...] = jnp.zeros_like(dg_ref)

        _, vjp = jax.vjp(_rmsnorm, x_ref[...], g_ref[...])
        du = (_dot_nt_bf(dqkv_ref[...], w_ref[:, 0:n_qkv]) + _dot_nt_bf(dgret_ref[...], w_ref[:, n_qkv:RET_COLS])
              + _dot_nt_bf(drwkv_ref[...], w_ref[:, RET_COLS:]))
        dx, dg = vjp(du)
        dx_ref[...] = dx + dh_ref[...]
        dg_ref[...] += dg

    return pl.pallas_call(
        body, name="inproj_bwd_x", grid=(T // TOK_TILE,),
        in_specs=[_rows(TOK_TILE, D_MODEL), _full((1, D_MODEL)), _rows(TOK_TILE, n_qkv), _rows(TOK_TILE, RET_WIDTH),
                  _rows(TOK_TILE, RWKV_COLS), _rows(TOK_TILE, D_MODEL), _full((D_MODEL, IN_COLS))],
        out_specs=[_rows(TOK_TILE, D_MODEL), _full((1, D_MODEL))],
        out_shape=[jax.ShapeDtypeStruct((T, D_MODEL), F32), jax.ShapeDtypeStruct((1, D_MODEL), F32)],
        compiler_params=_params(dimension_semantics=("arbitrary",)),
    )(x, norm_g, dp_qkv, dg_ret, dp_rwkv, dh, w_in)


def _grad_w(name, u, dp):
    T, n = dp.shape
    tile = 2 * TOK_TILE

    def body(u_ref, dp_ref, out_ref):
        @pl.when(pl.program_id(0) == 0)
        def _():
            out_ref[...] = jnp.zeros_like(out_ref)

        out_ref[...] += _dot_tn_bf(u_ref[...], dp_ref[...])

    return pl.pallas_call(
        body, name=name, grid=(T // tile,),
        in_specs=[_rows(tile, D_MODEL), _rows(tile, n)],
        out_specs=_full((D_MODEL, n)),
        out_shape=jax.ShapeDtypeStruct((D_MODEL, n), F32),
        compiler_params=_params(dimension_semantics=("arbitrary",)),
    )(u, dp)


def _pad_lora(w_up, top):
    z = jnp.zeros_like(w_up)
    return jnp.concatenate([w_up, z] if top else [z, w_up], axis=0)


def _local_grads(x, target, norm_g, w_in_bf, ret_gn_g, mu, w_lora_up, w0, a_lora_up, a0, k_k, k_a, r_k, gn_g, gn_b,
                 w_out_bf, final_g):
    T = x.shape[0]
    tabs = _rope_tables(T) + _ret_tables()
    ones64 = _block_mix(128, RWKV_HEAD)
    avg64 = _block_mix(128, RWKV_HEAD, 1.0 / RWKV_HEAD)
    avg128 = _block_mix(128, RET_DV, 1.0 / RET_DV)
    wup_pad, aup_pad = _pad_lora(w_lora_up, True), _pad_lora(a_lora_up, False)

    p_ret, p_rwkv, u = _inproj(x, norm_g, w_in_bf)
    ret_raw, s_saved = _ret_fwd(p_ret, tabs)
    r, w, k, v, kap, a, g_rw = _prep_fwd(p_rwkv, mu, w0, a0, k_k, k_a, wup_pad, aup_pad, ones64)
    o, s_all, sa_all = _wkv_fwd(r, w, k, v, kap, a)
    (dh, do, dr2, dk2, dv2, dgrw, dret, dgret, d_w_out, d_ret_gn_g, d_gn_g, d_gn_b, d_r_k, d_final_g, loss) = _post(
        o, r, k, v, g_rw, ret_raw, p_ret, x, target, ret_gn_g, gn_g, gn_b, r_k, final_g, w_out_bf, avg128, avg64, ones64)
    dr, dw, dk, dv, dkap, da = _wkv_bwd(r, w, k, v, kap, a, s_all, sa_all, do)
    dp_rwkv, d_mu, d_w0, d_a0, d_k_k, d_k_a, d_wup, d_aup = _prep_bwd(
        p_rwkv, mu, w0, a0, k_k, k_a, wup_pad, aup_pad, ones64, dr, dw, dk, dv, dkap, da, dgrw, dr2, dk2, dv2)
    dp_qkv = _ret_bwd(p_ret, s_saved, dret, tabs)
    dx, d_norm_g = _inproj_bwd_x(x, norm_g, dp_qkv, dgret, dp_rwkv, dh, w_in_bf)
    d_w_in = jnp.concatenate([_grad_w("grad_w_qkv", u, dp_qkv), _grad_w("grad_w_gret", u, dgret),
                              _grad_w("grad_w_rwkv", u, dp_rwkv)], axis=1)
    grads = dict(norm_g=d_norm_g, w_in=d_w_in, ret_gn_g=d_ret_gn_g, rwkv_mu=d_mu, w_lora_up=d_wup[:LORA], w0=d_w0,
                 a_lora_up=d_aup[LORA:], a0=d_a0, k_k=d_k_k, k_a=d_k_a, r_k=d_r_k, rwkv_gn_g=d_gn_g, rwkv_gn_b=d_gn_b,
                 w_out=d_w_out, final_norm_g=d_final_g)
    return loss, dx, grads


def _mesh_pos():
    return lax.axis_index("x"), lax.axis_index("y"), lax.axis_index("c")


def _all_gather(shards):
    n = len(shards)

    def body(*refs):
        x_refs, out_refs = refs[:n], refs[n:2 * n]
        send_sems, recv_sems, local_sems = refs[2 * n:]
        x, y, c = _mesh_pos()
        me, sibling = (x, y, c), (x, y, 1 - c)
        chips = [(1 - x, y), (x, 1 - y), (1 - x, 1 - y)]

        def rows(a, pos):
            m = x_refs[a].shape[0]
            return out_refs[a].at[pl.ds((4 * pos[0] + 2 * pos[1] + pos[2]) * m, m), :]

        def copy(a, k, block, to, src=None):
            return pltpu.make_async_remote_copy(
                src_ref=rows(a, block) if src is None else src, dst_ref=rows(a, block),
                send_sem=send_sems.at[a, k], recv_sem=recv_sems.at[a, k], device_id=to, device_id_type=MESH)

        mine = [pltpu.make_async_copy(x_refs[a], rows(a, me), local_sems.at[a]) for a in range(n)]
        for cp in mine:
            cp.start()
        first = []
        for a in range(n):
            first.append(copy(a, 0, me, sibling, src=x_refs[a]))
            first += [copy(a, 1 + j, me, (*chip, c), src=x_refs[a]) for j, chip in enumerate(chips)]
        for cp in first:
            cp.start()
        passed = []
        for j, chip in enumerate(chips):
            for a in range(n):
                copy(a, 1 + j, (*chip, c), me).wait_recv()
                passed.append(copy(a, 4 + j, (*chip, c), sibling))
                passed[-1].start()
        for a in range(n):
            copy(a, 0, sibling, me).wait_recv()
            for j, chip in enumerate(chips):
                copy(a, 4 + j, (*chip, 1 - c), me).wait_recv()
        for cp in first + passed:
            cp.wait_send()
        for cp in mine:
            cp.wait()

    vmem = pl.BlockSpec(memory_space=pltpu.VMEM)
    return pl.pallas_call(
        body, name="gather_weights",
        out_shape=[jax.ShapeDtypeStruct((N_DEV * s.shape[0], s.shape[1]), s.dtype) for s in shards],
        in_specs=[vmem] * n, out_specs=[vmem] * n,
        scratch_shapes=[pltpu.SemaphoreType.DMA((n, 7)), pltpu.SemaphoreType.DMA((n, 7)), pltpu.SemaphoreType.DMA((n,))],
        compiler_params=_params(),
    )(*shards)


N_CHIP = 4


def _exchange_pairs(big, small):
    nb, ns = len(big), len(small)

    def body(*refs):
        big_in, small_in = refs[:nb], refs[nb:nb + ns]
        theirs, small_out = refs[nb + ns:2 * nb + ns], refs[2 * nb + ns:2 * nb + 2 * ns]
        pair_send, pair_recv, send_sems, recv_sems, local_sems = refs[2 * nb + 2 * ns:]
        x, y, c = _mesh_pos()
        me = 4 * x + 2 * y + c
        local = [pltpu.make_async_copy(small_in[a].at[me], small_out[a].at[me], local_sems.at[a]) for a in range(ns)]
        for cp in local:
            cp.start()
        copies = [pltpu.make_async_remote_copy(
            src_ref=big_in[a], dst_ref=theirs[a], send_sem=pair_send.at[a], recv_sem=pair_recv.at[a],
            device_id=(x, y, 1 - c), device_id_type=MESH) for a in range(nb)]
        for k in range(1, N_DEV):
            peer = (x ^ (k >> 2), y ^ ((k >> 1) & 1), c ^ (k & 1))
            peer_idx = 4 * peer[0] + 2 * peer[1] + peer[2]
            copies += [pltpu.make_async_remote_copy(
                src_ref=small_in[a].at[peer_idx], dst_ref=small_out[a].at[me], send_sem=send_sems.at[a, k - 1],
                recv_sem=recv_sems.at[a, k - 1], device_id=peer, device_id_type=MESH) for a in range(ns)]
        for cp in copies:
            cp.start()
        for cp in copies:
            cp.wait()
        for cp in local:
            cp.wait()

    hbm = pl.BlockSpec(memory_space=pl.ANY)
    out_shape = [jax.ShapeDtypeStruct(p.shape, p.dtype) for p in big + small]
    dma = pltpu.SemaphoreType.DMA
    res = pl.pallas_call(
        body, name="exchange_pairs", out_shape=out_shape,
        in_specs=[hbm] * (nb + ns), out_specs=[hbm] * len(out_shape),
        scratch_shapes=[dma((nb,)), dma((nb,)), dma((ns, 7)), dma((ns, 7)), dma((ns,))],
        compiler_params=_params(),
    )(*big, *small)
    return res[:nb], res[nb:]


def _pair_sum(name, mine, theirs, row_tile):
    _, rows, cols = mine.shape

    def body(a_ref, b_ref, o_ref):
        o_ref[...] = (a_ref[...].astype(F32) + b_ref[...].astype(F32)).astype(o_ref.dtype)

    spec = pl.BlockSpec((N_CHIP, row_tile, cols), lambda i: (0, i, 0))
    return pl.pallas_call(
        body, name=name, grid=(rows // row_tile,), in_specs=[spec, spec], out_specs=spec,
        out_shape=jax.ShapeDtypeStruct(mine.shape, mine.dtype),
        compiler_params=_params(dimension_semantics=("arbitrary",)),
    )(mine, theirs)


def _exchange_chips(parts):
    n = len(parts)

    def body(*refs):
        in_refs, out_refs = refs[:n], refs[n:2 * n]
        send_sems, recv_sems, local_sems = refs[2 * n:]
        x, y, c = _mesh_pos()
        my_chip = 2 * x + y
        own = [pltpu.make_async_copy(in_refs[a].at[my_chip], out_refs[a].at[my_chip], local_sems.at[a]) for a in range(n)]
        for cp in own:
            cp.start()
        copies = []
        for k in range(1, N_CHIP):
            px, py = x ^ (k >> 1), y ^ (k & 1)
            copies += [pltpu.make_async_remote_copy(
                src_ref=in_refs[a].at[2 * px + py], dst_ref=out_refs[a].at[my_chip], send_sem=send_sems.at[a, k - 1],
                recv_sem=recv_sems.at[a, k - 1], device_id=(px, py, c), device_id_type=MESH) for a in range(n)]
        for cp in copies:
            cp.start()
        for cp in copies:
            cp.wait()
        for cp in own:
            cp.wait()

    hbm = pl.BlockSpec(memory_space=pl.ANY)
    dma = pltpu.SemaphoreType.DMA
    return pl.pallas_call(
        body, name="exchange_chips",
        out_shape=[jax.ShapeDtypeStruct(p.shape, p.dtype) for p in parts],
        in_specs=[hbm] * n, out_specs=[hbm] * n,
        scratch_shapes=[dma((n, N_CHIP - 1)), dma((n, N_CHIP - 1)), dma((n,))],
        compiler_params=_params(),
    )(*parts)


def _adamw(w, g, m, v):
    m = ADAM_B1 * m + (1.0 - ADAM_B1) * g
    v = ADAM_B2 * v + (1.0 - ADAM_B2) * (g * g)
    m_hat = m / (1.0 - ADAM_B1 ** ADAM_STEP)
    v_hat = v / (1.0 - ADAM_B2 ** ADAM_STEP)
    return -ADAM_LR * (m_hat / (jnp.sqrt(v_hat) + ADAM_EPS) + ADAM_WD * w), m, v


def _reduce_adamw(name, parts, w, m, v, row_tile):
    n_parts, rows, cols = parts.shape

    def body(p_ref, w_ref, m_ref, v_ref, g_ref, d_ref, nm_ref, nv_ref):
        g = p_ref[0].astype(F32)
        for s in range(1, n_parts):
            g = g + p_ref[s].astype(F32)
        g_ref[0] = g
        d_ref[0], nm_ref[0], nv_ref[0] = _adamw(w_ref[0], g, m_ref[0], v_ref[0])

    tile = pl.BlockSpec((1, row_tile, cols), lambda i: (0, i, 0))
    return pl.pallas_call(
        body, name=name, grid=(rows // row_tile,),
        in_specs=[pl.BlockSpec((n_parts, row_tile, cols), lambda i: (0, i, 0)), tile, tile, tile],
        out_specs=[tile] * 4,
        out_shape=[jax.ShapeDtypeStruct((1, rows, cols), F32)] * 4,
        compiler_params=_params(dimension_semantics=("arbitrary",)),
    )(parts, w, m, v)


_SMALL = (("norm_g", 1024), ("ret_gn_g", 512), ("rwkv_mu", 2176), ("w0", 512), ("a0", 512), ("k_k", 512), ("k_a", 512),
          ("r_k", 512), ("rwkv_gn_g", 512), ("rwkv_gn_b", 512), ("final_norm_g", 1024))
_SMALL_LANES = sum(n for _, n in _SMALL) + 128
_WEIGHTS = ("norm_g", "w_in", "ret_gn_g", "rwkv_mu", "w_lora_up", "w0", "a_lora_up", "a0", "k_k", "k_a", "r_k", "rwkv_gn_g",
            "rwkv_gn_b", "w_out", "final_norm_g")


def _adamw_vectors(parts, wts, mom, var):
    k = len(_SMALL)

    def body(p_ref, *refs):
        w_refs, m_refs, v_refs, outs = refs[:k], refs[k:2 * k], refs[2 * k:3 * k], refs[3 * k:]
        g_all = p_ref[0]
        for s in range(1, N_DEV):
            g_all = g_all + p_ref[s]
        off = 0
        for i, (name, n) in enumerate(_SMALL):
            g = g_all[:, off:off + n]
            off += n
            if name == "r_k":
                g = jnp.concatenate([g[:, RWKV_HEAD * h:RWKV_HEAD * (h + 1)] for h in range(RWKV_HEADS)], axis=0)[None]
            outs[4 * i][...] = g
            outs[4 * i + 1][...], outs[4 * i + 2][...], outs[4 * i + 3][...] = _adamw(
                w_refs[i][...], g, m_refs[i][...], v_refs[i][...])
        outs[4 * k][...] = g_all[:, off:off + 128]

    vmem = pl.BlockSpec(memory_space=pltpu.VMEM)
    shapes = [jax.ShapeDtypeStruct(wts[n].shape, F32) for n, _ in _SMALL for _ in range(4)] + [jax.ShapeDtypeStruct((1, 128), F32)]
    res = pl.pallas_call(
        body, name="adamw_vectors", out_shape=shapes,
        in_specs=[vmem] * (1 + 3 * k), out_specs=[vmem] * len(shapes), compiler_params=_params(),
    )(parts, *[wts[n] for n, _ in _SMALL], *[mom[n] for n, _ in _SMALL], *[var[n] for n, _ in _SMALL])
    return {n: res[4 * i:4 * i + 4] for i, (n, _) in enumerate(_SMALL)}, res[4 * k]


def kernel(x, norm_g, w_in, ret_gn_g, rwkv_mu, w_lora_up, w0, a_lora_up, a0, k_k, k_a, r_k, rwkv_gn_g, rwkv_gn_b, w_out, final_norm_g, loss_target, m_norm_g, m_w_in, m_ret_gn_g, m_rwkv_mu, m_w_lora_up, m_w0, m_a_lora_up, m_a0, m_k_k, m_k_a, m_r_k, m_rwkv_gn_g, m_rwkv_gn_b, m_w_out, m_final_norm_g, v_norm_g, v_w_in, v_ret_gn_g, v_rwkv_mu, v_w_lora_up, v_w0, v_a_lora_up, v_a0, v_k_k, v_k_a, v_r_k, v_rwkv_gn_g, v_rwkv_gn_b, v_w_out, v_final_norm_g):
    wts = dict(norm_g=norm_g, w_in=w_in, ret_gn_g=ret_gn_g, rwkv_mu=rwkv_mu, w_lora_up=w_lora_up, w0=w0, a_lora_up=a_lora_up,
               a0=a0, k_k=k_k, k_a=k_a, r_k=r_k, rwkv_gn_g=rwkv_gn_g, rwkv_gn_b=rwkv_gn_b, w_out=w_out,
               final_norm_g=final_norm_g)
    mom = dict(norm_g=m_norm_g, w_in=m_w_in, ret_gn_g=m_ret_gn_g, rwkv_mu=m_rwkv_mu, w_lora_up=m_w_lora_up, w0=m_w0,
               a_lora_up=m_a_lora_up, a0=m_a0, k_k=m_k_k, k_a=m_k_a, r_k=m_r_k, rwkv_gn_g=m_rwkv_gn_g,
               rwkv_gn_b=m_rwkv_gn_b, w_out=m_w_out, final_norm_g=m_final_norm_g)
    var = dict(norm_g=v_norm_g, w_in=v_w_in, ret_gn_g=v_ret_gn_g, rwkv_mu=v_rwkv_mu, w_lora_up=v_w_lora_up, w0=v_w0,
               a_lora_up=v_a_lora_up, a0=v_a0, k_k=v_k_k, k_a=v_k_a, r_k=v_r_k, rwkv_gn_g=v_rwkv_gn_g,
               rwkv_gn_b=v_rwkv_gn_b, w_out=v_w_out, final_norm_g=v_final_norm_g)
    shapes = {n: wts[n].shape for n in _WEIGHTS}

    g_in, g_out, g_wup, g_aup = _all_gather(
        [w_in[0].astype(BF16), w_out[0].astype(BF16), w_lora_up[0], a_lora_up[0]])
    w_in_bf = g_in.reshape(N_DEV, D_MODEL, SHARD_IN).transpose(1, 0, 2).reshape(D_MODEL, IN_COLS)
    wup_full = g_wup.reshape(N_DEV, LORA, SHARD_LORA).transpose(1, 0, 2).reshape(LORA, W)
    aup_full = g_aup.reshape(N_DEV, LORA, SHARD_LORA).transpose(1, 0, 2).reshape(LORA, W)

    loss, dx, g = _local_grads(
        x[0], loss_target[0], norm_g, w_in_bf, ret_gn_g, rwkv_mu, wup_full, w0, aup_full, a0, k_k, k_a,
        r_k.reshape(1, W), rwkv_gn_g, rwkv_gn_b, g_out, final_norm_g.reshape(1, D_MODEL))

    small = jnp.concatenate([g[n] for n, _ in _SMALL] + [loss], axis=1)
    core = lax.axis_index("c")
    by_core = lambda t: [lax.dynamic_index_in_dim(t, i, axis=1, keepdims=False) for i in (core, 1 - core)]
    in_mine, in_sib = by_core(g["w_in"].reshape(D_MODEL, N_CHIP, 2, SHARD_IN).transpose(1, 2, 0, 3).astype(BF16))
    out_mine, out_sib = by_core(g["w_out"].reshape(N_CHIP, 2, SHARD_OUT, D_MODEL).astype(BF16))
    (in_theirs, out_theirs), parts = _exchange_pairs(
        [in_sib, out_sib],
        [g["w_lora_up"].reshape(LORA, N_DEV, SHARD_LORA).transpose(1, 0, 2),
         g["a_lora_up"].reshape(LORA, N_DEV, SHARD_LORA).transpose(1, 0, 2),
         jnp.broadcast_to(small[None], (N_DEV, 1, _SMALL_LANES))])
    by_chip = _exchange_chips([_pair_sum("pair_sum_w_in", in_mine, in_theirs, 256),
                               _pair_sum("pair_sum_w_out", out_mine, out_theirs, SHARD_OUT)])
    res = {}
    res["w_in"] = _reduce_adamw("adamw_w_in", by_chip[0], w_in, m_w_in, v_w_in, 256)
    res["w_out"] = _reduce_adamw("adamw_w_out", by_chip[1], w_out, m_w_out, v_w_out, SHARD_OUT)
    res["w_lora_up"] = _reduce_adamw("adamw_w_lora_up", parts[0], w_lora_up, m_w_lora_up, v_w_lora_up, LORA)
    res["a_lora_up"] = _reduce_adamw("adamw_a_lora_up", parts[1], a_lora_up, m_a_lora_up, v_a_lora_up, LORA)
    as_row = lambda d: {n: d[n] if d[n].ndim > 1 else d[n].reshape(1, size) for n, size in _SMALL}
    vec, loss_row = _adamw_vectors(parts[2], as_row(wts), as_row(mom), as_row(var))
    res.update(vec)
    res = {n: [t.reshape(shapes[n]) for t in res[n]] for n in _WEIGHTS}
    return (loss_row[0, 0], dx[None], *[res[n][0] for n in _WEIGHTS], *[res[n][1] for n in _WEIGHTS],
            *[res[n][2] for n in _WEIGHTS], *[res[n][3] for n in _WEIGHTS])
```

```python
import functools

import numpy as np
import jax
import jax.numpy as jnp
from jax import lax
from jax.experimental import pallas as pl
from jax.experimental.pallas import tpu as pltpu

F32 = jnp.float32
BF16 = jnp.bfloat16

D_MODEL = 1024
CHUNK = 64
RET_HEADS = 4
RET_DV = 128
RET_DK = 64
RET_QK = 256
RET_WIDTH = 512
RWKV_WIDTH = 512
RWKV_HEAD = 64
RWKV_HEADS = 8
LORA = 64
RET_COLS = 2 * RET_QK + 2 * RET_WIDTH
RWKV_COLS = 4 * RWKV_WIDTH + 2 * LORA
IN_COLS = RET_COLS + RWKV_COLS
ROPE_BASE = 10000.0
RMS_EPS = 1e-6
RET_GN_EPS = 1e-5
RWKV_GN_EPS = 64e-5
ADAM_LR = 0.001
ADAM_B1 = 0.9
ADAM_B2 = 0.999
ADAM_EPS = 1e-08
ADAM_WD = 0.01
ADAM_STEP = 10
N_DEV = 8
SHARD_IN = IN_COLS // N_DEV
SHARD_OUT = D_MODEL // N_DEV
SHARD_LORA = RWKV_WIDTH // N_DEV
VMEM_LIMIT = 56 * 1024 * 1024
TOK_TILE = 256
WKV_CHUNK = 64

MESH = pl.DeviceIdType.MESH


def _dot_bf(a, b):
    return jnp.dot(a.astype(BF16), b.astype(BF16), preferred_element_type=F32)


def _dot_nt_bf(a, b):
    return lax.dot_general(a.astype(BF16), b.astype(BF16), (((1,), (1,)), ((), ())), preferred_element_type=F32)


def _dot_tn_bf(a, b):
    return lax.dot_general(a.astype(BF16), b.astype(BF16), (((0,), (0,)), ((), ())), preferred_element_type=F32)


@jax.custom_vjp
def _mm(a, b):
    return _dot_bf(a, b)


@jax.custom_vjp
def _mm_nt(a, b):
    return _dot_nt_bf(a, b)


@jax.custom_vjp
def _mm_tn(a, b):
    return _dot_tn_bf(a, b)


_mm.defvjp(lambda a, b: (_dot_bf(a, b), (a, b)), lambda res, g: (_dot_nt_bf(g, res[1]), _dot_tn_bf(res[0], g)))
_mm_nt.defvjp(lambda a, b: (_dot_nt_bf(a, b), (a, b)), lambda res, g: (_dot_bf(g, res[1]), _dot_tn_bf(g, res[0])))
_mm_tn.defvjp(lambda a, b: (_dot_tn_bf(a, b), (a, b)), lambda res, g: (_dot_nt_bf(res[1], g), _dot_bf(res[0], g)))


def _trunc(x):
    return lax.bitcast_convert_type(lax.bitcast_convert_type(x, jnp.uint32) & jnp.uint32(0xFFFF0000), F32)


def _two_piece(x):
    hi = _trunc(x)
    return jnp.concatenate([hi, x - hi], axis=1)


def _mix_raw(x, mat2):
    return _unstack(jnp.dot(_two_piece(_stack(x)), mat2, preferred_element_type=F32))


@jax.custom_vjp
def _head_mix(x, mat2):
    return _mix_raw(x, mat2)


_head_mix.defvjp(lambda x, mat2: (_mix_raw(x, mat2), mat2), lambda mat2, g: (_mix_raw(g, mat2), jnp.zeros_like(mat2)))


def _swap_halves(x):
    lane = lax.broadcasted_iota(jnp.int32, x.shape, 1)
    return jnp.where((lane & (RET_DK - 1)) < RET_DK // 2, pltpu.roll(x, RET_QK - RET_DK // 2, axis=1),
                     pltpu.roll(x, RET_DK // 2, axis=1))


@jax.custom_vjp
def _rot(x):
    return _swap_halves(x)


_rot.defvjp(lambda x: (_swap_halves(x), None), lambda _, g: (_swap_halves(g),))


def _params(**kw):
    return pltpu.CompilerParams(vmem_limit_bytes=VMEM_LIMIT, **kw)


def _full(shape):
    nd = len(shape)
    return pl.BlockSpec(shape, lambda i, _nd=nd: (0,) * _nd)


def _rows(tile, width):
    return pl.BlockSpec((tile, width), lambda i: (i, 0))


def _block_mix(n, blk, scale=1.0):
    idx = np.arange(n) // blk
    m = (idx[:, None] == idx[None, :]).astype(np.float32) * scale
    return jnp.asarray(np.concatenate([m, m], axis=0))


def _rope_tables(T):
    half = RET_DK // 2
    expo = -np.arange(half, dtype=np.float32) / np.float32(half)
    freqs = np.exp(expo * np.float32(np.log(ROPE_BASE))).astype(np.float32)
    ang = np.arange(T, dtype=np.float32)[:, None] * freqs[None, :]
    cos, sin = np.cos(ang).astype(np.float32), np.sin(ang).astype(np.float32)
    cos_h = np.concatenate([cos, cos], axis=1)
    sin_h = np.concatenate([-sin, sin], axis=1)
    cos_t = np.tile(cos_h, (1, RET_HEADS))
    sin_t = np.tile(sin_h, (1, RET_HEADS))
    return jnp.asarray(cos_t), jnp.asarray(sin_t)


def _ret_tables():
    h = np.arange(RET_HEADS, dtype=np.float32)
    lg = np.log(1.0 - np.exp2(-5.0 - h)).astype(np.float32)
    idx = np.arange(CHUNK, dtype=np.float32)
    intra = np.exp(lg[:, None, None] * np.abs(idx[:, None] - idx[None, :])).astype(np.float32)
    q_dec = np.exp(lg[:, None] * (idx[None, :] + 1.0)).astype(np.float32)
    k_dec = np.exp(lg[:, None] * (CHUNK - 1.0 - idx[None, :])).astype(np.float32)
    chunk_dec = np.exp(lg * CHUNK).astype(np.float32)
    lane_head = np.arange(RET_QK) // RET_DK
    mask = (lane_head[None, :] == np.arange(RET_HEADS)[:, None]).astype(np.float32)
    m = np.broadcast_to(mask[:, None, :], (RET_HEADS, CHUNK, RET_QK)).copy()
    qd = m * q_dec[:, :, None]
    kd = m * k_dec[:, :, None]
    return jnp.asarray(intra), jnp.asarray(m), jnp.asarray(qd), jnp.asarray(kd), [float(c) for c in chunk_dec]


def _rmsnorm(x, g):
    return x * lax.rsqrt(jnp.mean(x * x, axis=-1, keepdims=True) + RMS_EPS) * g


def _inproj(x, norm_g, w_in_t):
    T = x.shape[0]

    def body(x_ref, g_ref, w_ref, pr_ref, pw_ref, u_ref):
        ub = _rmsnorm(x_ref[...], g_ref[...]).astype(BF16)
        u_ref[...] = ub
        pr_ref[...] = _dot_nt_bf(ub, w_ref[:RET_COLS, :])
        pw_ref[...] = _dot_nt_bf(ub, w_ref[RET_COLS:, :])

    return pl.pallas_call(
        body, name="inproj", grid=(T // TOK_TILE,),
        in_specs=[_rows(TOK_TILE, D_MODEL), _full((1, D_MODEL)), _full((IN_COLS, D_MODEL))],
        out_specs=[_rows(TOK_TILE, RET_COLS), _rows(TOK_TILE, RWKV_COLS), _rows(TOK_TILE, D_MODEL)],
        out_shape=[jax.ShapeDtypeStruct((T, RET_COLS), F32), jax.ShapeDtypeStruct((T, RWKV_COLS), F32),
                   jax.ShapeDtypeStruct((T, D_MODEL), BF16)],
        compiler_params=_params(dimension_semantics=("arbitrary",)),
    )(x, norm_g, w_in_t)


def _ret_chunk(pq, pk, v_heads, s_heads, cos_t, sin_t, dec, hm, qd, kd, chunk_dec):
    q = pq * cos_t + _rot(pq) * sin_t
    k = (pk * cos_t + _rot(pk) * sin_t) * (RET_DK ** -0.5)
    outs, s_out = [], []
    for h in range(RET_HEADS):
        sc = _mm_nt(q * hm[h], k * hm[h]) * dec[h]
        intra = _mm(sc, v_heads[h])
        kv = _mm_tn(k * kd[h], v_heads[h])
        inter = _mm(q * qd[h], s_heads[h])
        outs.append(intra + inter)
        s_out.append(s_heads[h] * chunk_dec[h] + kv)
    return tuple(outs), tuple(s_out)


def _ret_specs():
    const = [_full((RET_HEADS, CHUNK, CHUNK)), _full((RET_HEADS, CHUNK, RET_QK)),
             _full((RET_HEADS, CHUNK, RET_QK)), _full((RET_HEADS, CHUNK, RET_QK))]
    return const


RET_GROUP = 4


def _ret_fwd(p_ret, tabs):
    T = p_ret.shape[0]
    G = RET_GROUP
    ng = T // (CHUNK * G)
    cos_t, sin_t, dec, hm, qd, kd, chunk_dec = tabs

    def body(p_ref, cos_ref, sin_ref, dec_ref, hm_ref, qd_ref, kd_ref, out_ref, sin_save_ref, s_scr):
        @pl.when(pl.program_id(0) == 0)
        def _():
            s_scr[...] = jnp.zeros_like(s_scr)

        consts = (dec_ref[...], hm_ref[...], qd_ref[...], kd_ref[...])
        s_heads = tuple(s_scr[h] for h in range(RET_HEADS))
        for c in range(G):
            rows = pl.ds(c * CHUNK, CHUNK)
            for h in range(RET_HEADS):
                sin_save_ref[c, h] = s_heads[h]
            v_heads = tuple(p_ref[rows, 2 * RET_QK + RET_DV * h:2 * RET_QK + RET_DV * (h + 1)] for h in range(RET_HEADS))
            outs, s_heads = _ret_chunk(p_ref[rows, 0:RET_QK], p_ref[rows, RET_QK:2 * RET_QK], v_heads, s_heads,
                                       cos_ref[rows, :], sin_ref[rows, :], *consts, chunk_dec)
            for h in range(RET_HEADS):
                out_ref[rows, RET_DV * h:RET_DV * (h + 1)] = outs[h]
        for h in range(RET_HEADS):
            s_scr[h] = s_heads[h]

    tok = CHUNK * G
    return pl.pallas_call(
        body, name="ret_fwd", grid=(ng,),
        in_specs=[pl.BlockSpec((tok, RET_COLS), lambda i: (i, 0)), _rows(tok, RET_QK), _rows(tok, RET_QK)] + _ret_specs(),
        out_specs=[_rows(tok, RET_WIDTH), pl.BlockSpec((G, RET_HEADS, RET_QK, RET_DV), lambda i: (i, 0, 0, 0))],
        out_shape=[jax.ShapeDtypeStruct((T, RET_WIDTH), F32),
                   jax.ShapeDtypeStruct((T // CHUNK, RET_HEADS, RET_QK, RET_DV), F32)],
        scratch_shapes=[pltpu.VMEM((RET_HEADS, RET_QK, RET_DV), F32)],
        compiler_params=_params(dimension_semantics=("arbitrary",)),
    )(p_ret, cos_t, sin_t, dec, hm, qd, kd)


def _ret_bwd(p_ret, s_saved, d_ret, tabs):
    T = p_ret.shape[0]
    G = RET_GROUP
    ng = T // (CHUNK * G)
    cos_t, sin_t, dec, hm, qd, kd, chunk_dec = tabs

    def body(p_ref, s_ref, dret_ref, cos_ref, sin_ref, dec_ref, hm_ref, qd_ref, kd_ref, dp_ref, ds_scr):
        @pl.when(pl.program_id(0) == 0)
        def _():
            ds_scr[...] = jnp.zeros_like(ds_scr)

        consts = (dec_ref[...], hm_ref[...], qd_ref[...], kd_ref[...])
        d_s = tuple(ds_scr[h] for h in range(RET_HEADS))
        for c in reversed(range(G)):
            rows = pl.ds(c * CHUNK, CHUNK)
            v_heads = tuple(p_ref[rows, 2 * RET_QK + RET_DV * h:2 * RET_QK + RET_DV * (h + 1)] for h in range(RET_HEADS))
            s_heads = tuple(s_ref[c, h] for h in range(RET_HEADS))
            tables = (cos_ref[rows, :], sin_ref[rows, :]) + consts
            _, vjp = jax.vjp(lambda a, b, c_, d: _ret_chunk(a, b, c_, d, *tables, chunk_dec),
                             p_ref[rows, 0:RET_QK], p_ref[rows, RET_QK:2 * RET_QK], v_heads, s_heads)
            d_out = tuple(dret_ref[rows, RET_DV * h:RET_DV * (h + 1)] for h in range(RET_HEADS))
            dq, dk, dv, d_s = vjp((d_out, d_s))
            dp_ref[rows, 0:RET_QK] = dq
            dp_ref[rows, RET_QK:2 * RET_QK] = dk
            for h in range(RET_HEADS):
                dp_ref[rows, 2 * RET_QK + RET_DV * h:2 * RET_QK + RET_DV * (h + 1)] = dv[h]
        for h in range(RET_HEADS):
            ds_scr[h] = d_s[h]

    tok = CHUNK * G
    rev = lambda i: (ng - 1 - i, 0)
    return pl.pallas_call(
        body, name="ret_bwd", grid=(ng,),
        in_specs=[pl.BlockSpec((tok, RET_COLS), rev),
                  pl.BlockSpec((G, RET_HEADS, RET_QK, RET_DV), lambda i: (ng - 1 - i, 0, 0, 0)),
                  pl.BlockSpec((tok, RET_WIDTH), rev), pl.BlockSpec((tok, RET_QK), rev), pl.BlockSpec((tok, RET_QK), rev)]
        + _ret_specs(),
        out_specs=pl.BlockSpec((tok, 2 * RET_QK + RET_WIDTH), rev),
        out_shape=jax.ShapeDtypeStruct((T, 2 * RET_QK + RET_WIDTH), F32),
        scratch_shapes=[pltpu.VMEM((RET_HEADS, RET_QK, RET_DV), F32)],
        compiler_params=_params(dimension_semantics=("arbitrary",)),
    )(p_ret, s_saved, d_ret, cos_t, sin_t, dec, hm, qd, kd)


def _wkv_consts():
    lane = lax.broadcasted_iota(jnp.int32, (RWKV_HEAD, RWKV_WIDTH), 1)
    sub = lax.broadcasted_iota(jnp.int32, (RWKV_HEAD, RWKV_WIDTH), 0)
    diag = ((lane & (RWKV_HEAD - 1)) == sub).astype(F32)
    r = lax.broadcasted_iota(jnp.int32, (3 * 128, 128), 0)
    c = lax.broadcasted_iota(jnp.int32, (3 * 128, 128), 1)
    ones = (((r & 127) >> 6) == (c >> 6)).astype(BF16)
    return diag, ones


def _stack(x):
    return jnp.concatenate([x[:, 128 * p:128 * (p + 1)] for p in range(4)], axis=0)


def _unstack(y):
    n = y.shape[0] // 4
    return jnp.concatenate([y[n * p:n * (p + 1)] for p in range(4)], axis=1)


def _split(x, n):
    pieces = []
    for _ in range(n):
        p = x.astype(BF16)
        pieces.append(p)
        x = x - p.astype(F32)
    return pieces


def _lane_sum(x, ones):
    return _unstack(jnp.dot(_two_piece(_stack(x)), ones[:256].astype(F32), preferred_element_type=F32))


def _lane_sum_bf(x, ones):
    return _unstack(jnp.dot(_stack(x).astype(BF16), ones[:128], preferred_element_type=F32))


def _colsum(x):
    return jnp.sum(x, axis=0, keepdims=True)


def _expand_cols(xt, t):
    lane = lax.broadcasted_iota(jnp.int32, (RWKV_HEAD, 128), 1)
    tiles = []
    for p in range(4):
        lo = jnp.broadcast_to(xt[128 * p:128 * p + RWKV_HEAD, t:t + 1], (RWKV_HEAD, 128))
        hi = jnp.broadcast_to(xt[128 * p + RWKV_HEAD:128 * (p + 1), t:t + 1], (RWKV_HEAD, 128))
        tiles.append(jnp.where(lane < RWKV_HEAD, lo, hi))
    return jnp.concatenate(tiles, axis=1)


def _head_sums(x, ones):
    return _unstack(jnp.dot(jnp.concatenate(_split(_stack(x), 3), axis=1), ones, preferred_element_type=F32))


def _wkv_fwd(r, w, k, v, kap, a):
    T = r.shape[0]
    C = WKV_CHUNK
    nc = T // C

    def body(r_ref, w_ref, k_ref, v_ref, kap_ref, a_ref, o_ref, s_all_ref, sa_all_ref, s_scr):
        @pl.when(pl.program_id(0) == 0)
        def _():
            s_scr[...] = jnp.zeros_like(s_scr)

        diag, ones = _wkv_consts()
        rr, ww, kk, vv, kap_, aa = (ref[...] for ref in (r_ref, w_ref, k_ref, v_ref, kap_ref, a_ref))
        bb = kap_ * aa
        c1 = _head_sums(pltpu.roll(bb, 1, axis=0) * kap_, ones)
        row = lambda x, t: x[t:t + 1]

        v_cols = vv.T

        s_prev = s_scr[...]
        sa = _lane_sum(s_prev * (-row(kap_, 0)), ones)
        ls, rows = None, []

        def emit_o(t, s_t):
            rows.append(_colsum(_lane_sum_bf(s_t * row(rr, t), ones) * diag))
            if t % 8 == 7:
                o_ref[t - 7:t + 1, :] = jnp.concatenate(rows, axis=0)
                rows.clear()

        for t in range(C):
            u = s_prev * row(ww, t) + _expand_cols(v_cols, t) * row(kk, t)
            if t > 0:
                sa = ls - sa * row(c1, t)
            if t + 1 < C:
                ls = _lane_sum(u * (-row(kap_, t + 1)), ones)
            if t > 0:
                emit_o(t - 1, s_prev)
            s_prev = u + sa * row(bb, t)
            s_all_ref[t] = s_prev
            sa_all_ref[t] = sa.astype(BF16)
        emit_o(C - 1, s_prev)
        s_scr[...] = s_prev

    spec = _rows(C, RWKV_WIDTH)
    return pl.pallas_call(
        body, name="wkv_fwd", grid=(nc,),
        in_specs=[spec] * 6,
        out_specs=[spec, pl.BlockSpec((C, RWKV_HEAD, RWKV_WIDTH), lambda i: (i, 0, 0)),
                   pl.BlockSpec((C, RWKV_HEAD, RWKV_WIDTH), lambda i: (i, 0, 0))],
        out_shape=[jax.ShapeDtypeStruct((T, RWKV_WIDTH), F32), jax.ShapeDtypeStruct((T, RWKV_HEAD, RWKV_WIDTH), F32),
                   jax.ShapeDtypeStruct((T, RWKV_HEAD, RWKV_WIDTH), BF16)],
        scratch_shapes=[pltpu.VMEM((RWKV_HEAD, RWKV_WIDTH), F32)],
        compiler_params=_params(dimension_semantics=("arbitrary",)),
    )(r, w, k, v, kap, a)


def _wkv_bwd(r, w, k, v, kap, a, s_all, sa_all, d_o):
    T = r.shape[0]
    C = WKV_CHUNK
    nc = T // C

    def body(r_ref, w_ref, k_ref, v_ref, kap_ref, a_ref, s_ref, s_before_ref, sa_ref, do_ref,
             dr_ref, dw_ref, dk_ref, dv_ref, dkap_ref, da_ref, ds_scr):
        first_chunk = pl.program_id(0) == nc - 1

        @pl.when(pl.program_id(0) == 0)
        def _():
            ds_scr[...] = jnp.zeros_like(ds_scr)

        diag, ones = _wkv_consts()
        rr, ww, kk, vv, kap_, aa, dd = (ref[...] for ref in (r_ref, w_ref, k_ref, v_ref, kap_ref, a_ref, do_ref))
        bb = kap_ * aa
        e1 = _head_sums(pltpu.roll(kap_, C - 1, axis=0) * bb, ones)
        row = lambda x, t: x[t:t + 1]

        def state_before(t):
            return s_ref[t - 1] if t > 0 else jnp.where(first_chunk, 0.0, s_before_ref[0])

        v_cols, do_cols = vv.T, dd.T

        d_sn, dsa, rows = None, None, [None] * C

        def emit_rows(t, d_sn_t, dsa_t):
            s_prev, dof = state_before(t), _expand_cols(do_cols, t)
            dv = _colsum(_lane_sum_bf(d_sn_t * row(kk, t), ones) * diag)
            db = _colsum(d_sn_t * sa_ref[t].astype(F32))
            rows[t] = (_colsum(s_ref[t] * dof), _colsum(d_sn_t * s_prev), _colsum(d_sn_t * _expand_cols(v_cols, t)), dv,
                       db * row(aa, t) - _colsum(dsa_t * s_prev), db * row(kap_, t))
            if t % 8 == 0:
                for j, ref in enumerate((dr_ref, dw_ref, dk_ref, dv_ref, dkap_ref, da_ref)):
                    ref[t:t + 8, :] = jnp.concatenate([rows[u][j] for u in range(t, t + 8)], axis=0)

        for t in reversed(range(C)):
            dof = _expand_cols(do_cols, t)
            if t == C - 1:
                d_sn = ds_scr[...] + dof * row(rr, t)
                dsa = _lane_sum(d_sn * row(bb, t), ones)
            else:
                v_t = d_sn * row(ww, t + 1) + dof * row(rr, t)
                ls = _lane_sum(v_t * row(bb, t), ones)
                emit_rows(t + 1, d_sn, dsa)
                d_sn = v_t - dsa * row(kap_, t + 1)
                dsa = ls - dsa * row(e1, t)
        emit_rows(0, d_sn, dsa)
        d_s = d_sn * row(ww, 0) - dsa * row(kap_, 0)
        ds_scr[...] = d_s

    spec = pl.BlockSpec((C, RWKV_WIDTH), lambda i: (nc - 1 - i, 0))
    states = pl.BlockSpec((C, RWKV_HEAD, RWKV_WIDTH), lambda i: (nc - 1 - i, 0, 0))
    before = pl.BlockSpec((1, RWKV_HEAD, RWKV_WIDTH), lambda i: (jnp.maximum((nc - 1 - i) * C - 1, 0), 0, 0))
    return pl.pallas_call(
        body, name="wkv_bwd", grid=(nc,),
        in_specs=[spec] * 6 + [states, before, states, spec],
        out_specs=[spec] * 6,
        out_shape=[jax.ShapeDtypeStruct((T, RWKV_WIDTH), F32)] * 6,
        scratch_shapes=[pltpu.VMEM((RWKV_HEAD, RWKV_WIDTH), F32)],
        compiler_params=_params(dimension_semantics=("arbitrary",)),
    )(r, w, k, v, kap, a, s_all, s_all, sa_all, d_o)


W = RWKV_WIDTH


def _softplus(y):
    return jnp.maximum(y, 0.0) + jnp.log(1.0 + jnp.exp(-jnp.abs(y)))


def _prep_fn(kr, xwa, w0, a0, k_k, k_a, wup_pad, aup_pad, ones64):
    w_log = -_softplus(-(w0 + _mm_nt(jnp.tanh(xwa), wup_pad))) - 0.5
    decay = jnp.exp(-jnp.exp(w_log))
    a = jax.nn.sigmoid(a0 + _mm_nt(xwa, aup_pad))
    kk = kr * k_k
    kap = kk / jnp.maximum(jnp.sqrt(_head_mix(kk * kk, ones64)), 1e-12)
    k = kr * (1.0 + (a - 1.0) * k_a)
    return decay, k, kap, a


def _shift_down(p, first_row):
    rows = lax.broadcasted_iota(jnp.int32, p.shape, 0)
    return jnp.where(rows == 0, first_row, pltpu.roll(p, 1, axis=0))


def _shift_up(z, last_row):
    n = z.shape[0]
    rows = lax.broadcasted_iota(jnp.int32, z.shape, 0)
    return jnp.where(rows == n - 1, last_row, pltpu.roll(z, n - 1, axis=0))


def _prev_block_spec():
    return pl.BlockSpec((8, RWKV_COLS), lambda i: (jnp.maximum(i * (TOK_TILE // 8) - 1, 0), 0))


def _mixed(p_ref, prev8_ref, mu_ref, first_tile):
    p = p_ref[...]
    first_row = jnp.where(first_tile, 0.0, prev8_ref[7:8, :])
    prev = _shift_down(p, first_row)
    return p, prev, p + mu_ref[...] * (prev - p)


def _prep_fwd(p_rwkv, mu, w0, a0, k_k, k_a, wup_pad, aup_pad, ones64):
    T = p_rwkv.shape[0]

    def body(p_ref, prev8_ref, mu_ref, w0_ref, a0_ref, kk_ref, ka_ref, wup_ref, aup_ref, ones_ref,
             r_ref, w_ref, k_ref, v_ref, kap_ref, a_ref, g_ref):
        _, _, ps = _mixed(p_ref, prev8_ref, mu_ref, pl.program_id(0) == 0)
        decay, k, kap, a = _prep_fn(ps[:, W:2 * W], ps[:, 4 * W:], w0_ref[...], a0_ref[...], kk_ref[...], ka_ref[...],
                                    wup_ref[...], aup_ref[...], ones_ref[...])
        r_ref[...] = ps[:, 0:W]
        w_ref[...] = decay
        k_ref[...] = k
        v_ref[...] = ps[:, 2 * W:3 * W]
        kap_ref[...] = kap
        a_ref[...] = a
        g_ref[...] = ps[:, 3 * W:4 * W]

    vec = _full((1, W))
    return pl.pallas_call(
        body, name="prep_fwd", grid=(T // TOK_TILE,),
        in_specs=[_rows(TOK_TILE, RWKV_COLS), _prev_block_spec(), _full((1, RWKV_COLS)), vec, vec, vec, vec,
                  _full((W, 2 * LORA)), _full((W, 2 * LORA)), _full((256, 128))],
        out_specs=[_rows(TOK_TILE, W)] * 7,
        out_shape=[jax.ShapeDtypeStruct((T, W), F32)] * 7,
        compiler_params=_params(dimension_semantics=("arbitrary",)),
    )(p_rwkv, p_rwkv, mu, w0, a0, k_k, k_a, wup_pad, aup_pad, ones64)


def _prep_bwd(p_rwkv, mu, w0, a0, k_k, k_a, wup_pad, aup_pad, ones64, dr, dw, dk, dv, dkap, da, dg, dr2, dk2, dv2):
    T = p_rwkv.shape[0]
    nt = T // TOK_TILE

    def body(p_ref, prev8_ref, mu_ref, w0_ref, a0_ref, kk_ref, ka_ref, wup_ref, aup_ref, ones_ref,
             dr_ref, dw_ref, dk_ref, dv_ref, dkap_ref, da_ref, dg_ref, dr2_ref, dk2_ref, dv2_ref,
             dp_ref, dmu_ref, dw0_ref, da0_ref, dkk_ref, dka_ref, dwup_ref, daup_ref, zrow_scr):
        i = pl.program_id(0)
        accs = (dmu_ref, dw0_ref, da0_ref, dkk_ref, dka_ref, dwup_ref, daup_ref)

        @pl.when(i == 0)
        def _():
            zrow_scr[...] = jnp.zeros_like(zrow_scr)
            for ref in accs:
                ref[...] = jnp.zeros_like(ref)

        p, prev, ps = _mixed(p_ref, prev8_ref, mu_ref, i == nt - 1)
        ones = ones_ref[...]
        _, vjp = jax.vjp(lambda *args: _prep_fn(*args, ones), ps[:, W:2 * W], ps[:, 4 * W:], w0_ref[...], a0_ref[...],
                         kk_ref[...], ka_ref[...], wup_ref[...], aup_ref[...])
        dkr, dxwa, dw0, da0, dkk, dka, dwup, daup = vjp(
            (dw_ref[...], dk_ref[...] + dk2_ref[...], dkap_ref[...], da_ref[...]))
        dps = jnp.concatenate([dr_ref[...] + dr2_ref[...], dkr, dv_ref[...] + dv2_ref[...], dg_ref[...], dxwa], axis=1)
        z = dps * mu_ref[...]
        dp_ref[...] = dps - z + _shift_up(z, zrow_scr[0:1, :])
        zrow_scr[0:1, :] = z[0:1, :]
        for ref, val in zip(accs, (_colsum(dps * (prev - p)), dw0, da0, dkk, dka, dwup, daup)):
            ref[...] += val

    rev = lambda i: (nt - 1 - i, 0)
    vec = _full((1, W))
    lora = _full((W, 2 * LORA))
    tile = pl.BlockSpec((TOK_TILE, W), rev)
    prev8 = pl.BlockSpec((8, RWKV_COLS), lambda i: (jnp.maximum((nt - 1 - i) * (TOK_TILE // 8) - 1, 0), 0))
    return pl.pallas_call(
        body, name="prep_bwd", grid=(nt,),
        in_specs=[pl.BlockSpec((TOK_TILE, RWKV_COLS), rev), prev8, _full((1, RWKV_COLS)), vec, vec, vec, vec, lora, lora,
                  _full((256, 128))] + [tile] * 10,
        out_specs=[pl.BlockSpec((TOK_TILE, RWKV_COLS), rev), _full((1, RWKV_COLS)), vec, vec, vec, vec, lora, lora],
        out_shape=[jax.ShapeDtypeStruct((T, RWKV_COLS), F32), jax.ShapeDtypeStruct((1, RWKV_COLS), F32)]
        + [jax.ShapeDtypeStruct((1, W), F32)] * 4 + [jax.ShapeDtypeStruct((W, 2 * LORA), F32)] * 2,
        scratch_shapes=[pltpu.VMEM((8, RWKV_COLS), F32)],
        compiler_params=_params(dimension_semantics=("arbitrary",)),
    )(p_rwkv, p_rwkv, mu, w0, a0, k_k, k_a, wup_pad, aup_pad, ones64, dr, dw, dk, dv, dkap, da, dg, dr2, dk2, dv2)


def _silu(x):
    return x * jax.nn.sigmoid(x)


def _post_y(o, r, k, v, g_rw, ret_raw, g_ret, ret_gn_g, gn_g, gn_b, r_k, avg128, avg64, ones64):
    xc = ret_raw - _head_mix(ret_raw, avg128)
    ret = xc * lax.rsqrt(_head_mix(xc * xc, avg128) + RET_GN_EPS)
    y_ret = _silu(g_ret) * (ret * ret_gn_g)
    oc = o - _head_mix(o, avg64)
    on = oc * lax.rsqrt(_head_mix(oc * oc, avg64) + RWKV_GN_EPS) * gn_g + gn_b
    bonus = _head_mix(r * k * r_k, ones64) * v
    y_rwkv = _silu(g_rw) * (on + bonus)
    return y_ret, y_rwkv


def _post_loss(h, final_g, target):
    err = _rmsnorm(h, final_g) - target
    return 0.5 * jnp.sum(jnp.mean(err * err, axis=-1))


def _post(o, r, k, v, g_rw, ret_raw, p_ret, x, target, ret_gn_g, gn_g, gn_b, r_k, final_g, w_out, avg128, avg64, ones64):
    T = x.shape[0]
    n_tok_out = 8

    def body(o_ref, r_ref, k_ref, v_ref, grw_ref, ret_ref, gret_ref, x_ref, tgt_ref, rg_ref, gg_ref, gb_ref, rk_ref, fg_ref,
             wo_ref, a128_ref, a64_ref, ones_ref, *outs):
        tok_outs, (dwo_ref, drg_ref, dgg_ref, dgb_ref, drk_ref, dfg_ref, loss_ref) = outs[:n_tok_out], outs[n_tok_out:]
        accs = (dwo_ref, drg_ref, dgg_ref, dgb_ref, drk_ref, dfg_ref, loss_ref)

        @pl.when(pl.program_id(0) == 0)
        def _():
            for ref in accs:
                ref[...] = jnp.zeros_like(ref)

        consts = (a128_ref[...], a64_ref[...], ones_ref[...])
        (y_ret, y_rwkv), vjp = jax.vjp(
            lambda *args: _post_y(*args, *consts), o_ref[...], r_ref[...], k_ref[...], v_ref[...], grw_ref[...], ret_ref[...],
            gret_ref[...], rg_ref[...], gg_ref[...], gb_ref[...], rk_ref[...])
        h = x_ref[...] + _dot_bf(y_ret, wo_ref[0:RET_WIDTH, :]) + _dot_bf(y_rwkv, wo_ref[RET_WIDTH:, :])
        loss, (dh, dfg) = jax.value_and_grad(_post_loss, argnums=(0, 1))(h, fg_ref[...], tgt_ref[...])
        dy_ret = _dot_nt_bf(dh, wo_ref[0:RET_WIDTH, :])
        dy_rwkv = _dot_nt_bf(dh, wo_ref[RET_WIDTH:, :])
        do, dr, dk, dv, dgrw, dret, dgret, drg, dgg, dgb, drk = vjp((dy_ret, dy_rwkv))
        for ref, val in zip(tok_outs, (dh, do, dr, dk, dv, dgrw, dret, dgret)):
            ref[...] = val
        dwo_ref[0:RET_WIDTH, :] += _dot_tn_bf(y_ret, dh)
        dwo_ref[RET_WIDTH:, :] += _dot_tn_bf(y_rwkv, dh)
        for ref, val in zip(accs[1:], (drg, dgg, dgb, drk, dfg, jnp.full((1, 128), loss, F32))):
            ref[...] += val

    tile = _rows(TOK_TILE, W)
    wide = _rows(TOK_TILE, D_MODEL)
    vec = _full((1, W))
    sq = _full((256, 128))
    return pl.pallas_call(
        body, name="post", grid=(T // TOK_TILE,),
        in_specs=[tile] * 6 + [pl.BlockSpec((TOK_TILE, W), lambda i: (i, 2)), wide, wide, vec, vec, vec, vec,
                               _full((1, D_MODEL)), _full((D_MODEL, D_MODEL)), sq, sq, sq],
        out_specs=[wide] + [tile] * 7 + [_full((D_MODEL, D_MODEL)), vec, vec, vec, vec, _full((1, D_MODEL)), _full((1, 128))],
        out_shape=[jax.ShapeDtypeStruct((T, D_MODEL), F32)] + [jax.ShapeDtypeStruct((T, W), F32)] * 7
        + [jax.ShapeDtypeStruct((D_MODEL, D_MODEL), F32)] + [jax.ShapeDtypeStruct((1, W), F32)] * 4
        + [jax.ShapeDtypeStruct((1, D_MODEL), F32), jax.ShapeDtypeStruct((1, 128), F32)],
        compiler_params=_params(dimension_semantics=("arbitrary",)),
    )(o, r, k, v, g_rw, ret_raw, p_ret, x, target, ret_gn_g, gn_g, gn_b, r_k, final_g, w_out, avg128, avg64, ones64)


def _inproj_bwd_x(x, norm_g, dp_qkv, dg_ret, dp_rwkv, dh, w_in_t):
    T = x.shape[0]
    n_qkv = 2 * RET_QK + RET_WIDTH

    def body(x_ref, g_ref, dqkv_ref, dgret_ref, drwkv_ref, dh_ref, w_ref, dx_ref, dg_ref):
        @pl.when(pl.program_id(0) == 0)
        def _():
            dg_ref[...] = jnp.zeros_like(dg_ref)

        _, vjp = jax.vjp(_rmsnorm, x_ref[...], g_ref[...])
        du = (_dot_bf(dqkv_ref[...], w_ref[0:n_qkv, :]) + _dot_bf(dgret_ref[...], w_ref[n_qkv:RET_COLS, :])
              + _dot_bf(drwkv_ref[...], w_ref[RET_COLS:, :]))
        dx, dg = vjp(du)
        dx_ref[...] = dx + dh_ref[...]
        dg_ref[...] += dg

    return pl.pallas_call(
        body, name="inproj_bwd_x", grid=(T // TOK_TILE,),
        in_specs=[_rows(TOK_TILE, D_MODEL), _full((1, D_MODEL)), _rows(TOK_TILE, n_qkv), _rows(TOK_TILE, RET_WIDTH),
                  _rows(TOK_TILE, RWKV_COLS), _rows(TOK_TILE, D_MODEL), _full((IN_COLS, D_MODEL))],
        out_specs=[_rows(TOK_TILE, D_MODEL), _full((1, D_MODEL))],
        out_shape=[jax.ShapeDtypeStruct((T, D_MODEL), F32), jax.ShapeDtypeStruct((1, D_MODEL), F32)],
        compiler_params=_params(dimension_semantics=("arbitrary",)),
    )(x, norm_g, dp_qkv, dg_ret, dp_rwkv, dh, w_in_t)


def _grad_w(name, u, dp):
    T, n = dp.shape
    tile = 2 * TOK_TILE
    steps = T // tile

    def body(u_ref, dp_ref, out_ref, acc_ref):
        @pl.when(pl.program_id(0) == 0)
        def _():
            acc_ref[...] = jnp.zeros_like(acc_ref)

        acc_ref[...] += _dot_tn_bf(dp_ref[...], u_ref[...])

        @pl.when(pl.program_id(0) == steps - 1)
        def _():
            out_ref[...] = acc_ref[...].astype(BF16)

    return pl.pallas_call(
        body, name=name, grid=(steps,),
        in_specs=[_rows(tile, D_MODEL), _rows(tile, n)],
        out_specs=_full((n, D_MODEL)),
        out_shape=jax.ShapeDtypeStruct((n, D_MODEL), BF16),
        scratch_shapes=[pltpu.VMEM((n, D_MODEL), F32)],
        compiler_params=_params(dimension_semantics=("arbitrary",)),
    )(u, dp)


def _pad_lora(w_up_t, first):
    z = jnp.zeros_like(w_up_t)
    return jnp.concatenate([w_up_t, z] if first else [z, w_up_t], axis=1)


def _local_grads(x, target, norm_g, w_in_t, ret_gn_g, mu, w_lora_up_t, w0, a_lora_up_t, a0, k_k, k_a, r_k, gn_g, gn_b,
                 w_out_bf, final_g):
    T = x.shape[0]
    tabs = _rope_tables(T) + _ret_tables()
    ones64 = _block_mix(128, RWKV_HEAD)
    avg64 = _block_mix(128, RWKV_HEAD, 1.0 / RWKV_HEAD)
    avg128 = _block_mix(128, RET_DV, 1.0 / RET_DV)
    wup_pad, aup_pad = _pad_lora(w_lora_up_t, True), _pad_lora(a_lora_up_t, False)

    p_ret, p_rwkv, u = _inproj(x, norm_g, w_in_t)
    ret_raw, s_saved = _ret_fwd(p_ret, tabs)
    r, w, k, v, kap, a, g_rw = _prep_fwd(p_rwkv, mu, w0, a0, k_k, k_a, wup_pad, aup_pad, ones64)
    o, s_all, sa_all = _wkv_fwd(r, w, k, v, kap, a)
    (dh, do, dr2, dk2, dv2, dgrw, dret, dgret, d_w_out, d_ret_gn_g, d_gn_g, d_gn_b, d_r_k, d_final_g, loss) = _post(
        o, r, k, v, g_rw, ret_raw, p_ret, x, target, ret_gn_g, gn_g, gn_b, r_k, final_g, w_out_bf, avg128, avg64, ones64)
    dr, dw, dk, dv, dkap, da = _wkv_bwd(r, w, k, v, kap, a, s_all, sa_all, do)
    dp_rwkv, d_mu, d_w0, d_a0, d_k_k, d_k_a, d_wup, d_aup = _prep_bwd(
        p_rwkv, mu, w0, a0, k_k, k_a, wup_pad, aup_pad, ones64, dr, dw, dk, dv, dkap, da, dgrw, dr2, dk2, dv2)
    dp_qkv = _ret_bwd(p_ret, s_saved, dret, tabs)
    dx, d_norm_g = _inproj_bwd_x(x, norm_g, dp_qkv, dgret, dp_rwkv, dh, w_in_t)
    d_w_in = jnp.concatenate([_grad_w("grad_w_qkv", u, dp_qkv), _grad_w("grad_w_gret", u, dgret),
                              _grad_w("grad_w_rwkv", u, dp_rwkv)], axis=0)
    grads = dict(norm_g=d_norm_g, w_in=d_w_in, ret_gn_g=d_ret_gn_g, rwkv_mu=d_mu, w_lora_up=d_wup[:, :LORA], w0=d_w0,
                 a_lora_up=d_aup[:, LORA:], a0=d_a0, k_k=d_k_k, k_a=d_k_a, r_k=d_r_k, rwkv_gn_g=d_gn_g, rwkv_gn_b=d_gn_b,
                 w_out=d_w_out, final_norm_g=d_final_g)
    return loss, dx, grads


def _mesh_pos():
    return lax.axis_index("x"), lax.axis_index("y"), lax.axis_index("c")


def _all_gather(shards):
    n = len(shards)

    def body(*refs):
        x_refs, out_refs = refs[:n], refs[n:2 * n]
        send_sems, recv_sems, local_sems = refs[2 * n:]
        x, y, c = _mesh_pos()
        me, sibling = (x, y, c), (x, y, 1 - c)
        chips = [(1 - x, y), (x, 1 - y), (1 - x, 1 - y)]

        def rows(a, pos):
            m = x_refs[a].shape[0]
            return out_refs[a].at[pl.ds((4 * pos[0] + 2 * pos[1] + pos[2]) * m, m), :]

        def copy(a, k, block, to, src=None):
            return pltpu.make_async_remote_copy(
                src_ref=rows(a, block) if src is None else src, dst_ref=rows(a, block),
                send_sem=send_sems.at[a, k], recv_sem=recv_sems.at[a, k], device_id=to, device_id_type=MESH)

        mine = [pltpu.make_async_copy(x_refs[a], rows(a, me), local_sems.at[a]) for a in range(n)]
        for cp in mine:
            cp.start()
        first = []
        for a in range(n):
            first.append(copy(a, 0, me, sibling, src=x_refs[a]))
            first += [copy(a, 1 + j, me, (*chip, c), src=x_refs[a]) for j, chip in enumerate(chips)]
        for cp in first:
            cp.start()
        passed = []
        for j, chip in enumerate(chips):
            for a in range(n):
                copy(a, 1 + j, (*chip, c), me).wait_recv()
                passed.append(copy(a, 4 + j, (*chip, c), sibling))
                passed[-1].start()
        for a in range(n):
            copy(a, 0, sibling, me).wait_recv()
            for j, chip in enumerate(chips):
                copy(a, 4 + j, (*chip, 1 - c), me).wait_recv()
        for cp in first + passed:
            cp.wait_send()
        for cp in mine:
            cp.wait()

    vmem = pl.BlockSpec(memory_space=pltpu.VMEM)
    return pl.pallas_call(
        body, name="gather_weights",
        out_shape=[jax.ShapeDtypeStruct((N_DEV * s.shape[0], s.shape[1]), s.dtype) for s in shards],
        in_specs=[vmem] * n, out_specs=[vmem] * n,
        scratch_shapes=[pltpu.SemaphoreType.DMA((n, 7)), pltpu.SemaphoreType.DMA((n, 7)), pltpu.SemaphoreType.DMA((n,))],
        compiler_params=_params(),
    )(*shards)


N_CHIP = 4


def _exchange_pairs(big, small):
    nb, ns = len(big), len(small)

    def body(*refs):
        big_in, small_in = refs[:nb], refs[nb:nb + ns]
        theirs, small_out = refs[nb + ns:2 * nb + ns], refs[2 * nb + ns:2 * nb + 2 * ns]
        pair_send, pair_recv, send_sems, recv_sems, local_sems = refs[2 * nb + 2 * ns:]
        x, y, c = _mesh_pos()
        me = 4 * x + 2 * y + c
        local = [pltpu.make_async_copy(small_in[a].at[me], small_out[a].at[me], local_sems.at[a]) for a in range(ns)]
        for cp in local:
            cp.start()
        copies = [pltpu.make_async_remote_copy(
            src_ref=big_in[a], dst_ref=theirs[a], send_sem=pair_send.at[a], recv_sem=pair_recv.at[a],
            device_id=(x, y, 1 - c), device_id_type=MESH) for a in range(nb)]
        for k in range(1, N_DEV):
            peer = (x ^ (k >> 2), y ^ ((k >> 1) & 1), c ^ (k & 1))
            peer_idx = 4 * peer[0] + 2 * peer[1] + peer[2]
            copies += [pltpu.make_async_remote_copy(
                src_ref=small_in[a].at[peer_idx], dst_ref=small_out[a].at[me], send_sem=send_sems.at[a, k - 1],
                recv_sem=recv_sems.at[a, k - 1], device_id=peer, device_id_type=MESH) for a in range(ns)]
        for cp in copies:
            cp.start()
        for cp in copies:
            cp.wait()
        for cp in local:
            cp.wait()

    hbm = pl.BlockSpec(memory_space=pl.ANY)
    out_shape = [jax.ShapeDtypeStruct(p.shape, p.dtype) for p in big + small]
    dma = pltpu.SemaphoreType.DMA
    res = pl.pallas_call(
        body, name="exchange_pairs", out_shape=out_shape,
        in_specs=[hbm] * (nb + ns), out_specs=[hbm] * len(out_shape),
        scratch_shapes=[dma((nb,)), dma((nb,)), dma((ns, 7)), dma((ns, 7)), dma((ns,))],
        compiler_params=_params(),
    )(*big, *small)
    return res[:nb], res[nb:]


def _pair_sum(name, mine, theirs, row_tile):
    _, rows, cols = mine.shape

    def body(a_ref, b_ref, o_ref):
        o_ref[...] = (a_ref[...].astype(F32) + b_ref[...].astype(F32)).astype(o_ref.dtype)

    spec = pl.BlockSpec((N_CHIP, row_tile, cols), lambda i: (0, i, 0))
    return pl.pallas_call(
        body, name=name, grid=(rows // row_tile,), in_specs=[spec, spec], out_specs=spec,
        out_shape=jax.ShapeDtypeStruct(mine.shape, mine.dtype),
        compiler_params=_params(dimension_semantics=("arbitrary",)),
    )(mine, theirs)


def _exchange_chips(parts):
    n = len(parts)

    def body(*refs):
        in_refs, out_refs = refs[:n], refs[n:2 * n]
        send_sems, recv_sems, local_sems = refs[2 * n:]
        x, y, c = _mesh_pos()
        my_chip = 2 * x + y
        own = [pltpu.make_async_copy(in_refs[a].at[my_chip], out_refs[a].at[my_chip], local_sems.at[a]) for a in range(n)]
        for cp in own:
            cp.start()
        copies = []
        for k in range(1, N_CHIP):
            px, py = x ^ (k >> 1), y ^ (k & 1)
            copies += [pltpu.make_async_remote_copy(
                src_ref=in_refs[a].at[2 * px + py], dst_ref=out_refs[a].at[my_chip], send_sem=send_sems.at[a, k - 1],
                recv_sem=recv_sems.at[a, k - 1], device_id=(px, py, c), device_id_type=MESH) for a in range(n)]
        for cp in copies:
            cp.start()
        for cp in copies:
            cp.wait()
        for cp in own:
            cp.wait()

    hbm = pl.BlockSpec(memory_space=pl.ANY)
    dma = pltpu.SemaphoreType.DMA
    return pl.pallas_call(
        body, name="exchange_chips",
        out_shape=[jax.ShapeDtypeStruct(p.shape, p.dtype) for p in parts],
        in_specs=[hbm] * n, out_specs=[hbm] * n,
        scratch_shapes=[dma((n, N_CHIP - 1)), dma((n, N_CHIP - 1)), dma((n,))],
        compiler_params=_params(),
    )(*parts)


def _adamw(w, g, m, v):
    m = ADAM_B1 * m + (1.0 - ADAM_B1) * g
    v = ADAM_B2 * v + (1.0 - ADAM_B2) * (g * g)
    m_hat = m / (1.0 - ADAM_B1 ** ADAM_STEP)
    v_hat = v / (1.0 - ADAM_B2 ** ADAM_STEP)
    return -ADAM_LR * (m_hat / (jnp.sqrt(v_hat) + ADAM_EPS) + ADAM_WD * w), m, v


def _sum_parts(name, parts, row_tile):
    n_parts, rows, cols = parts.shape

    def body(p_ref, g_ref):
        g = p_ref[0].astype(F32)
        for s in range(1, n_parts):
            g = g + p_ref[s].astype(F32)
        g_ref[...] = g

    return pl.pallas_call(
        body, name=name, grid=(rows // row_tile,),
        in_specs=[pl.BlockSpec((n_parts, row_tile, cols), lambda i: (0, i, 0))],
        out_specs=pl.BlockSpec((row_tile, cols), lambda i: (i, 0)),
        out_shape=jax.ShapeDtypeStruct((rows, cols), F32),
        compiler_params=_params(dimension_semantics=("arbitrary",)),
    )(parts)


def _adamw_apply(name, g, w, m, v, row_tile):
    _, rows, cols = w.shape

    def body(g_ref, w_ref, m_ref, v_ref, d_ref, nm_ref, nv_ref):
        d_ref[0], nm_ref[0], nv_ref[0] = _adamw(w_ref[0], g_ref[0], m_ref[0], v_ref[0])

    tile = pl.BlockSpec((1, row_tile, cols), lambda i: (0, i, 0))
    return pl.pallas_call(
        body, name=name, grid=(rows // row_tile,), in_specs=[tile] * 4, out_specs=[tile] * 3,
        out_shape=[jax.ShapeDtypeStruct((1, rows, cols), F32)] * 3,
        compiler_params=_params(dimension_semantics=("arbitrary",)),
    )(g, w, m, v)


def _reduce_adamw_t(name, parts_t, w, m, v, sum_tile, row_tile):
    g = _sum_parts(name + "_sum", parts_t, sum_tile).T[None]
    return [g] + list(_adamw_apply(name, g, w, m, v, row_tile))


def _reduce_adamw(name, parts, w, m, v, row_tile):
    n_parts, rows, cols = parts.shape

    def body(p_ref, w_ref, m_ref, v_ref, g_ref, d_ref, nm_ref, nv_ref):
        g = p_ref[0].astype(F32)
        for s in range(1, n_parts):
            g = g + p_ref[s].astype(F32)
        g_ref[0] = g
        d_ref[0], nm_ref[0], nv_ref[0] = _adamw(w_ref[0], g, m_ref[0], v_ref[0])

    tile = pl.BlockSpec((1, row_tile, cols), lambda i: (0, i, 0))
    return pl.pallas_call(
        body, name=name, grid=(rows // row_tile,),
        in_specs=[pl.BlockSpec((n_parts, row_tile, cols), lambda i: (0, i, 0)), tile, tile, tile],
        out_specs=[tile] * 4,
        out_shape=[jax.ShapeDtypeStruct((1, rows, cols), F32)] * 4,
        compiler_params=_params(dimension_semantics=("arbitrary",)),
    )(parts, w, m, v)


_SMALL = (("norm_g", 1024), ("ret_gn_g", 512), ("rwkv_mu", 2176), ("w0", 512), ("a0", 512), ("k_k", 512), ("k_a", 512),
          ("r_k", 512), ("rwkv_gn_g", 512), ("rwkv_gn_b", 512), ("final_norm_g", 1024))
_SMALL_LANES = sum(n for _, n in _SMALL) + 128
_WEIGHTS = ("norm_g", "w_in", "ret_gn_g", "rwkv_mu", "w_lora_up", "w0", "a_lora_up", "a0", "k_k", "k_a", "r_k", "rwkv_gn_g",
            "rwkv_gn_b", "w_out", "final_norm_g")


def _adamw_vectors(parts, wts, mom, var):
    k = len(_SMALL)

    def body(p_ref, *refs):
        w_refs, m_refs, v_refs, outs = refs[:k], refs[k:2 * k], refs[2 * k:3 * k], refs[3 * k:]
        g_all = p_ref[0]
        for s in range(1, N_DEV):
            g_all = g_all + p_ref[s]
        off = 0
        for i, (name, n) in enumerate(_SMALL):
            g = g_all[:, off:off + n]
            off += n
            if name == "r_k":
                g = jnp.concatenate([g[:, RWKV_HEAD * h:RWKV_HEAD * (h + 1)] for h in range(RWKV_HEADS)], axis=0)[None]
            outs[4 * i][...] = g
            outs[4 * i + 1][...], outs[4 * i + 2][...], outs[4 * i + 3][...] = _adamw(
                w_refs[i][...], g, m_refs[i][...], v_refs[i][...])
        outs[4 * k][...] = g_all[:, off:off + 128]

    vmem = pl.BlockSpec(memory_space=pltpu.VMEM)
    shapes = [jax.ShapeDtypeStruct(wts[n].shape, F32) for n, _ in _SMALL for _ in range(4)] + [jax.ShapeDtypeStruct((1, 128), F32)]
    res = pl.pallas_call(
        body, name="adamw_vectors", out_shape=shapes,
        in_specs=[vmem] * (1 + 3 * k), out_specs=[vmem] * len(shapes), compiler_params=_params(),
    )(parts, *[wts[n] for n, _ in _SMALL], *[mom[n] for n, _ in _SMALL], *[var[n] for n, _ in _SMALL])
    return {n: res[4 * i:4 * i + 4] for i, (n, _) in enumerate(_SMALL)}, res[4 * k]


def kernel(x, norm_g, w_in, ret_gn_g, rwkv_mu, w_lora_up, w0, a_lora_up, a0, k_k, k_a, r_k, rwkv_gn_g, rwkv_gn_b, w_out, final_norm_g, loss_target, m_norm_g, m_w_in, m_ret_gn_g, m_rwkv_mu, m_w_lora_up, m_w0, m_a_lora_up, m_a0, m_k_k, m_k_a, m_r_k, m_rwkv_gn_g, m_rwkv_gn_b, m_w_out, m_final_norm_g, v_norm_g, v_w_in, v_ret_gn_g, v_rwkv_mu, v_w_lora_up, v_w0, v_a_lora_up, v_a0, v_k_k, v_k_a, v_r_k, v_rwkv_gn_g, v_rwkv_gn_b, v_w_out, v_final_norm_g):
    wts = dict(norm_g=norm_g, w_in=w_in, ret_gn_g=ret_gn_g, rwkv_mu=rwkv_mu, w_lora_up=w_lora_up, w0=w0, a_lora_up=a_lora_up,
               a0=a0, k_k=k_k, k_a=k_a, r_k=r_k, rwkv_gn_g=rwkv_gn_g, rwkv_gn_b=rwkv_gn_b, w_out=w_out,
               final_norm_g=final_norm_g)
    mom = dict(norm_g=m_norm_g, w_in=m_w_in, ret_gn_g=m_ret_gn_g, rwkv_mu=m_rwkv_mu, w_lora_up=m_w_lora_up, w0=m_w0,
               a_lora_up=m_a_lora_up, a0=m_a0, k_k=m_k_k, k_a=m_k_a, r_k=m_r_k, rwkv_gn_g=m_rwkv_gn_g,
               rwkv_gn_b=m_rwkv_gn_b, w_out=m_w_out, final_norm_g=m_final_norm_g)
    var = dict(norm_g=v_norm_g, w_in=v_w_in, ret_gn_g=v_ret_gn_g, rwkv_mu=v_rwkv_mu, w_lora_up=v_w_lora_up, w0=v_w0,
               a_lora_up=v_a_lora_up, a0=v_a0, k_k=v_k_k, k_a=v_k_a, r_k=v_r_k, rwkv_gn_g=v_rwkv_gn_g,
               rwkv_gn_b=v_rwkv_gn_b, w_out=v_w_out, final_norm_g=v_final_norm_g)
    shapes = {n: wts[n].shape for n in _WEIGHTS}

    w_in_t, w_out_bf, wup_t, aup_t = _all_gather(
        [w_in[0].T.astype(BF16), w_out[0].astype(BF16), w_lora_up[0].T, a_lora_up[0].T])

    loss, dx, g = _local_grads(
        x[0], loss_target[0], norm_g, w_in_t, ret_gn_g, rwkv_mu, wup_t, w0, aup_t, a0, k_k, k_a,
        r_k.reshape(1, W), rwkv_gn_g, rwkv_gn_b, w_out_bf, final_norm_g.reshape(1, D_MODEL))

    small = jnp.concatenate([g[n] for n, _ in _SMALL] + [loss], axis=1)
    core = lax.axis_index("c")
    by_core = lambda t: [lax.dynamic_index_in_dim(t, i, axis=1, keepdims=False) for i in (core, 1 - core)]
    in_mine, in_sib = by_core(g["w_in"].reshape(N_CHIP, 2, SHARD_IN, D_MODEL))
    out_mine, out_sib = by_core(g["w_out"].reshape(N_CHIP, 2, SHARD_OUT, D_MODEL).astype(BF16))
    (in_theirs, out_theirs), parts = _exchange_pairs(
        [in_sib, out_sib],
        [g["w_lora_up"].reshape(N_DEV, SHARD_LORA, LORA), g["a_lora_up"].reshape(N_DEV, SHARD_LORA, LORA),
         jnp.broadcast_to(small[None], (N_DEV, 1, _SMALL_LANES))])
    by_chip = _exchange_chips([_pair_sum("pair_sum_w_in", in_mine, in_theirs, SHARD_IN // 2),
                               _pair_sum("pair_sum_w_out", out_mine, out_theirs, SHARD_OUT)])
    res = {}
    res["w_in"] = _reduce_adamw_t("adamw_w_in", by_chip[0], w_in, m_w_in, v_w_in, SHARD_IN // 2, 256)
    res["w_out"] = _reduce_adamw("adamw_w_out", by_chip[1], w_out, m_w_out, v_w_out, SHARD_OUT)
    res["w_lora_up"] = _reduce_adamw_t("adamw_w_lora_up", parts[0], w_lora_up, m_w_lora_up, v_w_lora_up, LORA, LORA)
    res["a_lora_up"] = _reduce_adamw_t("adamw_a_lora_up", parts[1], a_lora_up, m_a_lora_up, v_a_lora_up, LORA, LORA)
    as_row = lambda d: {n: d[n] if d[n].ndim > 1 else d[n].reshape(1, size) for n, size in _SMALL}
    vec, loss_row = _adamw_vectors(parts[2], as_row(wts), as_row(mom), as_row(var))
    res.update(vec)
    res = {n: [t.reshape(shapes[n]) for t in res[n]] for n in _WEIGHTS}
    return (loss_row[0, 0], dx[None], *[res[n][0] for n in _WEIGHTS], *[res[n][1] for n in _WEIGHTS],
            *[res[n][2] for n in _WEIGHTS], *[res[n][3] for n in _WEIGHTS])
```

```python
import functools

import numpy as np
import jax
import jax.numpy as jnp
from jax import lax
from jax.experimental import pallas as pl
from jax.experimental.pallas import tpu as pltpu

F32 = jnp.float32
BF16 = jnp.bfloat16

D_MODEL = 1024
CHUNK = 64
RET_HEADS = 4
RET_DV = 128
RET_DK = 64
RET_QK = 256
RET_WIDTH = 512
RWKV_WIDTH = 512
RWKV_HEAD = 64
RWKV_HEADS = 8
LORA = 64
RET_COLS = 2 * RET_QK + 2 * RET_WIDTH
RWKV_COLS = 4 * RWKV_WIDTH + 2 * LORA
IN_COLS = RET_COLS + RWKV_COLS
ROPE_BASE = 10000.0
RMS_EPS = 1e-6
RET_GN_EPS = 1e-5
RWKV_GN_EPS = 64e-5
ADAM_LR = 0.001
ADAM_B1 = 0.9
ADAM_B2 = 0.999
ADAM_EPS = 1e-08
ADAM_WD = 0.01
ADAM_STEP = 10
N_DEV = 8
SHARD_IN = IN_COLS // N_DEV
SHARD_OUT = D_MODEL // N_DEV
SHARD_LORA = RWKV_WIDTH // N_DEV
VMEM_LIMIT = 56 * 1024 * 1024
TOK_TILE = 256
WKV_CHUNK = 64

MESH = pl.DeviceIdType.MESH


def _dot_bf(a, b):
    return jnp.dot(a.astype(BF16), b.astype(BF16), preferred_element_type=F32)


def _dot_nt_bf(a, b):
    return lax.dot_general(a.astype(BF16), b.astype(BF16), (((1,), (1,)), ((), ())), preferred_element_type=F32)


def _dot_tn_bf(a, b):
    return lax.dot_general(a.astype(BF16), b.astype(BF16), (((0,), (0,)), ((), ())), preferred_element_type=F32)


@jax.custom_vjp
def _mm(a, b):
    return _dot_bf(a, b)


@jax.custom_vjp
def _mm_nt(a, b):
    return _dot_nt_bf(a, b)


@jax.custom_vjp
def _mm_tn(a, b):
    return _dot_tn_bf(a, b)


_mm.defvjp(lambda a, b: (_dot_bf(a, b), (a, b)), lambda res, g: (_dot_nt_bf(g, res[1]), _dot_tn_bf(res[0], g)))
_mm_nt.defvjp(lambda a, b: (_dot_nt_bf(a, b), (a, b)), lambda res, g: (_dot_bf(g, res[1]), _dot_tn_bf(g, res[0])))
_mm_tn.defvjp(lambda a, b: (_dot_tn_bf(a, b), (a, b)), lambda res, g: (_dot_nt_bf(res[1], g), _dot_bf(res[0], g)))


def _trunc(x):
    return lax.bitcast_convert_type(lax.bitcast_convert_type(x, jnp.uint32) & jnp.uint32(0xFFFF0000), F32)


def _two_piece(x):
    hi = _trunc(x)
    return jnp.concatenate([hi, x - hi], axis=1)


def _mix_raw(x, mat2):
    return _unstack(jnp.dot(_two_piece(_stack(x)), mat2, preferred_element_type=F32))


@jax.custom_vjp
def _head_mix(x, mat2):
    return _mix_raw(x, mat2)


_head_mix.defvjp(lambda x, mat2: (_mix_raw(x, mat2), mat2), lambda mat2, g: (_mix_raw(g, mat2), jnp.zeros_like(mat2)))


def _swap_halves(x):
    lane = lax.broadcasted_iota(jnp.int32, x.shape, 1)
    return jnp.where((lane & (RET_DK - 1)) < RET_DK // 2, pltpu.roll(x, RET_QK - RET_DK // 2, axis=1),
                     pltpu.roll(x, RET_DK // 2, axis=1))


@jax.custom_vjp
def _rot(x):
    return _swap_halves(x)


_rot.defvjp(lambda x: (_swap_halves(x), None), lambda _, g: (_swap_halves(g),))


def _params(**kw):
    return pltpu.CompilerParams(vmem_limit_bytes=VMEM_LIMIT, **kw)


def _full(shape):
    nd = len(shape)
    return pl.BlockSpec(shape, lambda i, _nd=nd: (0,) * _nd)


def _rows(tile, width):
    return pl.BlockSpec((tile, width), lambda i: (i, 0))


def _block_mix(n, blk, scale=1.0):
    idx = np.arange(n) // blk
    m = (idx[:, None] == idx[None, :]).astype(np.float32) * scale
    return jnp.asarray(np.concatenate([m, m], axis=0))


def _rope_tables(T):
    half = RET_DK // 2
    expo = -np.arange(half, dtype=np.float32) / np.float32(half)
    freqs = np.exp(expo * np.float32(np.log(ROPE_BASE))).astype(np.float32)
    ang = np.arange(T, dtype=np.float32)[:, None] * freqs[None, :]
    cos, sin = np.cos(ang).astype(np.float32), np.sin(ang).astype(np.float32)
    cos_h = np.concatenate([cos, cos], axis=1)
    sin_h = np.concatenate([-sin, sin], axis=1)
    cos_t = np.tile(cos_h, (1, RET_HEADS))
    sin_t = np.tile(sin_h, (1, RET_HEADS))
    return jnp.asarray(cos_t), jnp.asarray(sin_t)


def _ret_tables():
    h = np.arange(RET_HEADS, dtype=np.float32)
    lg = np.log(1.0 - np.exp2(-5.0 - h)).astype(np.float32)
    idx = np.arange(CHUNK, dtype=np.float32)
    intra = np.exp(lg[:, None, None] * np.abs(idx[:, None] - idx[None, :])).astype(np.float32)
    q_dec = np.exp(lg[:, None] * (idx[None, :] + 1.0)).astype(np.float32)
    k_dec = np.exp(lg[:, None] * (CHUNK - 1.0 - idx[None, :])).astype(np.float32)
    chunk_dec = np.exp(lg * CHUNK).astype(np.float32)
    lane_head = np.arange(RET_QK) // RET_DK
    mask = (lane_head[None, :] == np.arange(RET_HEADS)[:, None]).astype(np.float32)
    m = np.broadcast_to(mask[:, None, :], (RET_HEADS, CHUNK, RET_QK)).copy()
    qd = m * q_dec[:, :, None]
    kd = m * k_dec[:, :, None]
    return jnp.asarray(intra), jnp.asarray(m), jnp.asarray(qd), jnp.asarray(kd), [float(c) for c in chunk_dec]


def _rmsnorm(x, g):
    return x * lax.rsqrt(jnp.mean(x * x, axis=-1, keepdims=True) + RMS_EPS) * g


def _inproj(x, norm_g, w_in_t):
    T = x.shape[0]

    def body(x_ref, g_ref, w_ref, pr_ref, pw_ref, u_ref):
        ub = _rmsnorm(x_ref[...], g_ref[...]).astype(BF16)
        u_ref[...] = ub
        pr_ref[...] = _dot_nt_bf(ub, w_ref[:RET_COLS, :])
        pw_ref[...] = _dot_nt_bf(ub, w_ref[RET_COLS:, :])

    return pl.pallas_call(
        body, name="inproj", grid=(T // TOK_TILE,),
        in_specs=[_rows(TOK_TILE, D_MODEL), _full((1, D_MODEL)), _full((IN_COLS, D_MODEL))],
        out_specs=[_rows(TOK_TILE, RET_COLS), _rows(TOK_TILE, RWKV_COLS), _rows(TOK_TILE, D_MODEL)],
        out_shape=[jax.ShapeDtypeStruct((T, RET_COLS), F32), jax.ShapeDtypeStruct((T, RWKV_COLS), F32),
                   jax.ShapeDtypeStruct((T, D_MODEL), BF16)],
        compiler_params=_params(dimension_semantics=("arbitrary",)),
    )(x, norm_g, w_in_t)


def _ret_chunk(pq, pk, v_heads, s_heads, cos_t, sin_t, dec, hm, qd, kd, chunk_dec):
    q = pq * cos_t + _rot(pq) * sin_t
    k = (pk * cos_t + _rot(pk) * sin_t) * (RET_DK ** -0.5)
    outs, s_out = [], []
    for h in range(RET_HEADS):
        sc = _mm_nt(q * hm[h], k * hm[h]) * dec[h]
        intra = _mm(sc, v_heads[h])
        kv = _mm_tn(k * kd[h], v_heads[h])
        inter = _mm(q * qd[h], s_heads[h])
        outs.append(intra + inter)
        s_out.append(s_heads[h] * chunk_dec[h] + kv)
    return tuple(outs), tuple(s_out)


def _ret_specs():
    const = [_full((RET_HEADS, CHUNK, CHUNK)), _full((RET_HEADS, CHUNK, RET_QK)),
             _full((RET_HEADS, CHUNK, RET_QK)), _full((RET_HEADS, CHUNK, RET_QK))]
    return const


RET_GROUP = 4


def _ret_fwd(p_ret, tabs):
    T = p_ret.shape[0]
    G = RET_GROUP
    ng = T // (CHUNK * G)
    cos_t, sin_t, dec, hm, qd, kd, chunk_dec = tabs

    def body(p_ref, cos_ref, sin_ref, dec_ref, hm_ref, qd_ref, kd_ref, out_ref, sin_save_ref, s_scr):
        @pl.when(pl.program_id(0) == 0)
        def _():
            s_scr[...] = jnp.zeros_like(s_scr)

        consts = (dec_ref[...], hm_ref[...], qd_ref[...], kd_ref[...])
        s_heads = tuple(s_scr[h] for h in range(RET_HEADS))
        for c in range(G):
            rows = pl.ds(c * CHUNK, CHUNK)
            for h in range(RET_HEADS):
                sin_save_ref[c, h] = s_heads[h]
            v_heads = tuple(p_ref[rows, 2 * RET_QK + RET_DV * h:2 * RET_QK + RET_DV * (h + 1)] for h in range(RET_HEADS))
            outs, s_heads = _ret_chunk(p_ref[rows, 0:RET_QK], p_ref[rows, RET_QK:2 * RET_QK], v_heads, s_heads,
                                       cos_ref[rows, :], sin_ref[rows, :], *consts, chunk_dec)
            for h in range(RET_HEADS):
                out_ref[rows, RET_DV * h:RET_DV * (h + 1)] = outs[h]
        for h in range(RET_HEADS):
            s_scr[h] = s_heads[h]

    tok = CHUNK * G
    return pl.pallas_call(
        body, name="ret_fwd", grid=(ng,),
        in_specs=[pl.BlockSpec((tok, RET_COLS), lambda i: (i, 0)), _rows(tok, RET_QK), _rows(tok, RET_QK)] + _ret_specs(),
        out_specs=[_rows(tok, RET_WIDTH), pl.BlockSpec((G, RET_HEADS, RET_QK, RET_DV), lambda i: (i, 0, 0, 0))],
        out_shape=[jax.ShapeDtypeStruct((T, RET_WIDTH), F32),
                   jax.ShapeDtypeStruct((T // CHUNK, RET_HEADS, RET_QK, RET_DV), F32)],
        scratch_shapes=[pltpu.VMEM((RET_HEADS, RET_QK, RET_DV), F32)],
        compiler_params=_params(dimension_semantics=("arbitrary",)),
    )(p_ret, cos_t, sin_t, dec, hm, qd, kd)


def _ret_bwd(p_ret, s_saved, d_ret, tabs):
    T = p_ret.shape[0]
    G = RET_GROUP
    ng = T // (CHUNK * G)
    cos_t, sin_t, dec, hm, qd, kd, chunk_dec = tabs

    def body(p_ref, s_ref, dret_ref, cos_ref, sin_ref, dec_ref, hm_ref, qd_ref, kd_ref, dp_ref, ds_scr):
        @pl.when(pl.program_id(0) == 0)
        def _():
            ds_scr[...] = jnp.zeros_like(ds_scr)

        consts = (dec_ref[...], hm_ref[...], qd_ref[...], kd_ref[...])
        d_s = tuple(ds_scr[h] for h in range(RET_HEADS))
        for c in reversed(range(G)):
            rows = pl.ds(c * CHUNK, CHUNK)
            v_heads = tuple(p_ref[rows, 2 * RET_QK + RET_DV * h:2 * RET_QK + RET_DV * (h + 1)] for h in range(RET_HEADS))
            s_heads = tuple(s_ref[c, h] for h in range(RET_HEADS))
            tables = (cos_ref[rows, :], sin_ref[rows, :]) + consts
            _, vjp = jax.vjp(lambda a, b, c_, d: _ret_chunk(a, b, c_, d, *tables, chunk_dec),
                             p_ref[rows, 0:RET_QK], p_ref[rows, RET_QK:2 * RET_QK], v_heads, s_heads)
            d_out = tuple(dret_ref[rows, RET_DV * h:RET_DV * (h + 1)] for h in range(RET_HEADS))
            dq, dk, dv, d_s = vjp((d_out, d_s))
            dp_ref[rows, 0:RET_QK] = dq
            dp_ref[rows, RET_QK:2 * RET_QK] = dk
            for h in range(RET_HEADS):
                dp_ref[rows, 2 * RET_QK + RET_DV * h:2 * RET_QK + RET_DV * (h + 1)] = dv[h]
        for h in range(RET_HEADS):
            ds_scr[h] = d_s[h]

    tok = CHUNK * G
    rev = lambda i: (ng - 1 - i, 0)
    return pl.pallas_call(
        body, name="ret_bwd", grid=(ng,),
        in_specs=[pl.BlockSpec((tok, RET_COLS), rev),
                  pl.BlockSpec((G, RET_HEADS, RET_QK, RET_DV), lambda i: (ng - 1 - i, 0, 0, 0)),
                  pl.BlockSpec((tok, RET_WIDTH), rev), pl.BlockSpec((tok, RET_QK), rev), pl.BlockSpec((tok, RET_QK), rev)]
        + _ret_specs(),
        out_specs=pl.BlockSpec((tok, 2 * RET_QK + RET_WIDTH), rev),
        out_shape=jax.ShapeDtypeStruct((T, 2 * RET_QK + RET_WIDTH), F32),
        scratch_shapes=[pltpu.VMEM((RET_HEADS, RET_QK, RET_DV), F32)],
        compiler_params=_params(dimension_semantics=("arbitrary",)),
    )(p_ret, s_saved, d_ret, cos_t, sin_t, dec, hm, qd, kd)


def _wkv_consts():
    lane = lax.broadcasted_iota(jnp.int32, (RWKV_HEAD, RWKV_WIDTH), 1)
    sub = lax.broadcasted_iota(jnp.int32, (RWKV_HEAD, RWKV_WIDTH), 0)
    diag = ((lane & (RWKV_HEAD - 1)) == sub).astype(F32)
    r = lax.broadcasted_iota(jnp.int32, (3 * 128, 128), 0)
    c = lax.broadcasted_iota(jnp.int32, (3 * 128, 128), 1)
    ones = (((r & 127) >> 6) == (c >> 6)).astype(BF16)
    return diag, ones


def _stack(x):
    return jnp.concatenate([x[:, 128 * p:128 * (p + 1)] for p in range(4)], axis=0)


def _unstack(y):
    n = y.shape[0] // 4
    return jnp.concatenate([y[n * p:n * (p + 1)] for p in range(4)], axis=1)


def _split(x, n):
    pieces = []
    for _ in range(n):
        p = x.astype(BF16)
        pieces.append(p)
        x = x - p.astype(F32)
    return pieces


def _lane_sum(x, ones):
    return _unstack(jnp.dot(_two_piece(_stack(x)), ones[:256].astype(F32), preferred_element_type=F32))


def _lane_sum_bf(x, ones):
    return _unstack(jnp.dot(_stack(x).astype(BF16), ones[:128], preferred_element_type=F32))


def _colsum(x):
    return jnp.sum(x, axis=0, keepdims=True)


def _rows_times(vec, mat):
    lane = lax.broadcasted_iota(jnp.int32, (1, 128), 1)
    tiles = []
    for p in range(4):
        lhs = jnp.concatenate([vec[:, 128 * p:128 * p + RWKV_HEAD], vec[:, 128 * p + RWKV_HEAD:128 * (p + 1)]], axis=0)
        out = jnp.dot(lhs, mat[:, 128 * p:128 * (p + 1)], preferred_element_type=F32)
        tiles.append(jnp.where(lane < RWKV_HEAD, out[0:1], out[1:2]))
    return jnp.concatenate(tiles, axis=1)


def _expand_cols(xt, t):
    lane = lax.broadcasted_iota(jnp.int32, (RWKV_HEAD, 128), 1)
    tiles = []
    for p in range(4):
        lo = jnp.broadcast_to(xt[128 * p:128 * p + RWKV_HEAD, t:t + 1], (RWKV_HEAD, 128))
        hi = jnp.broadcast_to(xt[128 * p + RWKV_HEAD:128 * (p + 1), t:t + 1], (RWKV_HEAD, 128))
        tiles.append(jnp.where(lane < RWKV_HEAD, lo, hi))
    return jnp.concatenate(tiles, axis=1)


def _head_sums(x, ones):
    return _unstack(jnp.dot(jnp.concatenate(_split(_stack(x), 3), axis=1), ones, preferred_element_type=F32))


def _wkv_fwd(r, w, k, v, kap, a):
    T = r.shape[0]
    C = WKV_CHUNK
    nc = T // C

    def body(r_ref, w_ref, k_ref, v_ref, kap_ref, a_ref, o_ref, s_all_ref, sa_all_ref, s_scr):
        @pl.when(pl.program_id(0) == 0)
        def _():
            s_scr[...] = jnp.zeros_like(s_scr)

        diag, ones = _wkv_consts()
        rr, ww, kk, vv, kap_, aa = (ref[...] for ref in (r_ref, w_ref, k_ref, v_ref, kap_ref, a_ref))
        bb = kap_ * aa
        c1 = _head_sums(pltpu.roll(bb, 1, axis=0) * kap_, ones)
        row = lambda x, t: x[t:t + 1]

        v_cols = vv.T

        s_prev = s_scr[...]
        sa = _lane_sum(s_prev * (-row(kap_, 0)), ones)
        ls, rows = None, []

        def emit_o(t, s_t):
            rows.append(_colsum(_lane_sum_bf(s_t * row(rr, t), ones) * diag))
            if t % 8 == 7:
                o_ref[t - 7:t + 1, :] = jnp.concatenate(rows, axis=0)
                rows.clear()

        for t in range(C):
            u = s_prev * row(ww, t) + _expand_cols(v_cols, t) * row(kk, t)
            if t > 0:
                sa = ls - sa * row(c1, t)
            if t + 1 < C:
                ls = _lane_sum(u * (-row(kap_, t + 1)), ones)
            if t > 0:
                emit_o(t - 1, s_prev)
            s_prev = u + sa * row(bb, t)
            s_all_ref[t] = s_prev
            sa_all_ref[t] = sa.astype(BF16)
        emit_o(C - 1, s_prev)
        s_scr[...] = s_prev

    spec = _rows(C, RWKV_WIDTH)
    return pl.pallas_call(
        body, name="wkv_fwd", grid=(nc,),
        in_specs=[spec] * 6,
        out_specs=[spec, pl.BlockSpec((C, RWKV_HEAD, RWKV_WIDTH), lambda i: (i, 0, 0)),
                   pl.BlockSpec((C, RWKV_HEAD, RWKV_WIDTH), lambda i: (i, 0, 0))],
        out_shape=[jax.ShapeDtypeStruct((T, RWKV_WIDTH), F32), jax.ShapeDtypeStruct((T, RWKV_HEAD, RWKV_WIDTH), F32),
                   jax.ShapeDtypeStruct((T, RWKV_HEAD, RWKV_WIDTH), BF16)],
        scratch_shapes=[pltpu.VMEM((RWKV_HEAD, RWKV_WIDTH), F32)],
        compiler_params=_params(dimension_semantics=("arbitrary",)),
    )(r, w, k, v, kap, a)


def _wkv_bwd(r, w, k, v, kap, a, s_all, sa_all, d_o):
    T = r.shape[0]
    C = WKV_CHUNK
    nc = T // C

    def body(r_ref, w_ref, k_ref, v_ref, kap_ref, a_ref, s_ref, s_before_ref, sa_ref, do_ref,
             dr_ref, dw_ref, dk_ref, dv_ref, dkap_ref, da_ref, ds_scr):
        first_chunk = pl.program_id(0) == nc - 1

        @pl.when(pl.program_id(0) == 0)
        def _():
            ds_scr[...] = jnp.zeros_like(ds_scr)

        diag, ones = _wkv_consts()
        rr, ww, kk, vv, kap_, aa, dd = (ref[...] for ref in (r_ref, w_ref, k_ref, v_ref, kap_ref, a_ref, do_ref))
        bb = kap_ * aa
        e1 = _head_sums(pltpu.roll(kap_, C - 1, axis=0) * bb, ones)
        row = lambda x, t: x[t:t + 1]

        def state_before(t):
            return s_ref[t - 1] if t > 0 else jnp.where(first_chunk, 0.0, s_before_ref[0])

        do_cols = dd.T

        d_sn, dsa, rows = None, None, [None] * C

        def emit_rows(t, d_sn_t, dsa_t):
            s_prev, dof = state_before(t), _expand_cols(do_cols, t)
            dv = _colsum(_lane_sum_bf(d_sn_t * row(kk, t), ones) * diag)
            db = _colsum(d_sn_t * sa_ref[t].astype(F32))
            rows[t] = (_colsum(s_ref[t] * dof), _colsum(d_sn_t * s_prev), _rows_times(row(vv, t), d_sn_t), dv,
                       db * row(aa, t) - _colsum(dsa_t * s_prev), db * row(kap_, t))
            if t % 8 == 0:
                for j, ref in enumerate((dr_ref, dw_ref, dk_ref, dv_ref, dkap_ref, da_ref)):
                    ref[t:t + 8, :] = jnp.concatenate([rows[u][j] for u in range(t, t + 8)], axis=0)

        for t in reversed(range(C)):
            dof = _expand_cols(do_cols, t)
            if t == C - 1:
                d_sn = ds_scr[...] + dof * row(rr, t)
                dsa = _lane_sum(d_sn * row(bb, t), ones)
            else:
                v_t = d_sn * row(ww, t + 1) + dof * row(rr, t)
                ls = _lane_sum(v_t * row(bb, t), ones)
                emit_rows(t + 1, d_sn, dsa)
                d_sn = v_t - dsa * row(kap_, t + 1)
                dsa = ls - dsa * row(e1, t)
        emit_rows(0, d_sn, dsa)
        d_s = d_sn * row(ww, 0) - dsa * row(kap_, 0)
        ds_scr[...] = d_s

    spec = pl.BlockSpec((C, RWKV_WIDTH), lambda i: (nc - 1 - i, 0))
    states = pl.BlockSpec((C, RWKV_HEAD, RWKV_WIDTH), lambda i: (nc - 1 - i, 0, 0))
    before = pl.BlockSpec((1, RWKV_HEAD, RWKV_WIDTH), lambda i: (jnp.maximum((nc - 1 - i) * C - 1, 0), 0, 0))
    return pl.pallas_call(
        body, name="wkv_bwd", grid=(nc,),
        in_specs=[spec] * 6 + [states, before, states, spec],
        out_specs=[spec] * 6,
        out_shape=[jax.ShapeDtypeStruct((T, RWKV_WIDTH), F32)] * 6,
        scratch_shapes=[pltpu.VMEM((RWKV_HEAD, RWKV_WIDTH), F32)],
        compiler_params=_params(dimension_semantics=("arbitrary",)),
    )(r, w, k, v, kap, a, s_all, s_all, sa_all, d_o)


W = RWKV_WIDTH


def _softplus(y):
    return jnp.maximum(y, 0.0) + jnp.log(1.0 + jnp.exp(-jnp.abs(y)))


def _prep_fn(kr, xwa, w0, a0, k_k, k_a, wup_pad, aup_pad, ones64):
    w_log = -_softplus(-(w0 + _mm_nt(jnp.tanh(xwa), wup_pad))) - 0.5
    decay = jnp.exp(-jnp.exp(w_log))
    a = jax.nn.sigmoid(a0 + _mm_nt(xwa, aup_pad))
    kk = kr * k_k
    kap = kk / jnp.maximum(jnp.sqrt(_head_mix(kk * kk, ones64)), 1e-12)
    k = kr * (1.0 + (a - 1.0) * k_a)
    return decay, k, kap, a


def _shift_down(p, first_row):
    rows = lax.broadcasted_iota(jnp.int32, p.shape, 0)
    return jnp.where(rows == 0, first_row, pltpu.roll(p, 1, axis=0))


def _shift_up(z, last_row):
    n = z.shape[0]
    rows = lax.broadcasted_iota(jnp.int32, z.shape, 0)
    return jnp.where(rows == n - 1, last_row, pltpu.roll(z, n - 1, axis=0))


def _prev_block_spec():
    return pl.BlockSpec((8, RWKV_COLS), lambda i: (jnp.maximum(i * (TOK_TILE // 8) - 1, 0), 0))


def _mixed(p_ref, prev8_ref, mu_ref, first_tile):
    p = p_ref[...]
    first_row = jnp.where(first_tile, 0.0, prev8_ref[7:8, :])
    prev = _shift_down(p, first_row)
    return p, prev, p + mu_ref[...] * (prev - p)


def _prep_fwd(p_rwkv, mu, w0, a0, k_k, k_a, wup_pad, aup_pad, ones64):
    T = p_rwkv.shape[0]

    def body(p_ref, prev8_ref, mu_ref, w0_ref, a0_ref, kk_ref, ka_ref, wup_ref, aup_ref, ones_ref,
             r_ref, w_ref, k_ref, v_ref, kap_ref, a_ref, g_ref):
        _, _, ps = _mixed(p_ref, prev8_ref, mu_ref, pl.program_id(0) == 0)
        decay, k, kap, a = _prep_fn(ps[:, W:2 * W], ps[:, 4 * W:], w0_ref[...], a0_ref[...], kk_ref[...], ka_ref[...],
                                    wup_ref[...], aup_ref[...], ones_ref[...])
        r_ref[...] = ps[:, 0:W]
        w_ref[...] = decay
        k_ref[...] = k
        v_ref[...] = ps[:, 2 * W:3 * W]
        kap_ref[...] = kap
        a_ref[...] = a
        g_ref[...] = ps[:, 3 * W:4 * W]

    vec = _full((1, W))
    return pl.pallas_call(
        body, name="prep_fwd", grid=(T // TOK_TILE,),
        in_specs=[_rows(TOK_TILE, RWKV_COLS), _prev_block_spec(), _full((1, RWKV_COLS)), vec, vec, vec, vec,
                  _full((W, 2 * LORA)), _full((W, 2 * LORA)), _full((256, 128))],
        out_specs=[_rows(TOK_TILE, W)] * 7,
        out_shape=[jax.ShapeDtypeStruct((T, W), F32)] * 7,
        compiler_params=_params(dimension_semantics=("arbitrary",)),
    )(p_rwkv, p_rwkv, mu, w0, a0, k_k, k_a, wup_pad, aup_pad, ones64)


def _prep_bwd(p_rwkv, mu, w0, a0, k_k, k_a, wup_pad, aup_pad, ones64, dr, dw, dk, dv, dkap, da, dg, dr2, dk2, dv2):
    T = p_rwkv.shape[0]
    nt = T // TOK_TILE

    def body(p_ref, prev8_ref, mu_ref, w0_ref, a0_ref, kk_ref, ka_ref, wup_ref, aup_ref, ones_ref,
             dr_ref, dw_ref, dk_ref, dv_ref, dkap_ref, da_ref, dg_ref, dr2_ref, dk2_ref, dv2_ref,
             dp_ref, dmu_ref, dw0_ref, da0_ref, dkk_ref, dka_ref, dwup_ref, daup_ref, zrow_scr):
        i = pl.program_id(0)
        accs = (dmu_ref, dw0_ref, da0_ref, dkk_ref, dka_ref, dwup_ref, daup_ref)

        @pl.when(i == 0)
        def _():
            zrow_scr[...] = jnp.zeros_like(zrow_scr)
            for ref in accs:
                ref[...] = jnp.zeros_like(ref)

        p, prev, ps = _mixed(p_ref, prev8_ref, mu_ref, i == nt - 1)
        ones = ones_ref[...]
        _, vjp = jax.vjp(lambda *args: _prep_fn(*args, ones), ps[:, W:2 * W], ps[:, 4 * W:], w0_ref[...], a0_ref[...],
                         kk_ref[...], ka_ref[...], wup_ref[...], aup_ref[...])
        dkr, dxwa, dw0, da0, dkk, dka, dwup, daup = vjp(
            (dw_ref[...], dk_ref[...] + dk2_ref[...], dkap_ref[...], da_ref[...]))
        dps = jnp.concatenate([dr_ref[...] + dr2_ref[...], dkr, dv_ref[...] + dv2_ref[...], dg_ref[...], dxwa], axis=1)
        z = dps * mu_ref[...]
        dp_ref[...] = dps - z + _shift_up(z, zrow_scr[0:1, :])
        zrow_scr[0:1, :] = z[0:1, :]
        for ref, val in zip(accs, (_colsum(dps * (prev - p)), dw0, da0, dkk, dka, dwup, daup)):
            ref[...] += val

    rev = lambda i: (nt - 1 - i, 0)
    vec = _full((1, W))
    lora = _full((W, 2 * LORA))
    tile = pl.BlockSpec((TOK_TILE, W), rev)
    prev8 = pl.BlockSpec((8, RWKV_COLS), lambda i: (jnp.maximum((nt - 1 - i) * (TOK_TILE // 8) - 1, 0), 0))
    return pl.pallas_call(
        body, name="prep_bwd", grid=(nt,),
        in_specs=[pl.BlockSpec((TOK_TILE, RWKV_COLS), rev), prev8, _full((1, RWKV_COLS)), vec, vec, vec, vec, lora, lora,
                  _full((256, 128))] + [tile] * 10,
        out_specs=[pl.BlockSpec((TOK_TILE, RWKV_COLS), rev), _full((1, RWKV_COLS)), vec, vec, vec, vec, lora, lora],
        out_shape=[jax.ShapeDtypeStruct((T, RWKV_COLS), F32), jax.ShapeDtypeStruct((1, RWKV_COLS), F32)]
        + [jax.ShapeDtypeStruct((1, W), F32)] * 4 + [jax.ShapeDtypeStruct((W, 2 * LORA), F32)] * 2,
        scratch_shapes=[pltpu.VMEM((8, RWKV_COLS), F32)],
        compiler_params=_params(dimension_semantics=("arbitrary",)),
    )(p_rwkv, p_rwkv, mu, w0, a0, k_k, k_a, wup_pad, aup_pad, ones64, dr, dw, dk, dv, dkap, da, dg, dr2, dk2, dv2)


def _silu(x):
    return x * jax.nn.sigmoid(x)


def _post_y(o, r, k, v, g_rw, ret_raw, g_ret, ret_gn_g, gn_g, gn_b, r_k, avg128, avg64, ones64):
    xc = ret_raw - _head_mix(ret_raw, avg128)
    ret = xc * lax.rsqrt(_head_mix(xc * xc, avg128) + RET_GN_EPS)
    y_ret = _silu(g_ret) * (ret * ret_gn_g)
    oc = o - _head_mix(o, avg64)
    on = oc * lax.rsqrt(_head_mix(oc * oc, avg64) + RWKV_GN_EPS) * gn_g + gn_b
    bonus = _head_mix(r * k * r_k, ones64) * v
    y_rwkv = _silu(g_rw) * (on + bonus)
    return y_ret, y_rwkv


def _post_loss(h, final_g, target):
    err = _rmsnorm(h, final_g) - target
    return 0.5 * jnp.sum(jnp.mean(err * err, axis=-1))


def _post(o, r, k, v, g_rw, ret_raw, p_ret, x, target, ret_gn_g, gn_g, gn_b, r_k, final_g, w_out, avg128, avg64, ones64):
    T = x.shape[0]
    n_tok_out = 8

    def body(o_ref, r_ref, k_ref, v_ref, grw_ref, ret_ref, gret_ref, x_ref, tgt_ref, rg_ref, gg_ref, gb_ref, rk_ref, fg_ref,
             wo_ref, a128_ref, a64_ref, ones_ref, *outs):
        tok_outs, (dwo_ref, drg_ref, dgg_ref, dgb_ref, drk_ref, dfg_ref, loss_ref) = outs[:n_tok_out], outs[n_tok_out:]
        accs = (dwo_ref, drg_ref, dgg_ref, dgb_ref, drk_ref, dfg_ref, loss_ref)

        @pl.when(pl.program_id(0) == 0)
        def _():
            for ref in accs:
                ref[...] = jnp.zeros_like(ref)

        consts = (a128_ref[...], a64_ref[...], ones_ref[...])
        (y_ret, y_rwkv), vjp = jax.vjp(
            lambda *args: _post_y(*args, *consts), o_ref[...], r_ref[...], k_ref[...], v_ref[...], grw_ref[...], ret_ref[...],
            gret_ref[...], rg_ref[...], gg_ref[...], gb_ref[...], rk_ref[...])
        h = x_ref[...] + _dot_bf(y_ret, wo_ref[0:RET_WIDTH, :]) + _dot_bf(y_rwkv, wo_ref[RET_WIDTH:, :])
        loss, (dh, dfg) = jax.value_and_grad(_post_loss, argnums=(0, 1))(h, fg_ref[...], tgt_ref[...])
        dy_ret = _dot_nt_bf(dh, wo_ref[0:RET_WIDTH, :])
        dy_rwkv = _dot_nt_bf(dh, wo_ref[RET_WIDTH:, :])
        do, dr, dk, dv, dgrw, dret, dgret, drg, dgg, dgb, drk = vjp((dy_ret, dy_rwkv))
        for ref, val in zip(tok_outs, (dh, do, dr, dk, dv, dgrw, dret, dgret)):
            ref[...] = val
        dwo_ref[0:RET_WIDTH, :] += _dot_tn_bf(y_ret, dh)
        dwo_ref[RET_WIDTH:, :] += _dot_tn_bf(y_rwkv, dh)
        for ref, val in zip(accs[1:], (drg, dgg, dgb, drk, dfg, jnp.full((1, 128), loss, F32))):
            ref[...] += val

    tile = _rows(TOK_TILE, W)
    wide = _rows(TOK_TILE, D_MODEL)
    vec = _full((1, W))
    sq = _full((256, 128))
    return pl.pallas_call(
        body, name="post", grid=(T // TOK_TILE,),
        in_specs=[tile] * 6 + [pl.BlockSpec((TOK_TILE, W), lambda i: (i, 2)), wide, wide, vec, vec, vec, vec,
                               _full((1, D_MODEL)), _full((D_MODEL, D_MODEL)), sq, sq, sq],
        out_specs=[wide] + [tile] * 7 + [_full((D_MODEL, D_MODEL)), vec, vec, vec, vec, _full((1, D_MODEL)), _full((1, 128))],
        out_shape=[jax.ShapeDtypeStruct((T, D_MODEL), F32)] + [jax.ShapeDtypeStruct((T, W), F32)] * 7
        + [jax.ShapeDtypeStruct((D_MODEL, D_MODEL), F32)] + [jax.ShapeDtypeStruct((1, W), F32)] * 4
        + [jax.ShapeDtypeStruct((1, D_MODEL), F32), jax.ShapeDtypeStruct((1, 128), F32)],
        compiler_params=_params(dimension_semantics=("arbitrary",)),
    )(o, r, k, v, g_rw, ret_raw, p_ret, x, target, ret_gn_g, gn_g, gn_b, r_k, final_g, w_out, avg128, avg64, ones64)


def _inproj_bwd_x(x, norm_g, dp_qkv, dg_ret, dp_rwkv, dh, w_in_t):
    T = x.shape[0]
    n_qkv = 2 * RET_QK + RET_WIDTH

    def body(x_ref, g_ref, dqkv_ref, dgret_ref, drwkv_ref, dh_ref, w_ref, dx_ref, dg_ref):
        @pl.when(pl.program_id(0) == 0)
        def _():
            dg_ref[...] = jnp.zeros_like(dg_ref)

        _, vjp = jax.vjp(_rmsnorm, x_ref[...], g_ref[...])
        du = (_dot_bf(dqkv_ref[...], w_ref[0:n_qkv, :]) + _dot_bf(dgret_ref[...], w_ref[n_qkv:RET_COLS, :])
              + _dot_bf(drwkv_ref[...], w_ref[RET_COLS:, :]))
        dx, dg = vjp(du)
        dx_ref[...] = dx + dh_ref[...]
        dg_ref[...] += dg

    return pl.pallas_call(
        body, name="inproj_bwd_x", grid=(T // TOK_TILE,),
        in_specs=[_rows(TOK_TILE, D_MODEL), _full((1, D_MODEL)), _rows(TOK_TILE, n_qkv), _rows(TOK_TILE, RET_WIDTH),
                  _rows(TOK_TILE, RWKV_COLS), _rows(TOK_TILE, D_MODEL), _full((IN_COLS, D_MODEL))],
        out_specs=[_rows(TOK_TILE, D_MODEL), _full((1, D_MODEL))],
        out_shape=[jax.ShapeDtypeStruct((T, D_MODEL), F32), jax.ShapeDtypeStruct((1, D_MODEL), F32)],
        compiler_params=_params(dimension_semantics=("arbitrary",)),
    )(x, norm_g, dp_qkv, dg_ret, dp_rwkv, dh, w_in_t)


def _grad_w(name, u, dp):
    T, n = dp.shape
    tile = 2 * TOK_TILE
    steps = T // tile

    def body(u_ref, dp_ref, out_ref, acc_ref):
        @pl.when(pl.program_id(0) == 0)
        def _():
            acc_ref[...] = jnp.zeros_like(acc_ref)

        acc_ref[...] += _dot_tn_bf(dp_ref[...], u_ref[...])

        @pl.when(pl.program_id(0) == steps - 1)
        def _():
            out_ref[...] = acc_ref[...].astype(BF16)

    return pl.pallas_call(
        body, name=name, grid=(steps,),
        in_specs=[_rows(tile, D_MODEL), _rows(tile, n)],
        out_specs=_full((n, D_MODEL)),
        out_shape=jax.ShapeDtypeStruct((n, D_MODEL), BF16),
        scratch_shapes=[pltpu.VMEM((n, D_MODEL), F32)],
        compiler_params=_params(dimension_semantics=("arbitrary",)),
    )(u, dp)


def _pad_lora(w_up_t, first):
    z = jnp.zeros_like(w_up_t)
    return jnp.concatenate([w_up_t, z] if first else [z, w_up_t], axis=1)


def _local_grads(x, target, norm_g, w_in_t, ret_gn_g, mu, w_lora_up_t, w0, a_lora_up_t, a0, k_k, k_a, r_k, gn_g, gn_b,
                 w_out_bf, final_g):
    T = x.shape[0]
    tabs = _rope_tables(T) + _ret_tables()
    ones64 = _block_mix(128, RWKV_HEAD)
    avg64 = _block_mix(128, RWKV_HEAD, 1.0 / RWKV_HEAD)
    avg128 = _block_mix(128, RET_DV, 1.0 / RET_DV)
    wup_pad, aup_pad = _pad_lora(w_lora_up_t, True), _pad_lora(a_lora_up_t, False)

    p_ret, p_rwkv, u = _inproj(x, norm_g, w_in_t)
    ret_raw, s_saved = _ret_fwd(p_ret, tabs)
    r, w, k, v, kap, a, g_rw = _prep_fwd(p_rwkv, mu, w0, a0, k_k, k_a, wup_pad, aup_pad, ones64)
    o, s_all, sa_all = _wkv_fwd(r, w, k, v, kap, a)
    (dh, do, dr2, dk2, dv2, dgrw, dret, dgret, d_w_out, d_ret_gn_g, d_gn_g, d_gn_b, d_r_k, d_final_g, loss) = _post(
        o, r, k, v, g_rw, ret_raw, p_ret, x, target, ret_gn_g, gn_g, gn_b, r_k, final_g, w_out_bf, avg128, avg64, ones64)
    dr, dw, dk, dv, dkap, da = _wkv_bwd(r, w, k, v, kap, a, s_all, sa_all, do)
    dp_rwkv, d_mu, d_w0, d_a0, d_k_k, d_k_a, d_wup, d_aup = _prep_bwd(
        p_rwkv, mu, w0, a0, k_k, k_a, wup_pad, aup_pad, ones64, dr, dw, dk, dv, dkap, da, dgrw, dr2, dk2, dv2)
    dp_qkv = _ret_bwd(p_ret, s_saved, dret, tabs)
    dx, d_norm_g = _inproj_bwd_x(x, norm_g, dp_qkv, dgret, dp_rwkv, dh, w_in_t)
    d_w_in = jnp.concatenate([_grad_w("grad_w_qkv", u, dp_qkv), _grad_w("grad_w_gret", u, dgret),
                              _grad_w("grad_w_rwkv", u, dp_rwkv)], axis=0)
    grads = dict(norm_g=d_norm_g, w_in=d_w_in, ret_gn_g=d_ret_gn_g, rwkv_mu=d_mu, w_lora_up=d_wup[:, :LORA], w0=d_w0,
                 a_lora_up=d_aup[:, LORA:], a0=d_a0, k_k=d_k_k, k_a=d_k_a, r_k=d_r_k, rwkv_gn_g=d_gn_g, rwkv_gn_b=d_gn_b,
                 w_out=d_w_out, final_norm_g=d_final_g)
    return loss, dx, grads


def _mesh_pos():
    return lax.axis_index("x"), lax.axis_index("y"), lax.axis_index("c")


def _all_gather(shards):
    n = len(shards)

    def body(*refs):
        x_refs, out_refs = refs[:n], refs[n:2 * n]
        send_sems, recv_sems, local_sems = refs[2 * n:]
        x, y, c = _mesh_pos()
        me, sibling = (x, y, c), (x, y, 1 - c)
        chips = [(1 - x, y), (x, 1 - y), (1 - x, 1 - y)]

        def rows(a, pos):
            m = x_refs[a].shape[0]
            return out_refs[a].at[pl.ds((4 * pos[0] + 2 * pos[1] + pos[2]) * m, m), :]

        def copy(a, k, block, to, src=None):
            return pltpu.make_async_remote_copy(
                src_ref=rows(a, block) if src is None else src, dst_ref=rows(a, block),
                send_sem=send_sems.at[a, k], recv_sem=recv_sems.at[a, k], device_id=to, device_id_type=MESH)

        mine = [pltpu.make_async_copy(x_refs[a], rows(a, me), local_sems.at[a]) for a in range(n)]
        for cp in mine:
            cp.start()
        first = []
        for a in range(n):
            first.append(copy(a, 0, me, sibling, src=x_refs[a]))
            first += [copy(a, 1 + j, me, (*chip, c), src=x_refs[a]) for j, chip in enumerate(chips)]
        for cp in first:
            cp.start()
        passed = []
        for j, chip in enumerate(chips):
            for a in range(n):
                copy(a, 1 + j, (*chip, c), me).wait_recv()
                passed.append(copy(a, 4 + j, (*chip, c), sibling))
                passed[-1].start()
        for a in range(n):
            copy(a, 0, sibling, me).wait_recv()
            for j, chip in enumerate(chips):
                copy(a, 4 + j, (*chip, 1 - c), me).wait_recv()
        for cp in first + passed:
            cp.wait_send()
        for cp in mine:
            cp.wait()

    vmem = pl.BlockSpec(memory_space=pltpu.VMEM)
    return pl.pallas_call(
        body, name="gather_weights",
        out_shape=[jax.ShapeDtypeStruct((N_DEV * s.shape[0], s.shape[1]), s.dtype) for s in shards],
        in_specs=[vmem] * n, out_specs=[vmem] * n,
        scratch_shapes=[pltpu.SemaphoreType.DMA((n, 7)), pltpu.SemaphoreType.DMA((n, 7)), pltpu.SemaphoreType.DMA((n,))],
        compiler_params=_params(),
    )(*shards)


N_CHIP = 4


def _exchange_pairs(big, small):
    nb, ns = len(big), len(small)

    def body(*refs):
        big_in, small_in = refs[:nb], refs[nb:nb + ns]
        theirs, small_out = refs[nb + ns:2 * nb + ns], refs[2 * nb + ns:2 * nb + 2 * ns]
        pair_send, pair_recv, send_sems, recv_sems, local_sems = refs[2 * nb + 2 * ns:]
        x, y, c = _mesh_pos()
        me = 4 * x + 2 * y + c
        local = [pltpu.make_async_copy(small_in[a].at[me], small_out[a].at[me], local_sems.at[a]) for a in range(ns)]
        for cp in local:
            cp.start()
        copies = [pltpu.make_async_remote_copy(
            src_ref=big_in[a], dst_ref=theirs[a], send_sem=pair_send.at[a], recv_sem=pair_recv.at[a],
            device_id=(x, y, 1 - c), device_id_type=MESH) for a in range(nb)]
        for k in range(1, N_DEV):
            peer = (x ^ (k >> 2), y ^ ((k >> 1) & 1), c ^ (k & 1))
            peer_idx = 4 * peer[0] + 2 * peer[1] + peer[2]
            copies += [pltpu.make_async_remote_copy(
                src_ref=small_in[a].at[peer_idx], dst_ref=small_out[a].at[me], send_sem=send_sems.at[a, k - 1],
                recv_sem=recv_sems.at[a, k - 1], device_id=peer, device_id_type=MESH) for a in range(ns)]
        for cp in copies:
            cp.start()
        for cp in copies:
            cp.wait()
        for cp in local:
            cp.wait()

    hbm = pl.BlockSpec(memory_space=pl.ANY)
    out_shape = [jax.ShapeDtypeStruct(p.shape, p.dtype) for p in big + small]
    dma = pltpu.SemaphoreType.DMA
    res = pl.pallas_call(
        body, name="exchange_pairs", out_shape=out_shape,
        in_specs=[hbm] * (nb + ns), out_specs=[hbm] * len(out_shape),
        scratch_shapes=[dma((nb,)), dma((nb,)), dma((ns, 7)), dma((ns, 7)), dma((ns,))],
        compiler_params=_params(),
    )(*big, *small)
    return res[:nb], res[nb:]


def _pair_sum(name, mine, theirs, row_tile):
    _, rows, cols = mine.shape

    def body(a_ref, b_ref, o_ref):
        o_ref[...] = (a_ref[...].astype(F32) + b_ref[...].astype(F32)).astype(o_ref.dtype)

    spec = pl.BlockSpec((N_CHIP, row_tile, cols), lambda i: (0, i, 0))
    return pl.pallas_call(
        body, name=name, grid=(rows // row_tile,), in_specs=[spec, spec], out_specs=spec,
        out_shape=jax.ShapeDtypeStruct(mine.shape, mine.dtype),
        compiler_params=_params(dimension_semantics=("arbitrary",)),
    )(mine, theirs)


def _exchange_chips(parts):
    n = len(parts)

    def body(*refs):
        in_refs, out_refs = refs[:n], refs[n:2 * n]
        send_sems, recv_sems, local_sems = refs[2 * n:]
        x, y, c = _mesh_pos()
        my_chip = 2 * x + y
        own = [pltpu.make_async_copy(in_refs[a].at[my_chip], out_refs[a].at[my_chip], local_sems.at[a]) for a in range(n)]
        for cp in own:
            cp.start()
        copies = []
        for k in range(1, N_CHIP):
            px, py = x ^ (k >> 1), y ^ (k & 1)
            copies += [pltpu.make_async_remote_copy(
                src_ref=in_refs[a].at[2 * px + py], dst_ref=out_refs[a].at[my_chip], send_sem=send_sems.at[a, k - 1],
                recv_sem=recv_sems.at[a, k - 1], device_id=(px, py, c), device_id_type=MESH) for a in range(n)]
        for cp in copies:
            cp.start()
        for cp in copies:
            cp.wait()
        for cp in own:
            cp.wait()

    hbm = pl.BlockSpec(memory_space=pl.ANY)
    dma = pltpu.SemaphoreType.DMA
    return pl.pallas_call(
        body, name="exchange_chips",
        out_shape=[jax.ShapeDtypeStruct(p.shape, p.dtype) for p in parts],
        in_specs=[hbm] * n, out_specs=[hbm] * n,
        scratch_shapes=[dma((n, N_CHIP - 1)), dma((n, N_CHIP - 1)), dma((n,))],
        compiler_params=_params(),
    )(*parts)


def _adamw(w, g, m, v):
    m = ADAM_B1 * m + (1.0 - ADAM_B1) * g
    v = ADAM_B2 * v + (1.0 - ADAM_B2) * (g * g)
    m_hat = m / (1.0 - ADAM_B1 ** ADAM_STEP)
    v_hat = v / (1.0 - ADAM_B2 ** ADAM_STEP)
    return -ADAM_LR * (m_hat / (jnp.sqrt(v_hat) + ADAM_EPS) + ADAM_WD * w), m, v


def _sum_parts(name, parts, row_tile):
    n_parts, rows, cols = parts.shape

    def body(p_ref, g_ref):
        g = p_ref[0].astype(F32)
        for s in range(1, n_parts):
            g = g + p_ref[s].astype(F32)
        g_ref[...] = g

    return pl.pallas_call(
        body, name=name, grid=(rows // row_tile,),
        in_specs=[pl.BlockSpec((n_parts, row_tile, cols), lambda i: (0, i, 0))],
        out_specs=pl.BlockSpec((row_tile, cols), lambda i: (i, 0)),
        out_shape=jax.ShapeDtypeStruct((rows, cols), F32),
        compiler_params=_params(dimension_semantics=("arbitrary",)),
    )(parts)


def _adamw_apply(name, g, w, m, v, row_tile):
    _, rows, cols = w.shape

    def body(g_ref, w_ref, m_ref, v_ref, d_ref, nm_ref, nv_ref):
        d_ref[0], nm_ref[0], nv_ref[0] = _adamw(w_ref[0], g_ref[0], m_ref[0], v_ref[0])

    tile = pl.BlockSpec((1, row_tile, cols), lambda i: (0, i, 0))
    return pl.pallas_call(
        body, name=name, grid=(rows // row_tile,), in_specs=[tile] * 4, out_specs=[tile] * 3,
        out_shape=[jax.ShapeDtypeStruct((1, rows, cols), F32)] * 3,
        compiler_params=_params(dimension_semantics=("arbitrary",)),
    )(g, w, m, v)


def _reduce_adamw_t(name, parts_t, w, m, v, sum_tile, row_tile):
    g = _sum_parts(name + "_sum", parts_t, sum_tile).T[None]
    return [g] + list(_adamw_apply(name, g, w, m, v, row_tile))


def _reduce_adamw(name, parts, w, m, v, row_tile):
    n_parts, rows, cols = parts.shape

    def body(p_ref, w_ref, m_ref, v_ref, g_ref, d_ref, nm_ref, nv_ref):
        g = p_ref[0].astype(F32)
        for s in range(1, n_parts):
            g = g + p_ref[s].astype(F32)
        g_ref[0] = g
        d_ref[0], nm_ref[0], nv_ref[0] = _adamw(w_ref[0], g, m_ref[0], v_ref[0])

    tile = pl.BlockSpec((1, row_tile, cols), lambda i: (0, i, 0))
    return pl.pallas_call(
        body, name=name, grid=(rows // row_tile,),
        in_specs=[pl.BlockSpec((n_parts, row_tile, cols), lambda i: (0, i, 0)), tile, tile, tile],
        out_specs=[tile] * 4,
        out_shape=[jax.ShapeDtypeStruct((1, rows, cols), F32)] * 4,
        compiler_params=_params(dimension_semantics=("arbitrary",)),
    )(parts, w, m, v)


_SMALL = (("norm_g", 1024), ("ret_gn_g", 512), ("rwkv_mu", 2176), ("w0", 512), ("a0", 512), ("k_k", 512), ("k_a", 512),
          ("r_k", 512), ("rwkv_gn_g", 512), ("rwkv_gn_b", 512), ("final_norm_g", 1024))
_SMALL_LANES = sum(n for _, n in _SMALL) + 128
_WEIGHTS = ("norm_g", "w_in", "ret_gn_g", "rwkv_mu", "w_lora_up", "w0", "a_lora_up", "a0", "k_k", "k_a", "r_k", "rwkv_gn_g",
            "rwkv_gn_b", "w_out", "final_norm_g")


def _adamw_vectors(parts, wts, mom, var):
    k = len(_SMALL)

    def body(p_ref, *refs):
        w_refs, m_refs, v_refs, outs = refs[:k], refs[k:2 * k], refs[2 * k:3 * k], refs[3 * k:]
        g_all = p_ref[0]
        for s in range(1, N_DEV):
            g_all = g_all + p_ref[s]
        off = 0
        for i, (name, n) in enumerate(_SMALL):
            g = g_all[:, off:off + n]
            off += n
            if name == "r_k":
                g = jnp.concatenate([g[:, RWKV_HEAD * h:RWKV_HEAD * (h + 1)] for h in range(RWKV_HEADS)], axis=0)[None]
            outs[4 * i][...] = g
            outs[4 * i + 1][...], outs[4 * i + 2][...], outs[4 * i + 3][...] = _adamw(
                w_refs[i][...], g, m_refs[i][...], v_refs[i][...])
        outs[4 * k][...] = g_all[:, off:off + 128]

    vmem = pl.BlockSpec(memory_space=pltpu.VMEM)
    shapes = [jax.ShapeDtypeStruct(wts[n].shape, F32) for n, _ in _SMALL for _ in range(4)] + [jax.ShapeDtypeStruct((1, 128), F32)]
    res = pl.pallas_call(
        body, name="adamw_vectors", out_shape=shapes,
        in_specs=[vmem] * (1 + 3 * k), out_specs=[vmem] * len(shapes), compiler_params=_params(),
    )(parts, *[wts[n] for n, _ in _SMALL], *[mom[n] for n, _ in _SMALL], *[var[n] for n, _ in _SMALL])
    return {n: res[4 * i:4 * i + 4] for i, (n, _) in enumerate(_SMALL)}, res[4 * k]


def kernel(x, norm_g, w_in, ret_gn_g, rwkv_mu, w_lora_up, w0, a_lora_up, a0, k_k, k_a, r_k, rwkv_gn_g, rwkv_gn_b, w_out, final_norm_g, loss_target, m_norm_g, m_w_in, m_ret_gn_g, m_rwkv_mu, m_w_lora_up, m_w0, m_a_lora_up, m_a0, m_k_k, m_k_a, m_r_k, m_rwkv_gn_g, m_rwkv_gn_b, m_w_out, m_final_norm_g, v_norm_g, v_w_in, v_ret_gn_g, v_rwkv_mu, v_w_lora_up, v_w0, v_a_lora_up, v_a0, v_k_k, v_k_a, v_r_k, v_rwkv_gn_g, v_rwkv_gn_b, v_w_out, v_final_norm_g):
    wts = dict(norm_g=norm_g, w_in=w_in, ret_gn_g=ret_gn_g, rwkv_mu=rwkv_mu, w_lora_up=w_lora_up, w0=w0, a_lora_up=a_lora_up,
               a0=a0, k_k=k_k, k_a=k_a, r_k=r_k, rwkv_gn_g=rwkv_gn_g, rwkv_gn_b=rwkv_gn_b, w_out=w_out,
               final_norm_g=final_norm_g)
    mom = dict(norm_g=m_norm_g, w_in=m_w_in, ret_gn_g=m_ret_gn_g, rwkv_mu=m_rwkv_mu, w_lora_up=m_w_lora_up, w0=m_w0,
               a_lora_up=m_a_lora_up, a0=m_a0, k_k=m_k_k, k_a=m_k_a, r_k=m_r_k, rwkv_gn_g=m_rwkv_gn_g,
               rwkv_gn_b=m_rwkv_gn_b, w_out=m_w_out, final_norm_g=m_final_norm_g)
    var = dict(norm_g=v_norm_g, w_in=v_w_in, ret_gn_g=v_ret_gn_g, rwkv_mu=v_rwkv_mu, w_lora_up=v_w_lora_up, w0=v_w0,
               a_lora_up=v_a_lora_up, a0=v_a0, k_k=v_k_k, k_a=v_k_a, r_k=v_r_k, rwkv_gn_g=v_rwkv_gn_g,
               rwkv_gn_b=v_rwkv_gn_b, w_out=v_w_out, final_norm_g=v_final_norm_g)
    shapes = {n: wts[n].shape for n in _WEIGHTS}

    w_in_t, w_out_bf, wup_t, aup_t = _all_gather(
        [w_in[0].T.astype(BF16), w_out[0].astype(BF16), w_lora_up[0].T, a_lora_up[0].T])

    loss, dx, g = _local_grads(
        x[0], loss_target[0], norm_g, w_in_t, ret_gn_g, rwkv_mu, wup_t, w0, aup_t, a0, k_k, k_a,
        r_k.reshape(1, W), rwkv_gn_g, rwkv_gn_b, w_out_bf, final_norm_g.reshape(1, D_MODEL))

    small = jnp.concatenate([g[n] for n, _ in _SMALL] + [loss], axis=1)
    core = lax.axis_index("c")
    by_core = lambda t: [lax.dynamic_index_in_dim(t, i, axis=1, keepdims=False) for i in (core, 1 - core)]
    in_mine, in_sib = by_core(g["w_in"].reshape(N_CHIP, 2, SHARD_IN, D_MODEL))
    out_mine, out_sib = by_core(g["w_out"].reshape(N_CHIP, 2, SHARD_OUT, D_MODEL).astype(BF16))
    (in_theirs, out_theirs), parts = _exchange_pairs(
        [in_sib, out_sib],
        [g["w_lora_up"].reshape(N_DEV, SHARD_LORA, LORA), g["a_lora_up"].reshape(N_DEV, SHARD_LORA, LORA),
         jnp.broadcast_to(small[None], (N_DEV, 1, _SMALL_LANES))])
    by_chip = _exchange_chips([_pair_sum("pair_sum_w_in", in_mine, in_theirs, SHARD_IN // 2),
                               _pair_sum("pair_sum_w_out", out_mine, out_theirs, SHARD_OUT)])
    res = {}
    res["w_in"] = _reduce_adamw_t("adamw_w_in", by_chip[0], w_in, m_w_in, v_w_in, SHARD_IN // 2, 256)
    res["w_out"] = _reduce_adamw("adamw_w_out", by_chip[1], w_out, m_w_out, v_w_out, SHARD_OUT)
    res["w_lora_up"] = _reduce_adamw_t("adamw_w_lora_up", parts[0], w_lora_up, m_w_lora_up, v_w_lora_up, LORA, LORA)
    res["a_lora_up"] = _reduce_adamw_t("adamw_a_lora_up", parts[1], a_lora_up, m_a_lora_up, v_a_lora_up, LORA, LORA)
    as_row = lambda d: {n: d[n] if d[n].ndim > 1 else d[n].reshape(1, size) for n, size in _SMALL}
    vec, loss_row = _adamw_vectors(parts[2], as_row(wts), as_row(mom), as_row(var))
    res.update(vec)
    res = {n: [t.reshape(shapes[n]) for t in res[n]] for n in _WEIGHTS}
    return (loss_row[0, 0], dx[None], *[res[n][0] for n in _WEIGHTS], *[res[n][1] for n in _WEIGHTS],
            *[res[n][2] for n in _WEIGHTS], *[res[n][3] for n in _WEIGHTS])
```

```python
import numpy as np
import jax
import jax.numpy as jnp
from jax import lax
from jax.experimental import pallas as pl
from jax.experimental.pallas import tpu as pltpu

F32 = jnp.float32
BF16 = jnp.bfloat16

D_MODEL = 1024
CHUNK = 64
RET_HEADS = 4
RET_DV = 128
RET_DK = 64
RET_QK = 256
RET_WIDTH = 512
RWKV_WIDTH = 512
RWKV_HEAD = 64
RWKV_HEADS = 8
LORA = 64
RET_COLS = 2 * RET_QK + 2 * RET_WIDTH
RWKV_COLS = 4 * RWKV_WIDTH + 2 * LORA
IN_COLS = RET_COLS + RWKV_COLS
ROPE_BASE = 10000.0
RMS_EPS = 1e-6
RET_GN_EPS = 1e-5
RWKV_GN_EPS = 64e-5
ADAM_LR = 0.001
ADAM_B1 = 0.9
ADAM_B2 = 0.999
ADAM_EPS = 1e-08
ADAM_WD = 0.01
ADAM_STEP = 10
N_DEV = 8
SHARD_IN = IN_COLS // N_DEV
SHARD_OUT = D_MODEL // N_DEV
SHARD_LORA = RWKV_WIDTH // N_DEV
VMEM_LIMIT = 56 * 1024 * 1024
TOK_TILE = 256
WKV_CHUNK = 64

MESH = pl.DeviceIdType.MESH


def _dot_bf(a, b):
    return jnp.dot(a.astype(BF16), b.astype(BF16), preferred_element_type=F32)


def _dot_nt_bf(a, b):
    return lax.dot_general(a.astype(BF16), b.astype(BF16), (((1,), (1,)), ((), ())), preferred_element_type=F32)


def _dot_tn_bf(a, b):
    return lax.dot_general(a.astype(BF16), b.astype(BF16), (((0,), (0,)), ((), ())), preferred_element_type=F32)


@jax.custom_vjp
def _mm(a, b):
    return _dot_bf(a, b)


@jax.custom_vjp
def _mm_nt(a, b):
    return _dot_nt_bf(a, b)


@jax.custom_vjp
def _mm_tn(a, b):
    return _dot_tn_bf(a, b)


_mm.defvjp(lambda a, b: (_dot_bf(a, b), (a, b)), lambda res, g: (_dot_nt_bf(g, res[1]), _dot_tn_bf(res[0], g)))
_mm_nt.defvjp(lambda a, b: (_dot_nt_bf(a, b), (a, b)), lambda res, g: (_dot_bf(g, res[1]), _dot_tn_bf(g, res[0])))
_mm_tn.defvjp(lambda a, b: (_dot_tn_bf(a, b), (a, b)), lambda res, g: (_dot_nt_bf(res[1], g), _dot_bf(res[0], g)))


def _trunc(x):
    return lax.bitcast_convert_type(lax.bitcast_convert_type(x, jnp.uint32) & jnp.uint32(0xFFFF0000), F32)


def _two_piece(x):
    hi = _trunc(x)
    return jnp.concatenate([hi, x - hi], axis=1)


def _mix_raw(x, mat2):
    return _unstack(jnp.dot(_two_piece(_stack(x)), mat2, preferred_element_type=F32))


@jax.custom_vjp
def _head_mix(x, mat2):
    return _mix_raw(x, mat2)


_head_mix.defvjp(lambda x, mat2: (_mix_raw(x, mat2), mat2), lambda mat2, g: (_mix_raw(g, mat2), jnp.zeros_like(mat2)))


def _swap_halves(x):
    lane = lax.broadcasted_iota(jnp.int32, x.shape, 1)
    return jnp.where((lane & (RET_DK - 1)) < RET_DK // 2, pltpu.roll(x, RET_QK - RET_DK // 2, axis=1),
                     pltpu.roll(x, RET_DK // 2, axis=1))


@jax.custom_vjp
def _rot(x):
    return _swap_halves(x)


_rot.defvjp(lambda x: (_swap_halves(x), None), lambda _, g: (_swap_halves(g),))


def _params(**kw):
    return pltpu.CompilerParams(vmem_limit_bytes=VMEM_LIMIT, **kw)


def _full(shape):
    nd = len(shape)
    return pl.BlockSpec(shape, lambda i, _nd=nd: (0,) * _nd)


def _rows(tile, width):
    return pl.BlockSpec((tile, width), lambda i: (i, 0))


def _rows_of_one(tile, width):
    return pl.BlockSpec((None, tile, width), lambda i: (0, i, 0))


def _block_mix(n, blk, scale=1.0):
    idx = np.arange(n) // blk
    m = (idx[:, None] == idx[None, :]).astype(np.float32) * scale
    return jnp.asarray(np.concatenate([m, m], axis=0))


def _rope_tables(T):
    half = RET_DK // 2
    expo = -np.arange(half, dtype=np.float32) / np.float32(half)
    freqs = np.exp(expo * np.float32(np.log(ROPE_BASE))).astype(np.float32)
    ang = np.arange(T, dtype=np.float32)[:, None] * freqs[None, :]
    cos, sin = np.cos(ang).astype(np.float32), np.sin(ang).astype(np.float32)
    cos_h = np.concatenate([cos, cos], axis=1)
    sin_h = np.concatenate([-sin, sin], axis=1)
    cos_t = np.tile(cos_h, (1, RET_HEADS))
    sin_t = np.tile(sin_h, (1, RET_HEADS))
    return jnp.asarray(cos_t), jnp.asarray(sin_t)


def _ret_tables():
    h = np.arange(RET_HEADS, dtype=np.float32)
    lg = np.log(1.0 - np.exp2(-5.0 - h)).astype(np.float32)
    idx = np.arange(CHUNK, dtype=np.float32)
    intra = np.exp(lg[:, None, None] * np.abs(idx[:, None] - idx[None, :])).astype(np.float32)
    q_dec = np.exp(lg[:, None] * (idx[None, :] + 1.0)).astype(np.float32)
    k_dec = np.exp(lg[:, None] * (CHUNK - 1.0 - idx[None, :])).astype(np.float32)
    chunk_dec = np.exp(lg * CHUNK).astype(np.float32)
    lane_head = np.arange(RET_QK) // RET_DK
    mask = (lane_head[None, :] == np.arange(RET_HEADS)[:, None]).astype(np.float32)
    m = np.broadcast_to(mask[:, None, :], (RET_HEADS, CHUNK, RET_QK)).copy()
    qd = m * q_dec[:, :, None]
    kd = m * k_dec[:, :, None]
    return jnp.asarray(intra), jnp.asarray(m), jnp.asarray(qd), jnp.asarray(kd), [float(c) for c in chunk_dec]


def _rmsnorm(x, g):
    return x * lax.rsqrt(jnp.mean(x * x, axis=-1, keepdims=True) + RMS_EPS) * g


def _inproj(x, norm_g, w_in_t):
    T = x.shape[1]

    def body(x_ref, g_ref, w_ref, pr_ref, pw_ref, u_ref):
        ub = _rmsnorm(x_ref[...], g_ref[...]).astype(BF16)
        u_ref[...] = ub
        pr_ref[...] = _dot_nt_bf(ub, w_ref[:RET_COLS, :])
        pw_ref[...] = _dot_nt_bf(ub, w_ref[RET_COLS:, :])

    return pl.pallas_call(
        body, name="inproj", grid=(T // TOK_TILE,),
        in_specs=[_rows_of_one(TOK_TILE, D_MODEL), _full((1, D_MODEL)), _full((IN_COLS, D_MODEL))],
        out_specs=[_rows(TOK_TILE, RET_COLS), _rows(TOK_TILE, RWKV_COLS), _rows(TOK_TILE, D_MODEL)],
        out_shape=[jax.ShapeDtypeStruct((T, RET_COLS), F32), jax.ShapeDtypeStruct((T, RWKV_COLS), F32),
                   jax.ShapeDtypeStruct((T, D_MODEL), BF16)],
        compiler_params=_params(dimension_semantics=("arbitrary",)),
    )(x, norm_g, w_in_t)


def _ret_chunk(pq, pk, v_heads, s_heads, cos_t, sin_t, dec, hm, qd, kd, chunk_dec):
    q = pq * cos_t + _rot(pq) * sin_t
    k = (pk * cos_t + _rot(pk) * sin_t) * (RET_DK ** -0.5)
    outs, s_out = [], []
    for h in range(RET_HEADS):
        sc = _mm_nt(q * hm[h], k * hm[h]) * dec[h]
        intra = _mm(sc, v_heads[h])
        kv = _mm_tn(k * kd[h], v_heads[h])
        inter = _mm(q * qd[h], s_heads[h])
        outs.append(intra + inter)
        s_out.append(s_heads[h] * chunk_dec[h] + kv)
    return tuple(outs), tuple(s_out)


def _ret_specs():
    const = [_full((RET_HEADS, CHUNK, CHUNK)), _full((RET_HEADS, CHUNK, RET_QK)),
             _full((RET_HEADS, CHUNK, RET_QK)), _full((RET_HEADS, CHUNK, RET_QK))]
    return const


RET_GROUP = 4


def _ret_fwd(p_ret, tabs):
    T = p_ret.shape[0]
    G = RET_GROUP
    ng = T // (CHUNK * G)
    cos_t, sin_t, dec, hm, qd, kd, chunk_dec = tabs

    def body(p_ref, cos_ref, sin_ref, dec_ref, hm_ref, qd_ref, kd_ref, out_ref, sin_save_ref, s_scr):
        @pl.when(pl.program_id(0) == 0)
        def _():
            s_scr[...] = jnp.zeros_like(s_scr)

        consts = (dec_ref[...], hm_ref[...], qd_ref[...], kd_ref[...])
        s_heads = tuple(s_scr[h] for h in range(RET_HEADS))
        for c in range(G):
            rows = pl.ds(c * CHUNK, CHUNK)
            for h in range(RET_HEADS):
                sin_save_ref[c, h] = s_heads[h]
            v_heads = tuple(p_ref[rows, 2 * RET_QK + RET_DV * h:2 * RET_QK + RET_DV * (h + 1)] for h in range(RET_HEADS))
            outs, s_heads = _ret_chunk(p_ref[rows, 0:RET_QK], p_ref[rows, RET_QK:2 * RET_QK], v_heads, s_heads,
                                       cos_ref[rows, :], sin_ref[rows, :], *consts, chunk_dec)
            for h in range(RET_HEADS):
                out_ref[rows, RET_DV * h:RET_DV * (h + 1)] = outs[h]
        for h in range(RET_HEADS):
            s_scr[h] = s_heads[h]

    tok = CHUNK * G
    return pl.pallas_call(
        body, name="ret_fwd", grid=(ng,),
        in_specs=[pl.BlockSpec((tok, RET_COLS), lambda i: (i, 0)), _rows(tok, RET_QK), _rows(tok, RET_QK)] + _ret_specs(),
        out_specs=[_rows(tok, RET_WIDTH), pl.BlockSpec((G, RET_HEADS, RET_QK, RET_DV), lambda i: (i, 0, 0, 0))],
        out_shape=[jax.ShapeDtypeStruct((T, RET_WIDTH), F32),
                   jax.ShapeDtypeStruct((T // CHUNK, RET_HEADS, RET_QK, RET_DV), F32)],
        scratch_shapes=[pltpu.VMEM((RET_HEADS, RET_QK, RET_DV), F32)],
        compiler_params=_params(dimension_semantics=("arbitrary",)),
    )(p_ret, cos_t, sin_t, dec, hm, qd, kd)


def _ret_bwd(p_ret, s_saved, d_ret, tabs):
    T = p_ret.shape[0]
    G = RET_GROUP
    ng = T // (CHUNK * G)
    cos_t, sin_t, dec, hm, qd, kd, chunk_dec = tabs

    def body(p_ref, s_ref, dret_ref, cos_ref, sin_ref, dec_ref, hm_ref, qd_ref, kd_ref, dp_ref, ds_scr):
        @pl.when(pl.program_id(0) == 0)
        def _():
            ds_scr[...] = jnp.zeros_like(ds_scr)

        consts = (dec_ref[...], hm_ref[...], qd_ref[...], kd_ref[...])
        d_s = tuple(ds_scr[h] for h in range(RET_HEADS))
        for c in reversed(range(G)):
            rows = pl.ds(c * CHUNK, CHUNK)
            v_heads = tuple(p_ref[rows, 2 * RET_QK + RET_DV * h:2 * RET_QK + RET_DV * (h + 1)] for h in range(RET_HEADS))
            s_heads = tuple(s_ref[c, h] for h in range(RET_HEADS))
            tables = (cos_ref[rows, :], sin_ref[rows, :]) + consts
            _, vjp = jax.vjp(lambda a, b, c_, d: _ret_chunk(a, b, c_, d, *tables, chunk_dec),
                             p_ref[rows, 0:RET_QK], p_ref[rows, RET_QK:2 * RET_QK], v_heads, s_heads)
            d_out = tuple(dret_ref[rows, RET_DV * h:RET_DV * (h + 1)] for h in range(RET_HEADS))
            dq, dk, dv, d_s = vjp((d_out, d_s))
            dp_ref[rows, 0:RET_QK] = dq
            dp_ref[rows, RET_QK:2 * RET_QK] = dk
            for h in range(RET_HEADS):
                dp_ref[rows, 2 * RET_QK + RET_DV * h:2 * RET_QK + RET_DV * (h + 1)] = dv[h]
        for h in range(RET_HEADS):
            ds_scr[h] = d_s[h]

    tok = CHUNK * G
    rev = lambda i: (ng - 1 - i, 0)
    return pl.pallas_call(
        body, name="ret_bwd", grid=(ng,),
        in_specs=[pl.BlockSpec((tok, RET_COLS), rev),
                  pl.BlockSpec((G, RET_HEADS, RET_QK, RET_DV), lambda i: (ng - 1 - i, 0, 0, 0)),
                  pl.BlockSpec((tok, RET_WIDTH), rev), pl.BlockSpec((tok, RET_QK), rev), pl.BlockSpec((tok, RET_QK), rev)]
        + _ret_specs(),
        out_specs=pl.BlockSpec((tok, 2 * RET_QK + RET_WIDTH), rev),
        out_shape=jax.ShapeDtypeStruct((T, 2 * RET_QK + RET_WIDTH), F32),
        scratch_shapes=[pltpu.VMEM((RET_HEADS, RET_QK, RET_DV), F32)],
        compiler_params=_params(dimension_semantics=("arbitrary",)),
    )(p_ret, s_saved, d_ret, cos_t, sin_t, dec, hm, qd, kd)


def _wkv_consts():
    lane = lax.broadcasted_iota(jnp.int32, (RWKV_HEAD, RWKV_WIDTH), 1)
    sub = lax.broadcasted_iota(jnp.int32, (RWKV_HEAD, RWKV_WIDTH), 0)
    diag = ((lane & (RWKV_HEAD - 1)) == sub).astype(F32)
    r = lax.broadcasted_iota(jnp.int32, (3 * 128, 128), 0)
    c = lax.broadcasted_iota(jnp.int32, (3 * 128, 128), 1)
    ones = (((r & 127) >> 6) == (c >> 6)).astype(BF16)
    return diag, ones


def _stack(x):
    return jnp.concatenate([x[:, 128 * p:128 * (p + 1)] for p in range(4)], axis=0)


def _unstack(y):
    n = y.shape[0] // 4
    return jnp.concatenate([y[n * p:n * (p + 1)] for p in range(4)], axis=1)


def _split(x, n):
    pieces = []
    for _ in range(n):
        p = x.astype(BF16)
        pieces.append(p)
        x = x - p.astype(F32)
    return pieces


def _lane_sum(x, ones):
    return _unstack(jnp.dot(_two_piece(_stack(x)), ones[:256].astype(F32), preferred_element_type=F32))


def _lane_sum_bf(x, ones):
    return _unstack(jnp.dot(_stack(x).astype(BF16), ones[:128], preferred_element_type=F32))


def _colsum(x):
    return jnp.sum(x, axis=0, keepdims=True)


def _rows_times(vec, mat):
    lane = lax.broadcasted_iota(jnp.int32, (1, 128), 1)
    tiles = []
    for p in range(4):
        lhs = jnp.concatenate([vec[:, 128 * p:128 * p + RWKV_HEAD], vec[:, 128 * p + RWKV_HEAD:128 * (p + 1)]], axis=0)
        out = jnp.dot(lhs, mat[:, 128 * p:128 * (p + 1)], preferred_element_type=F32)
        tiles.append(jnp.where(lane < RWKV_HEAD, out[0:1], out[1:2]))
    return jnp.concatenate(tiles, axis=1)


def _expand_cols(xt, t):
    lane = lax.broadcasted_iota(jnp.int32, (RWKV_HEAD, 128), 1)
    tiles = []
    for p in range(4):
        lo = jnp.broadcast_to(xt[128 * p:128 * p + RWKV_HEAD, t:t + 1], (RWKV_HEAD, 128))
        hi = jnp.broadcast_to(xt[128 * p + RWKV_HEAD:128 * (p + 1), t:t + 1], (RWKV_HEAD, 128))
        tiles.append(jnp.where(lane < RWKV_HEAD, lo, hi))
    return jnp.concatenate(tiles, axis=1)


def _head_sums(x, ones):
    return _unstack(jnp.dot(jnp.concatenate(_split(_stack(x), 3), axis=1), ones, preferred_element_type=F32))


def _wkv_fwd(r, w, k, v, kap, a):
    T = r.shape[0]
    C = WKV_CHUNK
    nc = T // C

    def body(r_ref, w_ref, k_ref, v_ref, kap_ref, a_ref, o_ref, s_all_ref, sa_all_ref, s_scr):
        @pl.when(pl.program_id(0) == 0)
        def _():
            s_scr[...] = jnp.zeros_like(s_scr)

        diag, ones = _wkv_consts()
        rr, ww, kk, vv, kap_, aa = (ref[...] for ref in (r_ref, w_ref, k_ref, v_ref, kap_ref, a_ref))
        bb = kap_ * aa
        c1 = _head_sums(pltpu.roll(bb, 1, axis=0) * kap_, ones)
        row = lambda x, t: x[t:t + 1]

        v_cols = vv.T

        s_prev = s_scr[...]
        sa = _lane_sum(s_prev * (-row(kap_, 0)), ones)
        ls, rows = None, []

        def emit_o(t, s_t):
            rows.append(_colsum(_lane_sum_bf(s_t * row(rr, t), ones) * diag))
            if t % 8 == 7:
                o_ref[t - 7:t + 1, :] = jnp.concatenate(rows, axis=0)
                rows.clear()

        for t in range(C):
            u = s_prev * row(ww, t) + _expand_cols(v_cols, t) * row(kk, t)
            if t > 0:
                sa = ls - sa * row(c1, t)
            if t + 1 < C:
                ls = _lane_sum(u * (-row(kap_, t + 1)), ones)
            if t > 0:
                emit_o(t - 1, s_prev)
            s_prev = u + sa * row(bb, t)
            s_all_ref[t] = s_prev
            sa_all_ref[t] = sa.astype(BF16)
        emit_o(C - 1, s_prev)
        s_scr[...] = s_prev

    spec = _rows(C, RWKV_WIDTH)
    return pl.pallas_call(
        body, name="wkv_fwd", grid=(nc,),
        in_specs=[spec] * 6,
        out_specs=[spec, pl.BlockSpec((C, RWKV_HEAD, RWKV_WIDTH), lambda i: (i, 0, 0)),
                   pl.BlockSpec((C, RWKV_HEAD, RWKV_WIDTH), lambda i: (i, 0, 0))],
        out_shape=[jax.ShapeDtypeStruct((T, RWKV_WIDTH), F32), jax.ShapeDtypeStruct((T, RWKV_HEAD, RWKV_WIDTH), F32),
                   jax.ShapeDtypeStruct((T, RWKV_HEAD, RWKV_WIDTH), BF16)],
        scratch_shapes=[pltpu.VMEM((RWKV_HEAD, RWKV_WIDTH), F32)],
        compiler_params=_params(dimension_semantics=("arbitrary",)),
    )(r, w, k, v, kap, a)


def _wkv_bwd(r, w, k, v, kap, a, s_all, sa_all, d_o):
    T = r.shape[0]
    C = WKV_CHUNK
    nc = T // C

    def body(r_ref, w_ref, k_ref, v_ref, kap_ref, a_ref, s_ref, s_before_ref, sa_ref, do_ref,
             dr_ref, dw_ref, dk_ref, dv_ref, dkap_ref, da_ref, ds_scr):
        first_chunk = pl.program_id(0) == nc - 1

        @pl.when(pl.program_id(0) == 0)
        def _():
            ds_scr[...] = jnp.zeros_like(ds_scr)

        diag, ones = _wkv_consts()
        rr, ww, kk, vv, kap_, aa, dd = (ref[...] for ref in (r_ref, w_ref, k_ref, v_ref, kap_ref, a_ref, do_ref))
        bb = kap_ * aa
        e1 = _head_sums(pltpu.roll(kap_, C - 1, axis=0) * bb, ones)
        row = lambda x, t: x[t:t + 1]

        def state_before(t):
            return s_ref[t - 1] if t > 0 else jnp.where(first_chunk, 0.0, s_before_ref[0])

        do_cols = dd.T

        d_sn, dsa, rows = None, None, [None] * C

        def emit_rows(t, d_sn_t, dsa_t):
            s_prev, dof = state_before(t), _expand_cols(do_cols, t)
            dv = _colsum(_lane_sum_bf(d_sn_t * row(kk, t), ones) * diag)
            db = _colsum(d_sn_t * sa_ref[t].astype(F32))
            rows[t] = (_colsum(s_ref[t] * dof), _colsum(d_sn_t * s_prev), _rows_times(row(vv, t), d_sn_t), dv,
                       db * row(aa, t) - _colsum(dsa_t * s_prev), db * row(kap_, t))
            if t % 8 == 0:
                for j, ref in enumerate((dr_ref, dw_ref, dk_ref, dv_ref, dkap_ref, da_ref)):
                    ref[t:t + 8, :] = jnp.concatenate([rows[u][j] for u in range(t, t + 8)], axis=0)

        for t in reversed(range(C)):
            dof = _expand_cols(do_cols, t)
            if t == C - 1:
                d_sn = ds_scr[...] + dof * row(rr, t)
                dsa = _lane_sum(d_sn * row(bb, t), ones)
            else:
                v_t = d_sn * row(ww, t + 1) + dof * row(rr, t)
                ls = _lane_sum(v_t * row(bb, t), ones)
                emit_rows(t + 1, d_sn, dsa)
                d_sn = v_t - dsa * row(kap_, t + 1)
                dsa = ls - dsa * row(e1, t)
        emit_rows(0, d_sn, dsa)
        d_s = d_sn * row(ww, 0) - dsa * row(kap_, 0)
        ds_scr[...] = d_s

    spec = pl.BlockSpec((C, RWKV_WIDTH), lambda i: (nc - 1 - i, 0))
    states = pl.BlockSpec((C, RWKV_HEAD, RWKV_WIDTH), lambda i: (nc - 1 - i, 0, 0))
    before = pl.BlockSpec((1, RWKV_HEAD, RWKV_WIDTH), lambda i: (jnp.maximum((nc - 1 - i) * C - 1, 0), 0, 0))
    return pl.pallas_call(
        body, name="wkv_bwd", grid=(nc,),
        in_specs=[spec] * 6 + [states, before, states, spec],
        out_specs=[spec] * 6,
        out_shape=[jax.ShapeDtypeStruct((T, RWKV_WIDTH), F32)] * 6,
        scratch_shapes=[pltpu.VMEM((RWKV_HEAD, RWKV_WIDTH), F32)],
        compiler_params=_params(dimension_semantics=("arbitrary",)),
    )(r, w, k, v, kap, a, s_all, s_all, sa_all, d_o)


W = RWKV_WIDTH


def _softplus(y):
    return jnp.maximum(y, 0.0) + jnp.log(1.0 + jnp.exp(-jnp.abs(y)))


def _prep_fn(kr, xwa, w0, a0, k_k, k_a, wup_pad, aup_pad, ones64):
    w_log = -_softplus(-(w0 + _mm_nt(jnp.tanh(xwa), wup_pad))) - 0.5
    decay = jnp.exp(-jnp.exp(w_log))
    a = jax.nn.sigmoid(a0 + _mm_nt(xwa, aup_pad))
    kk = kr * k_k
    kap = kk / jnp.maximum(jnp.sqrt(_head_mix(kk * kk, ones64)), 1e-12)
    k = kr * (1.0 + (a - 1.0) * k_a)
    return decay, k, kap, a


def _shift_down(p, first_row):
    rows = lax.broadcasted_iota(jnp.int32, p.shape, 0)
    return jnp.where(rows == 0, first_row, pltpu.roll(p, 1, axis=0))


def _shift_up(z, last_row):
    n = z.shape[0]
    rows = lax.broadcasted_iota(jnp.int32, z.shape, 0)
    return jnp.where(rows == n - 1, last_row, pltpu.roll(z, n - 1, axis=0))


def _prev_block_spec():
    return pl.BlockSpec((8, RWKV_COLS), lambda i: (jnp.maximum(i * (TOK_TILE // 8) - 1, 0), 0))


def _mixed(p_ref, prev8_ref, mu_ref, first_tile):
    p = p_ref[...]
    first_row = jnp.where(first_tile, 0.0, prev8_ref[7:8, :])
    prev = _shift_down(p, first_row)
    return p, prev, p + mu_ref[...] * (prev - p)


def _prep_fwd(p_rwkv, mu, w0, a0, k_k, k_a, wup_pad, aup_pad, ones64):
    T = p_rwkv.shape[0]

    def body(p_ref, prev8_ref, mu_ref, w0_ref, a0_ref, kk_ref, ka_ref, wup_ref, aup_ref, ones_ref,
             r_ref, w_ref, k_ref, v_ref, kap_ref, a_ref, g_ref):
        _, _, ps = _mixed(p_ref, prev8_ref, mu_ref, pl.program_id(0) == 0)
        decay, k, kap, a = _prep_fn(ps[:, W:2 * W], ps[:, 4 * W:], w0_ref[...], a0_ref[...], kk_ref[...], ka_ref[...],
                                    wup_ref[...], aup_ref[...], ones_ref[...])
        r_ref[...] = ps[:, 0:W]
        w_ref[...] = decay
        k_ref[...] = k
        v_ref[...] = ps[:, 2 * W:3 * W]
        kap_ref[...] = kap
        a_ref[...] = a
        g_ref[...] = ps[:, 3 * W:4 * W]

    vec = _full((1, W))
    return pl.pallas_call(
        body, name="prep_fwd", grid=(T // TOK_TILE,),
        in_specs=[_rows(TOK_TILE, RWKV_COLS), _prev_block_spec(), _full((1, RWKV_COLS)), vec, vec, vec, vec,
                  _full((W, 2 * LORA)), _full((W, 2 * LORA)), _full((256, 128))],
        out_specs=[_rows(TOK_TILE, W)] * 7,
        out_shape=[jax.ShapeDtypeStruct((T, W), F32)] * 7,
        compiler_params=_params(dimension_semantics=("arbitrary",)),
    )(p_rwkv, p_rwkv, mu, w0, a0, k_k, k_a, wup_pad, aup_pad, ones64)


def _prep_bwd(p_rwkv, mu, w0, a0, k_k, k_a, wup_pad, aup_pad, ones64, dr, dw, dk, dv, dkap, da, dg, dr2, dk2, dv2):
    T = p_rwkv.shape[0]
    nt = T // TOK_TILE

    def body(p_ref, prev8_ref, mu_ref, w0_ref, a0_ref, kk_ref, ka_ref, wup_ref, aup_ref, ones_ref,
             dr_ref, dw_ref, dk_ref, dv_ref, dkap_ref, da_ref, dg_ref, dr2_ref, dk2_ref, dv2_ref,
             dp_ref, dmu_ref, dw0_ref, da0_ref, dkk_ref, dka_ref, dwup_ref, daup_ref, zrow_scr):
        i = pl.program_id(0)
        accs = (dmu_ref, dw0_ref, da0_ref, dkk_ref, dka_ref, dwup_ref, daup_ref)

        @pl.when(i == 0)
        def _():
            zrow_scr[...] = jnp.zeros_like(zrow_scr)
            for ref in accs:
                ref[...] = jnp.zeros_like(ref)

        p, prev, ps = _mixed(p_ref, prev8_ref, mu_ref, i == nt - 1)
        ones = ones_ref[...]
        _, vjp = jax.vjp(lambda *args: _prep_fn(*args, ones), ps[:, W:2 * W], ps[:, 4 * W:], w0_ref[...], a0_ref[...],
                         kk_ref[...], ka_ref[...], wup_ref[...], aup_ref[...])
        dkr, dxwa, dw0, da0, dkk, dka, dwup, daup = vjp(
            (dw_ref[...], dk_ref[...] + dk2_ref[...], dkap_ref[...], da_ref[...]))
        dps = jnp.concatenate([dr_ref[...] + dr2_ref[...], dkr, dv_ref[...] + dv2_ref[...], dg_ref[...], dxwa], axis=1)
        z = dps * mu_ref[...]
        dp_ref[...] = dps - z + _shift_up(z, zrow_scr[0:1, :])
        zrow_scr[0:1, :] = z[0:1, :]
        for ref, val in zip(accs, (_colsum(dps * (prev - p)), dw0, da0, dkk, dka, dwup, daup)):
            ref[...] += val

    rev = lambda i: (nt - 1 - i, 0)
    vec = _full((1, W))
    lora = _full((W, 2 * LORA))
    tile = pl.BlockSpec((TOK_TILE, W), rev)
    prev8 = pl.BlockSpec((8, RWKV_COLS), lambda i: (jnp.maximum((nt - 1 - i) * (TOK_TILE // 8) - 1, 0), 0))
    return pl.pallas_call(
        body, name="prep_bwd", grid=(nt,),
        in_specs=[pl.BlockSpec((TOK_TILE, RWKV_COLS), rev), prev8, _full((1, RWKV_COLS)), vec, vec, vec, vec, lora, lora,
                  _full((256, 128))] + [tile] * 10,
        out_specs=[pl.BlockSpec((TOK_TILE, RWKV_COLS), rev), _full((1, RWKV_COLS)), vec, vec, vec, vec, lora, lora],
        out_shape=[jax.ShapeDtypeStruct((T, RWKV_COLS), F32), jax.ShapeDtypeStruct((1, RWKV_COLS), F32)]
        + [jax.ShapeDtypeStruct((1, W), F32)] * 4 + [jax.ShapeDtypeStruct((W, 2 * LORA), F32)] * 2,
        scratch_shapes=[pltpu.VMEM((8, RWKV_COLS), F32)],
        compiler_params=_params(dimension_semantics=("arbitrary",)),
    )(p_rwkv, p_rwkv, mu, w0, a0, k_k, k_a, wup_pad, aup_pad, ones64, dr, dw, dk, dv, dkap, da, dg, dr2, dk2, dv2)


def _silu(x):
    return x * jax.nn.sigmoid(x)


def _post_y(o, r, k, v, g_rw, ret_raw, g_ret, ret_gn_g, gn_g, gn_b, r_k, avg128, avg64, ones64):
    xc = ret_raw - _head_mix(ret_raw, avg128)
    ret = xc * lax.rsqrt(_head_mix(xc * xc, avg128) + RET_GN_EPS)
    y_ret = _silu(g_ret) * (ret * ret_gn_g)
    oc = o - _head_mix(o, avg64)
    on = oc * lax.rsqrt(_head_mix(oc * oc, avg64) + RWKV_GN_EPS) * gn_g + gn_b
    bonus = _head_mix(r * k * r_k, ones64) * v
    y_rwkv = _silu(g_rw) * (on + bonus)
    return y_ret, y_rwkv


def _post_loss(h, final_g, target):
    err = _rmsnorm(h, final_g) - target
    return 0.5 * jnp.sum(jnp.mean(err * err, axis=-1))


def _post(o, r, k, v, g_rw, ret_raw, p_ret, x, target, ret_gn_g, gn_g, gn_b, r_k, final_g, w_out, avg128, avg64, ones64):
    T = x.shape[1]
    n_tok_out = 8

    def body(o_ref, r_ref, k_ref, v_ref, grw_ref, ret_ref, gret_ref, x_ref, tgt_ref, rg_ref, gg_ref, gb_ref, rk_ref, fg_ref,
             wo_ref, a128_ref, a64_ref, ones_ref, *outs):
        tok_outs, (dwo_ref, drg_ref, dgg_ref, dgb_ref, drk_ref, dfg_ref, loss_ref) = outs[:n_tok_out], outs[n_tok_out:]
        accs = (dwo_ref, drg_ref, dgg_ref, dgb_ref, drk_ref, dfg_ref, loss_ref)

        @pl.when(pl.program_id(0) == 0)
        def _():
            for ref in accs:
                ref[...] = jnp.zeros_like(ref)

        consts = (a128_ref[...], a64_ref[...], ones_ref[...])
        (y_ret, y_rwkv), vjp = jax.vjp(
            lambda *args: _post_y(*args, *consts), o_ref[...], r_ref[...], k_ref[...], v_ref[...], grw_ref[...], ret_ref[...],
            gret_ref[...], rg_ref[...], gg_ref[...], gb_ref[...], rk_ref[...])
        h = x_ref[...] + _dot_bf(y_ret, wo_ref[0:RET_WIDTH, :]) + _dot_bf(y_rwkv, wo_ref[RET_WIDTH:, :])
        loss, (dh, dfg) = jax.value_and_grad(_post_loss, argnums=(0, 1))(h, fg_ref[...], tgt_ref[...])
        dy_ret = _dot_nt_bf(dh, wo_ref[0:RET_WIDTH, :])
        dy_rwkv = _dot_nt_bf(dh, wo_ref[RET_WIDTH:, :])
        do, dr, dk, dv, dgrw, dret, dgret, drg, dgg, dgb, drk = vjp((dy_ret, dy_rwkv))
        for ref, val in zip(tok_outs, (dh, do, dr, dk, dv, dgrw, dret, dgret)):
            ref[...] = val
        dwo_ref[0:RET_WIDTH, :] += _dot_tn_bf(y_ret, dh)
        dwo_ref[RET_WIDTH:, :] += _dot_tn_bf(y_rwkv, dh)
        for ref, val in zip(accs[1:], (drg, dgg, dgb, drk, dfg, jnp.full((1, 128), loss, F32))):
            ref[...] += val

    tile = _rows(TOK_TILE, W)
    wide = _rows(TOK_TILE, D_MODEL)
    wide_of_one = _rows_of_one(TOK_TILE, D_MODEL)
    vec = _full((1, W))
    sq = _full((256, 128))
    return pl.pallas_call(
        body, name="post", grid=(T // TOK_TILE,),
        in_specs=[tile] * 6 + [pl.BlockSpec((TOK_TILE, W), lambda i: (i, 2)), wide_of_one, wide_of_one, vec, vec, vec, vec,
                               _full((1, D_MODEL)), _full((D_MODEL, D_MODEL)), sq, sq, sq],
        out_specs=[wide] + [tile] * 7 + [_full((D_MODEL, D_MODEL)), vec, vec, vec, vec, _full((1, D_MODEL)), _full((1, 128))],
        out_shape=[jax.ShapeDtypeStruct((T, D_MODEL), F32)] + [jax.ShapeDtypeStruct((T, W), F32)] * 7
        + [jax.ShapeDtypeStruct((D_MODEL, D_MODEL), F32)] + [jax.ShapeDtypeStruct((1, W), F32)] * 4
        + [jax.ShapeDtypeStruct((1, D_MODEL), F32), jax.ShapeDtypeStruct((1, 128), F32)],
        compiler_params=_params(dimension_semantics=("arbitrary",)),
    )(o, r, k, v, g_rw, ret_raw, p_ret, x, target, ret_gn_g, gn_g, gn_b, r_k, final_g, w_out, avg128, avg64, ones64)


def _inproj_bwd_x(x, norm_g, dp_qkv, dg_ret, dp_rwkv, dh, w_in_t):
    T = x.shape[1]
    n_qkv = 2 * RET_QK + RET_WIDTH

    def body(x_ref, g_ref, dqkv_ref, dgret_ref, drwkv_ref, dh_ref, w_ref, dx_ref, dg_ref):
        @pl.when(pl.program_id(0) == 0)
        def _():
            dg_ref[...] = jnp.zeros_like(dg_ref)

        _, vjp = jax.vjp(_rmsnorm, x_ref[...], g_ref[...])
        du = (_dot_bf(dqkv_ref[...], w_ref[0:n_qkv, :]) + _dot_bf(dgret_ref[...], w_ref[n_qkv:RET_COLS, :])
              + _dot_bf(drwkv_ref[...], w_ref[RET_COLS:, :]))
        dx, dg = vjp(du)
        dx_ref[...] = dx + dh_ref[...]
        dg_ref[...] += dg

    return pl.pallas_call(
        body, name="inproj_bwd_x", grid=(T // TOK_TILE,),
        in_specs=[_rows_of_one(TOK_TILE, D_MODEL), _full((1, D_MODEL)), _rows(TOK_TILE, n_qkv), _rows(TOK_TILE, RET_WIDTH),
                  _rows(TOK_TILE, RWKV_COLS), _rows(TOK_TILE, D_MODEL), _full((IN_COLS, D_MODEL))],
        out_specs=[_rows_of_one(TOK_TILE, D_MODEL), _full((1, D_MODEL))],
        out_shape=[jax.ShapeDtypeStruct((1, T, D_MODEL), F32), jax.ShapeDtypeStruct((1, D_MODEL), F32)],
        compiler_params=_params(dimension_semantics=("arbitrary",)),
    )(x, norm_g, dp_qkv, dg_ret, dp_rwkv, dh, w_in_t)


def _grad_w(name, u, dp):
    T, n = dp.shape
    tile = 2 * TOK_TILE
    steps = T // tile

    def body(u_ref, dp_ref, out_ref, acc_ref):
        @pl.when(pl.program_id(0) == 0)
        def _():
            acc_ref[...] = jnp.zeros_like(acc_ref)

        acc_ref[...] += _dot_tn_bf(dp_ref[...], u_ref[...])

        @pl.when(pl.program_id(0) == steps - 1)
        def _():
            out_ref[...] = acc_ref[...].astype(BF16)

    return pl.pallas_call(
        body, name=name, grid=(steps,),
        in_specs=[_rows(tile, D_MODEL), _rows(tile, n)],
        out_specs=_full((n, D_MODEL)),
        out_shape=jax.ShapeDtypeStruct((n, D_MODEL), BF16),
        scratch_shapes=[pltpu.VMEM((n, D_MODEL), F32)],
        compiler_params=_params(dimension_semantics=("arbitrary",)),
    )(u, dp)


def _pad_lora(w_up_t, first):
    z = jnp.zeros_like(w_up_t)
    return jnp.concatenate([w_up_t, z] if first else [z, w_up_t], axis=1)


def _local_grads(x, target, norm_g, w_in_t, ret_gn_g, mu, w_lora_up_t, w0, a_lora_up_t, a0, k_k, k_a, r_k, gn_g, gn_b,
                 w_out_bf, final_g):
    T = x.shape[1]
    tabs = _rope_tables(T) + _ret_tables()
    ones64 = _block_mix(128, RWKV_HEAD)
    avg64 = _block_mix(128, RWKV_HEAD, 1.0 / RWKV_HEAD)
    avg128 = _block_mix(128, RET_DV, 1.0 / RET_DV)
    wup_pad, aup_pad = _pad_lora(w_lora_up_t, True), _pad_lora(a_lora_up_t, False)

    p_ret, p_rwkv, u = _inproj(x, norm_g, w_in_t)
    ret_raw, s_saved = _ret_fwd(p_ret, tabs)
    r, w, k, v, kap, a, g_rw = _prep_fwd(p_rwkv, mu, w0, a0, k_k, k_a, wup_pad, aup_pad, ones64)
    o, s_all, sa_all = _wkv_fwd(r, w, k, v, kap, a)
    (dh, do, dr2, dk2, dv2, dgrw, dret, dgret, d_w_out, d_ret_gn_g, d_gn_g, d_gn_b, d_r_k, d_final_g, loss) = _post(
        o, r, k, v, g_rw, ret_raw, p_ret, x, target, ret_gn_g, gn_g, gn_b, r_k, final_g, w_out_bf, avg128, avg64, ones64)
    dr, dw, dk, dv, dkap, da = _wkv_bwd(r, w, k, v, kap, a, s_all, sa_all, do)
    dp_rwkv, d_mu, d_w0, d_a0, d_k_k, d_k_a, d_wup, d_aup = _prep_bwd(
        p_rwkv, mu, w0, a0, k_k, k_a, wup_pad, aup_pad, ones64, dr, dw, dk, dv, dkap, da, dgrw, dr2, dk2, dv2)
    dp_qkv = _ret_bwd(p_ret, s_saved, dret, tabs)
    dx, d_norm_g = _inproj_bwd_x(x, norm_g, dp_qkv, dgret, dp_rwkv, dh, w_in_t)
    d_w_in = jnp.concatenate([_grad_w("grad_w_qkv", u, dp_qkv), _grad_w("grad_w_gret", u, dgret),
                              _grad_w("grad_w_rwkv", u, dp_rwkv)], axis=0)
    grads = dict(norm_g=d_norm_g, w_in=d_w_in, ret_gn_g=d_ret_gn_g, rwkv_mu=d_mu, w_lora_up=d_wup[:, :LORA], w0=d_w0,
                 a_lora_up=d_aup[:, LORA:], a0=d_a0, k_k=d_k_k, k_a=d_k_a, r_k=d_r_k, rwkv_gn_g=d_gn_g, rwkv_gn_b=d_gn_b,
                 w_out=d_w_out, final_norm_g=d_final_g)
    return loss, dx, grads


def _mesh_pos():
    return lax.axis_index("x"), lax.axis_index("y"), lax.axis_index("c")


def _all_gather(shards):
    n = len(shards)

    def body(*refs):
        x_refs, out_refs = refs[:n], refs[n:2 * n]
        send_sems, recv_sems, local_sems = refs[2 * n:]
        x, y, c = _mesh_pos()
        me, sibling = (x, y, c), (x, y, 1 - c)
        chips = [(1 - x, y), (x, 1 - y), (1 - x, 1 - y)]

        def rows(a, pos):
            m = x_refs[a].shape[0]
            return out_refs[a].at[pl.ds((4 * pos[0] + 2 * pos[1] + pos[2]) * m, m), :]

        def copy(a, k, block, to, src=None):
            return pltpu.make_async_remote_copy(
                src_ref=rows(a, block) if src is None else src, dst_ref=rows(a, block),
                send_sem=send_sems.at[a, k], recv_sem=recv_sems.at[a, k], device_id=to, device_id_type=MESH)

        mine = [pltpu.make_async_copy(x_refs[a], rows(a, me), local_sems.at[a]) for a in range(n)]
        for cp in mine:
            cp.start()
        first = []
        for a in range(n):
            first.append(copy(a, 0, me, sibling, src=x_refs[a]))
            first += [copy(a, 1 + j, me, (*chip, c), src=x_refs[a]) for j, chip in enumerate(chips)]
        for cp in first:
            cp.start()
        passed = []
        for j, chip in enumerate(chips):
            for a in range(n):
                copy(a, 1 + j, (*chip, c), me).wait_recv()
                passed.append(copy(a, 4 + j, (*chip, c), sibling))
                passed[-1].start()
        for a in range(n):
            copy(a, 0, sibling, me).wait_recv()
            for j, chip in enumerate(chips):
                copy(a, 4 + j, (*chip, 1 - c), me).wait_recv()
        for cp in first + passed:
            cp.wait_send()
        for cp in mine:
            cp.wait()

    vmem = pl.BlockSpec(memory_space=pltpu.VMEM)
    return pl.pallas_call(
        body, name="gather_weights",
        out_shape=[jax.ShapeDtypeStruct((N_DEV * s.shape[0], s.shape[1]), s.dtype) for s in shards],
        in_specs=[vmem] * n, out_specs=[vmem] * n,
        scratch_shapes=[pltpu.SemaphoreType.DMA((n, 7)), pltpu.SemaphoreType.DMA((n, 7)), pltpu.SemaphoreType.DMA((n,))],
        compiler_params=_params(),
    )(*shards)


N_CHIP = 4


def _exchange_pairs(big, small):
    nb, ns = len(big), len(small)

    def body(*refs):
        big_in, small_in = refs[:nb], refs[nb:nb + ns]
        theirs, small_out = refs[nb + ns:2 * nb + ns], refs[2 * nb + ns:2 * nb + 2 * ns]
        pair_send, pair_recv, send_sems, recv_sems, local_sems = refs[2 * nb + 2 * ns:]
        x, y, c = _mesh_pos()
        me = 4 * x + 2 * y + c
        local = [pltpu.make_async_copy(small_in[a].at[me], small_out[a].at[me], local_sems.at[a]) for a in range(ns)]
        for cp in local:
            cp.start()
        copies = [pltpu.make_async_remote_copy(
            src_ref=big_in[a], dst_ref=theirs[a], send_sem=pair_send.at[a], recv_sem=pair_recv.at[a],
            device_id=(x, y, 1 - c), device_id_type=MESH) for a in range(nb)]
        for k in range(1, N_DEV):
            peer = (x ^ (k >> 2), y ^ ((k >> 1) & 1), c ^ (k & 1))
            peer_idx = 4 * peer[0] + 2 * peer[1] + peer[2]
            copies += [pltpu.make_async_remote_copy(
                src_ref=small_in[a].at[peer_idx], dst_ref=small_out[a].at[me], send_sem=send_sems.at[a, k - 1],
                recv_sem=recv_sems.at[a, k - 1], device_id=peer, device_id_type=MESH) for a in range(ns)]
        for cp in copies:
            cp.start()
        for cp in copies:
            cp.wait()
        for cp in local:
            cp.wait()

    hbm = pl.BlockSpec(memory_space=pl.ANY)
    out_shape = [jax.ShapeDtypeStruct(p.shape, p.dtype) for p in big + small]
    dma = pltpu.SemaphoreType.DMA
    res = pl.pallas_call(
        body, name="exchange_pairs", out_shape=out_shape,
        in_specs=[hbm] * (nb + ns), out_specs=[hbm] * len(out_shape),
        scratch_shapes=[dma((nb,)), dma((nb,)), dma((ns, 7)), dma((ns, 7)), dma((ns,))],
        compiler_params=_params(),
    )(*big, *small)
    return res[:nb], res[nb:]


def _pair_sum(name, mine, theirs, row_tile):
    _, rows, cols = mine.shape

    def body(a_ref, b_ref, o_ref):
        o_ref[...] = (a_ref[...].astype(F32) + b_ref[...].astype(F32)).astype(o_ref.dtype)

    spec = pl.BlockSpec((N_CHIP, row_tile, cols), lambda i: (0, i, 0))
    return pl.pallas_call(
        body, name=name, grid=(rows // row_tile,), in_specs=[spec, spec], out_specs=spec,
        out_shape=jax.ShapeDtypeStruct(mine.shape, mine.dtype),
        compiler_params=_params(dimension_semantics=("arbitrary",)),
    )(mine, theirs)


def _exchange_chips(parts):
    n = len(parts)

    def body(*refs):
        in_refs, out_refs = refs[:n], refs[n:2 * n]
        send_sems, recv_sems, local_sems = refs[2 * n:]
        x, y, c = _mesh_pos()
        my_chip = 2 * x + y
        own = [pltpu.make_async_copy(in_refs[a].at[my_chip], out_refs[a].at[my_chip], local_sems.at[a]) for a in range(n)]
        for cp in own:
            cp.start()
        copies = []
        for k in range(1, N_CHIP):
            px, py = x ^ (k >> 1), y ^ (k & 1)
            copies += [pltpu.make_async_remote_copy(
                src_ref=in_refs[a].at[2 * px + py], dst_ref=out_refs[a].at[my_chip], send_sem=send_sems.at[a, k - 1],
                recv_sem=recv_sems.at[a, k - 1], device_id=(px, py, c), device_id_type=MESH) for a in range(n)]
        for cp in copies:
            cp.start()
        for cp in copies:
            cp.wait()
        for cp in own:
            cp.wait()

    hbm = pl.BlockSpec(memory_space=pl.ANY)
    dma = pltpu.SemaphoreType.DMA
    return pl.pallas_call(
        body, name="exchange_chips",
        out_shape=[jax.ShapeDtypeStruct(p.shape, p.dtype) for p in parts],
        in_specs=[hbm] * n, out_specs=[hbm] * n,
        scratch_shapes=[dma((n, N_CHIP - 1)), dma((n, N_CHIP - 1)), dma((n,))],
        compiler_params=_params(),
    )(*parts)


def _adamw(w, g, m, v):
    m = ADAM_B1 * m + (1.0 - ADAM_B1) * g
    v = ADAM_B2 * v + (1.0 - ADAM_B2) * (g * g)
    m_hat = m / (1.0 - ADAM_B1 ** ADAM_STEP)
    v_hat = v / (1.0 - ADAM_B2 ** ADAM_STEP)
    return -ADAM_LR * (m_hat / (jnp.sqrt(v_hat) + ADAM_EPS) + ADAM_WD * w), m, v


def _sum_parts(name, parts, row_tile):
    n_parts, rows, cols = parts.shape

    def body(p_ref, g_ref):
        g = p_ref[0].astype(F32)
        for s in range(1, n_parts):
            g = g + p_ref[s].astype(F32)
        g_ref[...] = g

    return pl.pallas_call(
        body, name=name, grid=(rows // row_tile,),
        in_specs=[pl.BlockSpec((n_parts, row_tile, cols), lambda i: (0, i, 0))],
        out_specs=pl.BlockSpec((row_tile, cols), lambda i: (i, 0)),
        out_shape=jax.ShapeDtypeStruct((rows, cols), F32),
        compiler_params=_params(dimension_semantics=("arbitrary",)),
    )(parts)


def _adamw_apply(name, g, w, m, v, row_tile):
    _, rows, cols = w.shape

    def body(g_ref, w_ref, m_ref, v_ref, d_ref, nm_ref, nv_ref):
        d_ref[0], nm_ref[0], nv_ref[0] = _adamw(w_ref[0], g_ref[0], m_ref[0], v_ref[0])

    tile = pl.BlockSpec((1, row_tile, cols), lambda i: (0, i, 0))
    return pl.pallas_call(
        body, name=name, grid=(rows // row_tile,), in_specs=[tile] * 4, out_specs=[tile] * 3,
        out_shape=[jax.ShapeDtypeStruct((1, rows, cols), F32)] * 3,
        compiler_params=_params(dimension_semantics=("arbitrary",)),
    )(g, w, m, v)


def _reduce_adamw_t(name, parts_t, w, m, v, sum_tile, row_tile):
    g = _sum_parts(name + "_sum", parts_t, sum_tile).T[None]
    return [g] + list(_adamw_apply(name, g, w, m, v, row_tile))


def _reduce_adamw(name, parts, w, m, v, row_tile):
    n_parts, rows, cols = parts.shape

    def body(p_ref, w_ref, m_ref, v_ref, g_ref, d_ref, nm_ref, nv_ref):
        g = p_ref[0].astype(F32)
        for s in range(1, n_parts):
            g = g + p_ref[s].astype(F32)
        g_ref[0] = g
        d_ref[0], nm_ref[0], nv_ref[0] = _adamw(w_ref[0], g, m_ref[0], v_ref[0])

    tile = pl.BlockSpec((1, row_tile, cols), lambda i: (0, i, 0))
    return pl.pallas_call(
        body, name=name, grid=(rows // row_tile,),
        in_specs=[pl.BlockSpec((n_parts, row_tile, cols), lambda i: (0, i, 0)), tile, tile, tile],
        out_specs=[tile] * 4,
        out_shape=[jax.ShapeDtypeStruct((1, rows, cols), F32)] * 4,
        compiler_params=_params(dimension_semantics=("arbitrary",)),
    )(parts, w, m, v)


_SMALL = (("norm_g", 1024), ("ret_gn_g", 512), ("rwkv_mu", 2176), ("w0", 512), ("a0", 512), ("k_k", 512), ("k_a", 512),
          ("r_k", 512), ("rwkv_gn_g", 512), ("rwkv_gn_b", 512), ("final_norm_g", 1024))
_SMALL_LANES = sum(n for _, n in _SMALL) + 128
_WEIGHTS = ("norm_g", "w_in", "ret_gn_g", "rwkv_mu", "w_lora_up", "w0", "a_lora_up", "a0", "k_k", "k_a", "r_k", "rwkv_gn_g",
            "rwkv_gn_b", "w_out", "final_norm_g")


def _adamw_vectors(parts, wts, mom, var):
    k = len(_SMALL)

    def body(p_ref, *refs):
        w_refs, m_refs, v_refs, outs = refs[:k], refs[k:2 * k], refs[2 * k:3 * k], refs[3 * k:]
        g_all = p_ref[0]
        for s in range(1, N_DEV):
            g_all = g_all + p_ref[s]
        off = 0
        for i, (name, n) in enumerate(_SMALL):
            g = g_all[:, off:off + n]
            off += n
            if name == "r_k":
                g = jnp.concatenate([g[:, RWKV_HEAD * h:RWKV_HEAD * (h + 1)] for h in range(RWKV_HEADS)], axis=0)[None]
            outs[4 * i][...] = g
            outs[4 * i + 1][...], outs[4 * i + 2][...], outs[4 * i + 3][...] = _adamw(
                w_refs[i][...], g, m_refs[i][...], v_refs[i][...])
        outs[4 * k][...] = g_all[:, off:off + 128]

    vmem = pl.BlockSpec(memory_space=pltpu.VMEM)
    shapes = [jax.ShapeDtypeStruct(wts[n].shape, F32) for n, _ in _SMALL for _ in range(4)] + [jax.ShapeDtypeStruct((1, 128), F32)]
    res = pl.pallas_call(
        body, name="adamw_vectors", out_shape=shapes,
        in_specs=[vmem] * (1 + 3 * k), out_specs=[vmem] * len(shapes), compiler_params=_params(),
    )(parts, *[wts[n] for n, _ in _SMALL], *[mom[n] for n, _ in _SMALL], *[var[n] for n, _ in _SMALL])
    return {n: res[4 * i:4 * i + 4] for i, (n, _) in enumerate(_SMALL)}, res[4 * k]


def kernel(x, norm_g, w_in, ret_gn_g, rwkv_mu, w_lora_up, w0, a_lora_up, a0, k_k, k_a, r_k, rwkv_gn_g, rwkv_gn_b, w_out, final_norm_g, loss_target, m_norm_g, m_w_in, m_ret_gn_g, m_rwkv_mu, m_w_lora_up, m_w0, m_a_lora_up, m_a0, m_k_k, m_k_a, m_r_k, m_rwkv_gn_g, m_rwkv_gn_b, m_w_out, m_final_norm_g, v_norm_g, v_w_in, v_ret_gn_g, v_rwkv_mu, v_w_lora_up, v_w0, v_a_lora_up, v_a0, v_k_k, v_k_a, v_r_k, v_rwkv_gn_g, v_rwkv_gn_b, v_w_out, v_final_norm_g):
    wts = dict(norm_g=norm_g, w_in=w_in, ret_gn_g=ret_gn_g, rwkv_mu=rwkv_mu, w_lora_up=w_lora_up, w0=w0, a_lora_up=a_lora_up,
               a0=a0, k_k=k_k, k_a=k_a, r_k=r_k, rwkv_gn_g=rwkv_gn_g, rwkv_gn_b=rwkv_gn_b, w_out=w_out,
               final_norm_g=final_norm_g)
    mom = dict(norm_g=m_norm_g, w_in=m_w_in, ret_gn_g=m_ret_gn_g, rwkv_mu=m_rwkv_mu, w_lora_up=m_w_lora_up, w0=m_w0,
               a_lora_up=m_a_lora_up, a0=m_a0, k_k=m_k_k, k_a=m_k_a, r_k=m_r_k, rwkv_gn_g=m_rwkv_gn_g,
               rwkv_gn_b=m_rwkv_gn_b, w_out=m_w_out, final_norm_g=m_final_norm_g)
    var = dict(norm_g=v_norm_g, w_in=v_w_in, ret_gn_g=v_ret_gn_g, rwkv_mu=v_rwkv_mu, w_lora_up=v_w_lora_up, w0=v_w0,
               a_lora_up=v_a_lora_up, a0=v_a0, k_k=v_k_k, k_a=v_k_a, r_k=v_r_k, rwkv_gn_g=v_rwkv_gn_g,
               rwkv_gn_b=v_rwkv_gn_b, w_out=v_w_out, final_norm_g=v_final_norm_g)
    shapes = {n: wts[n].shape for n in _WEIGHTS}

    w_in_t, w_out_bf, wup_t, aup_t = _all_gather(
        [w_in[0].T.astype(BF16), w_out[0].astype(BF16), w_lora_up[0].T, a_lora_up[0].T])

    loss, dx, g = _local_grads(
        x, loss_target, norm_g, w_in_t, ret_gn_g, rwkv_mu, wup_t, w0, aup_t, a0, k_k, k_a,
        r_k.reshape(1, W), rwkv_gn_g, rwkv_gn_b, w_out_bf, final_norm_g.reshape(1, D_MODEL))

    small = jnp.concatenate([g[n] for n, _ in _SMALL] + [loss], axis=1)
    core = lax.axis_index("c")
    by_core = lambda t: [lax.dynamic_index_in_dim(t, i, axis=1, keepdims=False) for i in (core, 1 - core)]
    in_mine, in_sib = by_core(g["w_in"].reshape(N_CHIP, 2, SHARD_IN, D_MODEL))
    out_mine, out_sib = by_core(g["w_out"].reshape(N_CHIP, 2, SHARD_OUT, D_MODEL).astype(BF16))
    (in_theirs, out_theirs), parts = _exchange_pairs(
        [in_sib, out_sib],
        [g["w_lora_up"].reshape(N_DEV, SHARD_LORA, LORA), g["a_lora_up"].reshape(N_DEV, SHARD_LORA, LORA),
         jnp.broadcast_to(small[None], (N_DEV, 1, _SMALL_LANES))])
    by_chip = _exchange_chips([_pair_sum("pair_sum_w_in", in_mine, in_theirs, SHARD_IN // 2),
                               _pair_sum("pair_sum_w_out", out_mine, out_theirs, SHARD_OUT)])
    res = {}
    res["w_in"] = _reduce_adamw_t("adamw_w_in", by_chip[0], w_in, m_w_in, v_w_in, SHARD_IN // 2, 256)
    res["w_out"] = _reduce_adamw("adamw_w_out", by_chip[1], w_out, m_w_out, v_w_out, SHARD_OUT)
    res["w_lora_up"] = _reduce_adamw_t("adamw_w_lora_up", parts[0], w_lora_up, m_w_lora_up, v_w_lora_up, LORA, LORA)
    res["a_lora_up"] = _reduce_adamw_t("adamw_a_lora_up", parts[1], a_lora_up, m_a_lora_up, v_a_lora_up, LORA, LORA)
    as_row = lambda d: {n: d[n] if d[n].ndim > 1 else d[n].reshape(1, size) for n, size in _SMALL}
    vec, loss_row = _adamw_vectors(parts[2], as_row(wts), as_row(mom), as_row(var))
    res.update(vec)
    res = {n: [t.reshape(shapes[n]) for t in res[n]] for n in _WEIGHTS}
    return (loss_row[0, 0], dx, *[res[n][0] for n in _WEIGHTS], *[res[n][1] for n in _WEIGHTS],
            *[res[n][2] for n in _WEIGHTS], *[res[n][3] for n in _WEIGHTS])
```

```python
import numpy as np
import jax
import jax.numpy as jnp
from jax import lax
from jax.experimental import pallas as pl
from jax.experimental.pallas import tpu as pltpu

F32 = jnp.float32
BF16 = jnp.bfloat16

D_MODEL = 1024
CHUNK = 64
RET_HEADS = 4
RET_DV = 128
RET_DK = 64
RET_QK = 256
RET_WIDTH = 512
RWKV_WIDTH = 512
RWKV_HEAD = 64
RWKV_HEADS = 8
LORA = 64
RET_COLS = 2 * RET_QK + 2 * RET_WIDTH
RWKV_COLS = 4 * RWKV_WIDTH + 2 * LORA
IN_COLS = RET_COLS + RWKV_COLS
ROPE_BASE = 10000.0
RMS_EPS = 1e-6
RET_GN_EPS = 1e-5
RWKV_GN_EPS = 64e-5
ADAM_LR = 0.001
ADAM_B1 = 0.9
ADAM_B2 = 0.999
ADAM_EPS = 1e-08
ADAM_WD = 0.01
ADAM_STEP = 10
N_DEV = 8
SHARD_IN = IN_COLS // N_DEV
SHARD_OUT = D_MODEL // N_DEV
SHARD_LORA = RWKV_WIDTH // N_DEV
VMEM_LIMIT = 56 * 1024 * 1024
TOK_TILE = 256
WKV_CHUNK = 64

MESH = pl.DeviceIdType.MESH


def _dot_bf(a, b):
    return jnp.dot(a.astype(BF16), b.astype(BF16), preferred_element_type=F32)


def _dot_nt_bf(a, b):
    return lax.dot_general(a.astype(BF16), b.astype(BF16), (((1,), (1,)), ((), ())), preferred_element_type=F32)


def _dot_tn_bf(a, b):
    return lax.dot_general(a.astype(BF16), b.astype(BF16), (((0,), (0,)), ((), ())), preferred_element_type=F32)


@jax.custom_vjp
def _mm(a, b):
    return _dot_bf(a, b)


@jax.custom_vjp
def _mm_nt(a, b):
    return _dot_nt_bf(a, b)


@jax.custom_vjp
def _mm_tn(a, b):
    return _dot_tn_bf(a, b)


_mm.defvjp(lambda a, b: (_dot_bf(a, b), (a, b)), lambda res, g: (_dot_nt_bf(g, res[1]), _dot_tn_bf(res[0], g)))
_mm_nt.defvjp(lambda a, b: (_dot_nt_bf(a, b), (a, b)), lambda res, g: (_dot_bf(g, res[1]), _dot_tn_bf(g, res[0])))
_mm_tn.defvjp(lambda a, b: (_dot_tn_bf(a, b), (a, b)), lambda res, g: (_dot_nt_bf(res[1], g), _dot_bf(res[0], g)))


def _trunc(x):
    return lax.bitcast_convert_type(lax.bitcast_convert_type(x, jnp.uint32) & jnp.uint32(0xFFFF0000), F32)


def _two_piece(x):
    hi = _trunc(x)
    return jnp.concatenate([hi, x - hi], axis=1)


def _mix_raw(x, mat2):
    return _unstack(jnp.dot(_two_piece(_stack(x)), mat2, preferred_element_type=F32))


@jax.custom_vjp
def _head_mix(x, mat2):
    return _mix_raw(x, mat2)


_head_mix.defvjp(lambda x, mat2: (_mix_raw(x, mat2), mat2), lambda mat2, g: (_mix_raw(g, mat2), jnp.zeros_like(mat2)))


def _swap_halves(x):
    lane = lax.broadcasted_iota(jnp.int32, x.shape, 1)
    return jnp.where((lane & (RET_DK - 1)) < RET_DK // 2, pltpu.roll(x, RET_QK - RET_DK // 2, axis=1),
                     pltpu.roll(x, RET_DK // 2, axis=1))


@jax.custom_vjp
def _rot(x):
    return _swap_halves(x)


_rot.defvjp(lambda x: (_swap_halves(x), None), lambda _, g: (_swap_halves(g),))


def _params(**kw):
    return pltpu.CompilerParams(vmem_limit_bytes=VMEM_LIMIT, **kw)


def _full(shape):
    nd = len(shape)
    return pl.BlockSpec(shape, lambda i, _nd=nd: (0,) * _nd)


def _rows(tile, width):
    return pl.BlockSpec((tile, width), lambda i: (i, 0))


def _rows_of_one(tile, width):
    return pl.BlockSpec((None, tile, width), lambda i: (0, i, 0))


def _block_mix(n, blk, scale=1.0):
    idx = np.arange(n) // blk
    m = (idx[:, None] == idx[None, :]).astype(np.float32) * scale
    return jnp.asarray(np.concatenate([m, m], axis=0))


def _rope_tables(T):
    half = RET_DK // 2
    expo = -np.arange(half, dtype=np.float32) / np.float32(half)
    freqs = np.exp(expo * np.float32(np.log(ROPE_BASE))).astype(np.float32)
    ang = np.arange(T, dtype=np.float32)[:, None] * freqs[None, :]
    cos, sin = np.cos(ang).astype(np.float32), np.sin(ang).astype(np.float32)
    cos_h = np.concatenate([cos, cos], axis=1)
    sin_h = np.concatenate([-sin, sin], axis=1)
    cos_t = np.tile(cos_h, (1, RET_HEADS))
    sin_t = np.tile(sin_h, (1, RET_HEADS))
    return jnp.asarray(cos_t), jnp.asarray(sin_t)


def _ret_tables():
    h = np.arange(RET_HEADS, dtype=np.float32)
    lg = np.log(1.0 - np.exp2(-5.0 - h)).astype(np.float32)
    idx = np.arange(CHUNK, dtype=np.float32)
    intra = np.exp(lg[:, None, None] * np.abs(idx[:, None] - idx[None, :])).astype(np.float32)
    q_dec = np.exp(lg[:, None] * (idx[None, :] + 1.0)).astype(np.float32)
    k_dec = np.exp(lg[:, None] * (CHUNK - 1.0 - idx[None, :])).astype(np.float32)
    chunk_dec = np.exp(lg * CHUNK).astype(np.float32)
    lane_head = np.arange(RET_QK) // RET_DK
    mask = (lane_head[None, :] == np.arange(RET_HEADS)[:, None]).astype(np.float32)
    m = np.broadcast_to(mask[:, None, :], (RET_HEADS, CHUNK, RET_QK)).copy()
    qd = m * q_dec[:, :, None]
    kd = m * k_dec[:, :, None]
    return jnp.asarray(intra), jnp.asarray(m), jnp.asarray(qd), jnp.asarray(kd), [float(c) for c in chunk_dec]


def _rmsnorm(x, g):
    return x * lax.rsqrt(jnp.mean(x * x, axis=-1, keepdims=True) + RMS_EPS) * g


def _inproj(x, norm_g, w_in_t):
    T = x.shape[1]

    def body(x_ref, g_ref, w_ref, pr_ref, pw_ref, u_ref):
        ub = _rmsnorm(x_ref[...], g_ref[...]).astype(BF16)
        u_ref[...] = ub
        pr_ref[...] = _dot_nt_bf(ub, w_ref[:RET_COLS, :])
        pw_ref[...] = _dot_nt_bf(ub, w_ref[RET_COLS:, :])

    return pl.pallas_call(
        body, name="inproj", grid=(T // TOK_TILE,),
        in_specs=[_rows_of_one(TOK_TILE, D_MODEL), _full((1, D_MODEL)), _full((IN_COLS, D_MODEL))],
        out_specs=[_rows(TOK_TILE, RET_COLS), _rows(TOK_TILE, RWKV_COLS), _rows(TOK_TILE, D_MODEL)],
        out_shape=[jax.ShapeDtypeStruct((T, RET_COLS), F32), jax.ShapeDtypeStruct((T, RWKV_COLS), F32),
                   jax.ShapeDtypeStruct((T, D_MODEL), BF16)],
        compiler_params=_params(dimension_semantics=("arbitrary",)),
    )(x, norm_g, w_in_t)


def _ret_chunk(pq, pk, v_heads, s_heads, cos_t, sin_t, dec, hm, qd, kd, chunk_dec):
    q = pq * cos_t + _rot(pq) * sin_t
    k = (pk * cos_t + _rot(pk) * sin_t) * (RET_DK ** -0.5)
    outs, s_out = [], []
    for h in range(RET_HEADS):
        sc = _mm_nt(q * hm[h], k * hm[h]) * dec[h]
        intra = _mm(sc, v_heads[h])
        kv = _mm_tn(k * kd[h], v_heads[h])
        inter = _mm(q * qd[h], s_heads[h])
        outs.append(intra + inter)
        s_out.append(s_heads[h] * chunk_dec[h] + kv)
    return tuple(outs), tuple(s_out)


def _ret_specs():
    const = [_full((RET_HEADS, CHUNK, CHUNK)), _full((RET_HEADS, CHUNK, RET_QK)),
             _full((RET_HEADS, CHUNK, RET_QK)), _full((RET_HEADS, CHUNK, RET_QK))]
    return const


RET_GROUP = 4


def _ret_fwd(p_ret, tabs):
    T = p_ret.shape[0]
    G = RET_GROUP
    ng = T // (CHUNK * G)
    cos_t, sin_t, dec, hm, qd, kd, chunk_dec = tabs

    def body(p_ref, cos_ref, sin_ref, dec_ref, hm_ref, qd_ref, kd_ref, out_ref, sin_save_ref, s_scr):
        @pl.when(pl.program_id(0) == 0)
        def _():
            s_scr[...] = jnp.zeros_like(s_scr)

        consts = (dec_ref[...], hm_ref[...], qd_ref[...], kd_ref[...])
        s_heads = tuple(s_scr[h] for h in range(RET_HEADS))
        for c in range(G):
            rows = pl.ds(c * CHUNK, CHUNK)
            for h in range(RET_HEADS):
                sin_save_ref[c, h] = s_heads[h]
            v_heads = tuple(p_ref[rows, 2 * RET_QK + RET_DV * h:2 * RET_QK + RET_DV * (h + 1)] for h in range(RET_HEADS))
            outs, s_heads = _ret_chunk(p_ref[rows, 0:RET_QK], p_ref[rows, RET_QK:2 * RET_QK], v_heads, s_heads,
                                       cos_ref[rows, :], sin_ref[rows, :], *consts, chunk_dec)
            for h in range(RET_HEADS):
                out_ref[rows, RET_DV * h:RET_DV * (h + 1)] = outs[h]
        for h in range(RET_HEADS):
            s_scr[h] = s_heads[h]

    tok = CHUNK * G
    return pl.pallas_call(
        body, name="ret_fwd", grid=(ng,),
        in_specs=[pl.BlockSpec((tok, RET_COLS), lambda i: (i, 0)), _rows(tok, RET_QK), _rows(tok, RET_QK)] + _ret_specs(),
        out_specs=[_rows(tok, RET_WIDTH), pl.BlockSpec((G, RET_HEADS, RET_QK, RET_DV), lambda i: (i, 0, 0, 0))],
        out_shape=[jax.ShapeDtypeStruct((T, RET_WIDTH), F32),
                   jax.ShapeDtypeStruct((T // CHUNK, RET_HEADS, RET_QK, RET_DV), F32)],
        scratch_shapes=[pltpu.VMEM((RET_HEADS, RET_QK, RET_DV), F32)],
        compiler_params=_params(dimension_semantics=("arbitrary",)),
    )(p_ret, cos_t, sin_t, dec, hm, qd, kd)


def _ret_bwd(p_ret, s_saved, d_ret, tabs):
    T = p_ret.shape[0]
    G = RET_GROUP
    ng = T // (CHUNK * G)
    cos_t, sin_t, dec, hm, qd, kd, chunk_dec = tabs

    def body(p_ref, s_ref, dret_ref, cos_ref, sin_ref, dec_ref, hm_ref, qd_ref, kd_ref, dp_ref, ds_scr):
        @pl.when(pl.program_id(0) == 0)
        def _():
            ds_scr[...] = jnp.zeros_like(ds_scr)

        consts = (dec_ref[...], hm_ref[...], qd_ref[...], kd_ref[...])
        d_s = tuple(ds_scr[h] for h in range(RET_HEADS))
        for c in reversed(range(G)):
            rows = pl.ds(c * CHUNK, CHUNK)
            v_heads = tuple(p_ref[rows, 2 * RET_QK + RET_DV * h:2 * RET_QK + RET_DV * (h + 1)] for h in range(RET_HEADS))
            s_heads = tuple(s_ref[c, h] for h in range(RET_HEADS))
            tables = (cos_ref[rows, :], sin_ref[rows, :]) + consts
            _, vjp = jax.vjp(lambda a, b, c_, d: _ret_chunk(a, b, c_, d, *tables, chunk_dec),
                             p_ref[rows, 0:RET_QK], p_ref[rows, RET_QK:2 * RET_QK], v_heads, s_heads)
            d_out = tuple(dret_ref[rows, RET_DV * h:RET_DV * (h + 1)] for h in range(RET_HEADS))
            dq, dk, dv, d_s = vjp((d_out, d_s))
            dp_ref[rows, 0:RET_QK] = dq
            dp_ref[rows, RET_QK:2 * RET_QK] = dk
            for h in range(RET_HEADS):
                dp_ref[rows, 2 * RET_QK + RET_DV * h:2 * RET_QK + RET_DV * (h + 1)] = dv[h]
        for h in range(RET_HEADS):
            ds_scr[h] = d_s[h]

    tok = CHUNK * G
    rev = lambda i: (ng - 1 - i, 0)
    return pl.pallas_call(
        body, name="ret_bwd", grid=(ng,),
        in_specs=[pl.BlockSpec((tok, RET_COLS), rev),
                  pl.BlockSpec((G, RET_HEADS, RET_QK, RET_DV), lambda i: (ng - 1 - i, 0, 0, 0)),
                  pl.BlockSpec((tok, RET_WIDTH), rev), pl.BlockSpec((tok, RET_QK), rev), pl.BlockSpec((tok, RET_QK), rev)]
        + _ret_specs(),
        out_specs=pl.BlockSpec((tok, 2 * RET_QK + RET_WIDTH), rev),
        out_shape=jax.ShapeDtypeStruct((T, 2 * RET_QK + RET_WIDTH), F32),
        scratch_shapes=[pltpu.VMEM((RET_HEADS, RET_QK, RET_DV), F32)],
        compiler_params=_params(dimension_semantics=("arbitrary",)),
    )(p_ret, s_saved, d_ret, cos_t, sin_t, dec, hm, qd, kd)


def _wkv_consts():
    lane = lax.broadcasted_iota(jnp.int32, (RWKV_HEAD, RWKV_WIDTH), 1)
    sub = lax.broadcasted_iota(jnp.int32, (RWKV_HEAD, RWKV_WIDTH), 0)
    diag = ((lane & (RWKV_HEAD - 1)) == sub).astype(F32)
    r = lax.broadcasted_iota(jnp.int32, (3 * 128, 128), 0)
    c = lax.broadcasted_iota(jnp.int32, (3 * 128, 128), 1)
    ones = (((r & 127) >> 6) == (c >> 6)).astype(BF16)
    return diag, ones


def _stack(x):
    return jnp.concatenate([x[:, 128 * p:128 * (p + 1)] for p in range(4)], axis=0)


def _unstack(y):
    n = y.shape[0] // 4
    return jnp.concatenate([y[n * p:n * (p + 1)] for p in range(4)], axis=1)


def _split(x, n):
    pieces = []
    for _ in range(n):
        p = x.astype(BF16)
        pieces.append(p)
        x = x - p.astype(F32)
    return pieces


def _lane_sum(x, ones):
    return _unstack(jnp.dot(_two_piece(_stack(x)), ones[:256].astype(F32), preferred_element_type=F32))


def _lane_sum_bf(x, ones):
    return _unstack(jnp.dot(_stack(x).astype(BF16), ones[:128], preferred_element_type=F32))


def _colsum(x):
    return jnp.sum(x, axis=0, keepdims=True)


def _rows_times(vec, mat):
    lane = lax.broadcasted_iota(jnp.int32, (1, 128), 1)
    tiles = []
    for p in range(4):
        lhs = jnp.concatenate([vec[:, 128 * p:128 * p + RWKV_HEAD], vec[:, 128 * p + RWKV_HEAD:128 * (p + 1)]], axis=0)
        out = jnp.dot(lhs, mat[:, 128 * p:128 * (p + 1)], preferred_element_type=F32)
        tiles.append(jnp.where(lane < RWKV_HEAD, out[0:1], out[1:2]))
    return jnp.concatenate(tiles, axis=1)


def _expand_cols(xt, t):
    lane = lax.broadcasted_iota(jnp.int32, (RWKV_HEAD, 128), 1)
    tiles = []
    for p in range(4):
        lo = jnp.broadcast_to(xt[128 * p:128 * p + RWKV_HEAD, t:t + 1], (RWKV_HEAD, 128))
        hi = jnp.broadcast_to(xt[128 * p + RWKV_HEAD:128 * (p + 1), t:t + 1], (RWKV_HEAD, 128))
        tiles.append(jnp.where(lane < RWKV_HEAD, lo, hi))
    return jnp.concatenate(tiles, axis=1)


def _head_sums(x, ones):
    return _unstack(jnp.dot(jnp.concatenate(_split(_stack(x), 3), axis=1), ones, preferred_element_type=F32))


def _wkv_fwd(r, w, k, v, kap, a):
    T = r.shape[0]
    C = WKV_CHUNK
    nc = T // C

    def body(r_ref, w_ref, k_ref, v_ref, kap_ref, a_ref, o_ref, s_all_ref, sa_all_ref, s_scr):
        @pl.when(pl.program_id(0) == 0)
        def _():
            s_scr[...] = jnp.zeros_like(s_scr)

        diag, ones = _wkv_consts()
        rr, ww, kk, vv, kap_, aa = (ref[...] for ref in (r_ref, w_ref, k_ref, v_ref, kap_ref, a_ref))
        bb = kap_ * aa
        c1 = _head_sums(pltpu.roll(bb, 1, axis=0) * kap_, ones)
        row = lambda x, t: x[t:t + 1]

        v_cols = vv.T

        s_prev = s_scr[...]
        sa = _lane_sum(s_prev * (-row(kap_, 0)), ones)
        ls, rows = None, []

        def emit_o(t, s_t):
            rows.append(_colsum(_lane_sum_bf(s_t * row(rr, t), ones) * diag))
            if t % 8 == 7:
                o_ref[t - 7:t + 1, :] = jnp.concatenate(rows, axis=0)
                rows.clear()

        for t in range(C):
            u = s_prev * row(ww, t) + _expand_cols(v_cols, t) * row(kk, t)
            if t > 0:
                sa = ls - sa * row(c1, t)
            if t + 1 < C:
                ls = _lane_sum(u * (-row(kap_, t + 1)), ones)
            if t > 0:
                emit_o(t - 1, s_prev)
            s_prev = u + sa * row(bb, t)
            s_all_ref[t] = s_prev
            sa_all_ref[t] = sa.astype(BF16)
        emit_o(C - 1, s_prev)
        s_scr[...] = s_prev

    spec = _rows(C, RWKV_WIDTH)
    return pl.pallas_call(
        body, name="wkv_fwd", grid=(nc,),
        in_specs=[spec] * 6,
        out_specs=[spec, pl.BlockSpec((C, RWKV_HEAD, RWKV_WIDTH), lambda i: (i, 0, 0)),
                   pl.BlockSpec((C, RWKV_HEAD, RWKV_WIDTH), lambda i: (i, 0, 0))],
        out_shape=[jax.ShapeDtypeStruct((T, RWKV_WIDTH), F32), jax.ShapeDtypeStruct((T, RWKV_HEAD, RWKV_WIDTH), F32),
                   jax.ShapeDtypeStruct((T, RWKV_HEAD, RWKV_WIDTH), BF16)],
        scratch_shapes=[pltpu.VMEM((RWKV_HEAD, RWKV_WIDTH), F32)],
        compiler_params=_params(dimension_semantics=("arbitrary",)),
    )(r, w, k, v, kap, a)


def _wkv_bwd(r, w, k, v, kap, a, s_all, sa_all, d_o):
    T = r.shape[0]
    C = WKV_CHUNK
    nc = T // C

    def body(r_ref, w_ref, k_ref, v_ref, kap_ref, a_ref, s_ref, s_before_ref, sa_ref, do_ref,
             dr_ref, dw_ref, dk_ref, dv_ref, dkap_ref, da_ref, ds_scr):
        first_chunk = pl.program_id(0) == nc - 1

        @pl.when(pl.program_id(0) == 0)
        def _():
            ds_scr[...] = jnp.zeros_like(ds_scr)

        diag, ones = _wkv_consts()
        rr, ww, kk, vv, kap_, aa, dd = (ref[...] for ref in (r_ref, w_ref, k_ref, v_ref, kap_ref, a_ref, do_ref))
        bb = kap_ * aa
        e1 = _head_sums(pltpu.roll(kap_, C - 1, axis=0) * bb, ones)
        row = lambda x, t: x[t:t + 1]

        def state_before(t):
            return s_ref[t - 1] if t > 0 else jnp.where(first_chunk, 0.0, s_before_ref[0])

        do_cols = dd.T

        d_sn, dsa, rows = None, None, [None] * C

        def emit_rows(t, d_sn_t, dsa_t):
            s_prev, dof = state_before(t), _expand_cols(do_cols, t)
            dv = _colsum(_lane_sum_bf(d_sn_t * row(kk, t), ones) * diag)
            db = _colsum(d_sn_t * sa_ref[t].astype(F32))
            rows[t] = (_colsum(s_ref[t] * dof), _colsum(d_sn_t * s_prev), _rows_times(row(vv, t), d_sn_t), dv,
                       db * row(aa, t) - _colsum(dsa_t * s_prev), db * row(kap_, t))
            if t % 8 == 0:
                for j, ref in enumerate((dr_ref, dw_ref, dk_ref, dv_ref, dkap_ref, da_ref)):
                    ref[t:t + 8, :] = jnp.concatenate([rows[u][j] for u in range(t, t + 8)], axis=0)

        for t in reversed(range(C)):
            dof = _expand_cols(do_cols, t)
            if t == C - 1:
                d_sn = ds_scr[...] + dof * row(rr, t)
                dsa = _lane_sum(d_sn * row(bb, t), ones)
            else:
                v_t = d_sn * row(ww, t + 1) + dof * row(rr, t)
                ls = _lane_sum(v_t * row(bb, t), ones)
                emit_rows(t + 1, d_sn, dsa)
                d_sn = v_t - dsa * row(kap_, t + 1)
                dsa = ls - dsa * row(e1, t)
        emit_rows(0, d_sn, dsa)
        d_s = d_sn * row(ww, 0) - dsa * row(kap_, 0)
        ds_scr[...] = d_s

    spec = pl.BlockSpec((C, RWKV_WIDTH), lambda i: (nc - 1 - i, 0))
    states = pl.BlockSpec((C, RWKV_HEAD, RWKV_WIDTH), lambda i: (nc - 1 - i, 0, 0))
    before = pl.BlockSpec((1, RWKV_HEAD, RWKV_WIDTH), lambda i: (jnp.maximum((nc - 1 - i) * C - 1, 0), 0, 0))
    return pl.pallas_call(
        body, name="wkv_bwd", grid=(nc,),
        in_specs=[spec] * 6 + [states, before, states, spec],
        out_specs=[spec] * 6,
        out_shape=[jax.ShapeDtypeStruct((T, RWKV_WIDTH), F32)] * 6,
        scratch_shapes=[pltpu.VMEM((RWKV_HEAD, RWKV_WIDTH), F32)],
        compiler_params=_params(dimension_semantics=("arbitrary",)),
    )(r, w, k, v, kap, a, s_all, s_all, sa_all, d_o)


W = RWKV_WIDTH


def _softplus(y):
    return jnp.maximum(y, 0.0) + jnp.log(1.0 + jnp.exp(-jnp.abs(y)))


def _prep_fn(kr, xwa, w0, a0, k_k, k_a, wup_pad, aup_pad, ones64):
    w_log = -_softplus(-(w0 + _mm_nt(jnp.tanh(xwa), wup_pad))) - 0.5
    decay = jnp.exp(-jnp.exp(w_log))
    a = jax.nn.sigmoid(a0 + _mm_nt(xwa, aup_pad))
    kk = kr * k_k
    kap = kk / jnp.maximum(jnp.sqrt(_head_mix(kk * kk, ones64)), 1e-12)
    k = kr * (1.0 + (a - 1.0) * k_a)
    return decay, k, kap, a


def _shift_down(p, first_row):
    rows = lax.broadcasted_iota(jnp.int32, p.shape, 0)
    return jnp.where(rows == 0, first_row, pltpu.roll(p, 1, axis=0))


def _shift_up(z, last_row):
    n = z.shape[0]
    rows = lax.broadcasted_iota(jnp.int32, z.shape, 0)
    return jnp.where(rows == n - 1, last_row, pltpu.roll(z, n - 1, axis=0))


def _prev_block_spec():
    return pl.BlockSpec((8, RWKV_COLS), lambda i: (jnp.maximum(i * (TOK_TILE // 8) - 1, 0), 0))


def _mixed(p_ref, prev8_ref, mu_ref, first_tile):
    p = p_ref[...]
    first_row = jnp.where(first_tile, 0.0, prev8_ref[7:8, :])
    prev = _shift_down(p, first_row)
    return p, prev, p + mu_ref[...] * (prev - p)


def _prep_fwd(p_rwkv, mu, w0, a0, k_k, k_a, wup_pad, aup_pad, ones64):
    T = p_rwkv.shape[0]

    def body(p_ref, prev8_ref, mu_ref, w0_ref, a0_ref, kk_ref, ka_ref, wup_ref, aup_ref, ones_ref,
             r_ref, w_ref, k_ref, v_ref, kap_ref, a_ref, g_ref):
        _, _, ps = _mixed(p_ref, prev8_ref, mu_ref, pl.program_id(0) == 0)
        decay, k, kap, a = _prep_fn(ps[:, W:2 * W], ps[:, 4 * W:], w0_ref[...], a0_ref[...], kk_ref[...], ka_ref[...],
                                    wup_ref[...], aup_ref[...], ones_ref[...])
        r_ref[...] = ps[:, 0:W]
        w_ref[...] = decay
        k_ref[...] = k
        v_ref[...] = ps[:, 2 * W:3 * W]
        kap_ref[...] = kap
        a_ref[...] = a
        g_ref[...] = ps[:, 3 * W:4 * W]

    vec = _full((1, W))
    return pl.pallas_call(
        body, name="prep_fwd", grid=(T // TOK_TILE,),
        in_specs=[_rows(TOK_TILE, RWKV_COLS), _prev_block_spec(), _full((1, RWKV_COLS)), vec, vec, vec, vec,
                  _full((W, 2 * LORA)), _full((W, 2 * LORA)), _full((256, 128))],
        out_specs=[_rows(TOK_TILE, W)] * 7,
        out_shape=[jax.ShapeDtypeStruct((T, W), F32)] * 7,
        compiler_params=_params(dimension_semantics=("arbitrary",)),
    )(p_rwkv, p_rwkv, mu, w0, a0, k_k, k_a, wup_pad, aup_pad, ones64)


def _prep_bwd(p_rwkv, mu, w0, a0, k_k, k_a, wup_pad, aup_pad, ones64, dr, dw, dk, dv, dkap, da, dg, dr2, dk2, dv2):
    T = p_rwkv.shape[0]
    nt = T // TOK_TILE

    def body(p_ref, prev8_ref, mu_ref, w0_ref, a0_ref, kk_ref, ka_ref, wup_ref, aup_ref, ones_ref,
             dr_ref, dw_ref, dk_ref, dv_ref, dkap_ref, da_ref, dg_ref, dr2_ref, dk2_ref, dv2_ref,
             dp_ref, dmu_ref, dw0_ref, da0_ref, dkk_ref, dka_ref, dwup_ref, daup_ref, zrow_scr):
        i = pl.program_id(0)
        accs = (dmu_ref, dw0_ref, da0_ref, dkk_ref, dka_ref, dwup_ref, daup_ref)

        @pl.when(i == 0)
        def _():
            zrow_scr[...] = jnp.zeros_like(zrow_scr)
            for ref in accs:
                ref[...] = jnp.zeros_like(ref)

        p, prev, ps = _mixed(p_ref, prev8_ref, mu_ref, i == nt - 1)
        ones = ones_ref[...]
        _, vjp = jax.vjp(lambda *args: _prep_fn(*args, ones), ps[:, W:2 * W], ps[:, 4 * W:], w0_ref[...], a0_ref[...],
                         kk_ref[...], ka_ref[...], wup_ref[...], aup_ref[...])
        dkr, dxwa, dw0, da0, dkk, dka, dwup, daup = vjp(
            (dw_ref[...], dk_ref[...] + dk2_ref[...], dkap_ref[...], da_ref[...]))
        dps = jnp.concatenate([dr_ref[...] + dr2_ref[...], dkr, dv_ref[...] + dv2_ref[...], dg_ref[...], dxwa], axis=1)
        z = dps * mu_ref[...]
        dp_ref[...] = dps - z + _shift_up(z, zrow_scr[0:1, :])
        zrow_scr[0:1, :] = z[0:1, :]
        for ref, val in zip(accs, (_colsum(dps * (prev - p)), dw0, da0, dkk, dka, dwup, daup)):
            ref[...] += val

    rev = lambda i: (nt - 1 - i, 0)
    vec = _full((1, W))
    lora = _full((W, 2 * LORA))
    tile = pl.BlockSpec((TOK_TILE, W), rev)
    prev8 = pl.BlockSpec((8, RWKV_COLS), lambda i: (jnp.maximum((nt - 1 - i) * (TOK_TILE // 8) - 1, 0), 0))
    return pl.pallas_call(
        body, name="prep_bwd", grid=(nt,),
        in_specs=[pl.BlockSpec((TOK_TILE, RWKV_COLS), rev), prev8, _full((1, RWKV_COLS)), vec, vec, vec, vec, lora, lora,
                  _full((256, 128))] + [tile] * 10,
        out_specs=[pl.BlockSpec((TOK_TILE, RWKV_COLS), rev), _full((1, RWKV_COLS)), vec, vec, vec, vec, lora, lora],
        out_shape=[jax.ShapeDtypeStruct((T, RWKV_COLS), F32), jax.ShapeDtypeStruct((1, RWKV_COLS), F32)]
        + [jax.ShapeDtypeStruct((1, W), F32)] * 4 + [jax.ShapeDtypeStruct((W, 2 * LORA), F32)] * 2,
        scratch_shapes=[pltpu.VMEM((8, RWKV_COLS), F32)],
        compiler_params=_params(dimension_semantics=("arbitrary",)),
    )(p_rwkv, p_rwkv, mu, w0, a0, k_k, k_a, wup_pad, aup_pad, ones64, dr, dw, dk, dv, dkap, da, dg, dr2, dk2, dv2)


def _silu(x):
    return x * jax.nn.sigmoid(x)


def _post_y(o, r, k, v, g_rw, ret_raw, g_ret, ret_gn_g, gn_g, gn_b, r_k, avg128, avg64, ones64):
    xc = ret_raw - _head_mix(ret_raw, avg128)
    ret = xc * lax.rsqrt(_head_mix(xc * xc, avg128) + RET_GN_EPS)
    y_ret = _silu(g_ret) * (ret * ret_gn_g)
    oc = o - _head_mix(o, avg64)
    on = oc * lax.rsqrt(_head_mix(oc * oc, avg64) + RWKV_GN_EPS) * gn_g + gn_b
    bonus = _head_mix(r * k * r_k, ones64) * v
    y_rwkv = _silu(g_rw) * (on + bonus)
    return y_ret, y_rwkv


def _post_loss(h, final_g, target):
    err = _rmsnorm(h, final_g) - target
    return 0.5 * jnp.sum(jnp.mean(err * err, axis=-1))


def _post(o, r, k, v, g_rw, ret_raw, p_ret, x, target, ret_gn_g, gn_g, gn_b, r_k, final_g, w_out, avg128, avg64, ones64):
    T = x.shape[1]
    n_tok_out = 8

    def body(o_ref, r_ref, k_ref, v_ref, grw_ref, ret_ref, gret_ref, x_ref, tgt_ref, rg_ref, gg_ref, gb_ref, rk_ref, fg_ref,
             wo_ref, a128_ref, a64_ref, ones_ref, *outs):
        tok_outs, (dwo_ref, drg_ref, dgg_ref, dgb_ref, drk_ref, dfg_ref, loss_ref) = outs[:n_tok_out], outs[n_tok_out:]
        accs = (dwo_ref, drg_ref, dgg_ref, dgb_ref, drk_ref, dfg_ref, loss_ref)

        @pl.when(pl.program_id(0) == 0)
        def _():
            for ref in accs:
                ref[...] = jnp.zeros_like(ref)

        consts = (a128_ref[...], a64_ref[...], ones_ref[...])
        (y_ret, y_rwkv), vjp = jax.vjp(
            lambda *args: _post_y(*args, *consts), o_ref[...], r_ref[...], k_ref[...], v_ref[...], grw_ref[...], ret_ref[...],
            gret_ref[...], rg_ref[...], gg_ref[...], gb_ref[...], rk_ref[...])
        h = x_ref[...] + _dot_bf(y_ret, wo_ref[0:RET_WIDTH, :]) + _dot_bf(y_rwkv, wo_ref[RET_WIDTH:, :])
        loss, (dh, dfg) = jax.value_and_grad(_post_loss, argnums=(0, 1))(h, fg_ref[...], tgt_ref[...])
        dy_ret = _dot_nt_bf(dh, wo_ref[0:RET_WIDTH, :])
        dy_rwkv = _dot_nt_bf(dh, wo_ref[RET_WIDTH:, :])
        do, dr, dk, dv, dgrw, dret, dgret, drg, dgg, dgb, drk = vjp((dy_ret, dy_rwkv))
        for ref, val in zip(tok_outs, (dh, do, dr, dk, dv, dgrw, dret, dgret)):
            ref[...] = val
        dwo_ref[0:RET_WIDTH, :] += _dot_tn_bf(y_ret, dh)
        dwo_ref[RET_WIDTH:, :] += _dot_tn_bf(y_rwkv, dh)
        for ref, val in zip(accs[1:], (drg, dgg, dgb, drk, dfg, jnp.full((1, 128), loss, F32))):
            ref[...] += val

    tile = _rows(TOK_TILE, W)
    wide = _rows(TOK_TILE, D_MODEL)
    wide_of_one = _rows_of_one(TOK_TILE, D_MODEL)
    vec = _full((1, W))
    sq = _full((256, 128))
    return pl.pallas_call(
        body, name="post", grid=(T // TOK_TILE,),
        in_specs=[tile] * 6 + [pl.BlockSpec((TOK_TILE, W), lambda i: (i, 2)), wide_of_one, wide_of_one, vec, vec, vec, vec,
                               _full((1, D_MODEL)), _full((D_MODEL, D_MODEL)), sq, sq, sq],
        out_specs=[wide] + [tile] * 7 + [_full((D_MODEL, D_MODEL)), vec, vec, vec, vec, _full((1, D_MODEL)), _full((1, 128))],
        out_shape=[jax.ShapeDtypeStruct((T, D_MODEL), F32)] + [jax.ShapeDtypeStruct((T, W), F32)] * 7
        + [jax.ShapeDtypeStruct((D_MODEL, D_MODEL), F32)] + [jax.ShapeDtypeStruct((1, W), F32)] * 4
        + [jax.ShapeDtypeStruct((1, D_MODEL), F32), jax.ShapeDtypeStruct((1, 128), F32)],
        compiler_params=_params(dimension_semantics=("arbitrary",)),
    )(o, r, k, v, g_rw, ret_raw, p_ret, x, target, ret_gn_g, gn_g, gn_b, r_k, final_g, w_out, avg128, avg64, ones64)


def _inproj_bwd_x(x, norm_g, dp_qkv, dg_ret, dp_rwkv, dh, w_in_t):
    T = x.shape[1]
    n_qkv = 2 * RET_QK + RET_WIDTH

    def body(x_ref, g_ref, dqkv_ref, dgret_ref, drwkv_ref, dh_ref, w_ref, dx_ref, dg_ref):
        @pl.when(pl.program_id(0) == 0)
        def _():
            dg_ref[...] = jnp.zeros_like(dg_ref)

        _, vjp = jax.vjp(_rmsnorm, x_ref[...], g_ref[...])
        du = (_dot_bf(dqkv_ref[...], w_ref[0:n_qkv, :]) + _dot_bf(dgret_ref[...], w_ref[n_qkv:RET_COLS, :])
              + _dot_bf(drwkv_ref[...], w_ref[RET_COLS:, :]))
        dx, dg = vjp(du)
        dx_ref[...] = dx + dh_ref[...]
        dg_ref[...] += dg

    return pl.pallas_call(
        body, name="inproj_bwd_x", grid=(T // TOK_TILE,),
        in_specs=[_rows_of_one(TOK_TILE, D_MODEL), _full((1, D_MODEL)), _rows(TOK_TILE, n_qkv), _rows(TOK_TILE, RET_WIDTH),
                  _rows(TOK_TILE, RWKV_COLS), _rows(TOK_TILE, D_MODEL), _full((IN_COLS, D_MODEL))],
        out_specs=[_rows_of_one(TOK_TILE, D_MODEL), _full((1, D_MODEL))],
        out_shape=[jax.ShapeDtypeStruct((1, T, D_MODEL), F32), jax.ShapeDtypeStruct((1, D_MODEL), F32)],
        compiler_params=_params(dimension_semantics=("arbitrary",)),
    )(x, norm_g, dp_qkv, dg_ret, dp_rwkv, dh, w_in_t)


def _grad_w(name, u, dp):
    T, n = dp.shape
    tile = 2 * TOK_TILE
    steps = T // tile

    def body(u_ref, dp_ref, out_ref, acc_ref):
        @pl.when(pl.program_id(0) == 0)
        def _():
            acc_ref[...] = jnp.zeros_like(acc_ref)

        acc_ref[...] += _dot_tn_bf(dp_ref[...], u_ref[...])

        @pl.when(pl.program_id(0) == steps - 1)
        def _():
            out_ref[...] = acc_ref[...].astype(BF16)

    return pl.pallas_call(
        body, name=name, grid=(steps,),
        in_specs=[_rows(tile, D_MODEL), _rows(tile, n)],
        out_specs=_full((n, D_MODEL)),
        out_shape=jax.ShapeDtypeStruct((n, D_MODEL), BF16),
        scratch_shapes=[pltpu.VMEM((n, D_MODEL), F32)],
        compiler_params=_params(dimension_semantics=("arbitrary",)),
    )(u, dp)


def _pad_lora(w_up_t, first):
    z = jnp.zeros_like(w_up_t)
    return jnp.concatenate([w_up_t, z] if first else [z, w_up_t], axis=1)


def _local_grads(x, target, norm_g, w_in_t, ret_gn_g, mu, w_lora_up_t, w0, a_lora_up_t, a0, k_k, k_a, r_k, gn_g, gn_b,
                 w_out_bf, final_g):
    T = x.shape[1]
    tabs = _rope_tables(T) + _ret_tables()
    ones64 = _block_mix(128, RWKV_HEAD)
    avg64 = _block_mix(128, RWKV_HEAD, 1.0 / RWKV_HEAD)
    avg128 = _block_mix(128, RET_DV, 1.0 / RET_DV)
    wup_pad, aup_pad = _pad_lora(w_lora_up_t, True), _pad_lora(a_lora_up_t, False)

    p_ret, p_rwkv, u = _inproj(x, norm_g, w_in_t)
    ret_raw, s_saved = _ret_fwd(p_ret, tabs)
    r, w, k, v, kap, a, g_rw = _prep_fwd(p_rwkv, mu, w0, a0, k_k, k_a, wup_pad, aup_pad, ones64)
    o, s_all, sa_all = _wkv_fwd(r, w, k, v, kap, a)
    (dh, do, dr2, dk2, dv2, dgrw, dret, dgret, d_w_out, d_ret_gn_g, d_gn_g, d_gn_b, d_r_k, d_final_g, loss) = _post(
        o, r, k, v, g_rw, ret_raw, p_ret, x, target, ret_gn_g, gn_g, gn_b, r_k, final_g, w_out_bf, avg128, avg64, ones64)
    dr, dw, dk, dv, dkap, da = _wkv_bwd(r, w, k, v, kap, a, s_all, sa_all, do)
    dp_rwkv, d_mu, d_w0, d_a0, d_k_k, d_k_a, d_wup, d_aup = _prep_bwd(
        p_rwkv, mu, w0, a0, k_k, k_a, wup_pad, aup_pad, ones64, dr, dw, dk, dv, dkap, da, dgrw, dr2, dk2, dv2)
    dp_qkv = _ret_bwd(p_ret, s_saved, dret, tabs)
    dx, d_norm_g = _inproj_bwd_x(x, norm_g, dp_qkv, dgret, dp_rwkv, dh, w_in_t)
    d_w_in = jnp.concatenate([_grad_w("grad_w_qkv", u, dp_qkv), _grad_w("grad_w_gret", u, dgret),
                              _grad_w("grad_w_rwkv", u, dp_rwkv)], axis=0)
    grads = dict(norm_g=d_norm_g, w_in=d_w_in, ret_gn_g=d_ret_gn_g, rwkv_mu=d_mu, w_lora_up=d_wup[:, :LORA], w0=d_w0,
                 a_lora_up=d_aup[:, LORA:], a0=d_a0, k_k=d_k_k, k_a=d_k_a, r_k=d_r_k, rwkv_gn_g=d_gn_g, rwkv_gn_b=d_gn_b,
                 w_out=d_w_out, final_norm_g=d_final_g)
    return loss, dx, grads


def _mesh_pos():
    return lax.axis_index("x"), lax.axis_index("y"), lax.axis_index("c")


def _all_gather(shards):
    n = len(shards)

    def body(*refs):
        x_refs, out_refs = refs[:n], refs[n:2 * n]
        send_sems, recv_sems, local_sems = refs[2 * n:]
        x, y, c = _mesh_pos()
        me, sibling = (x, y, c), (x, y, 1 - c)
        chips = [(1 - x, y), (x, 1 - y), (1 - x, 1 - y)]

        def rows(a, pos):
            m = x_refs[a].shape[0]
            return out_refs[a].at[pl.ds((4 * pos[0] + 2 * pos[1] + pos[2]) * m, m), :]

        def copy(a, k, block, to, src=None):
            return pltpu.make_async_remote_copy(
                src_ref=rows(a, block) if src is None else src, dst_ref=rows(a, block),
                send_sem=send_sems.at[a, k], recv_sem=recv_sems.at[a, k], device_id=to, device_id_type=MESH)

        mine = [pltpu.make_async_copy(x_refs[a], rows(a, me), local_sems.at[a]) for a in range(n)]
        for cp in mine:
            cp.start()
        first = []
        for a in range(n):
            first.append(copy(a, 0, me, sibling, src=x_refs[a]))
            first += [copy(a, 1 + j, me, (*chip, c), src=x_refs[a]) for j, chip in enumerate(chips)]
        for cp in first:
            cp.start()
        passed = []
        for j, chip in enumerate(chips):
            for a in range(n):
                copy(a, 1 + j, (*chip, c), me).wait_recv()
                passed.append(copy(a, 4 + j, (*chip, c), sibling))
                passed[-1].start()
        for a in range(n):
            copy(a, 0, sibling, me).wait_recv()
            for j, chip in enumerate(chips):
                copy(a, 4 + j, (*chip, 1 - c), me).wait_recv()
        for cp in first + passed:
            cp.wait_send()
        for cp in mine:
            cp.wait()

    vmem = pl.BlockSpec(memory_space=pltpu.VMEM)
    return pl.pallas_call(
        body, name="gather_weights",
        out_shape=[jax.ShapeDtypeStruct((N_DEV * s.shape[0], s.shape[1]), s.dtype) for s in shards],
        in_specs=[vmem] * n, out_specs=[vmem] * n,
        scratch_shapes=[pltpu.SemaphoreType.DMA((n, 7)), pltpu.SemaphoreType.DMA((n, 7)), pltpu.SemaphoreType.DMA((n,))],
        compiler_params=_params(),
    )(*shards)


N_CHIP = 4


def _exchange_pairs(big, small):
    nb, ns = len(big), len(small)

    def body(*refs):
        big_in, small_in = refs[:nb], refs[nb:nb + ns]
        theirs, small_out = refs[nb + ns:2 * nb + ns], refs[2 * nb + ns:2 * nb + 2 * ns]
        pair_send, pair_recv, send_sems, recv_sems, local_sems = refs[2 * nb + 2 * ns:]
        x, y, c = _mesh_pos()
        me = 4 * x + 2 * y + c
        local = [pltpu.make_async_copy(small_in[a].at[me], small_out[a].at[me], local_sems.at[a]) for a in range(ns)]
        for cp in local:
            cp.start()
        copies = [pltpu.make_async_remote_copy(
            src_ref=big_in[a], dst_ref=theirs[a], send_sem=pair_send.at[a], recv_sem=pair_recv.at[a],
            device_id=(x, y, 1 - c), device_id_type=MESH) for a in range(nb)]
        for k in range(1, N_DEV):
            peer = (x ^ (k >> 2), y ^ ((k >> 1) & 1), c ^ (k & 1))
            peer_idx = 4 * peer[0] + 2 * peer[1] + peer[2]
            copies += [pltpu.make_async_remote_copy(
                src_ref=small_in[a].at[peer_idx], dst_ref=small_out[a].at[me], send_sem=send_sems.at[a, k - 1],
                recv_sem=recv_sems.at[a, k - 1], device_id=peer, device_id_type=MESH) for a in range(ns)]
        for cp in copies:
            cp.start()
        for cp in copies:
            cp.wait()
        for cp in local:
            cp.wait()

    hbm = pl.BlockSpec(memory_space=pl.ANY)
    out_shape = [jax.ShapeDtypeStruct(p.shape, p.dtype) for p in big + small]
    dma = pltpu.SemaphoreType.DMA
    res = pl.pallas_call(
        body, name="exchange_pairs", out_shape=out_shape,
        in_specs=[hbm] * (nb + ns), out_specs=[hbm] * len(out_shape),
        scratch_shapes=[dma((nb,)), dma((nb,)), dma((ns, 7)), dma((ns, 7)), dma((ns,))],
        compiler_params=_params(),
    )(*big, *small)
    return res[:nb], res[nb:]


def _pair_sum(name, mine, theirs, row_tile):
    _, rows, cols = mine.shape

    def body(a_ref, b_ref, o_ref):
        o_ref[...] = (a_ref[...].astype(F32) + b_ref[...].astype(F32)).astype(o_ref.dtype)

    spec = pl.BlockSpec((N_CHIP, row_tile, cols), lambda i: (0, i, 0))
    return pl.pallas_call(
        body, name=name, grid=(rows // row_tile,), in_specs=[spec, spec], out_specs=spec,
        out_shape=jax.ShapeDtypeStruct(mine.shape, mine.dtype),
        compiler_params=_params(dimension_semantics=("arbitrary",)),
    )(mine, theirs)


def _exchange_chips(parts):
    n = len(parts)

    def body(*refs):
        in_refs, out_refs = refs[:n], refs[n:2 * n]
        send_sems, recv_sems, local_sems = refs[2 * n:]
        x, y, c = _mesh_pos()
        my_chip = 2 * x + y
        own = [pltpu.make_async_copy(in_refs[a].at[my_chip], out_refs[a].at[my_chip], local_sems.at[a]) for a in range(n)]
        for cp in own:
            cp.start()
        copies = []
        for k in range(1, N_CHIP):
            px, py = x ^ (k >> 1), y ^ (k & 1)
            copies += [pltpu.make_async_remote_copy(
                src_ref=in_refs[a].at[2 * px + py], dst_ref=out_refs[a].at[my_chip], send_sem=send_sems.at[a, k - 1],
                recv_sem=recv_sems.at[a, k - 1], device_id=(px, py, c), device_id_type=MESH) for a in range(n)]
        for cp in copies:
            cp.start()
        for cp in copies:
            cp.wait()
        for cp in own:
            cp.wait()

    hbm = pl.BlockSpec(memory_space=pl.ANY)
    dma = pltpu.SemaphoreType.DMA
    return pl.pallas_call(
        body, name="exchange_chips",
        out_shape=[jax.ShapeDtypeStruct(p.shape, p.dtype) for p in parts],
        in_specs=[hbm] * n, out_specs=[hbm] * n,
        scratch_shapes=[dma((n, N_CHIP - 1)), dma((n, N_CHIP - 1)), dma((n,))],
        compiler_params=_params(),
    )(*parts)


def _adamw(w, g, m, v):
    m = ADAM_B1 * m + (1.0 - ADAM_B1) * g
    v = ADAM_B2 * v + (1.0 - ADAM_B2) * (g * g)
    m_hat = m / (1.0 - ADAM_B1 ** ADAM_STEP)
    v_hat = v / (1.0 - ADAM_B2 ** ADAM_STEP)
    return -ADAM_LR * (m_hat / (jnp.sqrt(v_hat) + ADAM_EPS) + ADAM_WD * w), m, v


def _sum_parts(name, parts, row_tile):
    n_parts, rows, cols = parts.shape

    def body(p_ref, g_ref):
        g = p_ref[0].astype(F32)
        for s in range(1, n_parts):
            g = g + p_ref[s].astype(F32)
        g_ref[...] = g

    return pl.pallas_call(
        body, name=name, grid=(rows // row_tile,),
        in_specs=[pl.BlockSpec((n_parts, row_tile, cols), lambda i: (0, i, 0))],
        out_specs=pl.BlockSpec((row_tile, cols), lambda i: (i, 0)),
        out_shape=jax.ShapeDtypeStruct((rows, cols), F32),
        compiler_params=_params(dimension_semantics=("arbitrary",)),
    )(parts)


def _adamw_apply(name, g, w, m, v, row_tile):
    _, rows, cols = w.shape

    def body(g_ref, w_ref, m_ref, v_ref, d_ref, nm_ref, nv_ref):
        d_ref[0], nm_ref[0], nv_ref[0] = _adamw(w_ref[0], g_ref[0], m_ref[0], v_ref[0])

    tile = pl.BlockSpec((1, row_tile, cols), lambda i: (0, i, 0))
    return pl.pallas_call(
        body, name=name, grid=(rows // row_tile,), in_specs=[tile] * 4, out_specs=[tile] * 3,
        out_shape=[jax.ShapeDtypeStruct((1, rows, cols), F32)] * 3,
        compiler_params=_params(dimension_semantics=("arbitrary",)),
    )(g, w, m, v)


def _reduce_adamw_2d(name, parts, w, m, v, row_tile):
    n_parts, rows, cols = parts.shape

    def body(p_ref, w_ref, m_ref, v_ref, g_ref, d_ref, nm_ref, nv_ref):
        g = p_ref[0].astype(F32)
        for s in range(1, n_parts):
            g = g + p_ref[s].astype(F32)
        g_ref[...] = g
        d_ref[...], nm_ref[...], nv_ref[...] = _adamw(w_ref[...], g, m_ref[...], v_ref[...])

    tile = pl.BlockSpec((row_tile, cols), lambda i: (i, 0))
    return pl.pallas_call(
        body, name=name, grid=(rows // row_tile,),
        in_specs=[pl.BlockSpec((n_parts, row_tile, cols), lambda i: (0, i, 0)), tile, tile, tile],
        out_specs=[tile] * 4,
        out_shape=[jax.ShapeDtypeStruct((rows, cols), F32)] * 4,
        compiler_params=_params(dimension_semantics=("arbitrary",)),
    )(parts, w, m, v)


def _reduce_adamw_t(name, parts_t, w, m, v, sum_tile, row_tile):
    g = _sum_parts(name + "_sum", parts_t, sum_tile).T[None]
    return [g] + list(_adamw_apply(name, g, w, m, v, row_tile))


def _reduce_adamw(name, parts, w, m, v, row_tile):
    n_parts, rows, cols = parts.shape

    def body(p_ref, w_ref, m_ref, v_ref, g_ref, d_ref, nm_ref, nv_ref):
        g = p_ref[0].astype(F32)
        for s in range(1, n_parts):
            g = g + p_ref[s].astype(F32)
        g_ref[0] = g
        d_ref[0], nm_ref[0], nv_ref[0] = _adamw(w_ref[0], g, m_ref[0], v_ref[0])

    tile = pl.BlockSpec((1, row_tile, cols), lambda i: (0, i, 0))
    return pl.pallas_call(
        body, name=name, grid=(rows // row_tile,),
        in_specs=[pl.BlockSpec((n_parts, row_tile, cols), lambda i: (0, i, 0)), tile, tile, tile],
        out_specs=[tile] * 4,
        out_shape=[jax.ShapeDtypeStruct((1, rows, cols), F32)] * 4,
        compiler_params=_params(dimension_semantics=("arbitrary",)),
    )(parts, w, m, v)


_SMALL = (("norm_g", 1024), ("ret_gn_g", 512), ("rwkv_mu", 2176), ("w0", 512), ("a0", 512), ("k_k", 512), ("k_a", 512),
          ("r_k", 512), ("rwkv_gn_g", 512), ("rwkv_gn_b", 512), ("final_norm_g", 1024))
_SMALL_LANES = sum(n for _, n in _SMALL) + 128
_WEIGHTS = ("norm_g", "w_in", "ret_gn_g", "rwkv_mu", "w_lora_up", "w0", "a_lora_up", "a0", "k_k", "k_a", "r_k", "rwkv_gn_g",
            "rwkv_gn_b", "w_out", "final_norm_g")


def _adamw_vectors(parts, wts, mom, var):
    k = len(_SMALL)

    def body(p_ref, *refs):
        w_refs, m_refs, v_refs, outs = refs[:k], refs[k:2 * k], refs[2 * k:3 * k], refs[3 * k:]
        g_all = p_ref[0]
        for s in range(1, N_DEV):
            g_all = g_all + p_ref[s]
        off = 0
        for i, (name, n) in enumerate(_SMALL):
            g = g_all[:, off:off + n]
            off += n
            if name == "r_k":
                g = jnp.concatenate([g[:, RWKV_HEAD * h:RWKV_HEAD * (h + 1)] for h in range(RWKV_HEADS)], axis=0)[None]
            outs[4 * i][...] = g
            outs[4 * i + 1][...], outs[4 * i + 2][...], outs[4 * i + 3][...] = _adamw(
                w_refs[i][...], g, m_refs[i][...], v_refs[i][...])
        outs[4 * k][...] = g_all[:, off:off + 128]

    vmem = pl.BlockSpec(memory_space=pltpu.VMEM)
    shapes = [jax.ShapeDtypeStruct(wts[n].shape, F32) for n, _ in _SMALL for _ in range(4)] + [jax.ShapeDtypeStruct((1, 128), F32)]
    res = pl.pallas_call(
        body, name="adamw_vectors", out_shape=shapes,
        in_specs=[vmem] * (1 + 3 * k), out_specs=[vmem] * len(shapes), compiler_params=_params(),
    )(parts, *[wts[n] for n, _ in _SMALL], *[mom[n] for n, _ in _SMALL], *[var[n] for n, _ in _SMALL])
    return {n: res[4 * i:4 * i + 4] for i, (n, _) in enumerate(_SMALL)}, res[4 * k]


def kernel(x, norm_g, w_in, ret_gn_g, rwkv_mu, w_lora_up, w0, a_lora_up, a0, k_k, k_a, r_k, rwkv_gn_g, rwkv_gn_b, w_out, final_norm_g, loss_target, m_norm_g, m_w_in, m_ret_gn_g, m_rwkv_mu, m_w_lora_up, m_w0, m_a_lora_up, m_a0, m_k_k, m_k_a, m_r_k, m_rwkv_gn_g, m_rwkv_gn_b, m_w_out, m_final_norm_g, v_norm_g, v_w_in, v_ret_gn_g, v_rwkv_mu, v_w_lora_up, v_w0, v_a_lora_up, v_a0, v_k_k, v_k_a, v_r_k, v_rwkv_gn_g, v_rwkv_gn_b, v_w_out, v_final_norm_g):
    wts = dict(norm_g=norm_g, w_in=w_in, ret_gn_g=ret_gn_g, rwkv_mu=rwkv_mu, w_lora_up=w_lora_up, w0=w0, a_lora_up=a_lora_up,
               a0=a0, k_k=k_k, k_a=k_a, r_k=r_k, rwkv_gn_g=rwkv_gn_g, rwkv_gn_b=rwkv_gn_b, w_out=w_out,
               final_norm_g=final_norm_g)
    mom = dict(norm_g=m_norm_g, w_in=m_w_in, ret_gn_g=m_ret_gn_g, rwkv_mu=m_rwkv_mu, w_lora_up=m_w_lora_up, w0=m_w0,
               a_lora_up=m_a_lora_up, a0=m_a0, k_k=m_k_k, k_a=m_k_a, r_k=m_r_k, rwkv_gn_g=m_rwkv_gn_g,
               rwkv_gn_b=m_rwkv_gn_b, w_out=m_w_out, final_norm_g=m_final_norm_g)
    var = dict(norm_g=v_norm_g, w_in=v_w_in, ret_gn_g=v_ret_gn_g, rwkv_mu=v_rwkv_mu, w_lora_up=v_w_lora_up, w0=v_w0,
               a_lora_up=v_a_lora_up, a0=v_a0, k_k=v_k_k, k_a=v_k_a, r_k=v_r_k, rwkv_gn_g=v_rwkv_gn_g,
               rwkv_gn_b=v_rwkv_gn_b, w_out=v_w_out, final_norm_g=v_final_norm_g)
    shapes = {n: wts[n].shape for n in _WEIGHTS}

    w_in_t, w_out_bf, wup_t, aup_t = _all_gather(
        [w_in[0].T.astype(BF16), w_out[0].astype(BF16), w_lora_up[0].T, a_lora_up[0].T])

    loss, dx, g = _local_grads(
        x, loss_target, norm_g, w_in_t, ret_gn_g, rwkv_mu, wup_t, w0, aup_t, a0, k_k, k_a,
        r_k.reshape(1, W), rwkv_gn_g, rwkv_gn_b, w_out_bf, final_norm_g.reshape(1, D_MODEL))

    small = jnp.concatenate([g[n] for n, _ in _SMALL] + [loss], axis=1)
    core = lax.axis_index("c")
    by_core = lambda t: [lax.dynamic_index_in_dim(t, i, axis=1, keepdims=False) for i in (core, 1 - core)]
    in_mine, in_sib = by_core(g["w_in"].reshape(N_CHIP, 2, SHARD_IN, D_MODEL))
    out_mine, out_sib = by_core(g["w_out"].reshape(N_CHIP, 2, SHARD_OUT, D_MODEL).astype(BF16))
    (in_theirs, out_theirs), parts = _exchange_pairs(
        [in_sib, out_sib],
        [g["w_lora_up"].reshape(N_DEV, SHARD_LORA, LORA), g["a_lora_up"].reshape(N_DEV, SHARD_LORA, LORA),
         jnp.broadcast_to(small[None], (N_DEV, 1, _SMALL_LANES))])
    by_chip = _exchange_chips([_pair_sum("pair_sum_w_in", in_mine, in_theirs, SHARD_IN // 2),
                               _pair_sum("pair_sum_w_out", out_mine, out_theirs, SHARD_OUT)])
    res = {}
    res["w_in"] = [t.T[None] for t in _reduce_adamw_2d(
        "adamw_w_in", by_chip[0], w_in[0].T, m_w_in[0].T, v_w_in[0].T, SHARD_IN // 2)]
    res["w_out"] = _reduce_adamw("adamw_w_out", by_chip[1], w_out, m_w_out, v_w_out, SHARD_OUT)
    res["w_lora_up"] = _reduce_adamw_t("adamw_w_lora_up", parts[0], w_lora_up, m_w_lora_up, v_w_lora_up, LORA, LORA)
    res["a_lora_up"] = _reduce_adamw_t("adamw_a_lora_up", parts[1], a_lora_up, m_a_lora_up, v_a_lora_up, LORA, LORA)
    as_row = lambda d: {n: d[n] if d[n].ndim > 1 else d[n].reshape(1, size) for n, size in _SMALL}
    vec, loss_row = _adamw_vectors(parts[2], as_row(wts), as_row(mom), as_row(var))
    res.update(vec)
    res = {n: [t.reshape(shapes[n]) for t in res[n]] for n in _WEIGHTS}
    return (loss_row[0, 0], dx, *[res[n][0] for n in _WEIGHTS], *[res[n][1] for n in _WEIGHTS],
            *[res[n][2] for n in _WEIGHTS], *[res[n][3] for n in _WEIGHTS])
```

```python
import numpy as np
import jax
import jax.numpy as jnp
from jax import lax
from jax.experimental import pallas as pl
from jax.experimental.pallas import tpu as pltpu

F32 = jnp.float32
BF16 = jnp.bfloat16

D_MODEL = 1024
CHUNK = 64
RET_HEADS = 4
RET_DV = 128
RET_DK = 64
RET_QK = 256
RET_WIDTH = 512
RWKV_WIDTH = 512
RWKV_HEAD = 64
RWKV_HEADS = 8
LORA = 64
RET_COLS = 2 * RET_QK + 2 * RET_WIDTH
RWKV_COLS = 4 * RWKV_WIDTH + 2 * LORA
IN_COLS = RET_COLS + RWKV_COLS
ROPE_BASE = 10000.0
RMS_EPS = 1e-6
RET_GN_EPS = 1e-5
RWKV_GN_EPS = 64e-5
ADAM_LR = 0.001
ADAM_B1 = 0.9
ADAM_B2 = 0.999
ADAM_EPS = 1e-08
ADAM_WD = 0.01
ADAM_STEP = 10
N_DEV = 8
SHARD_IN = IN_COLS // N_DEV
SHARD_OUT = D_MODEL // N_DEV
SHARD_LORA = RWKV_WIDTH // N_DEV
VMEM_LIMIT = 56 * 1024 * 1024
TOK_TILE = 256
WKV_CHUNK = 64

MESH = pl.DeviceIdType.MESH


def _dot_bf(a, b):
    return jnp.dot(a.astype(BF16), b.astype(BF16), preferred_element_type=F32)


def _dot_nt_bf(a, b):
    return lax.dot_general(a.astype(BF16), b.astype(BF16), (((1,), (1,)), ((), ())), preferred_element_type=F32)


def _dot_tn_bf(a, b):
    return lax.dot_general(a.astype(BF16), b.astype(BF16), (((0,), (0,)), ((), ())), preferred_element_type=F32)


@jax.custom_vjp
def _mm(a, b):
    return _dot_bf(a, b)


@jax.custom_vjp
def _mm_nt(a, b):
    return _dot_nt_bf(a, b)


@jax.custom_vjp
def _mm_tn(a, b):
    return _dot_tn_bf(a, b)


_mm.defvjp(lambda a, b: (_dot_bf(a, b), (a, b)), lambda res, g: (_dot_nt_bf(g, res[1]), _dot_tn_bf(res[0], g)))
_mm_nt.defvjp(lambda a, b: (_dot_nt_bf(a, b), (a, b)), lambda res, g: (_dot_bf(g, res[1]), _dot_tn_bf(g, res[0])))
_mm_tn.defvjp(lambda a, b: (_dot_tn_bf(a, b), (a, b)), lambda res, g: (_dot_nt_bf(res[1], g), _dot_bf(res[0], g)))


def _trunc(x):
    return lax.bitcast_convert_type(lax.bitcast_convert_type(x, jnp.uint32) & jnp.uint32(0xFFFF0000), F32)


def _two_piece(x):
    hi = _trunc(x)
    return jnp.concatenate([hi, x - hi], axis=1)


def _mix_raw(x, mat2):
    return _unstack(jnp.dot(_two_piece(_stack(x)), mat2, preferred_element_type=F32))


@jax.custom_vjp
def _head_mix(x, mat2):
    return _mix_raw(x, mat2)


_head_mix.defvjp(lambda x, mat2: (_mix_raw(x, mat2), mat2), lambda mat2, g: (_mix_raw(g, mat2), jnp.zeros_like(mat2)))


def _swap_halves(x):
    lane = lax.broadcasted_iota(jnp.int32, x.shape, 1)
    return jnp.where((lane & (RET_DK - 1)) < RET_DK // 2, pltpu.roll(x, RET_QK - RET_DK // 2, axis=1),
                     pltpu.roll(x, RET_DK // 2, axis=1))


@jax.custom_vjp
def _rot(x):
    return _swap_halves(x)


_rot.defvjp(lambda x: (_swap_halves(x), None), lambda _, g: (_swap_halves(g),))


def _params(**kw):
    return pltpu.CompilerParams(vmem_limit_bytes=VMEM_LIMIT, **kw)


def _full(shape):
    nd = len(shape)
    return pl.BlockSpec(shape, lambda i, _nd=nd: (0,) * _nd)


def _rows(tile, width):
    return pl.BlockSpec((tile, width), lambda i: (i, 0))


def _rows_of_one(tile, width):
    return pl.BlockSpec((None, tile, width), lambda i: (0, i, 0))


def _block_mix(n, blk, scale=1.0):
    idx = np.arange(n) // blk
    m = (idx[:, None] == idx[None, :]).astype(np.float32) * scale
    return jnp.asarray(np.concatenate([m, m], axis=0))


def _rope_tables(T):
    half = RET_DK // 2
    expo = -np.arange(half, dtype=np.float32) / np.float32(half)
    freqs = np.exp(expo * np.float32(np.log(ROPE_BASE))).astype(np.float32)
    ang = np.arange(T, dtype=np.float32)[:, None] * freqs[None, :]
    cos, sin = np.cos(ang).astype(np.float32), np.sin(ang).astype(np.float32)
    cos_h = np.concatenate([cos, cos], axis=1)
    sin_h = np.concatenate([-sin, sin], axis=1)
    cos_t = np.tile(cos_h, (1, RET_HEADS))
    sin_t = np.tile(sin_h, (1, RET_HEADS))
    return jnp.asarray(cos_t), jnp.asarray(sin_t)


def _ret_tables():
    h = np.arange(RET_HEADS, dtype=np.float32)
    lg = np.log(1.0 - np.exp2(-5.0 - h)).astype(np.float32)
    idx = np.arange(CHUNK, dtype=np.float32)
    intra = np.exp(lg[:, None, None] * np.abs(idx[:, None] - idx[None, :])).astype(np.float32)
    q_dec = np.exp(lg[:, None] * (idx[None, :] + 1.0)).astype(np.float32)
    k_dec = np.exp(lg[:, None] * (CHUNK - 1.0 - idx[None, :])).astype(np.float32)
    chunk_dec = np.exp(lg * CHUNK).astype(np.float32)
    lane_head = np.arange(RET_QK) // RET_DK
    mask = (lane_head[None, :] == np.arange(RET_HEADS)[:, None]).astype(np.float32)
    m = np.broadcast_to(mask[:, None, :], (RET_HEADS, CHUNK, RET_QK)).copy()
    qd = m * q_dec[:, :, None]
    kd = m * k_dec[:, :, None]
    return jnp.asarray(intra), jnp.asarray(m), jnp.asarray(qd), jnp.asarray(kd), [float(c) for c in chunk_dec]


def _rmsnorm(x, g):
    return x * lax.rsqrt(jnp.mean(x * x, axis=-1, keepdims=True) + RMS_EPS) * g


def _inproj(x, norm_g, w_in_t):
    T = x.shape[1]

    def body(x_ref, g_ref, w_ref, pr_ref, pw_ref, u_ref):
        ub = _rmsnorm(x_ref[...], g_ref[...]).astype(BF16)
        u_ref[...] = ub
        pr_ref[...] = _dot_nt_bf(ub, w_ref[:RET_COLS, :])
        pw_ref[...] = _dot_nt_bf(ub, w_ref[RET_COLS:, :])

    return pl.pallas_call(
        body, name="inproj", grid=(T // TOK_TILE,),
        in_specs=[_rows_of_one(TOK_TILE, D_MODEL), _full((1, D_MODEL)), _full((IN_COLS, D_MODEL))],
        out_specs=[_rows(TOK_TILE, RET_COLS), _rows(TOK_TILE, RWKV_COLS), _rows(TOK_TILE, D_MODEL)],
        out_shape=[jax.ShapeDtypeStruct((T, RET_COLS), F32), jax.ShapeDtypeStruct((T, RWKV_COLS), F32),
                   jax.ShapeDtypeStruct((T, D_MODEL), BF16)],
        compiler_params=_params(dimension_semantics=("arbitrary",)),
    )(x, norm_g, w_in_t)


def _ret_chunk(pq, pk, v_heads, s_heads, cos_t, sin_t, dec, hm, qd, kd, chunk_dec):
    q = pq * cos_t + _rot(pq) * sin_t
    k = (pk * cos_t + _rot(pk) * sin_t) * (RET_DK ** -0.5)
    outs, s_out = [], []
    for h in range(RET_HEADS):
        sc = _mm_nt(q * hm[h], k * hm[h]) * dec[h]
        intra = _mm(sc, v_heads[h])
        kv = _mm_tn(k * kd[h], v_heads[h])
        inter = _mm(q * qd[h], s_heads[h])
        outs.append(intra + inter)
        s_out.append(s_heads[h] * chunk_dec[h] + kv)
    return tuple(outs), tuple(s_out)


def _ret_specs():
    const = [_full((RET_HEADS, CHUNK, CHUNK)), _full((RET_HEADS, CHUNK, RET_QK)),
             _full((RET_HEADS, CHUNK, RET_QK)), _full((RET_HEADS, CHUNK, RET_QK))]
    return const


RET_GROUP = 4


def _ret_fwd(p_ret, tabs):
    T = p_ret.shape[0]
    G = RET_GROUP
    ng = T // (CHUNK * G)
    cos_t, sin_t, dec, hm, qd, kd, chunk_dec = tabs

    def body(p_ref, cos_ref, sin_ref, dec_ref, hm_ref, qd_ref, kd_ref, out_ref, sin_save_ref, s_scr):
        @pl.when(pl.program_id(0) == 0)
        def _():
            s_scr[...] = jnp.zeros_like(s_scr)

        consts = (dec_ref[...], hm_ref[...], qd_ref[...], kd_ref[...])
        s_heads = tuple(s_scr[h] for h in range(RET_HEADS))
        for c in range(G):
            rows = pl.ds(c * CHUNK, CHUNK)
            for h in range(RET_HEADS):
                sin_save_ref[c, h] = s_heads[h]
            v_heads = tuple(p_ref[rows, 2 * RET_QK + RET_DV * h:2 * RET_QK + RET_DV * (h + 1)] for h in range(RET_HEADS))
            outs, s_heads = _ret_chunk(p_ref[rows, 0:RET_QK], p_ref[rows, RET_QK:2 * RET_QK], v_heads, s_heads,
                                       cos_ref[rows, :], sin_ref[rows, :], *consts, chunk_dec)
            for h in range(RET_HEADS):
                out_ref[rows, RET_DV * h:RET_DV * (h + 1)] = outs[h]
        for h in range(RET_HEADS):
            s_scr[h] = s_heads[h]

    tok = CHUNK * G
    return pl.pallas_call(
        body, name="ret_fwd", grid=(ng,),
        in_specs=[pl.BlockSpec((tok, RET_COLS), lambda i: (i, 0)), _rows(tok, RET_QK), _rows(tok, RET_QK)] + _ret_specs(),
        out_specs=[_rows(tok, RET_WIDTH), pl.BlockSpec((G, RET_HEADS, RET_QK, RET_DV), lambda i: (i, 0, 0, 0))],
        out_shape=[jax.ShapeDtypeStruct((T, RET_WIDTH), F32),
                   jax.ShapeDtypeStruct((T // CHUNK, RET_HEADS, RET_QK, RET_DV), F32)],
        scratch_shapes=[pltpu.VMEM((RET_HEADS, RET_QK, RET_DV), F32)],
        compiler_params=_params(dimension_semantics=("arbitrary",)),
    )(p_ret, cos_t, sin_t, dec, hm, qd, kd)


def _ret_bwd(p_ret, s_saved, d_ret, tabs):
    T = p_ret.shape[0]
    G = RET_GROUP
    ng = T // (CHUNK * G)
    cos_t, sin_t, dec, hm, qd, kd, chunk_dec = tabs

    def body(p_ref, s_ref, dret_ref, cos_ref, sin_ref, dec_ref, hm_ref, qd_ref, kd_ref, dp_ref, ds_scr):
        @pl.when(pl.program_id(0) == 0)
        def _():
            ds_scr[...] = jnp.zeros_like(ds_scr)

        consts = (dec_ref[...], hm_ref[...], qd_ref[...], kd_ref[...])
        d_s = tuple(ds_scr[h] for h in range(RET_HEADS))
        for c in reversed(range(G)):
            rows = pl.ds(c * CHUNK, CHUNK)
            v_heads = tuple(p_ref[rows, 2 * RET_QK + RET_DV * h:2 * RET_QK + RET_DV * (h + 1)] for h in range(RET_HEADS))
            s_heads = tuple(s_ref[c, h] for h in range(RET_HEADS))
            tables = (cos_ref[rows, :], sin_ref[rows, :]) + consts
            _, vjp = jax.vjp(lambda a, b, c_, d: _ret_chunk(a, b, c_, d, *tables, chunk_dec),
                             p_ref[rows, 0:RET_QK], p_ref[rows, RET_QK:2 * RET_QK], v_heads, s_heads)
            d_out = tuple(dret_ref[rows, RET_DV * h:RET_DV * (h + 1)] for h in range(RET_HEADS))
            dq, dk, dv, d_s = vjp((d_out, d_s))
            dp_ref[rows, 0:RET_QK] = dq
            dp_ref[rows, RET_QK:2 * RET_QK] = dk
            for h in range(RET_HEADS):
                dp_ref[rows, 2 * RET_QK + RET_DV * h:2 * RET_QK + RET_DV * (h + 1)] = dv[h]
        for h in range(RET_HEADS):
            ds_scr[h] = d_s[h]

    tok = CHUNK * G
    rev = lambda i: (ng - 1 - i, 0)
    return pl.pallas_call(
        body, name="ret_bwd", grid=(ng,),
        in_specs=[pl.BlockSpec((tok, RET_COLS), rev),
                  pl.BlockSpec((G, RET_HEADS, RET_QK, RET_DV), lambda i: (ng - 1 - i, 0, 0, 0)),
                  pl.BlockSpec((tok, RET_WIDTH), rev), pl.BlockSpec((tok, RET_QK), rev), pl.BlockSpec((tok, RET_QK), rev)]
        + _ret_specs(),
        out_specs=pl.BlockSpec((tok, 2 * RET_QK + RET_WIDTH), rev),
        out_shape=jax.ShapeDtypeStruct((T, 2 * RET_QK + RET_WIDTH), F32),
        scratch_shapes=[pltpu.VMEM((RET_HEADS, RET_QK, RET_DV), F32)],
        compiler_params=_params(dimension_semantics=("arbitrary",)),
    )(p_ret, s_saved, d_ret, cos_t, sin_t, dec, hm, qd, kd)


def _wkv_consts():
    lane = lax.broadcasted_iota(jnp.int32, (RWKV_HEAD, RWKV_WIDTH), 1)
    sub = lax.broadcasted_iota(jnp.int32, (RWKV_HEAD, RWKV_WIDTH), 0)
    diag = ((lane & (RWKV_HEAD - 1)) == sub).astype(F32)
    r = lax.broadcasted_iota(jnp.int32, (3 * 128, 128), 0)
    c = lax.broadcasted_iota(jnp.int32, (3 * 128, 128), 1)
    ones = (((r & 127) >> 6) == (c >> 6)).astype(BF16)
    return diag, ones


def _stack(x):
    return jnp.concatenate([x[:, 128 * p:128 * (p + 1)] for p in range(4)], axis=0)


def _unstack(y):
    n = y.shape[0] // 4
    return jnp.concatenate([y[n * p:n * (p + 1)] for p in range(4)], axis=1)


def _split(x, n):
    pieces = []
    for _ in range(n):
        p = x.astype(BF16)
        pieces.append(p)
        x = x - p.astype(F32)
    return pieces


def _lane_sum(x, ones):
    return _unstack(jnp.dot(_stack(x).astype(BF16), ones[:128], preferred_element_type=F32))


def _colsum(x):
    return jnp.sum(x, axis=0, keepdims=True)


def _rows_times(vec, mat):
    lane = lax.broadcasted_iota(jnp.int32, (1, 128), 1)
    tiles = []
    for p in range(4):
        lhs = jnp.concatenate([vec[:, 128 * p:128 * p + RWKV_HEAD], vec[:, 128 * p + RWKV_HEAD:128 * (p + 1)]], axis=0)
        out = jnp.dot(lhs, mat[:, 128 * p:128 * (p + 1)], preferred_element_type=F32)
        tiles.append(jnp.where(lane < RWKV_HEAD, out[0:1], out[1:2]))
    return jnp.concatenate(tiles, axis=1)


def _expand_cols(xt, t):
    lane = lax.broadcasted_iota(jnp.int32, (RWKV_HEAD, 128), 1)
    tiles = []
    for p in range(4):
        lo = jnp.broadcast_to(xt[128 * p:128 * p + RWKV_HEAD, t:t + 1], (RWKV_HEAD, 128))
        hi = jnp.broadcast_to(xt[128 * p + RWKV_HEAD:128 * (p + 1), t:t + 1], (RWKV_HEAD, 128))
        tiles.append(jnp.where(lane < RWKV_HEAD, lo, hi))
    return jnp.concatenate(tiles, axis=1)


def _head_sums(x, ones):
    return _unstack(jnp.dot(jnp.concatenate(_split(_stack(x), 3), axis=1), ones, preferred_element_type=F32))


def _wkv_fwd(r, w, k, v, kap, a):
    T = r.shape[0]
    C = WKV_CHUNK
    nc = T // C

    def body(r_ref, w_ref, k_ref, v_ref, kap_ref, a_ref, o_ref, s_all_ref, sa_all_ref, s_scr):
        @pl.when(pl.program_id(0) == 0)
        def _():
            s_scr[...] = jnp.zeros_like(s_scr)

        diag, ones = _wkv_consts()
        rr, ww, kk, vv, kap_, aa = (ref[...] for ref in (r_ref, w_ref, k_ref, v_ref, kap_ref, a_ref))
        bb = kap_ * aa
        c1 = _head_sums(pltpu.roll(bb, 1, axis=0) * kap_, ones)
        row = lambda x, t: x[t:t + 1]

        v_cols = vv.T

        s_prev = s_scr[...]
        sa = _lane_sum(s_prev * (-row(kap_, 0)), ones)
        ls, rows = None, []

        def emit_o(t, s_t):
            rows.append(_colsum(_lane_sum(s_t * row(rr, t), ones) * diag))
            if t % 8 == 7:
                o_ref[t - 7:t + 1, :] = jnp.concatenate(rows, axis=0)
                rows.clear()

        for t in range(C):
            u = s_prev * row(ww, t) + _expand_cols(v_cols, t) * row(kk, t)
            if t > 0:
                sa = ls - sa * row(c1, t)
            if t + 1 < C:
                ls = _lane_sum(u * (-row(kap_, t + 1)), ones)
            if t > 0:
                emit_o(t - 1, s_prev)
            s_prev = u + sa * row(bb, t)
            s_all_ref[t] = s_prev
            sa_all_ref[t] = sa.astype(BF16)
        emit_o(C - 1, s_prev)
        s_scr[...] = s_prev

    spec = _rows(C, RWKV_WIDTH)
    return pl.pallas_call(
        body, name="wkv_fwd", grid=(nc,),
        in_specs=[spec] * 6,
        out_specs=[spec, pl.BlockSpec((C, RWKV_HEAD, RWKV_WIDTH), lambda i: (i, 0, 0)),
                   pl.BlockSpec((C, RWKV_HEAD, RWKV_WIDTH), lambda i: (i, 0, 0))],
        out_shape=[jax.ShapeDtypeStruct((T, RWKV_WIDTH), F32), jax.ShapeDtypeStruct((T, RWKV_HEAD, RWKV_WIDTH), F32),
                   jax.ShapeDtypeStruct((T, RWKV_HEAD, RWKV_WIDTH), BF16)],
        scratch_shapes=[pltpu.VMEM((RWKV_HEAD, RWKV_WIDTH), F32)],
        compiler_params=_params(dimension_semantics=("arbitrary",)),
    )(r, w, k, v, kap, a)


def _wkv_bwd(r, w, k, v, kap, a, s_all, sa_all, d_o):
    T = r.shape[0]
    C = WKV_CHUNK
    nc = T // C

    def body(r_ref, w_ref, k_ref, v_ref, kap_ref, a_ref, s_ref, s_before_ref, sa_ref, do_ref,
             dr_ref, dw_ref, dk_ref, dv_ref, dkap_ref, da_ref, ds_scr):
        first_chunk = pl.program_id(0) == nc - 1

        @pl.when(pl.program_id(0) == 0)
        def _():
            ds_scr[...] = jnp.zeros_like(ds_scr)

        diag, ones = _wkv_consts()
        rr, ww, kk, vv, kap_, aa, dd = (ref[...] for ref in (r_ref, w_ref, k_ref, v_ref, kap_ref, a_ref, do_ref))
        bb = kap_ * aa
        e1 = _head_sums(pltpu.roll(kap_, C - 1, axis=0) * bb, ones)
        row = lambda x, t: x[t:t + 1]

        def state_before(t):
            return s_ref[t - 1] if t > 0 else jnp.where(first_chunk, 0.0, s_before_ref[0])

        do_cols = dd.T

        d_sn, dsa, rows = None, None, [None] * C

        def emit_rows(t, d_sn_t, dsa_t):
            s_prev, dof = state_before(t), _expand_cols(do_cols, t)
            dv = _colsum(_lane_sum(d_sn_t * row(kk, t), ones) * diag)
            db = _colsum(d_sn_t * sa_ref[t].astype(F32))
            rows[t] = (_colsum(s_ref[t] * dof), _colsum(d_sn_t * s_prev), _rows_times(row(vv, t), d_sn_t), dv,
                       db * row(aa, t) - _colsum(dsa_t * s_prev), db * row(kap_, t))
            if t % 8 == 0:
                for j, ref in enumerate((dr_ref, dw_ref, dk_ref, dv_ref, dkap_ref, da_ref)):
                    ref[t:t + 8, :] = jnp.concatenate([rows[u][j] for u in range(t, t + 8)], axis=0)

        for t in reversed(range(C)):
            dof = _expand_cols(do_cols, t)
            if t == C - 1:
                d_sn = ds_scr[...] + dof * row(rr, t)
                dsa = _lane_sum(d_sn * row(bb, t), ones)
            else:
                v_t = d_sn * row(ww, t + 1) + dof * row(rr, t)
                ls = _lane_sum(v_t * row(bb, t), ones)
                emit_rows(t + 1, d_sn, dsa)
                d_sn = v_t - dsa * row(kap_, t + 1)
                dsa = ls - dsa * row(e1, t)
        emit_rows(0, d_sn, dsa)
        d_s = d_sn * row(ww, 0) - dsa * row(kap_, 0)
        ds_scr[...] = d_s

    spec = pl.BlockSpec((C, RWKV_WIDTH), lambda i: (nc - 1 - i, 0))
    states = pl.BlockSpec((C, RWKV_HEAD, RWKV_WIDTH), lambda i: (nc - 1 - i, 0, 0))
    before = pl.BlockSpec((1, RWKV_HEAD, RWKV_WIDTH), lambda i: (jnp.maximum((nc - 1 - i) * C - 1, 0), 0, 0))
    return pl.pallas_call(
        body, name="wkv_bwd", grid=(nc,),
        in_specs=[spec] * 6 + [states, before, states, spec],
        out_specs=[spec] * 6,
        out_shape=[jax.ShapeDtypeStruct((T, RWKV_WIDTH), F32)] * 6,
        scratch_shapes=[pltpu.VMEM((RWKV_HEAD, RWKV_WIDTH), F32)],
        compiler_params=_params(dimension_semantics=("arbitrary",)),
    )(r, w, k, v, kap, a, s_all, s_all, sa_all, d_o)


W = RWKV_WIDTH


def _softplus(y):
    return jnp.maximum(y, 0.0) + jnp.log(1.0 + jnp.exp(-jnp.abs(y)))


def _prep_fn(kr, xwa, w0, a0, k_k, k_a, wup_pad, aup_pad, ones64):
    w_log = -_softplus(-(w0 + _mm_nt(jnp.tanh(xwa), wup_pad))) - 0.5
    decay = jnp.exp(-jnp.exp(w_log))
    a = jax.nn.sigmoid(a0 + _mm_nt(xwa, aup_pad))
    kk = kr * k_k
    kap = kk / jnp.maximum(jnp.sqrt(_head_mix(kk * kk, ones64)), 1e-12)
    k = kr * (1.0 + (a - 1.0) * k_a)
    return decay, k, kap, a


def _shift_down(p, first_row):
    rows = lax.broadcasted_iota(jnp.int32, p.shape, 0)
    return jnp.where(rows == 0, first_row, pltpu.roll(p, 1, axis=0))


def _shift_up(z, last_row):
    n = z.shape[0]
    rows = lax.broadcasted_iota(jnp.int32, z.shape, 0)
    return jnp.where(rows == n - 1, last_row, pltpu.roll(z, n - 1, axis=0))


def _prev_block_spec():
    return pl.BlockSpec((8, RWKV_COLS), lambda i: (jnp.maximum(i * (TOK_TILE // 8) - 1, 0), 0))


def _mixed(p_ref, prev8_ref, mu_ref, first_tile):
    p = p_ref[...]
    first_row = jnp.where(first_tile, 0.0, prev8_ref[7:8, :])
    prev = _shift_down(p, first_row)
    return p, prev, p + mu_ref[...] * (prev - p)


def _prep_fwd(p_rwkv, mu, w0, a0, k_k, k_a, wup_pad, aup_pad, ones64):
    T = p_rwkv.shape[0]

    def body(p_ref, prev8_ref, mu_ref, w0_ref, a0_ref, kk_ref, ka_ref, wup_ref, aup_ref, ones_ref,
             r_ref, w_ref, k_ref, v_ref, kap_ref, a_ref, g_ref):
        _, _, ps = _mixed(p_ref, prev8_ref, mu_ref, pl.program_id(0) == 0)
        decay, k, kap, a = _prep_fn(ps[:, W:2 * W], ps[:, 4 * W:], w0_ref[...], a0_ref[...], kk_ref[...], ka_ref[...],
                                    wup_ref[...], aup_ref[...], ones_ref[...])
        r_ref[...] = ps[:, 0:W]
        w_ref[...] = decay
        k_ref[...] = k
        v_ref[...] = ps[:, 2 * W:3 * W]
        kap_ref[...] = kap
        a_ref[...] = a
        g_ref[...] = ps[:, 3 * W:4 * W]

    vec = _full((1, W))
    return pl.pallas_call(
        body, name="prep_fwd", grid=(T // TOK_TILE,),
        in_specs=[_rows(TOK_TILE, RWKV_COLS), _prev_block_spec(), _full((1, RWKV_COLS)), vec, vec, vec, vec,
                  _full((W, 2 * LORA)), _full((W, 2 * LORA)), _full((256, 128))],
        out_specs=[_rows(TOK_TILE, W)] * 7,
        out_shape=[jax.ShapeDtypeStruct((T, W), F32)] * 7,
        compiler_params=_params(dimension_semantics=("arbitrary",)),
    )(p_rwkv, p_rwkv, mu, w0, a0, k_k, k_a, wup_pad, aup_pad, ones64)


def _prep_bwd(p_rwkv, mu, w0, a0, k_k, k_a, wup_pad, aup_pad, ones64, dr, dw, dk, dv, dkap, da, dg, dr2, dk2, dv2):
    T = p_rwkv.shape[0]
    nt = T // TOK_TILE

    def body(p_ref, prev8_ref, mu_ref, w0_ref, a0_ref, kk_ref, ka_ref, wup_ref, aup_ref, ones_ref,
             dr_ref, dw_ref, dk_ref, dv_ref, dkap_ref, da_ref, dg_ref, dr2_ref, dk2_ref, dv2_ref,
             dp_ref, dmu_ref, dw0_ref, da0_ref, dkk_ref, dka_ref, dwup_ref, daup_ref, zrow_scr):
        i = pl.program_id(0)
        accs = (dmu_ref, dw0_ref, da0_ref, dkk_ref, dka_ref, dwup_ref, daup_ref)

        @pl.when(i == 0)
        def _():
            zrow_scr[...] = jnp.zeros_like(zrow_scr)
            for ref in accs:
                ref[...] = jnp.zeros_like(ref)

        p, prev, ps = _mixed(p_ref, prev8_ref, mu_ref, i == nt - 1)
        ones = ones_ref[...]
        _, vjp = jax.vjp(lambda *args: _prep_fn(*args, ones), ps[:, W:2 * W], ps[:, 4 * W:], w0_ref[...], a0_ref[...],
                         kk_ref[...], ka_ref[...], wup_ref[...], aup_ref[...])
        dkr, dxwa, dw0, da0, dkk, dka, dwup, daup = vjp(
            (dw_ref[...], dk_ref[...] + dk2_ref[...], dkap_ref[...], da_ref[...]))
        dps = jnp.concatenate([dr_ref[...] + dr2_ref[...], dkr, dv_ref[...] + dv2_ref[...], dg_ref[...], dxwa], axis=1)
        z = dps * mu_ref[...]
        dp_ref[...] = dps - z + _shift_up(z, zrow_scr[0:1, :])
        zrow_scr[0:1, :] = z[0:1, :]
        for ref, val in zip(accs, (_colsum(dps * (prev - p)), dw0, da0, dkk, dka, dwup, daup)):
            ref[...] += val

    rev = lambda i: (nt - 1 - i, 0)
    vec = _full((1, W))
    lora = _full((W, 2 * LORA))
    tile = pl.BlockSpec((TOK_TILE, W), rev)
    prev8 = pl.BlockSpec((8, RWKV_COLS), lambda i: (jnp.maximum((nt - 1 - i) * (TOK_TILE // 8) - 1, 0), 0))
    return pl.pallas_call(
        body, name="prep_bwd", grid=(nt,),
        in_specs=[pl.BlockSpec((TOK_TILE, RWKV_COLS), rev), prev8, _full((1, RWKV_COLS)), vec, vec, vec, vec, lora, lora,
                  _full((256, 128))] + [tile] * 10,
        out_specs=[pl.BlockSpec((TOK_TILE, RWKV_COLS), rev), _full((1, RWKV_COLS)), vec, vec, vec, vec, lora, lora],
        out_shape=[jax.ShapeDtypeStruct((T, RWKV_COLS), F32), jax.ShapeDtypeStruct((1, RWKV_COLS), F32)]
        + [jax.ShapeDtypeStruct((1, W), F32)] * 4 + [jax.ShapeDtypeStruct((W, 2 * LORA), F32)] * 2,
        scratch_shapes=[pltpu.VMEM((8, RWKV_COLS), F32)],
        compiler_params=_params(dimension_semantics=("arbitrary",)),
    )(p_rwkv, p_rwkv, mu, w0, a0, k_k, k_a, wup_pad, aup_pad, ones64, dr, dw, dk, dv, dkap, da, dg, dr2, dk2, dv2)


def _silu(x):
    return x * jax.nn.sigmoid(x)


def _post_y(o, r, k, v, g_rw, ret_raw, g_ret, ret_gn_g, gn_g, gn_b, r_k, avg128, avg64, ones64):
    xc = ret_raw - _head_mix(ret_raw, avg128)
    ret = xc * lax.rsqrt(_head_mix(xc * xc, avg128) + RET_GN_EPS)
    y_ret = _silu(g_ret) * (ret * ret_gn_g)
    oc = o - _head_mix(o, avg64)
    on = oc * lax.rsqrt(_head_mix(oc * oc, avg64) + RWKV_GN_EPS) * gn_g + gn_b
    bonus = _head_mix(r * k * r_k, ones64) * v
    y_rwkv = _silu(g_rw) * (on + bonus)
    return y_ret, y_rwkv


def _post_loss(h, final_g, target):
    err = _rmsnorm(h, final_g) - target
    return 0.5 * jnp.sum(jnp.mean(err * err, axis=-1))


def _post(o, r, k, v, g_rw, ret_raw, p_ret, x, target, ret_gn_g, gn_g, gn_b, r_k, final_g, w_out, avg128, avg64, ones64):
    T = x.shape[1]
    n_tok_out = 8

    def body(o_ref, r_ref, k_ref, v_ref, grw_ref, ret_ref, gret_ref, x_ref, tgt_ref, rg_ref, gg_ref, gb_ref, rk_ref, fg_ref,
             wo_ref, a128_ref, a64_ref, ones_ref, *outs):
        tok_outs, (dwo_ref, drg_ref, dgg_ref, dgb_ref, drk_ref, dfg_ref, loss_ref) = outs[:n_tok_out], outs[n_tok_out:]
        accs = (dwo_ref, drg_ref, dgg_ref, dgb_ref, drk_ref, dfg_ref, loss_ref)

        @pl.when(pl.program_id(0) == 0)
        def _():
            for ref in accs:
                ref[...] = jnp.zeros_like(ref)

        consts = (a128_ref[...], a64_ref[...], ones_ref[...])
        (y_ret, y_rwkv), vjp = jax.vjp(
            lambda *args: _post_y(*args, *consts), o_ref[...], r_ref[...], k_ref[...], v_ref[...], grw_ref[...], ret_ref[...],
            gret_ref[...], rg_ref[...], gg_ref[...], gb_ref[...], rk_ref[...])
        h = x_ref[...] + _dot_bf(y_ret, wo_ref[0:RET_WIDTH, :]) + _dot_bf(y_rwkv, wo_ref[RET_WIDTH:, :])
        loss, (dh, dfg) = jax.value_and_grad(_post_loss, argnums=(0, 1))(h, fg_ref[...], tgt_ref[...])
        dy_ret = _dot_nt_bf(dh, wo_ref[0:RET_WIDTH, :])
        dy_rwkv = _dot_nt_bf(dh, wo_ref[RET_WIDTH:, :])
        do, dr, dk, dv, dgrw, dret, dgret, drg, dgg, dgb, drk = vjp((dy_ret, dy_rwkv))
        for ref, val in zip(tok_outs, (dh, do, dr, dk, dv, dgrw, dret, dgret)):
            ref[...] = val
        dwo_ref[0:RET_WIDTH, :] += _dot_tn_bf(y_ret, dh)
        dwo_ref[RET_WIDTH:, :] += _dot_tn_bf(y_rwkv, dh)
        for ref, val in zip(accs[1:], (drg, dgg, dgb, drk, dfg, jnp.full((1, 128), loss, F32))):
            ref[...] += val

    tile = _rows(TOK_TILE, W)
    wide = _rows(TOK_TILE, D_MODEL)
    wide_of_one = _rows_of_one(TOK_TILE, D_MODEL)
    vec = _full((1, W))
    sq = _full((256, 128))
    return pl.pallas_call(
        body, name="post", grid=(T // TOK_TILE,),
        in_specs=[tile] * 6 + [pl.BlockSpec((TOK_TILE, W), lambda i: (i, 2)), wide_of_one, wide_of_one, vec, vec, vec, vec,
                               _full((1, D_MODEL)), _full((D_MODEL, D_MODEL)), sq, sq, sq],
        out_specs=[wide] + [tile] * 7 + [_full((D_MODEL, D_MODEL)), vec, vec, vec, vec, _full((1, D_MODEL)), _full((1, 128))],
        out_shape=[jax.ShapeDtypeStruct((T, D_MODEL), F32)] + [jax.ShapeDtypeStruct((T, W), F32)] * 7
        + [jax.ShapeDtypeStruct((D_MODEL, D_MODEL), F32)] + [jax.ShapeDtypeStruct((1, W), F32)] * 4
        + [jax.ShapeDtypeStruct((1, D_MODEL), F32), jax.ShapeDtypeStruct((1, 128), F32)],
        compiler_params=_params(dimension_semantics=("arbitrary",)),
    )(o, r, k, v, g_rw, ret_raw, p_ret, x, target, ret_gn_g, gn_g, gn_b, r_k, final_g, w_out, avg128, avg64, ones64)


def _inproj_bwd_x(x, norm_g, dp_qkv, dg_ret, dp_rwkv, dh, w_in_t):
    T = x.shape[1]
    n_qkv = 2 * RET_QK + RET_WIDTH

    def body(x_ref, g_ref, dqkv_ref, dgret_ref, drwkv_ref, dh_ref, w_ref, dx_ref, dg_ref):
        @pl.when(pl.program_id(0) == 0)
        def _():
            dg_ref[...] = jnp.zeros_like(dg_ref)

        _, vjp = jax.vjp(_rmsnorm, x_ref[...], g_ref[...])
        du = (_dot_bf(dqkv_ref[...], w_ref[0:n_qkv, :]) + _dot_bf(dgret_ref[...], w_ref[n_qkv:RET_COLS, :])
              + _dot_bf(drwkv_ref[...], w_ref[RET_COLS:, :]))
        dx, dg = vjp(du)
        dx_ref[...] = dx + dh_ref[...]
        dg_ref[...] += dg

    return pl.pallas_call(
        body, name="inproj_bwd_x", grid=(T // TOK_TILE,),
        in_specs=[_rows_of_one(TOK_TILE, D_MODEL), _full((1, D_MODEL)), _rows(TOK_TILE, n_qkv), _rows(TOK_TILE, RET_WIDTH),
                  _rows(TOK_TILE, RWKV_COLS), _rows(TOK_TILE, D_MODEL), _full((IN_COLS, D_MODEL))],
        out_specs=[_rows_of_one(TOK_TILE, D_MODEL), _full((1, D_MODEL))],
        out_shape=[jax.ShapeDtypeStruct((1, T, D_MODEL), F32), jax.ShapeDtypeStruct((1, D_MODEL), F32)],
        compiler_params=_params(dimension_semantics=("arbitrary",)),
    )(x, norm_g, dp_qkv, dg_ret, dp_rwkv, dh, w_in_t)


def _grad_w(name, u, dp):
    T, n = dp.shape
    tile = 2 * TOK_TILE
    steps = T // tile

    def body(u_ref, dp_ref, out_ref, acc_ref):
        @pl.when(pl.program_id(0) == 0)
        def _():
            acc_ref[...] = jnp.zeros_like(acc_ref)

        acc_ref[...] += _dot_tn_bf(dp_ref[...], u_ref[...])

        @pl.when(pl.program_id(0) == steps - 1)
        def _():
            out_ref[...] = acc_ref[...].astype(BF16)

    return pl.pallas_call(
        body, name=name, grid=(steps,),
        in_specs=[_rows(tile, D_MODEL), _rows(tile, n)],
        out_specs=_full((n, D_MODEL)),
        out_shape=jax.ShapeDtypeStruct((n, D_MODEL), BF16),
        scratch_shapes=[pltpu.VMEM((n, D_MODEL), F32)],
        compiler_params=_params(dimension_semantics=("arbitrary",)),
    )(u, dp)


def _pad_lora(w_up_t, first):
    z = jnp.zeros_like(w_up_t)
    return jnp.concatenate([w_up_t, z] if first else [z, w_up_t], axis=1)


def _local_grads(x, target, norm_g, w_in_t, ret_gn_g, mu, w_lora_up_t, w0, a_lora_up_t, a0, k_k, k_a, r_k, gn_g, gn_b,
                 w_out_bf, final_g):
    T = x.shape[1]
    tabs = _rope_tables(T) + _ret_tables()
    ones64 = _block_mix(128, RWKV_HEAD)
    avg64 = _block_mix(128, RWKV_HEAD, 1.0 / RWKV_HEAD)
    avg128 = _block_mix(128, RET_DV, 1.0 / RET_DV)
    wup_pad, aup_pad = _pad_lora(w_lora_up_t, True), _pad_lora(a_lora_up_t, False)

    p_ret, p_rwkv, u = _inproj(x, norm_g, w_in_t)
    ret_raw, s_saved = _ret_fwd(p_ret, tabs)
    r, w, k, v, kap, a, g_rw = _prep_fwd(p_rwkv, mu, w0, a0, k_k, k_a, wup_pad, aup_pad, ones64)
    o, s_all, sa_all = _wkv_fwd(r, w, k, v, kap, a)
    (dh, do, dr2, dk2, dv2, dgrw, dret, dgret, d_w_out, d_ret_gn_g, d_gn_g, d_gn_b, d_r_k, d_final_g, loss) = _post(
        o, r, k, v, g_rw, ret_raw, p_ret, x, target, ret_gn_g, gn_g, gn_b, r_k, final_g, w_out_bf, avg128, avg64, ones64)
    dr, dw, dk, dv, dkap, da = _wkv_bwd(r, w, k, v, kap, a, s_all, sa_all, do)
    dp_rwkv, d_mu, d_w0, d_a0, d_k_k, d_k_a, d_wup, d_aup = _prep_bwd(
        p_rwkv, mu, w0, a0, k_k, k_a, wup_pad, aup_pad, ones64, dr, dw, dk, dv, dkap, da, dgrw, dr2, dk2, dv2)
    dp_qkv = _ret_bwd(p_ret, s_saved, dret, tabs)
    dx, d_norm_g = _inproj_bwd_x(x, norm_g, dp_qkv, dgret, dp_rwkv, dh, w_in_t)
    d_w_in = jnp.concatenate([_grad_w("grad_w_qkv", u, dp_qkv), _grad_w("grad_w_gret", u, dgret),
                              _grad_w("grad_w_rwkv", u, dp_rwkv)], axis=0)
    grads = dict(norm_g=d_norm_g, w_in=d_w_in, ret_gn_g=d_ret_gn_g, rwkv_mu=d_mu, w_lora_up=d_wup[:, :LORA], w0=d_w0,
                 a_lora_up=d_aup[:, LORA:], a0=d_a0, k_k=d_k_k, k_a=d_k_a, r_k=d_r_k, rwkv_gn_g=d_gn_g, rwkv_gn_b=d_gn_b,
                 w_out=d_w_out, final_norm_g=d_final_g)
    return loss, dx, grads


def _mesh_pos():
    return lax.axis_index("x"), lax.axis_index("y"), lax.axis_index("c")


def _all_gather(shards):
    n = len(shards)

    def body(*refs):
        x_refs, out_refs = refs[:n], refs[n:2 * n]
        send_sems, recv_sems, local_sems = refs[2 * n:]
        x, y, c = _mesh_pos()
        me, sibling = (x, y, c), (x, y, 1 - c)
        chips = [(1 - x, y), (x, 1 - y), (1 - x, 1 - y)]

        def rows(a, pos):
            m = x_refs[a].shape[0]
            return out_refs[a].at[pl.ds((4 * pos[0] + 2 * pos[1] + pos[2]) * m, m), :]

        def copy(a, k, block, to, src=None):
            return pltpu.make_async_remote_copy(
                src_ref=rows(a, block) if src is None else src, dst_ref=rows(a, block),
                send_sem=send_sems.at[a, k], recv_sem=recv_sems.at[a, k], device_id=to, device_id_type=MESH)

        mine = [pltpu.make_async_copy(x_refs[a], rows(a, me), local_sems.at[a]) for a in range(n)]
        for cp in mine:
            cp.start()
        first = []
        for a in range(n):
            first.append(copy(a, 0, me, sibling, src=x_refs[a]))
            first += [copy(a, 1 + j, me, (*chip, c), src=x_refs[a]) for j, chip in enumerate(chips)]
        for cp in first:
            cp.start()
        passed = []
        for j, chip in enumerate(chips):
            for a in range(n):
                copy(a, 1 + j, (*chip, c), me).wait_recv()
                passed.append(copy(a, 4 + j, (*chip, c), sibling))
                passed[-1].start()
        for a in range(n):
            copy(a, 0, sibling, me).wait_recv()
            for j, chip in enumerate(chips):
                copy(a, 4 + j, (*chip, 1 - c), me).wait_recv()
        for cp in first + passed:
            cp.wait_send()
        for cp in mine:
            cp.wait()

    vmem = pl.BlockSpec(memory_space=pltpu.VMEM)
    return pl.pallas_call(
        body, name="gather_weights",
        out_shape=[jax.ShapeDtypeStruct((N_DEV * s.shape[0], s.shape[1]), s.dtype) for s in shards],
        in_specs=[vmem] * n, out_specs=[vmem] * n,
        scratch_shapes=[pltpu.SemaphoreType.DMA((n, 7)), pltpu.SemaphoreType.DMA((n, 7)), pltpu.SemaphoreType.DMA((n,))],
        compiler_params=_params(),
    )(*shards)


N_CHIP = 4


def _exchange_pairs(big, small):
    nb, ns = len(big), len(small)

    def body(*refs):
        big_in, small_in = refs[:nb], refs[nb:nb + ns]
        theirs, small_out = refs[nb + ns:2 * nb + ns], refs[2 * nb + ns:2 * nb + 2 * ns]
        pair_send, pair_recv, send_sems, recv_sems, local_sems = refs[2 * nb + 2 * ns:]
        x, y, c = _mesh_pos()
        me = 4 * x + 2 * y + c
        local = [pltpu.make_async_copy(small_in[a].at[me], small_out[a].at[me], local_sems.at[a]) for a in range(ns)]
        for cp in local:
            cp.start()
        copies = [pltpu.make_async_remote_copy(
            src_ref=big_in[a], dst_ref=theirs[a], send_sem=pair_send.at[a], recv_sem=pair_recv.at[a],
            device_id=(x, y, 1 - c), device_id_type=MESH) for a in range(nb)]
        for k in range(1, N_DEV):
            peer = (x ^ (k >> 2), y ^ ((k >> 1) & 1), c ^ (k & 1))
            peer_idx = 4 * peer[0] + 2 * peer[1] + peer[2]
            copies += [pltpu.make_async_remote_copy(
                src_ref=small_in[a].at[peer_idx], dst_ref=small_out[a].at[me], send_sem=send_sems.at[a, k - 1],
                recv_sem=recv_sems.at[a, k - 1], device_id=peer, device_id_type=MESH) for a in range(ns)]
        for cp in copies:
            cp.start()
        for cp in copies:
            cp.wait()
        for cp in local:
            cp.wait()

    hbm = pl.BlockSpec(memory_space=pl.ANY)
    out_shape = [jax.ShapeDtypeStruct(p.shape, p.dtype) for p in big + small]
    dma = pltpu.SemaphoreType.DMA
    res = pl.pallas_call(
        body, name="exchange_pairs", out_shape=out_shape,
        in_specs=[hbm] * (nb + ns), out_specs=[hbm] * len(out_shape),
        scratch_shapes=[dma((nb,)), dma((nb,)), dma((ns, 7)), dma((ns, 7)), dma((ns,))],
        compiler_params=_params(),
    )(*big, *small)
    return res[:nb], res[nb:]


def _pair_sum(name, mine, theirs, row_tile):
    _, rows, cols = mine.shape

    def body(a_ref, b_ref, o_ref):
        o_ref[...] = (a_ref[...].astype(F32) + b_ref[...].astype(F32)).astype(o_ref.dtype)

    spec = pl.BlockSpec((N_CHIP, row_tile, cols), lambda i: (0, i, 0))
    return pl.pallas_call(
        body, name=name, grid=(rows // row_tile,), in_specs=[spec, spec], out_specs=spec,
        out_shape=jax.ShapeDtypeStruct(mine.shape, mine.dtype),
        compiler_params=_params(dimension_semantics=("arbitrary",)),
    )(mine, theirs)


def _exchange_chips(parts):
    n = len(parts)

    def body(*refs):
        in_refs, out_refs = refs[:n], refs[n:2 * n]
        send_sems, recv_sems, local_sems = refs[2 * n:]
        x, y, c = _mesh_pos()
        my_chip = 2 * x + y
        own = [pltpu.make_async_copy(in_refs[a].at[my_chip], out_refs[a].at[my_chip], local_sems.at[a]) for a in range(n)]
        for cp in own:
            cp.start()
        copies = []
        for k in range(1, N_CHIP):
            px, py = x ^ (k >> 1), y ^ (k & 1)
            copies += [pltpu.make_async_remote_copy(
                src_ref=in_refs[a].at[2 * px + py], dst_ref=out_refs[a].at[my_chip], send_sem=send_sems.at[a, k - 1],
                recv_sem=recv_sems.at[a, k - 1], device_id=(px, py, c), device_id_type=MESH) for a in range(n)]
        for cp in copies:
            cp.start()
        for cp in copies:
            cp.wait()
        for cp in own:
            cp.wait()

    hbm = pl.BlockSpec(memory_space=pl.ANY)
    dma = pltpu.SemaphoreType.DMA
    return pl.pallas_call(
        body, name="exchange_chips",
        out_shape=[jax.ShapeDtypeStruct(p.shape, p.dtype) for p in parts],
        in_specs=[hbm] * n, out_specs=[hbm] * n,
        scratch_shapes=[dma((n, N_CHIP - 1)), dma((n, N_CHIP - 1)), dma((n,))],
        compiler_params=_params(),
    )(*parts)


def _adamw(w, g, m, v):
    m = ADAM_B1 * m + (1.0 - ADAM_B1) * g
    v = ADAM_B2 * v + (1.0 - ADAM_B2) * (g * g)
    m_hat = m / (1.0 - ADAM_B1 ** ADAM_STEP)
    v_hat = v / (1.0 - ADAM_B2 ** ADAM_STEP)
    return -ADAM_LR * (m_hat / (jnp.sqrt(v_hat) + ADAM_EPS) + ADAM_WD * w), m, v


def _sum_parts(name, parts, row_tile):
    n_parts, rows, cols = parts.shape

    def body(p_ref, g_ref):
        g = p_ref[0].astype(F32)
        for s in range(1, n_parts):
            g = g + p_ref[s].astype(F32)
        g_ref[...] = g

    return pl.pallas_call(
        body, name=name, grid=(rows // row_tile,),
        in_specs=[pl.BlockSpec((n_parts, row_tile, cols), lambda i: (0, i, 0))],
        out_specs=pl.BlockSpec((row_tile, cols), lambda i: (i, 0)),
        out_shape=jax.ShapeDtypeStruct((rows, cols), F32),
        compiler_params=_params(dimension_semantics=("arbitrary",)),
    )(parts)


def _adamw_apply(name, g, w, m, v, row_tile):
    _, rows, cols = w.shape

    def body(g_ref, w_ref, m_ref, v_ref, d_ref, nm_ref, nv_ref):
        d_ref[0], nm_ref[0], nv_ref[0] = _adamw(w_ref[0], g_ref[0], m_ref[0], v_ref[0])

    tile = pl.BlockSpec((1, row_tile, cols), lambda i: (0, i, 0))
    return pl.pallas_call(
        body, name=name, grid=(rows // row_tile,), in_specs=[tile] * 4, out_specs=[tile] * 3,
        out_shape=[jax.ShapeDtypeStruct((1, rows, cols), F32)] * 3,
        compiler_params=_params(dimension_semantics=("arbitrary",)),
    )(g, w, m, v)


def _reduce_adamw_2d(name, parts, w, m, v, row_tile):
    n_parts, rows, cols = parts.shape

    def body(p_ref, w_ref, m_ref, v_ref, g_ref, d_ref, nm_ref, nv_ref):
        g = p_ref[0].astype(F32)
        for s in range(1, n_parts):
            g = g + p_ref[s].astype(F32)
        g_ref[...] = g
        d_ref[...], nm_ref[...], nv_ref[...] = _adamw(w_ref[...], g, m_ref[...], v_ref[...])

    tile = pl.BlockSpec((row_tile, cols), lambda i: (i, 0))
    return pl.pallas_call(
        body, name=name, grid=(rows // row_tile,),
        in_specs=[pl.BlockSpec((n_parts, row_tile, cols), lambda i: (0, i, 0)), tile, tile, tile],
        out_specs=[tile] * 4,
        out_shape=[jax.ShapeDtypeStruct((rows, cols), F32)] * 4,
        compiler_params=_params(dimension_semantics=("arbitrary",)),
    )(parts, w, m, v)


def _reduce_adamw_t(name, parts_t, w, m, v, sum_tile, row_tile):
    g = _sum_parts(name + "_sum", parts_t, sum_tile).T[None]
    return [g] + list(_adamw_apply(name, g, w, m, v, row_tile))


def _reduce_adamw(name, parts, w, m, v, row_tile):
    n_parts, rows, cols = parts.shape

    def body(p_ref, w_ref, m_ref, v_ref, g_ref, d_ref, nm_ref, nv_ref):
        g = p_ref[0].astype(F32)
        for s in range(1, n_parts):
            g = g + p_ref[s].astype(F32)
        g_ref[0] = g
        d_ref[0], nm_ref[0], nv_ref[0] = _adamw(w_ref[0], g, m_ref[0], v_ref[0])

    tile = pl.BlockSpec((1, row_tile, cols), lambda i: (0, i, 0))
    return pl.pallas_call(
        body, name=name, grid=(rows // row_tile,),
        in_specs=[pl.BlockSpec((n_parts, row_tile, cols), lambda i: (0, i, 0)), tile, tile, tile],
        out_specs=[tile] * 4,
        out_shape=[jax.ShapeDtypeStruct((1, rows, cols), F32)] * 4,
        compiler_params=_params(dimension_semantics=("arbitrary",)),
    )(parts, w, m, v)


_SMALL = (("norm_g", 1024), ("ret_gn_g", 512), ("rwkv_mu", 2176), ("w0", 512), ("a0", 512), ("k_k", 512), ("k_a", 512),
          ("r_k", 512), ("rwkv_gn_g", 512), ("rwkv_gn_b", 512), ("final_norm_g", 1024))
_SMALL_LANES = sum(n for _, n in _SMALL) + 128
_WEIGHTS = ("norm_g", "w_in", "ret_gn_g", "rwkv_mu", "w_lora_up", "w0", "a_lora_up", "a0", "k_k", "k_a", "r_k", "rwkv_gn_g",
            "rwkv_gn_b", "w_out", "final_norm_g")


def _adamw_vectors(parts, wts, mom, var):
    k = len(_SMALL)

    def body(p_ref, *refs):
        w_refs, m_refs, v_refs, outs = refs[:k], refs[k:2 * k], refs[2 * k:3 * k], refs[3 * k:]
        g_all = p_ref[0]
        for s in range(1, N_DEV):
            g_all = g_all + p_ref[s]
        off = 0
        for i, (name, n) in enumerate(_SMALL):
            g = g_all[:, off:off + n]
            off += n
            if name == "r_k":
                g = jnp.concatenate([g[:, RWKV_HEAD * h:RWKV_HEAD * (h + 1)] for h in range(RWKV_HEADS)], axis=0)[None]
            outs[4 * i][...] = g
            outs[4 * i + 1][...], outs[4 * i + 2][...], outs[4 * i + 3][...] = _adamw(
                w_refs[i][...], g, m_refs[i][...], v_refs[i][...])
        outs[4 * k][...] = g_all[:, off:off + 128]

    vmem = pl.BlockSpec(memory_space=pltpu.VMEM)
    shapes = [jax.ShapeDtypeStruct(wts[n].shape, F32) for n, _ in _SMALL for _ in range(4)] + [jax.ShapeDtypeStruct((1, 128), F32)]
    res = pl.pallas_call(
        body, name="adamw_vectors", out_shape=shapes,
        in_specs=[vmem] * (1 + 3 * k), out_specs=[vmem] * len(shapes), compiler_params=_params(),
    )(parts, *[wts[n] for n, _ in _SMALL], *[mom[n] for n, _ in _SMALL], *[var[n] for n, _ in _SMALL])
    return {n: res[4 * i:4 * i + 4] for i, (n, _) in enumerate(_SMALL)}, res[4 * k]


def kernel(x, norm_g, w_in, ret_gn_g, rwkv_mu, w_lora_up, w0, a_lora_up, a0, k_k, k_a, r_k, rwkv_gn_g, rwkv_gn_b, w_out, final_norm_g, loss_target, m_norm_g, m_w_in, m_ret_gn_g, m_rwkv_mu, m_w_lora_up, m_w0, m_a_lora_up, m_a0, m_k_k, m_k_a, m_r_k, m_rwkv_gn_g, m_rwkv_gn_b, m_w_out, m_final_norm_g, v_norm_g, v_w_in, v_ret_gn_g, v_rwkv_mu, v_w_lora_up, v_w0, v_a_lora_up, v_a0, v_k_k, v_k_a, v_r_k, v_rwkv_gn_g, v_rwkv_gn_b, v_w_out, v_final_norm_g):
    wts = dict(norm_g=norm_g, w_in=w_in, ret_gn_g=ret_gn_g, rwkv_mu=rwkv_mu, w_lora_up=w_lora_up, w0=w0, a_lora_up=a_lora_up,
               a0=a0, k_k=k_k, k_a=k_a, r_k=r_k, rwkv_gn_g=rwkv_gn_g, rwkv_gn_b=rwkv_gn_b, w_out=w_out,
               final_norm_g=final_norm_g)
    mom = dict(norm_g=m_norm_g, w_in=m_w_in, ret_gn_g=m_ret_gn_g, rwkv_mu=m_rwkv_mu, w_lora_up=m_w_lora_up, w0=m_w0,
               a_lora_up=m_a_lora_up, a0=m_a0, k_k=m_k_k, k_a=m_k_a, r_k=m_r_k, rwkv_gn_g=m_rwkv_gn_g,
               rwkv_gn_b=m_rwkv_gn_b, w_out=m_w_out, final_norm_g=m_final_norm_g)
    var = dict(norm_g=v_norm_g, w_in=v_w_in, ret_gn_g=v_ret_gn_g, rwkv_mu=v_rwkv_mu, w_lora_up=v_w_lora_up, w0=v_w0,
               a_lora_up=v_a_lora_up, a0=v_a0, k_k=v_k_k, k_a=v_k_a, r_k=v_r_k, rwkv_gn_g=v_rwkv_gn_g,
               rwkv_gn_b=v_rwkv_gn_b, w_out=v_w_out, final_norm_g=v_final_norm_g)
    shapes = {n: wts[n].shape for n in _WEIGHTS}

    w_in_t, w_out_bf, wup_t, aup_t = _all_gather(
        [w_in[0].T.astype(BF16), w_out[0].astype(BF16), w_lora_up[0].T, a_lora_up[0].T])

    loss, dx, g = _local_grads(
        x, loss_target, norm_g, w_in_t, ret_gn_g, rwkv_mu, wup_t, w0, aup_t, a0, k_k, k_a,
        r_k.reshape(1, W), rwkv_gn_g, rwkv_gn_b, w_out_bf, final_norm_g.reshape(1, D_MODEL))

    small = jnp.concatenate([g[n] for n, _ in _SMALL] + [loss], axis=1)
    core = lax.axis_index("c")
    by_core = lambda t: [lax.dynamic_index_in_dim(t, i, axis=1, keepdims=False) for i in (core, 1 - core)]
    in_mine, in_sib = by_core(g["w_in"].reshape(N_CHIP, 2, SHARD_IN, D_MODEL))
    out_mine, out_sib = by_core(g["w_out"].reshape(N_CHIP, 2, SHARD_OUT, D_MODEL).astype(BF16))
    (in_theirs, out_theirs), parts = _exchange_pairs(
        [in_sib, out_sib],
        [g["w_lora_up"].reshape(N_DEV, SHARD_LORA, LORA), g["a_lora_up"].reshape(N_DEV, SHARD_LORA, LORA),
         jnp.broadcast_to(small[None], (N_DEV, 1, _SMALL_LANES))])
    by_chip = _exchange_chips([_pair_sum("pair_sum_w_in", in_mine, in_theirs, SHARD_IN // 2),
                               _pair_sum("pair_sum_w_out", out_mine, out_theirs, SHARD_OUT)])
    res = {}
    res["w_in"] = [t.T[None] for t in _reduce_adamw_2d(
        "adamw_w_in", by_chip[0], w_in[0].T, m_w_in[0].T, v_w_in[0].T, SHARD_IN // 2)]
    res["w_out"] = _reduce_adamw("adamw_w_out", by_chip[1], w_out, m_w_out, v_w_out, SHARD_OUT)
    res["w_lora_up"] = _reduce_adamw_t("adamw_w_lora_up", parts[0], w_lora_up, m_w_lora_up, v_w_lora_up, LORA, LORA)
    res["a_lora_up"] = _reduce_adamw_t("adamw_a_lora_up", parts[1], a_lora_up, m_a_lora_up, v_a_lora_up, LORA, LORA)
    as_row = lambda d: {n: d[n] if d[n].ndim > 1 else d[n].reshape(1, size) for n, size in _SMALL}
    vec, loss_row = _adamw_vectors(parts[2], as_row(wts), as_row(mom), as_row(var))
    res.update(vec)
    res = {n: [t.reshape(shapes[n]) for t in res[n]] for n in _WEIGHTS}
    return (loss_row[0, 0], dx, *[res[n][0] for n in _WEIGHTS], *[res[n][1] for n in _WEIGHTS],
            *[res[n][2] for n in _WEIGHTS], *[res[n][3] for n in _WEIGHTS])
```

```python
import numpy as np
import jax
import jax.numpy as jnp
from jax import lax
from jax.experimental import pallas as pl
from jax.experimental.pallas import tpu as pltpu

F32 = jnp.float32
BF16 = jnp.bfloat16

D_MODEL = 1024
CHUNK = 64
RET_HEADS = 4
RET_DV = 128
RET_DK = 64
RET_QK = 256
RET_WIDTH = 512
RWKV_WIDTH = 512
RWKV_HEAD = 64
RWKV_HEADS = 8
LORA = 64
RET_COLS = 2 * RET_QK + 2 * RET_WIDTH
RWKV_COLS = 4 * RWKV_WIDTH + 2 * LORA
IN_COLS = RET_COLS + RWKV_COLS
ROPE_BASE = 10000.0
RMS_EPS = 1e-6
RET_GN_EPS = 1e-5
RWKV_GN_EPS = 64e-5
ADAM_LR = 0.001
ADAM_B1 = 0.9
ADAM_B2 = 0.999
ADAM_EPS = 1e-08
ADAM_WD = 0.01
ADAM_STEP = 10
N_DEV = 8
SHARD_IN = IN_COLS // N_DEV
SHARD_OUT = D_MODEL // N_DEV
SHARD_LORA = RWKV_WIDTH // N_DEV
VMEM_LIMIT = 56 * 1024 * 1024
TOK_TILE = 256
WKV_CHUNK = 64

MESH = pl.DeviceIdType.MESH


def _dot_bf(a, b):
    return jnp.dot(a.astype(BF16), b.astype(BF16), preferred_element_type=F32)


def _dot_nt_bf(a, b):
    return lax.dot_general(a.astype(BF16), b.astype(BF16), (((1,), (1,)), ((), ())), preferred_element_type=F32)


def _dot_tn_bf(a, b):
    return lax.dot_general(a.astype(BF16), b.astype(BF16), (((0,), (0,)), ((), ())), preferred_element_type=F32)


@jax.custom_vjp
def _mm(a, b):
    return _dot_bf(a, b)


@jax.custom_vjp
def _mm_nt(a, b):
    return _dot_nt_bf(a, b)


@jax.custom_vjp
def _mm_tn(a, b):
    return _dot_tn_bf(a, b)


_mm.defvjp(lambda a, b: (_dot_bf(a, b), (a, b)), lambda res, g: (_dot_nt_bf(g, res[1]), _dot_tn_bf(res[0], g)))
_mm_nt.defvjp(lambda a, b: (_dot_nt_bf(a, b), (a, b)), lambda res, g: (_dot_bf(g, res[1]), _dot_tn_bf(g, res[0])))
_mm_tn.defvjp(lambda a, b: (_dot_tn_bf(a, b), (a, b)), lambda res, g: (_dot_nt_bf(res[1], g), _dot_bf(res[0], g)))


def _trunc(x):
    return lax.bitcast_convert_type(lax.bitcast_convert_type(x, jnp.uint32) & jnp.uint32(0xFFFF0000), F32)


def _two_piece(x):
    hi = _trunc(x)
    return jnp.concatenate([hi, x - hi], axis=1)


def _mix_raw(x, mat2):
    return _unstack(jnp.dot(_two_piece(_stack(x)), mat2, preferred_element_type=F32))


@jax.custom_vjp
def _head_mix(x, mat2):
    return _mix_raw(x, mat2)


_head_mix.defvjp(lambda x, mat2: (_mix_raw(x, mat2), mat2), lambda mat2, g: (_mix_raw(g, mat2), jnp.zeros_like(mat2)))


def _swap_halves(x):
    lane = lax.broadcasted_iota(jnp.int32, x.shape, 1)
    return jnp.where((lane & (RET_DK - 1)) < RET_DK // 2, pltpu.roll(x, RET_QK - RET_DK // 2, axis=1),
                     pltpu.roll(x, RET_DK // 2, axis=1))


@jax.custom_vjp
def _rot(x):
    return _swap_halves(x)


_rot.defvjp(lambda x: (_swap_halves(x), None), lambda _, g: (_swap_halves(g),))


def _params(**kw):
    return pltpu.CompilerParams(vmem_limit_bytes=VMEM_LIMIT, **kw)


def _full(shape):
    nd = len(shape)
    return pl.BlockSpec(shape, lambda i, _nd=nd: (0,) * _nd)


def _rows(tile, width):
    return pl.BlockSpec((tile, width), lambda i: (i, 0))


def _rows_of_one(tile, width):
    return pl.BlockSpec((None, tile, width), lambda i: (0, i, 0))


def _block_mix(n, blk, scale=1.0):
    idx = np.arange(n) // blk
    m = (idx[:, None] == idx[None, :]).astype(np.float32) * scale
    return jnp.asarray(np.concatenate([m, m], axis=0))


def _rope_tables(T):
    half = RET_DK // 2
    expo = -np.arange(half, dtype=np.float32) / np.float32(half)
    freqs = np.exp(expo * np.float32(np.log(ROPE_BASE))).astype(np.float32)
    ang = np.arange(T, dtype=np.float32)[:, None] * freqs[None, :]
    cos, sin = np.cos(ang).astype(np.float32), np.sin(ang).astype(np.float32)
    cos_h = np.concatenate([cos, cos], axis=1)
    sin_h = np.concatenate([-sin, sin], axis=1)
    cos_t = np.tile(cos_h, (1, RET_HEADS))
    sin_t = np.tile(sin_h, (1, RET_HEADS))
    return jnp.asarray(cos_t), jnp.asarray(sin_t)


def _ret_tables():
    h = np.arange(RET_HEADS, dtype=np.float32)
    lg = np.log(1.0 - np.exp2(-5.0 - h)).astype(np.float32)
    idx = np.arange(CHUNK, dtype=np.float32)
    intra = np.exp(lg[:, None, None] * np.abs(idx[:, None] - idx[None, :])).astype(np.float32)
    q_dec = np.exp(lg[:, None] * (idx[None, :] + 1.0)).astype(np.float32)
    k_dec = np.exp(lg[:, None] * (CHUNK - 1.0 - idx[None, :])).astype(np.float32)
    chunk_dec = np.exp(lg * CHUNK).astype(np.float32)
    lane_head = np.arange(RET_QK) // RET_DK
    mask = (lane_head[None, :] == np.arange(RET_HEADS)[:, None]).astype(np.float32)
    m = np.broadcast_to(mask[:, None, :], (RET_HEADS, CHUNK, RET_QK)).copy()
    qd = m * q_dec[:, :, None]
    kd = m * k_dec[:, :, None]
    return jnp.asarray(intra), jnp.asarray(m), jnp.asarray(qd), jnp.asarray(kd), [float(c) for c in chunk_dec]


def _rmsnorm(x, g):
    return x * lax.rsqrt(jnp.mean(x * x, axis=-1, keepdims=True) + RMS_EPS) * g


def _inproj(x, norm_g, w_in_t):
    T = x.shape[1]

    def body(x_ref, g_ref, w_ref, pr_ref, pw_ref, u_ref):
        ub = _rmsnorm(x_ref[...], g_ref[...]).astype(BF16)
        u_ref[...] = ub
        pr_ref[...] = _dot_nt_bf(ub, w_ref[:RET_COLS, :])
        pw_ref[...] = _dot_nt_bf(ub, w_ref[RET_COLS:, :])

    return pl.pallas_call(
        body, name="inproj", grid=(T // TOK_TILE,),
        in_specs=[_rows_of_one(TOK_TILE, D_MODEL), _full((1, D_MODEL)), _full((IN_COLS, D_MODEL))],
        out_specs=[_rows(TOK_TILE, RET_COLS), _rows(TOK_TILE, RWKV_COLS), _rows(TOK_TILE, D_MODEL)],
        out_shape=[jax.ShapeDtypeStruct((T, RET_COLS), F32), jax.ShapeDtypeStruct((T, RWKV_COLS), F32),
                   jax.ShapeDtypeStruct((T, D_MODEL), BF16)],
        compiler_params=_params(dimension_semantics=("arbitrary",)),
    )(x, norm_g, w_in_t)


def _ret_chunk(pq, pk, v_heads, s_heads, cos_t, sin_t, dec, hm, qd, kd, chunk_dec):
    q = pq * cos_t + _rot(pq) * sin_t
    k = (pk * cos_t + _rot(pk) * sin_t) * (RET_DK ** -0.5)
    outs, s_out = [], []
    for h in range(RET_HEADS):
        sc = _mm_nt(q * hm[h], k * hm[h]) * dec[h]
        intra = _mm(sc, v_heads[h])
        kv = _mm_tn(k * kd[h], v_heads[h])
        inter = _mm(q * qd[h], s_heads[h])
        outs.append(intra + inter)
        s_out.append(s_heads[h] * chunk_dec[h] + kv)
    return tuple(outs), tuple(s_out)


def _ret_specs():
    const = [_full((RET_HEADS, CHUNK, CHUNK)), _full((RET_HEADS, CHUNK, RET_QK)),
             _full((RET_HEADS, CHUNK, RET_QK)), _full((RET_HEADS, CHUNK, RET_QK))]
    return const


RET_GROUP = 4


def _ret_fwd(p_ret, tabs):
    T = p_ret.shape[0]
    G = RET_GROUP
    ng = T // (CHUNK * G)
    cos_t, sin_t, dec, hm, qd, kd, chunk_dec = tabs

    def body(p_ref, cos_ref, sin_ref, dec_ref, hm_ref, qd_ref, kd_ref, out_ref, sin_save_ref, s_scr):
        @pl.when(pl.program_id(0) == 0)
        def _():
            s_scr[...] = jnp.zeros_like(s_scr)

        consts = (dec_ref[...], hm_ref[...], qd_ref[...], kd_ref[...])
        s_heads = tuple(s_scr[h] for h in range(RET_HEADS))
        for c in range(G):
            rows = pl.ds(c * CHUNK, CHUNK)
            for h in range(RET_HEADS):
                sin_save_ref[c, h] = s_heads[h]
            v_heads = tuple(p_ref[rows, 2 * RET_QK + RET_DV * h:2 * RET_QK + RET_DV * (h + 1)] for h in range(RET_HEADS))
            outs, s_heads = _ret_chunk(p_ref[rows, 0:RET_QK], p_ref[rows, RET_QK:2 * RET_QK], v_heads, s_heads,
                                       cos_ref[rows, :], sin_ref[rows, :], *consts, chunk_dec)
            for h in range(RET_HEADS):
                out_ref[rows, RET_DV * h:RET_DV * (h + 1)] = outs[h]
        for h in range(RET_HEADS):
            s_scr[h] = s_heads[h]

    tok = CHUNK * G
    return pl.pallas_call(
        body, name="ret_fwd", grid=(ng,),
        in_specs=[pl.BlockSpec((tok, RET_COLS), lambda i: (i, 0)), _rows(tok, RET_QK), _rows(tok, RET_QK)] + _ret_specs(),
        out_specs=[_rows(tok, RET_WIDTH), pl.BlockSpec((G, RET_HEADS, RET_QK, RET_DV), lambda i: (i, 0, 0, 0))],
        out_shape=[jax.ShapeDtypeStruct((T, RET_WIDTH), F32),
                   jax.ShapeDtypeStruct((T // CHUNK, RET_HEADS, RET_QK, RET_DV), F32)],
        scratch_shapes=[pltpu.VMEM((RET_HEADS, RET_QK, RET_DV), F32)],
        compiler_params=_params(dimension_semantics=("arbitrary",)),
    )(p_ret, cos_t, sin_t, dec, hm, qd, kd)


def _ret_bwd(p_ret, s_saved, d_ret, tabs):
    T = p_ret.shape[0]
    G = RET_GROUP
    ng = T // (CHUNK * G)
    cos_t, sin_t, dec, hm, qd, kd, chunk_dec = tabs

    def body(p_ref, s_ref, dret_ref, cos_ref, sin_ref, dec_ref, hm_ref, qd_ref, kd_ref, dp_ref, ds_scr):
        @pl.when(pl.program_id(0) == 0)
        def _():
            ds_scr[...] = jnp.zeros_like(ds_scr)

        consts = (dec_ref[...], hm_ref[...], qd_ref[...], kd_ref[...])
        d_s = tuple(ds_scr[h] for h in range(RET_HEADS))
        for c in reversed(range(G)):
            rows = pl.ds(c * CHUNK, CHUNK)
            v_heads = tuple(p_ref[rows, 2 * RET_QK + RET_DV * h:2 * RET_QK + RET_DV * (h + 1)] for h in range(RET_HEADS))
            s_heads = tuple(s_ref[c, h] for h in range(RET_HEADS))
            tables = (cos_ref[rows, :], sin_ref[rows, :]) + consts
            _, vjp = jax.vjp(lambda a, b, c_, d: _ret_chunk(a, b, c_, d, *tables, chunk_dec),
                             p_ref[rows, 0:RET_QK], p_ref[rows, RET_QK:2 * RET_QK], v_heads, s_heads)
            d_out = tuple(dret_ref[rows, RET_DV * h:RET_DV * (h + 1)] for h in range(RET_HEADS))
            dq, dk, dv, d_s = vjp((d_out, d_s))
            dp_ref[rows, 0:RET_QK] = dq
            dp_ref[rows, RET_QK:2 * RET_QK] = dk
            for h in range(RET_HEADS):
                dp_ref[rows, 2 * RET_QK + RET_DV * h:2 * RET_QK + RET_DV * (h + 1)] = dv[h]
        for h in range(RET_HEADS):
            ds_scr[h] = d_s[h]

    tok = CHUNK * G
    rev = lambda i: (ng - 1 - i, 0)
    return pl.pallas_call(
        body, name="ret_bwd", grid=(ng,),
        in_specs=[pl.BlockSpec((tok, RET_COLS), rev),
                  pl.BlockSpec((G, RET_HEADS, RET_QK, RET_DV), lambda i: (ng - 1 - i, 0, 0, 0)),
                  pl.BlockSpec((tok, RET_WIDTH), rev), pl.BlockSpec((tok, RET_QK), rev), pl.BlockSpec((tok, RET_QK), rev)]
        + _ret_specs(),
        out_specs=pl.BlockSpec((tok, 2 * RET_QK + RET_WIDTH), rev),
        out_shape=jax.ShapeDtypeStruct((T, 2 * RET_QK + RET_WIDTH), F32),
        scratch_shapes=[pltpu.VMEM((RET_HEADS, RET_QK, RET_DV), F32)],
        compiler_params=_params(dimension_semantics=("arbitrary",)),
    )(p_ret, s_saved, d_ret, cos_t, sin_t, dec, hm, qd, kd)


def _wkv_consts():
    lane = lax.broadcasted_iota(jnp.int32, (RWKV_HEAD, RWKV_WIDTH), 1)
    sub = lax.broadcasted_iota(jnp.int32, (RWKV_HEAD, RWKV_WIDTH), 0)
    diag = ((lane & (RWKV_HEAD - 1)) == sub).astype(F32)
    r = lax.broadcasted_iota(jnp.int32, (3 * 128, 128), 0)
    c = lax.broadcasted_iota(jnp.int32, (3 * 128, 128), 1)
    ones = (((r & 127) >> 6) == (c >> 6)).astype(BF16)
    return diag, ones


def _stack(x):
    return jnp.concatenate([x[:, 128 * p:128 * (p + 1)] for p in range(4)], axis=0)


def _unstack(y):
    n = y.shape[0] // 4
    return jnp.concatenate([y[n * p:n * (p + 1)] for p in range(4)], axis=1)


def _split(x, n):
    pieces = []
    for _ in range(n):
        p = x.astype(BF16)
        pieces.append(p)
        x = x - p.astype(F32)
    return pieces


def _lane_sum(x, ones):
    return _unstack(jnp.dot(_stack(x).astype(BF16), ones[:128], preferred_element_type=F32))


def _colsum(x):
    return jnp.sum(x, axis=0, keepdims=True)


def _rows_times(vec, mat):
    lane = lax.broadcasted_iota(jnp.int32, (1, 128), 1)
    tiles = []
    for p in range(4):
        lhs = jnp.concatenate([vec[:, 128 * p:128 * p + RWKV_HEAD], vec[:, 128 * p + RWKV_HEAD:128 * (p + 1)]], axis=0)
        out = jnp.dot(lhs, mat[:, 128 * p:128 * (p + 1)], preferred_element_type=F32)
        tiles.append(jnp.where(lane < RWKV_HEAD, out[0:1], out[1:2]))
    return jnp.concatenate(tiles, axis=1)


def _expand_cols(xt, t):
    lane = lax.broadcasted_iota(jnp.int32, (RWKV_HEAD, 128), 1)
    tiles = []
    for p in range(4):
        lo = jnp.broadcast_to(xt[128 * p:128 * p + RWKV_HEAD, t:t + 1], (RWKV_HEAD, 128))
        hi = jnp.broadcast_to(xt[128 * p + RWKV_HEAD:128 * (p + 1), t:t + 1], (RWKV_HEAD, 128))
        tiles.append(jnp.where(lane < RWKV_HEAD, lo, hi))
    return jnp.concatenate(tiles, axis=1)


def _head_sums(x, ones):
    return _unstack(jnp.dot(jnp.concatenate(_split(_stack(x), 3), axis=1), ones, preferred_element_type=F32))


def _wkv_fwd(r, w, k, v, kap, a):
    T = r.shape[0]
    C = WKV_CHUNK
    nc = T // C

    def body(r_ref, w_ref, k_ref, v_ref, kap_ref, a_ref, o_ref, s_all_ref, sa_all_ref, s_scr):
        @pl.when(pl.program_id(0) == 0)
        def _():
            s_scr[...] = jnp.zeros_like(s_scr)

        diag, ones = _wkv_consts()
        rr, ww, kk, vv, kap_, aa = (ref[...] for ref in (r_ref, w_ref, k_ref, v_ref, kap_ref, a_ref))
        bb = kap_ * aa
        c1 = _head_sums(pltpu.roll(bb, 1, axis=0) * kap_, ones)
        row = lambda x, t: x[t:t + 1]

        v_cols = vv.T

        s_prev = s_scr[...]
        sa = _lane_sum(s_prev * (-row(kap_, 0)), ones)
        ls, rows = None, []

        def emit_o(t, s_t):
            rows.append(_colsum(_lane_sum(s_t * row(rr, t), ones) * diag))
            if t % 8 == 7:
                o_ref[t - 7:t + 1, :] = jnp.concatenate(rows, axis=0)
                rows.clear()

        for t in range(C):
            u = s_prev * row(ww, t) + _expand_cols(v_cols, t) * row(kk, t)
            if t > 0:
                sa = ls - sa * row(c1, t)
            if t + 1 < C:
                ls = _lane_sum(u * (-row(kap_, t + 1)), ones)
            if t > 0:
                emit_o(t - 1, s_prev)
            s_prev = u + sa * row(bb, t)
            s_all_ref[t] = s_prev
            sa_all_ref[t] = sa.astype(BF16)
        emit_o(C - 1, s_prev)
        s_scr[...] = s_prev

    spec = _rows(C, RWKV_WIDTH)
    return pl.pallas_call(
        body, name="wkv_fwd", grid=(nc,),
        in_specs=[spec] * 6,
        out_specs=[spec, pl.BlockSpec((C, RWKV_HEAD, RWKV_WIDTH), lambda i: (i, 0, 0)),
                   pl.BlockSpec((C, RWKV_HEAD, RWKV_WIDTH), lambda i: (i, 0, 0))],
        out_shape=[jax.ShapeDtypeStruct((T, RWKV_WIDTH), F32), jax.ShapeDtypeStruct((T, RWKV_HEAD, RWKV_WIDTH), F32),
                   jax.ShapeDtypeStruct((T, RWKV_HEAD, RWKV_WIDTH), BF16)],
        scratch_shapes=[pltpu.VMEM((RWKV_HEAD, RWKV_WIDTH), F32)],
        compiler_params=_params(dimension_semantics=("arbitrary",)),
    )(r, w, k, v, kap, a)


def _wkv_bwd(r, w, k, v, kap, a, s_all, sa_all, d_o):
    T = r.shape[0]
    C = WKV_CHUNK
    nc = T // C

    def body(r_ref, w_ref, k_ref, v_ref, kap_ref, a_ref, s_ref, s_before_ref, sa_ref, do_ref,
             dr_ref, dw_ref, dk_ref, dv_ref, dkap_ref, da_ref, ds_scr):
        first_chunk = pl.program_id(0) == nc - 1

        @pl.when(pl.program_id(0) == 0)
        def _():
            ds_scr[...] = jnp.zeros_like(ds_scr)

        diag, ones = _wkv_consts()
        rr, ww, kk, vv, kap_, aa, dd = (ref[...] for ref in (r_ref, w_ref, k_ref, v_ref, kap_ref, a_ref, do_ref))
        bb = kap_ * aa
        e1 = _head_sums(pltpu.roll(kap_, C - 1, axis=0) * bb, ones)
        row = lambda x, t: x[t:t + 1]

        def state_before(t):
            return s_ref[t - 1] if t > 0 else jnp.where(first_chunk, 0.0, s_before_ref[0])

        do_cols = dd.T

        d_sn, dsa, rows = None, None, [None] * C

        def emit_rows(t, d_sn_t, dsa_t):
            s_prev, dof = state_before(t), _expand_cols(do_cols, t)
            dv = _colsum(_lane_sum(d_sn_t * row(kk, t), ones) * diag)
            db = _colsum(d_sn_t * sa_ref[t].astype(F32))
            rows[t] = (_colsum(s_ref[t] * dof), _colsum(d_sn_t * s_prev), _rows_times(row(vv, t), d_sn_t), dv,
                       db * row(aa, t) - _colsum(dsa_t * s_prev), db * row(kap_, t))
            if t % 8 == 0:
                for j, ref in enumerate((dr_ref, dw_ref, dk_ref, dv_ref, dkap_ref, da_ref)):
                    ref[t:t + 8, :] = jnp.concatenate([rows[u][j] for u in range(t, t + 8)], axis=0)

        for t in reversed(range(C)):
            dof = _expand_cols(do_cols, t)
            if t == C - 1:
                d_sn = ds_scr[...] + dof * row(rr, t)
                dsa = _lane_sum(d_sn * row(bb, t), ones)
            else:
                v_t = d_sn * row(ww, t + 1) + dof * row(rr, t)
                ls = _lane_sum(v_t * row(bb, t), ones)
                emit_rows(t + 1, d_sn, dsa)
                d_sn = v_t - dsa * row(kap_, t + 1)
                dsa = ls - dsa * row(e1, t)
        emit_rows(0, d_sn, dsa)
        d_s = d_sn * row(ww, 0) - dsa * row(kap_, 0)
        ds_scr[...] = d_s

    spec = pl.BlockSpec((C, RWKV_WIDTH), lambda i: (nc - 1 - i, 0))
    states = pl.BlockSpec((C, RWKV_HEAD, RWKV_WIDTH), lambda i: (nc - 1 - i, 0, 0))
    before = pl.BlockSpec((1, RWKV_HEAD, RWKV_WIDTH), lambda i: (jnp.maximum((nc - 1 - i) * C - 1, 0), 0, 0))
    return pl.pallas_call(
        body, name="wkv_bwd", grid=(nc,),
        in_specs=[spec] * 6 + [states, before, states, spec],
        out_specs=[spec] * 6,
        out_shape=[jax.ShapeDtypeStruct((T, RWKV_WIDTH), F32)] * 6,
        scratch_shapes=[pltpu.VMEM((RWKV_HEAD, RWKV_WIDTH), F32)],
        compiler_params=_params(dimension_semantics=("arbitrary",)),
    )(r, w, k, v, kap, a, s_all, s_all, sa_all, d_o)


W = RWKV_WIDTH


def _softplus(y):
    return jnp.maximum(y, 0.0) + jnp.log(1.0 + jnp.exp(-jnp.abs(y)))


def _prep_fn(kr, xwa, w0, a0, k_k, k_a, wup_pad, aup_pad, ones64):
    w_log = -_softplus(-(w0 + _mm_nt(jnp.tanh(xwa), wup_pad))) - 0.5
    decay = jnp.exp(-jnp.exp(w_log))
    a = jax.nn.sigmoid(a0 + _mm_nt(xwa, aup_pad))
    kk = kr * k_k
    kap = kk / jnp.maximum(jnp.sqrt(_head_mix(kk * kk, ones64)), 1e-12)
    k = kr * (1.0 + (a - 1.0) * k_a)
    return decay, k, kap, a


def _shift_down(p, first_row):
    rows = lax.broadcasted_iota(jnp.int32, p.shape, 0)
    return jnp.where(rows == 0, first_row, pltpu.roll(p, 1, axis=0))


def _shift_up(z, last_row):
    n = z.shape[0]
    rows = lax.broadcasted_iota(jnp.int32, z.shape, 0)
    return jnp.where(rows == n - 1, last_row, pltpu.roll(z, n - 1, axis=0))


def _prev_block_spec():
    return pl.BlockSpec((8, RWKV_COLS), lambda i: (jnp.maximum(i * (TOK_TILE // 8) - 1, 0), 0))


def _mixed(p_ref, prev8_ref, mu_ref, first_tile):
    p = p_ref[...]
    first_row = jnp.where(first_tile, 0.0, prev8_ref[7:8, :])
    prev = _shift_down(p, first_row)
    return p, prev, p + mu_ref[...] * (prev - p)


def _prep_fwd(p_rwkv, mu, w0, a0, k_k, k_a, wup_pad, aup_pad, ones64):
    T = p_rwkv.shape[0]

    def body(p_ref, prev8_ref, mu_ref, w0_ref, a0_ref, kk_ref, ka_ref, wup_ref, aup_ref, ones_ref,
             r_ref, w_ref, k_ref, v_ref, kap_ref, a_ref, g_ref):
        _, _, ps = _mixed(p_ref, prev8_ref, mu_ref, pl.program_id(0) == 0)
        decay, k, kap, a = _prep_fn(ps[:, W:2 * W], ps[:, 4 * W:], w0_ref[...], a0_ref[...], kk_ref[...], ka_ref[...],
                                    wup_ref[...], aup_ref[...], ones_ref[...])
        r_ref[...] = ps[:, 0:W]
        w_ref[...] = decay
        k_ref[...] = k
        v_ref[...] = ps[:, 2 * W:3 * W]
        kap_ref[...] = kap
        a_ref[...] = a
        g_ref[...] = ps[:, 3 * W:4 * W]

    vec = _full((1, W))
    return pl.pallas_call(
        body, name="prep_fwd", grid=(T // TOK_TILE,),
        in_specs=[_rows(TOK_TILE, RWKV_COLS), _prev_block_spec(), _full((1, RWKV_COLS)), vec, vec, vec, vec,
                  _full((W, 2 * LORA)), _full((W, 2 * LORA)), _full((256, 128))],
        out_specs=[_rows(TOK_TILE, W)] * 7,
        out_shape=[jax.ShapeDtypeStruct((T, W), F32)] * 7,
        compiler_params=_params(dimension_semantics=("arbitrary",)),
    )(p_rwkv, p_rwkv, mu, w0, a0, k_k, k_a, wup_pad, aup_pad, ones64)


def _prep_bwd(p_rwkv, mu, w0, a0, k_k, k_a, wup_pad, aup_pad, ones64, dr, dw, dk, dv, dkap, da, dg, dr2, dk2, dv2):
    T = p_rwkv.shape[0]
    nt = T // TOK_TILE

    def body(p_ref, prev8_ref, mu_ref, w0_ref, a0_ref, kk_ref, ka_ref, wup_ref, aup_ref, ones_ref,
             dr_ref, dw_ref, dk_ref, dv_ref, dkap_ref, da_ref, dg_ref, dr2_ref, dk2_ref, dv2_ref,
             dp_ref, dmu_ref, dw0_ref, da0_ref, dkk_ref, dka_ref, dwup_ref, daup_ref, zrow_scr):
        i = pl.program_id(0)
        accs = (dmu_ref, dw0_ref, da0_ref, dkk_ref, dka_ref, dwup_ref, daup_ref)

        @pl.when(i == 0)
        def _():
            zrow_scr[...] = jnp.zeros_like(zrow_scr)
            for ref in accs:
                ref[...] = jnp.zeros_like(ref)

        p, prev, ps = _mixed(p_ref, prev8_ref, mu_ref, i == nt - 1)
        ones = ones_ref[...]
        _, vjp = jax.vjp(lambda *args: _prep_fn(*args, ones), ps[:, W:2 * W], ps[:, 4 * W:], w0_ref[...], a0_ref[...],
                         kk_ref[...], ka_ref[...], wup_ref[...], aup_ref[...])
        dkr, dxwa, dw0, da0, dkk, dka, dwup, daup = vjp(
            (dw_ref[...], dk_ref[...] + dk2_ref[...], dkap_ref[...], da_ref[...]))
        dps = jnp.concatenate([dr_ref[...] + dr2_ref[...], dkr, dv_ref[...] + dv2_ref[...], dg_ref[...], dxwa], axis=1)
        z = dps * mu_ref[...]
        dp_ref[...] = dps - z + _shift_up(z, zrow_scr[0:1, :])
        zrow_scr[0:1, :] = z[0:1, :]
        for ref, val in zip(accs, (_colsum(dps * (prev - p)), dw0, da0, dkk, dka, dwup, daup)):
            ref[...] += val

    rev = lambda i: (nt - 1 - i, 0)
    vec = _full((1, W))
    lora = _full((W, 2 * LORA))
    tile = pl.BlockSpec((TOK_TILE, W), rev)
    prev8 = pl.BlockSpec((8, RWKV_COLS), lambda i: (jnp.maximum((nt - 1 - i) * (TOK_TILE // 8) - 1, 0), 0))
    return pl.pallas_call(
        body, name="prep_bwd", grid=(nt,),
        in_specs=[pl.BlockSpec((TOK_TILE, RWKV_COLS), rev), prev8, _full((1, RWKV_COLS)), vec, vec, vec, vec, lora, lora,
                  _full((256, 128))] + [tile] * 10,
        out_specs=[pl.BlockSpec((TOK_TILE, RWKV_COLS), rev), _full((1, RWKV_COLS)), vec, vec, vec, vec, lora, lora],
        out_shape=[jax.ShapeDtypeStruct((T, RWKV_COLS), F32), jax.ShapeDtypeStruct((1, RWKV_COLS), F32)]
        + [jax.ShapeDtypeStruct((1, W), F32)] * 4 + [jax.ShapeDtypeStruct((W, 2 * LORA), F32)] * 2,
        scratch_shapes=[pltpu.VMEM((8, RWKV_COLS), F32)],
        compiler_params=_params(dimension_semantics=("arbitrary",)),
    )(p_rwkv, p_rwkv, mu, w0, a0, k_k, k_a, wup_pad, aup_pad, ones64, dr, dw, dk, dv, dkap, da, dg, dr2, dk2, dv2)


def _silu(x):
    return x * jax.nn.sigmoid(x)


def _post_y(o, r, k, v, g_rw, ret_raw, g_ret, ret_gn_g, gn_g, gn_b, r_k, avg128, avg64, ones64):
    xc = ret_raw - _head_mix(ret_raw, avg128)
    ret = xc * lax.rsqrt(_head_mix(xc * xc, avg128) + RET_GN_EPS)
    y_ret = _silu(g_ret) * (ret * ret_gn_g)
    oc = o - _head_mix(o, avg64)
    on = oc * lax.rsqrt(_head_mix(oc * oc, avg64) + RWKV_GN_EPS) * gn_g + gn_b
    bonus = _head_mix(r * k * r_k, ones64) * v
    y_rwkv = _silu(g_rw) * (on + bonus)
    return y_ret, y_rwkv


def _post_loss(h, final_g, target):
    err = _rmsnorm(h, final_g) - target
    return 0.5 * jnp.sum(jnp.mean(err * err, axis=-1))


def _post(o, r, k, v, g_rw, ret_raw, p_ret, x, target, ret_gn_g, gn_g, gn_b, r_k, final_g, w_out, avg128, avg64, ones64):
    T = x.shape[1]
    n_tok_out = 8

    def body(o_ref, r_ref, k_ref, v_ref, grw_ref, ret_ref, gret_ref, x_ref, tgt_ref, rg_ref, gg_ref, gb_ref, rk_ref, fg_ref,
             wo_ref, a128_ref, a64_ref, ones_ref, *outs):
        tok_outs, (dwo_ref, drg_ref, dgg_ref, dgb_ref, drk_ref, dfg_ref, loss_ref) = outs[:n_tok_out], outs[n_tok_out:]
        accs = (dwo_ref, drg_ref, dgg_ref, dgb_ref, drk_ref, dfg_ref, loss_ref)

        @pl.when(pl.program_id(0) == 0)
        def _():
            for ref in accs:
                ref[...] = jnp.zeros_like(ref)

        consts = (a128_ref[...], a64_ref[...], ones_ref[...])
        (y_ret, y_rwkv), vjp = jax.vjp(
            lambda *args: _post_y(*args, *consts), o_ref[...], r_ref[...], k_ref[...], v_ref[...], grw_ref[...], ret_ref[...],
            gret_ref[...], rg_ref[...], gg_ref[...], gb_ref[...], rk_ref[...])
        h = x_ref[...] + _dot_bf(y_ret, wo_ref[0:RET_WIDTH, :]) + _dot_bf(y_rwkv, wo_ref[RET_WIDTH:, :])
        loss, (dh, dfg) = jax.value_and_grad(_post_loss, argnums=(0, 1))(h, fg_ref[...], tgt_ref[...])
        dy_ret = _dot_nt_bf(dh, wo_ref[0:RET_WIDTH, :])
        dy_rwkv = _dot_nt_bf(dh, wo_ref[RET_WIDTH:, :])
        do, dr, dk, dv, dgrw, dret, dgret, drg, dgg, dgb, drk = vjp((dy_ret, dy_rwkv))
        for ref, val in zip(tok_outs, (dh, do, dr, dk, dv, dgrw, dret, dgret)):
            ref[...] = val
        dwo_ref[0:RET_WIDTH, :] += _dot_tn_bf(y_ret, dh)
        dwo_ref[RET_WIDTH:, :] += _dot_tn_bf(y_rwkv, dh)
        for ref, val in zip(accs[1:], (drg, dgg, dgb, drk, dfg, jnp.full((1, 128), loss, F32))):
            ref[...] += val

    tile = _rows(TOK_TILE, W)
    wide = _rows(TOK_TILE, D_MODEL)
    wide_of_one = _rows_of_one(TOK_TILE, D_MODEL)
    vec = _full((1, W))
    sq = _full((256, 128))
    return pl.pallas_call(
        body, name="post", grid=(T // TOK_TILE,),
        in_specs=[tile] * 6 + [pl.BlockSpec((TOK_TILE, W), lambda i: (i, 2)), wide_of_one, wide_of_one, vec, vec, vec, vec,
                               _full((1, D_MODEL)), _full((D_MODEL, D_MODEL)), sq, sq, sq],
        out_specs=[wide] + [tile] * 7 + [_full((D_MODEL, D_MODEL)), vec, vec, vec, vec, _full((1, D_MODEL)), _full((1, 128))],
        out_shape=[jax.ShapeDtypeStruct((T, D_MODEL), F32)] + [jax.ShapeDtypeStruct((T, W), F32)] * 7
        + [jax.ShapeDtypeStruct((D_MODEL, D_MODEL), F32)] + [jax.ShapeDtypeStruct((1, W), F32)] * 4
        + [jax.ShapeDtypeStruct((1, D_MODEL), F32), jax.ShapeDtypeStruct((1, 128), F32)],
        compiler_params=_params(dimension_semantics=("arbitrary",)),
    )(o, r, k, v, g_rw, ret_raw, p_ret, x, target, ret_gn_g, gn_g, gn_b, r_k, final_g, w_out, avg128, avg64, ones64)


def _inproj_bwd_x(x, norm_g, dp_qkv, dg_ret, dp_rwkv, dh, w_in_t):
    T = x.shape[1]
    n_qkv = 2 * RET_QK + RET_WIDTH

    def body(x_ref, g_ref, dqkv_ref, dgret_ref, drwkv_ref, dh_ref, w_ref, dx_ref, dg_ref):
        @pl.when(pl.program_id(0) == 0)
        def _():
            dg_ref[...] = jnp.zeros_like(dg_ref)

        _, vjp = jax.vjp(_rmsnorm, x_ref[...], g_ref[...])
        du = (_dot_bf(dqkv_ref[...], w_ref[0:n_qkv, :]) + _dot_bf(dgret_ref[...], w_ref[n_qkv:RET_COLS, :])
              + _dot_bf(drwkv_ref[...], w_ref[RET_COLS:, :]))
        dx, dg = vjp(du)
        dx_ref[...] = dx + dh_ref[...]
        dg_ref[...] += dg

    return pl.pallas_call(
        body, name="inproj_bwd_x", grid=(T // TOK_TILE,),
        in_specs=[_rows_of_one(TOK_TILE, D_MODEL), _full((1, D_MODEL)), _rows(TOK_TILE, n_qkv), _rows(TOK_TILE, RET_WIDTH),
                  _rows(TOK_TILE, RWKV_COLS), _rows(TOK_TILE, D_MODEL), _full((IN_COLS, D_MODEL))],
        out_specs=[_rows_of_one(TOK_TILE, D_MODEL), _full((1, D_MODEL))],
        out_shape=[jax.ShapeDtypeStruct((1, T, D_MODEL), F32), jax.ShapeDtypeStruct((1, D_MODEL), F32)],
        compiler_params=_params(dimension_semantics=("arbitrary",)),
    )(x, norm_g, dp_qkv, dg_ret, dp_rwkv, dh, w_in_t)


def _grad_w_in(u, dps):
    T = u.shape[0]
    widths = [dp.shape[1] for dp in dps]
    tile = 2 * TOK_TILE
    steps = T // tile

    def body(u_ref, *refs):
        dp_refs, (mine_ref, sib_ref, acc_ref) = refs[:len(dps)], refs[len(dps):]

        @pl.when(pl.program_id(0) == 0)
        def _():
            acc_ref[...] = jnp.zeros_like(acc_ref)

        row = 0
        for dp_ref, n in zip(dp_refs, widths):
            acc_ref[row:row + n, :] += _dot_tn_bf(dp_ref[...], u_ref[...])
            row += n

        @pl.when(pl.program_id(0) == steps - 1)
        def _():
            core = lax.axis_index("c")
            for dev in range(N_DEV):
                block = acc_ref[dev * SHARD_IN:(dev + 1) * SHARD_IN, :].astype(BF16)

                @pl.when(core == dev % 2)
                def _():
                    mine_ref[dev // 2] = block

                @pl.when(core != dev % 2)
                def _():
                    sib_ref[dev // 2] = block

    half = jax.ShapeDtypeStruct((N_DEV // 2, SHARD_IN, D_MODEL), BF16)
    return pl.pallas_call(
        body, name="grad_w_in", grid=(steps,),
        in_specs=[_rows(tile, D_MODEL)] + [_rows(tile, n) for n in widths],
        out_specs=[_full(half.shape), _full(half.shape)],
        out_shape=[half, half],
        scratch_shapes=[pltpu.VMEM((IN_COLS, D_MODEL), F32)],
        compiler_params=_params(dimension_semantics=("arbitrary",)),
    )(u, *dps)


def _pad_lora(w_up_t, first):
    z = jnp.zeros_like(w_up_t)
    return jnp.concatenate([w_up_t, z] if first else [z, w_up_t], axis=1)


def _local_grads(x, target, norm_g, w_in_t, ret_gn_g, mu, w_lora_up_t, w0, a_lora_up_t, a0, k_k, k_a, r_k, gn_g, gn_b,
                 w_out_bf, final_g):
    T = x.shape[1]
    tabs = _rope_tables(T) + _ret_tables()
    ones64 = _block_mix(128, RWKV_HEAD)
    avg64 = _block_mix(128, RWKV_HEAD, 1.0 / RWKV_HEAD)
    avg128 = _block_mix(128, RET_DV, 1.0 / RET_DV)
    wup_pad, aup_pad = _pad_lora(w_lora_up_t, True), _pad_lora(a_lora_up_t, False)

    p_ret, p_rwkv, u = _inproj(x, norm_g, w_in_t)
    ret_raw, s_saved = _ret_fwd(p_ret, tabs)
    r, w, k, v, kap, a, g_rw = _prep_fwd(p_rwkv, mu, w0, a0, k_k, k_a, wup_pad, aup_pad, ones64)
    o, s_all, sa_all = _wkv_fwd(r, w, k, v, kap, a)
    (dh, do, dr2, dk2, dv2, dgrw, dret, dgret, d_w_out, d_ret_gn_g, d_gn_g, d_gn_b, d_r_k, d_final_g, loss) = _post(
        o, r, k, v, g_rw, ret_raw, p_ret, x, target, ret_gn_g, gn_g, gn_b, r_k, final_g, w_out_bf, avg128, avg64, ones64)
    dr, dw, dk, dv, dkap, da = _wkv_bwd(r, w, k, v, kap, a, s_all, sa_all, do)
    dp_rwkv, d_mu, d_w0, d_a0, d_k_k, d_k_a, d_wup, d_aup = _prep_bwd(
        p_rwkv, mu, w0, a0, k_k, k_a, wup_pad, aup_pad, ones64, dr, dw, dk, dv, dkap, da, dgrw, dr2, dk2, dv2)
    dp_qkv = _ret_bwd(p_ret, s_saved, dret, tabs)
    dx, d_norm_g = _inproj_bwd_x(x, norm_g, dp_qkv, dgret, dp_rwkv, dh, w_in_t)
    d_w_in = _grad_w_in(u, [dp_qkv, dgret, dp_rwkv])
    grads = dict(norm_g=d_norm_g, w_in=d_w_in, ret_gn_g=d_ret_gn_g, rwkv_mu=d_mu, w_lora_up=d_wup[:, :LORA], w0=d_w0,
                 a_lora_up=d_aup[:, LORA:], a0=d_a0, k_k=d_k_k, k_a=d_k_a, r_k=d_r_k, rwkv_gn_g=d_gn_g, rwkv_gn_b=d_gn_b,
                 w_out=d_w_out, final_norm_g=d_final_g)
    return loss, dx, grads


def _mesh_pos():
    return lax.axis_index("x"), lax.axis_index("y"), lax.axis_index("c")


def _all_gather(shards):
    n = len(shards)

    def body(*refs):
        x_refs, out_refs = refs[:n], refs[n:2 * n]
        send_sems, recv_sems, local_sems = refs[2 * n:]
        x, y, c = _mesh_pos()
        me, sibling = (x, y, c), (x, y, 1 - c)
        chips = [(1 - x, y), (x, 1 - y), (1 - x, 1 - y)]

        def rows(a, pos):
            m = x_refs[a].shape[0]
            return out_refs[a].at[pl.ds((4 * pos[0] + 2 * pos[1] + pos[2]) * m, m), :]

        def copy(a, k, block, to, src=None):
            return pltpu.make_async_remote_copy(
                src_ref=rows(a, block) if src is None else src, dst_ref=rows(a, block),
                send_sem=send_sems.at[a, k], recv_sem=recv_sems.at[a, k], device_id=to, device_id_type=MESH)

        mine = [pltpu.make_async_copy(x_refs[a], rows(a, me), local_sems.at[a]) for a in range(n)]
        for cp in mine:
            cp.start()
        first = []
        for a in range(n):
            first.append(copy(a, 0, me, sibling, src=x_refs[a]))
            first += [copy(a, 1 + j, me, (*chip, c), src=x_refs[a]) for j, chip in enumerate(chips)]
        for cp in first:
            cp.start()
        passed = []
        for j, chip in enumerate(chips):
            for a in range(n):
                copy(a, 1 + j, (*chip, c), me).wait_recv()
                passed.append(copy(a, 4 + j, (*chip, c), sibling))
                passed[-1].start()
        for a in range(n):
            copy(a, 0, sibling, me).wait_recv()
            for j, chip in enumerate(chips):
                copy(a, 4 + j, (*chip, 1 - c), me).wait_recv()
        for cp in first + passed:
            cp.wait_send()
        for cp in mine:
            cp.wait()

    vmem = pl.BlockSpec(memory_space=pltpu.VMEM)
    return pl.pallas_call(
        body, name="gather_weights",
        out_shape=[jax.ShapeDtypeStruct((N_DEV * s.shape[0], s.shape[1]), s.dtype) for s in shards],
        in_specs=[vmem] * n, out_specs=[vmem] * n,
        scratch_shapes=[pltpu.SemaphoreType.DMA((n, 7)), pltpu.SemaphoreType.DMA((n, 7)), pltpu.SemaphoreType.DMA((n,))],
        compiler_params=_params(),
    )(*shards)


N_CHIP = 4


def _exchange_pairs(big, small):
    nb, ns = len(big), len(small)

    def body(*refs):
        big_in, small_in = refs[:nb], refs[nb:nb + ns]
        theirs, small_out = refs[nb + ns:2 * nb + ns], refs[2 * nb + ns:2 * nb + 2 * ns]
        pair_send, pair_recv, send_sems, recv_sems, local_sems = refs[2 * nb + 2 * ns:]
        x, y, c = _mesh_pos()
        me = 4 * x + 2 * y + c
        local = [pltpu.make_async_copy(small_in[a].at[me], small_out[a].at[me], local_sems.at[a]) for a in range(ns)]
        for cp in local:
            cp.start()
        copies = [pltpu.make_async_remote_copy(
            src_ref=big_in[a], dst_ref=theirs[a], send_sem=pair_send.at[a], recv_sem=pair_recv.at[a],
            device_id=(x, y, 1 - c), device_id_type=MESH) for a in range(nb)]
        for k in range(1, N_DEV):
            peer = (x ^ (k >> 2), y ^ ((k >> 1) & 1), c ^ (k & 1))
            peer_idx = 4 * peer[0] + 2 * peer[1] + peer[2]
            copies += [pltpu.make_async_remote_copy(
                src_ref=small_in[a].at[peer_idx], dst_ref=small_out[a].at[me], send_sem=send_sems.at[a, k - 1],
                recv_sem=recv_sems.at[a, k - 1], device_id=peer, device_id_type=MESH) for a in range(ns)]
        for cp in copies:
            cp.start()
        for cp in copies:
            cp.wait()
        for cp in local:
            cp.wait()

    hbm = pl.BlockSpec(memory_space=pl.ANY)
    out_shape = [jax.ShapeDtypeStruct(p.shape, p.dtype) for p in big + small]
    dma = pltpu.SemaphoreType.DMA
    res = pl.pallas_call(
        body, name="exchange_pairs", out_shape=out_shape,
        in_specs=[hbm] * (nb + ns), out_specs=[hbm] * len(out_shape),
        scratch_shapes=[dma((nb,)), dma((nb,)), dma((ns, 7)), dma((ns, 7)), dma((ns,))],
        compiler_params=_params(),
    )(*big, *small)
    return res[:nb], res[nb:]


def _pair_sum(name, mine, theirs, row_tile):
    _, rows, cols = mine.shape

    def body(a_ref, b_ref, o_ref):
        o_ref[...] = (a_ref[...].astype(F32) + b_ref[...].astype(F32)).astype(o_ref.dtype)

    spec = pl.BlockSpec((N_CHIP, row_tile, cols), lambda i: (0, i, 0))
    return pl.pallas_call(
        body, name=name, grid=(rows // row_tile,), in_specs=[spec, spec], out_specs=spec,
        out_shape=jax.ShapeDtypeStruct(mine.shape, mine.dtype),
        compiler_params=_params(dimension_semantics=("arbitrary",)),
    )(mine, theirs)


def _exchange_chips(parts):
    n = len(parts)

    def body(*refs):
        in_refs, out_refs = refs[:n], refs[n:2 * n]
        send_sems, recv_sems, local_sems = refs[2 * n:]
        x, y, c = _mesh_pos()
        my_chip = 2 * x + y
        own = [pltpu.make_async_copy(in_refs[a].at[my_chip], out_refs[a].at[my_chip], local_sems.at[a]) for a in range(n)]
        for cp in own:
            cp.start()
        copies = []
        for k in range(1, N_CHIP):
            px, py = x ^ (k >> 1), y ^ (k & 1)
            copies += [pltpu.make_async_remote_copy(
                src_ref=in_refs[a].at[2 * px + py], dst_ref=out_refs[a].at[my_chip], send_sem=send_sems.at[a, k - 1],
                recv_sem=recv_sems.at[a, k - 1], device_id=(px, py, c), device_id_type=MESH) for a in range(n)]
        for cp in copies:
            cp.start()
        for cp in copies:
            cp.wait()
        for cp in own:
            cp.wait()

    hbm = pl.BlockSpec(memory_space=pl.ANY)
    dma = pltpu.SemaphoreType.DMA
    return pl.pallas_call(
        body, name="exchange_chips",
        out_shape=[jax.ShapeDtypeStruct(p.shape, p.dtype) for p in parts],
        in_specs=[hbm] * n, out_specs=[hbm] * n,
        scratch_shapes=[dma((n, N_CHIP - 1)), dma((n, N_CHIP - 1)), dma((n,))],
        compiler_params=_params(),
    )(*parts)


def _adamw(w, g, m, v):
    m = ADAM_B1 * m + (1.0 - ADAM_B1) * g
    v = ADAM_B2 * v + (1.0 - ADAM_B2) * (g * g)
    m_hat = m / (1.0 - ADAM_B1 ** ADAM_STEP)
    v_hat = v / (1.0 - ADAM_B2 ** ADAM_STEP)
    return -ADAM_LR * (m_hat / (jnp.sqrt(v_hat) + ADAM_EPS) + ADAM_WD * w), m, v


def _sum_parts(name, parts, row_tile):
    n_parts, rows, cols = parts.shape

    def body(p_ref, g_ref):
        g = p_ref[0].astype(F32)
        for s in range(1, n_parts):
            g = g + p_ref[s].astype(F32)
        g_ref[...] = g

    return pl.pallas_call(
        body, name=name, grid=(rows // row_tile,),
        in_specs=[pl.BlockSpec((n_parts, row_tile, cols), lambda i: (0, i, 0))],
        out_specs=pl.BlockSpec((row_tile, cols), lambda i: (i, 0)),
        out_shape=jax.ShapeDtypeStruct((rows, cols), F32),
        compiler_params=_params(dimension_semantics=("arbitrary",)),
    )(parts)


def _adamw_apply(name, g, w, m, v, row_tile):
    _, rows, cols = w.shape

    def body(g_ref, w_ref, m_ref, v_ref, d_ref, nm_ref, nv_ref):
        d_ref[0], nm_ref[0], nv_ref[0] = _adamw(w_ref[0], g_ref[0], m_ref[0], v_ref[0])

    tile = pl.BlockSpec((1, row_tile, cols), lambda i: (0, i, 0))
    return pl.pallas_call(
        body, name=name, grid=(rows // row_tile,), in_specs=[tile] * 4, out_specs=[tile] * 3,
        out_shape=[jax.ShapeDtypeStruct((1, rows, cols), F32)] * 3,
        compiler_params=_params(dimension_semantics=("arbitrary",)),
    )(g, w, m, v)


def _reduce_adamw_2d(name, parts, w, m, v, row_tile):
    n_parts, rows, cols = parts.shape

    def body(p_ref, w_ref, m_ref, v_ref, g_ref, d_ref, nm_ref, nv_ref):
        g = p_ref[0].astype(F32)
        for s in range(1, n_parts):
            g = g + p_ref[s].astype(F32)
        g_ref[...] = g
        d_ref[...], nm_ref[...], nv_ref[...] = _adamw(w_ref[...], g, m_ref[...], v_ref[...])

    tile = pl.BlockSpec((row_tile, cols), lambda i: (i, 0))
    return pl.pallas_call(
        body, name=name, grid=(rows // row_tile,),
        in_specs=[pl.BlockSpec((n_parts, row_tile, cols), lambda i: (0, i, 0)), tile, tile, tile],
        out_specs=[tile] * 4,
        out_shape=[jax.ShapeDtypeStruct((rows, cols), F32)] * 4,
        compiler_params=_params(dimension_semantics=("arbitrary",)),
    )(parts, w, m, v)


def _reduce_adamw_t(name, parts_t, w, m, v, sum_tile, row_tile):
    g = _sum_parts(name + "_sum", parts_t, sum_tile).T[None]
    return [g] + list(_adamw_apply(name, g, w, m, v, row_tile))


def _reduce_adamw(name, parts, w, m, v, row_tile):
    n_parts, rows, cols = parts.shape

    def body(p_ref, w_ref, m_ref, v_ref, g_ref, d_ref, nm_ref, nv_ref):
        g = p_ref[0].astype(F32)
        for s in range(1, n_parts):
            g = g + p_ref[s].astype(F32)
        g_ref[0] = g
        d_ref[0], nm_ref[0], nv_ref[0] = _adamw(w_ref[0], g, m_ref[0], v_ref[0])

    tile = pl.BlockSpec((1, row_tile, cols), lambda i: (0, i, 0))
    return pl.pallas_call(
        body, name=name, grid=(rows // row_tile,),
        in_specs=[pl.BlockSpec((n_parts, row_tile, cols), lambda i: (0, i, 0)), tile, tile, tile],
        out_specs=[tile] * 4,
        out_shape=[jax.ShapeDtypeStruct((1, rows, cols), F32)] * 4,
        compiler_params=_params(dimension_semantics=("arbitrary",)),
    )(parts, w, m, v)


_SMALL = (("norm_g", 1024), ("ret_gn_g", 512), ("rwkv_mu", 2176), ("w0", 512), ("a0", 512), ("k_k", 512), ("k_a", 512),
          ("r_k", 512), ("rwkv_gn_g", 512), ("rwkv_gn_b", 512), ("final_norm_g", 1024))
_SMALL_LANES = sum(n for _, n in _SMALL) + 128
_WEIGHTS = ("norm_g", "w_in", "ret_gn_g", "rwkv_mu", "w_lora_up", "w0", "a_lora_up", "a0", "k_k", "k_a", "r_k", "rwkv_gn_g",
            "rwkv_gn_b", "w_out", "final_norm_g")


def _adamw_vectors(parts, wts, mom, var):
    k = len(_SMALL)

    def body(p_ref, *refs):
        w_refs, m_refs, v_refs, outs = refs[:k], refs[k:2 * k], refs[2 * k:3 * k], refs[3 * k:]
        g_all = p_ref[0]
        for s in range(1, N_DEV):
            g_all = g_all + p_ref[s]
        off = 0
        for i, (name, n) in enumerate(_SMALL):
            g = g_all[:, off:off + n]
            off += n
            if name == "r_k":
                g = jnp.concatenate([g[:, RWKV_HEAD * h:RWKV_HEAD * (h + 1)] for h in range(RWKV_HEADS)], axis=0)[None]
            outs[4 * i][...] = g
            outs[4 * i + 1][...], outs[4 * i + 2][...], outs[4 * i + 3][...] = _adamw(
                w_refs[i][...], g, m_refs[i][...], v_refs[i][...])
        outs[4 * k][...] = g_all[:, off:off + 128]

    vmem = pl.BlockSpec(memory_space=pltpu.VMEM)
    shapes = [jax.ShapeDtypeStruct(wts[n].shape, F32) for n, _ in _SMALL for _ in range(4)] + [jax.ShapeDtypeStruct((1, 128), F32)]
    res = pl.pallas_call(
        body, name="adamw_vectors", out_shape=shapes,
        in_specs=[vmem] * (1 + 3 * k), out_specs=[vmem] * len(shapes), compiler_params=_params(),
    )(parts, *[wts[n] for n, _ in _SMALL], *[mom[n] for n, _ in _SMALL], *[var[n] for n, _ in _SMALL])
    return {n: res[4 * i:4 * i + 4] for i, (n, _) in enumerate(_SMALL)}, res[4 * k]


def kernel(x, norm_g, w_in, ret_gn_g, rwkv_mu, w_lora_up, w0, a_lora_up, a0, k_k, k_a, r_k, rwkv_gn_g, rwkv_gn_b, w_out, final_norm_g, loss_target, m_norm_g, m_w_in, m_ret_gn_g, m_rwkv_mu, m_w_lora_up, m_w0, m_a_lora_up, m_a0, m_k_k, m_k_a, m_r_k, m_rwkv_gn_g, m_rwkv_gn_b, m_w_out, m_final_norm_g, v_norm_g, v_w_in, v_ret_gn_g, v_rwkv_mu, v_w_lora_up, v_w0, v_a_lora_up, v_a0, v_k_k, v_k_a, v_r_k, v_rwkv_gn_g, v_rwkv_gn_b, v_w_out, v_final_norm_g):
    wts = dict(norm_g=norm_g, w_in=w_in, ret_gn_g=ret_gn_g, rwkv_mu=rwkv_mu, w_lora_up=w_lora_up, w0=w0, a_lora_up=a_lora_up,
               a0=a0, k_k=k_k, k_a=k_a, r_k=r_k, rwkv_gn_g=rwkv_gn_g, rwkv_gn_b=rwkv_gn_b, w_out=w_out,
               final_norm_g=final_norm_g)
    mom = dict(norm_g=m_norm_g, w_in=m_w_in, ret_gn_g=m_ret_gn_g, rwkv_mu=m_rwkv_mu, w_lora_up=m_w_lora_up, w0=m_w0,
               a_lora_up=m_a_lora_up, a0=m_a0, k_k=m_k_k, k_a=m_k_a, r_k=m_r_k, rwkv_gn_g=m_rwkv_gn_g,
               rwkv_gn_b=m_rwkv_gn_b, w_out=m_w_out, final_norm_g=m_final_norm_g)
    var = dict(norm_g=v_norm_g, w_in=v_w_in, ret_gn_g=v_ret_gn_g, rwkv_mu=v_rwkv_mu, w_lora_up=v_w_lora_up, w0=v_w0,
               a_lora_up=v_a_lora_up, a0=v_a0, k_k=v_k_k, k_a=v_k_a, r_k=v_r_k, rwkv_gn_g=v_rwkv_gn_g,
               rwkv_gn_b=v_rwkv_gn_b, w_out=v_w_out, final_norm_g=v_final_norm_g)
    shapes = {n: wts[n].shape for n in _WEIGHTS}

    w_in_t, w_out_bf, wup_t, aup_t = _all_gather(
        [w_in[0].T.astype(BF16), w_out[0].astype(BF16), w_lora_up[0].T, a_lora_up[0].T])

    loss, dx, g = _local_grads(
        x, loss_target, norm_g, w_in_t, ret_gn_g, rwkv_mu, wup_t, w0, aup_t, a0, k_k, k_a,
        r_k.reshape(1, W), rwkv_gn_g, rwkv_gn_b, w_out_bf, final_norm_g.reshape(1, D_MODEL))

    small = jnp.concatenate([g[n] for n, _ in _SMALL] + [loss], axis=1)
    core = lax.axis_index("c")
    by_core = lambda t: [lax.dynamic_index_in_dim(t, i, axis=1, keepdims=False) for i in (core, 1 - core)]
    in_mine, in_sib = g["w_in"]
    out_mine, out_sib = by_core(g["w_out"].reshape(N_CHIP, 2, SHARD_OUT, D_MODEL).astype(BF16))
    (in_theirs, out_theirs), parts = _exchange_pairs(
        [in_sib, out_sib],
        [g["w_lora_up"].reshape(N_DEV, SHARD_LORA, LORA), g["a_lora_up"].reshape(N_DEV, SHARD_LORA, LORA),
         jnp.broadcast_to(small[None], (N_DEV, 1, _SMALL_LANES))])
    by_chip = _exchange_chips([_pair_sum("pair_sum_w_in", in_mine, in_theirs, SHARD_IN // 2),
                               _pair_sum("pair_sum_w_out", out_mine, out_theirs, SHARD_OUT)])
    res = {}
    res["w_in"] = [t.T[None] for t in _reduce_adamw_2d(
        "adamw_w_in", by_chip[0], w_in[0].T, m_w_in[0].T, v_w_in[0].T, SHARD_IN // 2)]
    res["w_out"] = _reduce_adamw("adamw_w_out", by_chip[1], w_out, m_w_out, v_w_out, SHARD_OUT)
    res["w_lora_up"] = _reduce_adamw_t("adamw_w_lora_up", parts[0], w_lora_up, m_w_lora_up, v_w_lora_up, LORA, LORA)
    res["a_lora_up"] = _reduce_adamw_t("adamw_a_lora_up", parts[1], a_lora_up, m_a_lora_up, v_a_lora_up, LORA, LORA)
    as_row = lambda d: {n: d[n] if d[n].ndim > 1 else d[n].reshape(1, size) for n, size in _SMALL}
    vec, loss_row = _adamw_vectors(parts[2], as_row(wts), as_row(mom), as_row(var))
    res.update(vec)
    res = {n: [t.reshape(shapes[n]) for t in res[n]] for n in _WEIGHTS}
    return (loss_row[0, 0], dx, *[res[n][0] for n in _WEIGHTS], *[res[n][1] for n in _WEIGHTS],
            *[res[n][2] for n in _WEIGHTS], *[res[n][3] for n in _WEIGHTS])
```

```python
import numpy as np
import jax
import jax.numpy as jnp
from jax import lax
from jax.experimental import pallas as pl
from jax.experimental.pallas import tpu as pltpu

F32 = jnp.float32
BF16 = jnp.bfloat16

D_MODEL = 1024
CHUNK = 64
RET_HEADS = 4
RET_DV = 128
RET_DK = 64
RET_QK = 256
RET_WIDTH = 512
RWKV_WIDTH = 512
RWKV_HEAD = 64
RWKV_HEADS = 8
LORA = 64
RET_COLS = 2 * RET_QK + 2 * RET_WIDTH
RWKV_COLS = 4 * RWKV_WIDTH + 2 * LORA
IN_COLS = RET_COLS + RWKV_COLS
ROPE_BASE = 10000.0
RMS_EPS = 1e-6
RET_GN_EPS = 1e-5
RWKV_GN_EPS = 64e-5
ADAM_LR = 0.001
ADAM_B1 = 0.9
ADAM_B2 = 0.999
ADAM_EPS = 1e-08
ADAM_WD = 0.01
ADAM_STEP = 10
N_DEV = 8
SHARD_IN = IN_COLS // N_DEV
SHARD_OUT = D_MODEL // N_DEV
SHARD_LORA = RWKV_WIDTH // N_DEV
VMEM_LIMIT = 56 * 1024 * 1024
TOK_TILE = 256
WKV_CHUNK = 64

MESH = pl.DeviceIdType.MESH


def _dot_bf(a, b):
    return jnp.dot(a.astype(BF16), b.astype(BF16), preferred_element_type=F32)


def _dot_nt_bf(a, b):
    return lax.dot_general(a.astype(BF16), b.astype(BF16), (((1,), (1,)), ((), ())), preferred_element_type=F32)


def _dot_tn_bf(a, b):
    return lax.dot_general(a.astype(BF16), b.astype(BF16), (((0,), (0,)), ((), ())), preferred_element_type=F32)


@jax.custom_vjp
def _mm(a, b):
    return _dot_bf(a, b)


@jax.custom_vjp
def _mm_nt(a, b):
    return _dot_nt_bf(a, b)


@jax.custom_vjp
def _mm_tn(a, b):
    return _dot_tn_bf(a, b)


_mm.defvjp(lambda a, b: (_dot_bf(a, b), (a, b)), lambda res, g: (_dot_nt_bf(g, res[1]), _dot_tn_bf(res[0], g)))
_mm_nt.defvjp(lambda a, b: (_dot_nt_bf(a, b), (a, b)), lambda res, g: (_dot_bf(g, res[1]), _dot_tn_bf(g, res[0])))
_mm_tn.defvjp(lambda a, b: (_dot_tn_bf(a, b), (a, b)), lambda res, g: (_dot_nt_bf(res[1], g), _dot_bf(res[0], g)))


def _trunc(x):
    return lax.bitcast_convert_type(lax.bitcast_convert_type(x, jnp.uint32) & jnp.uint32(0xFFFF0000), F32)


def _two_piece(x):
    hi = _trunc(x)
    return jnp.concatenate([hi, x - hi], axis=1)


def _mix_raw(x, mat2):
    return _unstack(jnp.dot(_two_piece(_stack(x)), mat2, preferred_element_type=F32))


@jax.custom_vjp
def _head_mix(x, mat2):
    return _mix_raw(x, mat2)


_head_mix.defvjp(lambda x, mat2: (_mix_raw(x, mat2), mat2), lambda mat2, g: (_mix_raw(g, mat2), jnp.zeros_like(mat2)))


def _swap_halves(x):
    lane = lax.broadcasted_iota(jnp.int32, x.shape, 1)
    return jnp.where((lane & (RET_DK - 1)) < RET_DK // 2, pltpu.roll(x, RET_QK - RET_DK // 2, axis=1),
                     pltpu.roll(x, RET_DK // 2, axis=1))


@jax.custom_vjp
def _rot(x):
    return _swap_halves(x)


_rot.defvjp(lambda x: (_swap_halves(x), None), lambda _, g: (_swap_halves(g),))


def _params(**kw):
    return pltpu.CompilerParams(vmem_limit_bytes=VMEM_LIMIT, **kw)


def _full(shape):
    nd = len(shape)
    return pl.BlockSpec(shape, lambda i, _nd=nd: (0,) * _nd)


def _rows(tile, width):
    return pl.BlockSpec((tile, width), lambda i: (i, 0))


def _rows_of_one(tile, width):
    return pl.BlockSpec((None, tile, width), lambda i: (0, i, 0))


def _block_mix(n, blk, scale=1.0):
    idx = np.arange(n) // blk
    m = (idx[:, None] == idx[None, :]).astype(np.float32) * scale
    return jnp.asarray(np.concatenate([m, m], axis=0))


def _rope_tables(T):
    half = RET_DK // 2
    expo = -np.arange(half, dtype=np.float32) / np.float32(half)
    freqs = np.exp(expo * np.float32(np.log(ROPE_BASE))).astype(np.float32)
    ang = np.arange(T, dtype=np.float32)[:, None] * freqs[None, :]
    cos, sin = np.cos(ang).astype(np.float32), np.sin(ang).astype(np.float32)
    cos_h = np.concatenate([cos, cos], axis=1)
    sin_h = np.concatenate([-sin, sin], axis=1)
    cos_t = np.tile(cos_h, (1, RET_HEADS))
    sin_t = np.tile(sin_h, (1, RET_HEADS))
    return jnp.asarray(cos_t), jnp.asarray(sin_t)


def _ret_tables():
    h = np.arange(RET_HEADS, dtype=np.float32)
    lg = np.log(1.0 - np.exp2(-5.0 - h)).astype(np.float32)
    idx = np.arange(CHUNK, dtype=np.float32)
    intra = np.exp(lg[:, None, None] * np.abs(idx[:, None] - idx[None, :])).astype(np.float32)
    q_dec = np.exp(lg[:, None] * (idx[None, :] + 1.0)).astype(np.float32)
    k_dec = np.exp(lg[:, None] * (CHUNK - 1.0 - idx[None, :])).astype(np.float32)
    chunk_dec = np.exp(lg * CHUNK).astype(np.float32)
    lane_head = np.arange(RET_QK) // RET_DK
    mask = (lane_head[None, :] == np.arange(RET_HEADS)[:, None]).astype(np.float32)
    m = np.broadcast_to(mask[:, None, :], (RET_HEADS, CHUNK, RET_QK)).copy()
    qd = m * q_dec[:, :, None]
    kd = m * k_dec[:, :, None]
    return jnp.asarray(intra), jnp.asarray(m), jnp.asarray(qd), jnp.asarray(kd), [float(c) for c in chunk_dec]


def _rmsnorm(x, g):
    return x * lax.rsqrt(jnp.mean(x * x, axis=-1, keepdims=True) + RMS_EPS) * g


def _inproj(x, norm_g, w_in_t):
    T = x.shape[1]

    def body(x_ref, g_ref, w_ref, pr_ref, pw_ref, u_ref):
        ub = _rmsnorm(x_ref[...], g_ref[...]).astype(BF16)
        u_ref[...] = ub
        pr_ref[...] = _dot_nt_bf(ub, w_ref[:RET_COLS, :])
        pw_ref[...] = _dot_nt_bf(ub, w_ref[RET_COLS:, :])

    return pl.pallas_call(
        body, name="inproj", grid=(T // TOK_TILE,),
        in_specs=[_rows_of_one(TOK_TILE, D_MODEL), _full((1, D_MODEL)), _full((IN_COLS, D_MODEL))],
        out_specs=[_rows(TOK_TILE, RET_COLS), _rows(TOK_TILE, RWKV_COLS), _rows(TOK_TILE, D_MODEL)],
        out_shape=[jax.ShapeDtypeStruct((T, RET_COLS), F32), jax.ShapeDtypeStruct((T, RWKV_COLS), F32),
                   jax.ShapeDtypeStruct((T, D_MODEL), BF16)],
        compiler_params=_params(dimension_semantics=("arbitrary",)),
    )(x, norm_g, w_in_t)


def _ret_chunk(pq, pk, v_heads, s_heads, cos_t, sin_t, dec, hm, qd, kd, chunk_dec):
    q = pq * cos_t + _rot(pq) * sin_t
    k = (pk * cos_t + _rot(pk) * sin_t) * (RET_DK ** -0.5)
    outs, s_out = [], []
    for h in range(RET_HEADS):
        sc = _mm_nt(q * hm[h], k * hm[h]) * dec[h]
        intra = _mm(sc, v_heads[h])
        kv = _mm_tn(k * kd[h], v_heads[h])
        inter = _mm(q * qd[h], s_heads[h])
        outs.append(intra + inter)
        s_out.append(s_heads[h] * chunk_dec[h] + kv)
    return tuple(outs), tuple(s_out)


def _ret_specs():
    const = [_full((RET_HEADS, CHUNK, CHUNK)), _full((RET_HEADS, CHUNK, RET_QK)),
             _full((RET_HEADS, CHUNK, RET_QK)), _full((RET_HEADS, CHUNK, RET_QK))]
    return const


RET_GROUP = 4


def _ret_fwd(p_ret, tabs):
    T = p_ret.shape[0]
    G = RET_GROUP
    ng = T // (CHUNK * G)
    cos_t, sin_t, dec, hm, qd, kd, chunk_dec = tabs

    def body(p_ref, cos_ref, sin_ref, dec_ref, hm_ref, qd_ref, kd_ref, out_ref, sin_save_ref, s_scr):
        @pl.when(pl.program_id(0) == 0)
        def _():
            s_scr[...] = jnp.zeros_like(s_scr)

        consts = (dec_ref[...], hm_ref[...], qd_ref[...], kd_ref[...])
        s_heads = tuple(s_scr[h] for h in range(RET_HEADS))
        for c in range(G):
            rows = pl.ds(c * CHUNK, CHUNK)
            for h in range(RET_HEADS):
                sin_save_ref[c, h] = s_heads[h]
            v_heads = tuple(p_ref[rows, 2 * RET_QK + RET_DV * h:2 * RET_QK + RET_DV * (h + 1)] for h in range(RET_HEADS))
            outs, s_heads = _ret_chunk(p_ref[rows, 0:RET_QK], p_ref[rows, RET_QK:2 * RET_QK], v_heads, s_heads,
                                       cos_ref[rows, :], sin_ref[rows, :], *consts, chunk_dec)
            for h in range(RET_HEADS):
                out_ref[rows, RET_DV * h:RET_DV * (h + 1)] = outs[h]
        for h in range(RET_HEADS):
            s_scr[h] = s_heads[h]

    tok = CHUNK * G
    return pl.pallas_call(
        body, name="ret_fwd", grid=(ng,),
        in_specs=[pl.BlockSpec((tok, RET_COLS), lambda i: (i, 0)), _rows(tok, RET_QK), _rows(tok, RET_QK)] + _ret_specs(),
        out_specs=[_rows(tok, RET_WIDTH), pl.BlockSpec((G, RET_HEADS, RET_QK, RET_DV), lambda i: (i, 0, 0, 0))],
        out_shape=[jax.ShapeDtypeStruct((T, RET_WIDTH), F32),
                   jax.ShapeDtypeStruct((T // CHUNK, RET_HEADS, RET_QK, RET_DV), F32)],
        scratch_shapes=[pltpu.VMEM((RET_HEADS, RET_QK, RET_DV), F32)],
        compiler_params=_params(dimension_semantics=("arbitrary",)),
    )(p_ret, cos_t, sin_t, dec, hm, qd, kd)


def _ret_bwd(p_ret, s_saved, d_ret, tabs):
    T = p_ret.shape[0]
    G = RET_GROUP
    ng = T // (CHUNK * G)
    cos_t, sin_t, dec, hm, qd, kd, chunk_dec = tabs

    def body(p_ref, s_ref, dret_ref, cos_ref, sin_ref, dec_ref, hm_ref, qd_ref, kd_ref, dp_ref, ds_scr):
        @pl.when(pl.program_id(0) == 0)
        def _():
            ds_scr[...] = jnp.zeros_like(ds_scr)

        consts = (dec_ref[...], hm_ref[...], qd_ref[...], kd_ref[...])
        d_s = tuple(ds_scr[h] for h in range(RET_HEADS))
        for c in reversed(range(G)):
            rows = pl.ds(c * CHUNK, CHUNK)
            v_heads = tuple(p_ref[rows, 2 * RET_QK + RET_DV * h:2 * RET_QK + RET_DV * (h + 1)] for h in range(RET_HEADS))
            s_heads = tuple(s_ref[c, h] for h in range(RET_HEADS))
            tables = (cos_ref[rows, :], sin_ref[rows, :]) + consts
            _, vjp = jax.vjp(lambda a, b, c_, d: _ret_chunk(a, b, c_, d, *tables, chunk_dec),
                             p_ref[rows, 0:RET_QK], p_ref[rows, RET_QK:2 * RET_QK], v_heads, s_heads)
            d_out = tuple(dret_ref[rows, RET_DV * h:RET_DV * (h + 1)] for h in range(RET_HEADS))
            dq, dk, dv, d_s = vjp((d_out, d_s))
            dp_ref[rows, 0:RET_QK] = dq
            dp_ref[rows, RET_QK:2 * RET_QK] = dk
            for h in range(RET_HEADS):
                dp_ref[rows, 2 * RET_QK + RET_DV * h:2 * RET_QK + RET_DV * (h + 1)] = dv[h]
        for h in range(RET_HEADS):
            ds_scr[h] = d_s[h]

    tok = CHUNK * G
    rev = lambda i: (ng - 1 - i, 0)
    return pl.pallas_call(
        body, name="ret_bwd", grid=(ng,),
        in_specs=[pl.BlockSpec((tok, RET_COLS), rev),
                  pl.BlockSpec((G, RET_HEADS, RET_QK, RET_DV), lambda i: (ng - 1 - i, 0, 0, 0)),
                  pl.BlockSpec((tok, RET_WIDTH), rev), pl.BlockSpec((tok, RET_QK), rev), pl.BlockSpec((tok, RET_QK), rev)]
        + _ret_specs(),
        out_specs=pl.BlockSpec((tok, 2 * RET_QK + RET_WIDTH), rev),
        out_shape=jax.ShapeDtypeStruct((T, 2 * RET_QK + RET_WIDTH), F32),
        scratch_shapes=[pltpu.VMEM((RET_HEADS, RET_QK, RET_DV), F32)],
        compiler_params=_params(dimension_semantics=("arbitrary",)),
    )(p_ret, s_saved, d_ret, cos_t, sin_t, dec, hm, qd, kd)


def _wkv_consts():
    lane = lax.broadcasted_iota(jnp.int32, (RWKV_HEAD, RWKV_WIDTH), 1)
    sub = lax.broadcasted_iota(jnp.int32, (RWKV_HEAD, RWKV_WIDTH), 0)
    diag = ((lane & (RWKV_HEAD - 1)) == sub).astype(F32)
    r = lax.broadcasted_iota(jnp.int32, (3 * 128, 128), 0)
    c = lax.broadcasted_iota(jnp.int32, (3 * 128, 128), 1)
    ones = (((r & 127) >> 6) == (c >> 6)).astype(BF16)
    return diag, ones


def _stack(x):
    return jnp.concatenate([x[:, 128 * p:128 * (p + 1)] for p in range(4)], axis=0)


def _unstack(y):
    n = y.shape[0] // 4
    return jnp.concatenate([y[n * p:n * (p + 1)] for p in range(4)], axis=1)


def _split(x, n):
    pieces = []
    for _ in range(n):
        p = x.astype(BF16)
        pieces.append(p)
        x = x - p.astype(F32)
    return pieces


def _lane_sum(x, ones):
    return _unstack(jnp.dot(_stack(x).astype(BF16), ones[:128], preferred_element_type=F32))


def _colsum(x):
    return jnp.sum(x, axis=0, keepdims=True)


def _rows_times(vec, mat):
    lane = lax.broadcasted_iota(jnp.int32, (1, 128), 1)
    tiles = []
    for p in range(4):
        lhs = jnp.concatenate([vec[:, 128 * p:128 * p + RWKV_HEAD], vec[:, 128 * p + RWKV_HEAD:128 * (p + 1)]], axis=0)
        out = jnp.dot(lhs, mat[:, 128 * p:128 * (p + 1)], preferred_element_type=F32)
        tiles.append(jnp.where(lane < RWKV_HEAD, out[0:1], out[1:2]))
    return jnp.concatenate(tiles, axis=1)


def _expand_cols(xt, t):
    lane = lax.broadcasted_iota(jnp.int32, (RWKV_HEAD, 128), 1)
    tiles = []
    for p in range(4):
        lo = jnp.broadcast_to(xt[128 * p:128 * p + RWKV_HEAD, t:t + 1], (RWKV_HEAD, 128))
        hi = jnp.broadcast_to(xt[128 * p + RWKV_HEAD:128 * (p + 1), t:t + 1], (RWKV_HEAD, 128))
        tiles.append(jnp.where(lane < RWKV_HEAD, lo, hi))
    return jnp.concatenate(tiles, axis=1)


def _head_sums(x, ones):
    return _unstack(jnp.dot(jnp.concatenate(_split(_stack(x), 3), axis=1), ones, preferred_element_type=F32))


def _wkv_fwd(r, w, k, v, kap, a):
    T = r.shape[0]
    C = WKV_CHUNK
    nc = T // C

    def body(r_ref, w_ref, k_ref, v_ref, kap_ref, a_ref, o_ref, s_all_ref, sa_all_ref, s_scr):
        @pl.when(pl.program_id(0) == 0)
        def _():
            s_scr[...] = jnp.zeros_like(s_scr)

        diag, ones = _wkv_consts()
        rr, ww, kk, vv, kap_, aa = (ref[...] for ref in (r_ref, w_ref, k_ref, v_ref, kap_ref, a_ref))
        bb = kap_ * aa
        c1 = _head_sums(pltpu.roll(bb, 1, axis=0) * kap_, ones)
        row = lambda x, t: x[t:t + 1]

        v_cols = vv.T

        s_prev = s_scr[...]
        sa = _lane_sum(s_prev * (-row(kap_, 0)), ones)
        ls, rows = None, []

        def emit_o(t, s_t):
            rows.append(_colsum(_lane_sum(s_t * row(rr, t), ones) * diag))
            if t % 8 == 7:
                o_ref[t - 7:t + 1, :] = jnp.concatenate(rows, axis=0)
                rows.clear()

        for t in range(C):
            u = s_prev * row(ww, t) + _expand_cols(v_cols, t) * row(kk, t)
            if t > 0:
                sa = ls - sa * row(c1, t)
            if t + 1 < C:
                ls = _lane_sum(u * (-row(kap_, t + 1)), ones)
            if t > 0:
                emit_o(t - 1, s_prev)
            s_prev = u + sa * row(bb, t)
            s_all_ref[t] = s_prev
            sa_all_ref[t] = sa.astype(BF16)
        emit_o(C - 1, s_prev)
        s_scr[...] = s_prev

    spec = _rows(C, RWKV_WIDTH)
    return pl.pallas_call(
        body, name="wkv_fwd", grid=(nc,),
        in_specs=[spec] * 6,
        out_specs=[spec, pl.BlockSpec((C, RWKV_HEAD, RWKV_WIDTH), lambda i: (i, 0, 0)),
                   pl.BlockSpec((C, RWKV_HEAD, RWKV_WIDTH), lambda i: (i, 0, 0))],
        out_shape=[jax.ShapeDtypeStruct((T, RWKV_WIDTH), F32), jax.ShapeDtypeStruct((T, RWKV_HEAD, RWKV_WIDTH), F32),
                   jax.ShapeDtypeStruct((T, RWKV_HEAD, RWKV_WIDTH), BF16)],
        scratch_shapes=[pltpu.VMEM((RWKV_HEAD, RWKV_WIDTH), F32)],
        compiler_params=_params(dimension_semantics=("arbitrary",)),
    )(r, w, k, v, kap, a)


def _wkv_bwd(r, w, k, v, kap, a, s_all, sa_all, d_o):
    T = r.shape[0]
    C = WKV_CHUNK
    nc = T // C

    def body(r_ref, w_ref, k_ref, v_ref, kap_ref, a_ref, s_ref, s_before_ref, sa_ref, do_ref,
             dr_ref, dw_ref, dk_ref, dv_ref, dkap_ref, da_ref, ds_scr):
        first_chunk = pl.program_id(0) == nc - 1

        @pl.when(pl.program_id(0) == 0)
        def _():
            ds_scr[...] = jnp.zeros_like(ds_scr)

        diag, ones = _wkv_consts()
        rr, ww, kk, vv, kap_, aa, dd = (ref[...] for ref in (r_ref, w_ref, k_ref, v_ref, kap_ref, a_ref, do_ref))
        bb = kap_ * aa
        e1 = _head_sums(pltpu.roll(kap_, C - 1, axis=0) * bb, ones)
        row = lambda x, t: x[t:t + 1]

        def state_before(t):
            return s_ref[t - 1] if t > 0 else jnp.where(first_chunk, 0.0, s_before_ref[0])

        do_cols = dd.T

        d_sn, dsa, rows = None, None, [None] * C

        def emit_rows(t, d_sn_t, dsa_t):
            s_prev, dof = state_before(t), _expand_cols(do_cols, t)
            dv = _colsum(_lane_sum(d_sn_t * row(kk, t), ones) * diag)
            db = _colsum(d_sn_t * sa_ref[t].astype(F32))
            rows[t] = (_colsum(s_ref[t] * dof), _colsum(d_sn_t * s_prev), _rows_times(row(vv, t), d_sn_t), dv,
                       db * row(aa, t) - _colsum(dsa_t * s_prev), db * row(kap_, t))
            if t % 8 == 0:
                for j, ref in enumerate((dr_ref, dw_ref, dk_ref, dv_ref, dkap_ref, da_ref)):
                    ref[t:t + 8, :] = jnp.concatenate([rows[u][j] for u in range(t, t + 8)], axis=0)

        for t in reversed(range(C)):
            dof = _expand_cols(do_cols, t)
            if t == C - 1:
                d_sn = ds_scr[...] + dof * row(rr, t)
                dsa = _lane_sum(d_sn * row(bb, t), ones)
            else:
                v_t = d_sn * row(ww, t + 1) + dof * row(rr, t)
                ls = _lane_sum(v_t * row(bb, t), ones)
                emit_rows(t + 1, d_sn, dsa)
                d_sn = v_t - dsa * row(kap_, t + 1)
                dsa = ls - dsa * row(e1, t)
        emit_rows(0, d_sn, dsa)
        d_s = d_sn * row(ww, 0) - dsa * row(kap_, 0)
        ds_scr[...] = d_s

    spec = pl.BlockSpec((C, RWKV_WIDTH), lambda i: (nc - 1 - i, 0))
    states = pl.BlockSpec((C, RWKV_HEAD, RWKV_WIDTH), lambda i: (nc - 1 - i, 0, 0))
    before = pl.BlockSpec((1, RWKV_HEAD, RWKV_WIDTH), lambda i: (jnp.maximum((nc - 1 - i) * C - 1, 0), 0, 0))
    return pl.pallas_call(
        body, name="wkv_bwd", grid=(nc,),
        in_specs=[spec] * 6 + [states, before, states, spec],
        out_specs=[spec] * 6,
        out_shape=[jax.ShapeDtypeStruct((T, RWKV_WIDTH), F32)] * 6,
        scratch_shapes=[pltpu.VMEM((RWKV_HEAD, RWKV_WIDTH), F32)],
        compiler_params=_params(dimension_semantics=("arbitrary",)),
    )(r, w, k, v, kap, a, s_all, s_all, sa_all, d_o)


W = RWKV_WIDTH


def _softplus(y):
    return jnp.maximum(y, 0.0) + jnp.log(1.0 + jnp.exp(-jnp.abs(y)))


def _prep_fn(kr, xwa, w0, a0, k_k, k_a, wup_pad, aup_pad, ones64):
    w_log = -_softplus(-(w0 + _mm_nt(jnp.tanh(xwa), wup_pad))) - 0.5
    decay = jnp.exp(-jnp.exp(w_log))
    a = jax.nn.sigmoid(a0 + _mm_nt(xwa, aup_pad))
    kk = kr * k_k
    kap = kk / jnp.maximum(jnp.sqrt(_head_mix(kk * kk, ones64)), 1e-12)
    k = kr * (1.0 + (a - 1.0) * k_a)
    return decay, k, kap, a


def _shift_down(p, first_row):
    rows = lax.broadcasted_iota(jnp.int32, p.shape, 0)
    return jnp.where(rows == 0, first_row, pltpu.roll(p, 1, axis=0))


def _shift_up(z, last_row):
    n = z.shape[0]
    rows = lax.broadcasted_iota(jnp.int32, z.shape, 0)
    return jnp.where(rows == n - 1, last_row, pltpu.roll(z, n - 1, axis=0))


def _prev_block_spec():
    return pl.BlockSpec((8, RWKV_COLS), lambda i: (jnp.maximum(i * (TOK_TILE // 8) - 1, 0), 0))


def _mixed(p_ref, prev8_ref, mu_ref, first_tile):
    p = p_ref[...]
    first_row = jnp.where(first_tile, 0.0, prev8_ref[7:8, :])
    prev = _shift_down(p, first_row)
    return p, prev, p + mu_ref[...] * (prev - p)


def _prep_fwd(p_rwkv, mu, w0, a0, k_k, k_a, wup_pad, aup_pad, ones64):
    T = p_rwkv.shape[0]

    def body(p_ref, prev8_ref, mu_ref, w0_ref, a0_ref, kk_ref, ka_ref, wup_ref, aup_ref, ones_ref,
             r_ref, w_ref, k_ref, v_ref, kap_ref, a_ref, g_ref):
        _, _, ps = _mixed(p_ref, prev8_ref, mu_ref, pl.program_id(0) == 0)
        decay, k, kap, a = _prep_fn(ps[:, W:2 * W], ps[:, 4 * W:], w0_ref[...], a0_ref[...], kk_ref[...], ka_ref[...],
                                    wup_ref[...], aup_ref[...], ones_ref[...])
        r_ref[...] = ps[:, 0:W]
        w_ref[...] = decay
        k_ref[...] = k
        v_ref[...] = ps[:, 2 * W:3 * W]
        kap_ref[...] = kap
        a_ref[...] = a
        g_ref[...] = ps[:, 3 * W:4 * W]

    vec = _full((1, W))
    return pl.pallas_call(
        body, name="prep_fwd", grid=(T // TOK_TILE,),
        in_specs=[_rows(TOK_TILE, RWKV_COLS), _prev_block_spec(), _full((1, RWKV_COLS)), vec, vec, vec, vec,
                  _full((W, 2 * LORA)), _full((W, 2 * LORA)), _full((256, 128))],
        out_specs=[_rows(TOK_TILE, W)] * 7,
        out_shape=[jax.ShapeDtypeStruct((T, W), F32)] * 7,
        compiler_params=_params(dimension_semantics=("arbitrary",)),
    )(p_rwkv, p_rwkv, mu, w0, a0, k_k, k_a, wup_pad, aup_pad, ones64)


def _prep_bwd(p_rwkv, mu, w0, a0, k_k, k_a, wup_pad, aup_pad, ones64, dr, dw, dk, dv, dkap, da, dg, dr2, dk2, dv2):
    T = p_rwkv.shape[0]
    nt = T // TOK_TILE

    def body(p_ref, prev8_ref, mu_ref, w0_ref, a0_ref, kk_ref, ka_ref, wup_ref, aup_ref, ones_ref,
             dr_ref, dw_ref, dk_ref, dv_ref, dkap_ref, da_ref, dg_ref, dr2_ref, dk2_ref, dv2_ref,
             dp_ref, dmu_ref, dw0_ref, da0_ref, dkk_ref, dka_ref, dwup_ref, daup_ref, zrow_scr):
        i = pl.program_id(0)
        accs = (dmu_ref, dw0_ref, da0_ref, dkk_ref, dka_ref, dwup_ref, daup_ref)

        @pl.when(i == 0)
        def _():
            zrow_scr[...] = jnp.zeros_like(zrow_scr)
            for ref in accs:
                ref[...] = jnp.zeros_like(ref)

        p, prev, ps = _mixed(p_ref, prev8_ref, mu_ref, i == nt - 1)
        ones = ones_ref[...]
        _, vjp = jax.vjp(lambda *args: _prep_fn(*args, ones), ps[:, W:2 * W], ps[:, 4 * W:], w0_ref[...], a0_ref[...],
                         kk_ref[...], ka_ref[...], wup_ref[...], aup_ref[...])
        dkr, dxwa, dw0, da0, dkk, dka, dwup, daup = vjp(
            (dw_ref[...], dk_ref[...] + dk2_ref[...], dkap_ref[...], da_ref[...]))
        dps = jnp.concatenate([dr_ref[...] + dr2_ref[...], dkr, dv_ref[...] + dv2_ref[...], dg_ref[...], dxwa], axis=1)
        z = dps * mu_ref[...]
        dp_ref[...] = dps - z + _shift_up(z, zrow_scr[0:1, :])
        zrow_scr[0:1, :] = z[0:1, :]
        for ref, val in zip(accs, (_colsum(dps * (prev - p)), dw0, da0, dkk, dka, dwup, daup)):
            ref[...] += val

    rev = lambda i: (nt - 1 - i, 0)
    vec = _full((1, W))
    lora = _full((W, 2 * LORA))
    tile = pl.BlockSpec((TOK_TILE, W), rev)
    prev8 = pl.BlockSpec((8, RWKV_COLS), lambda i: (jnp.maximum((nt - 1 - i) * (TOK_TILE // 8) - 1, 0), 0))
    return pl.pallas_call(
        body, name="prep_bwd", grid=(nt,),
        in_specs=[pl.BlockSpec((TOK_TILE, RWKV_COLS), rev), prev8, _full((1, RWKV_COLS)), vec, vec, vec, vec, lora, lora,
                  _full((256, 128))] + [tile] * 10,
        out_specs=[pl.BlockSpec((TOK_TILE, RWKV_COLS), rev), _full((1, RWKV_COLS)), vec, vec, vec, vec, lora, lora],
        out_shape=[jax.ShapeDtypeStruct((T, RWKV_COLS), F32), jax.ShapeDtypeStruct((1, RWKV_COLS), F32)]
        + [jax.ShapeDtypeStruct((1, W), F32)] * 4 + [jax.ShapeDtypeStruct((W, 2 * LORA), F32)] * 2,
        scratch_shapes=[pltpu.VMEM((8, RWKV_COLS), F32)],
        compiler_params=_params(dimension_semantics=("arbitrary",)),
    )(p_rwkv, p_rwkv, mu, w0, a0, k_k, k_a, wup_pad, aup_pad, ones64, dr, dw, dk, dv, dkap, da, dg, dr2, dk2, dv2)


def _silu(x):
    return x * jax.nn.sigmoid(x)


def _post_y(o, r, k, v, g_rw, ret_raw, g_ret, ret_gn_g, gn_g, gn_b, r_k, avg128, avg64, ones64):
    xc = ret_raw - _head_mix(ret_raw, avg128)
    ret = xc * lax.rsqrt(_head_mix(xc * xc, avg128) + RET_GN_EPS)
    y_ret = _silu(g_ret) * (ret * ret_gn_g)
    oc = o - _head_mix(o, avg64)
    on = oc * lax.rsqrt(_head_mix(oc * oc, avg64) + RWKV_GN_EPS) * gn_g + gn_b
    bonus = _head_mix(r * k * r_k, ones64) * v
    y_rwkv = _silu(g_rw) * (on + bonus)
    return y_ret, y_rwkv


def _post_loss(h, final_g, target):
    err = _rmsnorm(h, final_g) - target
    return 0.5 * jnp.sum(jnp.mean(err * err, axis=-1))


def _post(o, r, k, v, g_rw, ret_raw, p_ret, x, target, ret_gn_g, gn_g, gn_b, r_k, final_g, w_out, avg128, avg64, ones64):
    T = x.shape[1]
    n_tok_out = 8

    def body(o_ref, r_ref, k_ref, v_ref, grw_ref, ret_ref, gret_ref, x_ref, tgt_ref, rg_ref, gg_ref, gb_ref, rk_ref, fg_ref,
             wo_ref, a128_ref, a64_ref, ones_ref, *outs):
        tok_outs, (dwo_ref, drg_ref, dgg_ref, dgb_ref, drk_ref, dfg_ref, loss_ref) = outs[:n_tok_out], outs[n_tok_out:]
        accs = (dwo_ref, drg_ref, dgg_ref, dgb_ref, drk_ref, dfg_ref, loss_ref)

        @pl.when(pl.program_id(0) == 0)
        def _():
            for ref in accs:
                ref[...] = jnp.zeros_like(ref)

        consts = (a128_ref[...], a64_ref[...], ones_ref[...])
        (y_ret, y_rwkv), vjp = jax.vjp(
            lambda *args: _post_y(*args, *consts), o_ref[...], r_ref[...], k_ref[...], v_ref[...], grw_ref[...], ret_ref[...],
            gret_ref[...], rg_ref[...], gg_ref[...], gb_ref[...], rk_ref[...])
        h = x_ref[...] + _dot_bf(y_ret, wo_ref[0:RET_WIDTH, :]) + _dot_bf(y_rwkv, wo_ref[RET_WIDTH:, :])
        loss, (dh, dfg) = jax.value_and_grad(_post_loss, argnums=(0, 1))(h, fg_ref[...], tgt_ref[...])
        dy_ret = _dot_nt_bf(dh, wo_ref[0:RET_WIDTH, :])
        dy_rwkv = _dot_nt_bf(dh, wo_ref[RET_WIDTH:, :])
        do, dr, dk, dv, dgrw, dret, dgret, drg, dgg, dgb, drk = vjp((dy_ret, dy_rwkv))
        for ref, val in zip(tok_outs, (dh, do, dr, dk, dv, dgrw, dret, dgret)):
            ref[...] = val
        dwo_ref[0:RET_WIDTH, :] += _dot_tn_bf(y_ret, dh)
        dwo_ref[RET_WIDTH:, :] += _dot_tn_bf(y_rwkv, dh)
        for ref, val in zip(accs[1:], (drg, dgg, dgb, drk, dfg, jnp.full((1, 128), loss, F32))):
            ref[...] += val

    tile = _rows(TOK_TILE, W)
    wide = _rows(TOK_TILE, D_MODEL)
    wide_of_one = _rows_of_one(TOK_TILE, D_MODEL)
    vec = _full((1, W))
    sq = _full((256, 128))
    return pl.pallas_call(
        body, name="post", grid=(T // TOK_TILE,),
        in_specs=[tile] * 6 + [pl.BlockSpec((TOK_TILE, W), lambda i: (i, 2)), wide_of_one, wide_of_one, vec, vec, vec, vec,
                               _full((1, D_MODEL)), _full((D_MODEL, D_MODEL)), sq, sq, sq],
        out_specs=[wide] + [tile] * 7 + [_full((D_MODEL, D_MODEL)), vec, vec, vec, vec, _full((1, D_MODEL)), _full((1, 128))],
        out_shape=[jax.ShapeDtypeStruct((T, D_MODEL), F32)] + [jax.ShapeDtypeStruct((T, W), F32)] * 7
        + [jax.ShapeDtypeStruct((D_MODEL, D_MODEL), F32)] + [jax.ShapeDtypeStruct((1, W), F32)] * 4
        + [jax.ShapeDtypeStruct((1, D_MODEL), F32), jax.ShapeDtypeStruct((1, 128), F32)],
        compiler_params=_params(dimension_semantics=("arbitrary",)),
    )(o, r, k, v, g_rw, ret_raw, p_ret, x, target, ret_gn_g, gn_g, gn_b, r_k, final_g, w_out, avg128, avg64, ones64)


def _inproj_bwd_x(x, norm_g, dp_qkv, dg_ret, dp_rwkv, dh, w_in_t):
    T = x.shape[1]
    n_qkv = 2 * RET_QK + RET_WIDTH

    def body(x_ref, g_ref, dqkv_ref, dgret_ref, drwkv_ref, dh_ref, w_ref, dx_ref, dg_ref):
        @pl.when(pl.program_id(0) == 0)
        def _():
            dg_ref[...] = jnp.zeros_like(dg_ref)

        _, vjp = jax.vjp(_rmsnorm, x_ref[...], g_ref[...])
        du = (_dot_bf(dqkv_ref[...], w_ref[0:n_qkv, :]) + _dot_bf(dgret_ref[...], w_ref[n_qkv:RET_COLS, :])
              + _dot_bf(drwkv_ref[...], w_ref[RET_COLS:, :]))
        dx, dg = vjp(du)
        dx_ref[...] = dx + dh_ref[...]
        dg_ref[...] += dg

    return pl.pallas_call(
        body, name="inproj_bwd_x", grid=(T // TOK_TILE,),
        in_specs=[_rows_of_one(TOK_TILE, D_MODEL), _full((1, D_MODEL)), _rows(TOK_TILE, n_qkv), _rows(TOK_TILE, RET_WIDTH),
                  _rows(TOK_TILE, RWKV_COLS), _rows(TOK_TILE, D_MODEL), _full((IN_COLS, D_MODEL))],
        out_specs=[_rows_of_one(TOK_TILE, D_MODEL), _full((1, D_MODEL))],
        out_shape=[jax.ShapeDtypeStruct((1, T, D_MODEL), F32), jax.ShapeDtypeStruct((1, D_MODEL), F32)],
        compiler_params=_params(dimension_semantics=("arbitrary",)),
    )(x, norm_g, dp_qkv, dg_ret, dp_rwkv, dh, w_in_t)


def _grad_w_in(u, dps):
    T = u.shape[0]
    widths = [dp.shape[1] for dp in dps]
    tile = 2 * TOK_TILE
    steps = T // tile

    def body(u_ref, *refs):
        dp_refs, (mine_ref, sib_ref, acc_ref) = refs[:len(dps)], refs[len(dps):]

        @pl.when(pl.program_id(0) == 0)
        def _():
            acc_ref[...] = jnp.zeros_like(acc_ref)

        row = 0
        for dp_ref, n in zip(dp_refs, widths):
            acc_ref[row:row + n, :] += _dot_tn_bf(dp_ref[...], u_ref[...])
            row += n

        @pl.when(pl.program_id(0) == steps - 1)
        def _():
            core = lax.axis_index("c")
            for dev in range(N_DEV):
                block = acc_ref[dev * SHARD_IN:(dev + 1) * SHARD_IN, :].astype(BF16)

                @pl.when(core == dev % 2)
                def _():
                    mine_ref[dev // 2] = block

                @pl.when(core != dev % 2)
                def _():
                    sib_ref[dev // 2] = block

    half = jax.ShapeDtypeStruct((N_DEV // 2, SHARD_IN, D_MODEL), BF16)
    return pl.pallas_call(
        body, name="grad_w_in", grid=(steps,),
        in_specs=[_rows(tile, D_MODEL)] + [_rows(tile, n) for n in widths],
        out_specs=[_full(half.shape), _full(half.shape)],
        out_shape=[half, half],
        scratch_shapes=[pltpu.VMEM((IN_COLS, D_MODEL), F32)],
        compiler_params=_params(dimension_semantics=("arbitrary",)),
    )(u, *dps)


def _pad_lora(w_up_t, first):
    z = jnp.zeros_like(w_up_t)
    return jnp.concatenate([w_up_t, z] if first else [z, w_up_t], axis=1)


def _local_grads(x, target, norm_g, w_in_t, ret_gn_g, mu, w_lora_up_t, w0, a_lora_up_t, a0, k_k, k_a, r_k, gn_g, gn_b,
                 w_out_bf, final_g):
    T = x.shape[1]
    tabs = _rope_tables(T) + _ret_tables()
    ones64 = _block_mix(128, RWKV_HEAD)
    avg64 = _block_mix(128, RWKV_HEAD, 1.0 / RWKV_HEAD)
    avg128 = _block_mix(128, RET_DV, 1.0 / RET_DV)
    wup_pad, aup_pad = _pad_lora(w_lora_up_t, True), _pad_lora(a_lora_up_t, False)

    p_ret, p_rwkv, u = _inproj(x, norm_g, w_in_t)
    ret_raw, s_saved = _ret_fwd(p_ret, tabs)
    r, w, k, v, kap, a, g_rw = _prep_fwd(p_rwkv, mu, w0, a0, k_k, k_a, wup_pad, aup_pad, ones64)
    o, s_all, sa_all = _wkv_fwd(r, w, k, v, kap, a)
    (dh, do, dr2, dk2, dv2, dgrw, dret, dgret, d_w_out, d_ret_gn_g, d_gn_g, d_gn_b, d_r_k, d_final_g, loss) = _post(
        o, r, k, v, g_rw, ret_raw, p_ret, x, target, ret_gn_g, gn_g, gn_b, r_k, final_g, w_out_bf, avg128, avg64, ones64)
    dr, dw, dk, dv, dkap, da = _wkv_bwd(r, w, k, v, kap, a, s_all, sa_all, do)
    dp_rwkv, d_mu, d_w0, d_a0, d_k_k, d_k_a, d_wup, d_aup = _prep_bwd(
        p_rwkv, mu, w0, a0, k_k, k_a, wup_pad, aup_pad, ones64, dr, dw, dk, dv, dkap, da, dgrw, dr2, dk2, dv2)
    dp_qkv = _ret_bwd(p_ret, s_saved, dret, tabs)
    dx, d_norm_g = _inproj_bwd_x(x, norm_g, dp_qkv, dgret, dp_rwkv, dh, w_in_t)
    d_w_in = _grad_w_in(u, [dp_qkv, dgret, dp_rwkv])
    grads = dict(norm_g=d_norm_g, w_in=d_w_in, ret_gn_g=d_ret_gn_g, rwkv_mu=d_mu, w_lora_up=d_wup[:, :LORA], w0=d_w0,
                 a_lora_up=d_aup[:, LORA:], a0=d_a0, k_k=d_k_k, k_a=d_k_a, r_k=d_r_k, rwkv_gn_g=d_gn_g, rwkv_gn_b=d_gn_b,
                 w_out=d_w_out, final_norm_g=d_final_g)
    return loss, dx, grads


def _mesh_pos():
    return lax.axis_index("x"), lax.axis_index("y"), lax.axis_index("c")


GATHER_CHUNK_ROWS = 160


def _all_gather(shards):
    n = len(shards)
    pieces = [(a, r0, min(GATHER_CHUNK_ROWS, s.shape[0] - r0), r0 // GATHER_CHUNK_ROWS)
              for a, s in enumerate(shards) for r0 in range(0, s.shape[0], GATHER_CHUNK_ROWS)]
    levels = max(p[3] for p in pieces) + 1

    def body(*refs):
        x_refs, out_refs = refs[:n], refs[n:2 * n]
        send_sems, recv_sems, local_sems = refs[2 * n:]
        x, y, c = _mesh_pos()
        me, sibling = (x, y, c), (x, y, 1 - c)
        chips = [(1 - x, y), (x, 1 - y), (1 - x, 1 - y)]

        def rows(u, pos):
            a, r0, size, _ = pieces[u]
            m = x_refs[a].shape[0]
            return out_refs[a].at[pl.ds((4 * pos[0] + 2 * pos[1] + pos[2]) * m + r0, size), :]

        def own(u):
            a, r0, size, _ = pieces[u]
            return x_refs[a].at[pl.ds(r0, size), :]

        def copy(u, k, block, to, src=None):
            return pltpu.make_async_remote_copy(
                src_ref=rows(u, block) if src is None else src, dst_ref=rows(u, block),
                send_sem=send_sems.at[u, k], recv_sem=recv_sems.at[u, k], device_id=to, device_id_type=MESH)

        units = range(len(pieces))
        mine = [pltpu.make_async_copy(own(u), rows(u, me), local_sems.at[u]) for u in units]
        for cp in mine:
            cp.start()
        first = []
        for u in units:
            first.append(copy(u, 0, me, sibling, src=own(u)))
            first += [copy(u, 1 + j, me, (*chip, c), src=own(u)) for j, chip in enumerate(chips)]
        for cp in first:
            cp.start()
        passed = []
        for level in range(levels):
            for j, chip in enumerate(chips):
                for u in units:
                    if pieces[u][3] == level:
                        copy(u, 1 + j, (*chip, c), me).wait_recv()
                        passed.append(copy(u, 4 + j, (*chip, c), sibling))
                        passed[-1].start()
        for u in units:
            copy(u, 0, sibling, me).wait_recv()
            for j, chip in enumerate(chips):
                copy(u, 4 + j, (*chip, 1 - c), me).wait_recv()
        for cp in first + passed:
            cp.wait_send()
        for cp in mine:
            cp.wait()

    vmem = pl.BlockSpec(memory_space=pltpu.VMEM)
    dma = pltpu.SemaphoreType.DMA
    return pl.pallas_call(
        body, name="gather_weights",
        out_shape=[jax.ShapeDtypeStruct((N_DEV * s.shape[0], s.shape[1]), s.dtype) for s in shards],
        in_specs=[vmem] * n, out_specs=[vmem] * n,
        scratch_shapes=[dma((len(pieces), 7)), dma((len(pieces), 7)), dma((len(pieces),))],
        compiler_params=_params(),
    )(*shards)


N_CHIP = 4


def _exchange_pairs(big, small):
    nb, ns = len(big), len(small)

    def body(*refs):
        big_in, small_in = refs[:nb], refs[nb:nb + ns]
        theirs, small_out = refs[nb + ns:2 * nb + ns], refs[2 * nb + ns:2 * nb + 2 * ns]
        pair_send, pair_recv, send_sems, recv_sems, local_sems = refs[2 * nb + 2 * ns:]
        x, y, c = _mesh_pos()
        me = 4 * x + 2 * y + c
        local = [pltpu.make_async_copy(small_in[a].at[me], small_out[a].at[me], local_sems.at[a]) for a in range(ns)]
        for cp in local:
            cp.start()
        copies = [pltpu.make_async_remote_copy(
            src_ref=big_in[a], dst_ref=theirs[a], send_sem=pair_send.at[a], recv_sem=pair_recv.at[a],
            device_id=(x, y, 1 - c), device_id_type=MESH) for a in range(nb)]
        for k in range(1, N_DEV):
            peer = (x ^ (k >> 2), y ^ ((k >> 1) & 1), c ^ (k & 1))
            peer_idx = 4 * peer[0] + 2 * peer[1] + peer[2]
            copies += [pltpu.make_async_remote_copy(
                src_ref=small_in[a].at[peer_idx], dst_ref=small_out[a].at[me], send_sem=send_sems.at[a, k - 1],
                recv_sem=recv_sems.at[a, k - 1], device_id=peer, device_id_type=MESH) for a in range(ns)]
        for cp in copies:
            cp.start()
        for cp in copies:
            cp.wait()
        for cp in local:
            cp.wait()

    hbm = pl.BlockSpec(memory_space=pl.ANY)
    out_shape = [jax.ShapeDtypeStruct(p.shape, p.dtype) for p in big + small]
    dma = pltpu.SemaphoreType.DMA
    res = pl.pallas_call(
        body, name="exchange_pairs", out_shape=out_shape,
        in_specs=[hbm] * (nb + ns), out_specs=[hbm] * len(out_shape),
        scratch_shapes=[dma((nb,)), dma((nb,)), dma((ns, 7)), dma((ns, 7)), dma((ns,))],
        compiler_params=_params(),
    )(*big, *small)
    return res[:nb], res[nb:]


def _pair_sum(name, mine, theirs, row_tile):
    _, rows, cols = mine.shape

    def body(a_ref, b_ref, o_ref):
        o_ref[...] = (a_ref[...].astype(F32) + b_ref[...].astype(F32)).astype(o_ref.dtype)

    spec = pl.BlockSpec((N_CHIP, row_tile, cols), lambda i: (0, i, 0))
    return pl.pallas_call(
        body, name=name, grid=(rows // row_tile,), in_specs=[spec, spec], out_specs=spec,
        out_shape=jax.ShapeDtypeStruct(mine.shape, mine.dtype),
        compiler_params=_params(dimension_semantics=("arbitrary",)),
    )(mine, theirs)


def _exchange_chips(parts):
    n = len(parts)

    def body(*refs):
        in_refs, out_refs = refs[:n], refs[n:2 * n]
        send_sems, recv_sems, local_sems = refs[2 * n:]
        x, y, c = _mesh_pos()
        my_chip = 2 * x + y
        own = [pltpu.make_async_copy(in_refs[a].at[my_chip], out_refs[a].at[my_chip], local_sems.at[a]) for a in range(n)]
        for cp in own:
            cp.start()
        copies = []
        for k in range(1, N_CHIP):
            px, py = x ^ (k >> 1), y ^ (k & 1)
            copies += [pltpu.make_async_remote_copy(
                src_ref=in_refs[a].at[2 * px + py], dst_ref=out_refs[a].at[my_chip], send_sem=send_sems.at[a, k - 1],
                recv_sem=recv_sems.at[a, k - 1], device_id=(px, py, c), device_id_type=MESH) for a in range(n)]
        for cp in copies:
            cp.start()
        for cp in copies:
            cp.wait()
        for cp in own:
            cp.wait()

    hbm = pl.BlockSpec(memory_space=pl.ANY)
    dma = pltpu.SemaphoreType.DMA
    return pl.pallas_call(
        body, name="exchange_chips",
        out_shape=[jax.ShapeDtypeStruct(p.shape, p.dtype) for p in parts],
        in_specs=[hbm] * n, out_specs=[hbm] * n,
        scratch_shapes=[dma((n, N_CHIP - 1)), dma((n, N_CHIP - 1)), dma((n,))],
        compiler_params=_params(),
    )(*parts)


def _adamw(w, g, m, v):
    m = ADAM_B1 * m + (1.0 - ADAM_B1) * g
    v = ADAM_B2 * v + (1.0 - ADAM_B2) * (g * g)
    m_hat = m / (1.0 - ADAM_B1 ** ADAM_STEP)
    v_hat = v / (1.0 - ADAM_B2 ** ADAM_STEP)
    return -ADAM_LR * (m_hat / (jnp.sqrt(v_hat) + ADAM_EPS) + ADAM_WD * w), m, v


def _sum_parts(name, parts, row_tile):
    n_parts, rows, cols = parts.shape

    def body(p_ref, g_ref):
        g = p_ref[0].astype(F32)
        for s in range(1, n_parts):
            g = g + p_ref[s].astype(F32)
        g_ref[...] = g

    return pl.pallas_call(
        body, name=name, grid=(rows // row_tile,),
        in_specs=[pl.BlockSpec((n_parts, row_tile, cols), lambda i: (0, i, 0))],
        out_specs=pl.BlockSpec((row_tile, cols), lambda i: (i, 0)),
        out_shape=jax.ShapeDtypeStruct((rows, cols), F32),
        compiler_params=_params(dimension_semantics=("arbitrary",)),
    )(parts)


def _adamw_apply(name, g, w, m, v, row_tile):
    _, rows, cols = w.shape

    def body(g_ref, w_ref, m_ref, v_ref, d_ref, nm_ref, nv_ref):
        d_ref[0], nm_ref[0], nv_ref[0] = _adamw(w_ref[0], g_ref[0], m_ref[0], v_ref[0])

    tile = pl.BlockSpec((1, row_tile, cols), lambda i: (0, i, 0))
    return pl.pallas_call(
        body, name=name, grid=(rows // row_tile,), in_specs=[tile] * 4, out_specs=[tile] * 3,
        out_shape=[jax.ShapeDtypeStruct((1, rows, cols), F32)] * 3,
        compiler_params=_params(dimension_semantics=("arbitrary",)),
    )(g, w, m, v)


def _reduce_adamw_2d(name, parts, w, m, v, row_tile):
    n_parts, rows, cols = parts.shape

    def body(p_ref, w_ref, m_ref, v_ref, g_ref, d_ref, nm_ref, nv_ref):
        g = p_ref[0].astype(F32)
        for s in range(1, n_parts):
            g = g + p_ref[s].astype(F32)
        g_ref[...] = g
        d_ref[...], nm_ref[...], nv_ref[...] = _adamw(w_ref[...], g, m_ref[...], v_ref[...])

    tile = pl.BlockSpec((row_tile, cols), lambda i: (i, 0))
    return pl.pallas_call(
        body, name=name, grid=(rows // row_tile,),
        in_specs=[pl.BlockSpec((n_parts, row_tile, cols), lambda i: (0, i, 0)), tile, tile, tile],
        out_specs=[tile] * 4,
        out_shape=[jax.ShapeDtypeStruct((rows, cols), F32)] * 4,
        compiler_params=_params(dimension_semantics=("arbitrary",)),
    )(parts, w, m, v)


def _reduce_adamw_t(name, parts_t, w, m, v, sum_tile, row_tile):
    g = _sum_parts(name + "_sum", parts_t, sum_tile).T[None]
    return [g] + list(_adamw_apply(name, g, w, m, v, row_tile))


def _reduce_adamw(name, parts, w, m, v, row_tile):
    n_parts, rows, cols = parts.shape

    def body(p_ref, w_ref, m_ref, v_ref, g_ref, d_ref, nm_ref, nv_ref):
        g = p_ref[0].astype(F32)
        for s in range(1, n_parts):
            g = g + p_ref[s].astype(F32)
        g_ref[0] = g
        d_ref[0], nm_ref[0], nv_ref[0] = _adamw(w_ref[0], g, m_ref[0], v_ref[0])

    tile = pl.BlockSpec((1, row_tile, cols), lambda i: (0, i, 0))
    return pl.pallas_call(
        body, name=name, grid=(rows // row_tile,),
        in_specs=[pl.BlockSpec((n_parts, row_tile, cols), lambda i: (0, i, 0)), tile, tile, tile],
        out_specs=[tile] * 4,
        out_shape=[jax.ShapeDtypeStruct((1, rows, cols), F32)] * 4,
        compiler_params=_params(dimension_semantics=("arbitrary",)),
    )(parts, w, m, v)


_SMALL = (("norm_g", 1024), ("ret_gn_g", 512), ("rwkv_mu", 2176), ("w0", 512), ("a0", 512), ("k_k", 512), ("k_a", 512),
          ("r_k", 512), ("rwkv_gn_g", 512), ("rwkv_gn_b", 512), ("final_norm_g", 1024))
_SMALL_LANES = sum(n for _, n in _SMALL) + 128
_WEIGHTS = ("norm_g", "w_in", "ret_gn_g", "rwkv_mu", "w_lora_up", "w0", "a_lora_up", "a0", "k_k", "k_a", "r_k", "rwkv_gn_g",
            "rwkv_gn_b", "w_out", "final_norm_g")


def _adamw_vectors(parts, wts, mom, var):
    k = len(_SMALL)

    def body(p_ref, *refs):
        w_refs, m_refs, v_refs, outs = refs[:k], refs[k:2 * k], refs[2 * k:3 * k], refs[3 * k:]
        g_all = p_ref[0]
        for s in range(1, N_DEV):
            g_all = g_all + p_ref[s]
        off = 0
        for i, (name, n) in enumerate(_SMALL):
            g = g_all[:, off:off + n]
            off += n
            if name == "r_k":
                g = jnp.concatenate([g[:, RWKV_HEAD * h:RWKV_HEAD * (h + 1)] for h in range(RWKV_HEADS)], axis=0)[None]
            outs[4 * i][...] = g
            outs[4 * i + 1][...], outs[4 * i + 2][...], outs[4 * i + 3][...] = _adamw(
                w_refs[i][...], g, m_refs[i][...], v_refs[i][...])
        outs[4 * k][...] = g_all[:, off:off + 128]

    vmem = pl.BlockSpec(memory_space=pltpu.VMEM)
    shapes = [jax.ShapeDtypeStruct(wts[n].shape, F32) for n, _ in _SMALL for _ in range(4)] + [jax.ShapeDtypeStruct((1, 128), F32)]
    res = pl.pallas_call(
        body, name="adamw_vectors", out_shape=shapes,
        in_specs=[vmem] * (1 + 3 * k), out_specs=[vmem] * len(shapes), compiler_params=_params(),
    )(parts, *[wts[n] for n, _ in _SMALL], *[mom[n] for n, _ in _SMALL], *[var[n] for n, _ in _SMALL])
    return {n: res[4 * i:4 * i + 4] for i, (n, _) in enumerate(_SMALL)}, res[4 * k]


def kernel(x, norm_g, w_in, ret_gn_g, rwkv_mu, w_lora_up, w0, a_lora_up, a0, k_k, k_a, r_k, rwkv_gn_g, rwkv_gn_b, w_out, final_norm_g, loss_target, m_norm_g, m_w_in, m_ret_gn_g, m_rwkv_mu, m_w_lora_up, m_w0, m_a_lora_up, m_a0, m_k_k, m_k_a, m_r_k, m_rwkv_gn_g, m_rwkv_gn_b, m_w_out, m_final_norm_g, v_norm_g, v_w_in, v_ret_gn_g, v_rwkv_mu, v_w_lora_up, v_w0, v_a_lora_up, v_a0, v_k_k, v_k_a, v_r_k, v_rwkv_gn_g, v_rwkv_gn_b, v_w_out, v_final_norm_g):
    wts = dict(norm_g=norm_g, w_in=w_in, ret_gn_g=ret_gn_g, rwkv_mu=rwkv_mu, w_lora_up=w_lora_up, w0=w0, a_lora_up=a_lora_up,
               a0=a0, k_k=k_k, k_a=k_a, r_k=r_k, rwkv_gn_g=rwkv_gn_g, rwkv_gn_b=rwkv_gn_b, w_out=w_out,
               final_norm_g=final_norm_g)
    mom = dict(norm_g=m_norm_g, w_in=m_w_in, ret_gn_g=m_ret_gn_g, rwkv_mu=m_rwkv_mu, w_lora_up=m_w_lora_up, w0=m_w0,
               a_lora_up=m_a_lora_up, a0=m_a0, k_k=m_k_k, k_a=m_k_a, r_k=m_r_k, rwkv_gn_g=m_rwkv_gn_g,
               rwkv_gn_b=m_rwkv_gn_b, w_out=m_w_out, final_norm_g=m_final_norm_g)
    var = dict(norm_g=v_norm_g, w_in=v_w_in, ret_gn_g=v_ret_gn_g, rwkv_mu=v_rwkv_mu, w_lora_up=v_w_lora_up, w0=v_w0,
               a_lora_up=v_a_lora_up, a0=v_a0, k_k=v_k_k, k_a=v_k_a, r_k=v_r_k, rwkv_gn_g=v_rwkv_gn_g,
               rwkv_gn_b=v_rwkv_gn_b, w_out=v_w_out, final_norm_g=v_final_norm_g)
    shapes = {n: wts[n].shape for n in _WEIGHTS}

    w_in_t, w_out_bf, wup_t, aup_t = _all_gather(
        [w_in[0].T.astype(BF16), w_out[0].astype(BF16), w_lora_up[0].T, a_lora_up[0].T])

    loss, dx, g = _local_grads(
        x, loss_target, norm_g, w_in_t, ret_gn_g, rwkv_mu, wup_t, w0, aup_t, a0, k_k, k_a,
        r_k.reshape(1, W), rwkv_gn_g, rwkv_gn_b, w_out_bf, final_norm_g.reshape(1, D_MODEL))

    small = jnp.concatenate([g[n] for n, _ in _SMALL] + [loss], axis=1)
    core = lax.axis_index("c")
    by_core = lambda t: [lax.dynamic_index_in_dim(t, i, axis=1, keepdims=False) for i in (core, 1 - core)]
    in_mine, in_sib = g["w_in"]
    out_mine, out_sib = by_core(g["w_out"].reshape(N_CHIP, 2, SHARD_OUT, D_MODEL).astype(BF16))
    (in_theirs, out_theirs), parts = _exchange_pairs(
        [in_sib, out_sib],
        [g["w_lora_up"].reshape(N_DEV, SHARD_LORA, LORA), g["a_lora_up"].reshape(N_DEV, SHARD_LORA, LORA),
         jnp.broadcast_to(small[None], (N_DEV, 1, _SMALL_LANES))])
    by_chip = _exchange_chips([_pair_sum("pair_sum_w_in", in_mine, in_theirs, SHARD_IN // 2),
                               _pair_sum("pair_sum_w_out", out_mine, out_theirs, SHARD_OUT)])
    res = {}
    res["w_in"] = [t.T[None] for t in _reduce_adamw_2d(
        "adamw_w_in", by_chip[0], w_in[0].T, m_w_in[0].T, v_w_in[0].T, SHARD_IN // 2)]
    res["w_out"] = _reduce_adamw("adamw_w_out", by_chip[1], w_out, m_w_out, v_w_out, SHARD_OUT)
    res["w_lora_up"] = _reduce_adamw_t("adamw_w_lora_up", parts[0], w_lora_up, m_w_lora_up, v_w_lora_up, LORA, LORA)
    res["a_lora_up"] = _reduce_adamw_t("adamw_a_lora_up", parts[1], a_lora_up, m_a_lora_up, v_a_lora_up, LORA, LORA)
    as_row = lambda d: {n: d[n] if d[n].ndim > 1 else d[n].reshape(1, size) for n, size in _SMALL}
    vec, loss_row = _adamw_vectors(parts[2], as_row(wts), as_row(mom), as_row(var))
    res.update(vec)
    res = {n: [t.reshape(shapes[n]) for t in res[n]] for n in _WEIGHTS}
    return (loss_row[0, 0], dx, *[res[n][0] for n in _WEIGHTS], *[res[n][1] for n in _WEIGHTS],
            *[res[n][2] for n in _WEIGHTS], *[res[n][3] for n in _WEIGHTS])
```

```python
import numpy as np
import jax
import jax.numpy as jnp
from jax import lax
from jax.experimental import pallas as pl
from jax.experimental.pallas import tpu as pltpu

F32 = jnp.float32
BF16 = jnp.bfloat16

D_MODEL = 1024
CHUNK = 64
RET_HEADS = 4
RET_DV = 128
RET_DK = 64
RET_QK = 256
RET_WIDTH = 512
RWKV_WIDTH = 512
RWKV_HEAD = 64
RWKV_HEADS = 8
LORA = 64
RET_COLS = 2 * RET_QK + 2 * RET_WIDTH
RWKV_COLS = 4 * RWKV_WIDTH + 2 * LORA
IN_COLS = RET_COLS + RWKV_COLS
ROPE_BASE = 10000.0
RMS_EPS = 1e-6
RET_GN_EPS = 1e-5
RWKV_GN_EPS = 64e-5
ADAM_LR = 0.001
ADAM_B1 = 0.9
ADAM_B2 = 0.999
ADAM_EPS = 1e-08
ADAM_WD = 0.01
ADAM_STEP = 10
N_DEV = 8
SHARD_IN = IN_COLS // N_DEV
SHARD_OUT = D_MODEL // N_DEV
SHARD_LORA = RWKV_WIDTH // N_DEV
VMEM_LIMIT = 56 * 1024 * 1024
TOK_TILE = 256
WKV_CHUNK = 64

MESH = pl.DeviceIdType.MESH


def _dot_bf(a, b):
    return jnp.dot(a.astype(BF16), b.astype(BF16), preferred_element_type=F32)


def _dot_nt_bf(a, b):
    return lax.dot_general(a.astype(BF16), b.astype(BF16), (((1,), (1,)), ((), ())), preferred_element_type=F32)


def _dot_tn_bf(a, b):
    return lax.dot_general(a.astype(BF16), b.astype(BF16), (((0,), (0,)), ((), ())), preferred_element_type=F32)


@jax.custom_vjp
def _mm(a, b):
    return _dot_bf(a, b)


@jax.custom_vjp
def _mm_nt(a, b):
    return _dot_nt_bf(a, b)


@jax.custom_vjp
def _mm_tn(a, b):
    return _dot_tn_bf(a, b)


_mm.defvjp(lambda a, b: (_dot_bf(a, b), (a, b)), lambda res, g: (_dot_nt_bf(g, res[1]), _dot_tn_bf(res[0], g)))
_mm_nt.defvjp(lambda a, b: (_dot_nt_bf(a, b), (a, b)), lambda res, g: (_dot_bf(g, res[1]), _dot_tn_bf(g, res[0])))
_mm_tn.defvjp(lambda a, b: (_dot_tn_bf(a, b), (a, b)), lambda res, g: (_dot_nt_bf(res[1], g), _dot_bf(res[0], g)))


def _trunc(x):
    return lax.bitcast_convert_type(lax.bitcast_convert_type(x, jnp.uint32) & jnp.uint32(0xFFFF0000), F32)


def _two_piece(x):
    hi = _trunc(x)
    return jnp.concatenate([hi, x - hi], axis=1)


def _mix_raw(x, mat2):
    return _unstack(jnp.dot(_two_piece(_stack(x)), mat2, preferred_element_type=F32))


@jax.custom_vjp
def _head_mix(x, mat2):
    return _mix_raw(x, mat2)


_head_mix.defvjp(lambda x, mat2: (_mix_raw(x, mat2), mat2), lambda mat2, g: (_mix_raw(g, mat2), jnp.zeros_like(mat2)))


def _swap_halves(x):
    lane = lax.broadcasted_iota(jnp.int32, x.shape, 1)
    return jnp.where((lane & (RET_DK - 1)) < RET_DK // 2, pltpu.roll(x, RET_QK - RET_DK // 2, axis=1),
                     pltpu.roll(x, RET_DK // 2, axis=1))


@jax.custom_vjp
def _rot(x):
    return _swap_halves(x)


_rot.defvjp(lambda x: (_swap_halves(x), None), lambda _, g: (_swap_halves(g),))


def _params(**kw):
    return pltpu.CompilerParams(vmem_limit_bytes=VMEM_LIMIT, **kw)


def _full(shape):
    nd = len(shape)
    return pl.BlockSpec(shape, lambda i, _nd=nd: (0,) * _nd)


def _rows(tile, width):
    return pl.BlockSpec((tile, width), lambda i: (i, 0))


def _rows_of_one(tile, width):
    return pl.BlockSpec((None, tile, width), lambda i: (0, i, 0))


def _block_mix(n, blk, scale=1.0):
    idx = np.arange(n) // blk
    m = (idx[:, None] == idx[None, :]).astype(np.float32) * scale
    return jnp.asarray(np.concatenate([m, m], axis=0))


def _rope_tables(T):
    half = RET_DK // 2
    expo = -np.arange(half, dtype=np.float32) / np.float32(half)
    freqs = np.exp(expo * np.float32(np.log(ROPE_BASE))).astype(np.float32)
    ang = np.arange(T, dtype=np.float32)[:, None] * freqs[None, :]
    cos, sin = np.cos(ang).astype(np.float32), np.sin(ang).astype(np.float32)
    cos_h = np.concatenate([cos, cos], axis=1)
    sin_h = np.concatenate([-sin, sin], axis=1)
    cos_t = np.tile(cos_h, (1, RET_HEADS))
    sin_t = np.tile(sin_h, (1, RET_HEADS))
    return jnp.asarray(cos_t), jnp.asarray(sin_t)


def _ret_tables():
    h = np.arange(RET_HEADS, dtype=np.float32)
    lg = np.log(1.0 - np.exp2(-5.0 - h)).astype(np.float32)
    idx = np.arange(CHUNK, dtype=np.float32)
    intra = np.exp(lg[:, None, None] * np.abs(idx[:, None] - idx[None, :])).astype(np.float32)
    q_dec = np.exp(lg[:, None] * (idx[None, :] + 1.0)).astype(np.float32)
    k_dec = np.exp(lg[:, None] * (CHUNK - 1.0 - idx[None, :])).astype(np.float32)
    chunk_dec = np.exp(lg * CHUNK).astype(np.float32)
    lane_head = np.arange(RET_QK) // RET_DK
    mask = (lane_head[None, :] == np.arange(RET_HEADS)[:, None]).astype(np.float32)
    m = np.broadcast_to(mask[:, None, :], (RET_HEADS, CHUNK, RET_QK)).copy()
    qd = m * q_dec[:, :, None]
    kd = m * k_dec[:, :, None]
    return jnp.asarray(intra), jnp.asarray(m), jnp.asarray(qd), jnp.asarray(kd), [float(c) for c in chunk_dec]


def _rmsnorm(x, g):
    return x * lax.rsqrt(jnp.mean(x * x, axis=-1, keepdims=True) + RMS_EPS) * g


def _inproj(x, norm_g, w_in_t):
    T = x.shape[1]

    def body(x_ref, g_ref, w_ref, pr_ref, pw_ref, u_ref):
        ub = _rmsnorm(x_ref[...], g_ref[...]).astype(BF16)
        u_ref[...] = ub
        pr_ref[...] = _dot_nt_bf(ub, w_ref[:RET_COLS, :])
        pw_ref[...] = _dot_nt_bf(ub, w_ref[RET_COLS:, :])

    return pl.pallas_call(
        body, name="inproj", grid=(T // TOK_TILE,),
        in_specs=[_rows_of_one(TOK_TILE, D_MODEL), _full((1, D_MODEL)), _full((IN_COLS, D_MODEL))],
        out_specs=[_rows(TOK_TILE, RET_COLS), _rows(TOK_TILE, RWKV_COLS), _rows(TOK_TILE, D_MODEL)],
        out_shape=[jax.ShapeDtypeStruct((T, RET_COLS), F32), jax.ShapeDtypeStruct((T, RWKV_COLS), F32),
                   jax.ShapeDtypeStruct((T, D_MODEL), BF16)],
        compiler_params=_params(dimension_semantics=("arbitrary",)),
    )(x, norm_g, w_in_t)


def _ret_chunk(pq, pk, v_heads, s_heads, cos_t, sin_t, dec, hm, qd, kd, chunk_dec):
    q = pq * cos_t + _rot(pq) * sin_t
    k = (pk * cos_t + _rot(pk) * sin_t) * (RET_DK ** -0.5)
    outs, s_out = [], []
    for h in range(RET_HEADS):
        sc = _mm_nt(q * hm[h], k * hm[h]) * dec[h]
        intra = _mm(sc, v_heads[h])
        kv = _mm_tn(k * kd[h], v_heads[h])
        inter = _mm(q * qd[h], s_heads[h])
        outs.append(intra + inter)
        s_out.append(s_heads[h] * chunk_dec[h] + kv)
    return tuple(outs), tuple(s_out)


def _ret_specs():
    const = [_full((RET_HEADS, CHUNK, CHUNK)), _full((RET_HEADS, CHUNK, RET_QK)),
             _full((RET_HEADS, CHUNK, RET_QK)), _full((RET_HEADS, CHUNK, RET_QK))]
    return const


RET_GROUP = 4


def _ret_fwd(p_ret, tabs):
    T = p_ret.shape[0]
    G = RET_GROUP
    ng = T // (CHUNK * G)
    cos_t, sin_t, dec, hm, qd, kd, chunk_dec = tabs

    def body(p_ref, cos_ref, sin_ref, dec_ref, hm_ref, qd_ref, kd_ref, out_ref, sin_save_ref, s_scr):
        @pl.when(pl.program_id(0) == 0)
        def _():
            s_scr[...] = jnp.zeros_like(s_scr)

        consts = (dec_ref[...], hm_ref[...], qd_ref[...], kd_ref[...])
        s_heads = tuple(s_scr[h] for h in range(RET_HEADS))
        for c in range(G):
            rows = pl.ds(c * CHUNK, CHUNK)
            for h in range(RET_HEADS):
                sin_save_ref[c, h] = s_heads[h]
            v_heads = tuple(p_ref[rows, 2 * RET_QK + RET_DV * h:2 * RET_QK + RET_DV * (h + 1)] for h in range(RET_HEADS))
            outs, s_heads = _ret_chunk(p_ref[rows, 0:RET_QK], p_ref[rows, RET_QK:2 * RET_QK], v_heads, s_heads,
                                       cos_ref[rows, :], sin_ref[rows, :], *consts, chunk_dec)
            for h in range(RET_HEADS):
                out_ref[rows, RET_DV * h:RET_DV * (h + 1)] = outs[h]
        for h in range(RET_HEADS):
            s_scr[h] = s_heads[h]

    tok = CHUNK * G
    return pl.pallas_call(
        body, name="ret_fwd", grid=(ng,),
        in_specs=[pl.BlockSpec((tok, RET_COLS), lambda i: (i, 0)), _rows(tok, RET_QK), _rows(tok, RET_QK)] + _ret_specs(),
        out_specs=[_rows(tok, RET_WIDTH), pl.BlockSpec((G, RET_HEADS, RET_QK, RET_DV), lambda i: (i, 0, 0, 0))],
        out_shape=[jax.ShapeDtypeStruct((T, RET_WIDTH), F32),
                   jax.ShapeDtypeStruct((T // CHUNK, RET_HEADS, RET_QK, RET_DV), F32)],
        scratch_shapes=[pltpu.VMEM((RET_HEADS, RET_QK, RET_DV), F32)],
        compiler_params=_params(dimension_semantics=("arbitrary",)),
    )(p_ret, cos_t, sin_t, dec, hm, qd, kd)


def _ret_bwd(p_ret, s_saved, d_ret, tabs):
    T = p_ret.shape[0]
    G = RET_GROUP
    ng = T // (CHUNK * G)
    cos_t, sin_t, dec, hm, qd, kd, chunk_dec = tabs

    def body(p_ref, s_ref, dret_ref, cos_ref, sin_ref, dec_ref, hm_ref, qd_ref, kd_ref, dp_ref, ds_scr):
        @pl.when(pl.program_id(0) == 0)
        def _():
            ds_scr[...] = jnp.zeros_like(ds_scr)

        consts = (dec_ref[...], hm_ref[...], qd_ref[...], kd_ref[...])
        d_s = tuple(ds_scr[h] for h in range(RET_HEADS))
        for c in reversed(range(G)):
            rows = pl.ds(c * CHUNK, CHUNK)
            v_heads = tuple(p_ref[rows, 2 * RET_QK + RET_DV * h:2 * RET_QK + RET_DV * (h + 1)] for h in range(RET_HEADS))
            s_heads = tuple(s_ref[c, h] for h in range(RET_HEADS))
            tables = (cos_ref[rows, :], sin_ref[rows, :]) + consts
            _, vjp = jax.vjp(lambda a, b, c_, d: _ret_chunk(a, b, c_, d, *tables, chunk_dec),
                             p_ref[rows, 0:RET_QK], p_ref[rows, RET_QK:2 * RET_QK], v_heads, s_heads)
            d_out = tuple(dret_ref[rows, RET_DV * h:RET_DV * (h + 1)] for h in range(RET_HEADS))
            dq, dk, dv, d_s = vjp((d_out, d_s))
            dp_ref[rows, 0:RET_QK] = dq
            dp_ref[rows, RET_QK:2 * RET_QK] = dk
            for h in range(RET_HEADS):
                dp_ref[rows, 2 * RET_QK + RET_DV * h:2 * RET_QK + RET_DV * (h + 1)] = dv[h]
        for h in range(RET_HEADS):
            ds_scr[h] = d_s[h]

    tok = CHUNK * G
    rev = lambda i: (ng - 1 - i, 0)
    return pl.pallas_call(
        body, name="ret_bwd", grid=(ng,),
        in_specs=[pl.BlockSpec((tok, RET_COLS), rev),
                  pl.BlockSpec((G, RET_HEADS, RET_QK, RET_DV), lambda i: (ng - 1 - i, 0, 0, 0)),
                  pl.BlockSpec((tok, RET_WIDTH), rev), pl.BlockSpec((tok, RET_QK), rev), pl.BlockSpec((tok, RET_QK), rev)]
        + _ret_specs(),
        out_specs=pl.BlockSpec((tok, 2 * RET_QK + RET_WIDTH), rev),
        out_shape=jax.ShapeDtypeStruct((T, 2 * RET_QK + RET_WIDTH), F32),
        scratch_shapes=[pltpu.VMEM((RET_HEADS, RET_QK, RET_DV), F32)],
        compiler_params=_params(dimension_semantics=("arbitrary",)),
    )(p_ret, s_saved, d_ret, cos_t, sin_t, dec, hm, qd, kd)


def _block_ones():
    r = lax.broadcasted_iota(jnp.int32, (3 * 128, 128), 0)
    c = lax.broadcasted_iota(jnp.int32, (3 * 128, 128), 1)
    return (((r & 127) >> 6) == (c >> 6)).astype(BF16)


def _stack(x):
    return jnp.concatenate([x[:, 128 * p:128 * (p + 1)] for p in range(4)], axis=0)


def _unstack(y):
    n = y.shape[0] // 4
    return jnp.concatenate([y[n * p:n * (p + 1)] for p in range(4)], axis=1)


def _split(x, n):
    pieces = []
    for _ in range(n):
        p = x.astype(BF16)
        pieces.append(p)
        x = x - p.astype(F32)
    return pieces


def _lane_sum(x, ones):
    return _unstack(jnp.dot(_stack(x).astype(BF16), ones[:128], preferred_element_type=F32))


def _colsum(x):
    return jnp.sum(x, axis=0, keepdims=True)


def _rows_times(vecs, mat):
    n = vecs.shape[0]
    lane = lax.broadcasted_iota(jnp.int32, (n, 128), 1)
    tiles = []
    for p in range(4):
        lhs = jnp.concatenate([vecs[:, 128 * p:128 * p + RWKV_HEAD], vecs[:, 128 * p + RWKV_HEAD:128 * (p + 1)]], axis=0)
        out = jnp.dot(lhs, mat[:, 128 * p:128 * (p + 1)], preferred_element_type=F32)
        tiles.append(jnp.where(lane < RWKV_HEAD, out[:n], out[n:]))
    return jnp.concatenate(tiles, axis=1)


def _expand_cols(xt, t):
    lane = lax.broadcasted_iota(jnp.int32, (RWKV_HEAD, 128), 1)
    tiles = []
    for p in range(4):
        lo = jnp.broadcast_to(xt[128 * p:128 * p + RWKV_HEAD, t:t + 1], (RWKV_HEAD, 128))
        hi = jnp.broadcast_to(xt[128 * p + RWKV_HEAD:128 * (p + 1), t:t + 1], (RWKV_HEAD, 128))
        tiles.append(jnp.where(lane < RWKV_HEAD, lo, hi))
    return jnp.concatenate(tiles, axis=1)


def _keep_step(acc_ref, x, t):
    lane = lax.broadcasted_iota(jnp.int32, (1, RWKV_WIDTH), 1)
    mask = jnp.broadcast_to((lane & (RWKV_HEAD - 1)) == t, x.shape)
    pltpu.store(acc_ref, x, mask=mask)


def _steps_to_rows(acc):
    assert WKV_CHUNK == RWKV_HEAD
    tiles = []
    for p in range(4):
        tt = acc[:, 128 * p:128 * (p + 1)].T
        tiles.append(jnp.concatenate([tt[:RWKV_HEAD], tt[RWKV_HEAD:]], axis=1))
    return jnp.concatenate(tiles, axis=1)


def _head_sums(x, ones):
    return _unstack(jnp.dot(jnp.concatenate(_split(_stack(x), 3), axis=1), ones, preferred_element_type=F32))


def _wkv_fwd(r, w, k, v, kap, a):
    T = r.shape[0]
    C = WKV_CHUNK
    nc = T // C

    def body(r_ref, w_ref, k_ref, v_ref, kap_ref, a_ref, o_ref, s_all_ref, sa_rows_ref, s_scr, o_acc, sa_acc):
        @pl.when(pl.program_id(0) == 0)
        def _():
            s_scr[...] = jnp.zeros_like(s_scr)

        ones = _block_ones()
        rr, ww, kk, vv, kap_, aa = (ref[...] for ref in (r_ref, w_ref, k_ref, v_ref, kap_ref, a_ref))
        bb = kap_ * aa
        c1 = _head_sums(pltpu.roll(bb, 1, axis=0) * kap_, ones)
        row = lambda x, t: x[t:t + 1]

        v_cols = vv.T

        s_prev = s_scr[...]
        sa = _lane_sum(s_prev * (-row(kap_, 0)), ones)
        ls = None

        def emit_o(t, s_t):
            _keep_step(o_acc, _lane_sum(s_t * row(rr, t), ones), t)

        for t in range(C):
            u = s_prev * row(ww, t) + _expand_cols(v_cols, t) * row(kk, t)
            if t > 0:
                sa = ls - sa * row(c1, t)
            if t + 1 < C:
                ls = _lane_sum(u * (-row(kap_, t + 1)), ones)
            if t > 0:
                emit_o(t - 1, s_prev)
            s_prev = u + sa * row(bb, t)
            s_all_ref[t] = s_prev
            _keep_step(sa_acc, sa, t)
        emit_o(C - 1, s_prev)
        s_scr[...] = s_prev
        o_ref[...] = _steps_to_rows(o_acc[...])
        sa_rows_ref[...] = _steps_to_rows(sa_acc[...])

    spec = _rows(C, RWKV_WIDTH)
    return pl.pallas_call(
        body, name="wkv_fwd", grid=(nc,),
        in_specs=[spec] * 6,
        out_specs=[spec, pl.BlockSpec((C, RWKV_HEAD, RWKV_WIDTH), lambda i: (i, 0, 0)), spec],
        out_shape=[jax.ShapeDtypeStruct((T, RWKV_WIDTH), F32), jax.ShapeDtypeStruct((T, RWKV_HEAD, RWKV_WIDTH), F32),
                   jax.ShapeDtypeStruct((T, RWKV_WIDTH), F32)],
        scratch_shapes=[pltpu.VMEM((RWKV_HEAD, RWKV_WIDTH), F32)] * 3,
        compiler_params=_params(dimension_semantics=("arbitrary",)),
    )(r, w, k, v, kap, a)


def _wkv_bwd(r, w, k, v, kap, a, s_all, sa_rows, d_o):
    T = r.shape[0]
    C = WKV_CHUNK
    nc = T // C

    def body(r_ref, w_ref, k_ref, v_ref, kap_ref, a_ref, sa_rows_ref, do_ref, s_ref, s_before_ref,
             dr_ref, dw_ref, dk_ref, dv_ref, dkap_ref, da_ref, ds_scr, dv_acc, dsa_acc):
        first_chunk = pl.program_id(0) == nc - 1

        @pl.when(pl.program_id(0) == 0)
        def _():
            ds_scr[...] = jnp.zeros_like(ds_scr)

        ones = _block_ones()
        rr, ww, kk, vv, kap_, aa, sar, dd = (ref[...] for ref in (r_ref, w_ref, k_ref, v_ref, kap_ref, a_ref, sa_rows_ref, do_ref))
        bb = kap_ * aa
        e1 = _head_sums(pltpu.roll(kap_, C - 1, axis=0) * bb, ones)
        row = lambda x, t: x[t:t + 1]

        def state_before(t):
            return s_ref[t - 1] if t > 0 else jnp.where(first_chunk, 0.0, s_before_ref[0])

        do_cols = dd.T

        d_sn, dsa, rows = None, None, [None] * C

        def emit_rows(t, d_sn_t, dsa_t):
            _keep_step(dv_acc, _lane_sum(d_sn_t * row(kk, t), ones), t)
            _keep_step(dsa_acc, dsa_t, t)
            dk_db = _rows_times(jnp.concatenate([row(vv, t), row(sar, t)], axis=0), d_sn_t)
            db = dk_db[1:2]
            rows[t] = (_colsum(d_sn_t * state_before(t)), dk_db[0:1], db * row(aa, t), db * row(kap_, t))
            if t % 8 == 0:
                for j, ref in enumerate((dw_ref, dk_ref, dkap_ref, da_ref)):
                    ref[t:t + 8, :] = jnp.concatenate([rows[u][j] for u in range(t, t + 8)], axis=0)

        def state_rows():
            dsa_rows = _steps_to_rows(dsa_acc[...])
            d_r, d_kap = [], []
            for j in range(-1, C):
                lhs = ([row(dd, j)] if j >= 0 else []) + ([row(dsa_rows, j + 1)] if j + 1 < C else [])
                out = _rows_times(jnp.concatenate(lhs, axis=0), state_before(j + 1))
                if j >= 0:
                    d_r.append(out[0:1])
                if j + 1 < C:
                    d_kap.append(out[-1:])
            dr_ref[...] = jnp.concatenate(d_r, axis=0)
            dkap_ref[...] = dkap_ref[...] - jnp.concatenate(d_kap, axis=0)

        for t in reversed(range(C)):
            dof = _expand_cols(do_cols, t)
            if t == C - 1:
                d_sn = ds_scr[...] + dof * row(rr, t)
                dsa = _lane_sum(d_sn * row(bb, t), ones)
            else:
                v_t = d_sn * row(ww, t + 1) + dof * row(rr, t)
                ls = _lane_sum(v_t * row(bb, t), ones)
                emit_rows(t + 1, d_sn, dsa)
                d_sn = v_t - dsa * row(kap_, t + 1)
                dsa = ls - dsa * row(e1, t)
        emit_rows(0, d_sn, dsa)
        d_s = d_sn * row(ww, 0) - dsa * row(kap_, 0)
        ds_scr[...] = d_s
        dv_ref[...] = _steps_to_rows(dv_acc[...])
        state_rows()

    spec = pl.BlockSpec((C, RWKV_WIDTH), lambda i: (nc - 1 - i, 0))
    states = pl.BlockSpec((C, RWKV_HEAD, RWKV_WIDTH), lambda i: (nc - 1 - i, 0, 0))
    before = pl.BlockSpec((1, RWKV_HEAD, RWKV_WIDTH), lambda i: (jnp.maximum((nc - 1 - i) * C - 1, 0), 0, 0))
    return pl.pallas_call(
        body, name="wkv_bwd", grid=(nc,),
        in_specs=[spec] * 8 + [states, before],
        out_specs=[spec] * 6,
        out_shape=[jax.ShapeDtypeStruct((T, RWKV_WIDTH), F32)] * 6,
        scratch_shapes=[pltpu.VMEM((RWKV_HEAD, RWKV_WIDTH), F32)] * 3,
        compiler_params=_params(dimension_semantics=("arbitrary",)),
    )(r, w, k, v, kap, a, sa_rows, d_o, s_all, s_all)


W = RWKV_WIDTH


def _softplus(y):
    return jnp.maximum(y, 0.0) + jnp.log(1.0 + jnp.exp(-jnp.abs(y)))


def _prep_fn(kr, xwa, w0, a0, k_k, k_a, wup_pad, aup_pad, ones64):
    w_log = -_softplus(-(w0 + _mm_nt(jnp.tanh(xwa), wup_pad))) - 0.5
    decay = jnp.exp(-jnp.exp(w_log))
    a = jax.nn.sigmoid(a0 + _mm_nt(xwa, aup_pad))
    kk = kr * k_k
    kap = kk / jnp.maximum(jnp.sqrt(_head_mix(kk * kk, ones64)), 1e-12)
    k = kr * (1.0 + (a - 1.0) * k_a)
    return decay, k, kap, a


def _shift_down(p, first_row):
    rows = lax.broadcasted_iota(jnp.int32, p.shape, 0)
    return jnp.where(rows == 0, first_row, pltpu.roll(p, 1, axis=0))


def _shift_up(z, last_row):
    n = z.shape[0]
    rows = lax.broadcasted_iota(jnp.int32, z.shape, 0)
    return jnp.where(rows == n - 1, last_row, pltpu.roll(z, n - 1, axis=0))


def _prev_block_spec():
    return pl.BlockSpec((8, RWKV_COLS), lambda i: (jnp.maximum(i * (TOK_TILE // 8) - 1, 0), 0))


def _mixed(p_ref, prev8_ref, mu_ref, first_tile):
    p = p_ref[...]
    first_row = jnp.where(first_tile, 0.0, prev8_ref[7:8, :])
    prev = _shift_down(p, first_row)
    return p, prev, p + mu_ref[...] * (prev - p)


def _prep_fwd(p_rwkv, mu, w0, a0, k_k, k_a, wup_pad, aup_pad, ones64):
    T = p_rwkv.shape[0]

    def body(p_ref, prev8_ref, mu_ref, w0_ref, a0_ref, kk_ref, ka_ref, wup_ref, aup_ref, ones_ref,
             r_ref, w_ref, k_ref, v_ref, kap_ref, a_ref, g_ref):
        _, _, ps = _mixed(p_ref, prev8_ref, mu_ref, pl.program_id(0) == 0)
        decay, k, kap, a = _prep_fn(ps[:, W:2 * W], ps[:, 4 * W:], w0_ref[...], a0_ref[...], kk_ref[...], ka_ref[...],
                                    wup_ref[...], aup_ref[...], ones_ref[...])
        r_ref[...] = ps[:, 0:W]
        w_ref[...] = decay
        k_ref[...] = k
        v_ref[...] = ps[:, 2 * W:3 * W]
        kap_ref[...] = kap
        a_ref[...] = a
        g_ref[...] = ps[:, 3 * W:4 * W]

    vec = _full((1, W))
    return pl.pallas_call(
        body, name="prep_fwd", grid=(T // TOK_TILE,),
        in_specs=[_rows(TOK_TILE, RWKV_COLS), _prev_block_spec(), _full((1, RWKV_COLS)), vec, vec, vec, vec,
                  _full((W, 2 * LORA)), _full((W, 2 * LORA)), _full((256, 128))],
        out_specs=[_rows(TOK_TILE, W)] * 7,
        out_shape=[jax.ShapeDtypeStruct((T, W), F32)] * 7,
        compiler_params=_params(dimension_semantics=("arbitrary",)),
    )(p_rwkv, p_rwkv, mu, w0, a0, k_k, k_a, wup_pad, aup_pad, ones64)


def _prep_bwd(p_rwkv, mu, w0, a0, k_k, k_a, wup_pad, aup_pad, ones64, dr, dw, dk, dv, dkap, da, dg, dr2, dk2, dv2):
    T = p_rwkv.shape[0]
    nt = T // TOK_TILE

    def body(p_ref, prev8_ref, mu_ref, w0_ref, a0_ref, kk_ref, ka_ref, wup_ref, aup_ref, ones_ref,
             dr_ref, dw_ref, dk_ref, dv_ref, dkap_ref, da_ref, dg_ref, dr2_ref, dk2_ref, dv2_ref,
             dp_ref, dmu_ref, dw0_ref, da0_ref, dkk_ref, dka_ref, dwup_ref, daup_ref, zrow_scr):
        i = pl.program_id(0)
        accs = (dmu_ref, dw0_ref, da0_ref, dkk_ref, dka_ref, dwup_ref, daup_ref)

        @pl.when(i == 0)
        def _():
            zrow_scr[...] = jnp.zeros_like(zrow_scr)
            for ref in accs:
                ref[...] = jnp.zeros_like(ref)

        p, prev, ps = _mixed(p_ref, prev8_ref, mu_ref, i == nt - 1)
        ones = ones_ref[...]
        _, vjp = jax.vjp(lambda *args: _prep_fn(*args, ones), ps[:, W:2 * W], ps[:, 4 * W:], w0_ref[...], a0_ref[...],
                         kk_ref[...], ka_ref[...], wup_ref[...], aup_ref[...])
        dkr, dxwa, dw0, da0, dkk, dka, dwup, daup = vjp(
            (dw_ref[...], dk_ref[...] + dk2_ref[...], dkap_ref[...], da_ref[...]))
        dps = jnp.concatenate([dr_ref[...] + dr2_ref[...], dkr, dv_ref[...] + dv2_ref[...], dg_ref[...], dxwa], axis=1)
        z = dps * mu_ref[...]
        dp_ref[...] = dps - z + _shift_up(z, zrow_scr[0:1, :])
        zrow_scr[0:1, :] = z[0:1, :]
        for ref, val in zip(accs, (_colsum(dps * (prev - p)), dw0, da0, dkk, dka, dwup, daup)):
            ref[...] += val

    rev = lambda i: (nt - 1 - i, 0)
    vec = _full((1, W))
    lora = _full((W, 2 * LORA))
    tile = pl.BlockSpec((TOK_TILE, W), rev)
    prev8 = pl.BlockSpec((8, RWKV_COLS), lambda i: (jnp.maximum((nt - 1 - i) * (TOK_TILE // 8) - 1, 0), 0))
    return pl.pallas_call(
        body, name="prep_bwd", grid=(nt,),
        in_specs=[pl.BlockSpec((TOK_TILE, RWKV_COLS), rev), prev8, _full((1, RWKV_COLS)), vec, vec, vec, vec, lora, lora,
                  _full((256, 128))] + [tile] * 10,
        out_specs=[pl.BlockSpec((TOK_TILE, RWKV_COLS), rev), _full((1, RWKV_COLS)), vec, vec, vec, vec, lora, lora],
        out_shape=[jax.ShapeDtypeStruct((T, RWKV_COLS), F32), jax.ShapeDtypeStruct((1, RWKV_COLS), F32)]
        + [jax.ShapeDtypeStruct((1, W), F32)] * 4 + [jax.ShapeDtypeStruct((W, 2 * LORA), F32)] * 2,
        scratch_shapes=[pltpu.VMEM((8, RWKV_COLS), F32)],
        compiler_params=_params(dimension_semantics=("arbitrary",)),
    )(p_rwkv, p_rwkv, mu, w0, a0, k_k, k_a, wup_pad, aup_pad, ones64, dr, dw, dk, dv, dkap, da, dg, dr2, dk2, dv2)


def _silu(x):
    return x * jax.nn.sigmoid(x)


def _post_y(o, r, k, v, g_rw, ret_raw, g_ret, ret_gn_g, gn_g, gn_b, r_k, avg128, avg64, ones64):
    xc = ret_raw - _head_mix(ret_raw, avg128)
    ret = xc * lax.rsqrt(_head_mix(xc * xc, avg128) + RET_GN_EPS)
    y_ret = _silu(g_ret) * (ret * ret_gn_g)
    oc = o - _head_mix(o, avg64)
    on = oc * lax.rsqrt(_head_mix(oc * oc, avg64) + RWKV_GN_EPS) * gn_g + gn_b
    bonus = _head_mix(r * k * r_k, ones64) * v
    y_rwkv = _silu(g_rw) * (on + bonus)
    return y_ret, y_rwkv


def _post_loss(h, final_g, target):
    err = _rmsnorm(h, final_g) - target
    return 0.5 * jnp.sum(jnp.mean(err * err, axis=-1))


def _post(o, r, k, v, g_rw, ret_raw, p_ret, x, target, ret_gn_g, gn_g, gn_b, r_k, final_g, w_out, avg128, avg64, ones64):
    T = x.shape[1]
    n_tok_out = 8

    def body(o_ref, r_ref, k_ref, v_ref, grw_ref, ret_ref, gret_ref, x_ref, tgt_ref, rg_ref, gg_ref, gb_ref, rk_ref, fg_ref,
             wo_ref, a128_ref, a64_ref, ones_ref, *outs):
        tok_outs, (dwo_ref, drg_ref, dgg_ref, dgb_ref, drk_ref, dfg_ref, loss_ref) = outs[:n_tok_out], outs[n_tok_out:]
        accs = (dwo_ref, drg_ref, dgg_ref, dgb_ref, drk_ref, dfg_ref, loss_ref)

        @pl.when(pl.program_id(0) == 0)
        def _():
            for ref in accs:
                ref[...] = jnp.zeros_like(ref)

        consts = (a128_ref[...], a64_ref[...], ones_ref[...])
        (y_ret, y_rwkv), vjp = jax.vjp(
            lambda *args: _post_y(*args, *consts), o_ref[...], r_ref[...], k_ref[...], v_ref[...], grw_ref[...], ret_ref[...],
            gret_ref[...], rg_ref[...], gg_ref[...], gb_ref[...], rk_ref[...])
        h = x_ref[...] + _dot_bf(y_ret, wo_ref[0:RET_WIDTH, :]) + _dot_bf(y_rwkv, wo_ref[RET_WIDTH:, :])
        loss, (dh, dfg) = jax.value_and_grad(_post_loss, argnums=(0, 1))(h, fg_ref[...], tgt_ref[...])
        dy_ret = _dot_nt_bf(dh, wo_ref[0:RET_WIDTH, :])
        dy_rwkv = _dot_nt_bf(dh, wo_ref[RET_WIDTH:, :])
        do, dr, dk, dv, dgrw, dret, dgret, drg, dgg, dgb, drk = vjp((dy_ret, dy_rwkv))
        for ref, val in zip(tok_outs, (dh, do, dr, dk, dv, dgrw, dret, dgret)):
            ref[...] = val
        dwo_ref[0:RET_WIDTH, :] += _dot_tn_bf(y_ret, dh)
        dwo_ref[RET_WIDTH:, :] += _dot_tn_bf(y_rwkv, dh)
        for ref, val in zip(accs[1:], (drg, dgg, dgb, drk, dfg, jnp.full((1, 128), loss, F32))):
            ref[...] += val

    tile = _rows(TOK_TILE, W)
    wide = _rows(TOK_TILE, D_MODEL)
    wide_of_one = _rows_of_one(TOK_TILE, D_MODEL)
    vec = _full((1, W))
    sq = _full((256, 128))
    return pl.pallas_call(
        body, name="post", grid=(T // TOK_TILE,),
        in_specs=[tile] * 6 + [pl.BlockSpec((TOK_TILE, W), lambda i: (i, 2)), wide_of_one, wide_of_one, vec, vec, vec, vec,
                               _full((1, D_MODEL)), _full((D_MODEL, D_MODEL)), sq, sq, sq],
        out_specs=[wide] + [tile] * 7 + [_full((D_MODEL, D_MODEL)), vec, vec, vec, vec, _full((1, D_MODEL)), _full((1, 128))],
        out_shape=[jax.ShapeDtypeStruct((T, D_MODEL), F32)] + [jax.ShapeDtypeStruct((T, W), F32)] * 7
        + [jax.ShapeDtypeStruct((D_MODEL, D_MODEL), F32)] + [jax.ShapeDtypeStruct((1, W), F32)] * 4
        + [jax.ShapeDtypeStruct((1, D_MODEL), F32), jax.ShapeDtypeStruct((1, 128), F32)],
        compiler_params=_params(dimension_semantics=("arbitrary",)),
    )(o, r, k, v, g_rw, ret_raw, p_ret, x, target, ret_gn_g, gn_g, gn_b, r_k, final_g, w_out, avg128, avg64, ones64)


def _inproj_bwd_x(x, norm_g, dp_qkv, dg_ret, dp_rwkv, dh, w_in_t):
    T = x.shape[1]
    n_qkv = 2 * RET_QK + RET_WIDTH

    def body(x_ref, g_ref, dqkv_ref, dgret_ref, drwkv_ref, dh_ref, w_ref, dx_ref, dg_ref):
        @pl.when(pl.program_id(0) == 0)
        def _():
            dg_ref[...] = jnp.zeros_like(dg_ref)

        _, vjp = jax.vjp(_rmsnorm, x_ref[...], g_ref[...])
        du = (_dot_bf(dqkv_ref[...], w_ref[0:n_qkv, :]) + _dot_bf(dgret_ref[...], w_ref[n_qkv:RET_COLS, :])
              + _dot_bf(drwkv_ref[...], w_ref[RET_COLS:, :]))
        dx, dg = vjp(du)
        dx_ref[...] = dx + dh_ref[...]
        dg_ref[...] += dg

    return pl.pallas_call(
        body, name="inproj_bwd_x", grid=(T // TOK_TILE,),
        in_specs=[_rows_of_one(TOK_TILE, D_MODEL), _full((1, D_MODEL)), _rows(TOK_TILE, n_qkv), _rows(TOK_TILE, RET_WIDTH),
                  _rows(TOK_TILE, RWKV_COLS), _rows(TOK_TILE, D_MODEL), _full((IN_COLS, D_MODEL))],
        out_specs=[_rows_of_one(TOK_TILE, D_MODEL), _full((1, D_MODEL))],
        out_shape=[jax.ShapeDtypeStruct((1, T, D_MODEL), F32), jax.ShapeDtypeStruct((1, D_MODEL), F32)],
        compiler_params=_params(dimension_semantics=("arbitrary",)),
    )(x, norm_g, dp_qkv, dg_ret, dp_rwkv, dh, w_in_t)


def _grad_w_in(u, dps):
    T = u.shape[0]
    widths = [dp.shape[1] for dp in dps]
    tile = 2 * TOK_TILE
    steps = T // tile

    def body(u_ref, *refs):
        dp_refs, (mine_ref, sib_ref, acc_ref) = refs[:len(dps)], refs[len(dps):]

        @pl.when(pl.program_id(0) == 0)
        def _():
            acc_ref[...] = jnp.zeros_like(acc_ref)

        row = 0
        for dp_ref, n in zip(dp_refs, widths):
            acc_ref[row:row + n, :] += _dot_tn_bf(dp_ref[...], u_ref[...])
            row += n

        @pl.when(pl.program_id(0) == steps - 1)
        def _():
            core = lax.axis_index("c")
            for dev in range(N_DEV):
                block = acc_ref[dev * SHARD_IN:(dev + 1) * SHARD_IN, :].astype(BF16)

                @pl.when(core == dev % 2)
                def _():
                    mine_ref[dev // 2] = block

                @pl.when(core != dev % 2)
                def _():
                    sib_ref[dev // 2] = block

    half = jax.ShapeDtypeStruct((N_DEV // 2, SHARD_IN, D_MODEL), BF16)
    return pl.pallas_call(
        body, name="grad_w_in", grid=(steps,),
        in_specs=[_rows(tile, D_MODEL)] + [_rows(tile, n) for n in widths],
        out_specs=[_full(half.shape), _full(half.shape)],
        out_shape=[half, half],
        scratch_shapes=[pltpu.VMEM((IN_COLS, D_MODEL), F32)],
        compiler_params=_params(dimension_semantics=("arbitrary",)),
    )(u, *dps)


def _pad_lora(w_up_t, first):
    z = jnp.zeros_like(w_up_t)
    return jnp.concatenate([w_up_t, z] if first else [z, w_up_t], axis=1)


def _local_grads(x, target, norm_g, w_in_t, ret_gn_g, mu, w_lora_up_t, w0, a_lora_up_t, a0, k_k, k_a, r_k, gn_g, gn_b,
                 w_out_bf, final_g):
    T = x.shape[1]
    tabs = _rope_tables(T) + _ret_tables()
    ones64 = _block_mix(128, RWKV_HEAD)
    avg64 = _block_mix(128, RWKV_HEAD, 1.0 / RWKV_HEAD)
    avg128 = _block_mix(128, RET_DV, 1.0 / RET_DV)
    wup_pad, aup_pad = _pad_lora(w_lora_up_t, True), _pad_lora(a_lora_up_t, False)

    p_ret, p_rwkv, u = _inproj(x, norm_g, w_in_t)
    ret_raw, s_saved = _ret_fwd(p_ret, tabs)
    r, w, k, v, kap, a, g_rw = _prep_fwd(p_rwkv, mu, w0, a0, k_k, k_a, wup_pad, aup_pad, ones64)
    o, s_all, sa_rows = _wkv_fwd(r, w, k, v, kap, a)
    (dh, do, dr2, dk2, dv2, dgrw, dret, dgret, d_w_out, d_ret_gn_g, d_gn_g, d_gn_b, d_r_k, d_final_g, loss) = _post(
        o, r, k, v, g_rw, ret_raw, p_ret, x, target, ret_gn_g, gn_g, gn_b, r_k, final_g, w_out_bf, avg128, avg64, ones64)
    dr, dw, dk, dv, dkap, da = _wkv_bwd(r, w, k, v, kap, a, s_all, sa_rows, do)
    dp_rwkv, d_mu, d_w0, d_a0, d_k_k, d_k_a, d_wup, d_aup = _prep_bwd(
        p_rwkv, mu, w0, a0, k_k, k_a, wup_pad, aup_pad, ones64, dr, dw, dk, dv, dkap, da, dgrw, dr2, dk2, dv2)
    dp_qkv = _ret_bwd(p_ret, s_saved, dret, tabs)
    dx, d_norm_g = _inproj_bwd_x(x, norm_g, dp_qkv, dgret, dp_rwkv, dh, w_in_t)
    d_w_in = _grad_w_in(u, [dp_qkv, dgret, dp_rwkv])
    grads = dict(norm_g=d_norm_g, w_in=d_w_in, ret_gn_g=d_ret_gn_g, rwkv_mu=d_mu, w_lora_up=d_wup[:, :LORA], w0=d_w0,
                 a_lora_up=d_aup[:, LORA:], a0=d_a0, k_k=d_k_k, k_a=d_k_a, r_k=d_r_k, rwkv_gn_g=d_gn_g, rwkv_gn_b=d_gn_b,
                 w_out=d_w_out, final_norm_g=d_final_g)
    return loss, dx, grads


def _mesh_pos():
    return lax.axis_index("x"), lax.axis_index("y"), lax.axis_index("c")


def _all_gather(shards):
    n = len(shards)

    def body(*refs):
        x_refs, out_refs = refs[:n], refs[n:2 * n]
        send_sems, recv_sems, local_sems = refs[2 * n:]
        x, y, c = _mesh_pos()
        me, sibling = (x, y, c), (x, y, 1 - c)
        chips = [(1 - x, y), (x, 1 - y), (1 - x, 1 - y)]

        def rows(a, pos):
            m = x_refs[a].shape[0]
            return out_refs[a].at[pl.ds((4 * pos[0] + 2 * pos[1] + pos[2]) * m, m), :]

        def copy(a, k, block, to, src=None):
            return pltpu.make_async_remote_copy(
                src_ref=rows(a, block) if src is None else src, dst_ref=rows(a, block),
                send_sem=send_sems.at[a, k], recv_sem=recv_sems.at[a, k], device_id=to, device_id_type=MESH)

        mine = [pltpu.make_async_copy(x_refs[a], rows(a, me), local_sems.at[a]) for a in range(n)]
        for cp in mine:
            cp.start()
        first = []
        for a in range(n):
            first.append(copy(a, 0, me, sibling, src=x_refs[a]))
            first += [copy(a, 1 + j, me, (*chip, c), src=x_refs[a]) for j, chip in enumerate(chips)]
        for cp in first:
            cp.start()
        passed = []
        for j, chip in enumerate(chips):
            for a in range(n):
                copy(a, 1 + j, (*chip, c), me).wait_recv()
                passed.append(copy(a, 4 + j, (*chip, c), sibling))
                passed[-1].start()
        for a in range(n):
            copy(a, 0, sibling, me).wait_recv()
            for j, chip in enumerate(chips):
                copy(a, 4 + j, (*chip, 1 - c), me).wait_recv()
        for cp in first + passed:
            cp.wait_send()
        for cp in mine:
            cp.wait()

    vmem = pl.BlockSpec(memory_space=pltpu.VMEM)
    return pl.pallas_call(
        body, name="gather_weights",
        out_shape=[jax.ShapeDtypeStruct((N_DEV * s.shape[0], s.shape[1]), s.dtype) for s in shards],
        in_specs=[vmem] * n, out_specs=[vmem] * n,
        scratch_shapes=[pltpu.SemaphoreType.DMA((n, 7)), pltpu.SemaphoreType.DMA((n, 7)), pltpu.SemaphoreType.DMA((n,))],
        compiler_params=_params(),
    )(*shards)


N_CHIP = 4


def _exchange_pairs(big, small):
    nb, ns = len(big), len(small)

    def body(*refs):
        big_in, small_in = refs[:nb], refs[nb:nb + ns]
        theirs, small_out = refs[nb + ns:2 * nb + ns], refs[2 * nb + ns:2 * nb + 2 * ns]
        pair_send, pair_recv, send_sems, recv_sems, local_sems = refs[2 * nb + 2 * ns:]
        x, y, c = _mesh_pos()
        me = 4 * x + 2 * y + c
        local = [pltpu.make_async_copy(small_in[a].at[me], small_out[a].at[me], local_sems.at[a]) for a in range(ns)]
        for cp in local:
            cp.start()
        copies = [pltpu.make_async_remote_copy(
            src_ref=big_in[a], dst_ref=theirs[a], send_sem=pair_send.at[a], recv_sem=pair_recv.at[a],
            device_id=(x, y, 1 - c), device_id_type=MESH) for a in range(nb)]
        for k in range(1, N_DEV):
            peer = (x ^ (k >> 2), y ^ ((k >> 1) & 1), c ^ (k & 1))
            peer_idx = 4 * peer[0] + 2 * peer[1] + peer[2]
            copies += [pltpu.make_async_remote_copy(
                src_ref=small_in[a].at[peer_idx], dst_ref=small_out[a].at[me], send_sem=send_sems.at[a, k - 1],
                recv_sem=recv_sems.at[a, k - 1], device_id=peer, device_id_type=MESH) for a in range(ns)]
        for cp in copies:
            cp.start()
        for cp in copies:
            cp.wait()
        for cp in local:
            cp.wait()

    hbm = pl.BlockSpec(memory_space=pl.ANY)
    out_shape = [jax.ShapeDtypeStruct(p.shape, p.dtype) for p in big + small]
    dma = pltpu.SemaphoreType.DMA
    res = pl.pallas_call(
        body, name="exchange_pairs", out_shape=out_shape,
        in_specs=[hbm] * (nb + ns), out_specs=[hbm] * len(out_shape),
        scratch_shapes=[dma((nb,)), dma((nb,)), dma((ns, 7)), dma((ns, 7)), dma((ns,))],
        compiler_params=_params(),
    )(*big, *small)
    return res[:nb], res[nb:]


def _pair_sum(name, mine, theirs, row_tile):
    _, rows, cols = mine.shape

    def body(a_ref, b_ref, o_ref):
        o_ref[...] = (a_ref[...].astype(F32) + b_ref[...].astype(F32)).astype(o_ref.dtype)

    spec = pl.BlockSpec((N_CHIP, row_tile, cols), lambda i: (0, i, 0))
    return pl.pallas_call(
        body, name=name, grid=(rows // row_tile,), in_specs=[spec, spec], out_specs=spec,
        out_shape=jax.ShapeDtypeStruct(mine.shape, mine.dtype),
        compiler_params=_params(dimension_semantics=("arbitrary",)),
    )(mine, theirs)


def _exchange_chips(parts):
    n = len(parts)

    def body(*refs):
        in_refs, out_refs = refs[:n], refs[n:2 * n]
        send_sems, recv_sems, local_sems = refs[2 * n:]
        x, y, c = _mesh_pos()
        my_chip = 2 * x + y
        own = [pltpu.make_async_copy(in_refs[a].at[my_chip], out_refs[a].at[my_chip], local_sems.at[a]) for a in range(n)]
        for cp in own:
            cp.start()
        copies = []
        for k in range(1, N_CHIP):
            px, py = x ^ (k >> 1), y ^ (k & 1)
            copies += [pltpu.make_async_remote_copy(
                src_ref=in_refs[a].at[2 * px + py], dst_ref=out_refs[a].at[my_chip], send_sem=send_sems.at[a, k - 1],
                recv_sem=recv_sems.at[a, k - 1], device_id=(px, py, c), device_id_type=MESH) for a in range(n)]
        for cp in copies:
            cp.start()
        for cp in copies:
            cp.wait()
        for cp in own:
            cp.wait()

    hbm = pl.BlockSpec(memory_space=pl.ANY)
    dma = pltpu.SemaphoreType.DMA
    return pl.pallas_call(
        body, name="exchange_chips",
        out_shape=[jax.ShapeDtypeStruct(p.shape, p.dtype) for p in parts],
        in_specs=[hbm] * n, out_specs=[hbm] * n,
        scratch_shapes=[dma((n, N_CHIP - 1)), dma((n, N_CHIP - 1)), dma((n,))],
        compiler_params=_params(),
    )(*parts)


def _adamw(w, g, m, v):
    m = ADAM_B1 * m + (1.0 - ADAM_B1) * g
    v = ADAM_B2 * v + (1.0 - ADAM_B2) * (g * g)
    m_hat = m / (1.0 - ADAM_B1 ** ADAM_STEP)
    v_hat = v / (1.0 - ADAM_B2 ** ADAM_STEP)
    return -ADAM_LR * (m_hat / (jnp.sqrt(v_hat) + ADAM_EPS) + ADAM_WD * w), m, v


def _sum_parts(name, parts, row_tile):
    n_parts, rows, cols = parts.shape

    def body(p_ref, g_ref):
        g = p_ref[0].astype(F32)
        for s in range(1, n_parts):
            g = g + p_ref[s].astype(F32)
        g_ref[...] = g

    return pl.pallas_call(
        body, name=name, grid=(rows // row_tile,),
        in_specs=[pl.BlockSpec((n_parts, row_tile, cols), lambda i: (0, i, 0))],
        out_specs=pl.BlockSpec((row_tile, cols), lambda i: (i, 0)),
        out_shape=jax.ShapeDtypeStruct((rows, cols), F32),
        compiler_params=_params(dimension_semantics=("arbitrary",)),
    )(parts)


def _adamw_apply(name, g, w, m, v, row_tile):
    _, rows, cols = w.shape

    def body(g_ref, w_ref, m_ref, v_ref, d_ref, nm_ref, nv_ref):
        d_ref[0], nm_ref[0], nv_ref[0] = _adamw(w_ref[0], g_ref[0], m_ref[0], v_ref[0])

    tile = pl.BlockSpec((1, row_tile, cols), lambda i: (0, i, 0))
    return pl.pallas_call(
        body, name=name, grid=(rows // row_tile,), in_specs=[tile] * 4, out_specs=[tile] * 3,
        out_shape=[jax.ShapeDtypeStruct((1, rows, cols), F32)] * 3,
        compiler_params=_params(dimension_semantics=("arbitrary",)),
    )(g, w, m, v)


def _reduce_adamw_2d(name, parts, w, m, v, row_tile):
    n_parts, rows, cols = parts.shape

    def body(p_ref, w_ref, m_ref, v_ref, g_ref, d_ref, nm_ref, nv_ref):
        g = p_ref[0].astype(F32)
        for s in range(1, n_parts):
            g = g + p_ref[s].astype(F32)
        g_ref[...] = g
        d_ref[...], nm_ref[...], nv_ref[...] = _adamw(w_ref[...], g, m_ref[...], v_ref[...])

    tile = pl.BlockSpec((row_tile, cols), lambda i: (i, 0))
    return pl.pallas_call(
        body, name=name, grid=(rows // row_tile,),
        in_specs=[pl.BlockSpec((n_parts, row_tile, cols), lambda i: (0, i, 0)), tile, tile, tile],
        out_specs=[tile] * 4,
        out_shape=[jax.ShapeDtypeStruct((rows, cols), F32)] * 4,
        compiler_params=_params(dimension_semantics=("arbitrary",)),
    )(parts, w, m, v)


def _reduce_adamw_t(name, parts_t, w, m, v, sum_tile, row_tile):
    g = _sum_parts(name + "_sum", parts_t, sum_tile).T[None]
    return [g] + list(_adamw_apply(name, g, w, m, v, row_tile))


def _reduce_adamw(name, parts, w, m, v, row_tile):
    n_parts, rows, cols = parts.shape

    def body(p_ref, w_ref, m_ref, v_ref, g_ref, d_ref, nm_ref, nv_ref):
        g = p_ref[0].astype(F32)
        for s in range(1, n_parts):
            g = g + p_ref[s].astype(F32)
        g_ref[0] = g
        d_ref[0], nm_ref[0], nv_ref[0] = _adamw(w_ref[0], g, m_ref[0], v_ref[0])

    tile = pl.BlockSpec((1, row_tile, cols), lambda i: (0, i, 0))
    return pl.pallas_call(
        body, name=name, grid=(rows // row_tile,),
        in_specs=[pl.BlockSpec((n_parts, row_tile, cols), lambda i: (0, i, 0)), tile, tile, tile],
        out_specs=[tile] * 4,
        out_shape=[jax.ShapeDtypeStruct((1, rows, cols), F32)] * 4,
        compiler_params=_params(dimension_semantics=("arbitrary",)),
    )(parts, w, m, v)


_SMALL = (("norm_g", 1024), ("ret_gn_g", 512), ("rwkv_mu", 2176), ("w0", 512), ("a0", 512), ("k_k", 512), ("k_a", 512),
          ("r_k", 512), ("rwkv_gn_g", 512), ("rwkv_gn_b", 512), ("final_norm_g", 1024))
_SMALL_LANES = sum(n for _, n in _SMALL) + 128
_WEIGHTS = ("norm_g", "w_in", "ret_gn_g", "rwkv_mu", "w_lora_up", "w0", "a_lora_up", "a0", "k_k", "k_a", "r_k", "rwkv_gn_g",
            "rwkv_gn_b", "w_out", "final_norm_g")


def _adamw_vectors(parts, wts, mom, var):
    k = len(_SMALL)

    def body(p_ref, *refs):
        w_refs, m_refs, v_refs, outs = refs[:k], refs[k:2 * k], refs[2 * k:3 * k], refs[3 * k:]
        g_all = p_ref[0]
        for s in range(1, N_DEV):
            g_all = g_all + p_ref[s]
        off = 0
        for i, (name, n) in enumerate(_SMALL):
            g = g_all[:, off:off + n]
            off += n
            if name == "r_k":
                g = jnp.concatenate([g[:, RWKV_HEAD * h:RWKV_HEAD * (h + 1)] for h in range(RWKV_HEADS)], axis=0)[None]
            outs[4 * i][...] = g
            outs[4 * i + 1][...], outs[4 * i + 2][...], outs[4 * i + 3][...] = _adamw(
                w_refs[i][...], g, m_refs[i][...], v_refs[i][...])
        outs[4 * k][...] = g_all[:, off:off + 128]

    vmem = pl.BlockSpec(memory_space=pltpu.VMEM)
    shapes = [jax.ShapeDtypeStruct(wts[n].shape, F32) for n, _ in _SMALL for _ in range(4)] + [jax.ShapeDtypeStruct((1, 128), F32)]
    res = pl.pallas_call(
        body, name="adamw_vectors", out_shape=shapes,
        in_specs=[vmem] * (1 + 3 * k), out_specs=[vmem] * len(shapes), compiler_params=_params(),
    )(parts, *[wts[n] for n, _ in _SMALL], *[mom[n] for n, _ in _SMALL], *[var[n] for n, _ in _SMALL])
    return {n: res[4 * i:4 * i + 4] for i, (n, _) in enumerate(_SMALL)}, res[4 * k]


def kernel(x, norm_g, w_in, ret_gn_g, rwkv_mu, w_lora_up, w0, a_lora_up, a0, k_k, k_a, r_k, rwkv_gn_g, rwkv_gn_b, w_out, final_norm_g, loss_target, m_norm_g, m_w_in, m_ret_gn_g, m_rwkv_mu, m_w_lora_up, m_w0, m_a_lora_up, m_a0, m_k_k, m_k_a, m_r_k, m_rwkv_gn_g, m_rwkv_gn_b, m_w_out, m_final_norm_g, v_norm_g, v_w_in, v_ret_gn_g, v_rwkv_mu, v_w_lora_up, v_w0, v_a_lora_up, v_a0, v_k_k, v_k_a, v_r_k, v_rwkv_gn_g, v_rwkv_gn_b, v_w_out, v_final_norm_g):
    wts = dict(norm_g=norm_g, w_in=w_in, ret_gn_g=ret_gn_g, rwkv_mu=rwkv_mu, w_lora_up=w_lora_up, w0=w0, a_lora_up=a_lora_up,
               a0=a0, k_k=k_k, k_a=k_a, r_k=r_k, rwkv_gn_g=rwkv_gn_g, rwkv_gn_b=rwkv_gn_b, w_out=w_out,
               final_norm_g=final_norm_g)
    mom = dict(norm_g=m_norm_g, w_in=m_w_in, ret_gn_g=m_ret_gn_g, rwkv_mu=m_rwkv_mu, w_lora_up=m_w_lora_up, w0=m_w0,
               a_lora_up=m_a_lora_up, a0=m_a0, k_k=m_k_k, k_a=m_k_a, r_k=m_r_k, rwkv_gn_g=m_rwkv_gn_g,
               rwkv_gn_b=m_rwkv_gn_b, w_out=m_w_out, final_norm_g=m_final_norm_g)
    var = dict(norm_g=v_norm_g, w_in=v_w_in, ret_gn_g=v_ret_gn_g, rwkv_mu=v_rwkv_mu, w_lora_up=v_w_lora_up, w0=v_w0,
               a_lora_up=v_a_lora_up, a0=v_a0, k_k=v_k_k, k_a=v_k_a, r_k=v_r_k, rwkv_gn_g=v_rwkv_gn_g,
               rwkv_gn_b=v_rwkv_gn_b, w_out=v_w_out, final_norm_g=v_final_norm_g)
    shapes = {n: wts[n].shape for n in _WEIGHTS}

    w_in_t, w_out_bf, wup_t, aup_t = _all_gather(
        [w_in[0].T.astype(BF16), w_out[0].astype(BF16), w_lora_up[0].T, a_lora_up[0].T])

    loss, dx, g = _local_grads(
        x, loss_target, norm_g, w_in_t, ret_gn_g, rwkv_mu, wup_t, w0, aup_t, a0, k_k, k_a,
        r_k.reshape(1, W), rwkv_gn_g, rwkv_gn_b, w_out_bf, final_norm_g.reshape(1, D_MODEL))

    small = jnp.concatenate([g[n] for n, _ in _SMALL] + [loss], axis=1)
    core = lax.axis_index("c")
    by_core = lambda t: [lax.dynamic_index_in_dim(t, i, axis=1, keepdims=False) for i in (core, 1 - core)]
    in_mine, in_sib = g["w_in"]
    out_mine, out_sib = by_core(g["w_out"].reshape(N_CHIP, 2, SHARD_OUT, D_MODEL).astype(BF16))
    (in_theirs, out_theirs), parts = _exchange_pairs(
        [in_sib, out_sib],
        [g["w_lora_up"].reshape(N_DEV, SHARD_LORA, LORA), g["a_lora_up"].reshape(N_DEV, SHARD_LORA, LORA),
         jnp.broadcast_to(small[None], (N_DEV, 1, _SMALL_LANES))])
    by_chip = _exchange_chips([_pair_sum("pair_sum_w_in", in_mine, in_theirs, SHARD_IN // 2),
                               _pair_sum("pair_sum_w_out", out_mine, out_theirs, SHARD_OUT)])
    res = {}
    res["w_in"] = [t.T[None] for t in _reduce_adamw_2d(
        "adamw_w_in", by_chip[0], w_in[0].T, m_w_in[0].T, v_w_in[0].T, SHARD_IN // 2)]
    res["w_out"] = _reduce_adamw("adamw_w_out", by_chip[1], w_out, m_w_out, v_w_out, SHARD_OUT)
    res["w_lora_up"] = _reduce_adamw_t("adamw_w_lora_up", parts[0], w_lora_up, m_w_lora_up, v_w_lora_up, LORA, LORA)
    res["a_lora_up"] = _reduce_adamw_t("adamw_a_lora_up", parts[1], a_lora_up, m_a_lora_up, v_a_lora_up, LORA, LORA)
    as_row = lambda d: {n: d[n] if d[n].ndim > 1 else d[n].reshape(1, size) for n, size in _SMALL}
    vec, loss_row = _adamw_vectors(parts[2], as_row(wts), as_row(mom), as_row(var))
    res.update(vec)
    res = {n: [t.reshape(shapes[n]) for t in res[n]] for n in _WEIGHTS}
    return (loss_row[0, 0], dx, *[res[n][0] for n in _WEIGHTS], *[res[n][1] for n in _WEIGHTS],
            *[res[n][2] for n in _WEIGHTS], *[res[n][3] for n in _WEIGHTS])
```

```python
import numpy as np
import jax
import jax.numpy as jnp
from jax import lax
from jax.experimental import pallas as pl
from jax.experimental.pallas import tpu as pltpu

F32 = jnp.float32
BF16 = jnp.bfloat16

D_MODEL = 1024
CHUNK = 64
RET_HEADS = 4
RET_DV = 128
RET_DK = 64
RET_QK = 256
RET_WIDTH = 512
RWKV_WIDTH = 512
RWKV_HEAD = 64
RWKV_HEADS = 8
LORA = 64
RET_COLS = 2 * RET_QK + 2 * RET_WIDTH
RWKV_COLS = 4 * RWKV_WIDTH + 2 * LORA
IN_COLS = RET_COLS + RWKV_COLS
ROPE_BASE = 10000.0
RMS_EPS = 1e-6
RET_GN_EPS = 1e-5
RWKV_GN_EPS = 64e-5
ADAM_LR = 0.001
ADAM_B1 = 0.9
ADAM_B2 = 0.999
ADAM_EPS = 1e-08
ADAM_WD = 0.01
ADAM_STEP = 10
N_DEV = 8
SHARD_IN = IN_COLS // N_DEV
SHARD_OUT = D_MODEL // N_DEV
SHARD_LORA = RWKV_WIDTH // N_DEV
VMEM_LIMIT = 56 * 1024 * 1024
TOK_TILE = 256
WKV_CHUNK = 64

MESH = pl.DeviceIdType.MESH


def _dot_bf(a, b):
    return jnp.dot(a.astype(BF16), b.astype(BF16), preferred_element_type=F32)


def _dot_nt_bf(a, b):
    return lax.dot_general(a.astype(BF16), b.astype(BF16), (((1,), (1,)), ((), ())), preferred_element_type=F32)


def _dot_tn_bf(a, b):
    return lax.dot_general(a.astype(BF16), b.astype(BF16), (((0,), (0,)), ((), ())), preferred_element_type=F32)


@jax.custom_vjp
def _mm(a, b):
    return _dot_bf(a, b)


@jax.custom_vjp
def _mm_nt(a, b):
    return _dot_nt_bf(a, b)


@jax.custom_vjp
def _mm_tn(a, b):
    return _dot_tn_bf(a, b)


_mm.defvjp(lambda a, b: (_dot_bf(a, b), (a, b)), lambda res, g: (_dot_nt_bf(g, res[1]), _dot_tn_bf(res[0], g)))
_mm_nt.defvjp(lambda a, b: (_dot_nt_bf(a, b), (a, b)), lambda res, g: (_dot_bf(g, res[1]), _dot_tn_bf(g, res[0])))
_mm_tn.defvjp(lambda a, b: (_dot_tn_bf(a, b), (a, b)), lambda res, g: (_dot_nt_bf(res[1], g), _dot_bf(res[0], g)))


def _trunc(x):
    return lax.bitcast_convert_type(lax.bitcast_convert_type(x, jnp.uint32) & jnp.uint32(0xFFFF0000), F32)


def _two_piece(x):
    hi = _trunc(x)
    return jnp.concatenate([hi, x - hi], axis=1)


def _mix_raw(x, mat2):
    return _unstack(jnp.dot(_two_piece(_stack(x)), mat2, preferred_element_type=F32))


@jax.custom_vjp
def _head_mix(x, mat2):
    return _mix_raw(x, mat2)


_head_mix.defvjp(lambda x, mat2: (_mix_raw(x, mat2), mat2), lambda mat2, g: (_mix_raw(g, mat2), jnp.zeros_like(mat2)))


def _swap_halves(x):
    lane = lax.broadcasted_iota(jnp.int32, x.shape, 1)
    return jnp.where((lane & (RET_DK - 1)) < RET_DK // 2, pltpu.roll(x, RET_QK - RET_DK // 2, axis=1),
                     pltpu.roll(x, RET_DK // 2, axis=1))


@jax.custom_vjp
def _rot(x):
    return _swap_halves(x)


_rot.defvjp(lambda x: (_swap_halves(x), None), lambda _, g: (_swap_halves(g),))


def _params(**kw):
    return pltpu.CompilerParams(vmem_limit_bytes=VMEM_LIMIT, **kw)


def _full(shape):
    nd = len(shape)
    return pl.BlockSpec(shape, lambda i, _nd=nd: (0,) * _nd)


def _rows(tile, width):
    return pl.BlockSpec((tile, width), lambda i: (i, 0))


def _rows_of_one(tile, width):
    return pl.BlockSpec((None, tile, width), lambda i: (0, i, 0))


def _block_mix(n, blk, scale=1.0):
    idx = np.arange(n) // blk
    m = (idx[:, None] == idx[None, :]).astype(np.float32) * scale
    return jnp.asarray(np.concatenate([m, m], axis=0))


def _rope_tables(T):
    half = RET_DK // 2
    expo = -np.arange(half, dtype=np.float32) / np.float32(half)
    freqs = np.exp(expo * np.float32(np.log(ROPE_BASE))).astype(np.float32)
    ang = np.arange(T, dtype=np.float32)[:, None] * freqs[None, :]
    cos, sin = np.cos(ang).astype(np.float32), np.sin(ang).astype(np.float32)
    cos_h = np.concatenate([cos, cos], axis=1)
    sin_h = np.concatenate([-sin, sin], axis=1)
    cos_t = np.tile(cos_h, (1, RET_HEADS))
    sin_t = np.tile(sin_h, (1, RET_HEADS))
    return jnp.asarray(cos_t), jnp.asarray(sin_t)


def _ret_tables():
    h = np.arange(RET_HEADS, dtype=np.float32)
    lg = np.log(1.0 - np.exp2(-5.0 - h)).astype(np.float32)
    idx = np.arange(CHUNK, dtype=np.float32)
    intra = np.exp(lg[:, None, None] * np.abs(idx[:, None] - idx[None, :])).astype(np.float32)
    q_dec = np.exp(lg[:, None] * (idx[None, :] + 1.0)).astype(np.float32)
    k_dec = np.exp(lg[:, None] * (CHUNK - 1.0 - idx[None, :])).astype(np.float32)
    chunk_dec = np.exp(lg * CHUNK).astype(np.float32)
    lane_head = np.arange(RET_QK) // RET_DK
    mask = (lane_head[None, :] == np.arange(RET_HEADS)[:, None]).astype(np.float32)
    m = np.broadcast_to(mask[:, None, :], (RET_HEADS, CHUNK, RET_QK)).copy()
    qd = m * q_dec[:, :, None]
    kd = m * k_dec[:, :, None]
    return jnp.asarray(intra), jnp.asarray(m), jnp.asarray(qd), jnp.asarray(kd), [float(c) for c in chunk_dec]


def _rmsnorm(x, g):
    return x * lax.rsqrt(jnp.mean(x * x, axis=-1, keepdims=True) + RMS_EPS) * g


def _ret_chunk(pq, pk, v_heads, s_heads, cos_t, sin_t, dec, hm, qd, kd, chunk_dec):
    q = pq * cos_t + _rot(pq) * sin_t
    k = (pk * cos_t + _rot(pk) * sin_t) * (RET_DK ** -0.5)
    outs, s_out = [], []
    for h in range(RET_HEADS):
        sc = _mm_nt(q * hm[h], k * hm[h]) * dec[h]
        intra = _mm(sc, v_heads[h])
        kv = _mm_tn(k * kd[h], v_heads[h])
        inter = _mm(q * qd[h], s_heads[h])
        outs.append(intra + inter)
        s_out.append(s_heads[h] * chunk_dec[h] + kv)
    return tuple(outs), tuple(s_out)


def _ret_specs():
    const = [_full((RET_HEADS, CHUNK, CHUNK)), _full((RET_HEADS, CHUNK, RET_QK)),
             _full((RET_HEADS, CHUNK, RET_QK)), _full((RET_HEADS, CHUNK, RET_QK))]
    return const


RET_GROUP = 8


def _ret_fwd(p_ret, tabs):
    T = p_ret.shape[0]
    G = RET_GROUP
    ng = T // (CHUNK * G)
    cos_t, sin_t, dec, hm, qd, kd, chunk_dec = tabs

    def body(p_ref, cos_ref, sin_ref, dec_ref, hm_ref, qd_ref, kd_ref, out_ref, sin_save_ref, s_scr):
        @pl.when(pl.program_id(0) == 0)
        def _():
            s_scr[...] = jnp.zeros_like(s_scr)

        consts = (dec_ref[...], hm_ref[...], qd_ref[...], kd_ref[...])
        s_heads = tuple(s_scr[h] for h in range(RET_HEADS))
        for c in range(G):
            rows = pl.ds(c * CHUNK, CHUNK)
            for h in range(RET_HEADS):
                sin_save_ref[c, h] = s_heads[h]
            v_heads = tuple(p_ref[rows, 2 * RET_QK + RET_DV * h:2 * RET_QK + RET_DV * (h + 1)] for h in range(RET_HEADS))
            outs, s_heads = _ret_chunk(p_ref[rows, 0:RET_QK], p_ref[rows, RET_QK:2 * RET_QK], v_heads, s_heads,
                                       cos_ref[rows, :], sin_ref[rows, :], *consts, chunk_dec)
            for h in range(RET_HEADS):
                out_ref[rows, RET_DV * h:RET_DV * (h + 1)] = outs[h]
        for h in range(RET_HEADS):
            s_scr[h] = s_heads[h]

    tok = CHUNK * G
    return pl.pallas_call(
        body, name="ret_fwd", grid=(ng,),
        in_specs=[pl.BlockSpec((tok, RET_COLS), lambda i: (i, 0)), _rows(tok, RET_QK), _rows(tok, RET_QK)] + _ret_specs(),
        out_specs=[_rows(tok, RET_WIDTH), pl.BlockSpec((G, RET_HEADS, RET_QK, RET_DV), lambda i: (i, 0, 0, 0))],
        out_shape=[jax.ShapeDtypeStruct((T, RET_WIDTH), F32),
                   jax.ShapeDtypeStruct((T // CHUNK, RET_HEADS, RET_QK, RET_DV), F32)],
        scratch_shapes=[pltpu.VMEM((RET_HEADS, RET_QK, RET_DV), F32)],
        compiler_params=_params(dimension_semantics=("arbitrary",)),
    )(p_ret, cos_t, sin_t, dec, hm, qd, kd)


def _ret_bwd(p_ret, s_saved, d_ret, tabs):
    T = p_ret.shape[0]
    G = RET_GROUP
    ng = T // (CHUNK * G)
    cos_t, sin_t, dec, hm, qd, kd, chunk_dec = tabs

    def body(p_ref, s_ref, dret_ref, cos_ref, sin_ref, dec_ref, hm_ref, qd_ref, kd_ref, dp_ref, ds_scr):
        @pl.when(pl.program_id(0) == 0)
        def _():
            ds_scr[...] = jnp.zeros_like(ds_scr)

        consts = (dec_ref[...], hm_ref[...], qd_ref[...], kd_ref[...])
        d_s = tuple(ds_scr[h] for h in range(RET_HEADS))
        for c in reversed(range(G)):
            rows = pl.ds(c * CHUNK, CHUNK)
            v_heads = tuple(p_ref[rows, 2 * RET_QK + RET_DV * h:2 * RET_QK + RET_DV * (h + 1)] for h in range(RET_HEADS))
            s_heads = tuple(s_ref[c, h] for h in range(RET_HEADS))
            tables = (cos_ref[rows, :], sin_ref[rows, :]) + consts
            _, vjp = jax.vjp(lambda a, b, c_, d: _ret_chunk(a, b, c_, d, *tables, chunk_dec),
                             p_ref[rows, 0:RET_QK], p_ref[rows, RET_QK:2 * RET_QK], v_heads, s_heads)
            d_out = tuple(dret_ref[rows, RET_DV * h:RET_DV * (h + 1)] for h in range(RET_HEADS))
            dq, dk, dv, d_s = vjp((d_out, d_s))
            dp_ref[rows, 0:RET_QK] = dq
            dp_ref[rows, RET_QK:2 * RET_QK] = dk
            for h in range(RET_HEADS):
                dp_ref[rows, 2 * RET_QK + RET_DV * h:2 * RET_QK + RET_DV * (h + 1)] = dv[h]
        for h in range(RET_HEADS):
            ds_scr[h] = d_s[h]

    tok = CHUNK * G
    rev = lambda i: (ng - 1 - i, 0)
    return pl.pallas_call(
        body, name="ret_bwd", grid=(ng,),
        in_specs=[pl.BlockSpec((tok, RET_COLS), rev),
                  pl.BlockSpec((G, RET_HEADS, RET_QK, RET_DV), lambda i: (ng - 1 - i, 0, 0, 0)),
                  pl.BlockSpec((tok, RET_WIDTH), rev), pl.BlockSpec((tok, RET_QK), rev), pl.BlockSpec((tok, RET_QK), rev)]
        + _ret_specs(),
        out_specs=pl.BlockSpec((tok, 2 * RET_QK + RET_WIDTH), rev),
        out_shape=jax.ShapeDtypeStruct((T, 2 * RET_QK + RET_WIDTH), F32),
        scratch_shapes=[pltpu.VMEM((RET_HEADS, RET_QK, RET_DV), F32)],
        compiler_params=_params(dimension_semantics=("arbitrary",)),
    )(p_ret, s_saved, d_ret, cos_t, sin_t, dec, hm, qd, kd)


def _block_ones():
    r = lax.broadcasted_iota(jnp.int32, (3 * 128, 128), 0)
    c = lax.broadcasted_iota(jnp.int32, (3 * 128, 128), 1)
    return (((r & 127) >> 6) == (c >> 6)).astype(BF16)


def _stack(x):
    return jnp.concatenate([x[:, 128 * p:128 * (p + 1)] for p in range(4)], axis=0)


def _unstack(y):
    n = y.shape[0] // 4
    return jnp.concatenate([y[n * p:n * (p + 1)] for p in range(4)], axis=1)


def _split(x, n):
    pieces = []
    for _ in range(n):
        p = x.astype(BF16)
        pieces.append(p)
        x = x - p.astype(F32)
    return pieces


def _lane_sum(x, ones):
    return _unstack(jnp.dot(_stack(x).astype(BF16), ones[:128], preferred_element_type=F32))


def _colsum(x):
    return jnp.sum(x, axis=0, keepdims=True)


def _rows_times(vecs, mat):
    n = vecs.shape[0]
    lane = lax.broadcasted_iota(jnp.int32, (n, 128), 1)
    tiles = []
    for p in range(4):
        lhs = jnp.concatenate([vecs[:, 128 * p:128 * p + RWKV_HEAD], vecs[:, 128 * p + RWKV_HEAD:128 * (p + 1)]], axis=0)
        out = jnp.dot(lhs, mat[:, 128 * p:128 * (p + 1)], preferred_element_type=F32)
        tiles.append(jnp.where(lane < RWKV_HEAD, out[:n], out[n:]))
    return jnp.concatenate(tiles, axis=1)


def _expand_cols(xt, t):
    lane = lax.broadcasted_iota(jnp.int32, (RWKV_HEAD, 128), 1)
    tiles = []
    for p in range(4):
        lo = jnp.broadcast_to(xt[128 * p:128 * p + RWKV_HEAD, t:t + 1], (RWKV_HEAD, 128))
        hi = jnp.broadcast_to(xt[128 * p + RWKV_HEAD:128 * (p + 1), t:t + 1], (RWKV_HEAD, 128))
        tiles.append(jnp.where(lane < RWKV_HEAD, lo, hi))
    return jnp.concatenate(tiles, axis=1)


def _keep_step(acc_ref, x, t):
    lane = lax.broadcasted_iota(jnp.int32, (1, RWKV_WIDTH), 1)
    mask = jnp.broadcast_to((lane & (RWKV_HEAD - 1)) == t, x.shape)
    pltpu.store(acc_ref, x, mask=mask)


def _steps_to_rows(acc):
    assert WKV_CHUNK == RWKV_HEAD
    tiles = []
    for p in range(4):
        tt = acc[:, 128 * p:128 * (p + 1)].T
        tiles.append(jnp.concatenate([tt[:RWKV_HEAD], tt[RWKV_HEAD:]], axis=1))
    return jnp.concatenate(tiles, axis=1)


def _head_sums(x, ones):
    return _unstack(jnp.dot(jnp.concatenate(_split(_stack(x), 3), axis=1), ones, preferred_element_type=F32))


def _wkv_fwd(r, w, k, v, kap, a):
    T = r.shape[0]
    C = WKV_CHUNK
    nc = T // C

    def body(r_ref, w_ref, k_ref, v_ref, kap_ref, a_ref, o_ref, s_all_ref, sa_rows_ref, s_scr, o_acc, sa_acc):
        @pl.when(pl.program_id(0) == 0)
        def _():
            s_scr[...] = jnp.zeros_like(s_scr)

        ones = _block_ones()
        rr, ww, kk, vv, kap_, aa = (ref[...] for ref in (r_ref, w_ref, k_ref, v_ref, kap_ref, a_ref))
        bb = kap_ * aa
        c1 = _head_sums(pltpu.roll(bb, 1, axis=0) * kap_, ones)
        row = lambda x, t: x[t:t + 1]

        v_cols = vv.T

        s_prev = s_scr[...]
        sa = _lane_sum(s_prev * (-row(kap_, 0)), ones)
        ls = None

        def emit_o(t, s_t):
            _keep_step(o_acc, _lane_sum(s_t * row(rr, t), ones), t)

        for t in range(C):
            u = s_prev * row(ww, t) + _expand_cols(v_cols, t) * row(kk, t)
            if t > 0:
                sa = ls - sa * row(c1, t)
            if t + 1 < C:
                ls = _lane_sum(u * (-row(kap_, t + 1)), ones)
            if t > 0:
                emit_o(t - 1, s_prev)
            s_prev = u + sa * row(bb, t)
            s_all_ref[t] = s_prev
            _keep_step(sa_acc, sa, t)
        emit_o(C - 1, s_prev)
        s_scr[...] = s_prev
        o_ref[...] = _steps_to_rows(o_acc[...])
        sa_rows_ref[...] = _steps_to_rows(sa_acc[...])

    spec = _rows(C, RWKV_WIDTH)
    return pl.pallas_call(
        body, name="wkv_fwd", grid=(nc,),
        in_specs=[spec] * 6,
        out_specs=[spec, pl.BlockSpec((C, RWKV_HEAD, RWKV_WIDTH), lambda i: (i, 0, 0)), spec],
        out_shape=[jax.ShapeDtypeStruct((T, RWKV_WIDTH), F32), jax.ShapeDtypeStruct((T, RWKV_HEAD, RWKV_WIDTH), F32),
                   jax.ShapeDtypeStruct((T, RWKV_WIDTH), F32)],
        scratch_shapes=[pltpu.VMEM((RWKV_HEAD, RWKV_WIDTH), F32)] * 3,
        compiler_params=_params(dimension_semantics=("arbitrary",)),
    )(r, w, k, v, kap, a)


def _wkv_bwd(r, w, k, v, kap, a, s_all, sa_rows, d_o):
    T = r.shape[0]
    C = WKV_CHUNK
    nc = T // C

    def body(r_ref, w_ref, k_ref, v_ref, kap_ref, a_ref, sa_rows_ref, do_ref, s_ref, s_before_ref,
             dr_ref, dw_ref, dk_ref, dv_ref, dkap_ref, da_ref, ds_scr, dv_acc, dsa_acc):
        first_chunk = pl.program_id(0) == nc - 1

        @pl.when(pl.program_id(0) == 0)
        def _():
            ds_scr[...] = jnp.zeros_like(ds_scr)

        ones = _block_ones()
        rr, ww, kk, vv, kap_, aa, sar, dd = (ref[...] for ref in (r_ref, w_ref, k_ref, v_ref, kap_ref, a_ref, sa_rows_ref, do_ref))
        bb = kap_ * aa
        e1 = _head_sums(pltpu.roll(kap_, C - 1, axis=0) * bb, ones)
        row = lambda x, t: x[t:t + 1]

        def state_before(t):
            return s_ref[t - 1] if t > 0 else jnp.where(first_chunk, 0.0, s_before_ref[0])

        do_cols = dd.T

        d_sn, dsa, rows = None, None, [None] * C

        def emit_rows(t, d_sn_t, dsa_t):
            _keep_step(dv_acc, _lane_sum(d_sn_t * row(kk, t), ones), t)
            _keep_step(dsa_acc, dsa_t, t)
            dk_db = _rows_times(jnp.concatenate([row(vv, t), row(sar, t)], axis=0), d_sn_t)
            db = dk_db[1:2]
            rows[t] = (_colsum(d_sn_t * state_before(t)), dk_db[0:1], db * row(aa, t), db * row(kap_, t))
            if t % 8 == 0:
                for j, ref in enumerate((dw_ref, dk_ref, dkap_ref, da_ref)):
                    ref[t:t + 8, :] = jnp.concatenate([rows[u][j] for u in range(t, t + 8)], axis=0)

        def state_rows():
            dsa_rows = _steps_to_rows(dsa_acc[...])
            d_r, d_kap = [], []
            for j in range(-1, C):
                lhs = ([row(dd, j)] if j >= 0 else []) + ([row(dsa_rows, j + 1)] if j + 1 < C else [])
                out = _rows_times(jnp.concatenate(lhs, axis=0), state_before(j + 1))
                if j >= 0:
                    d_r.append(out[0:1])
                if j + 1 < C:
                    d_kap.append(out[-1:])
            dr_ref[...] = jnp.concatenate(d_r, axis=0)
            dkap_ref[...] = dkap_ref[...] - jnp.concatenate(d_kap, axis=0)

        for t in reversed(range(C)):
            dof = _expand_cols(do_cols, t)
            if t == C - 1:
                d_sn = ds_scr[...] + dof * row(rr, t)
                dsa = _lane_sum(d_sn * row(bb, t), ones)
            else:
                v_t = d_sn * row(ww, t + 1) + dof * row(rr, t)
                ls = _lane_sum(v_t * row(bb, t), ones)
                emit_rows(t + 1, d_sn, dsa)
                d_sn = v_t - dsa * row(kap_, t + 1)
                dsa = ls - dsa * row(e1, t)
        emit_rows(0, d_sn, dsa)
        d_s = d_sn * row(ww, 0) - dsa * row(kap_, 0)
        ds_scr[...] = d_s
        dv_ref[...] = _steps_to_rows(dv_acc[...])
        state_rows()

    spec = pl.BlockSpec((C, RWKV_WIDTH), lambda i: (nc - 1 - i, 0))
    states = pl.BlockSpec((C, RWKV_HEAD, RWKV_WIDTH), lambda i: (nc - 1 - i, 0, 0))
    before = pl.BlockSpec((1, RWKV_HEAD, RWKV_WIDTH), lambda i: (jnp.maximum((nc - 1 - i) * C - 1, 0), 0, 0))
    return pl.pallas_call(
        body, name="wkv_bwd", grid=(nc,),
        in_specs=[spec] * 8 + [states, before],
        out_specs=[spec] * 6,
        out_shape=[jax.ShapeDtypeStruct((T, RWKV_WIDTH), F32)] * 6,
        scratch_shapes=[pltpu.VMEM((RWKV_HEAD, RWKV_WIDTH), F32)] * 3,
        compiler_params=_params(dimension_semantics=("arbitrary",)),
    )(r, w, k, v, kap, a, sa_rows, d_o, s_all, s_all)


W = RWKV_WIDTH


def _softplus(y):
    return jnp.maximum(y, 0.0) + jnp.log(1.0 + jnp.exp(-jnp.abs(y)))


def _prep_fn(kr, xwa, w0, a0, k_k, k_a, wup_pad, aup_pad, ones64):
    w_log = -_softplus(-(w0 + _mm_nt(jnp.tanh(xwa), wup_pad))) - 0.5
    decay = jnp.exp(-jnp.exp(w_log))
    a = jax.nn.sigmoid(a0 + _mm_nt(xwa, aup_pad))
    kk = kr * k_k
    kap = kk / jnp.maximum(jnp.sqrt(_head_mix(kk * kk, ones64)), 1e-12)
    k = kr * (1.0 + (a - 1.0) * k_a)
    return decay, k, kap, a


def _shift_down(p, first_row):
    rows = lax.broadcasted_iota(jnp.int32, p.shape, 0)
    return jnp.where(rows == 0, first_row, pltpu.roll(p, 1, axis=0))


def _shift_up(z, last_row):
    n = z.shape[0]
    rows = lax.broadcasted_iota(jnp.int32, z.shape, 0)
    return jnp.where(rows == n - 1, last_row, pltpu.roll(z, n - 1, axis=0))


def _prev_block_spec():
    return pl.BlockSpec((8, RWKV_COLS), lambda i: (jnp.maximum(i * (TOK_TILE // 8) - 1, 0), 0))


def _mixed(p_ref, prev8_ref, mu_ref, first_tile):
    p = p_ref[...]
    first_row = jnp.where(first_tile, 0.0, prev8_ref[7:8, :])
    prev = _shift_down(p, first_row)
    return p, prev, p + mu_ref[...] * (prev - p)


def _inproj_prep(x, norm_g, w_in_t, mu, w0, a0, k_k, k_a, wup_pad, aup_pad, ones64):
    T = x.shape[1]

    def body(x_ref, g_ref, w_in_ref, mu_ref, w0_ref, a0_ref, kk_ref, ka_ref, wup_ref, aup_ref, ones_ref,
             pr_ref, pw_ref, u_ref, r_ref, w_ref, k_ref, v_ref, kap_ref, a_ref, grw_ref, last_scr):
        @pl.when(pl.program_id(0) == 0)
        def _():
            last_scr[...] = jnp.zeros_like(last_scr)

        ub = _rmsnorm(x_ref[...], g_ref[...]).astype(BF16)
        u_ref[...] = ub
        pr_ref[...] = _dot_nt_bf(ub, w_in_ref[:RET_COLS, :])
        p = _dot_nt_bf(ub, w_in_ref[RET_COLS:, :])
        pw_ref[...] = p
        ps = p + mu_ref[...] * (_shift_down(p, last_scr[7:8, :]) - p)
        last_scr[...] = p[TOK_TILE - 8:, :]
        decay, k, kap, a = _prep_fn(ps[:, W:2 * W], ps[:, 4 * W:], w0_ref[...], a0_ref[...], kk_ref[...], ka_ref[...],
                                    wup_ref[...], aup_ref[...], ones_ref[...])
        r_ref[...] = ps[:, 0:W]
        w_ref[...] = decay
        k_ref[...] = k
        v_ref[...] = ps[:, 2 * W:3 * W]
        kap_ref[...] = kap
        a_ref[...] = a
        grw_ref[...] = ps[:, 3 * W:4 * W]

    vec = _full((1, W))
    return pl.pallas_call(
        body, name="inproj_prep", grid=(T // TOK_TILE,),
        in_specs=[_rows_of_one(TOK_TILE, D_MODEL), _full((1, D_MODEL)), _full((IN_COLS, D_MODEL)), _full((1, RWKV_COLS)),
                  vec, vec, vec, vec, _full((W, 2 * LORA)), _full((W, 2 * LORA)), _full((256, 128))],
        out_specs=[_rows(TOK_TILE, RET_COLS), _rows(TOK_TILE, RWKV_COLS), _rows(TOK_TILE, D_MODEL)] + [_rows(TOK_TILE, W)] * 7,
        out_shape=[jax.ShapeDtypeStruct((T, RET_COLS), F32), jax.ShapeDtypeStruct((T, RWKV_COLS), F32),
                   jax.ShapeDtypeStruct((T, D_MODEL), BF16)] + [jax.ShapeDtypeStruct((T, W), F32)] * 7,
        scratch_shapes=[pltpu.VMEM((8, RWKV_COLS), F32)],
        compiler_params=_params(dimension_semantics=("arbitrary",)),
    )(x, norm_g, w_in_t, mu, w0, a0, k_k, k_a, wup_pad, aup_pad, ones64)


def _prep_bwd(p_rwkv, mu, w0, a0, k_k, k_a, wup_pad, aup_pad, ones64, dr, dw, dk, dv, dkap, da, dg, dr2, dk2, dv2):
    T = p_rwkv.shape[0]
    nt = T // TOK_TILE

    def body(p_ref, prev8_ref, mu_ref, w0_ref, a0_ref, kk_ref, ka_ref, wup_ref, aup_ref, ones_ref,
             dr_ref, dw_ref, dk_ref, dv_ref, dkap_ref, da_ref, dg_ref, dr2_ref, dk2_ref, dv2_ref,
             dp_ref, dmu_ref, dw0_ref, da0_ref, dkk_ref, dka_ref, dwup_ref, daup_ref, zrow_scr):
        i = pl.program_id(0)
        accs = (dmu_ref, dw0_ref, da0_ref, dkk_ref, dka_ref, dwup_ref, daup_ref)

        @pl.when(i == 0)
        def _():
            zrow_scr[...] = jnp.zeros_like(zrow_scr)
            for ref in accs:
                ref[...] = jnp.zeros_like(ref)

        p, prev, ps = _mixed(p_ref, prev8_ref, mu_ref, i == nt - 1)
        ones = ones_ref[...]
        _, vjp = jax.vjp(lambda *args: _prep_fn(*args, ones), ps[:, W:2 * W], ps[:, 4 * W:], w0_ref[...], a0_ref[...],
                         kk_ref[...], ka_ref[...], wup_ref[...], aup_ref[...])
        dkr, dxwa, dw0, da0, dkk, dka, dwup, daup = vjp(
            (dw_ref[...], dk_ref[...] + dk2_ref[...], dkap_ref[...], da_ref[...]))
        dps = jnp.concatenate([dr_ref[...] + dr2_ref[...], dkr, dv_ref[...] + dv2_ref[...], dg_ref[...], dxwa], axis=1)
        z = dps * mu_ref[...]
        dp_ref[...] = dps - z + _shift_up(z, zrow_scr[0:1, :])
        zrow_scr[0:1, :] = z[0:1, :]
        for ref, val in zip(accs, (_colsum(dps * (prev - p)), dw0, da0, dkk, dka, dwup, daup)):
            ref[...] += val

    rev = lambda i: (nt - 1 - i, 0)
    vec = _full((1, W))
    lora = _full((W, 2 * LORA))
    tile = pl.BlockSpec((TOK_TILE, W), rev)
    prev8 = pl.BlockSpec((8, RWKV_COLS), lambda i: (jnp.maximum((nt - 1 - i) * (TOK_TILE // 8) - 1, 0), 0))
    return pl.pallas_call(
        body, name="prep_bwd", grid=(nt,),
        in_specs=[pl.BlockSpec((TOK_TILE, RWKV_COLS), rev), prev8, _full((1, RWKV_COLS)), vec, vec, vec, vec, lora, lora,
                  _full((256, 128))] + [tile] * 10,
        out_specs=[pl.BlockSpec((TOK_TILE, RWKV_COLS), rev), _full((1, RWKV_COLS)), vec, vec, vec, vec, lora, lora],
        out_shape=[jax.ShapeDtypeStruct((T, RWKV_COLS), F32), jax.ShapeDtypeStruct((1, RWKV_COLS), F32)]
        + [jax.ShapeDtypeStruct((1, W), F32)] * 4 + [jax.ShapeDtypeStruct((W, 2 * LORA), F32)] * 2,
        scratch_shapes=[pltpu.VMEM((8, RWKV_COLS), F32)],
        compiler_params=_params(dimension_semantics=("arbitrary",)),
    )(p_rwkv, p_rwkv, mu, w0, a0, k_k, k_a, wup_pad, aup_pad, ones64, dr, dw, dk, dv, dkap, da, dg, dr2, dk2, dv2)


def _silu(x):
    return x * jax.nn.sigmoid(x)


def _post_y(o, r, k, v, g_rw, ret_raw, g_ret, ret_gn_g, gn_g, gn_b, r_k, avg128, avg64, ones64):
    xc = ret_raw - _head_mix(ret_raw, avg128)
    ret = xc * lax.rsqrt(_head_mix(xc * xc, avg128) + RET_GN_EPS)
    y_ret = _silu(g_ret) * (ret * ret_gn_g)
    oc = o - _head_mix(o, avg64)
    on = oc * lax.rsqrt(_head_mix(oc * oc, avg64) + RWKV_GN_EPS) * gn_g + gn_b
    bonus = _head_mix(r * k * r_k, ones64) * v
    y_rwkv = _silu(g_rw) * (on + bonus)
    return y_ret, y_rwkv


def _post_loss(h, final_g, target):
    err = _rmsnorm(h, final_g) - target
    return 0.5 * jnp.sum(jnp.mean(err * err, axis=-1))


def _post(o, r, k, v, g_rw, ret_raw, p_ret, x, target, ret_gn_g, gn_g, gn_b, r_k, final_g, w_out, avg128, avg64, ones64):
    T = x.shape[1]
    n_tok_out = 8

    def body(o_ref, r_ref, k_ref, v_ref, grw_ref, ret_ref, gret_ref, x_ref, tgt_ref, rg_ref, gg_ref, gb_ref, rk_ref, fg_ref,
             wo_ref, a128_ref, a64_ref, ones_ref, *outs):
        tok_outs, (dwo_ref, drg_ref, dgg_ref, dgb_ref, drk_ref, dfg_ref, loss_ref) = outs[:n_tok_out], outs[n_tok_out:]
        accs = (dwo_ref, drg_ref, dgg_ref, dgb_ref, drk_ref, dfg_ref, loss_ref)

        @pl.when(pl.program_id(0) == 0)
        def _():
            for ref in accs:
                ref[...] = jnp.zeros_like(ref)

        consts = (a128_ref[...], a64_ref[...], ones_ref[...])
        (y_ret, y_rwkv), vjp = jax.vjp(
            lambda *args: _post_y(*args, *consts), o_ref[...], r_ref[...], k_ref[...], v_ref[...], grw_ref[...], ret_ref[...],
            gret_ref[...], rg_ref[...], gg_ref[...], gb_ref[...], rk_ref[...])
        h = x_ref[...] + _dot_bf(y_ret, wo_ref[0:RET_WIDTH, :]) + _dot_bf(y_rwkv, wo_ref[RET_WIDTH:, :])
        loss, (dh, dfg) = jax.value_and_grad(_post_loss, argnums=(0, 1))(h, fg_ref[...], tgt_ref[...])
        dy_ret = _dot_nt_bf(dh, wo_ref[0:RET_WIDTH, :])
        dy_rwkv = _dot_nt_bf(dh, wo_ref[RET_WIDTH:, :])
        do, dr, dk, dv, dgrw, dret, dgret, drg, dgg, dgb, drk = vjp((dy_ret, dy_rwkv))
        for ref, val in zip(tok_outs, (dh, do, dr, dk, dv, dgrw, dret, dgret)):
            ref[...] = val
        dwo_ref[0:RET_WIDTH, :] += _dot_tn_bf(y_ret, dh)
        dwo_ref[RET_WIDTH:, :] += _dot_tn_bf(y_rwkv, dh)
        for ref, val in zip(accs[1:], (drg, dgg, dgb, drk, dfg, jnp.full((1, 128), loss, F32))):
            ref[...] += val

    tile = _rows(TOK_TILE, W)
    wide = _rows(TOK_TILE, D_MODEL)
    wide_of_one = _rows_of_one(TOK_TILE, D_MODEL)
    vec = _full((1, W))
    sq = _full((256, 128))
    return pl.pallas_call(
        body, name="post", grid=(T // TOK_TILE,),
        in_specs=[tile] * 6 + [pl.BlockSpec((TOK_TILE, W), lambda i: (i, 2)), wide_of_one, wide_of_one, vec, vec, vec, vec,
                               _full((1, D_MODEL)), _full((D_MODEL, D_MODEL)), sq, sq, sq],
        out_specs=[wide] + [tile] * 7 + [_full((D_MODEL, D_MODEL)), vec, vec, vec, vec, _full((1, D_MODEL)), _full((1, 128))],
        out_shape=[jax.ShapeDtypeStruct((T, D_MODEL), F32)] + [jax.ShapeDtypeStruct((T, W), F32)] * 7
        + [jax.ShapeDtypeStruct((D_MODEL, D_MODEL), F32)] + [jax.ShapeDtypeStruct((1, W), F32)] * 4
        + [jax.ShapeDtypeStruct((1, D_MODEL), F32), jax.ShapeDtypeStruct((1, 128), F32)],
        compiler_params=_params(dimension_semantics=("arbitrary",)),
    )(o, r, k, v, g_rw, ret_raw, p_ret, x, target, ret_gn_g, gn_g, gn_b, r_k, final_g, w_out, avg128, avg64, ones64)


def _inproj_bwd_x(x, norm_g, dp_qkv, dg_ret, dp_rwkv, dh, w_in_t):
    T = x.shape[1]
    n_qkv = 2 * RET_QK + RET_WIDTH

    def body(x_ref, g_ref, dqkv_ref, dgret_ref, drwkv_ref, dh_ref, w_ref, dx_ref, dg_ref):
        @pl.when(pl.program_id(0) == 0)
        def _():
            dg_ref[...] = jnp.zeros_like(dg_ref)

        _, vjp = jax.vjp(_rmsnorm, x_ref[...], g_ref[...])
        du = (_dot_bf(dqkv_ref[...], w_ref[0:n_qkv, :]) + _dot_bf(dgret_ref[...], w_ref[n_qkv:RET_COLS, :])
              + _dot_bf(drwkv_ref[...], w_ref[RET_COLS:, :]))
        dx, dg = vjp(du)
        dx_ref[...] = dx + dh_ref[...]
        dg_ref[...] += dg

    return pl.pallas_call(
        body, name="inproj_bwd_x", grid=(T // TOK_TILE,),
        in_specs=[_rows_of_one(TOK_TILE, D_MODEL), _full((1, D_MODEL)), _rows(TOK_TILE, n_qkv), _rows(TOK_TILE, RET_WIDTH),
                  _rows(TOK_TILE, RWKV_COLS), _rows(TOK_TILE, D_MODEL), _full((IN_COLS, D_MODEL))],
        out_specs=[_rows_of_one(TOK_TILE, D_MODEL), _full((1, D_MODEL))],
        out_shape=[jax.ShapeDtypeStruct((1, T, D_MODEL), F32), jax.ShapeDtypeStruct((1, D_MODEL), F32)],
        compiler_params=_params(dimension_semantics=("arbitrary",)),
    )(x, norm_g, dp_qkv, dg_ret, dp_rwkv, dh, w_in_t)


def _grad_w_in(u, dps):
    T = u.shape[0]
    widths = [dp.shape[1] for dp in dps]
    tile = 2 * TOK_TILE
    steps = T // tile

    def body(u_ref, *refs):
        dp_refs, (mine_ref, sib_ref, acc_ref) = refs[:len(dps)], refs[len(dps):]

        @pl.when(pl.program_id(0) == 0)
        def _():
            acc_ref[...] = jnp.zeros_like(acc_ref)

        row = 0
        for dp_ref, n in zip(dp_refs, widths):
            acc_ref[row:row + n, :] += _dot_tn_bf(dp_ref[...], u_ref[...])
            row += n

        @pl.when(pl.program_id(0) == steps - 1)
        def _():
            core = lax.axis_index("c")
            for dev in range(N_DEV):
                block = acc_ref[dev * SHARD_IN:(dev + 1) * SHARD_IN, :].astype(BF16)

                @pl.when(core == dev % 2)
                def _():
                    mine_ref[dev // 2] = block

                @pl.when(core != dev % 2)
                def _():
                    sib_ref[dev // 2] = block

    half = jax.ShapeDtypeStruct((N_DEV // 2, SHARD_IN, D_MODEL), BF16)
    return pl.pallas_call(
        body, name="grad_w_in", grid=(steps,),
        in_specs=[_rows(tile, D_MODEL)] + [_rows(tile, n) for n in widths],
        out_specs=[_full(half.shape), _full(half.shape)],
        out_shape=[half, half],
        scratch_shapes=[pltpu.VMEM((IN_COLS, D_MODEL), F32)],
        compiler_params=_params(dimension_semantics=("arbitrary",)),
    )(u, *dps)


def _pad_lora(w_up_t, first):
    z = jnp.zeros_like(w_up_t)
    return jnp.concatenate([w_up_t, z] if first else [z, w_up_t], axis=1)


def _local_grads(x, target, norm_g, w_in_t, ret_gn_g, mu, w_lora_up_t, w0, a_lora_up_t, a0, k_k, k_a, r_k, gn_g, gn_b,
                 w_out_bf, final_g):
    T = x.shape[1]
    tabs = _rope_tables(T) + _ret_tables()
    ones64 = _block_mix(128, RWKV_HEAD)
    avg64 = _block_mix(128, RWKV_HEAD, 1.0 / RWKV_HEAD)
    avg128 = _block_mix(128, RET_DV, 1.0 / RET_DV)
    wup_pad, aup_pad = _pad_lora(w_lora_up_t, True), _pad_lora(a_lora_up_t, False)

    p_ret, p_rwkv, u, r, w, k, v, kap, a, g_rw = _inproj_prep(x, norm_g, w_in_t, mu, w0, a0, k_k, k_a, wup_pad, aup_pad, ones64)
    ret_raw, s_saved = _ret_fwd(p_ret, tabs)
    o, s_all, sa_rows = _wkv_fwd(r, w, k, v, kap, a)
    (dh, do, dr2, dk2, dv2, dgrw, dret, dgret, d_w_out, d_ret_gn_g, d_gn_g, d_gn_b, d_r_k, d_final_g, loss) = _post(
        o, r, k, v, g_rw, ret_raw, p_ret, x, target, ret_gn_g, gn_g, gn_b, r_k, final_g, w_out_bf, avg128, avg64, ones64)
    dr, dw, dk, dv, dkap, da = _wkv_bwd(r, w, k, v, kap, a, s_all, sa_rows, do)
    dp_rwkv, d_mu, d_w0, d_a0, d_k_k, d_k_a, d_wup, d_aup = _prep_bwd(
        p_rwkv, mu, w0, a0, k_k, k_a, wup_pad, aup_pad, ones64, dr, dw, dk, dv, dkap, da, dgrw, dr2, dk2, dv2)
    dp_qkv = _ret_bwd(p_ret, s_saved, dret, tabs)
    dx, d_norm_g = _inproj_bwd_x(x, norm_g, dp_qkv, dgret, dp_rwkv, dh, w_in_t)
    d_w_in = _grad_w_in(u, [dp_qkv, dgret, dp_rwkv])
    grads = dict(norm_g=d_norm_g, w_in=d_w_in, ret_gn_g=d_ret_gn_g, rwkv_mu=d_mu, w_lora_up=d_wup[:, :LORA], w0=d_w0,
                 a_lora_up=d_aup[:, LORA:], a0=d_a0, k_k=d_k_k, k_a=d_k_a, r_k=d_r_k, rwkv_gn_g=d_gn_g, rwkv_gn_b=d_gn_b,
                 w_out=d_w_out, final_norm_g=d_final_g)
    return loss, dx, grads


def _mesh_pos():
    return lax.axis_index("x"), lax.axis_index("y"), lax.axis_index("c")


def _all_gather(shards):
    n = len(shards)

    def body(*refs):
        x_refs, out_refs = refs[:n], refs[n:2 * n]
        send_sems, recv_sems, local_sems = refs[2 * n:]
        x, y, c = _mesh_pos()
        me, sibling = (x, y, c), (x, y, 1 - c)
        chips = [(1 - x, y), (x, 1 - y), (1 - x, 1 - y)]

        def rows(a, pos):
            m = x_refs[a].shape[0]
            return out_refs[a].at[pl.ds((4 * pos[0] + 2 * pos[1] + pos[2]) * m, m), :]

        def copy(a, k, block, to, src=None):
            return pltpu.make_async_remote_copy(
                src_ref=rows(a, block) if src is None else src, dst_ref=rows(a, block),
                send_sem=send_sems.at[a, k], recv_sem=recv_sems.at[a, k], device_id=to, device_id_type=MESH)

        mine = [pltpu.make_async_copy(x_refs[a], rows(a, me), local_sems.at[a]) for a in range(n)]
        for cp in mine:
            cp.start()
        first = []
        for a in range(n):
            first.append(copy(a, 0, me, sibling, src=x_refs[a]))
            first += [copy(a, 1 + j, me, (*chip, c), src=x_refs[a]) for j, chip in enumerate(chips)]
        for cp in first:
            cp.start()
        passed = []
        for j, chip in enumerate(chips):
            for a in range(n):
                copy(a, 1 + j, (*chip, c), me).wait_recv()
                passed.append(copy(a, 4 + j, (*chip, c), sibling))
                passed[-1].start()
        for a in range(n):
            copy(a, 0, sibling, me).wait_recv()
            for j, chip in enumerate(chips):
                copy(a, 4 + j, (*chip, 1 - c), me).wait_recv()
        for cp in first + passed:
            cp.wait_send()
        for cp in mine:
            cp.wait()

    vmem = pl.BlockSpec(memory_space=pltpu.VMEM)
    return pl.pallas_call(
        body, name="gather_weights",
        out_shape=[jax.ShapeDtypeStruct((N_DEV * s.shape[0], s.shape[1]), s.dtype) for s in shards],
        in_specs=[vmem] * n, out_specs=[vmem] * n,
        scratch_shapes=[pltpu.SemaphoreType.DMA((n, 7)), pltpu.SemaphoreType.DMA((n, 7)), pltpu.SemaphoreType.DMA((n,))],
        compiler_params=_params(),
    )(*shards)


N_CHIP = 4


def _exchange_pairs(big, small):
    nb, ns = len(big), len(small)

    def body(*refs):
        big_in, small_in = refs[:nb], refs[nb:nb + ns]
        theirs, small_out = refs[nb + ns:2 * nb + ns], refs[2 * nb + ns:2 * nb + 2 * ns]
        pair_send, pair_recv, send_sems, recv_sems, local_sems = refs[2 * nb + 2 * ns:]
        x, y, c = _mesh_pos()
        me = 4 * x + 2 * y + c
        local = [pltpu.make_async_copy(small_in[a].at[me], small_out[a].at[me], local_sems.at[a]) for a in range(ns)]
        for cp in local:
            cp.start()
        copies = [pltpu.make_async_remote_copy(
            src_ref=big_in[a], dst_ref=theirs[a], send_sem=pair_send.at[a], recv_sem=pair_recv.at[a],
            device_id=(x, y, 1 - c), device_id_type=MESH) for a in range(nb)]
        for k in range(1, N_DEV):
            peer = (x ^ (k >> 2), y ^ ((k >> 1) & 1), c ^ (k & 1))
            peer_idx = 4 * peer[0] + 2 * peer[1] + peer[2]
            copies += [pltpu.make_async_remote_copy(
                src_ref=small_in[a].at[peer_idx], dst_ref=small_out[a].at[me], send_sem=send_sems.at[a, k - 1],
                recv_sem=recv_sems.at[a, k - 1], device_id=peer, device_id_type=MESH) for a in range(ns)]
        for cp in copies:
            cp.start()
        for cp in copies:
            cp.wait()
        for cp in local:
            cp.wait()

    hbm = pl.BlockSpec(memory_space=pl.ANY)
    out_shape = [jax.ShapeDtypeStruct(p.shape, p.dtype) for p in big + small]
    dma = pltpu.SemaphoreType.DMA
    res = pl.pallas_call(
        body, name="exchange_pairs", out_shape=out_shape,
        in_specs=[hbm] * (nb + ns), out_specs=[hbm] * len(out_shape),
        scratch_shapes=[dma((nb,)), dma((nb,)), dma((ns, 7)), dma((ns, 7)), dma((ns,))],
        compiler_params=_params(),
    )(*big, *small)
    return res[:nb], res[nb:]


def _pair_sum(name, mine, theirs, row_tile):
    _, rows, cols = mine.shape

    def body(a_ref, b_ref, o_ref):
        o_ref[...] = (a_ref[...].astype(F32) + b_ref[...].astype(F32)).astype(o_ref.dtype)

    spec = pl.BlockSpec((N_CHIP, row_tile, cols), lambda i: (0, i, 0))
    return pl.pallas_call(
        body, name=name, grid=(rows // row_tile,), in_specs=[spec, spec], out_specs=spec,
        out_shape=jax.ShapeDtypeStruct(mine.shape, mine.dtype),
        compiler_params=_params(dimension_semantics=("arbitrary",)),
    )(mine, theirs)


def _exchange_chips(parts):
    n = len(parts)

    def body(*refs):
        in_refs, out_refs = refs[:n], refs[n:2 * n]
        send_sems, recv_sems, local_sems = refs[2 * n:]
        x, y, c = _mesh_pos()
        my_chip = 2 * x + y
        own = [pltpu.make_async_copy(in_refs[a].at[my_chip], out_refs[a].at[my_chip], local_sems.at[a]) for a in range(n)]
        for cp in own:
            cp.start()
        copies = []
        for k in range(1, N_CHIP):
            px, py = x ^ (k >> 1), y ^ (k & 1)
            copies += [pltpu.make_async_remote_copy(
                src_ref=in_refs[a].at[2 * px + py], dst_ref=out_refs[a].at[my_chip], send_sem=send_sems.at[a, k - 1],
                recv_sem=recv_sems.at[a, k - 1], device_id=(px, py, c), device_id_type=MESH) for a in range(n)]
        for cp in copies:
            cp.start()
        for cp in copies:
            cp.wait()
        for cp in own:
            cp.wait()

    hbm = pl.BlockSpec(memory_space=pl.ANY)
    dma = pltpu.SemaphoreType.DMA
    return pl.pallas_call(
        body, name="exchange_chips",
        out_shape=[jax.ShapeDtypeStruct(p.shape, p.dtype) for p in parts],
        in_specs=[hbm] * n, out_specs=[hbm] * n,
        scratch_shapes=[dma((n, N_CHIP - 1)), dma((n, N_CHIP - 1)), dma((n,))],
        compiler_params=_params(),
    )(*parts)


def _adamw(w, g, m, v):
    m = ADAM_B1 * m + (1.0 - ADAM_B1) * g
    v = ADAM_B2 * v + (1.0 - ADAM_B2) * (g * g)
    m_hat = m / (1.0 - ADAM_B1 ** ADAM_STEP)
    v_hat = v / (1.0 - ADAM_B2 ** ADAM_STEP)
    return -ADAM_LR * (m_hat / (jnp.sqrt(v_hat) + ADAM_EPS) + ADAM_WD * w), m, v


def _sum_parts(name, parts, row_tile):
    n_parts, rows, cols = parts.shape

    def body(p_ref, g_ref):
        g = p_ref[0].astype(F32)
        for s in range(1, n_parts):
            g = g + p_ref[s].astype(F32)
        g_ref[...] = g

    return pl.pallas_call(
        body, name=name, grid=(rows // row_tile,),
        in_specs=[pl.BlockSpec((n_parts, row_tile, cols), lambda i: (0, i, 0))],
        out_specs=pl.BlockSpec((row_tile, cols), lambda i: (i, 0)),
        out_shape=jax.ShapeDtypeStruct((rows, cols), F32),
        compiler_params=_params(dimension_semantics=("arbitrary",)),
    )(parts)


def _adamw_apply(name, g, w, m, v, row_tile):
    _, rows, cols = w.shape

    def body(g_ref, w_ref, m_ref, v_ref, d_ref, nm_ref, nv_ref):
        d_ref[0], nm_ref[0], nv_ref[0] = _adamw(w_ref[0], g_ref[0], m_ref[0], v_ref[0])

    tile = pl.BlockSpec((1, row_tile, cols), lambda i: (0, i, 0))
    return pl.pallas_call(
        body, name=name, grid=(rows // row_tile,), in_specs=[tile] * 4, out_specs=[tile] * 3,
        out_shape=[jax.ShapeDtypeStruct((1, rows, cols), F32)] * 3,
        compiler_params=_params(dimension_semantics=("arbitrary",)),
    )(g, w, m, v)


def _reduce_adamw_2d(name, parts, w, m, v, row_tile):
    n_parts, rows, cols = parts.shape

    def body(p_ref, w_ref, m_ref, v_ref, g_ref, d_ref, nm_ref, nv_ref):
        g = p_ref[0].astype(F32)
        for s in range(1, n_parts):
            g = g + p_ref[s].astype(F32)
        g_ref[...] = g
        d_ref[...], nm_ref[...], nv_ref[...] = _adamw(w_ref[...], g, m_ref[...], v_ref[...])

    tile = pl.BlockSpec((row_tile, cols), lambda i: (i, 0))
    return pl.pallas_call(
        body, name=name, grid=(rows // row_tile,),
        in_specs=[pl.BlockSpec((n_parts, row_tile, cols), lambda i: (0, i, 0)), tile, tile, tile],
        out_specs=[tile] * 4,
        out_shape=[jax.ShapeDtypeStruct((rows, cols), F32)] * 4,
        compiler_params=_params(dimension_semantics=("arbitrary",)),
    )(parts, w, m, v)


def _reduce_adamw_t(name, parts_t, w, m, v, sum_tile, row_tile):
    g = _sum_parts(name + "_sum", parts_t, sum_tile).T[None]
    return [g] + list(_adamw_apply(name, g, w, m, v, row_tile))


def _reduce_adamw(name, parts, w, m, v, row_tile):
    n_parts, rows, cols = parts.shape

    def body(p_ref, w_ref, m_ref, v_ref, g_ref, d_ref, nm_ref, nv_ref):
        g = p_ref[0].astype(F32)
        for s in range(1, n_parts):
            g = g + p_ref[s].astype(F32)
        g_ref[0] = g
        d_ref[0], nm_ref[0], nv_ref[0] = _adamw(w_ref[0], g, m_ref[0], v_ref[0])

    tile = pl.BlockSpec((1, row_tile, cols), lambda i: (0, i, 0))
    return pl.pallas_call(
        body, name=name, grid=(rows // row_tile,),
        in_specs=[pl.BlockSpec((n_parts, row_tile, cols), lambda i: (0, i, 0)), tile, tile, tile],
        out_specs=[tile] * 4,
        out_shape=[jax.ShapeDtypeStruct((1, rows, cols), F32)] * 4,
        compiler_params=_params(dimension_semantics=("arbitrary",)),
    )(parts, w, m, v)


_SMALL = (("norm_g", 1024), ("ret_gn_g", 512), ("rwkv_mu", 2176), ("w0", 512), ("a0", 512), ("k_k", 512), ("k_a", 512),
          ("r_k", 512), ("rwkv_gn_g", 512), ("rwkv_gn_b", 512), ("final_norm_g", 1024))
_SMALL_LANES = sum(n for _, n in _SMALL) + 128
_WEIGHTS = ("norm_g", "w_in", "ret_gn_g", "rwkv_mu", "w_lora_up", "w0", "a_lora_up", "a0", "k_k", "k_a", "r_k", "rwkv_gn_g",
            "rwkv_gn_b", "w_out", "final_norm_g")


def _adamw_vectors(parts, wts, mom, var):
    k = len(_SMALL)

    def body(p_ref, *refs):
        w_refs, m_refs, v_refs, outs = refs[:k], refs[k:2 * k], refs[2 * k:3 * k], refs[3 * k:]
        g_all = p_ref[0]
        for s in range(1, N_DEV):
            g_all = g_all + p_ref[s]
        off = 0
        for i, (name, n) in enumerate(_SMALL):
            g = g_all[:, off:off + n]
            off += n
            if name == "r_k":
                g = jnp.concatenate([g[:, RWKV_HEAD * h:RWKV_HEAD * (h + 1)] for h in range(RWKV_HEADS)], axis=0)[None]
            outs[4 * i][...] = g
            outs[4 * i + 1][...], outs[4 * i + 2][...], outs[4 * i + 3][...] = _adamw(
                w_refs[i][...], g, m_refs[i][...], v_refs[i][...])
        outs[4 * k][...] = g_all[:, off:off + 128]

    vmem = pl.BlockSpec(memory_space=pltpu.VMEM)
    shapes = [jax.ShapeDtypeStruct(wts[n].shape, F32) for n, _ in _SMALL for _ in range(4)] + [jax.ShapeDtypeStruct((1, 128), F32)]
    res = pl.pallas_call(
        body, name="adamw_vectors", out_shape=shapes,
        in_specs=[vmem] * (1 + 3 * k), out_specs=[vmem] * len(shapes), compiler_params=_params(),
    )(parts, *[wts[n] for n, _ in _SMALL], *[mom[n] for n, _ in _SMALL], *[var[n] for n, _ in _SMALL])
    return {n: res[4 * i:4 * i + 4] for i, (n, _) in enumerate(_SMALL)}, res[4 * k]


def kernel(x, norm_g, w_in, ret_gn_g, rwkv_mu, w_lora_up, w0, a_lora_up, a0, k_k, k_a, r_k, rwkv_gn_g, rwkv_gn_b, w_out, final_norm_g, loss_target, m_norm_g, m_w_in, m_ret_gn_g, m_rwkv_mu, m_w_lora_up, m_w0, m_a_lora_up, m_a0, m_k_k, m_k_a, m_r_k, m_rwkv_gn_g, m_rwkv_gn_b, m_w_out, m_final_norm_g, v_norm_g, v_w_in, v_ret_gn_g, v_rwkv_mu, v_w_lora_up, v_w0, v_a_lora_up, v_a0, v_k_k, v_k_a, v_r_k, v_rwkv_gn_g, v_rwkv_gn_b, v_w_out, v_final_norm_g):
    wts = dict(norm_g=norm_g, w_in=w_in, ret_gn_g=ret_gn_g, rwkv_mu=rwkv_mu, w_lora_up=w_lora_up, w0=w0, a_lora_up=a_lora_up,
               a0=a0, k_k=k_k, k_a=k_a, r_k=r_k, rwkv_gn_g=rwkv_gn_g, rwkv_gn_b=rwkv_gn_b, w_out=w_out,
               final_norm_g=final_norm_g)
    mom = dict(norm_g=m_norm_g, w_in=m_w_in, ret_gn_g=m_ret_gn_g, rwkv_mu=m_rwkv_mu, w_lora_up=m_w_lora_up, w0=m_w0,
               a_lora_up=m_a_lora_up, a0=m_a0, k_k=m_k_k, k_a=m_k_a, r_k=m_r_k, rwkv_gn_g=m_rwkv_gn_g,
               rwkv_gn_b=m_rwkv_gn_b, w_out=m_w_out, final_norm_g=m_final_norm_g)
    var = dict(norm_g=v_norm_g, w_in=v_w_in, ret_gn_g=v_ret_gn_g, rwkv_mu=v_rwkv_mu, w_lora_up=v_w_lora_up, w0=v_w0,
               a_lora_up=v_a_lora_up, a0=v_a0, k_k=v_k_k, k_a=v_k_a, r_k=v_r_k, rwkv_gn_g=v_rwkv_gn_g,
               rwkv_gn_b=v_rwkv_gn_b, w_out=v_w_out, final_norm_g=v_final_norm_g)
    shapes = {n: wts[n].shape for n in _WEIGHTS}

    w_in_t, w_out_bf, wup_t, aup_t = _all_gather(
        [w_in[0].T.astype(BF16), w_out[0].astype(BF16), w_lora_up[0].T, a_lora_up[0].T])

    loss, dx, g = _local_grads(
        x, loss_target, norm_g, w_in_t, ret_gn_g, rwkv_mu, wup_t, w0, aup_t, a0, k_k, k_a,
        r_k.reshape(1, W), rwkv_gn_g, rwkv_gn_b, w_out_bf, final_norm_g.reshape(1, D_MODEL))

    small = jnp.concatenate([g[n] for n, _ in _SMALL] + [loss], axis=1)
    core = lax.axis_index("c")
    by_core = lambda t: [lax.dynamic_index_in_dim(t, i, axis=1, keepdims=False) for i in (core, 1 - core)]
    in_mine, in_sib = g["w_in"]
    out_mine, out_sib = by_core(g["w_out"].reshape(N_CHIP, 2, SHARD_OUT, D_MODEL).astype(BF16))
    (in_theirs, out_theirs), parts = _exchange_pairs(
        [in_sib, out_sib],
        [g["w_lora_up"].reshape(N_DEV, SHARD_LORA, LORA), g["a_lora_up"].reshape(N_DEV, SHARD_LORA, LORA),
         jnp.broadcast_to(small[None], (N_DEV, 1, _SMALL_LANES))])
    by_chip = _exchange_chips([_pair_sum("pair_sum_w_in", in_mine, in_theirs, SHARD_IN // 2),
                               _pair_sum("pair_sum_w_out", out_mine, out_theirs, SHARD_OUT)])
    res = {}
    res["w_in"] = [t.T[None] for t in _reduce_adamw_2d(
        "adamw_w_in", by_chip[0], w_in[0].T, m_w_in[0].T, v_w_in[0].T, SHARD_IN // 2)]
    res["w_out"] = _reduce_adamw("adamw_w_out", by_chip[1], w_out, m_w_out, v_w_out, SHARD_OUT)
    res["w_lora_up"] = _reduce_adamw_t("adamw_w_lora_up", parts[0], w_lora_up, m_w_lora_up, v_w_lora_up, LORA, LORA)
    res["a_lora_up"] = _reduce_adamw_t("adamw_a_lora_up", parts[1], a_lora_up, m_a_lora_up, v_a_lora_up, LORA, LORA)
    as_row = lambda d: {n: d[n] if d[n].ndim > 1 else d[n].reshape(1, size) for n, size in _SMALL}
    vec, loss_row = _adamw_vectors(parts[2], as_row(wts), as_row(mom), as_row(var))
    res.update(vec)
    res = {n: [t.reshape(shapes[n]) for t in res[n]] for n in _WEIGHTS}
    return (loss_row[0, 0], dx, *[res[n][0] for n in _WEIGHTS], *[res[n][1] for n in _WEIGHTS],
            *[res[n][2] for n in _WEIGHTS], *[res[n][3] for n in _WEIGHTS])
```

```python
import numpy as np
import jax
import jax.numpy as jnp
from jax import lax
from jax.experimental import pallas as pl
from jax.experimental.pallas import tpu as pltpu

F32 = jnp.float32
BF16 = jnp.bfloat16

D_MODEL = 1024
CHUNK = 64
RET_HEADS = 4
RET_DV = 128
RET_DK = 64
RET_QK = 256
RET_WIDTH = 512
RWKV_WIDTH = 512
RWKV_HEAD = 64
RWKV_HEADS = 8
LORA = 64
RET_COLS = 2 * RET_QK + 2 * RET_WIDTH
RWKV_COLS = 4 * RWKV_WIDTH + 2 * LORA
IN_COLS = RET_COLS + RWKV_COLS
ROPE_BASE = 10000.0
RMS_EPS = 1e-6
RET_GN_EPS = 1e-5
RWKV_GN_EPS = 64e-5
ADAM_LR = 0.001
ADAM_B1 = 0.9
ADAM_B2 = 0.999
ADAM_EPS = 1e-08
ADAM_WD = 0.01
ADAM_STEP = 10
N_DEV = 8
SHARD_IN = IN_COLS // N_DEV
SHARD_OUT = D_MODEL // N_DEV
SHARD_LORA = RWKV_WIDTH // N_DEV
VMEM_LIMIT = 56 * 1024 * 1024
TOK_TILE = 256
WKV_CHUNK = 64

MESH = pl.DeviceIdType.MESH


def _dot_bf(a, b):
    return jnp.dot(a.astype(BF16), b.astype(BF16), preferred_element_type=F32)


def _dot_nt_bf(a, b):
    return lax.dot_general(a.astype(BF16), b.astype(BF16), (((1,), (1,)), ((), ())), preferred_element_type=F32)


def _dot_tn_bf(a, b):
    return lax.dot_general(a.astype(BF16), b.astype(BF16), (((0,), (0,)), ((), ())), preferred_element_type=F32)


@jax.custom_vjp
def _mm(a, b):
    return _dot_bf(a, b)


@jax.custom_vjp
def _mm_nt(a, b):
    return _dot_nt_bf(a, b)


@jax.custom_vjp
def _mm_tn(a, b):
    return _dot_tn_bf(a, b)


_mm.defvjp(lambda a, b: (_dot_bf(a, b), (a, b)), lambda res, g: (_dot_nt_bf(g, res[1]), _dot_tn_bf(res[0], g)))
_mm_nt.defvjp(lambda a, b: (_dot_nt_bf(a, b), (a, b)), lambda res, g: (_dot_bf(g, res[1]), _dot_tn_bf(g, res[0])))
_mm_tn.defvjp(lambda a, b: (_dot_tn_bf(a, b), (a, b)), lambda res, g: (_dot_nt_bf(res[1], g), _dot_bf(res[0], g)))


def _trunc(x):
    return lax.bitcast_convert_type(lax.bitcast_convert_type(x, jnp.uint32) & jnp.uint32(0xFFFF0000), F32)


def _two_piece(x):
    hi = _trunc(x)
    return jnp.concatenate([hi, x - hi], axis=1)


def _mix_raw(x, mat2):
    return _unstack(jnp.dot(_two_piece(_stack(x)), mat2, preferred_element_type=F32))


@jax.custom_vjp
def _head_mix(x, mat2):
    return _mix_raw(x, mat2)


_head_mix.defvjp(lambda x, mat2: (_mix_raw(x, mat2), mat2), lambda mat2, g: (_mix_raw(g, mat2), jnp.zeros_like(mat2)))


def _swap_halves(x):
    lane = lax.broadcasted_iota(jnp.int32, x.shape, 1)
    return jnp.where((lane & (RET_DK - 1)) < RET_DK // 2, pltpu.roll(x, RET_QK - RET_DK // 2, axis=1),
                     pltpu.roll(x, RET_DK // 2, axis=1))


@jax.custom_vjp
def _rot(x):
    return _swap_halves(x)


_rot.defvjp(lambda x: (_swap_halves(x), None), lambda _, g: (_swap_halves(g),))


def _params(**kw):
    return pltpu.CompilerParams(vmem_limit_bytes=VMEM_LIMIT, **kw)


def _full(shape):
    nd = len(shape)
    return pl.BlockSpec(shape, lambda i, _nd=nd: (0,) * _nd)


def _rows(tile, width):
    return pl.BlockSpec((tile, width), lambda i: (i, 0))


def _rows_of_one(tile, width):
    return pl.BlockSpec((None, tile, width), lambda i: (0, i, 0))


def _block_mix(n, blk, scale=1.0):
    idx = np.arange(n) // blk
    m = (idx[:, None] == idx[None, :]).astype(np.float32) * scale
    return jnp.asarray(np.concatenate([m, m], axis=0))


def _rope_tables(T):
    half = RET_DK // 2
    expo = -np.arange(half, dtype=np.float32) / np.float32(half)
    freqs = np.exp(expo * np.float32(np.log(ROPE_BASE))).astype(np.float32)
    ang = np.arange(T, dtype=np.float32)[:, None] * freqs[None, :]
    cos, sin = np.cos(ang).astype(np.float32), np.sin(ang).astype(np.float32)
    cos_h = np.concatenate([cos, cos], axis=1)
    sin_h = np.concatenate([-sin, sin], axis=1)
    cos_t = np.tile(cos_h, (1, RET_HEADS))
    sin_t = np.tile(sin_h, (1, RET_HEADS))
    return jnp.asarray(cos_t), jnp.asarray(sin_t)


def _ret_tables():
    h = np.arange(RET_HEADS, dtype=np.float32)
    lg = np.log(1.0 - np.exp2(-5.0 - h)).astype(np.float32)
    idx = np.arange(CHUNK, dtype=np.float32)
    intra = np.exp(lg[:, None, None] * np.abs(idx[:, None] - idx[None, :])).astype(np.float32)
    q_dec = np.exp(lg[:, None] * (idx[None, :] + 1.0)).astype(np.float32)
    k_dec = np.exp(lg[:, None] * (CHUNK - 1.0 - idx[None, :])).astype(np.float32)
    chunk_dec = np.exp(lg * CHUNK).astype(np.float32)
    lane_head = np.arange(RET_QK) // RET_DK
    mask = (lane_head[None, :] == np.arange(RET_HEADS)[:, None]).astype(np.float32)
    m = np.broadcast_to(mask[:, None, :], (RET_HEADS, CHUNK, RET_QK)).copy()
    qd = m * q_dec[:, :, None]
    kd = m * k_dec[:, :, None]
    return jnp.asarray(intra), jnp.asarray(m), jnp.asarray(qd), jnp.asarray(kd), [float(c) for c in chunk_dec]


def _rmsnorm(x, g):
    return x * lax.rsqrt(jnp.mean(x * x, axis=-1, keepdims=True) + RMS_EPS) * g


def _ret_chunk(pq, pk, v_heads, s_heads, cos_t, sin_t, dec, hm, qd, kd, chunk_dec):
    q = pq * cos_t + _rot(pq) * sin_t
    k = (pk * cos_t + _rot(pk) * sin_t) * (RET_DK ** -0.5)
    outs, s_out = [], []
    for h in range(RET_HEADS):
        sc = _mm_nt(q * hm[h], k * hm[h]) * dec[h]
        intra = _mm(sc, v_heads[h])
        kv = _mm_tn(k * kd[h], v_heads[h])
        inter = _mm(q * qd[h], s_heads[h])
        outs.append(intra + inter)
        s_out.append(s_heads[h] * chunk_dec[h] + kv)
    return tuple(outs), tuple(s_out)


def _ret_specs():
    const = [_full((RET_HEADS, CHUNK, CHUNK)), _full((RET_HEADS, CHUNK, RET_QK)),
             _full((RET_HEADS, CHUNK, RET_QK)), _full((RET_HEADS, CHUNK, RET_QK))]
    return const


RET_GROUP = 8


def _ret_fwd(p_ret, tabs):
    T = p_ret.shape[0]
    G = RET_GROUP
    ng = T // (CHUNK * G)
    cos_t, sin_t, dec, hm, qd, kd, chunk_dec = tabs

    def body(p_ref, cos_ref, sin_ref, dec_ref, hm_ref, qd_ref, kd_ref, out_ref, sin_save_ref, s_scr):
        @pl.when(pl.program_id(0) == 0)
        def _():
            s_scr[...] = jnp.zeros_like(s_scr)

        consts = (dec_ref[...], hm_ref[...], qd_ref[...], kd_ref[...])
        s_heads = tuple(s_scr[h] for h in range(RET_HEADS))
        for c in range(G):
            rows = pl.ds(c * CHUNK, CHUNK)
            for h in range(RET_HEADS):
                sin_save_ref[c, h] = s_heads[h]
            v_heads = tuple(p_ref[rows, 2 * RET_QK + RET_DV * h:2 * RET_QK + RET_DV * (h + 1)] for h in range(RET_HEADS))
            outs, s_heads = _ret_chunk(p_ref[rows, 0:RET_QK], p_ref[rows, RET_QK:2 * RET_QK], v_heads, s_heads,
                                       cos_ref[rows, :], sin_ref[rows, :], *consts, chunk_dec)
            for h in range(RET_HEADS):
                out_ref[rows, RET_DV * h:RET_DV * (h + 1)] = outs[h]
        for h in range(RET_HEADS):
            s_scr[h] = s_heads[h]

    tok = CHUNK * G
    return pl.pallas_call(
        body, name="ret_fwd", grid=(ng,),
        in_specs=[pl.BlockSpec((tok, RET_COLS), lambda i: (i, 0)), _rows(tok, RET_QK), _rows(tok, RET_QK)] + _ret_specs(),
        out_specs=[_rows(tok, RET_WIDTH), pl.BlockSpec((G, RET_HEADS, RET_QK, RET_DV), lambda i: (i, 0, 0, 0))],
        out_shape=[jax.ShapeDtypeStruct((T, RET_WIDTH), F32),
                   jax.ShapeDtypeStruct((T // CHUNK, RET_HEADS, RET_QK, RET_DV), F32)],
        scratch_shapes=[pltpu.VMEM((RET_HEADS, RET_QK, RET_DV), F32)],
        compiler_params=_params(dimension_semantics=("arbitrary",)),
    )(p_ret, cos_t, sin_t, dec, hm, qd, kd)


def _ret_bwd(p_ret, s_saved, d_ret, tabs):
    T = p_ret.shape[0]
    G = RET_GROUP
    ng = T // (CHUNK * G)
    cos_t, sin_t, dec, hm, qd, kd, chunk_dec = tabs

    def body(p_ref, s_ref, dret_ref, cos_ref, sin_ref, dec_ref, hm_ref, qd_ref, kd_ref, dp_ref, ds_scr):
        @pl.when(pl.program_id(0) == 0)
        def _():
            ds_scr[...] = jnp.zeros_like(ds_scr)

        consts = (dec_ref[...], hm_ref[...], qd_ref[...], kd_ref[...])
        d_s = tuple(ds_scr[h] for h in range(RET_HEADS))
        for c in reversed(range(G)):
            rows = pl.ds(c * CHUNK, CHUNK)
            v_heads = tuple(p_ref[rows, 2 * RET_QK + RET_DV * h:2 * RET_QK + RET_DV * (h + 1)] for h in range(RET_HEADS))
            s_heads = tuple(s_ref[c, h] for h in range(RET_HEADS))
            tables = (cos_ref[rows, :], sin_ref[rows, :]) + consts
            _, vjp = jax.vjp(lambda a, b, c_, d: _ret_chunk(a, b, c_, d, *tables, chunk_dec),
                             p_ref[rows, 0:RET_QK], p_ref[rows, RET_QK:2 * RET_QK], v_heads, s_heads)
            d_out = tuple(dret_ref[rows, RET_DV * h:RET_DV * (h + 1)] for h in range(RET_HEADS))
            dq, dk, dv, d_s = vjp((d_out, d_s))
            dp_ref[rows, 0:RET_QK] = dq
            dp_ref[rows, RET_QK:2 * RET_QK] = dk
            for h in range(RET_HEADS):
                dp_ref[rows, 2 * RET_QK + RET_DV * h:2 * RET_QK + RET_DV * (h + 1)] = dv[h]
        for h in range(RET_HEADS):
            ds_scr[h] = d_s[h]

    tok = CHUNK * G
    rev = lambda i: (ng - 1 - i, 0)
    return pl.pallas_call(
        body, name="ret_bwd", grid=(ng,),
        in_specs=[pl.BlockSpec((tok, RET_COLS), rev),
                  pl.BlockSpec((G, RET_HEADS, RET_QK, RET_DV), lambda i: (ng - 1 - i, 0, 0, 0)),
                  pl.BlockSpec((tok, RET_WIDTH), rev), pl.BlockSpec((tok, RET_QK), rev), pl.BlockSpec((tok, RET_QK), rev)]
        + _ret_specs(),
        out_specs=pl.BlockSpec((tok, 2 * RET_QK + RET_WIDTH), rev),
        out_shape=jax.ShapeDtypeStruct((T, 2 * RET_QK + RET_WIDTH), F32),
        scratch_shapes=[pltpu.VMEM((RET_HEADS, RET_QK, RET_DV), F32)],
        compiler_params=_params(dimension_semantics=("arbitrary",)),
    )(p_ret, s_saved, d_ret, cos_t, sin_t, dec, hm, qd, kd)


def _block_ones():
    r = lax.broadcasted_iota(jnp.int32, (3 * 128, 128), 0)
    c = lax.broadcasted_iota(jnp.int32, (3 * 128, 128), 1)
    return (((r & 127) >> 6) == (c >> 6)).astype(BF16)


def _stack(x):
    return jnp.concatenate([x[:, 128 * p:128 * (p + 1)] for p in range(4)], axis=0)


def _unstack(y):
    n = y.shape[0] // 4
    return jnp.concatenate([y[n * p:n * (p + 1)] for p in range(4)], axis=1)


def _split(x, n):
    pieces = []
    for _ in range(n):
        p = x.astype(BF16)
        pieces.append(p)
        x = x - p.astype(F32)
    return pieces


def _lane_sum(x, ones):
    return _unstack(jnp.dot(_stack(x).astype(BF16), ones[:128], preferred_element_type=F32))


def _colsum(x):
    return jnp.sum(x, axis=0, keepdims=True)


def _rows_times(vecs, mat):
    n = vecs.shape[0]
    lane = lax.broadcasted_iota(jnp.int32, (n, 128), 1)
    tiles = []
    for p in range(4):
        lhs = jnp.concatenate([vecs[:, 128 * p:128 * p + RWKV_HEAD], vecs[:, 128 * p + RWKV_HEAD:128 * (p + 1)]], axis=0)
        out = jnp.dot(lhs, mat[:, 128 * p:128 * (p + 1)], preferred_element_type=F32)
        tiles.append(jnp.where(lane < RWKV_HEAD, out[:n], out[n:]))
    return jnp.concatenate(tiles, axis=1)


def _expand_cols(xt, t):
    lane = lax.broadcasted_iota(jnp.int32, (RWKV_HEAD, 128), 1)
    tiles = []
    for p in range(4):
        lo = jnp.broadcast_to(xt[128 * p:128 * p + RWKV_HEAD, t:t + 1], (RWKV_HEAD, 128))
        hi = jnp.broadcast_to(xt[128 * p + RWKV_HEAD:128 * (p + 1), t:t + 1], (RWKV_HEAD, 128))
        tiles.append(jnp.where(lane < RWKV_HEAD, lo, hi))
    return jnp.concatenate(tiles, axis=1)


def _keep_step(acc_ref, x, t):
    lane = lax.broadcasted_iota(jnp.int32, (1, RWKV_WIDTH), 1)
    mask = jnp.broadcast_to((lane & (RWKV_HEAD - 1)) == t, x.shape)
    pltpu.store(acc_ref, x, mask=mask)


def _steps_to_rows(acc):
    assert WKV_CHUNK == RWKV_HEAD
    tiles = []
    for p in range(4):
        tt = acc[:, 128 * p:128 * (p + 1)].T
        tiles.append(jnp.concatenate([tt[:RWKV_HEAD], tt[RWKV_HEAD:]], axis=1))
    return jnp.concatenate(tiles, axis=1)


def _head_sums(x, ones):
    return _unstack(jnp.dot(jnp.concatenate(_split(_stack(x), 3), axis=1), ones, preferred_element_type=F32))


def _wkv_fwd(r, w, k, v, kap, a):
    T = r.shape[0]
    C = WKV_CHUNK
    nc = T // C

    def body(r_ref, w_ref, k_ref, v_ref, kap_ref, a_ref, o_ref, s_all_ref, sa_rows_ref, s_scr, o_acc, sa_acc):
        @pl.when(pl.program_id(0) == 0)
        def _():
            s_scr[...] = jnp.zeros_like(s_scr)

        ones = _block_ones()
        rr, ww, kk, vv, kap_, aa = (ref[...] for ref in (r_ref, w_ref, k_ref, v_ref, kap_ref, a_ref))
        bb = kap_ * aa
        c1 = _head_sums(pltpu.roll(bb, 1, axis=0) * kap_, ones)
        row = lambda x, t: x[t:t + 1]

        v_cols = vv.T

        s_prev = s_scr[...]
        sa = _lane_sum(s_prev * (-row(kap_, 0)), ones)
        ls = None

        def emit_o(t, s_t):
            _keep_step(o_acc, _lane_sum(s_t * row(rr, t), ones), t)

        for t in range(C):
            u = s_prev * row(ww, t) + _expand_cols(v_cols, t) * row(kk, t)
            if t > 0:
                sa = ls - sa * row(c1, t)
            if t + 1 < C:
                ls = _lane_sum(u * (-row(kap_, t + 1)), ones)
            if t > 0:
                emit_o(t - 1, s_prev)
            s_prev = u + sa * row(bb, t)
            s_all_ref[t] = s_prev
            _keep_step(sa_acc, sa, t)
        emit_o(C - 1, s_prev)
        s_scr[...] = s_prev
        o_ref[...] = _steps_to_rows(o_acc[...])
        sa_rows_ref[...] = _steps_to_rows(sa_acc[...])

    spec = _rows(C, RWKV_WIDTH)
    return pl.pallas_call(
        body, name="wkv_fwd", grid=(nc,),
        in_specs=[spec] * 6,
        out_specs=[spec, pl.BlockSpec((C, RWKV_HEAD, RWKV_WIDTH), lambda i: (i, 0, 0)), spec],
        out_shape=[jax.ShapeDtypeStruct((T, RWKV_WIDTH), F32), jax.ShapeDtypeStruct((T, RWKV_HEAD, RWKV_WIDTH), F32),
                   jax.ShapeDtypeStruct((T, RWKV_WIDTH), F32)],
        scratch_shapes=[pltpu.VMEM((RWKV_HEAD, RWKV_WIDTH), F32)] * 3,
        compiler_params=_params(dimension_semantics=("arbitrary",)),
    )(r, w, k, v, kap, a)


def _wkv_bwd(r, w, k, v, kap, a, s_all, sa_rows, d_o):
    T = r.shape[0]
    C = WKV_CHUNK
    nc = T // C

    def body(r_ref, w_ref, k_ref, v_ref, kap_ref, a_ref, sa_rows_ref, do_ref, s_ref, s_before_ref,
             dr_ref, dw_ref, dk_ref, dv_ref, dkap_ref, da_ref, ds_scr, dv_acc, dsa_acc):
        first_chunk = pl.program_id(0) == nc - 1

        @pl.when(pl.program_id(0) == 0)
        def _():
            ds_scr[...] = jnp.zeros_like(ds_scr)

        ones = _block_ones()
        rr, ww, kk, vv, kap_, aa, sar, dd = (ref[...] for ref in (r_ref, w_ref, k_ref, v_ref, kap_ref, a_ref, sa_rows_ref, do_ref))
        bb = kap_ * aa
        e1 = _head_sums(pltpu.roll(kap_, C - 1, axis=0) * bb, ones)
        row = lambda x, t: x[t:t + 1]

        def state_before(t):
            return s_ref[t - 1] if t > 0 else jnp.where(first_chunk, 0.0, s_before_ref[0])

        do_cols = dd.T

        d_sn, dsa, rows = None, None, [None] * C

        def emit_rows(t, d_sn_t, dsa_t):
            _keep_step(dv_acc, _lane_sum(d_sn_t * row(kk, t), ones), t)
            _keep_step(dsa_acc, dsa_t, t)
            dk_db = _rows_times(jnp.concatenate([row(vv, t), row(sar, t)], axis=0), d_sn_t)
            db = dk_db[1:2]
            rows[t] = (_colsum(d_sn_t * state_before(t)), dk_db[0:1], db * row(aa, t), db * row(kap_, t))
            if t % 8 == 0:
                for j, ref in enumerate((dw_ref, dk_ref, dkap_ref, da_ref)):
                    ref[t:t + 8, :] = jnp.concatenate([rows[u][j] for u in range(t, t + 8)], axis=0)

        def state_rows():
            dsa_rows = _steps_to_rows(dsa_acc[...])
            d_r, d_kap = [], []
            for j in range(-1, C):
                lhs = ([row(dd, j)] if j >= 0 else []) + ([row(dsa_rows, j + 1)] if j + 1 < C else [])
                out = _rows_times(jnp.concatenate(lhs, axis=0), state_before(j + 1))
                if j >= 0:
                    d_r.append(out[0:1])
                if j + 1 < C:
                    d_kap.append(out[-1:])
            dr_ref[...] = jnp.concatenate(d_r, axis=0)
            dkap_ref[...] = dkap_ref[...] - jnp.concatenate(d_kap, axis=0)

        for t in reversed(range(C)):
            dof = _expand_cols(do_cols, t)
            if t == C - 1:
                d_sn = ds_scr[...] + dof * row(rr, t)
                dsa = _lane_sum(d_sn * row(bb, t), ones)
            else:
                v_t = d_sn * row(ww, t + 1) + dof * row(rr, t)
                ls = _lane_sum(v_t * row(bb, t), ones)
                emit_rows(t + 1, d_sn, dsa)
                d_sn = v_t - dsa * row(kap_, t + 1)
                dsa = ls - dsa * row(e1, t)
        emit_rows(0, d_sn, dsa)
        d_s = d_sn * row(ww, 0) - dsa * row(kap_, 0)
        ds_scr[...] = d_s
        dv_ref[...] = _steps_to_rows(dv_acc[...])
        state_rows()

    spec = pl.BlockSpec((C, RWKV_WIDTH), lambda i: (nc - 1 - i, 0))
    states = pl.BlockSpec((C, RWKV_HEAD, RWKV_WIDTH), lambda i: (nc - 1 - i, 0, 0))
    before = pl.BlockSpec((1, RWKV_HEAD, RWKV_WIDTH), lambda i: (jnp.maximum((nc - 1 - i) * C - 1, 0), 0, 0))
    return pl.pallas_call(
        body, name="wkv_bwd", grid=(nc,),
        in_specs=[spec] * 8 + [states, before],
        out_specs=[spec] * 6,
        out_shape=[jax.ShapeDtypeStruct((T, RWKV_WIDTH), F32)] * 6,
        scratch_shapes=[pltpu.VMEM((RWKV_HEAD, RWKV_WIDTH), F32)] * 3,
        compiler_params=_params(dimension_semantics=("arbitrary",)),
    )(r, w, k, v, kap, a, sa_rows, d_o, s_all, s_all)


W = RWKV_WIDTH


def _softplus(y):
    return jnp.maximum(y, 0.0) + jnp.log(1.0 + jnp.exp(-jnp.abs(y)))


def _prep_fn(kr, xwa, w0, a0, k_k, k_a, wup_pad, aup_pad, ones64):
    w_log = -_softplus(-(w0 + _mm_nt(jnp.tanh(xwa), wup_pad))) - 0.5
    decay = jnp.exp(-jnp.exp(w_log))
    a = jax.nn.sigmoid(a0 + _mm_nt(xwa, aup_pad))
    kk = kr * k_k
    kap = kk / jnp.maximum(jnp.sqrt(_head_mix(kk * kk, ones64)), 1e-12)
    k = kr * (1.0 + (a - 1.0) * k_a)
    return decay, k, kap, a


def _shift_down(p, first_row):
    rows = lax.broadcasted_iota(jnp.int32, p.shape, 0)
    return jnp.where(rows == 0, first_row, pltpu.roll(p, 1, axis=0))


def _shift_up(z, last_row):
    n = z.shape[0]
    rows = lax.broadcasted_iota(jnp.int32, z.shape, 0)
    return jnp.where(rows == n - 1, last_row, pltpu.roll(z, n - 1, axis=0))


def _prev_block_spec():
    return pl.BlockSpec((8, RWKV_COLS), lambda i: (jnp.maximum(i * (TOK_TILE // 8) - 1, 0), 0))


def _mixed(p_ref, prev8_ref, mu_ref, first_tile):
    p = p_ref[...]
    first_row = jnp.where(first_tile, 0.0, prev8_ref[7:8, :])
    prev = _shift_down(p, first_row)
    return p, prev, p + mu_ref[...] * (prev - p)


def _inproj_prep(x, norm_g, w_in_t, mu, w0, a0, k_k, k_a, wup_pad, aup_pad, ones64):
    T = x.shape[1]

    def body(x_ref, g_ref, w_in_ref, mu_ref, w0_ref, a0_ref, kk_ref, ka_ref, wup_ref, aup_ref, ones_ref,
             pr_ref, pw_ref, r_ref, w_ref, k_ref, v_ref, kap_ref, a_ref, grw_ref, last_scr):
        @pl.when(pl.program_id(0) == 0)
        def _():
            last_scr[...] = jnp.zeros_like(last_scr)

        ub = _rmsnorm(x_ref[...], g_ref[...]).astype(BF16)
        pr_ref[...] = _dot_nt_bf(ub, w_in_ref[:RET_COLS, :])
        p = _dot_nt_bf(ub, w_in_ref[RET_COLS:, :])
        pw_ref[...] = p
        ps = p + mu_ref[...] * (_shift_down(p, last_scr[7:8, :]) - p)
        last_scr[...] = p[TOK_TILE - 8:, :]
        decay, k, kap, a = _prep_fn(ps[:, W:2 * W], ps[:, 4 * W:], w0_ref[...], a0_ref[...], kk_ref[...], ka_ref[...],
                                    wup_ref[...], aup_ref[...], ones_ref[...])
        r_ref[...] = ps[:, 0:W]
        w_ref[...] = decay
        k_ref[...] = k
        v_ref[...] = ps[:, 2 * W:3 * W]
        kap_ref[...] = kap
        a_ref[...] = a
        grw_ref[...] = ps[:, 3 * W:4 * W]

    vec = _full((1, W))
    return pl.pallas_call(
        body, name="inproj_prep", grid=(T // TOK_TILE,),
        in_specs=[_rows_of_one(TOK_TILE, D_MODEL), _full((1, D_MODEL)), _full((IN_COLS, D_MODEL)), _full((1, RWKV_COLS)),
                  vec, vec, vec, vec, _full((W, 2 * LORA)), _full((W, 2 * LORA)), _full((256, 128))],
        out_specs=[_rows(TOK_TILE, RET_COLS), _rows(TOK_TILE, RWKV_COLS)] + [_rows(TOK_TILE, W)] * 7,
        out_shape=[jax.ShapeDtypeStruct((T, RET_COLS), F32), jax.ShapeDtypeStruct((T, RWKV_COLS), F32)]
        + [jax.ShapeDtypeStruct((T, W), F32)] * 7,
        scratch_shapes=[pltpu.VMEM((8, RWKV_COLS), F32)],
        compiler_params=_params(dimension_semantics=("arbitrary",)),
    )(x, norm_g, w_in_t, mu, w0, a0, k_k, k_a, wup_pad, aup_pad, ones64)


def _prep_bwd(p_rwkv, mu, w0, a0, k_k, k_a, wup_pad, aup_pad, ones64, dr, dw, dk, dv, dkap, da, dg, dr2, dk2, dv2):
    T = p_rwkv.shape[0]
    nt = T // TOK_TILE

    def body(p_ref, prev8_ref, mu_ref, w0_ref, a0_ref, kk_ref, ka_ref, wup_ref, aup_ref, ones_ref,
             dr_ref, dw_ref, dk_ref, dv_ref, dkap_ref, da_ref, dg_ref, dr2_ref, dk2_ref, dv2_ref,
             dp_ref, dmu_ref, dw0_ref, da0_ref, dkk_ref, dka_ref, dwup_ref, daup_ref, zrow_scr):
        i = pl.program_id(0)
        accs = (dmu_ref, dw0_ref, da0_ref, dkk_ref, dka_ref, dwup_ref, daup_ref)

        @pl.when(i == 0)
        def _():
            zrow_scr[...] = jnp.zeros_like(zrow_scr)
            for ref in accs:
                ref[...] = jnp.zeros_like(ref)

        p, prev, ps = _mixed(p_ref, prev8_ref, mu_ref, i == nt - 1)
        ones = ones_ref[...]
        _, vjp = jax.vjp(lambda *args: _prep_fn(*args, ones), ps[:, W:2 * W], ps[:, 4 * W:], w0_ref[...], a0_ref[...],
                         kk_ref[...], ka_ref[...], wup_ref[...], aup_ref[...])
        dkr, dxwa, dw0, da0, dkk, dka, dwup, daup = vjp(
            (dw_ref[...], dk_ref[...] + dk2_ref[...], dkap_ref[...], da_ref[...]))
        dps = jnp.concatenate([dr_ref[...] + dr2_ref[...], dkr, dv_ref[...] + dv2_ref[...], dg_ref[...], dxwa], axis=1)
        z = dps * mu_ref[...]
        dp_ref[...] = dps - z + _shift_up(z, zrow_scr[0:1, :])
        zrow_scr[0:1, :] = z[0:1, :]
        for ref, val in zip(accs, (_colsum(dps * (prev - p)), dw0, da0, dkk, dka, dwup, daup)):
            ref[...] += val

    rev = lambda i: (nt - 1 - i, 0)
    vec = _full((1, W))
    lora = _full((W, 2 * LORA))
    tile = pl.BlockSpec((TOK_TILE, W), rev)
    prev8 = pl.BlockSpec((8, RWKV_COLS), lambda i: (jnp.maximum((nt - 1 - i) * (TOK_TILE // 8) - 1, 0), 0))
    return pl.pallas_call(
        body, name="prep_bwd", grid=(nt,),
        in_specs=[pl.BlockSpec((TOK_TILE, RWKV_COLS), rev), prev8, _full((1, RWKV_COLS)), vec, vec, vec, vec, lora, lora,
                  _full((256, 128))] + [tile] * 10,
        out_specs=[pl.BlockSpec((TOK_TILE, RWKV_COLS), rev), _full((1, RWKV_COLS)), vec, vec, vec, vec, lora, lora],
        out_shape=[jax.ShapeDtypeStruct((T, RWKV_COLS), F32), jax.ShapeDtypeStruct((1, RWKV_COLS), F32)]
        + [jax.ShapeDtypeStruct((1, W), F32)] * 4 + [jax.ShapeDtypeStruct((W, 2 * LORA), F32)] * 2,
        scratch_shapes=[pltpu.VMEM((8, RWKV_COLS), F32)],
        compiler_params=_params(dimension_semantics=("arbitrary",)),
    )(p_rwkv, p_rwkv, mu, w0, a0, k_k, k_a, wup_pad, aup_pad, ones64, dr, dw, dk, dv, dkap, da, dg, dr2, dk2, dv2)


def _silu(x):
    return x * jax.nn.sigmoid(x)


def _post_y(o, r, k, v, g_rw, ret_raw, g_ret, ret_gn_g, gn_g, gn_b, r_k, avg128, avg64, ones64):
    xc = ret_raw - _head_mix(ret_raw, avg128)
    ret = xc * lax.rsqrt(_head_mix(xc * xc, avg128) + RET_GN_EPS)
    y_ret = _silu(g_ret) * (ret * ret_gn_g)
    oc = o - _head_mix(o, avg64)
    on = oc * lax.rsqrt(_head_mix(oc * oc, avg64) + RWKV_GN_EPS) * gn_g + gn_b
    bonus = _head_mix(r * k * r_k, ones64) * v
    y_rwkv = _silu(g_rw) * (on + bonus)
    return y_ret, y_rwkv


def _post_loss(h, final_g, target):
    err = _rmsnorm(h, final_g) - target
    return 0.5 * jnp.sum(jnp.mean(err * err, axis=-1))


def _post(o, r, k, v, g_rw, ret_raw, p_ret, x, target, ret_gn_g, gn_g, gn_b, r_k, final_g, w_out, avg128, avg64, ones64):
    T = x.shape[1]
    n_tok_out = 8

    def body(o_ref, r_ref, k_ref, v_ref, grw_ref, ret_ref, gret_ref, x_ref, tgt_ref, rg_ref, gg_ref, gb_ref, rk_ref, fg_ref,
             wo_ref, a128_ref, a64_ref, ones_ref, *outs):
        tok_outs, (dwo_ref, drg_ref, dgg_ref, dgb_ref, drk_ref, dfg_ref, loss_ref) = outs[:n_tok_out], outs[n_tok_out:]
        accs = (dwo_ref, drg_ref, dgg_ref, dgb_ref, drk_ref, dfg_ref, loss_ref)

        @pl.when(pl.program_id(0) == 0)
        def _():
            for ref in accs:
                ref[...] = jnp.zeros_like(ref)

        consts = (a128_ref[...], a64_ref[...], ones_ref[...])
        (y_ret, y_rwkv), vjp = jax.vjp(
            lambda *args: _post_y(*args, *consts), o_ref[...], r_ref[...], k_ref[...], v_ref[...], grw_ref[...], ret_ref[...],
            gret_ref[...], rg_ref[...], gg_ref[...], gb_ref[...], rk_ref[...])
        h = x_ref[...] + _dot_bf(y_ret, wo_ref[0:RET_WIDTH, :]) + _dot_bf(y_rwkv, wo_ref[RET_WIDTH:, :])
        loss, (dh, dfg) = jax.value_and_grad(_post_loss, argnums=(0, 1))(h, fg_ref[...], tgt_ref[...])
        dy_ret = _dot_nt_bf(dh, wo_ref[0:RET_WIDTH, :])
        dy_rwkv = _dot_nt_bf(dh, wo_ref[RET_WIDTH:, :])
        do, dr, dk, dv, dgrw, dret, dgret, drg, dgg, dgb, drk = vjp((dy_ret, dy_rwkv))
        for ref, val in zip(tok_outs, (dh, do, dr, dk, dv, dgrw, dret, dgret)):
            ref[...] = val
        dwo_ref[0:RET_WIDTH, :] += _dot_tn_bf(y_ret, dh)
        dwo_ref[RET_WIDTH:, :] += _dot_tn_bf(y_rwkv, dh)
        for ref, val in zip(accs[1:], (drg, dgg, dgb, drk, dfg, jnp.full((1, 128), loss, F32))):
            ref[...] += val

    tile = _rows(TOK_TILE, W)
    wide = _rows(TOK_TILE, D_MODEL)
    wide_of_one = _rows_of_one(TOK_TILE, D_MODEL)
    vec = _full((1, W))
    sq = _full((256, 128))
    return pl.pallas_call(
        body, name="post", grid=(T // TOK_TILE,),
        in_specs=[tile] * 6 + [pl.BlockSpec((TOK_TILE, W), lambda i: (i, 2)), wide_of_one, wide_of_one, vec, vec, vec, vec,
                               _full((1, D_MODEL)), _full((D_MODEL, D_MODEL)), sq, sq, sq],
        out_specs=[wide] + [tile] * 7 + [_full((D_MODEL, D_MODEL)), vec, vec, vec, vec, _full((1, D_MODEL)), _full((1, 128))],
        out_shape=[jax.ShapeDtypeStruct((T, D_MODEL), F32)] + [jax.ShapeDtypeStruct((T, W), F32)] * 7
        + [jax.ShapeDtypeStruct((D_MODEL, D_MODEL), F32)] + [jax.ShapeDtypeStruct((1, W), F32)] * 4
        + [jax.ShapeDtypeStruct((1, D_MODEL), F32), jax.ShapeDtypeStruct((1, 128), F32)],
        compiler_params=_params(dimension_semantics=("arbitrary",)),
    )(o, r, k, v, g_rw, ret_raw, p_ret, x, target, ret_gn_g, gn_g, gn_b, r_k, final_g, w_out, avg128, avg64, ones64)


def _inproj_bwd(x, norm_g, dp_qkv, dg_ret, dp_rwkv, dh, w_in_t):
    T = x.shape[1]
    widths = [dp.shape[1] for dp in (dp_qkv, dg_ret, dp_rwkv)]
    steps = T // TOK_TILE

    def body(x_ref, g_ref, dqkv_ref, dgret_ref, drwkv_ref, dh_ref, w_ref, dx_ref, dg_ref, mine_ref, sib_ref, acc_ref):
        @pl.when(pl.program_id(0) == 0)
        def _():
            dg_ref[...] = jnp.zeros_like(dg_ref)
            acc_ref[...] = jnp.zeros_like(acc_ref)

        u, vjp = jax.vjp(_rmsnorm, x_ref[...], g_ref[...])
        ub = u.astype(BF16)
        du, row = 0.0, 0
        for dp_ref, n in zip((dqkv_ref, dgret_ref, drwkv_ref), widths):
            dp = dp_ref[...].astype(BF16)
            du = du + _dot_bf(dp, w_ref[row:row + n, :])
            acc_ref[row:row + n, :] += _dot_tn_bf(dp, ub)
            row += n
        dx, dg = vjp(du)
        dx_ref[...] = dx + dh_ref[...]
        dg_ref[...] += dg

        @pl.when(pl.program_id(0) == steps - 1)
        def _():
            core = lax.axis_index("c")
            for dev in range(N_DEV):
                block = acc_ref[dev * SHARD_IN:(dev + 1) * SHARD_IN, :].astype(BF16)

                @pl.when(core == dev % 2)
                def _():
                    mine_ref[dev // 2] = block

                @pl.when(core != dev % 2)
                def _():
                    sib_ref[dev // 2] = block

    half = jax.ShapeDtypeStruct((N_DEV // 2, SHARD_IN, D_MODEL), BF16)
    once = lambda shape: pl.BlockSpec(shape, lambda i: (0,) * len(shape), pipeline_mode=pl.Buffered(1))
    return pl.pallas_call(
        body, name="inproj_bwd", grid=(steps,),
        in_specs=[_rows_of_one(TOK_TILE, D_MODEL), _full((1, D_MODEL))] + [_rows(TOK_TILE, n) for n in widths]
        + [_rows(TOK_TILE, D_MODEL), once((IN_COLS, D_MODEL))],
        out_specs=[_rows_of_one(TOK_TILE, D_MODEL), _full((1, D_MODEL)), once(half.shape), once(half.shape)],
        out_shape=[jax.ShapeDtypeStruct((1, T, D_MODEL), F32), jax.ShapeDtypeStruct((1, D_MODEL), F32), half, half],
        scratch_shapes=[pltpu.VMEM((IN_COLS, D_MODEL), F32)],
        compiler_params=_params(dimension_semantics=("arbitrary",)),
    )(x, norm_g, dp_qkv, dg_ret, dp_rwkv, dh, w_in_t)


def _pad_lora(w_up_t, first):
    z = jnp.zeros_like(w_up_t)
    return jnp.concatenate([w_up_t, z] if first else [z, w_up_t], axis=1)


def _local_grads(x, target, norm_g, w_in_t, ret_gn_g, mu, w_lora_up_t, w0, a_lora_up_t, a0, k_k, k_a, r_k, gn_g, gn_b,
                 w_out_bf, final_g):
    T = x.shape[1]
    tabs = _rope_tables(T) + _ret_tables()
    ones64 = _block_mix(128, RWKV_HEAD)
    avg64 = _block_mix(128, RWKV_HEAD, 1.0 / RWKV_HEAD)
    avg128 = _block_mix(128, RET_DV, 1.0 / RET_DV)
    wup_pad, aup_pad = _pad_lora(w_lora_up_t, True), _pad_lora(a_lora_up_t, False)

    p_ret, p_rwkv, r, w, k, v, kap, a, g_rw = _inproj_prep(x, norm_g, w_in_t, mu, w0, a0, k_k, k_a, wup_pad, aup_pad, ones64)
    ret_raw, s_saved = _ret_fwd(p_ret, tabs)
    o, s_all, sa_rows = _wkv_fwd(r, w, k, v, kap, a)
    (dh, do, dr2, dk2, dv2, dgrw, dret, dgret, d_w_out, d_ret_gn_g, d_gn_g, d_gn_b, d_r_k, d_final_g, loss) = _post(
        o, r, k, v, g_rw, ret_raw, p_ret, x, target, ret_gn_g, gn_g, gn_b, r_k, final_g, w_out_bf, avg128, avg64, ones64)
    dr, dw, dk, dv, dkap, da = _wkv_bwd(r, w, k, v, kap, a, s_all, sa_rows, do)
    dp_rwkv, d_mu, d_w0, d_a0, d_k_k, d_k_a, d_wup, d_aup = _prep_bwd(
        p_rwkv, mu, w0, a0, k_k, k_a, wup_pad, aup_pad, ones64, dr, dw, dk, dv, dkap, da, dgrw, dr2, dk2, dv2)
    dp_qkv = _ret_bwd(p_ret, s_saved, dret, tabs)
    dx, d_norm_g, *d_w_in = _inproj_bwd(x, norm_g, dp_qkv, dgret, dp_rwkv, dh, w_in_t)
    grads = dict(norm_g=d_norm_g, w_in=d_w_in, ret_gn_g=d_ret_gn_g, rwkv_mu=d_mu, w_lora_up=d_wup[:, :LORA], w0=d_w0,
                 a_lora_up=d_aup[:, LORA:], a0=d_a0, k_k=d_k_k, k_a=d_k_a, r_k=d_r_k, rwkv_gn_g=d_gn_g, rwkv_gn_b=d_gn_b,
                 w_out=d_w_out, final_norm_g=d_final_g)
    return loss, dx, grads


def _mesh_pos():
    return lax.axis_index("x"), lax.axis_index("y"), lax.axis_index("c")


def _all_gather(shards):
    n = len(shards)

    def body(*refs):
        x_refs, out_refs = refs[:n], refs[n:2 * n]
        send_sems, recv_sems, local_sems = refs[2 * n:]
        x, y, c = _mesh_pos()
        me, sibling = (x, y, c), (x, y, 1 - c)
        chips = [(1 - x, y), (x, 1 - y), (1 - x, 1 - y)]

        def rows(a, pos):
            m = x_refs[a].shape[0]
            return out_refs[a].at[pl.ds((4 * pos[0] + 2 * pos[1] + pos[2]) * m, m), :]

        def copy(a, k, block, to, src=None):
            return pltpu.make_async_remote_copy(
                src_ref=rows(a, block) if src is None else src, dst_ref=rows(a, block),
                send_sem=send_sems.at[a, k], recv_sem=recv_sems.at[a, k], device_id=to, device_id_type=MESH)

        mine = [pltpu.make_async_copy(x_refs[a], rows(a, me), local_sems.at[a]) for a in range(n)]
        for cp in mine:
            cp.start()
        first = []
        for a in range(n):
            first.append(copy(a, 0, me, sibling, src=x_refs[a]))
            first += [copy(a, 1 + j, me, (*chip, c), src=x_refs[a]) for j, chip in enumerate(chips)]
        for cp in first:
            cp.start()
        passed = []
        for j, chip in enumerate(chips):
            for a in range(n):
                copy(a, 1 + j, (*chip, c), me).wait_recv()
                passed.append(copy(a, 4 + j, (*chip, c), sibling))
                passed[-1].start()
        for a in range(n):
            copy(a, 0, sibling, me).wait_recv()
            for j, chip in enumerate(chips):
                copy(a, 4 + j, (*chip, 1 - c), me).wait_recv()
        for cp in first + passed:
            cp.wait_send()
        for cp in mine:
            cp.wait()

    vmem = pl.BlockSpec(memory_space=pltpu.VMEM)
    return pl.pallas_call(
        body, name="gather_weights",
        out_shape=[jax.ShapeDtypeStruct((N_DEV * s.shape[0], s.shape[1]), s.dtype) for s in shards],
        in_specs=[vmem] * n, out_specs=[vmem] * n,
        scratch_shapes=[pltpu.SemaphoreType.DMA((n, 7)), pltpu.SemaphoreType.DMA((n, 7)), pltpu.SemaphoreType.DMA((n,))],
        compiler_params=_params(),
    )(*shards)


N_CHIP = 4


def _exchange_pairs(big, small):
    nb, ns = len(big), len(small)

    def body(*refs):
        big_in, small_in = refs[:nb], refs[nb:nb + ns]
        theirs, small_out = refs[nb + ns:2 * nb + ns], refs[2 * nb + ns:2 * nb + 2 * ns]
        pair_send, pair_recv, send_sems, recv_sems, local_sems = refs[2 * nb + 2 * ns:]
        x, y, c = _mesh_pos()
        me = 4 * x + 2 * y + c
        local = [pltpu.make_async_copy(small_in[a].at[me], small_out[a].at[me], local_sems.at[a]) for a in range(ns)]
        for cp in local:
            cp.start()
        copies = [pltpu.make_async_remote_copy(
            src_ref=big_in[a], dst_ref=theirs[a], send_sem=pair_send.at[a], recv_sem=pair_recv.at[a],
            device_id=(x, y, 1 - c), device_id_type=MESH) for a in range(nb)]
        for k in range(1, N_DEV):
            peer = (x ^ (k >> 2), y ^ ((k >> 1) & 1), c ^ (k & 1))
            peer_idx = 4 * peer[0] + 2 * peer[1] + peer[2]
            copies += [pltpu.make_async_remote_copy(
                src_ref=small_in[a].at[peer_idx], dst_ref=small_out[a].at[me], send_sem=send_sems.at[a, k - 1],
                recv_sem=recv_sems.at[a, k - 1], device_id=peer, device_id_type=MESH) for a in range(ns)]
        for cp in copies:
            cp.start()
        for cp in copies:
            cp.wait()
        for cp in local:
            cp.wait()

    hbm = pl.BlockSpec(memory_space=pl.ANY)
    out_shape = [jax.ShapeDtypeStruct(p.shape, p.dtype) for p in big + small]
    dma = pltpu.SemaphoreType.DMA
    res = pl.pallas_call(
        body, name="exchange_pairs", out_shape=out_shape,
        in_specs=[hbm] * (nb + ns), out_specs=[hbm] * len(out_shape),
        scratch_shapes=[dma((nb,)), dma((nb,)), dma((ns, 7)), dma((ns, 7)), dma((ns,))],
        compiler_params=_params(),
    )(*big, *small)
    return res[:nb], res[nb:]


def _pair_sum(name, mine, theirs, row_tile):
    _, rows, cols = mine.shape

    def body(a_ref, b_ref, o_ref):
        o_ref[...] = (a_ref[...].astype(F32) + b_ref[...].astype(F32)).astype(o_ref.dtype)

    spec = pl.BlockSpec((N_CHIP, row_tile, cols), lambda i: (0, i, 0))
    return pl.pallas_call(
        body, name=name, grid=(rows // row_tile,), in_specs=[spec, spec], out_specs=spec,
        out_shape=jax.ShapeDtypeStruct(mine.shape, mine.dtype),
        compiler_params=_params(dimension_semantics=("arbitrary",)),
    )(mine, theirs)


def _exchange_chips(parts):
    n = len(parts)

    def body(*refs):
        in_refs, out_refs = refs[:n], refs[n:2 * n]
        send_sems, recv_sems, local_sems = refs[2 * n:]
        x, y, c = _mesh_pos()
        my_chip = 2 * x + y
        own = [pltpu.make_async_copy(in_refs[a].at[my_chip], out_refs[a].at[my_chip], local_sems.at[a]) for a in range(n)]
        for cp in own:
            cp.start()
        copies = []
        for k in range(1, N_CHIP):
            px, py = x ^ (k >> 1), y ^ (k & 1)
            copies += [pltpu.make_async_remote_copy(
                src_ref=in_refs[a].at[2 * px + py], dst_ref=out_refs[a].at[my_chip], send_sem=send_sems.at[a, k - 1],
                recv_sem=recv_sems.at[a, k - 1], device_id=(px, py, c), device_id_type=MESH) for a in range(n)]
        for cp in copies:
            cp.start()
        for cp in copies:
            cp.wait()
        for cp in own:
            cp.wait()

    hbm = pl.BlockSpec(memory_space=pl.ANY)
    dma = pltpu.SemaphoreType.DMA
    return pl.pallas_call(
        body, name="exchange_chips",
        out_shape=[jax.ShapeDtypeStruct(p.shape, p.dtype) for p in parts],
        in_specs=[hbm] * n, out_specs=[hbm] * n,
        scratch_shapes=[dma((n, N_CHIP - 1)), dma((n, N_CHIP - 1)), dma((n,))],
        compiler_params=_params(),
    )(*parts)


def _adamw(w, g, m, v):
    m = ADAM_B1 * m + (1.0 - ADAM_B1) * g
    v = ADAM_B2 * v + (1.0 - ADAM_B2) * (g * g)
    m_hat = m / (1.0 - ADAM_B1 ** ADAM_STEP)
    v_hat = v / (1.0 - ADAM_B2 ** ADAM_STEP)
    return -ADAM_LR * (m_hat / (jnp.sqrt(v_hat) + ADAM_EPS) + ADAM_WD * w), m, v


def _sum_parts(name, parts, row_tile):
    n_parts, rows, cols = parts.shape

    def body(p_ref, g_ref):
        g = p_ref[0].astype(F32)
        for s in range(1, n_parts):
            g = g + p_ref[s].astype(F32)
        g_ref[...] = g

    return pl.pallas_call(
        body, name=name, grid=(rows // row_tile,),
        in_specs=[pl.BlockSpec((n_parts, row_tile, cols), lambda i: (0, i, 0))],
        out_specs=pl.BlockSpec((row_tile, cols), lambda i: (i, 0)),
        out_shape=jax.ShapeDtypeStruct((rows, cols), F32),
        compiler_params=_params(dimension_semantics=("arbitrary",)),
    )(parts)


def _adamw_apply(name, g, w, m, v, row_tile):
    _, rows, cols = w.shape

    def body(g_ref, w_ref, m_ref, v_ref, d_ref, nm_ref, nv_ref):
        d_ref[0], nm_ref[0], nv_ref[0] = _adamw(w_ref[0], g_ref[0], m_ref[0], v_ref[0])

    tile = pl.BlockSpec((1, row_tile, cols), lambda i: (0, i, 0))
    return pl.pallas_call(
        body, name=name, grid=(rows // row_tile,), in_specs=[tile] * 4, out_specs=[tile] * 3,
        out_shape=[jax.ShapeDtypeStruct((1, rows, cols), F32)] * 3,
        compiler_params=_params(dimension_semantics=("arbitrary",)),
    )(g, w, m, v)


def _reduce_adamw_2d(name, parts, w, m, v, row_tile):
    n_parts, rows, cols = parts.shape

    def body(p_ref, w_ref, m_ref, v_ref, g_ref, d_ref, nm_ref, nv_ref):
        g = p_ref[0].astype(F32)
        for s in range(1, n_parts):
            g = g + p_ref[s].astype(F32)
        g_ref[...] = g
        d_ref[...], nm_ref[...], nv_ref[...] = _adamw(w_ref[...], g, m_ref[...], v_ref[...])

    tile = pl.BlockSpec((row_tile, cols), lambda i: (i, 0))
    return pl.pallas_call(
        body, name=name, grid=(rows // row_tile,),
        in_specs=[pl.BlockSpec((n_parts, row_tile, cols), lambda i: (0, i, 0)), tile, tile, tile],
        out_specs=[tile] * 4,
        out_shape=[jax.ShapeDtypeStruct((rows, cols), F32)] * 4,
        compiler_params=_params(dimension_semantics=("arbitrary",)),
    )(parts, w, m, v)


def _reduce_adamw_t(name, parts_t, w, m, v, sum_tile, row_tile):
    g = _sum_parts(name + "_sum", parts_t, sum_tile).T[None]
    return [g] + list(_adamw_apply(name, g, w, m, v, row_tile))


def _reduce_adamw(name, parts, w, m, v, row_tile):
    n_parts, rows, cols = parts.shape

    def body(p_ref, w_ref, m_ref, v_ref, g_ref, d_ref, nm_ref, nv_ref):
        g = p_ref[0].astype(F32)
        for s in range(1, n_parts):
            g = g + p_ref[s].astype(F32)
        g_ref[0] = g
        d_ref[0], nm_ref[0], nv_ref[0] = _adamw(w_ref[0], g, m_ref[0], v_ref[0])

    tile = pl.BlockSpec((1, row_tile, cols), lambda i: (0, i, 0))
    return pl.pallas_call(
        body, name=name, grid=(rows // row_tile,),
        in_specs=[pl.BlockSpec((n_parts, row_tile, cols), lambda i: (0, i, 0)), tile, tile, tile],
        out_specs=[tile] * 4,
        out_shape=[jax.ShapeDtypeStruct((1, rows, cols), F32)] * 4,
        compiler_params=_params(dimension_semantics=("arbitrary",)),
    )(parts, w, m, v)


_SMALL = (("norm_g", 1024), ("ret_gn_g", 512), ("rwkv_mu", 2176), ("w0", 512), ("a0", 512), ("k_k", 512), ("k_a", 512),
          ("r_k", 512), ("rwkv_gn_g", 512), ("rwkv_gn_b", 512), ("final_norm_g", 1024))
_SMALL_LANES = sum(n for _, n in _SMALL) + 128
_WEIGHTS = ("norm_g", "w_in", "ret_gn_g", "rwkv_mu", "w_lora_up", "w0", "a_lora_up", "a0", "k_k", "k_a", "r_k", "rwkv_gn_g",
            "rwkv_gn_b", "w_out", "final_norm_g")


def _adamw_vectors(parts, wts, mom, var):
    k = len(_SMALL)

    def body(p_ref, *refs):
        w_refs, m_refs, v_refs, outs = refs[:k], refs[k:2 * k], refs[2 * k:3 * k], refs[3 * k:]
        g_all = p_ref[0]
        for s in range(1, N_DEV):
            g_all = g_all + p_ref[s]
        off = 0
        for i, (name, n) in enumerate(_SMALL):
            g = g_all[:, off:off + n]
            off += n
            if name == "r_k":
                g = jnp.concatenate([g[:, RWKV_HEAD * h:RWKV_HEAD * (h + 1)] for h in range(RWKV_HEADS)], axis=0)[None]
            outs[4 * i][...] = g
            outs[4 * i + 1][...], outs[4 * i + 2][...], outs[4 * i + 3][...] = _adamw(
                w_refs[i][...], g, m_refs[i][...], v_refs[i][...])
        outs[4 * k][...] = g_all[:, off:off + 128]

    vmem = pl.BlockSpec(memory_space=pltpu.VMEM)
    shapes = [jax.ShapeDtypeStruct(wts[n].shape, F32) for n, _ in _SMALL for _ in range(4)] + [jax.ShapeDtypeStruct((1, 128), F32)]
    res = pl.pallas_call(
        body, name="adamw_vectors", out_shape=shapes,
        in_specs=[vmem] * (1 + 3 * k), out_specs=[vmem] * len(shapes), compiler_params=_params(),
    )(parts, *[wts[n] for n, _ in _SMALL], *[mom[n] for n, _ in _SMALL], *[var[n] for n, _ in _SMALL])
    return {n: res[4 * i:4 * i + 4] for i, (n, _) in enumerate(_SMALL)}, res[4 * k]


def kernel(x, norm_g, w_in, ret_gn_g, rwkv_mu, w_lora_up, w0, a_lora_up, a0, k_k, k_a, r_k, rwkv_gn_g, rwkv_gn_b, w_out, final_norm_g, loss_target, m_norm_g, m_w_in, m_ret_gn_g, m_rwkv_mu, m_w_lora_up, m_w0, m_a_lora_up, m_a0, m_k_k, m_k_a, m_r_k, m_rwkv_gn_g, m_rwkv_gn_b, m_w_out, m_final_norm_g, v_norm_g, v_w_in, v_ret_gn_g, v_rwkv_mu, v_w_lora_up, v_w0, v_a_lora_up, v_a0, v_k_k, v_k_a, v_r_k, v_rwkv_gn_g, v_rwkv_gn_b, v_w_out, v_final_norm_g):
    wts = dict(norm_g=norm_g, w_in=w_in, ret_gn_g=ret_gn_g, rwkv_mu=rwkv_mu, w_lora_up=w_lora_up, w0=w0, a_lora_up=a_lora_up,
               a0=a0, k_k=k_k, k_a=k_a, r_k=r_k, rwkv_gn_g=rwkv_gn_g, rwkv_gn_b=rwkv_gn_b, w_out=w_out,
               final_norm_g=final_norm_g)
    mom = dict(norm_g=m_norm_g, w_in=m_w_in, ret_gn_g=m_ret_gn_g, rwkv_mu=m_rwkv_mu, w_lora_up=m_w_lora_up, w0=m_w0,
               a_lora_up=m_a_lora_up, a0=m_a0, k_k=m_k_k, k_a=m_k_a, r_k=m_r_k, rwkv_gn_g=m_rwkv_gn_g,
               rwkv_gn_b=m_rwkv_gn_b, w_out=m_w_out, final_norm_g=m_final_norm_g)
    var = dict(norm_g=v_norm_g, w_in=v_w_in, ret_gn_g=v_ret_gn_g, rwkv_mu=v_rwkv_mu, w_lora_up=v_w_lora_up, w0=v_w0,
               a_lora_up=v_a_lora_up, a0=v_a0, k_k=v_k_k, k_a=v_k_a, r_k=v_r_k, rwkv_gn_g=v_rwkv_gn_g,
               rwkv_gn_b=v_rwkv_gn_b, w_out=v_w_out, final_norm_g=v_final_norm_g)
    shapes = {n: wts[n].shape for n in _WEIGHTS}

    w_in_t, w_out_bf, wup_t, aup_t = _all_gather(
        [w_in[0].T.astype(BF16), w_out[0].astype(BF16), w_lora_up[0].T, a_lora_up[0].T])

    loss, dx, g = _local_grads(
        x, loss_target, norm_g, w_in_t, ret_gn_g, rwkv_mu, wup_t, w0, aup_t, a0, k_k, k_a,
        r_k.reshape(1, W), rwkv_gn_g, rwkv_gn_b, w_out_bf, final_norm_g.reshape(1, D_MODEL))

    small = jnp.concatenate([g[n] for n, _ in _SMALL] + [loss], axis=1)
    core = lax.axis_index("c")
    by_core = lambda t: [lax.dynamic_index_in_dim(t, i, axis=1, keepdims=False) for i in (core, 1 - core)]
    in_mine, in_sib = g["w_in"]
    out_mine, out_sib = by_core(g["w_out"].reshape(N_CHIP, 2, SHARD_OUT, D_MODEL).astype(BF16))
    (in_theirs, out_theirs), parts = _exchange_pairs(
        [in_sib, out_sib],
        [g["w_lora_up"].reshape(N_DEV, SHARD_LORA, LORA), g["a_lora_up"].reshape(N_DEV, SHARD_LORA, LORA),
         jnp.broadcast_to(small[None], (N_DEV, 1, _SMALL_LANES))])
    by_chip = _exchange_chips([_pair_sum("pair_sum_w_in", in_mine, in_theirs, SHARD_IN // 2),
                               _pair_sum("pair_sum_w_out", out_mine, out_theirs, SHARD_OUT)])
    res = {}
    res["w_in"] = [t.T[None] for t in _reduce_adamw_2d(
        "adamw_w_in", by_chip[0], w_in[0].T, m_w_in[0].T, v_w_in[0].T, SHARD_IN // 2)]
    res["w_out"] = _reduce_adamw("adamw_w_out", by_chip[1], w_out, m_w_out, v_w_out, SHARD_OUT)
    res["w_lora_up"] = _reduce_adamw_t("adamw_w_lora_up", parts[0], w_lora_up, m_w_lora_up, v_w_lora_up, LORA, LORA)
    res["a_lora_up"] = _reduce_adamw_t("adamw_a_lora_up", parts[1], a_lora_up, m_a_lora_up, v_a_lora_up, LORA, LORA)
    as_row = lambda d: {n: d[n] if d[n].ndim > 1 else d[n].reshape(1, size) for n, size in _SMALL}
    vec, loss_row = _adamw_vectors(parts[2], as_row(wts), as_row(mom), as_row(var))
    res.update(vec)
    res = {n: [t.reshape(shapes[n]) for t in res[n]] for n in _WEIGHTS}
    return (loss_row[0, 0], dx, *[res[n][0] for n in _WEIGHTS], *[res[n][1] for n in _WEIGHTS],
            *[res[n][2] for n in _WEIGHTS], *[res[n][3] for n in _WEIGHTS])
```

```python
import numpy as np
import jax
import jax.numpy as jnp
from jax import lax
from jax.experimental import pallas as pl
from jax.experimental.pallas import tpu as pltpu

F32 = jnp.float32
BF16 = jnp.bfloat16

D_MODEL = 1024
CHUNK = 64
RET_HEADS = 4
RET_DV = 128
RET_DK = 64
RET_QK = 256
RET_WIDTH = 512
RWKV_WIDTH = 512
RWKV_HEAD = 64
RWKV_HEADS = 8
LORA = 64
RET_COLS = 2 * RET_QK + 2 * RET_WIDTH
RWKV_COLS = 4 * RWKV_WIDTH + 2 * LORA
IN_COLS = RET_COLS + RWKV_COLS
ROPE_BASE = 10000.0
RMS_EPS = 1e-6
RET_GN_EPS = 1e-5
RWKV_GN_EPS = 64e-5
ADAM_LR = 0.001
ADAM_B1 = 0.9
ADAM_B2 = 0.999
ADAM_EPS = 1e-08
ADAM_WD = 0.01
ADAM_STEP = 10
N_DEV = 8
SHARD_IN = IN_COLS // N_DEV
SHARD_OUT = D_MODEL // N_DEV
SHARD_LORA = RWKV_WIDTH // N_DEV
VMEM_LIMIT = 56 * 1024 * 1024
TOK_TILE = 256
WKV_CHUNK = 64

MESH = pl.DeviceIdType.MESH


def _dot_bf(a, b):
    return jnp.dot(a.astype(BF16), b.astype(BF16), preferred_element_type=F32)


def _dot_nt_bf(a, b):
    return lax.dot_general(a.astype(BF16), b.astype(BF16), (((1,), (1,)), ((), ())), preferred_element_type=F32)


def _dot_tn_bf(a, b):
    return lax.dot_general(a.astype(BF16), b.astype(BF16), (((0,), (0,)), ((), ())), preferred_element_type=F32)


@jax.custom_vjp
def _mm(a, b):
    return _dot_bf(a, b)


@jax.custom_vjp
def _mm_nt(a, b):
    return _dot_nt_bf(a, b)


@jax.custom_vjp
def _mm_tn(a, b):
    return _dot_tn_bf(a, b)


_mm.defvjp(lambda a, b: (_dot_bf(a, b), (a, b)), lambda res, g: (_dot_nt_bf(g, res[1]), _dot_tn_bf(res[0], g)))
_mm_nt.defvjp(lambda a, b: (_dot_nt_bf(a, b), (a, b)), lambda res, g: (_dot_bf(g, res[1]), _dot_tn_bf(g, res[0])))
_mm_tn.defvjp(lambda a, b: (_dot_tn_bf(a, b), (a, b)), lambda res, g: (_dot_nt_bf(res[1], g), _dot_bf(res[0], g)))


def _trunc(x):
    return lax.bitcast_convert_type(lax.bitcast_convert_type(x, jnp.uint32) & jnp.uint32(0xFFFF0000), F32)


def _two_piece(x):
    hi = _trunc(x)
    return jnp.concatenate([hi, x - hi], axis=1)


def _mix_raw(x, mat2):
    return _unstack(jnp.dot(_two_piece(_stack(x)), mat2, preferred_element_type=F32))


@jax.custom_vjp
def _head_mix(x, mat2):
    return _mix_raw(x, mat2)


_head_mix.defvjp(lambda x, mat2: (_mix_raw(x, mat2), mat2), lambda mat2, g: (_mix_raw(g, mat2), jnp.zeros_like(mat2)))


def _swap_halves(x):
    lane = lax.broadcasted_iota(jnp.int32, x.shape, 1)
    return jnp.where((lane & (RET_DK - 1)) < RET_DK // 2, pltpu.roll(x, RET_QK - RET_DK // 2, axis=1),
                     pltpu.roll(x, RET_DK // 2, axis=1))


@jax.custom_vjp
def _rot(x):
    return _swap_halves(x)


_rot.defvjp(lambda x: (_swap_halves(x), None), lambda _, g: (_swap_halves(g),))


def _params(**kw):
    return pltpu.CompilerParams(vmem_limit_bytes=VMEM_LIMIT, **kw)


def _full(shape):
    nd = len(shape)
    return pl.BlockSpec(shape, lambda i, _nd=nd: (0,) * _nd)


def _rows(tile, width):
    return pl.BlockSpec((tile, width), lambda i: (i, 0))


def _rows_of_one(tile, width):
    return pl.BlockSpec((None, tile, width), lambda i: (0, i, 0))


def _block_mix(n, blk, scale=1.0):
    idx = np.arange(n) // blk
    m = (idx[:, None] == idx[None, :]).astype(np.float32) * scale
    return jnp.asarray(np.concatenate([m, m], axis=0))


def _rope_tables(T):
    half = RET_DK // 2
    expo = -np.arange(half, dtype=np.float32) / np.float32(half)
    freqs = np.exp(expo * np.float32(np.log(ROPE_BASE))).astype(np.float32)
    ang = np.arange(T, dtype=np.float32)[:, None] * freqs[None, :]
    cos, sin = np.cos(ang).astype(np.float32), np.sin(ang).astype(np.float32)
    cos_h = np.concatenate([cos, cos], axis=1)
    sin_h = np.concatenate([-sin, sin], axis=1)
    cos_t = np.tile(cos_h, (1, RET_HEADS))
    sin_t = np.tile(sin_h, (1, RET_HEADS))
    return jnp.asarray(cos_t), jnp.asarray(sin_t)


def _ret_tables():
    h = np.arange(RET_HEADS, dtype=np.float32)
    lg = np.log(1.0 - np.exp2(-5.0 - h)).astype(np.float32)
    idx = np.arange(CHUNK, dtype=np.float32)
    intra = np.exp(lg[:, None, None] * np.abs(idx[:, None] - idx[None, :])).astype(np.float32)
    q_dec = np.exp(lg[:, None] * (idx[None, :] + 1.0)).astype(np.float32)
    k_dec = np.exp(lg[:, None] * (CHUNK - 1.0 - idx[None, :])).astype(np.float32)
    chunk_dec = np.exp(lg * CHUNK).astype(np.float32)
    lane_head = np.arange(RET_QK) // RET_DK
    mask = (lane_head[None, :] == np.arange(RET_HEADS)[:, None]).astype(np.float32)
    m = np.broadcast_to(mask[:, None, :], (RET_HEADS, CHUNK, RET_QK)).copy()
    qd = m * q_dec[:, :, None]
    kd = m * k_dec[:, :, None]
    return jnp.asarray(intra), jnp.asarray(m), jnp.asarray(qd), jnp.asarray(kd), [float(c) for c in chunk_dec]


def _rmsnorm(x, g):
    return x * lax.rsqrt(jnp.mean(x * x, axis=-1, keepdims=True) + RMS_EPS) * g


def _ret_chunk(pq, pk, v_heads, s_heads, cos_t, sin_t, dec, hm, qd, kd, chunk_dec):
    q = pq * cos_t + _rot(pq) * sin_t
    k = (pk * cos_t + _rot(pk) * sin_t) * (RET_DK ** -0.5)
    outs, s_out = [], []
    for h in range(RET_HEADS):
        sc = _mm_nt(q * hm[h], k * hm[h]) * dec[h]
        intra = _mm(sc, v_heads[h])
        kv = _mm_tn(k * kd[h], v_heads[h])
        inter = _mm(q * qd[h], s_heads[h])
        outs.append(intra + inter)
        s_out.append(s_heads[h] * chunk_dec[h] + kv)
    return tuple(outs), tuple(s_out)


def _ret_specs():
    const = [_full((RET_HEADS, CHUNK, CHUNK)), _full((RET_HEADS, CHUNK, RET_QK)),
             _full((RET_HEADS, CHUNK, RET_QK)), _full((RET_HEADS, CHUNK, RET_QK))]
    return const


RET_GROUP = 8


def _ret_fwd(p_ret, tabs):
    T = p_ret.shape[0]
    G = RET_GROUP
    ng = T // (CHUNK * G)
    cos_t, sin_t, dec, hm, qd, kd, chunk_dec = tabs

    def body(p_ref, cos_ref, sin_ref, dec_ref, hm_ref, qd_ref, kd_ref, out_ref, sin_save_ref, s_scr):
        @pl.when(pl.program_id(0) == 0)
        def _():
            s_scr[...] = jnp.zeros_like(s_scr)

        consts = (dec_ref[...], hm_ref[...], qd_ref[...], kd_ref[...])
        s_heads = tuple(s_scr[h] for h in range(RET_HEADS))
        for c in range(G):
            rows = pl.ds(c * CHUNK, CHUNK)
            for h in range(RET_HEADS):
                sin_save_ref[c, h] = s_heads[h]
            v_heads = tuple(p_ref[rows, 2 * RET_QK + RET_DV * h:2 * RET_QK + RET_DV * (h + 1)] for h in range(RET_HEADS))
            outs, s_heads = _ret_chunk(p_ref[rows, 0:RET_QK], p_ref[rows, RET_QK:2 * RET_QK], v_heads, s_heads,
                                       cos_ref[rows, :], sin_ref[rows, :], *consts, chunk_dec)
            for h in range(RET_HEADS):
                out_ref[rows, RET_DV * h:RET_DV * (h + 1)] = outs[h]
        for h in range(RET_HEADS):
            s_scr[h] = s_heads[h]

    tok = CHUNK * G
    return pl.pallas_call(
        body, name="ret_fwd", grid=(ng,),
        in_specs=[pl.BlockSpec((tok, RET_COLS), lambda i: (i, 0)), _rows(tok, RET_QK), _rows(tok, RET_QK)] + _ret_specs(),
        out_specs=[_rows(tok, RET_WIDTH), pl.BlockSpec((G, RET_HEADS, RET_QK, RET_DV), lambda i: (i, 0, 0, 0))],
        out_shape=[jax.ShapeDtypeStruct((T, RET_WIDTH), F32),
                   jax.ShapeDtypeStruct((T // CHUNK, RET_HEADS, RET_QK, RET_DV), F32)],
        scratch_shapes=[pltpu.VMEM((RET_HEADS, RET_QK, RET_DV), F32)],
        compiler_params=_params(dimension_semantics=("arbitrary",)),
    )(p_ret, cos_t, sin_t, dec, hm, qd, kd)


def _ret_bwd(p_ret, s_saved, d_ret, tabs):
    T = p_ret.shape[0]
    G = RET_GROUP
    ng = T // (CHUNK * G)
    cos_t, sin_t, dec, hm, qd, kd, chunk_dec = tabs

    def body(p_ref, s_ref, dret_ref, cos_ref, sin_ref, dec_ref, hm_ref, qd_ref, kd_ref, dp_ref, ds_scr):
        @pl.when(pl.program_id(0) == 0)
        def _():
            ds_scr[...] = jnp.zeros_like(ds_scr)

        consts = (dec_ref[...], hm_ref[...], qd_ref[...], kd_ref[...])
        d_s = tuple(ds_scr[h] for h in range(RET_HEADS))
        for c in reversed(range(G)):
            rows = pl.ds(c * CHUNK, CHUNK)
            v_heads = tuple(p_ref[rows, 2 * RET_QK + RET_DV * h:2 * RET_QK + RET_DV * (h + 1)] for h in range(RET_HEADS))
            s_heads = tuple(s_ref[c, h] for h in range(RET_HEADS))
            tables = (cos_ref[rows, :], sin_ref[rows, :]) + consts
            _, vjp = jax.vjp(lambda a, b, c_, d: _ret_chunk(a, b, c_, d, *tables, chunk_dec),
                             p_ref[rows, 0:RET_QK], p_ref[rows, RET_QK:2 * RET_QK], v_heads, s_heads)
            d_out = tuple(dret_ref[rows, RET_DV * h:RET_DV * (h + 1)] for h in range(RET_HEADS))
            dq, dk, dv, d_s = vjp((d_out, d_s))
            dp_ref[rows, 0:RET_QK] = dq
            dp_ref[rows, RET_QK:2 * RET_QK] = dk
            for h in range(RET_HEADS):
                dp_ref[rows, 2 * RET_QK + RET_DV * h:2 * RET_QK + RET_DV * (h + 1)] = dv[h]
        for h in range(RET_HEADS):
            ds_scr[h] = d_s[h]

    tok = CHUNK * G
    rev = lambda i: (ng - 1 - i, 0)
    return pl.pallas_call(
        body, name="ret_bwd", grid=(ng,),
        in_specs=[pl.BlockSpec((tok, RET_COLS), rev),
                  pl.BlockSpec((G, RET_HEADS, RET_QK, RET_DV), lambda i: (ng - 1 - i, 0, 0, 0)),
                  pl.BlockSpec((tok, RET_WIDTH), rev), pl.BlockSpec((tok, RET_QK), rev), pl.BlockSpec((tok, RET_QK), rev)]
        + _ret_specs(),
        out_specs=pl.BlockSpec((tok, 2 * RET_QK + RET_WIDTH), rev),
        out_shape=jax.ShapeDtypeStruct((T, 2 * RET_QK + RET_WIDTH), F32),
        scratch_shapes=[pltpu.VMEM((RET_HEADS, RET_QK, RET_DV), F32)],
        compiler_params=_params(dimension_semantics=("arbitrary",)),
    )(p_ret, s_saved, d_ret, cos_t, sin_t, dec, hm, qd, kd)


def _block_ones():
    r = lax.broadcasted_iota(jnp.int32, (3 * 128, 128), 0)
    c = lax.broadcasted_iota(jnp.int32, (3 * 128, 128), 1)
    return (((r & 127) >> 6) == (c >> 6)).astype(BF16)


def _stack(x):
    return jnp.concatenate([x[:, 128 * p:128 * (p + 1)] for p in range(4)], axis=0)


def _unstack(y):
    n = y.shape[0] // 4
    return jnp.concatenate([y[n * p:n * (p + 1)] for p in range(4)], axis=1)


def _split(x, n):
    pieces = []
    for _ in range(n):
        p = x.astype(BF16)
        pieces.append(p)
        x = x - p.astype(F32)
    return pieces


def _lane_sum(x, ones):
    return _unstack(jnp.dot(_stack(x).astype(BF16), ones[:128], preferred_element_type=F32))


def _colsum(x):
    return jnp.sum(x, axis=0, keepdims=True)


def _rows_times(vecs, mat):
    n = vecs.shape[0]
    lane = lax.broadcasted_iota(jnp.int32, (n, 128), 1)
    lhs = jnp.concatenate([vecs[:, RWKV_HEAD * h:RWKV_HEAD * (h + 1)] for h in range(RWKV_HEADS)], axis=0)
    out = jnp.dot(lhs, mat, preferred_element_type=F32)
    tiles = [jnp.where(lane < RWKV_HEAD, out[2 * p * n:(2 * p + 1) * n, 128 * p:128 * (p + 1)],
                       out[(2 * p + 1) * n:(2 * p + 2) * n, 128 * p:128 * (p + 1)]) for p in range(4)]
    return jnp.concatenate(tiles, axis=1)


def _expand_cols(xt, t):
    lane = lax.broadcasted_iota(jnp.int32, (RWKV_HEAD, 128), 1)
    tiles = []
    for p in range(4):
        lo = jnp.broadcast_to(xt[128 * p:128 * p + RWKV_HEAD, t:t + 1], (RWKV_HEAD, 128))
        hi = jnp.broadcast_to(xt[128 * p + RWKV_HEAD:128 * (p + 1), t:t + 1], (RWKV_HEAD, 128))
        tiles.append(jnp.where(lane < RWKV_HEAD, lo, hi))
    return jnp.concatenate(tiles, axis=1)


def _keep_step(acc_ref, x, t):
    lane = lax.broadcasted_iota(jnp.int32, (1, RWKV_WIDTH), 1)
    mask = jnp.broadcast_to((lane & (RWKV_HEAD - 1)) == t, x.shape)
    pltpu.store(acc_ref, x, mask=mask)


def _steps_to_rows(acc):
    assert WKV_CHUNK == RWKV_HEAD
    tiles = []
    for p in range(4):
        tt = acc[:, 128 * p:128 * (p + 1)].T
        tiles.append(jnp.concatenate([tt[:RWKV_HEAD], tt[RWKV_HEAD:]], axis=1))
    return jnp.concatenate(tiles, axis=1)


def _head_sums(x, ones):
    return _unstack(jnp.dot(jnp.concatenate(_split(_stack(x), 3), axis=1), ones, preferred_element_type=F32))


def _wkv_fwd(r, w, k, v, kap, a):
    T = r.shape[0]
    C = WKV_CHUNK
    nc = T // C

    def body(r_ref, w_ref, k_ref, v_ref, kap_ref, a_ref, o_ref, s_all_ref, sa_rows_ref, s_scr, o_acc, sa_acc):
        @pl.when(pl.program_id(0) == 0)
        def _():
            s_scr[...] = jnp.zeros_like(s_scr)

        ones = _block_ones()
        rr, ww, kk, vv, kap_, aa = (ref[...] for ref in (r_ref, w_ref, k_ref, v_ref, kap_ref, a_ref))
        bb = kap_ * aa
        c1 = _head_sums(pltpu.roll(bb, 1, axis=0) * kap_, ones)
        row = lambda x, t: x[t:t + 1]

        v_cols = vv.T

        s_prev = s_scr[...]
        sa = _lane_sum(s_prev * (-row(kap_, 0)), ones)
        ls = None

        def emit_o(t, s_t):
            _keep_step(o_acc, _lane_sum(s_t * row(rr, t), ones), t)

        for t in range(C):
            u = s_prev * row(ww, t) + _expand_cols(v_cols, t) * row(kk, t)
            if t > 0:
                sa = ls - sa * row(c1, t)
            if t + 1 < C:
                ls = _lane_sum(u * (-row(kap_, t + 1)), ones)
            if t > 0:
                emit_o(t - 1, s_prev)
            s_prev = u + sa * row(bb, t)
            s_all_ref[t] = s_prev
            _keep_step(sa_acc, sa, t)
        emit_o(C - 1, s_prev)
        s_scr[...] = s_prev
        o_ref[...] = _steps_to_rows(o_acc[...])
        sa_rows_ref[...] = _steps_to_rows(sa_acc[...])

    spec = _rows(C, RWKV_WIDTH)
    return pl.pallas_call(
        body, name="wkv_fwd", grid=(nc,),
        in_specs=[spec] * 6,
        out_specs=[spec, pl.BlockSpec((C, RWKV_HEAD, RWKV_WIDTH), lambda i: (i, 0, 0)), spec],
        out_shape=[jax.ShapeDtypeStruct((T, RWKV_WIDTH), F32), jax.ShapeDtypeStruct((T, RWKV_HEAD, RWKV_WIDTH), F32),
                   jax.ShapeDtypeStruct((T, RWKV_WIDTH), F32)],
        scratch_shapes=[pltpu.VMEM((RWKV_HEAD, RWKV_WIDTH), F32)] * 3,
        compiler_params=_params(dimension_semantics=("arbitrary",)),
    )(r, w, k, v, kap, a)


def _wkv_bwd(r, w, k, v, kap, a, s_all, sa_rows, d_o):
    T = r.shape[0]
    C = WKV_CHUNK
    nc = T // C

    def body(r_ref, w_ref, k_ref, v_ref, kap_ref, a_ref, sa_rows_ref, do_ref, s_ref, s_before_ref,
             dr_ref, dw_ref, dk_ref, dv_ref, dkap_ref, da_ref, ds_scr, dv_acc, dsa_acc):
        first_chunk = pl.program_id(0) == nc - 1

        @pl.when(pl.program_id(0) == 0)
        def _():
            ds_scr[...] = jnp.zeros_like(ds_scr)

        ones = _block_ones()
        rr, ww, kk, vv, kap_, aa, sar, dd = (ref[...] for ref in (r_ref, w_ref, k_ref, v_ref, kap_ref, a_ref, sa_rows_ref, do_ref))
        bb = kap_ * aa
        e1 = _head_sums(pltpu.roll(kap_, C - 1, axis=0) * bb, ones)
        row = lambda x, t: x[t:t + 1]

        def state_before(t):
            return s_ref[t - 1] if t > 0 else jnp.where(first_chunk, 0.0, s_before_ref[0])

        do_cols = dd.T

        d_sn, dsa, rows = None, None, [None] * C

        def emit_rows(t, d_sn_t, dsa_t):
            _keep_step(dv_acc, _lane_sum(d_sn_t * row(kk, t), ones), t)
            _keep_step(dsa_acc, dsa_t, t)
            dk_db = _rows_times(jnp.concatenate([row(vv, t), row(sar, t)], axis=0), d_sn_t)
            db = dk_db[1:2]
            rows[t] = (_colsum(d_sn_t * state_before(t)), dk_db[0:1], db * row(aa, t), db * row(kap_, t))
            if t % 8 == 0:
                for j, ref in enumerate((dw_ref, dk_ref, dkap_ref, da_ref)):
                    ref[t:t + 8, :] = jnp.concatenate([rows[u][j] for u in range(t, t + 8)], axis=0)

        def state_rows():
            dsa_rows = _steps_to_rows(dsa_acc[...])
            d_r, d_kap = [], []
            for j in range(-1, C):
                lhs = ([row(dd, j)] if j >= 0 else []) + ([row(dsa_rows, j + 1)] if j + 1 < C else [])
                out = _rows_times(jnp.concatenate(lhs, axis=0), state_before(j + 1))
                if j >= 0:
                    d_r.append(out[0:1])
                if j + 1 < C:
                    d_kap.append(out[-1:])
            dr_ref[...] = jnp.concatenate(d_r, axis=0)
            dkap_ref[...] = dkap_ref[...] - jnp.concatenate(d_kap, axis=0)

        for t in reversed(range(C)):
            dof = _expand_cols(do_cols, t)
            if t == C - 1:
                d_sn = ds_scr[...] + dof * row(rr, t)
                dsa = _lane_sum(d_sn * row(bb, t), ones)
            else:
                v_t = d_sn * row(ww, t + 1) + dof * row(rr, t)
                ls = _lane_sum(v_t * row(bb, t), ones)
                emit_rows(t + 1, d_sn, dsa)
                d_sn = v_t - dsa * row(kap_, t + 1)
                dsa = ls - dsa * row(e1, t)
        emit_rows(0, d_sn, dsa)
        d_s = d_sn * row(ww, 0) - dsa * row(kap_, 0)
        ds_scr[...] = d_s
        dv_ref[...] = _steps_to_rows(dv_acc[...])
        state_rows()

    spec = pl.BlockSpec((C, RWKV_WIDTH), lambda i: (nc - 1 - i, 0))
    states = pl.BlockSpec((C, RWKV_HEAD, RWKV_WIDTH), lambda i: (nc - 1 - i, 0, 0))
    before = pl.BlockSpec((1, RWKV_HEAD, RWKV_WIDTH), lambda i: (jnp.maximum((nc - 1 - i) * C - 1, 0), 0, 0))
    return pl.pallas_call(
        body, name="wkv_bwd", grid=(nc,),
        in_specs=[spec] * 8 + [states, before],
        out_specs=[spec] * 6,
        out_shape=[jax.ShapeDtypeStruct((T, RWKV_WIDTH), F32)] * 6,
        scratch_shapes=[pltpu.VMEM((RWKV_HEAD, RWKV_WIDTH), F32)] * 3,
        compiler_params=_params(dimension_semantics=("arbitrary",)),
    )(r, w, k, v, kap, a, sa_rows, d_o, s_all, s_all)


W = RWKV_WIDTH


def _softplus(y):
    return jnp.maximum(y, 0.0) + jnp.log(1.0 + jnp.exp(-jnp.abs(y)))


def _prep_fn(kr, xwa, w0, a0, k_k, k_a, wup_pad, aup_pad, ones64):
    w_log = -_softplus(-(w0 + _mm_nt(jnp.tanh(xwa), wup_pad))) - 0.5
    decay = jnp.exp(-jnp.exp(w_log))
    a = jax.nn.sigmoid(a0 + _mm_nt(xwa, aup_pad))
    kk = kr * k_k
    kap = kk / jnp.maximum(jnp.sqrt(_head_mix(kk * kk, ones64)), 1e-12)
    k = kr * (1.0 + (a - 1.0) * k_a)
    return decay, k, kap, a


def _shift_down(p, first_row):
    rows = lax.broadcasted_iota(jnp.int32, p.shape, 0)
    return jnp.where(rows == 0, first_row, pltpu.roll(p, 1, axis=0))


def _shift_up(z, last_row):
    n = z.shape[0]
    rows = lax.broadcasted_iota(jnp.int32, z.shape, 0)
    return jnp.where(rows == n - 1, last_row, pltpu.roll(z, n - 1, axis=0))


def _prev_block_spec():
    return pl.BlockSpec((8, RWKV_COLS), lambda i: (jnp.maximum(i * (TOK_TILE // 8) - 1, 0), 0))


def _mixed(p_ref, prev8_ref, mu_ref, first_tile):
    p = p_ref[...]
    first_row = jnp.where(first_tile, 0.0, prev8_ref[7:8, :])
    prev = _shift_down(p, first_row)
    return p, prev, p + mu_ref[...] * (prev - p)


def _inproj_prep(x, norm_g, w_in_t, mu, w0, a0, k_k, k_a, wup_pad, aup_pad, ones64):
    T = x.shape[1]

    def body(x_ref, g_ref, w_in_ref, mu_ref, w0_ref, a0_ref, kk_ref, ka_ref, wup_ref, aup_ref, ones_ref,
             pr_ref, pw_ref, r_ref, w_ref, k_ref, v_ref, kap_ref, a_ref, grw_ref, last_scr):
        @pl.when(pl.program_id(0) == 0)
        def _():
            last_scr[...] = jnp.zeros_like(last_scr)

        ub = _rmsnorm(x_ref[...], g_ref[...]).astype(BF16)
        pr_ref[...] = _dot_nt_bf(ub, w_in_ref[:RET_COLS, :])
        p = _dot_nt_bf(ub, w_in_ref[RET_COLS:, :])
        pw_ref[...] = p
        ps = p + mu_ref[...] * (_shift_down(p, last_scr[7:8, :]) - p)
        last_scr[...] = p[TOK_TILE - 8:, :]
        decay, k, kap, a = _prep_fn(ps[:, W:2 * W], ps[:, 4 * W:], w0_ref[...], a0_ref[...], kk_ref[...], ka_ref[...],
                                    wup_ref[...], aup_ref[...], ones_ref[...])
        r_ref[...] = ps[:, 0:W]
        w_ref[...] = decay
        k_ref[...] = k
        v_ref[...] = ps[:, 2 * W:3 * W]
        kap_ref[...] = kap
        a_ref[...] = a
        grw_ref[...] = ps[:, 3 * W:4 * W]

    vec = _full((1, W))
    return pl.pallas_call(
        body, name="inproj_prep", grid=(T // TOK_TILE,),
        in_specs=[_rows_of_one(TOK_TILE, D_MODEL), _full((1, D_MODEL)), _full((IN_COLS, D_MODEL)), _full((1, RWKV_COLS)),
                  vec, vec, vec, vec, _full((W, 2 * LORA)), _full((W, 2 * LORA)), _full((256, 128))],
        out_specs=[_rows(TOK_TILE, RET_COLS), _rows(TOK_TILE, RWKV_COLS)] + [_rows(TOK_TILE, W)] * 7,
        out_shape=[jax.ShapeDtypeStruct((T, RET_COLS), F32), jax.ShapeDtypeStruct((T, RWKV_COLS), F32)]
        + [jax.ShapeDtypeStruct((T, W), F32)] * 7,
        scratch_shapes=[pltpu.VMEM((8, RWKV_COLS), F32)],
        compiler_params=_params(dimension_semantics=("arbitrary",)),
    )(x, norm_g, w_in_t, mu, w0, a0, k_k, k_a, wup_pad, aup_pad, ones64)


def _prep_bwd(p_rwkv, mu, w0, a0, k_k, k_a, wup_pad, aup_pad, ones64, dr, dw, dk, dv, dkap, da, dg, dr2, dk2, dv2):
    T = p_rwkv.shape[0]
    nt = T // TOK_TILE

    def body(p_ref, prev8_ref, mu_ref, w0_ref, a0_ref, kk_ref, ka_ref, wup_ref, aup_ref, ones_ref,
             dr_ref, dw_ref, dk_ref, dv_ref, dkap_ref, da_ref, dg_ref, dr2_ref, dk2_ref, dv2_ref,
             dp_ref, dmu_ref, dw0_ref, da0_ref, dkk_ref, dka_ref, dwup_ref, daup_ref, zrow_scr):
        i = pl.program_id(0)
        accs = (dmu_ref, dw0_ref, da0_ref, dkk_ref, dka_ref, dwup_ref, daup_ref)

        @pl.when(i == 0)
        def _():
            zrow_scr[...] = jnp.zeros_like(zrow_scr)
            for ref in accs:
                ref[...] = jnp.zeros_like(ref)

        p, prev, ps = _mixed(p_ref, prev8_ref, mu_ref, i == nt - 1)
        ones = ones_ref[...]
        _, vjp = jax.vjp(lambda *args: _prep_fn(*args, ones), ps[:, W:2 * W], ps[:, 4 * W:], w0_ref[...], a0_ref[...],
                         kk_ref[...], ka_ref[...], wup_ref[...], aup_ref[...])
        dkr, dxwa, dw0, da0, dkk, dka, dwup, daup = vjp(
            (dw_ref[...], dk_ref[...] + dk2_ref[...], dkap_ref[...], da_ref[...]))
        dps = jnp.concatenate([dr_ref[...] + dr2_ref[...], dkr, dv_ref[...] + dv2_ref[...], dg_ref[...], dxwa], axis=1)
        z = dps * mu_ref[...]
        dp_ref[...] = dps - z + _shift_up(z, zrow_scr[0:1, :])
        zrow_scr[0:1, :] = z[0:1, :]
        for ref, val in zip(accs, (_colsum(dps * (prev - p)), dw0, da0, dkk, dka, dwup, daup)):
            ref[...] += val

    rev = lambda i: (nt - 1 - i, 0)
    vec = _full((1, W))
    lora = _full((W, 2 * LORA))
    tile = pl.BlockSpec((TOK_TILE, W), rev)
    prev8 = pl.BlockSpec((8, RWKV_COLS), lambda i: (jnp.maximum((nt - 1 - i) * (TOK_TILE // 8) - 1, 0), 0))
    return pl.pallas_call(
        body, name="prep_bwd", grid=(nt,),
        in_specs=[pl.BlockSpec((TOK_TILE, RWKV_COLS), rev), prev8, _full((1, RWKV_COLS)), vec, vec, vec, vec, lora, lora,
                  _full((256, 128))] + [tile] * 10,
        out_specs=[pl.BlockSpec((TOK_TILE, RWKV_COLS), rev), _full((1, RWKV_COLS)), vec, vec, vec, vec, lora, lora],
        out_shape=[jax.ShapeDtypeStruct((T, RWKV_COLS), F32), jax.ShapeDtypeStruct((1, RWKV_COLS), F32)]
        + [jax.ShapeDtypeStruct((1, W), F32)] * 4 + [jax.ShapeDtypeStruct((W, 2 * LORA), F32)] * 2,
        scratch_shapes=[pltpu.VMEM((8, RWKV_COLS), F32)],
        compiler_params=_params(dimension_semantics=("arbitrary",)),
    )(p_rwkv, p_rwkv, mu, w0, a0, k_k, k_a, wup_pad, aup_pad, ones64, dr, dw, dk, dv, dkap, da, dg, dr2, dk2, dv2)


def _silu(x):
    return x * jax.nn.sigmoid(x)


def _post_y(o, r, k, v, g_rw, ret_raw, g_ret, ret_gn_g, gn_g, gn_b, r_k, avg128, avg64, ones64):
    xc = ret_raw - _head_mix(ret_raw, avg128)
    ret = xc * lax.rsqrt(_head_mix(xc * xc, avg128) + RET_GN_EPS)
    y_ret = _silu(g_ret) * (ret * ret_gn_g)
    oc = o - _head_mix(o, avg64)
    on = oc * lax.rsqrt(_head_mix(oc * oc, avg64) + RWKV_GN_EPS) * gn_g + gn_b
    bonus = _head_mix(r * k * r_k, ones64) * v
    y_rwkv = _silu(g_rw) * (on + bonus)
    return y_ret, y_rwkv


def _post_loss(h, final_g, target):
    err = _rmsnorm(h, final_g) - target
    return 0.5 * jnp.sum(jnp.mean(err * err, axis=-1))


def _post(o, r, k, v, g_rw, ret_raw, p_ret, x, target, ret_gn_g, gn_g, gn_b, r_k, final_g, w_out, avg128, avg64, ones64):
    T = x.shape[1]
    n_tok_out = 8

    def body(o_ref, r_ref, k_ref, v_ref, grw_ref, ret_ref, gret_ref, x_ref, tgt_ref, rg_ref, gg_ref, gb_ref, rk_ref, fg_ref,
             wo_ref, a128_ref, a64_ref, ones_ref, *outs):
        tok_outs, (dwo_ref, drg_ref, dgg_ref, dgb_ref, drk_ref, dfg_ref, loss_ref) = outs[:n_tok_out], outs[n_tok_out:]
        accs = (dwo_ref, drg_ref, dgg_ref, dgb_ref, drk_ref, dfg_ref, loss_ref)

        @pl.when(pl.program_id(0) == 0)
        def _():
            for ref in accs:
                ref[...] = jnp.zeros_like(ref)

        consts = (a128_ref[...], a64_ref[...], ones_ref[...])
        (y_ret, y_rwkv), vjp = jax.vjp(
            lambda *args: _post_y(*args, *consts), o_ref[...], r_ref[...], k_ref[...], v_ref[...], grw_ref[...], ret_ref[...],
            gret_ref[...], rg_ref[...], gg_ref[...], gb_ref[...], rk_ref[...])
        h = x_ref[...] + _dot_bf(y_ret, wo_ref[0:RET_WIDTH, :]) + _dot_bf(y_rwkv, wo_ref[RET_WIDTH:, :])
        loss, (dh, dfg) = jax.value_and_grad(_post_loss, argnums=(0, 1))(h, fg_ref[...], tgt_ref[...])
        dy_ret = _dot_nt_bf(dh, wo_ref[0:RET_WIDTH, :])
        dy_rwkv = _dot_nt_bf(dh, wo_ref[RET_WIDTH:, :])
        do, dr, dk, dv, dgrw, dret, dgret, drg, dgg, dgb, drk = vjp((dy_ret, dy_rwkv))
        for ref, val in zip(tok_outs, (dh, do, dr, dk, dv, dgrw, dret, dgret)):
            ref[...] = val
        dwo_ref[0:RET_WIDTH, :] += _dot_tn_bf(y_ret, dh)
        dwo_ref[RET_WIDTH:, :] += _dot_tn_bf(y_rwkv, dh)
        for ref, val in zip(accs[1:], (drg, dgg, dgb, drk, dfg, jnp.full((1, 128), loss, F32))):
            ref[...] += val

    tile = _rows(TOK_TILE, W)
    wide = _rows(TOK_TILE, D_MODEL)
    wide_of_one = _rows_of_one(TOK_TILE, D_MODEL)
    vec = _full((1, W))
    sq = _full((256, 128))
    return pl.pallas_call(
        body, name="post", grid=(T // TOK_TILE,),
        in_specs=[tile] * 6 + [pl.BlockSpec((TOK_TILE, W), lambda i: (i, 2)), wide_of_one, wide_of_one, vec, vec, vec, vec,
                               _full((1, D_MODEL)), _full((D_MODEL, D_MODEL)), sq, sq, sq],
        out_specs=[wide] + [tile] * 7 + [_full((D_MODEL, D_MODEL)), vec, vec, vec, vec, _full((1, D_MODEL)), _full((1, 128))],
        out_shape=[jax.ShapeDtypeStruct((T, D_MODEL), F32)] + [jax.ShapeDtypeStruct((T, W), F32)] * 7
        + [jax.ShapeDtypeStruct((D_MODEL, D_MODEL), F32)] + [jax.ShapeDtypeStruct((1, W), F32)] * 4
        + [jax.ShapeDtypeStruct((1, D_MODEL), F32), jax.ShapeDtypeStruct((1, 128), F32)],
        compiler_params=_params(dimension_semantics=("arbitrary",)),
    )(o, r, k, v, g_rw, ret_raw, p_ret, x, target, ret_gn_g, gn_g, gn_b, r_k, final_g, w_out, avg128, avg64, ones64)


def _inproj_bwd(x, norm_g, dp_qkv, dg_ret, dp_rwkv, dh, w_in_t):
    T = x.shape[1]
    widths = [dp.shape[1] for dp in (dp_qkv, dg_ret, dp_rwkv)]
    steps = T // TOK_TILE

    def body(x_ref, g_ref, dqkv_ref, dgret_ref, drwkv_ref, dh_ref, w_ref, dx_ref, dg_ref, mine_ref, sib_ref, acc_ref):
        @pl.when(pl.program_id(0) == 0)
        def _():
            dg_ref[...] = jnp.zeros_like(dg_ref)
            acc_ref[...] = jnp.zeros_like(acc_ref)

        u, vjp = jax.vjp(_rmsnorm, x_ref[...], g_ref[...])
        ub = u.astype(BF16)
        du, row = 0.0, 0
        for dp_ref, n in zip((dqkv_ref, dgret_ref, drwkv_ref), widths):
            dp = dp_ref[...].astype(BF16)
            du = du + _dot_bf(dp, w_ref[row:row + n, :])
            acc_ref[row:row + n, :] += _dot_tn_bf(dp, ub)
            row += n
        dx, dg = vjp(du)
        dx_ref[...] = dx + dh_ref[...]
        dg_ref[...] += dg

        @pl.when(pl.program_id(0) == steps - 1)
        def _():
            core = lax.axis_index("c")
            for dev in range(N_DEV):
                block = acc_ref[dev * SHARD_IN:(dev + 1) * SHARD_IN, :].astype(BF16)

                @pl.when(core == dev % 2)
                def _():
                    mine_ref[dev // 2] = block

                @pl.when(core != dev % 2)
                def _():
                    sib_ref[dev // 2] = block

    half = jax.ShapeDtypeStruct((N_DEV // 2, SHARD_IN, D_MODEL), BF16)
    once = lambda shape: pl.BlockSpec(shape, lambda i: (0,) * len(shape), pipeline_mode=pl.Buffered(1))
    return pl.pallas_call(
        body, name="inproj_bwd", grid=(steps,),
        in_specs=[_rows_of_one(TOK_TILE, D_MODEL), _full((1, D_MODEL))] + [_rows(TOK_TILE, n) for n in widths]
        + [_rows(TOK_TILE, D_MODEL), once((IN_COLS, D_MODEL))],
        out_specs=[_rows_of_one(TOK_TILE, D_MODEL), _full((1, D_MODEL)), once(half.shape), once(half.shape)],
        out_shape=[jax.ShapeDtypeStruct((1, T, D_MODEL), F32), jax.ShapeDtypeStruct((1, D_MODEL), F32), half, half],
        scratch_shapes=[pltpu.VMEM((IN_COLS, D_MODEL), F32)],
        compiler_params=_params(dimension_semantics=("arbitrary",)),
    )(x, norm_g, dp_qkv, dg_ret, dp_rwkv, dh, w_in_t)


def _pad_lora(w_up_t, first):
    z = jnp.zeros_like(w_up_t)
    return jnp.concatenate([w_up_t, z] if first else [z, w_up_t], axis=1)


def _local_grads(x, target, norm_g, w_in_t, ret_gn_g, mu, w_lora_up_t, w0, a_lora_up_t, a0, k_k, k_a, r_k, gn_g, gn_b,
                 w_out_bf, final_g):
    T = x.shape[1]
    tabs = _rope_tables(T) + _ret_tables()
    ones64 = _block_mix(128, RWKV_HEAD)
    avg64 = _block_mix(128, RWKV_HEAD, 1.0 / RWKV_HEAD)
    avg128 = _block_mix(128, RET_DV, 1.0 / RET_DV)
    wup_pad, aup_pad = _pad_lora(w_lora_up_t, True), _pad_lora(a_lora_up_t, False)

    p_ret, p_rwkv, r, w, k, v, kap, a, g_rw = _inproj_prep(x, norm_g, w_in_t, mu, w0, a0, k_k, k_a, wup_pad, aup_pad, ones64)
    ret_raw, s_saved = _ret_fwd(p_ret, tabs)
    o, s_all, sa_rows = _wkv_fwd(r, w, k, v, kap, a)
    (dh, do, dr2, dk2, dv2, dgrw, dret, dgret, d_w_out, d_ret_gn_g, d_gn_g, d_gn_b, d_r_k, d_final_g, loss) = _post(
        o, r, k, v, g_rw, ret_raw, p_ret, x, target, ret_gn_g, gn_g, gn_b, r_k, final_g, w_out_bf, avg128, avg64, ones64)
    dr, dw, dk, dv, dkap, da = _wkv_bwd(r, w, k, v, kap, a, s_all, sa_rows, do)
    dp_rwkv, d_mu, d_w0, d_a0, d_k_k, d_k_a, d_wup, d_aup = _prep_bwd(
        p_rwkv, mu, w0, a0, k_k, k_a, wup_pad, aup_pad, ones64, dr, dw, dk, dv, dkap, da, dgrw, dr2, dk2, dv2)
    dp_qkv = _ret_bwd(p_ret, s_saved, dret, tabs)
    dx, d_norm_g, *d_w_in = _inproj_bwd(x, norm_g, dp_qkv, dgret, dp_rwkv, dh, w_in_t)
    grads = dict(norm_g=d_norm_g, w_in=d_w_in, ret_gn_g=d_ret_gn_g, rwkv_mu=d_mu, w_lora_up=d_wup[:, :LORA], w0=d_w0,
                 a_lora_up=d_aup[:, LORA:], a0=d_a0, k_k=d_k_k, k_a=d_k_a, r_k=d_r_k, rwkv_gn_g=d_gn_g, rwkv_gn_b=d_gn_b,
                 w_out=d_w_out, final_norm_g=d_final_g)
    return loss, dx, grads


def _mesh_pos():
    return lax.axis_index("x"), lax.axis_index("y"), lax.axis_index("c")


def _all_gather(shards):
    n = len(shards)

    def body(*refs):
        x_refs, out_refs = refs[:n], refs[n:2 * n]
        send_sems, recv_sems, local_sems = refs[2 * n:]
        x, y, c = _mesh_pos()
        me, sibling = (x, y, c), (x, y, 1 - c)
        chips = [(1 - x, y), (x, 1 - y), (1 - x, 1 - y)]

        def rows(a, pos):
            m = x_refs[a].shape[0]
            return out_refs[a].at[pl.ds((4 * pos[0] + 2 * pos[1] + pos[2]) * m, m), :]

        def copy(a, k, block, to, src=None):
            return pltpu.make_async_remote_copy(
                src_ref=rows(a, block) if src is None else src, dst_ref=rows(a, block),
                send_sem=send_sems.at[a, k], recv_sem=recv_sems.at[a, k], device_id=to, device_id_type=MESH)

        mine = [pltpu.make_async_copy(x_refs[a], rows(a, me), local_sems.at[a]) for a in range(n)]
        for cp in mine:
            cp.start()
        first = []
        for a in range(n):
            first.append(copy(a, 0, me, sibling, src=x_refs[a]))
            first += [copy(a, 1 + j, me, (*chip, c), src=x_refs[a]) for j, chip in enumerate(chips)]
        for cp in first:
            cp.start()
        passed = []
        for j, chip in enumerate(chips):
            for a in range(n):
                copy(a, 1 + j, (*chip, c), me).wait_recv()
                passed.append(copy(a, 4 + j, (*chip, c), sibling))
                passed[-1].start()
        for a in range(n):
            copy(a, 0, sibling, me).wait_recv()
            for j, chip in enumerate(chips):
                copy(a, 4 + j, (*chip, 1 - c), me).wait_recv()
        for cp in first + passed:
            cp.wait_send()
        for cp in mine:
            cp.wait()

    vmem = pl.BlockSpec(memory_space=pltpu.VMEM)
    return pl.pallas_call(
        body, name="gather_weights",
        out_shape=[jax.ShapeDtypeStruct((N_DEV * s.shape[0], s.shape[1]), s.dtype) for s in shards],
        in_specs=[vmem] * n, out_specs=[vmem] * n,
        scratch_shapes=[pltpu.SemaphoreType.DMA((n, 7)), pltpu.SemaphoreType.DMA((n, 7)), pltpu.SemaphoreType.DMA((n,))],
        compiler_params=_params(),
    )(*shards)


N_CHIP = 4


def _exchange_pairs(big, small):
    nb, ns = len(big), len(small)

    def body(*refs):
        big_in, small_in = refs[:nb], refs[nb:nb + ns]
        theirs, small_out = refs[nb + ns:2 * nb + ns], refs[2 * nb + ns:2 * nb + 2 * ns]
        pair_send, pair_recv, send_sems, recv_sems, local_sems = refs[2 * nb + 2 * ns:]
        x, y, c = _mesh_pos()
        me = 4 * x + 2 * y + c
        local = [pltpu.make_async_copy(small_in[a].at[me], small_out[a].at[me], local_sems.at[a]) for a in range(ns)]
        for cp in local:
            cp.start()
        copies = [pltpu.make_async_remote_copy(
            src_ref=big_in[a], dst_ref=theirs[a], send_sem=pair_send.at[a], recv_sem=pair_recv.at[a],
            device_id=(x, y, 1 - c), device_id_type=MESH) for a in range(nb)]
        for k in range(1, N_DEV):
            peer = (x ^ (k >> 2), y ^ ((k >> 1) & 1), c ^ (k & 1))
            peer_idx = 4 * peer[0] + 2 * peer[1] + peer[2]
            copies += [pltpu.make_async_remote_copy(
                src_ref=small_in[a].at[peer_idx], dst_ref=small_out[a].at[me], send_sem=send_sems.at[a, k - 1],
                recv_sem=recv_sems.at[a, k - 1], device_id=peer, device_id_type=MESH) for a in range(ns)]
        for cp in copies:
            cp.start()
        for cp in copies:
            cp.wait()
        for cp in local:
            cp.wait()

    hbm = pl.BlockSpec(memory_space=pl.ANY)
    out_shape = [jax.ShapeDtypeStruct(p.shape, p.dtype) for p in big + small]
    dma = pltpu.SemaphoreType.DMA
    res = pl.pallas_call(
        body, name="exchange_pairs", out_shape=out_shape,
        in_specs=[hbm] * (nb + ns), out_specs=[hbm] * len(out_shape),
        scratch_shapes=[dma((nb,)), dma((nb,)), dma((ns, 7)), dma((ns, 7)), dma((ns,))],
        compiler_params=_params(),
    )(*big, *small)
    return res[:nb], res[nb:]


def _pair_sum(name, mine, theirs, row_tile):
    _, rows, cols = mine.shape

    def body(a_ref, b_ref, o_ref):
        o_ref[...] = (a_ref[...].astype(F32) + b_ref[...].astype(F32)).astype(o_ref.dtype)

    spec = pl.BlockSpec((N_CHIP, row_tile, cols), lambda i: (0, i, 0))
    return pl.pallas_call(
        body, name=name, grid=(rows // row_tile,), in_specs=[spec, spec], out_specs=spec,
        out_shape=jax.ShapeDtypeStruct(mine.shape, mine.dtype),
        compiler_params=_params(dimension_semantics=("arbitrary",)),
    )(mine, theirs)


def _exchange_chips(parts):
    n = len(parts)

    def body(*refs):
        in_refs, out_refs = refs[:n], refs[n:2 * n]
        send_sems, recv_sems, local_sems = refs[2 * n:]
        x, y, c = _mesh_pos()
        my_chip = 2 * x + y
        own = [pltpu.make_async_copy(in_refs[a].at[my_chip], out_refs[a].at[my_chip], local_sems.at[a]) for a in range(n)]
        for cp in own:
            cp.start()
        copies = []
        for k in range(1, N_CHIP):
            px, py = x ^ (k >> 1), y ^ (k & 1)
            copies += [pltpu.make_async_remote_copy(
                src_ref=in_refs[a].at[2 * px + py], dst_ref=out_refs[a].at[my_chip], send_sem=send_sems.at[a, k - 1],
                recv_sem=recv_sems.at[a, k - 1], device_id=(px, py, c), device_id_type=MESH) for a in range(n)]
        for cp in copies:
            cp.start()
        for cp in copies:
            cp.wait()
        for cp in own:
            cp.wait()

    hbm = pl.BlockSpec(memory_space=pl.ANY)
    dma = pltpu.SemaphoreType.DMA
    return pl.pallas_call(
        body, name="exchange_chips",
        out_shape=[jax.ShapeDtypeStruct(p.shape, p.dtype) for p in parts],
        in_specs=[hbm] * n, out_specs=[hbm] * n,
        scratch_shapes=[dma((n, N_CHIP - 1)), dma((n, N_CHIP - 1)), dma((n,))],
        compiler_params=_params(),
    )(*parts)


def _adamw(w, g, m, v):
    m = ADAM_B1 * m + (1.0 - ADAM_B1) * g
    v = ADAM_B2 * v + (1.0 - ADAM_B2) * (g * g)
    m_hat = m / (1.0 - ADAM_B1 ** ADAM_STEP)
    v_hat = v / (1.0 - ADAM_B2 ** ADAM_STEP)
    return -ADAM_LR * (m_hat / (jnp.sqrt(v_hat) + ADAM_EPS) + ADAM_WD * w), m, v


def _sum_parts(name, parts, row_tile):
    n_parts, rows, cols = parts.shape

    def body(p_ref, g_ref):
        g = p_ref[0].astype(F32)
        for s in range(1, n_parts):
            g = g + p_ref[s].astype(F32)
        g_ref[...] = g

    return pl.pallas_call(
        body, name=name, grid=(rows // row_tile,),
        in_specs=[pl.BlockSpec((n_parts, row_tile, cols), lambda i: (0, i, 0))],
        out_specs=pl.BlockSpec((row_tile, cols), lambda i: (i, 0)),
        out_shape=jax.ShapeDtypeStruct((rows, cols), F32),
        compiler_params=_params(dimension_semantics=("arbitrary",)),
    )(parts)


def _adamw_apply(name, g, w, m, v, row_tile):
    _, rows, cols = w.shape

    def body(g_ref, w_ref, m_ref, v_ref, d_ref, nm_ref, nv_ref):
        d_ref[0], nm_ref[0], nv_ref[0] = _adamw(w_ref[0], g_ref[0], m_ref[0], v_ref[0])

    tile = pl.BlockSpec((1, row_tile, cols), lambda i: (0, i, 0))
    return pl.pallas_call(
        body, name=name, grid=(rows // row_tile,), in_specs=[tile] * 4, out_specs=[tile] * 3,
        out_shape=[jax.ShapeDtypeStruct((1, rows, cols), F32)] * 3,
        compiler_params=_params(dimension_semantics=("arbitrary",)),
    )(g, w, m, v)


def _reduce_adamw_2d(name, parts, w, m, v, row_tile):
    n_parts, rows, cols = parts.shape

    def body(p_ref, w_ref, m_ref, v_ref, g_ref, d_ref, nm_ref, nv_ref):
        g = p_ref[0].astype(F32)
        for s in range(1, n_parts):
            g = g + p_ref[s].astype(F32)
        g_ref[...] = g
        d_ref[...], nm_ref[...], nv_ref[...] = _adamw(w_ref[...], g, m_ref[...], v_ref[...])

    tile = pl.BlockSpec((row_tile, cols), lambda i: (i, 0))
    return pl.pallas_call(
        body, name=name, grid=(rows // row_tile,),
        in_specs=[pl.BlockSpec((n_parts, row_tile, cols), lambda i: (0, i, 0)), tile, tile, tile],
        out_specs=[tile] * 4,
        out_shape=[jax.ShapeDtypeStruct((rows, cols), F32)] * 4,
        compiler_params=_params(dimension_semantics=("arbitrary",)),
    )(parts, w, m, v)


def _reduce_adamw_t(name, parts_t, w, m, v, sum_tile, row_tile):
    g = _sum_parts(name + "_sum", parts_t, sum_tile).T[None]
    return [g] + list(_adamw_apply(name, g, w, m, v, row_tile))


def _reduce_adamw(name, parts, w, m, v, row_tile):
    n_parts, rows, cols = parts.shape

    def body(p_ref, w_ref, m_ref, v_ref, g_ref, d_ref, nm_ref, nv_ref):
        g = p_ref[0].astype(F32)
        for s in range(1, n_parts):
            g = g + p_ref[s].astype(F32)
        g_ref[0] = g
        d_ref[0], nm_ref[0], nv_ref[0] = _adamw(w_ref[0], g, m_ref[0], v_ref[0])

    tile = pl.BlockSpec((1, row_tile, cols), lambda i: (0, i, 0))
    return pl.pallas_call(
        body, name=name, grid=(rows // row_tile,),
        in_specs=[pl.BlockSpec((n_parts, row_tile, cols), lambda i: (0, i, 0)), tile, tile, tile],
        out_specs=[tile] * 4,
        out_shape=[jax.ShapeDtypeStruct((1, rows, cols), F32)] * 4,
        compiler_params=_params(dimension_semantics=("arbitrary",)),
    )(parts, w, m, v)


_SMALL = (("norm_g", 1024), ("ret_gn_g", 512), ("rwkv_mu", 2176), ("w0", 512), ("a0", 512), ("k_k", 512), ("k_a", 512),
          ("r_k", 512), ("rwkv_gn_g", 512), ("rwkv_gn_b", 512), ("final_norm_g", 1024))
_SMALL_LANES = sum(n for _, n in _SMALL) + 128
_WEIGHTS = ("norm_g", "w_in", "ret_gn_g", "rwkv_mu", "w_lora_up", "w0", "a_lora_up", "a0", "k_k", "k_a", "r_k", "rwkv_gn_g",
            "rwkv_gn_b", "w_out", "final_norm_g")


def _adamw_vectors(parts, wts, mom, var):
    k = len(_SMALL)

    def body(p_ref, *refs):
        w_refs, m_refs, v_refs, outs = refs[:k], refs[k:2 * k], refs[2 * k:3 * k], refs[3 * k:]
        g_all = p_ref[0]
        for s in range(1, N_DEV):
            g_all = g_all + p_ref[s]
        off = 0
        for i, (name, n) in enumerate(_SMALL):
            g = g_all[:, off:off + n]
            off += n
            if name == "r_k":
                g = jnp.concatenate([g[:, RWKV_HEAD * h:RWKV_HEAD * (h + 1)] for h in range(RWKV_HEADS)], axis=0)[None]
            outs[4 * i][...] = g
            outs[4 * i + 1][...], outs[4 * i + 2][...], outs[4 * i + 3][...] = _adamw(
                w_refs[i][...], g, m_refs[i][...], v_refs[i][...])
        outs[4 * k][...] = g_all[:, off:off + 128]

    vmem = pl.BlockSpec(memory_space=pltpu.VMEM)
    shapes = [jax.ShapeDtypeStruct(wts[n].shape, F32) for n, _ in _SMALL for _ in range(4)] + [jax.ShapeDtypeStruct((1, 128), F32)]
    res = pl.pallas_call(
        body, name="adamw_vectors", out_shape=shapes,
        in_specs=[vmem] * (1 + 3 * k), out_specs=[vmem] * len(shapes), compiler_params=_params(),
    )(parts, *[wts[n] for n, _ in _SMALL], *[mom[n] for n, _ in _SMALL], *[var[n] for n, _ in _SMALL])
    return {n: res[4 * i:4 * i + 4] for i, (n, _) in enumerate(_SMALL)}, res[4 * k]


def kernel(x, norm_g, w_in, ret_gn_g, rwkv_mu, w_lora_up, w0, a_lora_up, a0, k_k, k_a, r_k, rwkv_gn_g, rwkv_gn_b, w_out, final_norm_g, loss_target, m_norm_g, m_w_in, m_ret_gn_g, m_rwkv_mu, m_w_lora_up, m_w0, m_a_lora_up, m_a0, m_k_k, m_k_a, m_r_k, m_rwkv_gn_g, m_rwkv_gn_b, m_w_out, m_final_norm_g, v_norm_g, v_w_in, v_ret_gn_g, v_rwkv_mu, v_w_lora_up, v_w0, v_a_lora_up, v_a0, v_k_k, v_k_a, v_r_k, v_rwkv_gn_g, v_rwkv_gn_b, v_w_out, v_final_norm_g):
    wts = dict(norm_g=norm_g, w_in=w_in, ret_gn_g=ret_gn_g, rwkv_mu=rwkv_mu, w_lora_up=w_lora_up, w0=w0, a_lora_up=a_lora_up,
               a0=a0, k_k=k_k, k_a=k_a, r_k=r_k, rwkv_gn_g=rwkv_gn_g, rwkv_gn_b=rwkv_gn_b, w_out=w_out,
               final_norm_g=final_norm_g)
    mom = dict(norm_g=m_norm_g, w_in=m_w_in, ret_gn_g=m_ret_gn_g, rwkv_mu=m_rwkv_mu, w_lora_up=m_w_lora_up, w0=m_w0,
               a_lora_up=m_a_lora_up, a0=m_a0, k_k=m_k_k, k_a=m_k_a, r_k=m_r_k, rwkv_gn_g=m_rwkv_gn_g,
               rwkv_gn_b=m_rwkv_gn_b, w_out=m_w_out, final_norm_g=m_final_norm_g)
    var = dict(norm_g=v_norm_g, w_in=v_w_in, ret_gn_g=v_ret_gn_g, rwkv_mu=v_rwkv_mu, w_lora_up=v_w_lora_up, w0=v_w0,
               a_lora_up=v_a_lora_up, a0=v_a0, k_k=v_k_k, k_a=v_k_a, r_k=v_r_k, rwkv_gn_g=v_rwkv_gn_g,
               rwkv_gn_b=v_rwkv_gn_b, w_out=v_w_out, final_norm_g=v_final_norm_g)
    shapes = {n: wts[n].shape for n in _WEIGHTS}

    w_in_t, w_out_bf, wup_t, aup_t = _all_gather(
        [w_in[0].T.astype(BF16), w_out[0].astype(BF16), w_lora_up[0].T, a_lora_up[0].T])

    loss, dx, g = _local_grads(
        x, loss_target, norm_g, w_in_t, ret_gn_g, rwkv_mu, wup_t, w0, aup_t, a0, k_k, k_a,
        r_k.reshape(1, W), rwkv_gn_g, rwkv_gn_b, w_out_bf, final_norm_g.reshape(1, D_MODEL))

    small = jnp.concatenate([g[n] for n, _ in _SMALL] + [loss], axis=1)
    core = lax.axis_index("c")
    by_core = lambda t: [lax.dynamic_index_in_dim(t, i, axis=1, keepdims=False) for i in (core, 1 - core)]
    in_mine, in_sib = g["w_in"]
    out_mine, out_sib = by_core(g["w_out"].reshape(N_CHIP, 2, SHARD_OUT, D_MODEL).astype(BF16))
    (in_theirs, out_theirs), parts = _exchange_pairs(
        [in_sib, out_sib],
        [g["w_lora_up"].reshape(N_DEV, SHARD_LORA, LORA), g["a_lora_up"].reshape(N_DEV, SHARD_LORA, LORA),
         jnp.broadcast_to(small[None], (N_DEV, 1, _SMALL_LANES))])
    by_chip = _exchange_chips([_pair_sum("pair_sum_w_in", in_mine, in_theirs, SHARD_IN // 2),
                               _pair_sum("pair_sum_w_out", out_mine, out_theirs, SHARD_OUT)])
    res = {}
    res["w_in"] = [t.T[None] for t in _reduce_adamw_2d(
        "adamw_w_in", by_chip[0], w_in[0].T, m_w_in[0].T, v_w_in[0].T, SHARD_IN // 2)]
    res["w_out"] = _reduce_adamw("adamw_w_out", by_chip[1], w_out, m_w_out, v_w_out, SHARD_OUT)
    res["w_lora_up"] = _reduce_adamw_t("adamw_w_lora_up", parts[0], w_lora_up, m_w_lora_up, v_w_lora_up, LORA, LORA)
    res["a_lora_up"] = _reduce_adamw_t("adamw_a_lora_up", parts[1], a_lora_up, m_a_lora_up, v_a_lora_up, LORA, LORA)
    as_row = lambda d: {n: d[n] if d[n].ndim > 1 else d[n].reshape(1, size) for n, size in _SMALL}
    vec, loss_row = _adamw_vectors(parts[2], as_row(wts), as_row(mom), as_row(var))
    res.update(vec)
    res = {n: [t.reshape(shapes[n]) for t in res[n]] for n in _WEIGHTS}
    return (loss_row[0, 0], dx, *[res[n][0] for n in _WEIGHTS], *[res[n][1] for n in _WEIGHTS],
            *[res[n][2] for n in _WEIGHTS], *[res[n][3] for n in _WEIGHTS])
```

```python
import numpy as np
import jax
import jax.numpy as jnp
from jax import lax
from jax.experimental import pallas as pl
from jax.experimental.pallas import tpu as pltpu

F32 = jnp.float32
BF16 = jnp.bfloat16

D_MODEL = 1024
CHUNK = 64
RET_HEADS = 4
RET_DV = 128
RET_DK = 64
RET_QK = 256
RET_WIDTH = 512
RWKV_WIDTH = 512
RWKV_HEAD = 64
RWKV_HEADS = 8
LORA = 64
RET_COLS = 2 * RET_QK + 2 * RET_WIDTH
RWKV_COLS = 4 * RWKV_WIDTH + 2 * LORA
IN_COLS = RET_COLS + RWKV_COLS
ROPE_BASE = 10000.0
RMS_EPS = 1e-6
RET_GN_EPS = 1e-5
RWKV_GN_EPS = 64e-5
ADAM_LR = 0.001
ADAM_B1 = 0.9
ADAM_B2 = 0.999
ADAM_EPS = 1e-08
ADAM_WD = 0.01
ADAM_STEP = 10
N_DEV = 8
SHARD_IN = IN_COLS // N_DEV
SHARD_OUT = D_MODEL // N_DEV
SHARD_LORA = RWKV_WIDTH // N_DEV
VMEM_LIMIT = 56 * 1024 * 1024
TOK_TILE = 256
WKV_CHUNK = 64

MESH = pl.DeviceIdType.MESH


def _dot_bf(a, b):
    return jnp.dot(a.astype(BF16), b.astype(BF16), preferred_element_type=F32)


def _dot_nt_bf(a, b):
    return lax.dot_general(a.astype(BF16), b.astype(BF16), (((1,), (1,)), ((), ())), preferred_element_type=F32)


def _dot_tn_bf(a, b):
    return lax.dot_general(a.astype(BF16), b.astype(BF16), (((0,), (0,)), ((), ())), preferred_element_type=F32)


@jax.custom_vjp
def _mm(a, b):
    return _dot_bf(a, b)


@jax.custom_vjp
def _mm_nt(a, b):
    return _dot_nt_bf(a, b)


@jax.custom_vjp
def _mm_tn(a, b):
    return _dot_tn_bf(a, b)


_mm.defvjp(lambda a, b: (_dot_bf(a, b), (a, b)), lambda res, g: (_dot_nt_bf(g, res[1]), _dot_tn_bf(res[0], g)))
_mm_nt.defvjp(lambda a, b: (_dot_nt_bf(a, b), (a, b)), lambda res, g: (_dot_bf(g, res[1]), _dot_tn_bf(g, res[0])))
_mm_tn.defvjp(lambda a, b: (_dot_tn_bf(a, b), (a, b)), lambda res, g: (_dot_nt_bf(res[1], g), _dot_bf(res[0], g)))


def _trunc(x):
    return lax.bitcast_convert_type(lax.bitcast_convert_type(x, jnp.uint32) & jnp.uint32(0xFFFF0000), F32)


def _two_piece(x):
    hi = _trunc(x)
    return jnp.concatenate([hi, x - hi], axis=1)


def _mix_raw(x, mat2):
    return _unstack(jnp.dot(_two_piece(_stack(x)), mat2, preferred_element_type=F32))


@jax.custom_vjp
def _head_mix(x, mat2):
    return _mix_raw(x, mat2)


_head_mix.defvjp(lambda x, mat2: (_mix_raw(x, mat2), mat2), lambda mat2, g: (_mix_raw(g, mat2), jnp.zeros_like(mat2)))


def _swap_halves(x):
    lane = lax.broadcasted_iota(jnp.int32, x.shape, 1)
    return jnp.where((lane & (RET_DK - 1)) < RET_DK // 2, pltpu.roll(x, RET_QK - RET_DK // 2, axis=1),
                     pltpu.roll(x, RET_DK // 2, axis=1))


@jax.custom_vjp
def _rot(x):
    return _swap_halves(x)


_rot.defvjp(lambda x: (_swap_halves(x), None), lambda _, g: (_swap_halves(g),))


def _params(**kw):
    return pltpu.CompilerParams(vmem_limit_bytes=VMEM_LIMIT, **kw)


def _full(shape):
    nd = len(shape)
    return pl.BlockSpec(shape, lambda i, _nd=nd: (0,) * _nd)


def _rows(tile, width):
    return pl.BlockSpec((tile, width), lambda i: (i, 0))


def _rows_of_one(tile, width):
    return pl.BlockSpec((None, tile, width), lambda i: (0, i, 0))


def _block_mix(n, blk, scale=1.0):
    idx = np.arange(n) // blk
    m = (idx[:, None] == idx[None, :]).astype(np.float32) * scale
    return jnp.asarray(np.concatenate([m, m], axis=0))


def _rope_tables(T):
    half = RET_DK // 2
    expo = -np.arange(half, dtype=np.float32) / np.float32(half)
    freqs = np.exp(expo * np.float32(np.log(ROPE_BASE))).astype(np.float32)
    ang = np.arange(T, dtype=np.float32)[:, None] * freqs[None, :]
    cos, sin = np.cos(ang).astype(np.float32), np.sin(ang).astype(np.float32)
    cos_h = np.concatenate([cos, cos], axis=1)
    sin_h = np.concatenate([-sin, sin], axis=1)
    cos_t = np.tile(cos_h, (1, RET_HEADS))
    sin_t = np.tile(sin_h, (1, RET_HEADS))
    return jnp.asarray(cos_t), jnp.asarray(sin_t)


def _ret_tables():
    h = np.arange(RET_HEADS, dtype=np.float32)
    lg = np.log(1.0 - np.exp2(-5.0 - h)).astype(np.float32)
    idx = np.arange(CHUNK, dtype=np.float32)
    intra = np.exp(lg[:, None, None] * np.abs(idx[:, None] - idx[None, :])).astype(np.float32)
    q_dec = np.exp(lg[:, None] * (idx[None, :] + 1.0)).astype(np.float32)
    k_dec = np.exp(lg[:, None] * (CHUNK - 1.0 - idx[None, :])).astype(np.float32)
    chunk_dec = np.exp(lg * CHUNK).astype(np.float32)
    lane_head = np.arange(RET_QK) // RET_DK
    mask = (lane_head[None, :] == np.arange(RET_HEADS)[:, None]).astype(np.float32)
    m = np.broadcast_to(mask[:, None, :], (RET_HEADS, CHUNK, RET_QK)).copy()
    qd = m * q_dec[:, :, None]
    kd = m * k_dec[:, :, None]
    return jnp.asarray(intra), jnp.asarray(m), jnp.asarray(qd), jnp.asarray(kd), [float(c) for c in chunk_dec]


def _rmsnorm(x, g):
    return x * lax.rsqrt(jnp.mean(x * x, axis=-1, keepdims=True) + RMS_EPS) * g


def _ret_chunk(pq, pk, v_heads, s_heads, cos_t, sin_t, dec, hm, qd, kd, chunk_dec):
    q = pq * cos_t + _rot(pq) * sin_t
    k = (pk * cos_t + _rot(pk) * sin_t) * (RET_DK ** -0.5)
    outs, s_out = [], []
    for h in range(RET_HEADS):
        sc = _mm_nt(q * hm[h], k * hm[h]) * dec[h]
        intra = _mm(sc, v_heads[h])
        kv = _mm_tn(k * kd[h], v_heads[h])
        inter = _mm(q * qd[h], s_heads[h])
        outs.append(intra + inter)
        s_out.append(s_heads[h] * chunk_dec[h] + kv)
    return tuple(outs), tuple(s_out)


def _ret_specs():
    const = [_full((RET_HEADS, CHUNK, CHUNK)), _full((RET_HEADS, CHUNK, RET_QK)),
             _full((RET_HEADS, CHUNK, RET_QK)), _full((RET_HEADS, CHUNK, RET_QK))]
    return const


RET_GROUP = 8


def _ret_fwd(p_ret, tabs):
    T = p_ret.shape[0]
    G = RET_GROUP
    ng = T // (CHUNK * G)
    cos_t, sin_t, dec, hm, qd, kd, chunk_dec = tabs

    def body(p_ref, cos_ref, sin_ref, dec_ref, hm_ref, qd_ref, kd_ref, out_ref, sin_save_ref, s_scr):
        @pl.when(pl.program_id(0) == 0)
        def _():
            s_scr[...] = jnp.zeros_like(s_scr)

        consts = (dec_ref[...], hm_ref[...], qd_ref[...], kd_ref[...])
        s_heads = tuple(s_scr[h] for h in range(RET_HEADS))
        for c in range(G):
            rows = pl.ds(c * CHUNK, CHUNK)
            for h in range(RET_HEADS):
                sin_save_ref[c, h] = s_heads[h]
            v_heads = tuple(p_ref[rows, 2 * RET_QK + RET_DV * h:2 * RET_QK + RET_DV * (h + 1)] for h in range(RET_HEADS))
            outs, s_heads = _ret_chunk(p_ref[rows, 0:RET_QK], p_ref[rows, RET_QK:2 * RET_QK], v_heads, s_heads,
                                       cos_ref[rows, :], sin_ref[rows, :], *consts, chunk_dec)
            for h in range(RET_HEADS):
                out_ref[rows, RET_DV * h:RET_DV * (h + 1)] = outs[h]
        for h in range(RET_HEADS):
            s_scr[h] = s_heads[h]

    tok = CHUNK * G
    return pl.pallas_call(
        body, name="ret_fwd", grid=(ng,),
        in_specs=[pl.BlockSpec((tok, RET_COLS), lambda i: (i, 0)), _rows(tok, RET_QK), _rows(tok, RET_QK)] + _ret_specs(),
        out_specs=[_rows(tok, RET_WIDTH), pl.BlockSpec((G, RET_HEADS, RET_QK, RET_DV), lambda i: (i, 0, 0, 0))],
        out_shape=[jax.ShapeDtypeStruct((T, RET_WIDTH), F32),
                   jax.ShapeDtypeStruct((T // CHUNK, RET_HEADS, RET_QK, RET_DV), F32)],
        scratch_shapes=[pltpu.VMEM((RET_HEADS, RET_QK, RET_DV), F32)],
        compiler_params=_params(dimension_semantics=("arbitrary",)),
    )(p_ret, cos_t, sin_t, dec, hm, qd, kd)


def _ret_bwd(p_ret, s_saved, d_ret, tabs):
    T = p_ret.shape[0]
    G = RET_GROUP
    ng = T // (CHUNK * G)
    cos_t, sin_t, dec, hm, qd, kd, chunk_dec = tabs

    def body(p_ref, s_ref, dret_ref, cos_ref, sin_ref, dec_ref, hm_ref, qd_ref, kd_ref, dp_ref, ds_scr):
        @pl.when(pl.program_id(0) == 0)
        def _():
            ds_scr[...] = jnp.zeros_like(ds_scr)

        consts = (dec_ref[...], hm_ref[...], qd_ref[...], kd_ref[...])
        d_s = tuple(ds_scr[h] for h in range(RET_HEADS))
        for c in reversed(range(G)):
            rows = pl.ds(c * CHUNK, CHUNK)
            v_heads = tuple(p_ref[rows, 2 * RET_QK + RET_DV * h:2 * RET_QK + RET_DV * (h + 1)] for h in range(RET_HEADS))
            s_heads = tuple(s_ref[c, h] for h in range(RET_HEADS))
            tables = (cos_ref[rows, :], sin_ref[rows, :]) + consts
            _, vjp = jax.vjp(lambda a, b, c_, d: _ret_chunk(a, b, c_, d, *tables, chunk_dec),
                             p_ref[rows, 0:RET_QK], p_ref[rows, RET_QK:2 * RET_QK], v_heads, s_heads)
            d_out = tuple(dret_ref[rows, RET_DV * h:RET_DV * (h + 1)] for h in range(RET_HEADS))
            dq, dk, dv, d_s = vjp((d_out, d_s))
            dp_ref[rows, 0:RET_QK] = dq
            dp_ref[rows, RET_QK:2 * RET_QK] = dk
            for h in range(RET_HEADS):
                dp_ref[rows, 2 * RET_QK + RET_DV * h:2 * RET_QK + RET_DV * (h + 1)] = dv[h]
        for h in range(RET_HEADS):
            ds_scr[h] = d_s[h]

    tok = CHUNK * G
    rev = lambda i: (ng - 1 - i, 0)
    return pl.pallas_call(
        body, name="ret_bwd", grid=(ng,),
        in_specs=[pl.BlockSpec((tok, RET_COLS), rev),
                  pl.BlockSpec((G, RET_HEADS, RET_QK, RET_DV), lambda i: (ng - 1 - i, 0, 0, 0)),
                  pl.BlockSpec((tok, RET_WIDTH), rev), pl.BlockSpec((tok, RET_QK), rev), pl.BlockSpec((tok, RET_QK), rev)]
        + _ret_specs(),
        out_specs=pl.BlockSpec((tok, 2 * RET_QK + RET_WIDTH), rev),
        out_shape=jax.ShapeDtypeStruct((T, 2 * RET_QK + RET_WIDTH), F32),
        scratch_shapes=[pltpu.VMEM((RET_HEADS, RET_QK, RET_DV), F32)],
        compiler_params=_params(dimension_semantics=("arbitrary",)),
    )(p_ret, s_saved, d_ret, cos_t, sin_t, dec, hm, qd, kd)


def _block_ones():
    r = lax.broadcasted_iota(jnp.int32, (3 * 128, 128), 0)
    c = lax.broadcasted_iota(jnp.int32, (3 * 128, 128), 1)
    return (((r & 127) >> 6) == (c >> 6)).astype(BF16)


def _stack(x):
    return jnp.concatenate([x[:, 128 * p:128 * (p + 1)] for p in range(4)], axis=0)


def _unstack(y):
    n = y.shape[0] // 4
    return jnp.concatenate([y[n * p:n * (p + 1)] for p in range(4)], axis=1)


def _split(x, n):
    pieces = []
    for _ in range(n):
        p = x.astype(BF16)
        pieces.append(p)
        x = x - p.astype(F32)
    return pieces


def _lane_sum(x, ones):
    return _unstack(jnp.dot(_stack(x).astype(BF16), ones[:128], preferred_element_type=F32))


def _colsum(x):
    return jnp.sum(x, axis=0, keepdims=True)


def _rows_times(vecs, mat):
    n = vecs.shape[0]
    lane = lax.broadcasted_iota(jnp.int32, (n, 128), 1)
    lhs = jnp.concatenate([vecs[:, RWKV_HEAD * h:RWKV_HEAD * (h + 1)] for h in range(RWKV_HEADS)], axis=0)
    out = jnp.dot(lhs, mat, preferred_element_type=F32)
    tiles = [jnp.where(lane < RWKV_HEAD, out[2 * p * n:(2 * p + 1) * n, 128 * p:128 * (p + 1)],
                       out[(2 * p + 1) * n:(2 * p + 2) * n, 128 * p:128 * (p + 1)]) for p in range(4)]
    return jnp.concatenate(tiles, axis=1)


def _expand_cols(y, t):
    lane = lax.broadcasted_iota(jnp.int32, (RWKV_HEAD, 128), 1)
    idx = jnp.where(lane < RWKV_HEAD, t, RWKV_HEAD + t)
    return jnp.concatenate([jnp.take_along_axis(y[:, 128 * p:128 * (p + 1)], idx, axis=1) for p in range(4)], axis=1)


def _keep_step(acc_ref, x, t):
    lane = lax.broadcasted_iota(jnp.int32, (1, RWKV_WIDTH), 1)
    mask = jnp.broadcast_to((lane & (RWKV_HEAD - 1)) == t, x.shape)
    pltpu.store(acc_ref, x, mask=mask)


def _steps_to_rows(acc):
    assert WKV_CHUNK == RWKV_HEAD
    tiles = []
    for p in range(4):
        tt = acc[:, 128 * p:128 * (p + 1)].T
        tiles.append(jnp.concatenate([tt[:RWKV_HEAD], tt[RWKV_HEAD:]], axis=1))
    return jnp.concatenate(tiles, axis=1)


def _head_sums(x, ones):
    return _unstack(jnp.dot(jnp.concatenate(_split(_stack(x), 3), axis=1), ones, preferred_element_type=F32))


def _wkv_fwd(r, w, k, v, kap, a):
    T = r.shape[0]
    C = WKV_CHUNK
    nc = T // C

    def body(r_ref, w_ref, k_ref, v_ref, kap_ref, a_ref, o_ref, s_all_ref, sa_rows_ref, s_scr, o_acc, sa_acc):
        @pl.when(pl.program_id(0) == 0)
        def _():
            s_scr[...] = jnp.zeros_like(s_scr)

        ones = _block_ones()
        rr, ww, kk, vv, kap_, aa = (ref[...] for ref in (r_ref, w_ref, k_ref, v_ref, kap_ref, a_ref))
        bb = kap_ * aa
        c1 = _head_sums(pltpu.roll(bb, 1, axis=0) * kap_, ones)
        row = lambda x, t: x[t:t + 1]

        v_cols = _steps_to_rows(vv)

        s_prev = s_scr[...]
        sa = _lane_sum(s_prev * (-row(kap_, 0)), ones)
        ls = None

        def emit_o(t, s_t):
            _keep_step(o_acc, _lane_sum(s_t * row(rr, t), ones), t)

        for t in range(C):
            u = s_prev * row(ww, t) + _expand_cols(v_cols, t) * row(kk, t)
            if t > 0:
                sa = ls - sa * row(c1, t)
            if t + 1 < C:
                ls = _lane_sum(u * (-row(kap_, t + 1)), ones)
            if t > 0:
                emit_o(t - 1, s_prev)
            s_prev = u + sa * row(bb, t)
            s_all_ref[t] = s_prev
            _keep_step(sa_acc, sa, t)
        emit_o(C - 1, s_prev)
        s_scr[...] = s_prev
        o_ref[...] = _steps_to_rows(o_acc[...])
        sa_rows_ref[...] = _steps_to_rows(sa_acc[...])

    spec = _rows(C, RWKV_WIDTH)
    return pl.pallas_call(
        body, name="wkv_fwd", grid=(nc,),
        in_specs=[spec] * 6,
        out_specs=[spec, pl.BlockSpec((C, RWKV_HEAD, RWKV_WIDTH), lambda i: (i, 0, 0)), spec],
        out_shape=[jax.ShapeDtypeStruct((T, RWKV_WIDTH), F32), jax.ShapeDtypeStruct((T, RWKV_HEAD, RWKV_WIDTH), F32),
                   jax.ShapeDtypeStruct((T, RWKV_WIDTH), F32)],
        scratch_shapes=[pltpu.VMEM((RWKV_HEAD, RWKV_WIDTH), F32)] * 3,
        compiler_params=_params(dimension_semantics=("arbitrary",)),
    )(r, w, k, v, kap, a)


def _wkv_bwd(r, w, k, v, kap, a, s_all, sa_rows, d_o):
    T = r.shape[0]
    C = WKV_CHUNK
    nc = T // C

    def body(r_ref, w_ref, k_ref, v_ref, kap_ref, a_ref, sa_rows_ref, do_ref, s_ref, s_before_ref,
             dr_ref, dw_ref, dk_ref, dv_ref, dkap_ref, da_ref, ds_scr, dv_acc, dsa_acc):
        first_chunk = pl.program_id(0) == nc - 1

        @pl.when(pl.program_id(0) == 0)
        def _():
            ds_scr[...] = jnp.zeros_like(ds_scr)

        ones = _block_ones()
        rr, ww, kk, vv, kap_, aa, sar, dd = (ref[...] for ref in (r_ref, w_ref, k_ref, v_ref, kap_ref, a_ref, sa_rows_ref, do_ref))
        bb = kap_ * aa
        e1 = _head_sums(pltpu.roll(kap_, C - 1, axis=0) * bb, ones)
        row = lambda x, t: x[t:t + 1]

        def state_before(t):
            return s_ref[t - 1] if t > 0 else jnp.where(first_chunk, 0.0, s_before_ref[0])

        do_cols = _steps_to_rows(dd)

        d_sn, dsa, rows = None, None, [None] * C

        def emit_rows(t, d_sn_t, dsa_t):
            _keep_step(dv_acc, _lane_sum(d_sn_t * row(kk, t), ones), t)
            _keep_step(dsa_acc, dsa_t, t)
            dk_db = _rows_times(jnp.concatenate([row(vv, t), row(sar, t)], axis=0), d_sn_t)
            db = dk_db[1:2]
            rows[t] = (_colsum(d_sn_t * state_before(t)), dk_db[0:1], db * row(aa, t), db * row(kap_, t))
            if t % 8 == 0:
                for j, ref in enumerate((dw_ref, dk_ref, dkap_ref, da_ref)):
                    ref[t:t + 8, :] = jnp.concatenate([rows[u][j] for u in range(t, t + 8)], axis=0)

        def state_rows():
            dsa_rows = _steps_to_rows(dsa_acc[...])
            d_r, d_kap = [], []
            for j in range(-1, C):
                lhs = ([row(dd, j)] if j >= 0 else []) + ([row(dsa_rows, j + 1)] if j + 1 < C else [])
                out = _rows_times(jnp.concatenate(lhs, axis=0), state_before(j + 1))
                if j >= 0:
                    d_r.append(out[0:1])
                if j + 1 < C:
                    d_kap.append(out[-1:])
            dr_ref[...] = jnp.concatenate(d_r, axis=0)
            dkap_ref[...] = dkap_ref[...] - jnp.concatenate(d_kap, axis=0)

        for t in reversed(range(C)):
            dof = _expand_cols(do_cols, t)
            if t == C - 1:
                d_sn = ds_scr[...] + dof * row(rr, t)
                dsa = _lane_sum(d_sn * row(bb, t), ones)
            else:
                v_t = d_sn * row(ww, t + 1) + dof * row(rr, t)
                ls = _lane_sum(v_t * row(bb, t), ones)
                emit_rows(t + 1, d_sn, dsa)
                d_sn = v_t - dsa * row(kap_, t + 1)
                dsa = ls - dsa * row(e1, t)
        emit_rows(0, d_sn, dsa)
        d_s = d_sn * row(ww, 0) - dsa * row(kap_, 0)
        ds_scr[...] = d_s
        dv_ref[...] = _steps_to_rows(dv_acc[...])
        state_rows()

    spec = pl.BlockSpec((C, RWKV_WIDTH), lambda i: (nc - 1 - i, 0))
    states = pl.BlockSpec((C, RWKV_HEAD, RWKV_WIDTH), lambda i: (nc - 1 - i, 0, 0))
    before = pl.BlockSpec((1, RWKV_HEAD, RWKV_WIDTH), lambda i: (jnp.maximum((nc - 1 - i) * C - 1, 0), 0, 0))
    return pl.pallas_call(
        body, name="wkv_bwd", grid=(nc,),
        in_specs=[spec] * 8 + [states, before],
        out_specs=[spec] * 6,
        out_shape=[jax.ShapeDtypeStruct((T, RWKV_WIDTH), F32)] * 6,
        scratch_shapes=[pltpu.VMEM((RWKV_HEAD, RWKV_WIDTH), F32)] * 3,
        compiler_params=_params(dimension_semantics=("arbitrary",)),
    )(r, w, k, v, kap, a, sa_rows, d_o, s_all, s_all)


W = RWKV_WIDTH


def _softplus(y):
    return jnp.maximum(y, 0.0) + jnp.log(1.0 + jnp.exp(-jnp.abs(y)))


def _prep_fn(kr, xwa, w0, a0, k_k, k_a, wup_pad, aup_pad, ones64):
    w_log = -_softplus(-(w0 + _mm_nt(jnp.tanh(xwa), wup_pad))) - 0.5
    decay = jnp.exp(-jnp.exp(w_log))
    a = jax.nn.sigmoid(a0 + _mm_nt(xwa, aup_pad))
    kk = kr * k_k
    kap = kk / jnp.maximum(jnp.sqrt(_head_mix(kk * kk, ones64)), 1e-12)
    k = kr * (1.0 + (a - 1.0) * k_a)
    return decay, k, kap, a


def _shift_down(p, first_row):
    rows = lax.broadcasted_iota(jnp.int32, p.shape, 0)
    return jnp.where(rows == 0, first_row, pltpu.roll(p, 1, axis=0))


def _shift_up(z, last_row):
    n = z.shape[0]
    rows = lax.broadcasted_iota(jnp.int32, z.shape, 0)
    return jnp.where(rows == n - 1, last_row, pltpu.roll(z, n - 1, axis=0))


def _prev_block_spec():
    return pl.BlockSpec((8, RWKV_COLS), lambda i: (jnp.maximum(i * (TOK_TILE // 8) - 1, 0), 0))


def _mixed(p_ref, prev8_ref, mu_ref, first_tile):
    p = p_ref[...]
    first_row = jnp.where(first_tile, 0.0, prev8_ref[7:8, :])
    prev = _shift_down(p, first_row)
    return p, prev, p + mu_ref[...] * (prev - p)


def _inproj_prep(x, norm_g, w_in_t, mu, w0, a0, k_k, k_a, wup_pad, aup_pad, ones64):
    T = x.shape[1]

    def body(x_ref, g_ref, w_in_ref, mu_ref, w0_ref, a0_ref, kk_ref, ka_ref, wup_ref, aup_ref, ones_ref,
             pr_ref, pw_ref, r_ref, w_ref, k_ref, v_ref, kap_ref, a_ref, grw_ref, last_scr):
        @pl.when(pl.program_id(0) == 0)
        def _():
            last_scr[...] = jnp.zeros_like(last_scr)

        ub = _rmsnorm(x_ref[...], g_ref[...]).astype(BF16)
        pr_ref[...] = _dot_nt_bf(ub, w_in_ref[:RET_COLS, :])
        p = _dot_nt_bf(ub, w_in_ref[RET_COLS:, :])
        pw_ref[...] = p
        ps = p + mu_ref[...] * (_shift_down(p, last_scr[7:8, :]) - p)
        last_scr[...] = p[TOK_TILE - 8:, :]
        decay, k, kap, a = _prep_fn(ps[:, W:2 * W], ps[:, 4 * W:], w0_ref[...], a0_ref[...], kk_ref[...], ka_ref[...],
                                    wup_ref[...], aup_ref[...], ones_ref[...])
        r_ref[...] = ps[:, 0:W]
        w_ref[...] = decay
        k_ref[...] = k
        v_ref[...] = ps[:, 2 * W:3 * W]
        kap_ref[...] = kap
        a_ref[...] = a
        grw_ref[...] = ps[:, 3 * W:4 * W]

    vec = _full((1, W))
    return pl.pallas_call(
        body, name="inproj_prep", grid=(T // TOK_TILE,),
        in_specs=[_rows_of_one(TOK_TILE, D_MODEL), _full((1, D_MODEL)), _full((IN_COLS, D_MODEL)), _full((1, RWKV_COLS)),
                  vec, vec, vec, vec, _full((W, 2 * LORA)), _full((W, 2 * LORA)), _full((256, 128))],
        out_specs=[_rows(TOK_TILE, RET_COLS), _rows(TOK_TILE, RWKV_COLS)] + [_rows(TOK_TILE, W)] * 7,
        out_shape=[jax.ShapeDtypeStruct((T, RET_COLS), F32), jax.ShapeDtypeStruct((T, RWKV_COLS), F32)]
        + [jax.ShapeDtypeStruct((T, W), F32)] * 7,
        scratch_shapes=[pltpu.VMEM((8, RWKV_COLS), F32)],
        compiler_params=_params(dimension_semantics=("arbitrary",)),
    )(x, norm_g, w_in_t, mu, w0, a0, k_k, k_a, wup_pad, aup_pad, ones64)


def _prep_bwd(p_rwkv, mu, w0, a0, k_k, k_a, wup_pad, aup_pad, ones64, dr, dw, dk, dv, dkap, da, dg, dr2, dk2, dv2):
    T = p_rwkv.shape[0]
    nt = T // TOK_TILE

    def body(p_ref, prev8_ref, mu_ref, w0_ref, a0_ref, kk_ref, ka_ref, wup_ref, aup_ref, ones_ref,
             dr_ref, dw_ref, dk_ref, dv_ref, dkap_ref, da_ref, dg_ref, dr2_ref, dk2_ref, dv2_ref,
             dp_ref, dmu_ref, dw0_ref, da0_ref, dkk_ref, dka_ref, dwup_ref, daup_ref, zrow_scr):
        i = pl.program_id(0)
        accs = (dmu_ref, dw0_ref, da0_ref, dkk_ref, dka_ref, dwup_ref, daup_ref)

        @pl.when(i == 0)
        def _():
            zrow_scr[...] = jnp.zeros_like(zrow_scr)
            for ref in accs:
                ref[...] = jnp.zeros_like(ref)

        p, prev, ps = _mixed(p_ref, prev8_ref, mu_ref, i == nt - 1)
        ones = ones_ref[...]
        _, vjp = jax.vjp(lambda *args: _prep_fn(*args, ones), ps[:, W:2 * W], ps[:, 4 * W:], w0_ref[...], a0_ref[...],
                         kk_ref[...], ka_ref[...], wup_ref[...], aup_ref[...])
        dkr, dxwa, dw0, da0, dkk, dka, dwup, daup = vjp(
            (dw_ref[...], dk_ref[...] + dk2_ref[...], dkap_ref[...], da_ref[...]))
        dps = jnp.concatenate([dr_ref[...] + dr2_ref[...], dkr, dv_ref[...] + dv2_ref[...], dg_ref[...], dxwa], axis=1)
        z = dps * mu_ref[...]
        dp_ref[...] = dps - z + _shift_up(z, zrow_scr[0:1, :])
        zrow_scr[0:1, :] = z[0:1, :]
        for ref, val in zip(accs, (_colsum(dps * (prev - p)), dw0, da0, dkk, dka, dwup, daup)):
            ref[...] += val

    rev = lambda i: (nt - 1 - i, 0)
    vec = _full((1, W))
    lora = _full((W, 2 * LORA))
    tile = pl.BlockSpec((TOK_TILE, W), rev)
    prev8 = pl.BlockSpec((8, RWKV_COLS), lambda i: (jnp.maximum((nt - 1 - i) * (TOK_TILE // 8) - 1, 0), 0))
    return pl.pallas_call(
        body, name="prep_bwd", grid=(nt,),
        in_specs=[pl.BlockSpec((TOK_TILE, RWKV_COLS), rev), prev8, _full((1, RWKV_COLS)), vec, vec, vec, vec, lora, lora,
                  _full((256, 128))] + [tile] * 10,
        out_specs=[pl.BlockSpec((TOK_TILE, RWKV_COLS), rev), _full((1, RWKV_COLS)), vec, vec, vec, vec, lora, lora],
        out_shape=[jax.ShapeDtypeStruct((T, RWKV_COLS), F32), jax.ShapeDtypeStruct((1, RWKV_COLS), F32)]
        + [jax.ShapeDtypeStruct((1, W), F32)] * 4 + [jax.ShapeDtypeStruct((W, 2 * LORA), F32)] * 2,
        scratch_shapes=[pltpu.VMEM((8, RWKV_COLS), F32)],
        compiler_params=_params(dimension_semantics=("arbitrary",)),
    )(p_rwkv, p_rwkv, mu, w0, a0, k_k, k_a, wup_pad, aup_pad, ones64, dr, dw, dk, dv, dkap, da, dg, dr2, dk2, dv2)


def _silu(x):
    return x * jax.nn.sigmoid(x)


def _post_y(o, r, k, v, g_rw, ret_raw, g_ret, ret_gn_g, gn_g, gn_b, r_k, avg128, avg64, ones64):
    xc = ret_raw - _head_mix(ret_raw, avg128)
    ret = xc * lax.rsqrt(_head_mix(xc * xc, avg128) + RET_GN_EPS)
    y_ret = _silu(g_ret) * (ret * ret_gn_g)
    oc = o - _head_mix(o, avg64)
    on = oc * lax.rsqrt(_head_mix(oc * oc, avg64) + RWKV_GN_EPS) * gn_g + gn_b
    bonus = _head_mix(r * k * r_k, ones64) * v
    y_rwkv = _silu(g_rw) * (on + bonus)
    return y_ret, y_rwkv


def _post_loss(h, final_g, target):
    err = _rmsnorm(h, final_g) - target
    return 0.5 * jnp.sum(jnp.mean(err * err, axis=-1))


def _post(o, r, k, v, g_rw, ret_raw, p_ret, x, target, ret_gn_g, gn_g, gn_b, r_k, final_g, w_out, avg128, avg64, ones64):
    T = x.shape[1]
    n_tok_out = 8

    def body(o_ref, r_ref, k_ref, v_ref, grw_ref, ret_ref, gret_ref, x_ref, tgt_ref, rg_ref, gg_ref, gb_ref, rk_ref, fg_ref,
             wo_ref, a128_ref, a64_ref, ones_ref, *outs):
        tok_outs, (dwo_ref, drg_ref, dgg_ref, dgb_ref, drk_ref, dfg_ref, loss_ref) = outs[:n_tok_out], outs[n_tok_out:]
        accs = (dwo_ref, drg_ref, dgg_ref, dgb_ref, drk_ref, dfg_ref, loss_ref)

        @pl.when(pl.program_id(0) == 0)
        def _():
            for ref in accs:
                ref[...] = jnp.zeros_like(ref)

        consts = (a128_ref[...], a64_ref[...], ones_ref[...])
        (y_ret, y_rwkv), vjp = jax.vjp(
            lambda *args: _post_y(*args, *consts), o_ref[...], r_ref[...], k_ref[...], v_ref[...], grw_ref[...], ret_ref[...],
            gret_ref[...], rg_ref[...], gg_ref[...], gb_ref[...], rk_ref[...])
        h = x_ref[...] + _dot_bf(y_ret, wo_ref[0:RET_WIDTH, :]) + _dot_bf(y_rwkv, wo_ref[RET_WIDTH:, :])
        loss, (dh, dfg) = jax.value_and_grad(_post_loss, argnums=(0, 1))(h, fg_ref[...], tgt_ref[...])
        dy_ret = _dot_nt_bf(dh, wo_ref[0:RET_WIDTH, :])
        dy_rwkv = _dot_nt_bf(dh, wo_ref[RET_WIDTH:, :])
        do, dr, dk, dv, dgrw, dret, dgret, drg, dgg, dgb, drk = vjp((dy_ret, dy_rwkv))
        for ref, val in zip(tok_outs, (dh, do, dr, dk, dv, dgrw, dret, dgret)):
            ref[...] = val
        dwo_ref[0:RET_WIDTH, :] += _dot_tn_bf(y_ret, dh)
        dwo_ref[RET_WIDTH:, :] += _dot_tn_bf(y_rwkv, dh)
        for ref, val in zip(accs[1:], (drg, dgg, dgb, drk, dfg, jnp.full((1, 128), loss, F32))):
            ref[...] += val

    tile = _rows(TOK_TILE, W)
    wide = _rows(TOK_TILE, D_MODEL)
    wide_of_one = _rows_of_one(TOK_TILE, D_MODEL)
    vec = _full((1, W))
    sq = _full((256, 128))
    return pl.pallas_call(
        body, name="post", grid=(T // TOK_TILE,),
        in_specs=[tile] * 6 + [pl.BlockSpec((TOK_TILE, W), lambda i: (i, 2)), wide_of_one, wide_of_one, vec, vec, vec, vec,
                               _full((1, D_MODEL)), _full((D_MODEL, D_MODEL)), sq, sq, sq],
        out_specs=[wide] + [tile] * 7 + [_full((D_MODEL, D_MODEL)), vec, vec, vec, vec, _full((1, D_MODEL)), _full((1, 128))],
        out_shape=[jax.ShapeDtypeStruct((T, D_MODEL), F32)] + [jax.ShapeDtypeStruct((T, W), F32)] * 7
        + [jax.ShapeDtypeStruct((D_MODEL, D_MODEL), F32)] + [jax.ShapeDtypeStruct((1, W), F32)] * 4
        + [jax.ShapeDtypeStruct((1, D_MODEL), F32), jax.ShapeDtypeStruct((1, 128), F32)],
        compiler_params=_params(dimension_semantics=("arbitrary",)),
    )(o, r, k, v, g_rw, ret_raw, p_ret, x, target, ret_gn_g, gn_g, gn_b, r_k, final_g, w_out, avg128, avg64, ones64)


def _inproj_bwd(x, norm_g, dp_qkv, dg_ret, dp_rwkv, dh, w_in_t):
    T = x.shape[1]
    widths = [dp.shape[1] for dp in (dp_qkv, dg_ret, dp_rwkv)]
    steps = T // TOK_TILE

    def body(x_ref, g_ref, dqkv_ref, dgret_ref, drwkv_ref, dh_ref, w_ref, dx_ref, dg_ref, mine_ref, sib_ref, acc_ref):
        @pl.when(pl.program_id(0) == 0)
        def _():
            dg_ref[...] = jnp.zeros_like(dg_ref)
            acc_ref[...] = jnp.zeros_like(acc_ref)

        u, vjp = jax.vjp(_rmsnorm, x_ref[...], g_ref[...])
        ub = u.astype(BF16)
        du, row = 0.0, 0
        for dp_ref, n in zip((dqkv_ref, dgret_ref, drwkv_ref), widths):
            dp = dp_ref[...].astype(BF16)
            du = du + _dot_bf(dp, w_ref[row:row + n, :])
            acc_ref[row:row + n, :] += _dot_tn_bf(dp, ub)
            row += n
        dx, dg = vjp(du)
        dx_ref[...] = dx + dh_ref[...]
        dg_ref[...] += dg

        @pl.when(pl.program_id(0) == steps - 1)
        def _():
            core = lax.axis_index("c")
            for dev in range(N_DEV):
                block = acc_ref[dev * SHARD_IN:(dev + 1) * SHARD_IN, :].astype(BF16)

                @pl.when(core == dev % 2)
                def _():
                    mine_ref[dev // 2] = block

                @pl.when(core != dev % 2)
                def _():
                    sib_ref[dev // 2] = block

    half = jax.ShapeDtypeStruct((N_DEV // 2, SHARD_IN, D_MODEL), BF16)
    once = lambda shape: pl.BlockSpec(shape, lambda i: (0,) * len(shape), pipeline_mode=pl.Buffered(1))
    return pl.pallas_call(
        body, name="inproj_bwd", grid=(steps,),
        in_specs=[_rows_of_one(TOK_TILE, D_MODEL), _full((1, D_MODEL))] + [_rows(TOK_TILE, n) for n in widths]
        + [_rows(TOK_TILE, D_MODEL), once((IN_COLS, D_MODEL))],
        out_specs=[_rows_of_one(TOK_TILE, D_MODEL), _full((1, D_MODEL)), once(half.shape), once(half.shape)],
        out_shape=[jax.ShapeDtypeStruct((1, T, D_MODEL), F32), jax.ShapeDtypeStruct((1, D_MODEL), F32), half, half],
        scratch_shapes=[pltpu.VMEM((IN_COLS, D_MODEL), F32)],
        compiler_params=_params(dimension_semantics=("arbitrary",)),
    )(x, norm_g, dp_qkv, dg_ret, dp_rwkv, dh, w_in_t)


def _pad_lora(w_up_t, first):
    z = jnp.zeros_like(w_up_t)
    return jnp.concatenate([w_up_t, z] if first else [z, w_up_t], axis=1)


def _local_grads(x, target, norm_g, w_in_t, ret_gn_g, mu, w_lora_up_t, w0, a_lora_up_t, a0, k_k, k_a, r_k, gn_g, gn_b,
                 w_out_bf, final_g):
    T = x.shape[1]
    tabs = _rope_tables(T) + _ret_tables()
    ones64 = _block_mix(128, RWKV_HEAD)
    avg64 = _block_mix(128, RWKV_HEAD, 1.0 / RWKV_HEAD)
    avg128 = _block_mix(128, RET_DV, 1.0 / RET_DV)
    wup_pad, aup_pad = _pad_lora(w_lora_up_t, True), _pad_lora(a_lora_up_t, False)

    p_ret, p_rwkv, r, w, k, v, kap, a, g_rw = _inproj_prep(x, norm_g, w_in_t, mu, w0, a0, k_k, k_a, wup_pad, aup_pad, ones64)
    ret_raw, s_saved = _ret_fwd(p_ret, tabs)
    o, s_all, sa_rows = _wkv_fwd(r, w, k, v, kap, a)
    (dh, do, dr2, dk2, dv2, dgrw, dret, dgret, d_w_out, d_ret_gn_g, d_gn_g, d_gn_b, d_r_k, d_final_g, loss) = _post(
        o, r, k, v, g_rw, ret_raw, p_ret, x, target, ret_gn_g, gn_g, gn_b, r_k, final_g, w_out_bf, avg128, avg64, ones64)
    dr, dw, dk, dv, dkap, da = _wkv_bwd(r, w, k, v, kap, a, s_all, sa_rows, do)
    dp_rwkv, d_mu, d_w0, d_a0, d_k_k, d_k_a, d_wup, d_aup = _prep_bwd(
        p_rwkv, mu, w0, a0, k_k, k_a, wup_pad, aup_pad, ones64, dr, dw, dk, dv, dkap, da, dgrw, dr2, dk2, dv2)
    dp_qkv = _ret_bwd(p_ret, s_saved, dret, tabs)
    dx, d_norm_g, *d_w_in = _inproj_bwd(x, norm_g, dp_qkv, dgret, dp_rwkv, dh, w_in_t)
    grads = dict(norm_g=d_norm_g, w_in=d_w_in, ret_gn_g=d_ret_gn_g, rwkv_mu=d_mu, w_lora_up=d_wup[:, :LORA], w0=d_w0,
                 a_lora_up=d_aup[:, LORA:], a0=d_a0, k_k=d_k_k, k_a=d_k_a, r_k=d_r_k, rwkv_gn_g=d_gn_g, rwkv_gn_b=d_gn_b,
                 w_out=d_w_out, final_norm_g=d_final_g)
    return loss, dx, grads


def _mesh_pos():
    return lax.axis_index("x"), lax.axis_index("y"), lax.axis_index("c")


def _all_gather(shards):
    n = len(shards)

    def body(*refs):
        x_refs, out_refs = refs[:n], refs[n:2 * n]
        send_sems, recv_sems, local_sems = refs[2 * n:]
        x, y, c = _mesh_pos()
        me, sibling = (x, y, c), (x, y, 1 - c)
        chips = [(1 - x, y), (x, 1 - y), (1 - x, 1 - y)]

        def rows(a, pos):
            m = x_refs[a].shape[0]
            return out_refs[a].at[pl.ds((4 * pos[0] + 2 * pos[1] + pos[2]) * m, m), :]

        def copy(a, k, block, to, src=None):
            return pltpu.make_async_remote_copy(
                src_ref=rows(a, block) if src is None else src, dst_ref=rows(a, block),
                send_sem=send_sems.at[a, k], recv_sem=recv_sems.at[a, k], device_id=to, device_id_type=MESH)

        mine = [pltpu.make_async_copy(x_refs[a], rows(a, me), local_sems.at[a]) for a in range(n)]
        for cp in mine:
            cp.start()
        first = []
        for a in range(n):
            first.append(copy(a, 0, me, sibling, src=x_refs[a]))
            first += [copy(a, 1 + j, me, (*chip, c), src=x_refs[a]) for j, chip in enumerate(chips)]
        for cp in first:
            cp.start()
        passed = []
        for j, chip in enumerate(chips):
            for a in range(n):
                copy(a, 1 + j, (*chip, c), me).wait_recv()
                passed.append(copy(a, 4 + j, (*chip, c), sibling))
                passed[-1].start()
        for a in range(n):
            copy(a, 0, sibling, me).wait_recv()
            for j, chip in enumerate(chips):
                copy(a, 4 + j, (*chip, 1 - c), me).wait_recv()
        for cp in first + passed:
            cp.wait_send()
        for cp in mine:
            cp.wait()

    vmem = pl.BlockSpec(memory_space=pltpu.VMEM)
    return pl.pallas_call(
        body, name="gather_weights",
        out_shape=[jax.ShapeDtypeStruct((N_DEV * s.shape[0], s.shape[1]), s.dtype) for s in shards],
        in_specs=[vmem] * n, out_specs=[vmem] * n,
        scratch_shapes=[pltpu.SemaphoreType.DMA((n, 7)), pltpu.SemaphoreType.DMA((n, 7)), pltpu.SemaphoreType.DMA((n,))],
        compiler_params=_params(),
    )(*shards)


N_CHIP = 4


def _exchange_pairs(big, small):
    nb, ns = len(big), len(small)

    def body(*refs):
        big_in, small_in = refs[:nb], refs[nb:nb + ns]
        theirs, small_out = refs[nb + ns:2 * nb + ns], refs[2 * nb + ns:2 * nb + 2 * ns]
        pair_send, pair_recv, send_sems, recv_sems, local_sems = refs[2 * nb + 2 * ns:]
        x, y, c = _mesh_pos()
        me = 4 * x + 2 * y + c
        local = [pltpu.make_async_copy(small_in[a].at[me], small_out[a].at[me], local_sems.at[a]) for a in range(ns)]
        for cp in local:
            cp.start()
        copies = [pltpu.make_async_remote_copy(
            src_ref=big_in[a], dst_ref=theirs[a], send_sem=pair_send.at[a], recv_sem=pair_recv.at[a],
            device_id=(x, y, 1 - c), device_id_type=MESH) for a in range(nb)]
        for k in range(1, N_DEV):
            peer = (x ^ (k >> 2), y ^ ((k >> 1) & 1), c ^ (k & 1))
            peer_idx = 4 * peer[0] + 2 * peer[1] + peer[2]
            copies += [pltpu.make_async_remote_copy(
                src_ref=small_in[a].at[peer_idx], dst_ref=small_out[a].at[me], send_sem=send_sems.at[a, k - 1],
                recv_sem=recv_sems.at[a, k - 1], device_id=peer, device_id_type=MESH) for a in range(ns)]
        for cp in copies:
            cp.start()
        for cp in copies:
            cp.wait()
        for cp in local:
            cp.wait()

    hbm = pl.BlockSpec(memory_space=pl.ANY)
    out_shape = [jax.ShapeDtypeStruct(p.shape, p.dtype) for p in big + small]
    dma = pltpu.SemaphoreType.DMA
    res = pl.pallas_call(
        body, name="exchange_pairs", out_shape=out_shape,
        in_specs=[hbm] * (nb + ns), out_specs=[hbm] * len(out_shape),
        scratch_shapes=[dma((nb,)), dma((nb,)), dma((ns, 7)), dma((ns, 7)), dma((ns,))],
        compiler_params=_params(),
    )(*big, *small)
    return res[:nb], res[nb:]


def _pair_sum(name, mine, theirs, row_tile):
    _, rows, cols = mine.shape

    def body(a_ref, b_ref, o_ref):
        o_ref[...] = (a_ref[...].astype(F32) + b_ref[...].astype(F32)).astype(o_ref.dtype)

    spec = pl.BlockSpec((N_CHIP, row_tile, cols), lambda i: (0, i, 0))
    return pl.pallas_call(
        body, name=name, grid=(rows // row_tile,), in_specs=[spec, spec], out_specs=spec,
        out_shape=jax.ShapeDtypeStruct(mine.shape, mine.dtype),
        compiler_params=_params(dimension_semantics=("arbitrary",)),
    )(mine, theirs)


def _exchange_chips(parts):
    n = len(parts)

    def body(*refs):
        in_refs, out_refs = refs[:n], refs[n:2 * n]
        send_sems, recv_sems, local_sems = refs[2 * n:]
        x, y, c = _mesh_pos()
        my_chip = 2 * x + y
        own = [pltpu.make_async_copy(in_refs[a].at[my_chip], out_refs[a].at[my_chip], local_sems.at[a]) for a in range(n)]
        for cp in own:
            cp.start()
        copies = []
        for k in range(1, N_CHIP):
            px, py = x ^ (k >> 1), y ^ (k & 1)
            copies += [pltpu.make_async_remote_copy(
                src_ref=in_refs[a].at[2 * px + py], dst_ref=out_refs[a].at[my_chip], send_sem=send_sems.at[a, k - 1],
                recv_sem=recv_sems.at[a, k - 1], device_id=(px, py, c), device_id_type=MESH) for a in range(n)]
        for cp in copies:
            cp.start()
        for cp in copies:
            cp.wait()
        for cp in own:
            cp.wait()

    hbm = pl.BlockSpec(memory_space=pl.ANY)
    dma = pltpu.SemaphoreType.DMA
    return pl.pallas_call(
        body, name="exchange_chips",
        out_shape=[jax.ShapeDtypeStruct(p.shape, p.dtype) for p in parts],
        in_specs=[hbm] * n, out_specs=[hbm] * n,
        scratch_shapes=[dma((n, N_CHIP - 1)), dma((n, N_CHIP - 1)), dma((n,))],
        compiler_params=_params(),
    )(*parts)


def _adamw(w, g, m, v):
    m = ADAM_B1 * m + (1.0 - ADAM_B1) * g
    v = ADAM_B2 * v + (1.0 - ADAM_B2) * (g * g)
    m_hat = m / (1.0 - ADAM_B1 ** ADAM_STEP)
    v_hat = v / (1.0 - ADAM_B2 ** ADAM_STEP)
    return -ADAM_LR * (m_hat / (jnp.sqrt(v_hat) + ADAM_EPS) + ADAM_WD * w), m, v


def _sum_parts(name, parts, row_tile):
    n_parts, rows, cols = parts.shape

    def body(p_ref, g_ref):
        g = p_ref[0].astype(F32)
        for s in range(1, n_parts):
            g = g + p_ref[s].astype(F32)
        g_ref[...] = g

    return pl.pallas_call(
        body, name=name, grid=(rows // row_tile,),
        in_specs=[pl.BlockSpec((n_parts, row_tile, cols), lambda i: (0, i, 0))],
        out_specs=pl.BlockSpec((row_tile, cols), lambda i: (i, 0)),
        out_shape=jax.ShapeDtypeStruct((rows, cols), F32),
        compiler_params=_params(dimension_semantics=("arbitrary",)),
    )(parts)


def _adamw_apply(name, g, w, m, v, row_tile):
    _, rows, cols = w.shape

    def body(g_ref, w_ref, m_ref, v_ref, d_ref, nm_ref, nv_ref):
        d_ref[0], nm_ref[0], nv_ref[0] = _adamw(w_ref[0], g_ref[0], m_ref[0], v_ref[0])

    tile = pl.BlockSpec((1, row_tile, cols), lambda i: (0, i, 0))
    return pl.pallas_call(
        body, name=name, grid=(rows // row_tile,), in_specs=[tile] * 4, out_specs=[tile] * 3,
        out_shape=[jax.ShapeDtypeStruct((1, rows, cols), F32)] * 3,
        compiler_params=_params(dimension_semantics=("arbitrary",)),
    )(g, w, m, v)


def _reduce_adamw_2d(name, parts, w, m, v, row_tile):
    n_parts, rows, cols = parts.shape

    def body(p_ref, w_ref, m_ref, v_ref, g_ref, d_ref, nm_ref, nv_ref):
        g = p_ref[0].astype(F32)
        for s in range(1, n_parts):
            g = g + p_ref[s].astype(F32)
        g_ref[...] = g
        d_ref[...], nm_ref[...], nv_ref[...] = _adamw(w_ref[...], g, m_ref[...], v_ref[...])

    tile = pl.BlockSpec((row_tile, cols), lambda i: (i, 0))
    return pl.pallas_call(
        body, name=name, grid=(rows // row_tile,),
        in_specs=[pl.BlockSpec((n_parts, row_tile, cols), lambda i: (0, i, 0)), tile, tile, tile],
        out_specs=[tile] * 4,
        out_shape=[jax.ShapeDtypeStruct((rows, cols), F32)] * 4,
        compiler_params=_params(dimension_semantics=("arbitrary",)),
    )(parts, w, m, v)


def _reduce_adamw_t(name, parts_t, w, m, v, sum_tile, row_tile):
    g = _sum_parts(name + "_sum", parts_t, sum_tile).T[None]
    return [g] + list(_adamw_apply(name, g, w, m, v, row_tile))


def _reduce_adamw(name, parts, w, m, v, row_tile):
    n_parts, rows, cols = parts.shape

    def body(p_ref, w_ref, m_ref, v_ref, g_ref, d_ref, nm_ref, nv_ref):
        g = p_ref[0].astype(F32)
        for s in range(1, n_parts):
            g = g + p_ref[s].astype(F32)
        g_ref[0] = g
        d_ref[0], nm_ref[0], nv_ref[0] = _adamw(w_ref[0], g, m_ref[0], v_ref[0])

    tile = pl.BlockSpec((1, row_tile, cols), lambda i: (0, i, 0))
    return pl.pallas_call(
        body, name=name, grid=(rows // row_tile,),
        in_specs=[pl.BlockSpec((n_parts, row_tile, cols), lambda i: (0, i, 0)), tile, tile, tile],
        out_specs=[tile] * 4,
        out_shape=[jax.ShapeDtypeStruct((1, rows, cols), F32)] * 4,
        compiler_params=_params(dimension_semantics=("arbitrary",)),
    )(parts, w, m, v)


_SMALL = (("norm_g", 1024), ("ret_gn_g", 512), ("rwkv_mu", 2176), ("w0", 512), ("a0", 512), ("k_k", 512), ("k_a", 512),
          ("r_k", 512), ("rwkv_gn_g", 512), ("rwkv_gn_b", 512), ("final_norm_g", 1024))
_SMALL_LANES = sum(n for _, n in _SMALL) + 128
_WEIGHTS = ("norm_g", "w_in", "ret_gn_g", "rwkv_mu", "w_lora_up", "w0", "a_lora_up", "a0", "k_k", "k_a", "r_k", "rwkv_gn_g",
            "rwkv_gn_b", "w_out", "final_norm_g")


def _adamw_vectors(parts, wts, mom, var):
    k = len(_SMALL)

    def body(p_ref, *refs):
        w_refs, m_refs, v_refs, outs = refs[:k], refs[k:2 * k], refs[2 * k:3 * k], refs[3 * k:]
        g_all = p_ref[0]
        for s in range(1, N_DEV):
            g_all = g_all + p_ref[s]
        off = 0
        for i, (name, n) in enumerate(_SMALL):
            g = g_all[:, off:off + n]
            off += n
            if name == "r_k":
                g = jnp.concatenate([g[:, RWKV_HEAD * h:RWKV_HEAD * (h + 1)] for h in range(RWKV_HEADS)], axis=0)[None]
            outs[4 * i][...] = g
            outs[4 * i + 1][...], outs[4 * i + 2][...], outs[4 * i + 3][...] = _adamw(
                w_refs[i][...], g, m_refs[i][...], v_refs[i][...])
        outs[4 * k][...] = g_all[:, off:off + 128]

    vmem = pl.BlockSpec(memory_space=pltpu.VMEM)
    shapes = [jax.ShapeDtypeStruct(wts[n].shape, F32) for n, _ in _SMALL for _ in range(4)] + [jax.ShapeDtypeStruct((1, 128), F32)]
    res = pl.pallas_call(
        body, name="adamw_vectors", out_shape=shapes,
        in_specs=[vmem] * (1 + 3 * k), out_specs=[vmem] * len(shapes), compiler_params=_params(),
    )(parts, *[wts[n] for n, _ in _SMALL], *[mom[n] for n, _ in _SMALL], *[var[n] for n, _ in _SMALL])
    return {n: res[4 * i:4 * i + 4] for i, (n, _) in enumerate(_SMALL)}, res[4 * k]


def kernel(x, norm_g, w_in, ret_gn_g, rwkv_mu, w_lora_up, w0, a_lora_up, a0, k_k, k_a, r_k, rwkv_gn_g, rwkv_gn_b, w_out, final_norm_g, loss_target, m_norm_g, m_w_in, m_ret_gn_g, m_rwkv_mu, m_w_lora_up, m_w0, m_a_lora_up, m_a0, m_k_k, m_k_a, m_r_k, m_rwkv_gn_g, m_rwkv_gn_b, m_w_out, m_final_norm_g, v_norm_g, v_w_in, v_ret_gn_g, v_rwkv_mu, v_w_lora_up, v_w0, v_a_lora_up, v_a0, v_k_k, v_k_a, v_r_k, v_rwkv_gn_g, v_rwkv_gn_b, v_w_out, v_final_norm_g):
    wts = dict(norm_g=norm_g, w_in=w_in, ret_gn_g=ret_gn_g, rwkv_mu=rwkv_mu, w_lora_up=w_lora_up, w0=w0, a_lora_up=a_lora_up,
               a0=a0, k_k=k_k, k_a=k_a, r_k=r_k, rwkv_gn_g=rwkv_gn_g, rwkv_gn_b=rwkv_gn_b, w_out=w_out,
               final_norm_g=final_norm_g)
    mom = dict(norm_g=m_norm_g, w_in=m_w_in, ret_gn_g=m_ret_gn_g, rwkv_mu=m_rwkv_mu, w_lora_up=m_w_lora_up, w0=m_w0,
               a_lora_up=m_a_lora_up, a0=m_a0, k_k=m_k_k, k_a=m_k_a, r_k=m_r_k, rwkv_gn_g=m_rwkv_gn_g,
               rwkv_gn_b=m_rwkv_gn_b, w_out=m_w_out, final_norm_g=m_final_norm_g)
    var = dict(norm_g=v_norm_g, w_in=v_w_in, ret_gn_g=v_ret_gn_g, rwkv_mu=v_rwkv_mu, w_lora_up=v_w_lora_up, w0=v_w0,
               a_lora_up=v_a_lora_up, a0=v_a0, k_k=v_k_k, k_a=v_k_a, r_k=v_r_k, rwkv_gn_g=v_rwkv_gn_g,
               rwkv_gn_b=v_rwkv_gn_b, w_out=v_w_out, final_norm_g=v_final_norm_g)
    shapes = {n: wts[n].shape for n in _WEIGHTS}

    w_in_t, w_out_bf, wup_t, aup_t = _all_gather(
        [w_in[0].T.astype(BF16), w_out[0].astype(BF16), w_lora_up[0].T, a_lora_up[0].T])

    loss, dx, g = _local_grads(
        x, loss_target, norm_g, w_in_t, ret_gn_g, rwkv_mu, wup_t, w0, aup_t, a0, k_k, k_a,
        r_k.reshape(1, W), rwkv_gn_g, rwkv_gn_b, w_out_bf, final_norm_g.reshape(1, D_MODEL))

    small = jnp.concatenate([g[n] for n, _ in _SMALL] + [loss], axis=1)
    core = lax.axis_index("c")
    by_core = lambda t: [lax.dynamic_index_in_dim(t, i, axis=1, keepdims=False) for i in (core, 1 - core)]
    in_mine, in_sib = g["w_in"]
    out_mine, out_sib = by_core(g["w_out"].reshape(N_CHIP, 2, SHARD_OUT, D_MODEL).astype(BF16))
    (in_theirs, out_theirs), parts = _exchange_pairs(
        [in_sib, out_sib],
        [g["w_lora_up"].reshape(N_DEV, SHARD_LORA, LORA), g["a_lora_up"].reshape(N_DEV, SHARD_LORA, LORA),
         jnp.broadcast_to(small[None], (N_DEV, 1, _SMALL_LANES))])
    by_chip = _exchange_chips([_pair_sum("pair_sum_w_in", in_mine, in_theirs, SHARD_IN // 2),
                               _pair_sum("pair_sum_w_out", out_mine, out_theirs, SHARD_OUT)])
    res = {}
    res["w_in"] = [t.T[None] for t in _reduce_adamw_2d(
        "adamw_w_in", by_chip[0], w_in[0].T, m_w_in[0].T, v_w_in[0].T, SHARD_IN // 2)]
    res["w_out"] = _reduce_adamw("adamw_w_out", by_chip[1], w_out, m_w_out, v_w_out, SHARD_OUT)
    res["w_lora_up"] = _reduce_adamw_t("adamw_w_lora_up", parts[0], w_lora_up, m_w_lora_up, v_w_lora_up, LORA, LORA)
    res["a_lora_up"] = _reduce_adamw_t("adamw_a_lora_up", parts[1], a_lora_up, m_a_lora_up, v_a_lora_up, LORA, LORA)
    as_row = lambda d: {n: d[n] if d[n].ndim > 1 else d[n].reshape(1, size) for n, size in _SMALL}
    vec, loss_row = _adamw_vectors(parts[2], as_row(wts), as_row(mom), as_row(var))
    res.update(vec)
    res = {n: [t.reshape(shapes[n]) for t in res[n]] for n in _WEIGHTS}
    return (loss_row[0, 0], dx, *[res[n][0] for n in _WEIGHTS], *[res[n][1] for n in _WEIGHTS],
            *[res[n][2] for n in _WEIGHTS], *[res[n][3] for n in _WEIGHTS])
```

```python
import numpy as np
import jax
import jax.numpy as jnp
from jax import lax
from jax.experimental import pallas as pl
from jax.experimental.pallas import tpu as pltpu

F32 = jnp.float32
BF16 = jnp.bfloat16

D_MODEL = 1024
CHUNK = 64
RET_HEADS = 4
RET_DV = 128
RET_DK = 64
RET_QK = 256
RET_WIDTH = 512
RWKV_WIDTH = 512
RWKV_HEAD = 64
RWKV_HEADS = 8
LORA = 64
RET_COLS = 2 * RET_QK + 2 * RET_WIDTH
RWKV_COLS = 4 * RWKV_WIDTH + 2 * LORA
IN_COLS = RET_COLS + RWKV_COLS
ROPE_BASE = 10000.0
RMS_EPS = 1e-6
RET_GN_EPS = 1e-5
RWKV_GN_EPS = 64e-5
ADAM_LR = 0.001
ADAM_B1 = 0.9
ADAM_B2 = 0.999
ADAM_EPS = 1e-08
ADAM_WD = 0.01
ADAM_STEP = 10
N_DEV = 8
SHARD_IN = IN_COLS // N_DEV
SHARD_OUT = D_MODEL // N_DEV
SHARD_LORA = RWKV_WIDTH // N_DEV
VMEM_LIMIT = 56 * 1024 * 1024
TOK_TILE = 256
WKV_CHUNK = 64

MESH = pl.DeviceIdType.MESH


def _dot_bf(a, b):
    return jnp.dot(a.astype(BF16), b.astype(BF16), preferred_element_type=F32)


def _dot_nt_bf(a, b):
    return lax.dot_general(a.astype(BF16), b.astype(BF16), (((1,), (1,)), ((), ())), preferred_element_type=F32)


def _dot_tn_bf(a, b):
    return lax.dot_general(a.astype(BF16), b.astype(BF16), (((0,), (0,)), ((), ())), preferred_element_type=F32)


@jax.custom_vjp
def _mm(a, b):
    return _dot_bf(a, b)


@jax.custom_vjp
def _mm_nt(a, b):
    return _dot_nt_bf(a, b)


@jax.custom_vjp
def _mm_tn(a, b):
    return _dot_tn_bf(a, b)


_mm.defvjp(lambda a, b: (_dot_bf(a, b), (a, b)), lambda res, g: (_dot_nt_bf(g, res[1]), _dot_tn_bf(res[0], g)))
_mm_nt.defvjp(lambda a, b: (_dot_nt_bf(a, b), (a, b)), lambda res, g: (_dot_bf(g, res[1]), _dot_tn_bf(g, res[0])))
_mm_tn.defvjp(lambda a, b: (_dot_tn_bf(a, b), (a, b)), lambda res, g: (_dot_nt_bf(res[1], g), _dot_bf(res[0], g)))


def _trunc(x):
    return lax.bitcast_convert_type(lax.bitcast_convert_type(x, jnp.uint32) & jnp.uint32(0xFFFF0000), F32)


def _two_piece(x):
    hi = _trunc(x)
    return jnp.concatenate([hi, x - hi], axis=1)


def _mix_raw(x, mat2):
    return _unstack(jnp.dot(_two_piece(_stack(x)), mat2, preferred_element_type=F32))


@jax.custom_vjp
def _head_mix(x, mat2):
    return _mix_raw(x, mat2)


_head_mix.defvjp(lambda x, mat2: (_mix_raw(x, mat2), mat2), lambda mat2, g: (_mix_raw(g, mat2), jnp.zeros_like(mat2)))


def _swap_halves(x):
    lane = lax.broadcasted_iota(jnp.int32, x.shape, 1)
    return jnp.where((lane & (RET_DK - 1)) < RET_DK // 2, pltpu.roll(x, RET_QK - RET_DK // 2, axis=1),
                     pltpu.roll(x, RET_DK // 2, axis=1))


@jax.custom_vjp
def _rot(x):
    return _swap_halves(x)


_rot.defvjp(lambda x: (_swap_halves(x), None), lambda _, g: (_swap_halves(g),))


def _params(**kw):
    return pltpu.CompilerParams(vmem_limit_bytes=VMEM_LIMIT, **kw)


def _full(shape):
    nd = len(shape)
    return pl.BlockSpec(shape, lambda i, _nd=nd: (0,) * _nd)


def _rows(tile, width):
    return pl.BlockSpec((tile, width), lambda i: (i, 0))


def _rows_of_one(tile, width):
    return pl.BlockSpec((None, tile, width), lambda i: (0, i, 0))


def _block_mix(n, blk, scale=1.0):
    idx = np.arange(n) // blk
    m = (idx[:, None] == idx[None, :]).astype(np.float32) * scale
    return jnp.asarray(np.concatenate([m, m], axis=0))


def _rope_tables(T):
    half = RET_DK // 2
    expo = -np.arange(half, dtype=np.float32) / np.float32(half)
    freqs = np.exp(expo * np.float32(np.log(ROPE_BASE))).astype(np.float32)
    ang = np.arange(T, dtype=np.float32)[:, None] * freqs[None, :]
    cos, sin = np.cos(ang).astype(np.float32), np.sin(ang).astype(np.float32)
    cos_h = np.concatenate([cos, cos], axis=1)
    sin_h = np.concatenate([-sin, sin], axis=1)
    cos_t = np.tile(cos_h, (1, RET_HEADS))
    sin_t = np.tile(sin_h, (1, RET_HEADS))
    return jnp.asarray(cos_t), jnp.asarray(sin_t)


def _ret_tables():
    h = np.arange(RET_HEADS, dtype=np.float32)
    lg = np.log(1.0 - np.exp2(-5.0 - h)).astype(np.float32)
    idx = np.arange(CHUNK, dtype=np.float32)
    intra = np.exp(lg[:, None, None] * np.abs(idx[:, None] - idx[None, :])).astype(np.float32)
    q_dec = np.exp(lg[:, None] * (idx[None, :] + 1.0)).astype(np.float32)
    k_dec = np.exp(lg[:, None] * (CHUNK - 1.0 - idx[None, :])).astype(np.float32)
    chunk_dec = np.exp(lg * CHUNK).astype(np.float32)
    lane_head = np.arange(RET_QK) // RET_DK
    mask = (lane_head[None, :] == np.arange(RET_HEADS)[:, None]).astype(np.float32)
    m = np.broadcast_to(mask[:, None, :], (RET_HEADS, CHUNK, RET_QK)).copy()
    qd = m * q_dec[:, :, None]
    kd = m * k_dec[:, :, None]
    return jnp.asarray(intra), jnp.asarray(m), jnp.asarray(qd), jnp.asarray(kd), [float(c) for c in chunk_dec]


def _rmsnorm(x, g):
    return x * lax.rsqrt(jnp.mean(x * x, axis=-1, keepdims=True) + RMS_EPS) * g


def _ret_chunk(pq, pk, v_heads, s_heads, cos_t, sin_t, dec, hm, qd, kd, chunk_dec):
    q = pq * cos_t + _rot(pq) * sin_t
    k = (pk * cos_t + _rot(pk) * sin_t) * (RET_DK ** -0.5)
    outs, s_out = [], []
    for h in range(RET_HEADS):
        sc = _mm_nt(q * hm[h], k * hm[h]) * dec[h]
        intra = _mm(sc, v_heads[h])
        kv = _mm_tn(k * kd[h], v_heads[h])
        inter = _mm(q * qd[h], s_heads[h])
        outs.append(intra + inter)
        s_out.append(s_heads[h] * chunk_dec[h] + kv)
    return tuple(outs), tuple(s_out)


def _ret_specs():
    const = [_full((RET_HEADS, CHUNK, CHUNK)), _full((RET_HEADS, CHUNK, RET_QK)),
             _full((RET_HEADS, CHUNK, RET_QK)), _full((RET_HEADS, CHUNK, RET_QK))]
    return const


RET_GROUP = 8


def _ret_fwd(p_ret, tabs):
    T = p_ret.shape[0]
    G = RET_GROUP
    ng = T // (CHUNK * G)
    cos_t, sin_t, dec, hm, qd, kd, chunk_dec = tabs

    def body(p_ref, cos_ref, sin_ref, dec_ref, hm_ref, qd_ref, kd_ref, out_ref, sin_save_ref, s_scr):
        @pl.when(pl.program_id(0) == 0)
        def _():
            s_scr[...] = jnp.zeros_like(s_scr)

        consts = (dec_ref[...], hm_ref[...], qd_ref[...], kd_ref[...])
        s_heads = tuple(s_scr[h] for h in range(RET_HEADS))
        for c in range(G):
            rows = pl.ds(c * CHUNK, CHUNK)
            for h in range(RET_HEADS):
                sin_save_ref[c, h] = s_heads[h]
            v_heads = tuple(p_ref[rows, 2 * RET_QK + RET_DV * h:2 * RET_QK + RET_DV * (h + 1)] for h in range(RET_HEADS))
            outs, s_heads = _ret_chunk(p_ref[rows, 0:RET_QK], p_ref[rows, RET_QK:2 * RET_QK], v_heads, s_heads,
                                       cos_ref[rows, :], sin_ref[rows, :], *consts, chunk_dec)
            for h in range(RET_HEADS):
                out_ref[rows, RET_DV * h:RET_DV * (h + 1)] = outs[h]
        for h in range(RET_HEADS):
            s_scr[h] = s_heads[h]

    tok = CHUNK * G
    return pl.pallas_call(
        body, name="ret_fwd", grid=(ng,),
        in_specs=[pl.BlockSpec((tok, RET_COLS), lambda i: (i, 0)), _rows(tok, RET_QK), _rows(tok, RET_QK)] + _ret_specs(),
        out_specs=[_rows(tok, RET_WIDTH), pl.BlockSpec((G, RET_HEADS, RET_QK, RET_DV), lambda i: (i, 0, 0, 0))],
        out_shape=[jax.ShapeDtypeStruct((T, RET_WIDTH), F32),
                   jax.ShapeDtypeStruct((T // CHUNK, RET_HEADS, RET_QK, RET_DV), F32)],
        scratch_shapes=[pltpu.VMEM((RET_HEADS, RET_QK, RET_DV), F32)],
        compiler_params=_params(dimension_semantics=("arbitrary",)),
    )(p_ret, cos_t, sin_t, dec, hm, qd, kd)


def _ret_bwd(p_ret, s_saved, d_ret, tabs):
    T = p_ret.shape[0]
    G = RET_GROUP
    ng = T // (CHUNK * G)
    cos_t, sin_t, dec, hm, qd, kd, chunk_dec = tabs

    def body(p_ref, s_ref, dret_ref, cos_ref, sin_ref, dec_ref, hm_ref, qd_ref, kd_ref, dp_ref, ds_scr):
        @pl.when(pl.program_id(0) == 0)
        def _():
            ds_scr[...] = jnp.zeros_like(ds_scr)

        consts = (dec_ref[...], hm_ref[...], qd_ref[...], kd_ref[...])
        d_s = tuple(ds_scr[h] for h in range(RET_HEADS))
        for c in reversed(range(G)):
            rows = pl.ds(c * CHUNK, CHUNK)
            v_heads = tuple(p_ref[rows, 2 * RET_QK + RET_DV * h:2 * RET_QK + RET_DV * (h + 1)] for h in range(RET_HEADS))
            s_heads = tuple(s_ref[c, h] for h in range(RET_HEADS))
            tables = (cos_ref[rows, :], sin_ref[rows, :]) + consts
            _, vjp = jax.vjp(lambda a, b, c_, d: _ret_chunk(a, b, c_, d, *tables, chunk_dec),
                             p_ref[rows, 0:RET_QK], p_ref[rows, RET_QK:2 * RET_QK], v_heads, s_heads)
            d_out = tuple(dret_ref[rows, RET_DV * h:RET_DV * (h + 1)] for h in range(RET_HEADS))
            dq, dk, dv, d_s = vjp((d_out, d_s))
            dp_ref[rows, 0:RET_QK] = dq
            dp_ref[rows, RET_QK:2 * RET_QK] = dk
            for h in range(RET_HEADS):
                dp_ref[rows, 2 * RET_QK + RET_DV * h:2 * RET_QK + RET_DV * (h + 1)] = dv[h]
        for h in range(RET_HEADS):
            ds_scr[h] = d_s[h]

    tok = CHUNK * G
    rev = lambda i: (ng - 1 - i, 0)
    return pl.pallas_call(
        body, name="ret_bwd", grid=(ng,),
        in_specs=[pl.BlockSpec((tok, RET_COLS), rev),
                  pl.BlockSpec((G, RET_HEADS, RET_QK, RET_DV), lambda i: (ng - 1 - i, 0, 0, 0)),
                  pl.BlockSpec((tok, RET_WIDTH), rev), pl.BlockSpec((tok, RET_QK), rev), pl.BlockSpec((tok, RET_QK), rev)]
        + _ret_specs(),
        out_specs=pl.BlockSpec((tok, 2 * RET_QK + RET_WIDTH), rev),
        out_shape=jax.ShapeDtypeStruct((T, 2 * RET_QK + RET_WIDTH), F32),
        scratch_shapes=[pltpu.VMEM((RET_HEADS, RET_QK, RET_DV), F32)],
        compiler_params=_params(dimension_semantics=("arbitrary",)),
    )(p_ret, s_saved, d_ret, cos_t, sin_t, dec, hm, qd, kd)


def _block_ones():
    r = lax.broadcasted_iota(jnp.int32, (3 * 128, 128), 0)
    c = lax.broadcasted_iota(jnp.int32, (3 * 128, 128), 1)
    return (((r & 127) >> 6) == (c >> 6)).astype(BF16)


def _stack(x):
    return jnp.concatenate([x[:, 128 * p:128 * (p + 1)] for p in range(4)], axis=0)


def _unstack(y):
    n = y.shape[0] // 4
    return jnp.concatenate([y[n * p:n * (p + 1)] for p in range(4)], axis=1)


def _split(x, n):
    pieces = []
    for _ in range(n):
        p = x.astype(BF16)
        pieces.append(p)
        x = x - p.astype(F32)
    return pieces


def _lane_sum(x, ones):
    return _unstack(jnp.dot(_stack(x).astype(BF16), ones[:128], preferred_element_type=F32))


def _colsum(x):
    return jnp.sum(x, axis=0, keepdims=True)


def _rows_times(vecs, mat):
    n = vecs.shape[0]
    lane = lax.broadcasted_iota(jnp.int32, (n, 128), 1)
    lhs = jnp.concatenate([vecs[:, RWKV_HEAD * h:RWKV_HEAD * (h + 1)] for h in range(RWKV_HEADS)], axis=0)
    out = jnp.dot(lhs, mat, preferred_element_type=F32)
    tiles = [jnp.where(lane < RWKV_HEAD, out[2 * p * n:(2 * p + 1) * n, 128 * p:128 * (p + 1)],
                       out[(2 * p + 1) * n:(2 * p + 2) * n, 128 * p:128 * (p + 1)]) for p in range(4)]
    return jnp.concatenate(tiles, axis=1)


def _expand_cols(xt, t):
    lane = lax.broadcasted_iota(jnp.int32, (RWKV_HEAD, 128), 1)
    tiles = []
    for p in range(4):
        lo = jnp.broadcast_to(xt[128 * p:128 * p + RWKV_HEAD, t:t + 1], (RWKV_HEAD, 128))
        hi = jnp.broadcast_to(xt[128 * p + RWKV_HEAD:128 * (p + 1), t:t + 1], (RWKV_HEAD, 128))
        tiles.append(jnp.where(lane < RWKV_HEAD, lo, hi))
    return jnp.concatenate(tiles, axis=1)


def _expand_steps(y, t):
    lane = lax.broadcasted_iota(jnp.int32, (RWKV_HEAD, 128), 1)
    idx = jnp.where(lane < RWKV_HEAD, t, RWKV_HEAD + t)
    return jnp.concatenate([jnp.take_along_axis(y[:, 128 * p:128 * (p + 1)], idx, axis=1) for p in range(4)], axis=1)


def _keep_step(acc_ref, x, t):
    lane = lax.broadcasted_iota(jnp.int32, (1, RWKV_WIDTH), 1)
    mask = jnp.broadcast_to((lane & (RWKV_HEAD - 1)) == t, x.shape)
    pltpu.store(acc_ref, x, mask=mask)


def _steps_to_rows(acc):
    assert WKV_CHUNK == RWKV_HEAD
    tiles = []
    for p in range(4):
        tt = acc[:, 128 * p:128 * (p + 1)].T
        tiles.append(jnp.concatenate([tt[:RWKV_HEAD], tt[RWKV_HEAD:]], axis=1))
    return jnp.concatenate(tiles, axis=1)


def _head_sums(x, ones):
    return _unstack(jnp.dot(jnp.concatenate(_split(_stack(x), 3), axis=1), ones, preferred_element_type=F32))


def _wkv_fwd(r, w, k, v, kap, a):
    T = r.shape[0]
    C = WKV_CHUNK
    nc = T // C

    def body(r_ref, w_ref, k_ref, v_ref, kap_ref, a_ref, o_ref, s_all_ref, sa_rows_ref, s_scr, o_acc, sa_acc):
        @pl.when(pl.program_id(0) == 0)
        def _():
            s_scr[...] = jnp.zeros_like(s_scr)

        ones = _block_ones()
        rr, ww, kk, vv, kap_, aa = (ref[...] for ref in (r_ref, w_ref, k_ref, v_ref, kap_ref, a_ref))
        bb = kap_ * aa
        c1 = _head_sums(pltpu.roll(bb, 1, axis=0) * kap_, ones)
        row = lambda x, t: x[t:t + 1]

        v_cols = vv.T

        s_prev = s_scr[...]
        sa = _lane_sum(s_prev * (-row(kap_, 0)), ones)
        ls = None

        def emit_o(t, s_t):
            _keep_step(o_acc, _lane_sum(s_t * row(rr, t), ones), t)

        for t in range(C):
            u = s_prev * row(ww, t) + _expand_cols(v_cols, t) * row(kk, t)
            if t > 0:
                sa = ls - sa * row(c1, t)
            if t + 1 < C:
                ls = _lane_sum(u * (-row(kap_, t + 1)), ones)
            if t > 0:
                emit_o(t - 1, s_prev)
            s_prev = u + sa * row(bb, t)
            s_all_ref[t] = s_prev
            _keep_step(sa_acc, sa, t)
        emit_o(C - 1, s_prev)
        s_scr[...] = s_prev
        o_ref[...] = _steps_to_rows(o_acc[...])
        sa_rows_ref[...] = _steps_to_rows(sa_acc[...])

    spec = _rows(C, RWKV_WIDTH)
    return pl.pallas_call(
        body, name="wkv_fwd", grid=(nc,),
        in_specs=[spec] * 6,
        out_specs=[spec, pl.BlockSpec((C, RWKV_HEAD, RWKV_WIDTH), lambda i: (i, 0, 0)), spec],
        out_shape=[jax.ShapeDtypeStruct((T, RWKV_WIDTH), F32), jax.ShapeDtypeStruct((T, RWKV_HEAD, RWKV_WIDTH), F32),
                   jax.ShapeDtypeStruct((T, RWKV_WIDTH), F32)],
        scratch_shapes=[pltpu.VMEM((RWKV_HEAD, RWKV_WIDTH), F32)] * 3,
        compiler_params=_params(dimension_semantics=("arbitrary",)),
    )(r, w, k, v, kap, a)


def _wkv_bwd(r, w, k, v, kap, a, s_all, sa_rows, d_o):
    T = r.shape[0]
    C = WKV_CHUNK
    nc = T // C

    def body(r_ref, w_ref, k_ref, v_ref, kap_ref, a_ref, sa_rows_ref, do_ref, s_ref, s_before_ref,
             dr_ref, dw_ref, dk_ref, dv_ref, dkap_ref, da_ref, ds_scr, dv_acc, dsa_acc):
        first_chunk = pl.program_id(0) == nc - 1

        @pl.when(pl.program_id(0) == 0)
        def _():
            ds_scr[...] = jnp.zeros_like(ds_scr)

        ones = _block_ones()
        rr, ww, kk, vv, kap_, aa, sar, dd = (ref[...] for ref in (r_ref, w_ref, k_ref, v_ref, kap_ref, a_ref, sa_rows_ref, do_ref))
        bb = kap_ * aa
        e1 = _head_sums(pltpu.roll(kap_, C - 1, axis=0) * bb, ones)
        row = lambda x, t: x[t:t + 1]

        def state_before(t):
            return s_ref[t - 1] if t > 0 else jnp.where(first_chunk, 0.0, s_before_ref[0])

        do_cols = _steps_to_rows(dd)

        d_sn, dsa, rows = None, None, [None] * C

        def emit_rows(t, d_sn_t, dsa_t):
            _keep_step(dv_acc, _lane_sum(d_sn_t * row(kk, t), ones), t)
            _keep_step(dsa_acc, dsa_t, t)
            dk_db = _rows_times(jnp.concatenate([row(vv, t), row(sar, t)], axis=0), d_sn_t)
            db = dk_db[1:2]
            rows[t] = (_colsum(d_sn_t * state_before(t)), dk_db[0:1], db * row(aa, t), db * row(kap_, t))
            if t % 8 == 0:
                for j, ref in enumerate((dw_ref, dk_ref, dkap_ref, da_ref)):
                    ref[t:t + 8, :] = jnp.concatenate([rows[u][j] for u in range(t, t + 8)], axis=0)

        def state_rows():
            dsa_rows = _steps_to_rows(dsa_acc[...])
            d_r, d_kap = [], []
            for j in range(-1, C):
                lhs = ([row(dd, j)] if j >= 0 else []) + ([row(dsa_rows, j + 1)] if j + 1 < C else [])
                out = _rows_times(jnp.concatenate(lhs, axis=0), state_before(j + 1))
                if j >= 0:
                    d_r.append(out[0:1])
                if j + 1 < C:
                    d_kap.append(out[-1:])
            dr_ref[...] = jnp.concatenate(d_r, axis=0)
            dkap_ref[...] = dkap_ref[...] - jnp.concatenate(d_kap, axis=0)

        for t in reversed(range(C)):
            dof = _expand_steps(do_cols, t)
            if t == C - 1:
                d_sn = ds_scr[...] + dof * row(rr, t)
                dsa = _lane_sum(d_sn * row(bb, t), ones)
            else:
                v_t = d_sn * row(ww, t + 1) + dof * row(rr, t)
                ls = _lane_sum(v_t * row(bb, t), ones)
                emit_rows(t + 1, d_sn, dsa)
                d_sn = v_t - dsa * row(kap_, t + 1)
                dsa = ls - dsa * row(e1, t)
        emit_rows(0, d_sn, dsa)
        d_s = d_sn * row(ww, 0) - dsa * row(kap_, 0)
        ds_scr[...] = d_s
        dv_ref[...] = _steps_to_rows(dv_acc[...])
        state_rows()

    spec = pl.BlockSpec((C, RWKV_WIDTH), lambda i: (nc - 1 - i, 0))
    states = pl.BlockSpec((C, RWKV_HEAD, RWKV_WIDTH), lambda i: (nc - 1 - i, 0, 0))
    before = pl.BlockSpec((1, RWKV_HEAD, RWKV_WIDTH), lambda i: (jnp.maximum((nc - 1 - i) * C - 1, 0), 0, 0))
    return pl.pallas_call(
        body, name="wkv_bwd", grid=(nc,),
        in_specs=[spec] * 8 + [states, before],
        out_specs=[spec] * 6,
        out_shape=[jax.ShapeDtypeStruct((T, RWKV_WIDTH), F32)] * 6,
        scratch_shapes=[pltpu.VMEM((RWKV_HEAD, RWKV_WIDTH), F32)] * 3,
        compiler_params=_params(dimension_semantics=("arbitrary",)),
    )(r, w, k, v, kap, a, sa_rows, d_o, s_all, s_all)


W = RWKV_WIDTH


def _softplus(y):
    return jnp.maximum(y, 0.0) + jnp.log(1.0 + jnp.exp(-jnp.abs(y)))


def _prep_fn(kr, xwa, w0, a0, k_k, k_a, wup_pad, aup_pad, ones64):
    w_log = -_softplus(-(w0 + _mm_nt(jnp.tanh(xwa), wup_pad))) - 0.5
    decay = jnp.exp(-jnp.exp(w_log))
    a = jax.nn.sigmoid(a0 + _mm_nt(xwa, aup_pad))
    kk = kr * k_k
    kap = kk / jnp.maximum(jnp.sqrt(_head_mix(kk * kk, ones64)), 1e-12)
    k = kr * (1.0 + (a - 1.0) * k_a)
    return decay, k, kap, a


def _shift_down(p, first_row):
    rows = lax.broadcasted_iota(jnp.int32, p.shape, 0)
    return jnp.where(rows == 0, first_row, pltpu.roll(p, 1, axis=0))


def _shift_up(z, last_row):
    n = z.shape[0]
    rows = lax.broadcasted_iota(jnp.int32, z.shape, 0)
    return jnp.where(rows == n - 1, last_row, pltpu.roll(z, n - 1, axis=0))


def _prev_block_spec():
    return pl.BlockSpec((8, RWKV_COLS), lambda i: (jnp.maximum(i * (TOK_TILE // 8) - 1, 0), 0))


def _mixed(p_ref, prev8_ref, mu_ref, first_tile):
    p = p_ref[...]
    first_row = jnp.where(first_tile, 0.0, prev8_ref[7:8, :])
    prev = _shift_down(p, first_row)
    return p, prev, p + mu_ref[...] * (prev - p)


def _inproj_prep(x, norm_g, w_in_t, mu, w0, a0, k_k, k_a, wup_pad, aup_pad, ones64):
    T = x.shape[1]

    def body(x_ref, g_ref, w_in_ref, mu_ref, w0_ref, a0_ref, kk_ref, ka_ref, wup_ref, aup_ref, ones_ref,
             pr_ref, pw_ref, r_ref, w_ref, k_ref, v_ref, kap_ref, a_ref, grw_ref, last_scr):
        @pl.when(pl.program_id(0) == 0)
        def _():
            last_scr[...] = jnp.zeros_like(last_scr)

        ub = _rmsnorm(x_ref[...], g_ref[...]).astype(BF16)
        pr_ref[...] = _dot_nt_bf(ub, w_in_ref[:RET_COLS, :])
        p = _dot_nt_bf(ub, w_in_ref[RET_COLS:, :])
        pw_ref[...] = p
        ps = p + mu_ref[...] * (_shift_down(p, last_scr[7:8, :]) - p)
        last_scr[...] = p[TOK_TILE - 8:, :]
        decay, k, kap, a = _prep_fn(ps[:, W:2 * W], ps[:, 4 * W:], w0_ref[...], a0_ref[...], kk_ref[...], ka_ref[...],
                                    wup_ref[...], aup_ref[...], ones_ref[...])
        r_ref[...] = ps[:, 0:W]
        w_ref[...] = decay
        k_ref[...] = k
        v_ref[...] = ps[:, 2 * W:3 * W]
        kap_ref[...] = kap
        a_ref[...] = a
        grw_ref[...] = ps[:, 3 * W:4 * W]

    vec = _full((1, W))
    return pl.pallas_call(
        body, name="inproj_prep", grid=(T // TOK_TILE,),
        in_specs=[_rows_of_one(TOK_TILE, D_MODEL), _full((1, D_MODEL)), _full((IN_COLS, D_MODEL)), _full((1, RWKV_COLS)),
                  vec, vec, vec, vec, _full((W, 2 * LORA)), _full((W, 2 * LORA)), _full((256, 128))],
        out_specs=[_rows(TOK_TILE, RET_COLS), _rows(TOK_TILE, RWKV_COLS)] + [_rows(TOK_TILE, W)] * 7,
        out_shape=[jax.ShapeDtypeStruct((T, RET_COLS), F32), jax.ShapeDtypeStruct((T, RWKV_COLS), F32)]
        + [jax.ShapeDtypeStruct((T, W), F32)] * 7,
        scratch_shapes=[pltpu.VMEM((8, RWKV_COLS), F32)],
        compiler_params=_params(dimension_semantics=("arbitrary",)),
    )(x, norm_g, w_in_t, mu, w0, a0, k_k, k_a, wup_pad, aup_pad, ones64)


def _prep_bwd(p_rwkv, mu, w0, a0, k_k, k_a, wup_pad, aup_pad, ones64, dr, dw, dk, dv, dkap, da, dg, dr2, dk2, dv2):
    T = p_rwkv.shape[0]
    nt = T // TOK_TILE

    def body(p_ref, prev8_ref, mu_ref, w0_ref, a0_ref, kk_ref, ka_ref, wup_ref, aup_ref, ones_ref,
             dr_ref, dw_ref, dk_ref, dv_ref, dkap_ref, da_ref, dg_ref, dr2_ref, dk2_ref, dv2_ref,
             dp_ref, dmu_ref, dw0_ref, da0_ref, dkk_ref, dka_ref, dwup_ref, daup_ref, zrow_scr):
        i = pl.program_id(0)
        accs = (dmu_ref, dw0_ref, da0_ref, dkk_ref, dka_ref, dwup_ref, daup_ref)

        @pl.when(i == 0)
        def _():
            zrow_scr[...] = jnp.zeros_like(zrow_scr)
            for ref in accs:
                ref[...] = jnp.zeros_like(ref)

        p, prev, ps = _mixed(p_ref, prev8_ref, mu_ref, i == nt - 1)
        ones = ones_ref[...]
        _, vjp = jax.vjp(lambda *args: _prep_fn(*args, ones), ps[:, W:2 * W], ps[:, 4 * W:], w0_ref[...], a0_ref[...],
                         kk_ref[...], ka_ref[...], wup_ref[...], aup_ref[...])
        dkr, dxwa, dw0, da0, dkk, dka, dwup, daup = vjp(
            (dw_ref[...], dk_ref[...] + dk2_ref[...], dkap_ref[...], da_ref[...]))
        dps = jnp.concatenate([dr_ref[...] + dr2_ref[...], dkr, dv_ref[...] + dv2_ref[...], dg_ref[...], dxwa], axis=1)
        z = dps * mu_ref[...]
        dp_ref[...] = dps - z + _shift_up(z, zrow_scr[0:1, :])
        zrow_scr[0:1, :] = z[0:1, :]
        for ref, val in zip(accs, (_colsum(dps * (prev - p)), dw0, da0, dkk, dka, dwup, daup)):
            ref[...] += val

    rev = lambda i: (nt - 1 - i, 0)
    vec = _full((1, W))
    lora = _full((W, 2 * LORA))
    tile = pl.BlockSpec((TOK_TILE, W), rev)
    prev8 = pl.BlockSpec((8, RWKV_COLS), lambda i: (jnp.maximum((nt - 1 - i) * (TOK_TILE // 8) - 1, 0), 0))
    return pl.pallas_call(
        body, name="prep_bwd", grid=(nt,),
        in_specs=[pl.BlockSpec((TOK_TILE, RWKV_COLS), rev), prev8, _full((1, RWKV_COLS)), vec, vec, vec, vec, lora, lora,
                  _full((256, 128))] + [tile] * 10,
        out_specs=[pl.BlockSpec((TOK_TILE, RWKV_COLS), rev), _full((1, RWKV_COLS)), vec, vec, vec, vec, lora, lora],
        out_shape=[jax.ShapeDtypeStruct((T, RWKV_COLS), F32), jax.ShapeDtypeStruct((1, RWKV_COLS), F32)]
        + [jax.ShapeDtypeStruct((1, W), F32)] * 4 + [jax.ShapeDtypeStruct((W, 2 * LORA), F32)] * 2,
        scratch_shapes=[pltpu.VMEM((8, RWKV_COLS), F32)],
        compiler_params=_params(dimension_semantics=("arbitrary",)),
    )(p_rwkv, p_rwkv, mu, w0, a0, k_k, k_a, wup_pad, aup_pad, ones64, dr, dw, dk, dv, dkap, da, dg, dr2, dk2, dv2)


def _silu(x):
    return x * jax.nn.sigmoid(x)


def _post_y(o, r, k, v, g_rw, ret_raw, g_ret, ret_gn_g, gn_g, gn_b, r_k, avg128, avg64, ones64):
    xc = ret_raw - _head_mix(ret_raw, avg128)
    ret = xc * lax.rsqrt(_head_mix(xc * xc, avg128) + RET_GN_EPS)
    y_ret = _silu(g_ret) * (ret * ret_gn_g)
    oc = o - _head_mix(o, avg64)
    on = oc * lax.rsqrt(_head_mix(oc * oc, avg64) + RWKV_GN_EPS) * gn_g + gn_b
    bonus = _head_mix(r * k * r_k, ones64) * v
    y_rwkv = _silu(g_rw) * (on + bonus)
    return y_ret, y_rwkv


def _post_loss(h, final_g, target):
    err = _rmsnorm(h, final_g) - target
    return 0.5 * jnp.sum(jnp.mean(err * err, axis=-1))


def _post(o, r, k, v, g_rw, ret_raw, p_ret, x, target, ret_gn_g, gn_g, gn_b, r_k, final_g, w_out, avg128, avg64, ones64):
    T = x.shape[1]
    n_tok_out = 8

    def body(o_ref, r_ref, k_ref, v_ref, grw_ref, ret_ref, gret_ref, x_ref, tgt_ref, rg_ref, gg_ref, gb_ref, rk_ref, fg_ref,
             wo_ref, a128_ref, a64_ref, ones_ref, *outs):
        tok_outs, (dwo_ref, drg_ref, dgg_ref, dgb_ref, drk_ref, dfg_ref, loss_ref) = outs[:n_tok_out], outs[n_tok_out:]
        accs = (dwo_ref, drg_ref, dgg_ref, dgb_ref, drk_ref, dfg_ref, loss_ref)

        @pl.when(pl.program_id(0) == 0)
        def _():
            for ref in accs:
                ref[...] = jnp.zeros_like(ref)

        consts = (a128_ref[...], a64_ref[...], ones_ref[...])
        (y_ret, y_rwkv), vjp = jax.vjp(
            lambda *args: _post_y(*args, *consts), o_ref[...], r_ref[...], k_ref[...], v_ref[...], grw_ref[...], ret_ref[...],
            gret_ref[...], rg_ref[...], gg_ref[...], gb_ref[...], rk_ref[...])
        h = x_ref[...] + _dot_bf(y_ret, wo_ref[0:RET_WIDTH, :]) + _dot_bf(y_rwkv, wo_ref[RET_WIDTH:, :])
        loss, (dh, dfg) = jax.value_and_grad(_post_loss, argnums=(0, 1))(h, fg_ref[...], tgt_ref[...])
        dy_ret = _dot_nt_bf(dh, wo_ref[0:RET_WIDTH, :])
        dy_rwkv = _dot_nt_bf(dh, wo_ref[RET_WIDTH:, :])
        do, dr, dk, dv, dgrw, dret, dgret, drg, dgg, dgb, drk = vjp((dy_ret, dy_rwkv))
        for ref, val in zip(tok_outs, (dh, do, dr, dk, dv, dgrw, dret, dgret)):
            ref[...] = val
        dwo_ref[0:RET_WIDTH, :] += _dot_tn_bf(y_ret, dh)
        dwo_ref[RET_WIDTH:, :] += _dot_tn_bf(y_rwkv, dh)
        for ref, val in zip(accs[1:], (drg, dgg, dgb, drk, dfg, jnp.full((1, 128), loss, F32))):
            ref[...] += val

    tile = _rows(TOK_TILE, W)
    wide = _rows(TOK_TILE, D_MODEL)
    wide_of_one = _rows_of_one(TOK_TILE, D_MODEL)
    vec = _full((1, W))
    sq = _full((256, 128))
    return pl.pallas_call(
        body, name="post", grid=(T // TOK_TILE,),
        in_specs=[tile] * 6 + [pl.BlockSpec((TOK_TILE, W), lambda i: (i, 2)), wide_of_one, wide_of_one, vec, vec, vec, vec,
                               _full((1, D_MODEL)), _full((D_MODEL, D_MODEL)), sq, sq, sq],
        out_specs=[wide] + [tile] * 7 + [_full((D_MODEL, D_MODEL)), vec, vec, vec, vec, _full((1, D_MODEL)), _full((1, 128))],
        out_shape=[jax.ShapeDtypeStruct((T, D_MODEL), F32)] + [jax.ShapeDtypeStruct((T, W), F32)] * 7
        + [jax.ShapeDtypeStruct((D_MODEL, D_MODEL), F32)] + [jax.ShapeDtypeStruct((1, W), F32)] * 4
        + [jax.ShapeDtypeStruct((1, D_MODEL), F32), jax.ShapeDtypeStruct((1, 128), F32)],
        compiler_params=_params(dimension_semantics=("arbitrary",)),
    )(o, r, k, v, g_rw, ret_raw, p_ret, x, target, ret_gn_g, gn_g, gn_b, r_k, final_g, w_out, avg128, avg64, ones64)


def _inproj_bwd(x, norm_g, dp_qkv, dg_ret, dp_rwkv, dh, w_in_t):
    T = x.shape[1]
    widths = [dp.shape[1] for dp in (dp_qkv, dg_ret, dp_rwkv)]
    steps = T // TOK_TILE

    def body(x_ref, g_ref, dqkv_ref, dgret_ref, drwkv_ref, dh_ref, w_ref, dx_ref, dg_ref, mine_ref, sib_ref, acc_ref):
        @pl.when(pl.program_id(0) == 0)
        def _():
            dg_ref[...] = jnp.zeros_like(dg_ref)
            acc_ref[...] = jnp.zeros_like(acc_ref)

        u, vjp = jax.vjp(_rmsnorm, x_ref[...], g_ref[...])
        ub = u.astype(BF16)
        du, row = 0.0, 0
        for dp_ref, n in zip((dqkv_ref, dgret_ref, drwkv_ref), widths):
            dp = dp_ref[...].astype(BF16)
            du = du + _dot_bf(dp, w_ref[row:row + n, :])
            acc_ref[row:row + n, :] += _dot_tn_bf(dp, ub)
            row += n
        dx, dg = vjp(du)
        dx_ref[...] = dx + dh_ref[...]
        dg_ref[...] += dg

        @pl.when(pl.program_id(0) == steps - 1)
        def _():
            core = lax.axis_index("c")
            for dev in range(N_DEV):
                block = acc_ref[dev * SHARD_IN:(dev + 1) * SHARD_IN, :].astype(BF16)

                @pl.when(core == dev % 2)
                def _():
                    mine_ref[dev // 2] = block

                @pl.when(core != dev % 2)
                def _():
                    sib_ref[dev // 2] = block

    half = jax.ShapeDtypeStruct((N_DEV // 2, SHARD_IN, D_MODEL), BF16)
    once = lambda shape: pl.BlockSpec(shape, lambda i: (0,) * len(shape), pipeline_mode=pl.Buffered(1))
    return pl.pallas_call(
        body, name="inproj_bwd", grid=(steps,),
        in_specs=[_rows_of_one(TOK_TILE, D_MODEL), _full((1, D_MODEL))] + [_rows(TOK_TILE, n) for n in widths]
        + [_rows(TOK_TILE, D_MODEL), once((IN_COLS, D_MODEL))],
        out_specs=[_rows_of_one(TOK_TILE, D_MODEL), _full((1, D_MODEL)), once(half.shape), once(half.shape)],
        out_shape=[jax.ShapeDtypeStruct((1, T, D_MODEL), F32), jax.ShapeDtypeStruct((1, D_MODEL), F32), half, half],
        scratch_shapes=[pltpu.VMEM((IN_COLS, D_MODEL), F32)],
        compiler_params=_params(dimension_semantics=("arbitrary",)),
    )(x, norm_g, dp_qkv, dg_ret, dp_rwkv, dh, w_in_t)


def _pad_lora(w_up_t, first):
    z = jnp.zeros_like(w_up_t)
    return jnp.concatenate([w_up_t, z] if first else [z, w_up_t], axis=1)


def _local_grads(x, target, norm_g, w_in_t, ret_gn_g, mu, w_lora_up_t, w0, a_lora_up_t, a0, k_k, k_a, r_k, gn_g, gn_b,
                 w_out_bf, final_g):
    T = x.shape[1]
    tabs = _rope_tables(T) + _ret_tables()
    ones64 = _block_mix(128, RWKV_HEAD)
    avg64 = _block_mix(128, RWKV_HEAD, 1.0 / RWKV_HEAD)
    avg128 = _block_mix(128, RET_DV, 1.0 / RET_DV)
    wup_pad, aup_pad = _pad_lora(w_lora_up_t, True), _pad_lora(a_lora_up_t, False)

    p_ret, p_rwkv, r, w, k, v, kap, a, g_rw = _inproj_prep(x, norm_g, w_in_t, mu, w0, a0, k_k, k_a, wup_pad, aup_pad, ones64)
    ret_raw, s_saved = _ret_fwd(p_ret, tabs)
    o, s_all, sa_rows = _wkv_fwd(r, w, k, v, kap, a)
    (dh, do, dr2, dk2, dv2, dgrw, dret, dgret, d_w_out, d_ret_gn_g, d_gn_g, d_gn_b, d_r_k, d_final_g, loss) = _post(
        o, r, k, v, g_rw, ret_raw, p_ret, x, target, ret_gn_g, gn_g, gn_b, r_k, final_g, w_out_bf, avg128, avg64, ones64)
    dr, dw, dk, dv, dkap, da = _wkv_bwd(r, w, k, v, kap, a, s_all, sa_rows, do)
    dp_rwkv, d_mu, d_w0, d_a0, d_k_k, d_k_a, d_wup, d_aup = _prep_bwd(
        p_rwkv, mu, w0, a0, k_k, k_a, wup_pad, aup_pad, ones64, dr, dw, dk, dv, dkap, da, dgrw, dr2, dk2, dv2)
    dp_qkv = _ret_bwd(p_ret, s_saved, dret, tabs)
    dx, d_norm_g, *d_w_in = _inproj_bwd(x, norm_g, dp_qkv, dgret, dp_rwkv, dh, w_in_t)
    grads = dict(norm_g=d_norm_g, w_in=d_w_in, ret_gn_g=d_ret_gn_g, rwkv_mu=d_mu, w_lora_up=d_wup[:, :LORA], w0=d_w0,
                 a_lora_up=d_aup[:, LORA:], a0=d_a0, k_k=d_k_k, k_a=d_k_a, r_k=d_r_k, rwkv_gn_g=d_gn_g, rwkv_gn_b=d_gn_b,
                 w_out=d_w_out, final_norm_g=d_final_g)
    return loss, dx, grads


def _mesh_pos():
    return lax.axis_index("x"), lax.axis_index("y"), lax.axis_index("c")


def _all_gather(shards):
    n = len(shards)

    def body(*refs):
        x_refs, out_refs = refs[:n], refs[n:2 * n]
        send_sems, recv_sems, local_sems = refs[2 * n:]
        x, y, c = _mesh_pos()
        me, sibling = (x, y, c), (x, y, 1 - c)
        chips = [(1 - x, y), (x, 1 - y), (1 - x, 1 - y)]

        def rows(a, pos):
            m = x_refs[a].shape[0]
            return out_refs[a].at[pl.ds((4 * pos[0] + 2 * pos[1] + pos[2]) * m, m), :]

        def copy(a, k, block, to, src=None):
            return pltpu.make_async_remote_copy(
                src_ref=rows(a, block) if src is None else src, dst_ref=rows(a, block),
                send_sem=send_sems.at[a, k], recv_sem=recv_sems.at[a, k], device_id=to, device_id_type=MESH)

        mine = [pltpu.make_async_copy(x_refs[a], rows(a, me), local_sems.at[a]) for a in range(n)]
        for cp in mine:
            cp.start()
        first = []
        for a in range(n):
            first.append(copy(a, 0, me, sibling, src=x_refs[a]))
            first += [copy(a, 1 + j, me, (*chip, c), src=x_refs[a]) for j, chip in enumerate(chips)]
        for cp in first:
            cp.start()
        passed = []
        for j, chip in enumerate(chips):
            for a in range(n):
                copy(a, 1 + j, (*chip, c), me).wait_recv()
                passed.append(copy(a, 4 + j, (*chip, c), sibling))
                passed[-1].start()
        for a in range(n):
            copy(a, 0, sibling, me).wait_recv()
            for j, chip in enumerate(chips):
                copy(a, 4 + j, (*chip, 1 - c), me).wait_recv()
        for cp in first + passed:
            cp.wait_send()
        for cp in mine:
            cp.wait()

    vmem = pl.BlockSpec(memory_space=pltpu.VMEM)
    return pl.pallas_call(
        body, name="gather_weights",
        out_shape=[jax.ShapeDtypeStruct((N_DEV * s.shape[0], s.shape[1]), s.dtype) for s in shards],
        in_specs=[vmem] * n, out_specs=[vmem] * n,
        scratch_shapes=[pltpu.SemaphoreType.DMA((n, 7)), pltpu.SemaphoreType.DMA((n, 7)), pltpu.SemaphoreType.DMA((n,))],
        compiler_params=_params(),
    )(*shards)


N_CHIP = 4


def _exchange_pairs(big, small):
    nb, ns = len(big), len(small)

    def body(*refs):
        big_in, small_in = refs[:nb], refs[nb:nb + ns]
        theirs, small_out = refs[nb + ns:2 * nb + ns], refs[2 * nb + ns:2 * nb + 2 * ns]
        pair_send, pair_recv, send_sems, recv_sems, local_sems = refs[2 * nb + 2 * ns:]
        x, y, c = _mesh_pos()
        me = 4 * x + 2 * y + c
        local = [pltpu.make_async_copy(small_in[a].at[me], small_out[a].at[me], local_sems.at[a]) for a in range(ns)]
        for cp in local:
            cp.start()
        copies = [pltpu.make_async_remote_copy(
            src_ref=big_in[a], dst_ref=theirs[a], send_sem=pair_send.at[a], recv_sem=pair_recv.at[a],
            device_id=(x, y, 1 - c), device_id_type=MESH) for a in range(nb)]
        for k in range(1, N_DEV):
            peer = (x ^ (k >> 2), y ^ ((k >> 1) & 1), c ^ (k & 1))
            peer_idx = 4 * peer[0] + 2 * peer[1] + peer[2]
            copies += [pltpu.make_async_remote_copy(
                src_ref=small_in[a].at[peer_idx], dst_ref=small_out[a].at[me], send_sem=send_sems.at[a, k - 1],
                recv_sem=recv_sems.at[a, k - 1], device_id=peer, device_id_type=MESH) for a in range(ns)]
        for cp in copies:
            cp.start()
        for cp in copies:
            cp.wait()
        for cp in local:
            cp.wait()

    hbm = pl.BlockSpec(memory_space=pl.ANY)
    out_shape = [jax.ShapeDtypeStruct(p.shape, p.dtype) for p in big + small]
    dma = pltpu.SemaphoreType.DMA
    res = pl.pallas_call(
        body, name="exchange_pairs", out_shape=out_shape,
        in_specs=[hbm] * (nb + ns), out_specs=[hbm] * len(out_shape),
        scratch_shapes=[dma((nb,)), dma((nb,)), dma((ns, 7)), dma((ns, 7)), dma((ns,))],
        compiler_params=_params(),
    )(*big, *small)
    return res[:nb], res[nb:]


def _pair_sum(name, mine, theirs, row_tile):
    _, rows, cols = mine.shape

    def body(a_ref, b_ref, o_ref):
        o_ref[...] = (a_ref[...].astype(F32) + b_ref[...].astype(F32)).astype(o_ref.dtype)

    spec = pl.BlockSpec((N_CHIP, row_tile, cols), lambda i: (0, i, 0))
    return pl.pallas_call(
        body, name=name, grid=(rows // row_tile,), in_specs=[spec, spec], out_specs=spec,
        out_shape=jax.ShapeDtypeStruct(mine.shape, mine.dtype),
        compiler_params=_params(dimension_semantics=("arbitrary",)),
    )(mine, theirs)


def _exchange_chips(parts):
    n = len(parts)

    def body(*refs):
        in_refs, out_refs = refs[:n], refs[n:2 * n]
        send_sems, recv_sems, local_sems = refs[2 * n:]
        x, y, c = _mesh_pos()
        my_chip = 2 * x + y
        own = [pltpu.make_async_copy(in_refs[a].at[my_chip], out_refs[a].at[my_chip], local_sems.at[a]) for a in range(n)]
        for cp in own:
            cp.start()
        copies = []
        for k in range(1, N_CHIP):
            px, py = x ^ (k >> 1), y ^ (k & 1)
            copies += [pltpu.make_async_remote_copy(
                src_ref=in_refs[a].at[2 * px + py], dst_ref=out_refs[a].at[my_chip], send_sem=send_sems.at[a, k - 1],
                recv_sem=recv_sems.at[a, k - 1], device_id=(px, py, c), device_id_type=MESH) for a in range(n)]
        for cp in copies:
            cp.start()
        for cp in copies:
            cp.wait()
        for cp in own:
            cp.wait()

    hbm = pl.BlockSpec(memory_space=pl.ANY)
    dma = pltpu.SemaphoreType.DMA
    return pl.pallas_call(
        body, name="exchange_chips",
        out_shape=[jax.ShapeDtypeStruct(p.shape, p.dtype) for p in parts],
        in_specs=[hbm] * n, out_specs=[hbm] * n,
        scratch_shapes=[dma((n, N_CHIP - 1)), dma((n, N_CHIP - 1)), dma((n,))],
        compiler_params=_params(),
    )(*parts)


def _adamw(w, g, m, v):
    m = ADAM_B1 * m + (1.0 - ADAM_B1) * g
    v = ADAM_B2 * v + (1.0 - ADAM_B2) * (g * g)
    m_hat = m / (1.0 - ADAM_B1 ** ADAM_STEP)
    v_hat = v / (1.0 - ADAM_B2 ** ADAM_STEP)
    return -ADAM_LR * (m_hat / (jnp.sqrt(v_hat) + ADAM_EPS) + ADAM_WD * w), m, v


def _sum_parts(name, parts, row_tile):
    n_parts, rows, cols = parts.shape

    def body(p_ref, g_ref):
        g = p_ref[0].astype(F32)
        for s in range(1, n_parts):
            g = g + p_ref[s].astype(F32)
        g_ref[...] = g

    return pl.pallas_call(
        body, name=name, grid=(rows // row_tile,),
        in_specs=[pl.BlockSpec((n_parts, row_tile, cols), lambda i: (0, i, 0))],
        out_specs=pl.BlockSpec((row_tile, cols), lambda i: (i, 0)),
        out_shape=jax.ShapeDtypeStruct((rows, cols), F32),
        compiler_params=_params(dimension_semantics=("arbitrary",)),
    )(parts)


def _adamw_apply(name, g, w, m, v, row_tile):
    _, rows, cols = w.shape

    def body(g_ref, w_ref, m_ref, v_ref, d_ref, nm_ref, nv_ref):
        d_ref[0], nm_ref[0], nv_ref[0] = _adamw(w_ref[0], g_ref[0], m_ref[0], v_ref[0])

    tile = pl.BlockSpec((1, row_tile, cols), lambda i: (0, i, 0))
    return pl.pallas_call(
        body, name=name, grid=(rows // row_tile,), in_specs=[tile] * 4, out_specs=[tile] * 3,
        out_shape=[jax.ShapeDtypeStruct((1, rows, cols), F32)] * 3,
        compiler_params=_params(dimension_semantics=("arbitrary",)),
    )(g, w, m, v)


def _reduce_adamw_2d(name, parts, w, m, v, row_tile):
    n_parts, rows, cols = parts.shape

    def body(p_ref, w_ref, m_ref, v_ref, g_ref, d_ref, nm_ref, nv_ref):
        g = p_ref[0].astype(F32)
        for s in range(1, n_parts):
            g = g + p_ref[s].astype(F32)
        g_ref[...] = g
        d_ref[...], nm_ref[...], nv_ref[...] = _adamw(w_ref[...], g, m_ref[...], v_ref[...])

    tile = pl.BlockSpec((row_tile, cols), lambda i: (i, 0))
    return pl.pallas_call(
        body, name=name, grid=(rows // row_tile,),
        in_specs=[pl.BlockSpec((n_parts, row_tile, cols), lambda i: (0, i, 0)), tile, tile, tile],
        out_specs=[tile] * 4,
        out_shape=[jax.ShapeDtypeStruct((rows, cols), F32)] * 4,
        compiler_params=_params(dimension_semantics=("arbitrary",)),
    )(parts, w, m, v)


def _reduce_adamw_t(name, parts_t, w, m, v, sum_tile, row_tile):
    g = _sum_parts(name + "_sum", parts_t, sum_tile).T[None]
    return [g] + list(_adamw_apply(name, g, w, m, v, row_tile))


def _reduce_adamw(name, parts, w, m, v, row_tile):
    n_parts, rows, cols = parts.shape

    def body(p_ref, w_ref, m_ref, v_ref, g_ref, d_ref, nm_ref, nv_ref):
        g = p_ref[0].astype(F32)
        for s in range(1, n_parts):
            g = g + p_ref[s].astype(F32)
        g_ref[0] = g
        d_ref[0], nm_ref[0], nv_ref[0] = _adamw(w_ref[0], g, m_ref[0], v_ref[0])

    tile = pl.BlockSpec((1, row_tile, cols), lambda i: (0, i, 0))
    return pl.pallas_call(
        body, name=name, grid=(rows // row_tile,),
        in_specs=[pl.BlockSpec((n_parts, row_tile, cols), lambda i: (0, i, 0)), tile, tile, tile],
        out_specs=[tile] * 4,
        out_shape=[jax.ShapeDtypeStruct((1, rows, cols), F32)] * 4,
        compiler_params=_params(dimension_semantics=("arbitrary",)),
    )(parts, w, m, v)


_SMALL = (("norm_g", 1024), ("ret_gn_g", 512), ("rwkv_mu", 2176), ("w0", 512), ("a0", 512), ("k_k", 512), ("k_a", 512),
          ("r_k", 512), ("rwkv_gn_g", 512), ("rwkv_gn_b", 512), ("final_norm_g", 1024))
_SMALL_LANES = sum(n for _, n in _SMALL) + 128
_WEIGHTS = ("norm_g", "w_in", "ret_gn_g", "rwkv_mu", "w_lora_up", "w0", "a_lora_up", "a0", "k_k", "k_a", "r_k", "rwkv_gn_g",
            "rwkv_gn_b", "w_out", "final_norm_g")


def _adamw_vectors(parts, wts, mom, var):
    k = len(_SMALL)

    def body(p_ref, *refs):
        w_refs, m_refs, v_refs, outs = refs[:k], refs[k:2 * k], refs[2 * k:3 * k], refs[3 * k:]
        g_all = p_ref[0]
        for s in range(1, N_DEV):
            g_all = g_all + p_ref[s]
        off = 0
        for i, (name, n) in enumerate(_SMALL):
            g = g_all[:, off:off + n]
            off += n
            if name == "r_k":
                g = jnp.concatenate([g[:, RWKV_HEAD * h:RWKV_HEAD * (h + 1)] for h in range(RWKV_HEADS)], axis=0)[None]
            outs[4 * i][...] = g
            outs[4 * i + 1][...], outs[4 * i + 2][...], outs[4 * i + 3][...] = _adamw(
                w_refs[i][...], g, m_refs[i][...], v_refs[i][...])
        outs[4 * k][...] = g_all[:, off:off + 128]

    vmem = pl.BlockSpec(memory_space=pltpu.VMEM)
    shapes = [jax.ShapeDtypeStruct(wts[n].shape, F32) for n, _ in _SMALL for _ in range(4)] + [jax.ShapeDtypeStruct((1, 128), F32)]
    res = pl.pallas_call(
        body, name="adamw_vectors", out_shape=shapes,
        in_specs=[vmem] * (1 + 3 * k), out_specs=[vmem] * len(shapes), compiler_params=_params(),
    )(parts, *[wts[n] for n, _ in _SMALL], *[mom[n] for n, _ in _SMALL], *[var[n] for n, _ in _SMALL])
    return {n: res[4 * i:4 * i + 4] for i, (n, _) in enumerate(_SMALL)}, res[4 * k]


def kernel(x, norm_g, w_in, ret_gn_g, rwkv_mu, w_lora_up, w0, a_lora_up, a0, k_k, k_a, r_k, rwkv_gn_g, rwkv_gn_b, w_out, final_norm_g, loss_target, m_norm_g, m_w_in, m_ret_gn_g, m_rwkv_mu, m_w_lora_up, m_w0, m_a_lora_up, m_a0, m_k_k, m_k_a, m_r_k, m_rwkv_gn_g, m_rwkv_gn_b, m_w_out, m_final_norm_g, v_norm_g, v_w_in, v_ret_gn_g, v_rwkv_mu, v_w_lora_up, v_w0, v_a_lora_up, v_a0, v_k_k, v_k_a, v_r_k, v_rwkv_gn_g, v_rwkv_gn_b, v_w_out, v_final_norm_g):
    wts = dict(norm_g=norm_g, w_in=w_in, ret_gn_g=ret_gn_g, rwkv_mu=rwkv_mu, w_lora_up=w_lora_up, w0=w0, a_lora_up=a_lora_up,
               a0=a0, k_k=k_k, k_a=k_a, r_k=r_k, rwkv_gn_g=rwkv_gn_g, rwkv_gn_b=rwkv_gn_b, w_out=w_out,
               final_norm_g=final_norm_g)
    mom = dict(norm_g=m_norm_g, w_in=m_w_in, ret_gn_g=m_ret_gn_g, rwkv_mu=m_rwkv_mu, w_lora_up=m_w_lora_up, w0=m_w0,
               a_lora_up=m_a_lora_up, a0=m_a0, k_k=m_k_k, k_a=m_k_a, r_k=m_r_k, rwkv_gn_g=m_rwkv_gn_g,
               rwkv_gn_b=m_rwkv_gn_b, w_out=m_w_out, final_norm_g=m_final_norm_g)
    var = dict(norm_g=v_norm_g, w_in=v_w_in, ret_gn_g=v_ret_gn_g, rwkv_mu=v_rwkv_mu, w_lora_up=v_w_lora_up, w0=v_w0,
               a_lora_up=v_a_lora_up, a0=v_a0, k_k=v_k_k, k_a=v_k_a, r_k=v_r_k, rwkv_gn_g=v_rwkv_gn_g,
               rwkv_gn_b=v_rwkv_gn_b, w_out=v_w_out, final_norm_g=v_final_norm_g)
    shapes = {n: wts[n].shape for n in _WEIGHTS}

    w_in_t, w_out_bf, wup_t, aup_t = _all_gather(
        [w_in[0].T.astype(BF16), w_out[0].astype(BF16), w_lora_up[0].T, a_lora_up[0].T])

    loss, dx, g = _local_grads(
        x, loss_target, norm_g, w_in_t, ret_gn_g, rwkv_mu, wup_t, w0, aup_t, a0, k_k, k_a,
        r_k.reshape(1, W), rwkv_gn_g, rwkv_gn_b, w_out_bf, final_norm_g.reshape(1, D_MODEL))

    small = jnp.concatenate([g[n] for n, _ in _SMALL] + [loss], axis=1)
    core = lax.axis_index("c")
    by_core = lambda t: [lax.dynamic_index_in_dim(t, i, axis=1, keepdims=False) for i in (core, 1 - core)]
    in_mine, in_sib = g["w_in"]
    out_mine, out_sib = by_core(g["w_out"].reshape(N_CHIP, 2, SHARD_OUT, D_MODEL).astype(BF16))
    (in_theirs, out_theirs), parts = _exchange_pairs(
        [in_sib, out_sib],
        [g["w_lora_up"].reshape(N_DEV, SHARD_LORA, LORA), g["a_lora_up"].reshape(N_DEV, SHARD_LORA, LORA),
         jnp.broadcast_to(small[None], (N_DEV, 1, _SMALL_LANES))])
    by_chip = _exchange_chips([_pair_sum("pair_sum_w_in", in_mine, in_theirs, SHARD_IN // 2),
                               _pair_sum("pair_sum_w_out", out_mine, out_theirs, SHARD_OUT)])
    res = {}
    res["w_in"] = [t.T[None] for t in _reduce_adamw_2d(
        "adamw_w_in", by_chip[0], w_in[0].T, m_w_in[0].T, v_w_in[0].T, SHARD_IN // 2)]
    res["w_out"] = _reduce_adamw("adamw_w_out", by_chip[1], w_out, m_w_out, v_w_out, SHARD_OUT)
    res["w_lora_up"] = _reduce_adamw_t("adamw_w_lora_up", parts[0], w_lora_up, m_w_lora_up, v_w_lora_up, LORA, LORA)
    res["a_lora_up"] = _reduce_adamw_t("adamw_a_lora_up", parts[1], a_lora_up, m_a_lora_up, v_a_lora_up, LORA, LORA)
    as_row = lambda d: {n: d[n] if d[n].ndim > 1 else d[n].reshape(1, size) for n, size in _SMALL}
    vec, loss_row = _adamw_vectors(parts[2], as_row(wts), as_row(mom), as_row(var))
    res.update(vec)
    res = {n: [t.reshape(shapes[n]) for t in res[n]] for n in _WEIGHTS}
    return (loss_row[0, 0], dx, *[res[n][0] for n in _WEIGHTS], *[res[n][1] for n in _WEIGHTS],
            *[res[n][2] for n in _WEIGHTS], *[res[n][3] for n in _WEIGHTS])
```

```python
import numpy as np
import jax
import jax.numpy as jnp
from jax import lax
from jax.experimental import pallas as pl
from jax.experimental.pallas import tpu as pltpu

F32 = jnp.float32
BF16 = jnp.bfloat16

D_MODEL = 1024
CHUNK = 64
RET_HEADS = 4
RET_DV = 128
RET_DK = 64
RET_QK = 256
RET_WIDTH = 512
RWKV_WIDTH = 512
RWKV_HEAD = 64
RWKV_HEADS = 8
LORA = 64
RET_COLS = 2 * RET_QK + 2 * RET_WIDTH
RWKV_COLS = 4 * RWKV_WIDTH + 2 * LORA
IN_COLS = RET_COLS + RWKV_COLS
ROPE_BASE = 10000.0
RMS_EPS = 1e-6
RET_GN_EPS = 1e-5
RWKV_GN_EPS = 64e-5
ADAM_LR = 0.001
ADAM_B1 = 0.9
ADAM_B2 = 0.999
ADAM_EPS = 1e-08
ADAM_WD = 0.01
ADAM_STEP = 10
N_DEV = 8
SHARD_IN = IN_COLS // N_DEV
SHARD_OUT = D_MODEL // N_DEV
SHARD_LORA = RWKV_WIDTH // N_DEV
VMEM_LIMIT = 56 * 1024 * 1024
TOK_TILE = 256
WKV_CHUNK = 64

MESH = pl.DeviceIdType.MESH


def _dot_bf(a, b):
    return jnp.dot(a.astype(BF16), b.astype(BF16), preferred_element_type=F32)


def _dot_nt_bf(a, b):
    return lax.dot_general(a.astype(BF16), b.astype(BF16), (((1,), (1,)), ((), ())), preferred_element_type=F32)


def _dot_tn_bf(a, b):
    return lax.dot_general(a.astype(BF16), b.astype(BF16), (((0,), (0,)), ((), ())), preferred_element_type=F32)


@jax.custom_vjp
def _mm(a, b):
    return _dot_bf(a, b)


@jax.custom_vjp
def _mm_nt(a, b):
    return _dot_nt_bf(a, b)


@jax.custom_vjp
def _mm_tn(a, b):
    return _dot_tn_bf(a, b)


_mm.defvjp(lambda a, b: (_dot_bf(a, b), (a, b)), lambda res, g: (_dot_nt_bf(g, res[1]), _dot_tn_bf(res[0], g)))
_mm_nt.defvjp(lambda a, b: (_dot_nt_bf(a, b), (a, b)), lambda res, g: (_dot_bf(g, res[1]), _dot_tn_bf(g, res[0])))
_mm_tn.defvjp(lambda a, b: (_dot_tn_bf(a, b), (a, b)), lambda res, g: (_dot_nt_bf(res[1], g), _dot_bf(res[0], g)))


def _trunc(x):
    return lax.bitcast_convert_type(lax.bitcast_convert_type(x, jnp.uint32) & jnp.uint32(0xFFFF0000), F32)


def _two_piece(x):
    hi = _trunc(x)
    return jnp.concatenate([hi, x - hi], axis=1)


def _mix_raw(x, mat2):
    return _unstack(jnp.dot(_two_piece(_stack(x)), mat2, preferred_element_type=F32))


@jax.custom_vjp
def _head_mix(x, mat2):
    return _mix_raw(x, mat2)


_head_mix.defvjp(lambda x, mat2: (_mix_raw(x, mat2), mat2), lambda mat2, g: (_mix_raw(g, mat2), jnp.zeros_like(mat2)))


def _swap_halves(x):
    lane = lax.broadcasted_iota(jnp.int32, x.shape, 1)
    return jnp.where((lane & (RET_DK - 1)) < RET_DK // 2, pltpu.roll(x, RET_QK - RET_DK // 2, axis=1),
                     pltpu.roll(x, RET_DK // 2, axis=1))


@jax.custom_vjp
def _rot(x):
    return _swap_halves(x)


_rot.defvjp(lambda x: (_swap_halves(x), None), lambda _, g: (_swap_halves(g),))


def _params(**kw):
    return pltpu.CompilerParams(vmem_limit_bytes=VMEM_LIMIT, **kw)


def _full(shape):
    nd = len(shape)
    return pl.BlockSpec(shape, lambda i, _nd=nd: (0,) * _nd)


def _rows(tile, width):
    return pl.BlockSpec((tile, width), lambda i: (i, 0))


def _rows_of_one(tile, width):
    return pl.BlockSpec((None, tile, width), lambda i: (0, i, 0))


def _block_mix(n, blk, scale=1.0):
    idx = np.arange(n) // blk
    m = (idx[:, None] == idx[None, :]).astype(np.float32) * scale
    return jnp.asarray(np.concatenate([m, m], axis=0))


def _rope_tables(T):
    half = RET_DK // 2
    expo = -np.arange(half, dtype=np.float32) / np.float32(half)
    freqs = np.exp(expo * np.float32(np.log(ROPE_BASE))).astype(np.float32)
    ang = np.arange(T, dtype=np.float32)[:, None] * freqs[None, :]
    cos, sin = np.cos(ang).astype(np.float32), np.sin(ang).astype(np.float32)
    cos_h = np.concatenate([cos, cos], axis=1)
    sin_h = np.concatenate([-sin, sin], axis=1)
    cos_t = np.tile(cos_h, (1, RET_HEADS))
    sin_t = np.tile(sin_h, (1, RET_HEADS))
    return jnp.asarray(cos_t), jnp.asarray(sin_t)


def _ret_tables():
    h = np.arange(RET_HEADS, dtype=np.float32)
    lg = np.log(1.0 - np.exp2(-5.0 - h)).astype(np.float32)
    idx = np.arange(CHUNK, dtype=np.float32)
    intra = np.exp(lg[:, None, None] * np.abs(idx[:, None] - idx[None, :])).astype(np.float32)
    q_dec = np.exp(lg[:, None] * (idx[None, :] + 1.0)).astype(np.float32)
    k_dec = np.exp(lg[:, None] * (CHUNK - 1.0 - idx[None, :])).astype(np.float32)
    chunk_dec = np.exp(lg * CHUNK).astype(np.float32)
    lane_head = np.arange(RET_QK) // RET_DK
    mask = (lane_head[None, :] == np.arange(RET_HEADS)[:, None]).astype(np.float32)
    m = np.broadcast_to(mask[:, None, :], (RET_HEADS, CHUNK, RET_QK)).copy()
    qd = m * q_dec[:, :, None]
    kd = m * k_dec[:, :, None]
    return jnp.asarray(intra), jnp.asarray(m), jnp.asarray(qd), jnp.asarray(kd), [float(c) for c in chunk_dec]


def _rmsnorm(x, g):
    return x * lax.rsqrt(jnp.mean(x * x, axis=-1, keepdims=True) + RMS_EPS) * g


def _ret_chunk(pq, pk, v_heads, s_heads, cos_t, sin_t, dec, hm, qd, kd, chunk_dec):
    q = pq * cos_t + _rot(pq) * sin_t
    k = (pk * cos_t + _rot(pk) * sin_t) * (RET_DK ** -0.5)
    outs, s_out = [], []
    for h in range(RET_HEADS):
        sc = _mm_nt(q * hm[h], k * hm[h]) * dec[h]
        intra = _mm(sc, v_heads[h])
        kv = _mm_tn(k * kd[h], v_heads[h])
        inter = _mm(q * qd[h], s_heads[h])
        outs.append(intra + inter)
        s_out.append(s_heads[h] * chunk_dec[h] + kv)
    return tuple(outs), tuple(s_out)


def _ret_specs():
    const = [_full((RET_HEADS, CHUNK, CHUNK)), _full((RET_HEADS, CHUNK, RET_QK)),
             _full((RET_HEADS, CHUNK, RET_QK)), _full((RET_HEADS, CHUNK, RET_QK))]
    return const


RET_GROUP = 8


def _ret_fwd(p_ret, tabs):
    T = p_ret.shape[0]
    G = RET_GROUP
    ng = T // (CHUNK * G)
    cos_t, sin_t, dec, hm, qd, kd, chunk_dec = tabs

    def body(p_ref, cos_ref, sin_ref, dec_ref, hm_ref, qd_ref, kd_ref, out_ref, sin_save_ref, s_scr):
        @pl.when(pl.program_id(0) == 0)
        def _():
            s_scr[...] = jnp.zeros_like(s_scr)

        consts = (dec_ref[...], hm_ref[...], qd_ref[...], kd_ref[...])
        s_heads = tuple(s_scr[h] for h in range(RET_HEADS))
        for c in range(G):
            rows = pl.ds(c * CHUNK, CHUNK)
            for h in range(RET_HEADS):
                sin_save_ref[c, h] = s_heads[h]
            v_heads = tuple(p_ref[rows, 2 * RET_QK + RET_DV * h:2 * RET_QK + RET_DV * (h + 1)] for h in range(RET_HEADS))
            outs, s_heads = _ret_chunk(p_ref[rows, 0:RET_QK], p_ref[rows, RET_QK:2 * RET_QK], v_heads, s_heads,
                                       cos_ref[rows, :], sin_ref[rows, :], *consts, chunk_dec)
            for h in range(RET_HEADS):
                out_ref[rows, RET_DV * h:RET_DV * (h + 1)] = outs[h]
        for h in range(RET_HEADS):
            s_scr[h] = s_heads[h]

    tok = CHUNK * G
    return pl.pallas_call(
        body, name="ret_fwd", grid=(ng,),
        in_specs=[pl.BlockSpec((tok, RET_COLS), lambda i: (i, 0)), _rows(tok, RET_QK), _rows(tok, RET_QK)] + _ret_specs(),
        out_specs=[_rows(tok, RET_WIDTH), pl.BlockSpec((G, RET_HEADS, RET_QK, RET_DV), lambda i: (i, 0, 0, 0))],
        out_shape=[jax.ShapeDtypeStruct((T, RET_WIDTH), F32),
                   jax.ShapeDtypeStruct((T // CHUNK, RET_HEADS, RET_QK, RET_DV), F32)],
        scratch_shapes=[pltpu.VMEM((RET_HEADS, RET_QK, RET_DV), F32)],
        compiler_params=_params(dimension_semantics=("arbitrary",)),
    )(p_ret, cos_t, sin_t, dec, hm, qd, kd)


def _ret_bwd(p_ret, s_saved, d_ret, tabs):
    T = p_ret.shape[0]
    G = RET_GROUP
    ng = T // (CHUNK * G)
    cos_t, sin_t, dec, hm, qd, kd, chunk_dec = tabs

    def body(p_ref, s_ref, dret_ref, cos_ref, sin_ref, dec_ref, hm_ref, qd_ref, kd_ref, dp_ref, ds_scr):
        @pl.when(pl.program_id(0) == 0)
        def _():
            ds_scr[...] = jnp.zeros_like(ds_scr)

        consts = (dec_ref[...], hm_ref[...], qd_ref[...], kd_ref[...])
        d_s = tuple(ds_scr[h] for h in range(RET_HEADS))
        for c in reversed(range(G)):
            rows = pl.ds(c * CHUNK, CHUNK)
            v_heads = tuple(p_ref[rows, 2 * RET_QK + RET_DV * h:2 * RET_QK + RET_DV * (h + 1)] for h in range(RET_HEADS))
            s_heads = tuple(s_ref[c, h] for h in range(RET_HEADS))
            tables = (cos_ref[rows, :], sin_ref[rows, :]) + consts
            _, vjp = jax.vjp(lambda a, b, c_, d: _ret_chunk(a, b, c_, d, *tables, chunk_dec),
                             p_ref[rows, 0:RET_QK], p_ref[rows, RET_QK:2 * RET_QK], v_heads, s_heads)
            d_out = tuple(dret_ref[rows, RET_DV * h:RET_DV * (h + 1)] for h in range(RET_HEADS))
            dq, dk, dv, d_s = vjp((d_out, d_s))
            dp_ref[rows, 0:RET_QK] = dq
            dp_ref[rows, RET_QK:2 * RET_QK] = dk
            for h in range(RET_HEADS):
                dp_ref[rows, 2 * RET_QK + RET_DV * h:2 * RET_QK + RET_DV * (h + 1)] = dv[h]
        for h in range(RET_HEADS):
            ds_scr[h] = d_s[h]

    tok = CHUNK * G
    rev = lambda i: (ng - 1 - i, 0)
    return pl.pallas_call(
        body, name="ret_bwd", grid=(ng,),
        in_specs=[pl.BlockSpec((tok, RET_COLS), rev),
                  pl.BlockSpec((G, RET_HEADS, RET_QK, RET_DV), lambda i: (ng - 1 - i, 0, 0, 0)),
                  pl.BlockSpec((tok, RET_WIDTH), rev), pl.BlockSpec((tok, RET_QK), rev), pl.BlockSpec((tok, RET_QK), rev)]
        + _ret_specs(),
        out_specs=pl.BlockSpec((tok, 2 * RET_QK + RET_WIDTH), rev),
        out_shape=jax.ShapeDtypeStruct((T, 2 * RET_QK + RET_WIDTH), F32),
        scratch_shapes=[pltpu.VMEM((RET_HEADS, RET_QK, RET_DV), F32)],
        compiler_params=_params(dimension_semantics=("arbitrary",)),
    )(p_ret, s_saved, d_ret, cos_t, sin_t, dec, hm, qd, kd)


def _block_ones():
    r = lax.broadcasted_iota(jnp.int32, (3 * 128, 128), 0)
    c = lax.broadcasted_iota(jnp.int32, (3 * 128, 128), 1)
    return (((r & 127) >> 6) == (c >> 6)).astype(BF16)


def _stack(x):
    return jnp.concatenate([x[:, 128 * p:128 * (p + 1)] for p in range(4)], axis=0)


def _unstack(y):
    n = y.shape[0] // 4
    return jnp.concatenate([y[n * p:n * (p + 1)] for p in range(4)], axis=1)


def _split(x, n):
    pieces = []
    for _ in range(n):
        p = x.astype(BF16)
        pieces.append(p)
        x = x - p.astype(F32)
    return pieces


def _lane_sum(x, ones):
    return _unstack(jnp.dot(_stack(x).astype(BF16), ones[:128], preferred_element_type=F32))


def _colsum(x):
    return jnp.sum(x, axis=0, keepdims=True)


def _rows_times(vecs, mat):
    n = vecs.shape[0]
    lane = lax.broadcasted_iota(jnp.int32, (n, 128), 1)
    lhs = jnp.concatenate([vecs[:, RWKV_HEAD * h:RWKV_HEAD * (h + 1)] for h in range(RWKV_HEADS)], axis=0)
    out = jnp.dot(lhs, mat, preferred_element_type=F32)
    tiles = [jnp.where(lane < RWKV_HEAD, out[2 * p * n:(2 * p + 1) * n, 128 * p:128 * (p + 1)],
                       out[(2 * p + 1) * n:(2 * p + 2) * n, 128 * p:128 * (p + 1)]) for p in range(4)]
    return jnp.concatenate(tiles, axis=1)


def _expand_cols(xt, t):
    lane = lax.broadcasted_iota(jnp.int32, (RWKV_HEAD, 128), 1)
    tiles = []
    for p in range(4):
        lo = jnp.broadcast_to(xt[128 * p:128 * p + RWKV_HEAD, t:t + 1], (RWKV_HEAD, 128))
        hi = jnp.broadcast_to(xt[128 * p + RWKV_HEAD:128 * (p + 1), t:t + 1], (RWKV_HEAD, 128))
        tiles.append(jnp.where(lane < RWKV_HEAD, lo, hi))
    return jnp.concatenate(tiles, axis=1)


def _expand_steps(y, t):
    lane = lax.broadcasted_iota(jnp.int32, (RWKV_HEAD, 128), 1)
    idx = jnp.where(lane < RWKV_HEAD, t, RWKV_HEAD + t)
    return jnp.concatenate([jnp.take_along_axis(y[:, 128 * p:128 * (p + 1)], idx, axis=1) for p in range(4)], axis=1)


def _keep_step(acc_ref, x, t):
    lane = lax.broadcasted_iota(jnp.int32, (1, RWKV_WIDTH), 1)
    mask = jnp.broadcast_to((lane & (RWKV_HEAD - 1)) == t, x.shape)
    pltpu.store(acc_ref, x, mask=mask)


def _steps_to_rows(acc):
    assert WKV_CHUNK == RWKV_HEAD
    tiles = []
    for p in range(4):
        tt = acc[:, 128 * p:128 * (p + 1)].T
        tiles.append(jnp.concatenate([tt[:RWKV_HEAD], tt[RWKV_HEAD:]], axis=1))
    return jnp.concatenate(tiles, axis=1)


def _head_sums(x, ones):
    return _unstack(jnp.dot(jnp.concatenate(_split(_stack(x), 3), axis=1), ones, preferred_element_type=F32))


def _wkv_fwd(r, w, k, v, kap, a):
    T = r.shape[0]
    C = WKV_CHUNK
    nc = T // C

    def body(r_ref, w_ref, k_ref, v_ref, kap_ref, a_ref, o_ref, s_all_ref, sa_rows_ref, s_scr, o_acc, sa_acc):
        @pl.when(pl.program_id(0) == 0)
        def _():
            s_scr[...] = jnp.zeros_like(s_scr)

        ones = _block_ones()
        rr, ww, kk, vv, kap_, aa = (ref[...] for ref in (r_ref, w_ref, k_ref, v_ref, kap_ref, a_ref))
        bb = kap_ * aa
        c1 = _head_sums(pltpu.roll(bb, 1, axis=0) * kap_, ones)
        row = lambda x, t: x[t:t + 1]

        v_hi = _trunc(vv)
        v_hi, v_lo = v_hi.astype(BF16), (vv - v_hi).astype(BF16)
        lane = lax.broadcasted_iota(jnp.int32, (RWKV_HEAD, RWKV_WIDTH), 1)
        diag = (lane & (RWKV_HEAD - 1)) == lax.broadcasted_iota(jnp.int32, (RWKV_HEAD, RWKV_WIDTH), 0)

        def expand_v(t):
            pieces = [_stack(jnp.where(diag, p[t:t + 1], 0).astype(BF16)) for p in (v_hi, v_lo)]
            return _unstack(jnp.dot(jnp.concatenate(pieces, axis=1), ones[:256], preferred_element_type=F32))

        s_prev = s_scr[...]
        sa = _lane_sum(s_prev * (-row(kap_, 0)), ones)
        ls = None

        def emit_o(t, s_t):
            _keep_step(o_acc, _lane_sum(s_t * row(rr, t), ones), t)

        for t in range(C):
            u = s_prev * row(ww, t) + expand_v(t) * row(kk, t)
            if t > 0:
                sa = ls - sa * row(c1, t)
            if t + 1 < C:
                ls = _lane_sum(u * (-row(kap_, t + 1)), ones)
            if t > 0:
                emit_o(t - 1, s_prev)
            s_prev = u + sa * row(bb, t)
            s_all_ref[t] = s_prev
            _keep_step(sa_acc, sa, t)
        emit_o(C - 1, s_prev)
        s_scr[...] = s_prev
        o_ref[...] = _steps_to_rows(o_acc[...])
        sa_rows_ref[...] = _steps_to_rows(sa_acc[...])

    spec = _rows(C, RWKV_WIDTH)
    return pl.pallas_call(
        body, name="wkv_fwd", grid=(nc,),
        in_specs=[spec] * 6,
        out_specs=[spec, pl.BlockSpec((C, RWKV_HEAD, RWKV_WIDTH), lambda i: (i, 0, 0)), spec],
        out_shape=[jax.ShapeDtypeStruct((T, RWKV_WIDTH), F32), jax.ShapeDtypeStruct((T, RWKV_HEAD, RWKV_WIDTH), F32),
                   jax.ShapeDtypeStruct((T, RWKV_WIDTH), F32)],
        scratch_shapes=[pltpu.VMEM((RWKV_HEAD, RWKV_WIDTH), F32)] * 3,
        compiler_params=_params(dimension_semantics=("arbitrary",)),
    )(r, w, k, v, kap, a)


def _wkv_bwd(r, w, k, v, kap, a, s_all, sa_rows, d_o):
    T = r.shape[0]
    C = WKV_CHUNK
    nc = T // C

    def body(r_ref, w_ref, k_ref, v_ref, kap_ref, a_ref, sa_rows_ref, do_ref, s_ref, s_before_ref,
             dr_ref, dw_ref, dk_ref, dv_ref, dkap_ref, da_ref, ds_scr, dv_acc, dsa_acc):
        first_chunk = pl.program_id(0) == nc - 1

        @pl.when(pl.program_id(0) == 0)
        def _():
            ds_scr[...] = jnp.zeros_like(ds_scr)

        ones = _block_ones()
        rr, ww, kk, vv, kap_, aa, sar, dd = (ref[...] for ref in (r_ref, w_ref, k_ref, v_ref, kap_ref, a_ref, sa_rows_ref, do_ref))
        bb = kap_ * aa
        e1 = _head_sums(pltpu.roll(kap_, C - 1, axis=0) * bb, ones)
        row = lambda x, t: x[t:t + 1]

        def state_before(t):
            return s_ref[t - 1] if t > 0 else jnp.where(first_chunk, 0.0, s_before_ref[0])

        do_cols = _steps_to_rows(dd)

        d_sn, dsa, rows = None, None, [None] * C

        def emit_rows(t, d_sn_t, dsa_t):
            _keep_step(dv_acc, _lane_sum(d_sn_t * row(kk, t), ones), t)
            _keep_step(dsa_acc, dsa_t, t)
            dk_db = _rows_times(jnp.concatenate([row(vv, t), row(sar, t)], axis=0), d_sn_t)
            db = dk_db[1:2]
            rows[t] = (_colsum(d_sn_t * state_before(t)), dk_db[0:1], db * row(aa, t), db * row(kap_, t))
            if t % 8 == 0:
                for j, ref in enumerate((dw_ref, dk_ref, dkap_ref, da_ref)):
                    ref[t:t + 8, :] = jnp.concatenate([rows[u][j] for u in range(t, t + 8)], axis=0)

        def state_rows():
            dsa_rows = _steps_to_rows(dsa_acc[...])
            d_r, d_kap = [], []
            for j in range(-1, C):
                lhs = ([row(dd, j)] if j >= 0 else []) + ([row(dsa_rows, j + 1)] if j + 1 < C else [])
                out = _rows_times(jnp.concatenate(lhs, axis=0), state_before(j + 1))
                if j >= 0:
                    d_r.append(out[0:1])
                if j + 1 < C:
                    d_kap.append(out[-1:])
            dr_ref[...] = jnp.concatenate(d_r, axis=0)
            dkap_ref[...] = dkap_ref[...] - jnp.concatenate(d_kap, axis=0)

        for t in reversed(range(C)):
            dof = _expand_steps(do_cols, t)
            if t == C - 1:
                d_sn = ds_scr[...] + dof * row(rr, t)
                dsa = _lane_sum(d_sn * row(bb, t), ones)
            else:
                v_t = d_sn * row(ww, t + 1) + dof * row(rr, t)
                ls = _lane_sum(v_t * row(bb, t), ones)
                emit_rows(t + 1, d_sn, dsa)
                d_sn = v_t - dsa * row(kap_, t + 1)
                dsa = ls - dsa * row(e1, t)
        emit_rows(0, d_sn, dsa)
        d_s = d_sn * row(ww, 0) - dsa * row(kap_, 0)
        ds_scr[...] = d_s
        dv_ref[...] = _steps_to_rows(dv_acc[...])
        state_rows()

    spec = pl.BlockSpec((C, RWKV_WIDTH), lambda i: (nc - 1 - i, 0))
    states = pl.BlockSpec((C, RWKV_HEAD, RWKV_WIDTH), lambda i: (nc - 1 - i, 0, 0))
    before = pl.BlockSpec((1, RWKV_HEAD, RWKV_WIDTH), lambda i: (jnp.maximum((nc - 1 - i) * C - 1, 0), 0, 0))
    return pl.pallas_call(
        body, name="wkv_bwd", grid=(nc,),
        in_specs=[spec] * 8 + [states, before],
        out_specs=[spec] * 6,
        out_shape=[jax.ShapeDtypeStruct((T, RWKV_WIDTH), F32)] * 6,
        scratch_shapes=[pltpu.VMEM((RWKV_HEAD, RWKV_WIDTH), F32)] * 3,
        compiler_params=_params(dimension_semantics=("arbitrary",)),
    )(r, w, k, v, kap, a, sa_rows, d_o, s_all, s_all)


W = RWKV_WIDTH


def _softplus(y):
    return jnp.maximum(y, 0.0) + jnp.log(1.0 + jnp.exp(-jnp.abs(y)))


def _prep_fn(kr, xwa, w0, a0, k_k, k_a, wup_pad, aup_pad, ones64):
    w_log = -_softplus(-(w0 + _mm_nt(jnp.tanh(xwa), wup_pad))) - 0.5
    decay = jnp.exp(-jnp.exp(w_log))
    a = jax.nn.sigmoid(a0 + _mm_nt(xwa, aup_pad))
    kk = kr * k_k
    kap = kk / jnp.maximum(jnp.sqrt(_head_mix(kk * kk, ones64)), 1e-12)
    k = kr * (1.0 + (a - 1.0) * k_a)
    return decay, k, kap, a


def _shift_down(p, first_row):
    rows = lax.broadcasted_iota(jnp.int32, p.shape, 0)
    return jnp.where(rows == 0, first_row, pltpu.roll(p, 1, axis=0))


def _shift_up(z, last_row):
    n = z.shape[0]
    rows = lax.broadcasted_iota(jnp.int32, z.shape, 0)
    return jnp.where(rows == n - 1, last_row, pltpu.roll(z, n - 1, axis=0))


def _prev_block_spec():
    return pl.BlockSpec((8, RWKV_COLS), lambda i: (jnp.maximum(i * (TOK_TILE // 8) - 1, 0), 0))


def _mixed(p_ref, prev8_ref, mu_ref, first_tile):
    p = p_ref[...]
    first_row = jnp.where(first_tile, 0.0, prev8_ref[7:8, :])
    prev = _shift_down(p, first_row)
    return p, prev, p + mu_ref[...] * (prev - p)


def _inproj_prep(x, norm_g, w_in_t, mu, w0, a0, k_k, k_a, wup_pad, aup_pad, ones64):
    T = x.shape[1]

    def body(x_ref, g_ref, w_in_ref, mu_ref, w0_ref, a0_ref, kk_ref, ka_ref, wup_ref, aup_ref, ones_ref,
             pr_ref, pw_ref, r_ref, w_ref, k_ref, v_ref, kap_ref, a_ref, grw_ref, last_scr):
        @pl.when(pl.program_id(0) == 0)
        def _():
            last_scr[...] = jnp.zeros_like(last_scr)

        ub = _rmsnorm(x_ref[...], g_ref[...]).astype(BF16)
        pr_ref[...] = _dot_nt_bf(ub, w_in_ref[:RET_COLS, :])
        p = _dot_nt_bf(ub, w_in_ref[RET_COLS:, :])
        pw_ref[...] = p
        ps = p + mu_ref[...] * (_shift_down(p, last_scr[7:8, :]) - p)
        last_scr[...] = p[TOK_TILE - 8:, :]
        decay, k, kap, a = _prep_fn(ps[:, W:2 * W], ps[:, 4 * W:], w0_ref[...], a0_ref[...], kk_ref[...], ka_ref[...],
                                    wup_ref[...], aup_ref[...], ones_ref[...])
        r_ref[...] = ps[:, 0:W]
        w_ref[...] = decay
        k_ref[...] = k
        v_ref[...] = ps[:, 2 * W:3 * W]
        kap_ref[...] = kap
        a_ref[...] = a
        grw_ref[...] = ps[:, 3 * W:4 * W]

    vec = _full((1, W))
    return pl.pallas_call(
        body, name="inproj_prep", grid=(T // TOK_TILE,),
        in_specs=[_rows_of_one(TOK_TILE, D_MODEL), _full((1, D_MODEL)), _full((IN_COLS, D_MODEL)), _full((1, RWKV_COLS)),
                  vec, vec, vec, vec, _full((W, 2 * LORA)), _full((W, 2 * LORA)), _full((256, 128))],
        out_specs=[_rows(TOK_TILE, RET_COLS), _rows(TOK_TILE, RWKV_COLS)] + [_rows(TOK_TILE, W)] * 7,
        out_shape=[jax.ShapeDtypeStruct((T, RET_COLS), F32), jax.ShapeDtypeStruct((T, RWKV_COLS), F32)]
        + [jax.ShapeDtypeStruct((T, W), F32)] * 7,
        scratch_shapes=[pltpu.VMEM((8, RWKV_COLS), F32)],
        compiler_params=_params(dimension_semantics=("arbitrary",)),
    )(x, norm_g, w_in_t, mu, w0, a0, k_k, k_a, wup_pad, aup_pad, ones64)


def _prep_bwd(p_rwkv, mu, w0, a0, k_k, k_a, wup_pad, aup_pad, ones64, dr, dw, dk, dv, dkap, da, dg, dr2, dk2, dv2):
    T = p_rwkv.shape[0]
    nt = T // TOK_TILE

    def body(p_ref, prev8_ref, mu_ref, w0_ref, a0_ref, kk_ref, ka_ref, wup_ref, aup_ref, ones_ref,
             dr_ref, dw_ref, dk_ref, dv_ref, dkap_ref, da_ref, dg_ref, dr2_ref, dk2_ref, dv2_ref,
             dp_ref, dmu_ref, dw0_ref, da0_ref, dkk_ref, dka_ref, dwup_ref, daup_ref, zrow_scr):
        i = pl.program_id(0)
        accs = (dmu_ref, dw0_ref, da0_ref, dkk_ref, dka_ref, dwup_ref, daup_ref)

        @pl.when(i == 0)
        def _():
            zrow_scr[...] = jnp.zeros_like(zrow_scr)
            for ref in accs:
                ref[...] = jnp.zeros_like(ref)

        p, prev, ps = _mixed(p_ref, prev8_ref, mu_ref, i == nt - 1)
        ones = ones_ref[...]
        _, vjp = jax.vjp(lambda *args: _prep_fn(*args, ones), ps[:, W:2 * W], ps[:, 4 * W:], w0_ref[...], a0_ref[...],
                         kk_ref[...], ka_ref[...], wup_ref[...], aup_ref[...])
        dkr, dxwa, dw0, da0, dkk, dka, dwup, daup = vjp(
            (dw_ref[...], dk_ref[...] + dk2_ref[...], dkap_ref[...], da_ref[...]))
        dps = jnp.concatenate([dr_ref[...] + dr2_ref[...], dkr, dv_ref[...] + dv2_ref[...], dg_ref[...], dxwa], axis=1)
        z = dps * mu_ref[...]
        dp_ref[...] = dps - z + _shift_up(z, zrow_scr[0:1, :])
        zrow_scr[0:1, :] = z[0:1, :]
        for ref, val in zip(accs, (_colsum(dps * (prev - p)), dw0, da0, dkk, dka, dwup, daup)):
            ref[...] += val

    rev = lambda i: (nt - 1 - i, 0)
    vec = _full((1, W))
    lora = _full((W, 2 * LORA))
    tile = pl.BlockSpec((TOK_TILE, W), rev)
    prev8 = pl.BlockSpec((8, RWKV_COLS), lambda i: (jnp.maximum((nt - 1 - i) * (TOK_TILE // 8) - 1, 0), 0))
    return pl.pallas_call(
        body, name="prep_bwd", grid=(nt,),
        in_specs=[pl.BlockSpec((TOK_TILE, RWKV_COLS), rev), prev8, _full((1, RWKV_COLS)), vec, vec, vec, vec, lora, lora,
                  _full((256, 128))] + [tile] * 10,
        out_specs=[pl.BlockSpec((TOK_TILE, RWKV_COLS), rev), _full((1, RWKV_COLS)), vec, vec, vec, vec, lora, lora],
        out_shape=[jax.ShapeDtypeStruct((T, RWKV_COLS), F32), jax.ShapeDtypeStruct((1, RWKV_COLS), F32)]
        + [jax.ShapeDtypeStruct((1, W), F32)] * 4 + [jax.ShapeDtypeStruct((W, 2 * LORA), F32)] * 2,
        scratch_shapes=[pltpu.VMEM((8, RWKV_COLS), F32)],
        compiler_params=_params(dimension_semantics=("arbitrary",)),
    )(p_rwkv, p_rwkv, mu, w0, a0, k_k, k_a, wup_pad, aup_pad, ones64, dr, dw, dk, dv, dkap, da, dg, dr2, dk2, dv2)


def _silu(x):
    return x * jax.nn.sigmoid(x)


def _post_y(o, r, k, v, g_rw, ret_raw, g_ret, ret_gn_g, gn_g, gn_b, r_k, avg128, avg64, ones64):
    xc = ret_raw - _head_mix(ret_raw, avg128)
    ret = xc * lax.rsqrt(_head_mix(xc * xc, avg128) + RET_GN_EPS)
    y_ret = _silu(g_ret) * (ret * ret_gn_g)
    oc = o - _head_mix(o, avg64)
    on = oc * lax.rsqrt(_head_mix(oc * oc, avg64) + RWKV_GN_EPS) * gn_g + gn_b
    bonus = _head_mix(r * k * r_k, ones64) * v
    y_rwkv = _silu(g_rw) * (on + bonus)
    return y_ret, y_rwkv


def _post_loss(h, final_g, target):
    err = _rmsnorm(h, final_g) - target
    return 0.5 * jnp.sum(jnp.mean(err * err, axis=-1))


def _post(o, r, k, v, g_rw, ret_raw, p_ret, x, target, ret_gn_g, gn_g, gn_b, r_k, final_g, w_out, avg128, avg64, ones64):
    T = x.shape[1]
    n_tok_out = 8

    def body(o_ref, r_ref, k_ref, v_ref, grw_ref, ret_ref, gret_ref, x_ref, tgt_ref, rg_ref, gg_ref, gb_ref, rk_ref, fg_ref,
             wo_ref, a128_ref, a64_ref, ones_ref, *outs):
        tok_outs, (dwo_ref, drg_ref, dgg_ref, dgb_ref, drk_ref, dfg_ref, loss_ref) = outs[:n_tok_out], outs[n_tok_out:]
        accs = (dwo_ref, drg_ref, dgg_ref, dgb_ref, drk_ref, dfg_ref, loss_ref)

        @pl.when(pl.program_id(0) == 0)
        def _():
            for ref in accs:
                ref[...] = jnp.zeros_like(ref)

        consts = (a128_ref[...], a64_ref[...], ones_ref[...])
        (y_ret, y_rwkv), vjp = jax.vjp(
            lambda *args: _post_y(*args, *consts), o_ref[...], r_ref[...], k_ref[...], v_ref[...], grw_ref[...], ret_ref[...],
            gret_ref[...], rg_ref[...], gg_ref[...], gb_ref[...], rk_ref[...])
        h = x_ref[...] + _dot_bf(y_ret, wo_ref[0:RET_WIDTH, :]) + _dot_bf(y_rwkv, wo_ref[RET_WIDTH:, :])
        loss, (dh, dfg) = jax.value_and_grad(_post_loss, argnums=(0, 1))(h, fg_ref[...], tgt_ref[...])
        dy_ret = _dot_nt_bf(dh, wo_ref[0:RET_WIDTH, :])
        dy_rwkv = _dot_nt_bf(dh, wo_ref[RET_WIDTH:, :])
        do, dr, dk, dv, dgrw, dret, dgret, drg, dgg, dgb, drk = vjp((dy_ret, dy_rwkv))
        for ref, val in zip(tok_outs, (dh, do, dr, dk, dv, dgrw, dret, dgret)):
            ref[...] = val
        dwo_ref[0:RET_WIDTH, :] += _dot_tn_bf(y_ret, dh)
        dwo_ref[RET_WIDTH:, :] += _dot_tn_bf(y_rwkv, dh)
        for ref, val in zip(accs[1:], (drg, dgg, dgb, drk, dfg, jnp.full((1, 128), loss, F32))):
            ref[...] += val

    tile = _rows(TOK_TILE, W)
    wide = _rows(TOK_TILE, D_MODEL)
    wide_of_one = _rows_of_one(TOK_TILE, D_MODEL)
    vec = _full((1, W))
    sq = _full((256, 128))
    return pl.pallas_call(
        body, name="post", grid=(T // TOK_TILE,),
        in_specs=[tile] * 6 + [pl.BlockSpec((TOK_TILE, W), lambda i: (i, 2)), wide_of_one, wide_of_one, vec, vec, vec, vec,
                               _full((1, D_MODEL)), _full((D_MODEL, D_MODEL)), sq, sq, sq],
        out_specs=[wide] + [tile] * 7 + [_full((D_MODEL, D_MODEL)), vec, vec, vec, vec, _full((1, D_MODEL)), _full((1, 128))],
        out_shape=[jax.ShapeDtypeStruct((T, D_MODEL), F32)] + [jax.ShapeDtypeStruct((T, W), F32)] * 7
        + [jax.ShapeDtypeStruct((D_MODEL, D_MODEL), F32)] + [jax.ShapeDtypeStruct((1, W), F32)] * 4
        + [jax.ShapeDtypeStruct((1, D_MODEL), F32), jax.ShapeDtypeStruct((1, 128), F32)],
        compiler_params=_params(dimension_semantics=("arbitrary",)),
    )(o, r, k, v, g_rw, ret_raw, p_ret, x, target, ret_gn_g, gn_g, gn_b, r_k, final_g, w_out, avg128, avg64, ones64)


def _inproj_bwd(x, norm_g, dp_qkv, dg_ret, dp_rwkv, dh, w_in_t):
    T = x.shape[1]
    widths = [dp.shape[1] for dp in (dp_qkv, dg_ret, dp_rwkv)]
    steps = T // TOK_TILE

    def body(x_ref, g_ref, dqkv_ref, dgret_ref, drwkv_ref, dh_ref, w_ref, dx_ref, dg_ref, mine_ref, sib_ref, acc_ref):
        @pl.when(pl.program_id(0) == 0)
        def _():
            dg_ref[...] = jnp.zeros_like(dg_ref)
            acc_ref[...] = jnp.zeros_like(acc_ref)

        u, vjp = jax.vjp(_rmsnorm, x_ref[...], g_ref[...])
        ub = u.astype(BF16)
        du, row = 0.0, 0
        for dp_ref, n in zip((dqkv_ref, dgret_ref, drwkv_ref), widths):
            dp = dp_ref[...].astype(BF16)
            du = du + _dot_bf(dp, w_ref[row:row + n, :])
            acc_ref[row:row + n, :] += _dot_tn_bf(dp, ub)
            row += n
        dx, dg = vjp(du)
        dx_ref[...] = dx + dh_ref[...]
        dg_ref[...] += dg

        @pl.when(pl.program_id(0) == steps - 1)
        def _():
            core = lax.axis_index("c")
            for dev in range(N_DEV):
                block = acc_ref[dev * SHARD_IN:(dev + 1) * SHARD_IN, :].astype(BF16)

                @pl.when(core == dev % 2)
                def _():
                    mine_ref[dev // 2] = block

                @pl.when(core != dev % 2)
                def _():
                    sib_ref[dev // 2] = block

    half = jax.ShapeDtypeStruct((N_DEV // 2, SHARD_IN, D_MODEL), BF16)
    once = lambda shape: pl.BlockSpec(shape, lambda i: (0,) * len(shape), pipeline_mode=pl.Buffered(1))
    return pl.pallas_call(
        body, name="inproj_bwd", grid=(steps,),
        in_specs=[_rows_of_one(TOK_TILE, D_MODEL), _full((1, D_MODEL))] + [_rows(TOK_TILE, n) for n in widths]
        + [_rows(TOK_TILE, D_MODEL), once((IN_COLS, D_MODEL))],
        out_specs=[_rows_of_one(TOK_TILE, D_MODEL), _full((1, D_MODEL)), once(half.shape), once(half.shape)],
        out_shape=[jax.ShapeDtypeStruct((1, T, D_MODEL), F32), jax.ShapeDtypeStruct((1, D_MODEL), F32), half, half],
        scratch_shapes=[pltpu.VMEM((IN_COLS, D_MODEL), F32)],
        compiler_params=_params(dimension_semantics=("arbitrary",)),
    )(x, norm_g, dp_qkv, dg_ret, dp_rwkv, dh, w_in_t)


def _pad_lora(w_up_t, first):
    z = jnp.zeros_like(w_up_t)
    return jnp.concatenate([w_up_t, z] if first else [z, w_up_t], axis=1)


def _local_grads(x, target, norm_g, w_in_t, ret_gn_g, mu, w_lora_up_t, w0, a_lora_up_t, a0, k_k, k_a, r_k, gn_g, gn_b,
                 w_out_bf, final_g):
    T = x.shape[1]
    tabs = _rope_tables(T) + _ret_tables()
    ones64 = _block_mix(128, RWKV_HEAD)
    avg64 = _block_mix(128, RWKV_HEAD, 1.0 / RWKV_HEAD)
    avg128 = _block_mix(128, RET_DV, 1.0 / RET_DV)
    wup_pad, aup_pad = _pad_lora(w_lora_up_t, True), _pad_lora(a_lora_up_t, False)

    p_ret, p_rwkv, r, w, k, v, kap, a, g_rw = _inproj_prep(x, norm_g, w_in_t, mu, w0, a0, k_k, k_a, wup_pad, aup_pad, ones64)
    ret_raw, s_saved = _ret_fwd(p_ret, tabs)
    o, s_all, sa_rows = _wkv_fwd(r, w, k, v, kap, a)
    (dh, do, dr2, dk2, dv2, dgrw, dret, dgret, d_w_out, d_ret_gn_g, d_gn_g, d_gn_b, d_r_k, d_final_g, loss) = _post(
        o, r, k, v, g_rw, ret_raw, p_ret, x, target, ret_gn_g, gn_g, gn_b, r_k, final_g, w_out_bf, avg128, avg64, ones64)
    dr, dw, dk, dv, dkap, da = _wkv_bwd(r, w, k, v, kap, a, s_all, sa_rows, do)
    dp_rwkv, d_mu, d_w0, d_a0, d_k_k, d_k_a, d_wup, d_aup = _prep_bwd(
        p_rwkv, mu, w0, a0, k_k, k_a, wup_pad, aup_pad, ones64, dr, dw, dk, dv, dkap, da, dgrw, dr2, dk2, dv2)
    dp_qkv = _ret_bwd(p_ret, s_saved, dret, tabs)
    dx, d_norm_g, *d_w_in = _inproj_bwd(x, norm_g, dp_qkv, dgret, dp_rwkv, dh, w_in_t)
    grads = dict(norm_g=d_norm_g, w_in=d_w_in, ret_gn_g=d_ret_gn_g, rwkv_mu=d_mu, w_lora_up=d_wup[:, :LORA], w0=d_w0,
                 a_lora_up=d_aup[:, LORA:], a0=d_a0, k_k=d_k_k, k_a=d_k_a, r_k=d_r_k, rwkv_gn_g=d_gn_g, rwkv_gn_b=d_gn_b,
                 w_out=d_w_out, final_norm_g=d_final_g)
    return loss, dx, grads


def _mesh_pos():
    return lax.axis_index("x"), lax.axis_index("y"), lax.axis_index("c")


def _all_gather(shards):
    n = len(shards)

    def body(*refs):
        x_refs, out_refs = refs[:n], refs[n:2 * n]
        send_sems, recv_sems, local_sems = refs[2 * n:]
        x, y, c = _mesh_pos()
        me, sibling = (x, y, c), (x, y, 1 - c)
        chips = [(1 - x, y), (x, 1 - y), (1 - x, 1 - y)]

        def rows(a, pos):
            m = x_refs[a].shape[0]
            return out_refs[a].at[pl.ds((4 * pos[0] + 2 * pos[1] + pos[2]) * m, m), :]

        def copy(a, k, block, to, src=None):
            return pltpu.make_async_remote_copy(
                src_ref=rows(a, block) if src is None else src, dst_ref=rows(a, block),
                send_sem=send_sems.at[a, k], recv_sem=recv_sems.at[a, k], device_id=to, device_id_type=MESH)

        mine = [pltpu.make_async_copy(x_refs[a], rows(a, me), local_sems.at[a]) for a in range(n)]
        for cp in mine:
            cp.start()
        first = []
        for a in range(n):
            first.append(copy(a, 0, me, sibling, src=x_refs[a]))
            first += [copy(a, 1 + j, me, (*chip, c), src=x_refs[a]) for j, chip in enumerate(chips)]
        for cp in first:
            cp.start()
        passed = []
        for j, chip in enumerate(chips):
            for a in range(n):
                copy(a, 1 + j, (*chip, c), me).wait_recv()
                passed.append(copy(a, 4 + j, (*chip, c), sibling))
                passed[-1].start()
        for a in range(n):
            copy(a, 0, sibling, me).wait_recv()
            for j, chip in enumerate(chips):
                copy(a, 4 + j, (*chip, 1 - c), me).wait_recv()
        for cp in first + passed:
            cp.wait_send()
        for cp in mine:
            cp.wait()

    vmem = pl.BlockSpec(memory_space=pltpu.VMEM)
    return pl.pallas_call(
        body, name="gather_weights",
        out_shape=[jax.ShapeDtypeStruct((N_DEV * s.shape[0], s.shape[1]), s.dtype) for s in shards],
        in_specs=[vmem] * n, out_specs=[vmem] * n,
        scratch_shapes=[pltpu.SemaphoreType.DMA((n, 7)), pltpu.SemaphoreType.DMA((n, 7)), pltpu.SemaphoreType.DMA((n,))],
        compiler_params=_params(),
    )(*shards)


N_CHIP = 4


def _exchange_pairs(big, small):
    nb, ns = len(big), len(small)

    def body(*refs):
        big_in, small_in = refs[:nb], refs[nb:nb + ns]
        theirs, small_out = refs[nb + ns:2 * nb + ns], refs[2 * nb + ns:2 * nb + 2 * ns]
        pair_send, pair_recv, send_sems, recv_sems, local_sems = refs[2 * nb + 2 * ns:]
        x, y, c = _mesh_pos()
        me = 4 * x + 2 * y + c
        local = [pltpu.make_async_copy(small_in[a].at[me], small_out[a].at[me], local_sems.at[a]) for a in range(ns)]
        for cp in local:
            cp.start()
        copies = [pltpu.make_async_remote_copy(
            src_ref=big_in[a], dst_ref=theirs[a], send_sem=pair_send.at[a], recv_sem=pair_recv.at[a],
            device_id=(x, y, 1 - c), device_id_type=MESH) for a in range(nb)]
        for k in range(1, N_DEV):
            peer = (x ^ (k >> 2), y ^ ((k >> 1) & 1), c ^ (k & 1))
            peer_idx = 4 * peer[0] + 2 * peer[1] + peer[2]
            copies += [pltpu.make_async_remote_copy(
                src_ref=small_in[a].at[peer_idx], dst_ref=small_out[a].at[me], send_sem=send_sems.at[a, k - 1],
                recv_sem=recv_sems.at[a, k - 1], device_id=peer, device_id_type=MESH) for a in range(ns)]
        for cp in copies:
            cp.start()
        for cp in copies:
            cp.wait()
        for cp in local:
            cp.wait()

    hbm = pl.BlockSpec(memory_space=pl.ANY)
    out_shape = [jax.ShapeDtypeStruct(p.shape, p.dtype) for p in big + small]
    dma = pltpu.SemaphoreType.DMA
    res = pl.pallas_call(
        body, name="exchange_pairs", out_shape=out_shape,
        in_specs=[hbm] * (nb + ns), out_specs=[hbm] * len(out_shape),
        scratch_shapes=[dma((nb,)), dma((nb,)), dma((ns, 7)), dma((ns, 7)), dma((ns,))],
        compiler_params=_params(),
    )(*big, *small)
    return res[:nb], res[nb:]


def _pair_sum(name, mine, theirs, row_tile):
    _, rows, cols = mine.shape

    def body(a_ref, b_ref, o_ref):
        o_ref[...] = (a_ref[...].astype(F32) + b_ref[...].astype(F32)).astype(o_ref.dtype)

    spec = pl.BlockSpec((N_CHIP, row_tile, cols), lambda i: (0, i, 0))
    return pl.pallas_call(
        body, name=name, grid=(rows // row_tile,), in_specs=[spec, spec], out_specs=spec,
        out_shape=jax.ShapeDtypeStruct(mine.shape, mine.dtype),
        compiler_params=_params(dimension_semantics=("arbitrary",)),
    )(mine, theirs)


def _exchange_chips(parts):
    n = len(parts)

    def body(*refs):
        in_refs, out_refs = refs[:n], refs[n:2 * n]
        send_sems, recv_sems, local_sems = refs[2 * n:]
        x, y, c = _mesh_pos()
        my_chip = 2 * x + y
        own = [pltpu.make_async_copy(in_refs[a].at[my_chip], out_refs[a].at[my_chip], local_sems.at[a]) for a in range(n)]
        for cp in own:
            cp.start()
        copies = []
        for k in range(1, N_CHIP):
            px, py = x ^ (k >> 1), y ^ (k & 1)
            copies += [pltpu.make_async_remote_copy(
                src_ref=in_refs[a].at[2 * px + py], dst_ref=out_refs[a].at[my_chip], send_sem=send_sems.at[a, k - 1],
                recv_sem=recv_sems.at[a, k - 1], device_id=(px, py, c), device_id_type=MESH) for a in range(n)]
        for cp in copies:
            cp.start()
        for cp in copies:
            cp.wait()
        for cp in own:
            cp.wait()

    hbm = pl.BlockSpec(memory_space=pl.ANY)
    dma = pltpu.SemaphoreType.DMA
    return pl.pallas_call(
        body, name="exchange_chips",
        out_shape=[jax.ShapeDtypeStruct(p.shape, p.dtype) for p in parts],
        in_specs=[hbm] * n, out_specs=[hbm] * n,
        scratch_shapes=[dma((n, N_CHIP - 1)), dma((n, N_CHIP - 1)), dma((n,))],
        compiler_params=_params(),
    )(*parts)


def _adamw(w, g, m, v):
    m = ADAM_B1 * m + (1.0 - ADAM_B1) * g
    v = ADAM_B2 * v + (1.0 - ADAM_B2) * (g * g)
    m_hat = m / (1.0 - ADAM_B1 ** ADAM_STEP)
    v_hat = v / (1.0 - ADAM_B2 ** ADAM_STEP)
    return -ADAM_LR * (m_hat / (jnp.sqrt(v_hat) + ADAM_EPS) + ADAM_WD * w), m, v


def _sum_parts(name, parts, row_tile):
    n_parts, rows, cols = parts.shape

    def body(p_ref, g_ref):
        g = p_ref[0].astype(F32)
        for s in range(1, n_parts):
            g = g + p_ref[s].astype(F32)
        g_ref[...] = g

    return pl.pallas_call(
        body, name=name, grid=(rows // row_tile,),
        in_specs=[pl.BlockSpec((n_parts, row_tile, cols), lambda i: (0, i, 0))],
        out_specs=pl.BlockSpec((row_tile, cols), lambda i: (i, 0)),
        out_shape=jax.ShapeDtypeStruct((rows, cols), F32),
        compiler_params=_params(dimension_semantics=("arbitrary",)),
    )(parts)


def _adamw_apply(name, g, w, m, v, row_tile):
    _, rows, cols = w.shape

    def body(g_ref, w_ref, m_ref, v_ref, d_ref, nm_ref, nv_ref):
        d_ref[0], nm_ref[0], nv_ref[0] = _adamw(w_ref[0], g_ref[0], m_ref[0], v_ref[0])

    tile = pl.BlockSpec((1, row_tile, cols), lambda i: (0, i, 0))
    return pl.pallas_call(
        body, name=name, grid=(rows // row_tile,), in_specs=[tile] * 4, out_specs=[tile] * 3,
        out_shape=[jax.ShapeDtypeStruct((1, rows, cols), F32)] * 3,
        compiler_params=_params(dimension_semantics=("arbitrary",)),
    )(g, w, m, v)


def _reduce_adamw_2d(name, parts, w, m, v, row_tile):
    n_parts, rows, cols = parts.shape

    def body(p_ref, w_ref, m_ref, v_ref, g_ref, d_ref, nm_ref, nv_ref):
        g = p_ref[0].astype(F32)
        for s in range(1, n_parts):
            g = g + p_ref[s].astype(F32)
        g_ref[...] = g
        d_ref[...], nm_ref[...], nv_ref[...] = _adamw(w_ref[...], g, m_ref[...], v_ref[...])

    tile = pl.BlockSpec((row_tile, cols), lambda i: (i, 0))
    return pl.pallas_call(
        body, name=name, grid=(rows // row_tile,),
        in_specs=[pl.BlockSpec((n_parts, row_tile, cols), lambda i: (0, i, 0)), tile, tile, tile],
        out_specs=[tile] * 4,
        out_shape=[jax.ShapeDtypeStruct((rows, cols), F32)] * 4,
        compiler_params=_params(dimension_semantics=("arbitrary",)),
    )(parts, w, m, v)


def _reduce_adamw_t(name, parts_t, w, m, v, sum_tile, row_tile):
    g = _sum_parts(name + "_sum", parts_t, sum_tile).T[None]
    return [g] + list(_adamw_apply(name, g, w, m, v, row_tile))


def _reduce_adamw(name, parts, w, m, v, row_tile):
    n_parts, rows, cols = parts.shape

    def body(p_ref, w_ref, m_ref, v_ref, g_ref, d_ref, nm_ref, nv_ref):
        g = p_ref[0].astype(F32)
        for s in range(1, n_parts):
            g = g + p_ref[s].astype(F32)
        g_ref[0] = g
        d_ref[0], nm_ref[0], nv_ref[0] = _adamw(w_ref[0], g, m_ref[0], v_ref[0])

    tile = pl.BlockSpec((1, row_tile, cols), lambda i: (0, i, 0))
    return pl.pallas_call(
        body, name=name, grid=(rows // row_tile,),
        in_specs=[pl.BlockSpec((n_parts, row_tile, cols), lambda i: (0, i, 0)), tile, tile, tile],
        out_specs=[tile] * 4,
        out_shape=[jax.ShapeDtypeStruct((1, rows, cols), F32)] * 4,
        compiler_params=_params(dimension_semantics=("arbitrary",)),
    )(parts, w, m, v)


_SMALL = (("norm_g", 1024), ("ret_gn_g", 512), ("rwkv_mu", 2176), ("w0", 512), ("a0", 512), ("k_k", 512), ("k_a", 512),
          ("r_k", 512), ("rwkv_gn_g", 512), ("rwkv_gn_b", 512), ("final_norm_g", 1024))
_SMALL_LANES = sum(n for _, n in _SMALL) + 128
_WEIGHTS = ("norm_g", "w_in", "ret_gn_g", "rwkv_mu", "w_lora_up", "w0", "a_lora_up", "a0", "k_k", "k_a", "r_k", "rwkv_gn_g",
            "rwkv_gn_b", "w_out", "final_norm_g")


def _adamw_vectors(parts, wts, mom, var):
    k = len(_SMALL)

    def body(p_ref, *refs):
        w_refs, m_refs, v_refs, outs = refs[:k], refs[k:2 * k], refs[2 * k:3 * k], refs[3 * k:]
        g_all = p_ref[0]
        for s in range(1, N_DEV):
            g_all = g_all + p_ref[s]
        off = 0
        for i, (name, n) in enumerate(_SMALL):
            g = g_all[:, off:off + n]
            off += n
            if name == "r_k":
                g = jnp.concatenate([g[:, RWKV_HEAD * h:RWKV_HEAD * (h + 1)] for h in range(RWKV_HEADS)], axis=0)[None]
            outs[4 * i][...] = g
            outs[4 * i + 1][...], outs[4 * i + 2][...], outs[4 * i + 3][...] = _adamw(
                w_refs[i][...], g, m_refs[i][...], v_refs[i][...])
        outs[4 * k][...] = g_all[:, off:off + 128]

    vmem = pl.BlockSpec(memory_space=pltpu.VMEM)
    shapes = [jax.ShapeDtypeStruct(wts[n].shape, F32) for n, _ in _SMALL for _ in range(4)] + [jax.ShapeDtypeStruct((1, 128), F32)]
    res = pl.pallas_call(
        body, name="adamw_vectors", out_shape=shapes,
        in_specs=[vmem] * (1 + 3 * k), out_specs=[vmem] * len(shapes), compiler_params=_params(),
    )(parts, *[wts[n] for n, _ in _SMALL], *[mom[n] for n, _ in _SMALL], *[var[n] for n, _ in _SMALL])
    return {n: res[4 * i:4 * i + 4] for i, (n, _) in enumerate(_SMALL)}, res[4 * k]


def kernel(x, norm_g, w_in, ret_gn_g, rwkv_mu, w_lora_up, w0, a_lora_up, a0, k_k, k_a, r_k, rwkv_gn_g, rwkv_gn_b, w_out, final_norm_g, loss_target, m_norm_g, m_w_in, m_ret_gn_g, m_rwkv_mu, m_w_lora_up, m_w0, m_a_lora_up, m_a0, m_k_k, m_k_a, m_r_k, m_rwkv_gn_g, m_rwkv_gn_b, m_w_out, m_final_norm_g, v_norm_g, v_w_in, v_ret_gn_g, v_rwkv_mu, v_w_lora_up, v_w0, v_a_lora_up, v_a0, v_k_k, v_k_a, v_r_k, v_rwkv_gn_g, v_rwkv_gn_b, v_w_out, v_final_norm_g):
    wts = dict(norm_g=norm_g, w_in=w_in, ret_gn_g=ret_gn_g, rwkv_mu=rwkv_mu, w_lora_up=w_lora_up, w0=w0, a_lora_up=a_lora_up,
               a0=a0, k_k=k_k, k_a=k_a, r_k=r_k, rwkv_gn_g=rwkv_gn_g, rwkv_gn_b=rwkv_gn_b, w_out=w_out,
               final_norm_g=final_norm_g)
    mom = dict(norm_g=m_norm_g, w_in=m_w_in, ret_gn_g=m_ret_gn_g, rwkv_mu=m_rwkv_mu, w_lora_up=m_w_lora_up, w0=m_w0,
               a_lora_up=m_a_lora_up, a0=m_a0, k_k=m_k_k, k_a=m_k_a, r_k=m_r_k, rwkv_gn_g=m_rwkv_gn_g,
               rwkv_gn_b=m_rwkv_gn_b, w_out=m_w_out, final_norm_g=m_final_norm_g)
    var = dict(norm_g=v_norm_g, w_in=v_w_in, ret_gn_g=v_ret_gn_g, rwkv_mu=v_rwkv_mu, w_lora_up=v_w_lora_up, w0=v_w0,
               a_lora_up=v_a_lora_up, a0=v_a0, k_k=v_k_k, k_a=v_k_a, r_k=v_r_k, rwkv_gn_g=v_rwkv_gn_g,
               rwkv_gn_b=v_rwkv_gn_b, w_out=v_w_out, final_norm_g=v_final_norm_g)
    shapes = {n: wts[n].shape for n in _WEIGHTS}

    w_in_t, w_out_bf, wup_t, aup_t = _all_gather(
        [w_in[0].T.astype(BF16), w_out[0].astype(BF16), w_lora_up[0].T, a_lora_up[0].T])

    loss, dx, g = _local_grads(
        x, loss_target, norm_g, w_in_t, ret_gn_g, rwkv_mu, wup_t, w0, aup_t, a0, k_k, k_a,
        r_k.reshape(1, W), rwkv_gn_g, rwkv_gn_b, w_out_bf, final_norm_g.reshape(1, D_MODEL))

    small = jnp.concatenate([g[n] for n, _ in _SMALL] + [loss], axis=1)
    core = lax.axis_index("c")
    by_core = lambda t: [lax.dynamic_index_in_dim(t, i, axis=1, keepdims=False) for i in (core, 1 - core)]
    in_mine, in_sib = g["w_in"]
    out_mine, out_sib = by_core(g["w_out"].reshape(N_CHIP, 2, SHARD_OUT, D_MODEL).astype(BF16))
    (in_theirs, out_theirs), parts = _exchange_pairs(
        [in_sib, out_sib],
        [g["w_lora_up"].reshape(N_DEV, SHARD_LORA, LORA), g["a_lora_up"].reshape(N_DEV, SHARD_LORA, LORA),
         jnp.broadcast_to(small[None], (N_DEV, 1, _SMALL_LANES))])
    by_chip = _exchange_chips([_pair_sum("pair_sum_w_in", in_mine, in_theirs, SHARD_IN // 2),
                               _pair_sum("pair_sum_w_out", out_mine, out_theirs, SHARD_OUT)])
    res = {}
    res["w_in"] = [t.T[None] for t in _reduce_adamw_2d(
        "adamw_w_in", by_chip[0], w_in[0].T, m_w_in[0].T, v_w_in[0].T, SHARD_IN // 2)]
    res["w_out"] = _reduce_adamw("adamw_w_out", by_chip[1], w_out, m_w_out, v_w_out, SHARD_OUT)
    res["w_lora_up"] = _reduce_adamw_t("adamw_w_lora_up", parts[0], w_lora_up, m_w_lora_up, v_w_lora_up, LORA, LORA)
    res["a_lora_up"] = _reduce_adamw_t("adamw_a_lora_up", parts[1], a_lora_up, m_a_lora_up, v_a_lora_up, LORA, LORA)
    as_row = lambda d: {n: d[n] if d[n].ndim > 1 else d[n].reshape(1, size) for n, size in _SMALL}
    vec, loss_row = _adamw_vectors(parts[2], as_row(wts), as_row(mom), as_row(var))
    res.update(vec)
    res = {n: [t.reshape(shapes[n]) for t in res[n]] for n in _WEIGHTS}
    return (loss_row[0, 0], dx, *[res[n][0] for n in _WEIGHTS], *[res[n][1] for n in _WEIGHTS],
            *[res[n][2] for n in _WEIGHTS], *[res[n][3] for n in _WEIGHTS])
```
